```python
import math
import jax, jax.numpy as jnp
from jax import lax
import numpy as np

D_MODEL = 1024
BATCH = 8
SEQ = 4096
DEPTH = 2

N_HEADS_A = 8
HEAD_DIM_A = 128
CONV_A = 4
CHUNK = 64
N_HEADS_B = 16
N_KV_B = 4
HEAD_DIM_B = 64
WINDOW = 128
D_FF = 2816
FFN_CONV = 3
PLE_DIM = 256
EPS = 1e-6
N_A_LAYERS = (DEPTH + 1) // 2
N_B_LAYERS = DEPTH // 2

kernel_name = "hybrid_deltanet_swa_convffn_ple"


def rmsnorm(x, w):
    x32 = x.astype(jnp.float32)
    y = x32 * lax.rsqrt(jnp.mean(x32 * x32, axis=-1, keepdims=True) + EPS)
    return (y * w.astype(jnp.float32)).astype(x.dtype)


def l2norm(x):
    x32 = x.astype(jnp.float32)
    return x32 * lax.rsqrt(jnp.sum(x32 * x32, axis=-1, keepdims=True) + EPS)


def causal_dwconv(x, w):
    k_w = w.shape[0]
    t = x.shape[1]
    xp = jnp.pad(x, ((0, 0), (k_w - 1, 0), (0, 0)))
    y = w[0] * xp[:, 0:t]
    for i in range(1, k_w):
        y = y + w[i] * xp[:, i:i + t]
    return y


def gated_delta_rule(q, k, v, g, beta):
    b_, t_, h_, dk = q.shape
    dv = v.shape[-1]
    n_ch = t_ // CHUNK
    f32 = jnp.float32

    def chunks(a):
        return a.astype(f32).reshape(b_, n_ch, CHUNK, h_, a.shape[-1]).transpose(1, 0, 3, 2, 4)

    qc = chunks(q) * (dk ** -0.5)
    kc = chunks(k)
    vc = chunks(v)
    gcs = jnp.cumsum(g.astype(f32).reshape(b_, n_ch, CHUNK, h_).transpose(1, 0, 3, 2), axis=-1)
    bc = beta.astype(f32).reshape(b_, n_ch, CHUNK, h_).transpose(1, 0, 3, 2)

    idx = jnp.arange(CHUNK)
    incl = idx[:, None] >= idx[None, :]
    strict = idx[:, None] > idx[None, :]
    diff = gcs[..., :, None] - gcs[..., None, :]
    decay = jnp.exp(jnp.where(incl, diff, -jnp.inf))

    kk = jnp.einsum('nbhik,nbhjk->nbhij', kc, kc)
    lmat = jnp.where(strict, bc[..., :, None] * kk * decay, 0.0)
    eye_l = jnp.eye(CHUNK, dtype=f32) + lmat
    rhs = jnp.concatenate([bc[..., None] * vc,
                           (bc * jnp.exp(gcs))[..., None] * kc], axis=-1)
    sol = lax.linalg.triangular_solve(eye_l, rhs, left_side=True, lower=True, unit_diagonal=True)
    u0 = sol[..., :dv]
    wk = sol[..., dv:]

    qk = jnp.einsum('nbhik,nbhjk->nbhij', qc, kc) * decay
    q_dec = qc * jnp.exp(gcs)[..., None]
    k_dec = kc * jnp.exp(gcs[..., -1:] - gcs)[..., None]
    g_last = jnp.exp(gcs[..., -1])

    def step(s, inp):
        qd, kd, qkc, u0c, wc, gl = inp
        u = u0c - jnp.einsum('bhck,bhkv->bhcv', wc, s)
        o = jnp.einsum('bhck,bhkv->bhcv', qd, s) + jnp.einsum('bhij,bhjv->bhiv', qkc, u)
        s = gl[..., None, None] * s + jnp.einsum('bhck,bhcv->bhkv', kd, u)
        return s, o

    s0 = jnp.zeros((b_, h_, dk, dv), f32)
    _, o = lax.scan(step, s0, (q_dec, k_dec, qk, u0, wk, g_last))
    return o.transpose(1, 0, 3, 2, 4).reshape(b_, t_, h_, dv)


def deltanet_mixer(x, w_in, conv_w, a_log, dt_bias, norm_w, w_out):
    b_, t_, _ = x.shape
    hk = N_HEADS_A * HEAD_DIM_A
    proj = x @ w_in
    qkv = jax.nn.silu(causal_dwconv(proj[..., :3 * hk], conv_w))
    z = proj[..., 3 * hk:4 * hk].reshape(b_, t_, N_HEADS_A, HEAD_DIM_A)
    b_logit = proj[..., 4 * hk:4 * hk + N_HEADS_A]
    a_in = proj[..., 4 * hk + N_HEADS_A:]
    q = l2norm(qkv[..., :hk].reshape(b_, t_, N_HEADS_A, HEAD_DIM_A))
    k = l2norm(qkv[..., hk:2 * hk].reshape(b_, t_, N_HEADS_A, HEAD_DIM_A))
    v = qkv[..., 2 * hk:].reshape(b_, t_, N_HEADS_A, HEAD_DIM_A)
    beta = jax.nn.sigmoid(b_logit.astype(jnp.float32))
    g = -jnp.exp(a_log.astype(jnp.float32)) * jax.nn.softplus(a_in.astype(jnp.float32) + dt_bias.astype(jnp.float32))
    o = gated_delta_rule(q, k, v, g, beta).astype(x.dtype)
    o = rmsnorm(o, norm_w) * jax.nn.silu(z)
    return o.reshape(b_, t_, hk) @ w_out


def alibi_slopes(n_heads):
    return 2.0 ** (-8.0 * jnp.arange(1, n_heads + 1, dtype=jnp.float32) / n_heads)


def swa_mixer(x, w_in, sinks, w_out):
    b_, t_, _ = x.shape
    n_blk = t_ // WINDOW
    grp = N_HEADS_B // N_KV_B
    qd = N_HEADS_B * HEAD_DIM_B
    kd = N_KV_B * HEAD_DIM_B
    proj = x @ w_in
    q = proj[..., :qd].reshape(b_, n_blk, WINDOW, N_KV_B, grp, HEAD_DIM_B)
    k = proj[..., qd:qd + kd].reshape(b_, n_blk, WINDOW, N_KV_B, HEAD_DIM_B)
    v = proj[..., qd + kd:].reshape(b_, n_blk, WINDOW, N_KV_B, HEAD_DIM_B)

    def with_prev(a):
        prev = jnp.pad(a[:, :-1], ((0, 0), (1, 0), (0, 0), (0, 0), (0, 0)))
        return jnp.concatenate([prev, a], axis=2)

    kb, vb = with_prev(k), with_prev(v)
    scores = jnp.einsum('bnqhgd,bnkhd->bhgnqk', q, kb).astype(jnp.float32) * (HEAD_DIM_B ** -0.5)
    qi = jnp.arange(WINDOW)[:, None]
    kj = jnp.arange(2 * WINDOW)[None, :]
    dist = qi + WINDOW - kj
    blk = jnp.arange(n_blk)[:, None, None]
    valid = (dist >= 0) & (dist < WINDOW) & (blk * WINDOW - WINDOW + kj >= 0)
    slopes = alibi_slopes(N_HEADS_B).reshape(N_KV_B, grp)
    logits = scores - slopes[None, :, :, None, None, None] * dist.astype(jnp.float32)
    logits = jnp.where(valid, logits, -jnp.inf)
    sink = sinks.astype(jnp.float32).reshape(1, N_KV_B, grp, 1, 1, 1)
    m = jnp.maximum(jnp.max(logits, axis=-1, keepdims=True), sink)
    e = jnp.exp(logits - m)
    probs = e / (jnp.sum(e, axis=-1, keepdims=True) + jnp.exp(sink - m))
    out = jnp.einsum('bhgnqk,bnkhd->bnqhgd', probs.astype(x.dtype), vb).reshape(b_, t_, qd)
    return out @ w_out


def conv_ffn(x, w_up, conv_w, w_down):
    u = causal_dwconv(x @ w_up, conv_w)
    gate, val = u[..., :D_FF], u[..., D_FF:]
    return (jax.nn.silu(gate) * val) @ w_down


def _fwd_setup_inputs(seed: int = 0) -> dict:
    key = jax.random.key(seed)
    ks = jax.random.split(key, 24)
    f32 = jnp.float32
    hk = N_HEADS_A * HEAD_DIM_A

    def dense(k, shape, fan_in):
        return jax.random.normal(k, shape, f32) * (fan_in ** -0.5)

    def gain(k, shape):
        return 1.0 + 0.02 * jax.random.normal(k, shape, f32)

    x = jax.random.normal(ks[0], (BATCH, SEQ, D_MODEL), f32)
    p = jax.random.normal(ks[1], (DEPTH, BATCH, SEQ, PLE_DIM), f32)
    norm_mix = gain(ks[2], (DEPTH, D_MODEL))
    norm_ffn = gain(ks[3], (DEPTH, D_MODEL))
    norm_ple = gain(ks[4], (DEPTH, D_MODEL))
    norm_final = gain(ks[5], (D_MODEL,))

    a_w_in = dense(ks[6], (N_A_LAYERS, D_MODEL, 4 * hk + 2 * N_HEADS_A), D_MODEL)
    a_conv = jax.random.normal(ks[7], (N_A_LAYERS, CONV_A, 3 * hk), f32) * (CONV_A ** -0.5)
    a_log = jnp.log(jax.random.uniform(ks[8], (N_A_LAYERS, N_HEADS_A), f32, 1.0, 16.0))
    dt = jnp.exp(jax.random.uniform(ks[9], (N_A_LAYERS, N_HEADS_A), f32, math.log(1e-3), math.log(1e-1)))
    a_dt_bias = dt + jnp.log(-jnp.expm1(-dt))
    a_norm = gain(ks[10], (N_A_LAYERS, HEAD_DIM_A))
    a_w_out = dense(ks[11], (N_A_LAYERS, hk, D_MODEL), hk)

    qd = N_HEADS_B * HEAD_DIM_B
    kd = N_KV_B * HEAD_DIM_B
    b_w_in = dense(ks[12], (N_B_LAYERS, D_MODEL, qd + 2 * kd), D_MODEL)
    b_sinks = jax.random.normal(ks[13], (N_B_LAYERS, N_HEADS_B), f32)
    b_w_out = dense(ks[14], (N_B_LAYERS, qd, D_MODEL), qd)

    f_w_up = dense(ks[15], (DEPTH, D_MODEL, 2 * D_FF), D_MODEL)
    f_conv = jax.random.normal(ks[16], (DEPTH, FFN_CONV, 2 * D_FF), f32) * (FFN_CONV ** -0.5)
    f_w_down = dense(ks[17], (DEPTH, D_FF, D_MODEL), D_FF)

    ple_w_proj = dense(ks[18], (DEPTH, PLE_DIM, D_MODEL), PLE_DIM)
    ple_w_gate = dense(ks[19], (DEPTH, D_MODEL, D_MODEL), D_MODEL)

    return {"x": x, "p": p, "norm_mix": norm_mix, "norm_ffn": norm_ffn,
            "norm_ple": norm_ple, "norm_final": norm_final,
            "a_w_in": a_w_in, "a_conv": a_conv, "a_log": a_log, "a_dt_bias": a_dt_bias,
            "a_norm": a_norm, "a_w_out": a_w_out,
            "b_w_in": b_w_in, "b_sinks": b_sinks, "b_w_out": b_w_out,
            "f_w_up": f_w_up, "f_conv": f_conv, "f_w_down": f_w_down,
            "ple_w_proj": ple_w_proj, "ple_w_gate": ple_w_gate}


def _fwd_reference(x, p, norm_mix, norm_ffn, norm_ple, norm_final,
              a_w_in, a_conv, a_log, a_dt_bias, a_norm, a_w_out,
              b_w_in, b_sinks, b_w_out,
              f_w_up, f_conv, f_w_down,
              ple_w_proj, ple_w_gate):
    h = x
    for i in range(DEPTH):
        hn = rmsnorm(h, norm_mix[i])
        j = i // 2
        if i % 2 == 0:
            mix = deltanet_mixer(hn, a_w_in[j], a_conv[j], a_log[j], a_dt_bias[j], a_norm[j], a_w_out[j])
        else:
            mix = swa_mixer(hn, b_w_in[j], b_sinks[j], b_w_out[j])
        h = h + mix
        h = h + conv_ffn(rmsnorm(h, norm_ffn[i]), f_w_up[i], f_conv[i], f_w_down[i])
        gate = jax.nn.sigmoid(rmsnorm(h, norm_ple[i]) @ ple_w_gate[i])
        h = h + gate * (p[i] @ ple_w_proj[i])
    return rmsnorm(h, norm_final)


import jax as _jax
import jax.numpy as _jnp

TWIN_FORMAT = 'train_step'
FWD_PARAMS = ['x', 'p', 'norm_mix', 'norm_ffn', 'norm_ple', 'norm_final', 'a_w_in', 'a_conv', 'a_log', 'a_dt_bias', 'a_norm', 'a_w_out', 'b_w_in', 'b_sinks', 'b_w_out', 'f_w_up', 'f_conv', 'f_w_down', 'ple_w_proj', 'ple_w_gate']
TWIN_WEIGHTS = ['norm_mix', 'norm_ffn', 'norm_ple', 'norm_final', 'a_w_in', 'a_conv', 'a_log', 'a_dt_bias', 'a_norm', 'a_w_out', 'b_w_in', 'b_sinks', 'b_w_out', 'f_w_up', 'f_conv', 'f_w_down', 'ple_w_proj', 'ple_w_gate']
TWIN_DIFF_INPUT = 'x'
TWIN_INPUTS = ['x', 'p', 'norm_mix', 'norm_ffn', 'norm_ple', 'norm_final', 'a_w_in', 'a_conv', 'a_log', 'a_dt_bias', 'a_norm', 'a_w_out', 'b_w_in', 'b_sinks', 'b_w_out', 'f_w_up', 'f_conv', 'f_w_down', 'ple_w_proj', 'ple_w_gate', 'loss_target', 'm_norm_mix', 'm_norm_ffn', 'm_norm_ple', 'm_norm_final', 'm_a_w_in', 'm_a_conv', 'm_a_log', 'm_a_dt_bias', 'm_a_norm', 'm_a_w_out', 'm_b_w_in', 'm_b_sinks', 'm_b_w_out', 'm_f_w_up', 'm_f_conv', 'm_f_w_down', 'm_ple_w_proj', 'm_ple_w_gate', 'v_norm_mix', 'v_norm_ffn', 'v_norm_ple', 'v_norm_final', 'v_a_w_in', 'v_a_conv', 'v_a_log', 'v_a_dt_bias', 'v_a_norm', 'v_a_w_out', 'v_b_w_in', 'v_b_sinks', 'v_b_w_out', 'v_f_w_up', 'v_f_conv', 'v_f_w_down', 'v_ple_w_proj', 'v_ple_w_gate']
TWIN_OUTPUTS = ['loss', 'grad_x', 'grad_norm_mix', 'grad_norm_ffn', 'grad_norm_ple', 'grad_norm_final', 'grad_a_w_in', 'grad_a_conv', 'grad_a_log', 'grad_a_dt_bias', 'grad_a_norm', 'grad_a_w_out', 'grad_b_w_in', 'grad_b_sinks', 'grad_b_w_out', 'grad_f_w_up', 'grad_f_conv', 'grad_f_w_down', 'grad_ple_w_proj', 'grad_ple_w_gate', 'delta_norm_mix', 'delta_norm_ffn', 'delta_norm_ple', 'delta_norm_final', 'delta_a_w_in', 'delta_a_conv', 'delta_a_log', 'delta_a_dt_bias', 'delta_a_norm', 'delta_a_w_out', 'delta_b_w_in', 'delta_b_sinks', 'delta_b_w_out', 'delta_f_w_up', 'delta_f_conv', 'delta_f_w_down', 'delta_ple_w_proj', 'delta_ple_w_gate', 'new_m_norm_mix', 'new_m_norm_ffn', 'new_m_norm_ple', 'new_m_norm_final', 'new_m_a_w_in', 'new_m_a_conv', 'new_m_a_log', 'new_m_a_dt_bias', 'new_m_a_norm', 'new_m_a_w_out', 'new_m_b_w_in', 'new_m_b_sinks', 'new_m_b_w_out', 'new_m_f_w_up', 'new_m_f_conv', 'new_m_f_w_down', 'new_m_ple_w_proj', 'new_m_ple_w_gate', 'new_v_norm_mix', 'new_v_norm_ffn', 'new_v_norm_ple', 'new_v_norm_final', 'new_v_a_w_in', 'new_v_a_conv', 'new_v_a_log', 'new_v_a_dt_bias', 'new_v_a_norm', 'new_v_a_w_out', 'new_v_b_w_in', 'new_v_b_sinks', 'new_v_b_w_out', 'new_v_f_w_up', 'new_v_f_conv', 'new_v_f_w_down', 'new_v_ple_w_proj', 'new_v_ple_w_gate']
TWIN_LEAF_KINDS = {'loss': 'loss', 'grad_x': 'grad_x', 'grad_norm_mix': 'grad_w', 'grad_norm_ffn': 'grad_w', 'grad_norm_ple': 'grad_w', 'grad_norm_final': 'grad_w', 'grad_a_w_in': 'grad_w', 'grad_a_conv': 'grad_w', 'grad_a_log': 'grad_w', 'grad_a_dt_bias': 'grad_w', 'grad_a_norm': 'grad_w', 'grad_a_w_out': 'grad_w', 'grad_b_w_in': 'grad_w', 'grad_b_sinks': 'grad_w', 'grad_b_w_out': 'grad_w', 'grad_f_w_up': 'grad_w', 'grad_f_conv': 'grad_w', 'grad_f_w_down': 'grad_w', 'grad_ple_w_proj': 'grad_w', 'grad_ple_w_gate': 'grad_w', 'delta_norm_mix': 'delta_w', 'delta_norm_ffn': 'delta_w', 'delta_norm_ple': 'delta_w', 'delta_norm_final': 'delta_w', 'delta_a_w_in': 'delta_w', 'delta_a_conv': 'delta_w', 'delta_a_log': 'delta_w', 'delta_a_dt_bias': 'delta_w', 'delta_a_norm': 'delta_w', 'delta_a_w_out': 'delta_w', 'delta_b_w_in': 'delta_w', 'delta_b_sinks': 'delta_w', 'delta_b_w_out': 'delta_w', 'delta_f_w_up': 'delta_w', 'delta_f_conv': 'delta_w', 'delta_f_w_down': 'delta_w', 'delta_ple_w_proj': 'delta_w', 'delta_ple_w_gate': 'delta_w', 'new_m_norm_mix': 'new_m', 'new_m_norm_ffn': 'new_m', 'new_m_norm_ple': 'new_m', 'new_m_norm_final': 'new_m', 'new_m_a_w_in': 'new_m', 'new_m_a_conv': 'new_m', 'new_m_a_log': 'new_m', 'new_m_a_dt_bias': 'new_m', 'new_m_a_norm': 'new_m', 'new_m_a_w_out': 'new_m', 'new_m_b_w_in': 'new_m', 'new_m_b_sinks': 'new_m', 'new_m_b_w_out': 'new_m', 'new_m_f_w_up': 'new_m', 'new_m_f_conv': 'new_m', 'new_m_f_w_down': 'new_m', 'new_m_ple_w_proj': 'new_m', 'new_m_ple_w_gate': 'new_m', 'new_v_norm_mix': 'new_v', 'new_v_norm_ffn': 'new_v', 'new_v_norm_ple': 'new_v', 'new_v_norm_final': 'new_v', 'new_v_a_w_in': 'new_v', 'new_v_a_conv': 'new_v', 'new_v_a_log': 'new_v', 'new_v_a_dt_bias': 'new_v', 'new_v_a_norm': 'new_v', 'new_v_a_w_out': 'new_v', 'new_v_b_w_in': 'new_v', 'new_v_b_sinks': 'new_v', 'new_v_b_w_out': 'new_v', 'new_v_f_w_up': 'new_v', 'new_v_f_conv': 'new_v', 'new_v_f_w_down': 'new_v', 'new_v_ple_w_proj': 'new_v', 'new_v_ple_w_gate': 'new_v'}


def _forward(args):
    return _fwd_reference(*[args[k] for k in FWD_PARAMS])


def _output_shape():
    out = _jax.eval_shape(lambda: _forward(_fwd_setup_inputs(0)))
    return out.shape, out.dtype

N_MICROBATCH = 1
ADAM_LR = 0.001
ADAM_B1 = 0.9
ADAM_B2 = 0.999
ADAM_EPS = 1e-08
ADAM_WD = 0.01
ADAM_STEP = 10
PER_EXAMPLE_BATCH_AXIS = {'x': 0, 'p': 1, 'loss_target': 0}
SHARED_INPUTS = []
_WEIGHT_DTYPES = {'norm_mix': _jnp.float32, 'norm_ffn': _jnp.float32, 'norm_ple': _jnp.float32, 'norm_final': _jnp.float32, 'a_w_in': _jnp.float32, 'a_conv': _jnp.float32, 'a_log': _jnp.float32, 'a_dt_bias': _jnp.float32, 'a_norm': _jnp.float32, 'a_w_out': _jnp.float32, 'b_w_in': _jnp.float32, 'b_sinks': _jnp.float32, 'b_w_out': _jnp.float32, 'f_w_up': _jnp.float32, 'f_conv': _jnp.float32, 'f_w_down': _jnp.float32, 'ple_w_proj': _jnp.float32, 'ple_w_gate': _jnp.float32}
MOMENT_SCALE = {'norm_mix': 1.247424e-01, 'norm_ffn': 1.173501e-01, 'norm_ple': 2.639357e-02, 'norm_final': 3.198395e+01, 'a_w_in': 8.221275e-02, 'a_conv': 7.724014e-02, 'a_log': 6.003685e-01, 'a_dt_bias': 5.347590e-01, 'a_norm': 3.907730e-01, 'a_w_out': 9.744476e-02, 'b_w_in': 4.538658e-02, 'b_sinks': 5.138073e-02, 'b_w_out': 3.858284e-02, 'f_w_up': 4.502725e-02, 'f_conv': 4.521786e-02, 'f_w_down': 7.378101e-02, 'ple_w_proj': 6.497710e-02, 'ple_w_gate': 2.541492e-02}


def _to_microbatches(a, axis):
    t = _jnp.moveaxis(a, axis, 0)
    t = t.reshape((N_MICROBATCH, t.shape[0] // N_MICROBATCH) + t.shape[1:])
    return _jnp.moveaxis(t, 1, axis + 1)


def setup_inputs(seed: int = 0) -> dict:
    inp = _fwd_setup_inputs(seed)
    key = _jax.random.fold_in(_jax.random.key(seed), 7919)
    shape, _ = _output_shape()
    out = dict(inp)
    out["loss_target"] = _jax.random.normal(_jax.random.fold_in(key, 0), shape, _jnp.float32)
    for i, name in enumerate(TWIN_WEIGHTS):
        w = inp[name].astype(_jnp.float32)
        if MOMENT_SCALE is None:
            s = _jnp.sqrt(_jnp.mean(_jnp.square(w)) + 1e-30)
        else:
            s = MOMENT_SCALE[name]
        km, kv = _jax.random.split(_jax.random.fold_in(key, i + 1))
        out[name] = w
        out["m_" + name] = s * _jax.random.normal(km, w.shape, _jnp.float32)
        out["v_" + name] = (s * s) * _jax.random.uniform(kv, w.shape, _jnp.float32, 0.5, 1.5)
    if N_MICROBATCH > 1:
        for name, axis in PER_EXAMPLE_BATCH_AXIS.items():
            out[name] = _to_microbatches(out[name], axis)
    return {'x': out['x'], 'p': out['p'], 'norm_mix': out['norm_mix'], 'norm_ffn': out['norm_ffn'], 'norm_ple': out['norm_ple'], 'norm_final': out['norm_final'], 'a_w_in': out['a_w_in'], 'a_conv': out['a_conv'], 'a_log': out['a_log'], 'a_dt_bias': out['a_dt_bias'], 'a_norm': out['a_norm'], 'a_w_out': out['a_w_out'], 'b_w_in': out['b_w_in'], 'b_sinks': out['b_sinks'], 'b_w_out': out['b_w_out'], 'f_w_up': out['f_w_up'], 'f_conv': out['f_conv'], 'f_w_down': out['f_w_down'], 'ple_w_proj': out['ple_w_proj'], 'ple_w_gate': out['ple_w_gate'], 'loss_target': out['loss_target'], 'm_norm_mix': out['m_norm_mix'], 'm_norm_ffn': out['m_norm_ffn'], 'm_norm_ple': out['m_norm_ple'], 'm_norm_final': out['m_norm_final'], 'm_a_w_in': out['m_a_w_in'], 'm_a_conv': out['m_a_conv'], 'm_a_log': out['m_a_log'], 'm_a_dt_bias': out['m_a_dt_bias'], 'm_a_norm': out['m_a_norm'], 'm_a_w_out': out['m_a_w_out'], 'm_b_w_in': out['m_b_w_in'], 'm_b_sinks': out['m_b_sinks'], 'm_b_w_out': out['m_b_w_out'], 'm_f_w_up': out['m_f_w_up'], 'm_f_conv': out['m_f_conv'], 'm_f_w_down': out['m_f_w_down'], 'm_ple_w_proj': out['m_ple_w_proj'], 'm_ple_w_gate': out['m_ple_w_gate'], 'v_norm_mix': out['v_norm_mix'], 'v_norm_ffn': out['v_norm_ffn'], 'v_norm_ple': out['v_norm_ple'], 'v_norm_final': out['v_norm_final'], 'v_a_w_in': out['v_a_w_in'], 'v_a_conv': out['v_a_conv'], 'v_a_log': out['v_a_log'], 'v_a_dt_bias': out['v_a_dt_bias'], 'v_a_norm': out['v_a_norm'], 'v_a_w_out': out['v_a_w_out'], 'v_b_w_in': out['v_b_w_in'], 'v_b_sinks': out['v_b_sinks'], 'v_b_w_out': out['v_b_w_out'], 'v_f_w_up': out['v_f_w_up'], 'v_f_conv': out['v_f_conv'], 'v_f_w_down': out['v_f_w_down'], 'v_ple_w_proj': out['v_ple_w_proj'], 'v_ple_w_gate': out['v_ple_w_gate']}


def _loss(weights, diff, rest, loss_target):
    with _jax.named_scope("forward"):
        args = {**rest, TWIN_DIFF_INPUT: diff, **{k: w.astype(_WEIGHT_DTYPES[k]) for k, w in weights.items()}}
        y = _forward(args)
    with _jax.named_scope("loss_head"):
        err = _jnp.square(y.astype(_jnp.float32) - loss_target)
        return 0.5 * _jnp.sum(_jnp.mean(err, axis=-1)) if err.ndim else 0.5 * err


def _adamw(w, g, m, v):
    m = ADAM_B1 * m + (1.0 - ADAM_B1) * g
    v = ADAM_B2 * v + (1.0 - ADAM_B2) * _jnp.square(g)
    m_hat = m / (1.0 - ADAM_B1 ** ADAM_STEP)
    v_hat = v / (1.0 - ADAM_B2 ** ADAM_STEP)
    delta = -ADAM_LR * (m_hat / (_jnp.sqrt(v_hat) + ADAM_EPS) + ADAM_WD * w)
    return delta, m, v


def reference(x, p, norm_mix, norm_ffn, norm_ple, norm_final, a_w_in, a_conv, a_log, a_dt_bias, a_norm, a_w_out, b_w_in, b_sinks, b_w_out, f_w_up, f_conv, f_w_down, ple_w_proj, ple_w_gate, loss_target, m_norm_mix, m_norm_ffn, m_norm_ple, m_norm_final, m_a_w_in, m_a_conv, m_a_log, m_a_dt_bias, m_a_norm, m_a_w_out, m_b_w_in, m_b_sinks, m_b_w_out, m_f_w_up, m_f_conv, m_f_w_down, m_ple_w_proj, m_ple_w_gate, v_norm_mix, v_norm_ffn, v_norm_ple, v_norm_final, v_a_w_in, v_a_conv, v_a_log, v_a_dt_bias, v_a_norm, v_a_w_out, v_b_w_in, v_b_sinks, v_b_w_out, v_f_w_up, v_f_conv, v_f_w_down, v_ple_w_proj, v_ple_w_gate):
    given = dict(x=x, p=p, norm_mix=norm_mix, norm_ffn=norm_ffn, norm_ple=norm_ple, norm_final=norm_final, a_w_in=a_w_in, a_conv=a_conv, a_log=a_log, a_dt_bias=a_dt_bias, a_norm=a_norm, a_w_out=a_w_out, b_w_in=b_w_in, b_sinks=b_sinks, b_w_out=b_w_out, f_w_up=f_w_up, f_conv=f_conv, f_w_down=f_w_down, ple_w_proj=ple_w_proj, ple_w_gate=ple_w_gate, loss_target=loss_target, m_norm_mix=m_norm_mix, m_norm_ffn=m_norm_ffn, m_norm_ple=m_norm_ple, m_norm_final=m_norm_final, m_a_w_in=m_a_w_in, m_a_conv=m_a_conv, m_a_log=m_a_log, m_a_dt_bias=m_a_dt_bias, m_a_norm=m_a_norm, m_a_w_out=m_a_w_out, m_b_w_in=m_b_w_in, m_b_sinks=m_b_sinks, m_b_w_out=m_b_w_out, m_f_w_up=m_f_w_up, m_f_conv=m_f_conv, m_f_w_down=m_f_w_down, m_ple_w_proj=m_ple_w_proj, m_ple_w_gate=m_ple_w_gate, v_norm_mix=v_norm_mix, v_norm_ffn=v_norm_ffn, v_norm_ple=v_norm_ple, v_norm_final=v_norm_final, v_a_w_in=v_a_w_in, v_a_conv=v_a_conv, v_a_log=v_a_log, v_a_dt_bias=v_a_dt_bias, v_a_norm=v_a_norm, v_a_w_out=v_a_w_out, v_b_w_in=v_b_w_in, v_b_sinks=v_b_sinks, v_b_w_out=v_b_w_out, v_f_w_up=v_f_w_up, v_f_conv=v_f_conv, v_f_w_down=v_f_w_down, v_ple_w_proj=v_ple_w_proj, v_ple_w_gate=v_ple_w_gate)
    weights = {n: given[n] for n in TWIN_WEIGHTS}
    shared = {n: given[n] for n in SHARED_INPUTS}
    per_example = {n: given[n] for n in ['x', 'p']}
    grad_fn = _jax.value_and_grad(_loss, argnums=(0, 1))

    def one_microbatch(ex, loss_target):
        ex = dict(ex)
        diff = ex.pop(TWIN_DIFF_INPUT)
        return grad_fn(weights, diff, {**shared, **ex}, loss_target)

    if N_MICROBATCH == 1:
        loss, (grad_w, grad_x) = one_microbatch(per_example, given["loss_target"])
    else:
        def body(carry, xs):
            loss_sum, grad_sum = carry
            l_k, (gw_k, gx_k) = one_microbatch(xs[0], xs[1])
            with _jax.named_scope("update"):
                return (loss_sum + l_k, _jax.tree.map(_jnp.add, grad_sum, gw_k)), gx_k

        init = (_jnp.zeros((), _jnp.float32), _jax.tree.map(_jnp.zeros_like, weights))
        (loss, grad_w), grad_x = _jax.lax.scan(body, init, (per_example, given["loss_target"]))
    with _jax.named_scope("update"):
        delta_w, new_m, new_v = {}, {}, {}
        for n in TWIN_WEIGHTS:
            delta_w[n], new_m[n], new_v[n] = _adamw(weights[n], grad_w[n], given["m_" + n], given["v_" + n])
    return (loss, grad_x, *[grad_w[n] for n in TWIN_WEIGHTS], *[delta_w[n] for n in TWIN_WEIGHTS],
            *[new_m[n] for n in TWIN_WEIGHTS], *[new_v[n] for n in TWIN_WEIGHTS])
```

```python
import functools
import math

import jax
import jax.numpy as jnp
from jax import lax
from jax.experimental import pallas as pl
from jax.experimental.pallas import tpu as pltpu

F32 = jnp.float32
BF16 = jnp.bfloat16

D_MODEL = 1024
N_HEADS_A = 8
HEAD_DIM_A = 128
CONV_A = 4
CHUNK = 128
N_HEADS_B = 16
N_KV_B = 4
GROUP_B = N_HEADS_B // N_KV_B
HEAD_DIM_B = 64
WINDOW = 128
D_FF = 2816
FFN_CONV = 3
PLE_DIM = 256
EPS = 1e-6
N_DEV = 8
HALO = 8

ADAM_LR = 0.001
ADAM_B1 = 0.9
ADAM_B2 = 0.999
ADAM_EPS = 1e-08
ADAM_WD = 0.01
ADAM_STEP = 10

LANES = 128
VMEM_LIMIT_BYTES = 56 * 1024 * 1024
NEG_BIG = -1e30

MESH_AXES = ("x", "y", "c")


def _params(sem=None):
    return pltpu.CompilerParams(dimension_semantics=sem, vmem_limit_bytes=VMEM_LIMIT_BYTES)


def _tile(n, target):
    best = None
    for t in range(LANES, min(n, target) + 1, LANES):
        if n % t == 0:
            best = t
    return best or n


def _sigmoid(x):
    return 1.0 / (1.0 + jnp.exp(-x))


def _softplus(x):
    return jnp.maximum(x, 0.0) + jnp.log1p(jnp.exp(-jnp.abs(x)))


def _matmul(a, b, *, name, ta=False, tb=False, res=None, out_dtype=F32, tm=512, tn=512, tk=2048):
    m = a.shape[1] if ta else a.shape[0]
    k = a.shape[0] if ta else a.shape[1]
    n = b.shape[0] if tb else b.shape[1]
    assert (b.shape[1] if tb else b.shape[0]) == k, (a.shape, b.shape, ta, tb)
    tm, tn, tk = _tile(m, tm), _tile(n, tn), _tile(k, tk)
    nk = k // tk
    dims = (((0 if ta else 1,), (1 if tb else 0,)), ((), ()))
    a_spec = pl.BlockSpec((tk, tm), lambda i, j, kk: (kk, i)) if ta else pl.BlockSpec((tm, tk), lambda i, j, kk: (i, kk))
    b_spec = pl.BlockSpec((tn, tk), lambda i, j, kk: (j, kk)) if tb else pl.BlockSpec((tk, tn), lambda i, j, kk: (kk, j))
    o_spec = pl.BlockSpec((tm, tn), lambda i, j, kk: (i, j))
    has_res = res is not None

    def body(*refs):
        a_ref, b_ref = refs[0], refs[1]
        r_ref = refs[2] if has_res else None
        o_ref = refs[3] if has_res else refs[2]
        part = lax.dot_general(a_ref[...].astype(BF16), b_ref[...].astype(BF16), dims, preferred_element_type=F32)

        def finish(acc):
            if has_res:
                acc = acc + r_ref[...].astype(F32)
            o_ref[...] = acc.astype(out_dtype)

        if nk == 1:
            finish(part)
        else:
            acc_ref = refs[-1]
            kk = pl.program_id(2)

            @pl.when(kk == 0)
            def _():
                acc_ref[...] = part

            @pl.when(kk > 0)
            def _():
                acc_ref[...] += part

            @pl.when(kk == nk - 1)
            def _():
                finish(acc_ref[...])

    in_specs = [a_spec, b_spec] + ([o_spec] if has_res else [])
    args = (a, b) + ((res,) if has_res else ())
    return pl.pallas_call(
        body,
        name=name,
        grid=(m // tm, n // tn, nk),
        in_specs=in_specs,
        out_specs=o_spec,
        out_shape=jax.ShapeDtypeStruct((m, n), out_dtype),
        scratch_shapes=[pltpu.VMEM((tm, tn), F32)] if nk > 1 else [],
        compiler_params=_params(("parallel", "parallel", "arbitrary")),
    )(*args)


def _rms_fwd(h, w, *, name, tm=512):
    t, d = h.shape
    tm = _tile(t, tm)

    def body(h_ref, w_ref, o_ref):
        x = h_ref[...]
        r = lax.rsqrt(jnp.mean(x * x, axis=-1, keepdims=True) + EPS)
        o_ref[...] = (x * r * w_ref[...]).astype(BF16)

    return pl.pallas_call(
        body,
        name=name,
        grid=(t // tm,),
        in_specs=[pl.BlockSpec((tm, d), lambda i: (i, 0)), pl.BlockSpec((1, d), lambda i: (0, 0))],
        out_specs=pl.BlockSpec((tm, d), lambda i: (i, 0)),
        out_shape=jax.ShapeDtypeStruct((t, d), BF16),
        compiler_params=_params(("parallel",)),
    )(h, w.reshape(1, d))


def _rms_bwd(h, w, dn, skip, *, name, tm=512):
    t, d = h.shape
    tm = _tile(t, tm)

    def body(h_ref, w_ref, dn_ref, skip_ref, dh_ref, dw_ref):
        i = pl.program_id(0)
        x = h_ref[...]
        r = lax.rsqrt(jnp.mean(x * x, axis=-1, keepdims=True) + EPS)
        nh = x * r
        g = dn_ref[...].astype(F32)
        gw = g * w_ref[...]
        dh_ref[...] = r * (gw - nh * jnp.mean(gw * nh, axis=-1, keepdims=True)) + skip_ref[...]
        part = jnp.sum(g * nh, axis=0, keepdims=True)

        @pl.when(i == 0)
        def _():
            dw_ref[...] = part

        @pl.when(i > 0)
        def _():
            dw_ref[...] += part

    row = pl.BlockSpec((tm, d), lambda i: (i, 0))
    vec = pl.BlockSpec((1, d), lambda i: (0, 0))
    return pl.pallas_call(
        body,
        name=name,
        grid=(t // tm,),
        in_specs=[row, vec, row, row],
        out_specs=[row, vec],
        out_shape=[jax.ShapeDtypeStruct((t, d), F32), jax.ShapeDtypeStruct((1, d), F32)],
        compiler_params=_params(("arbitrary",)),
    )(h, w.reshape(1, d), dn, skip)


def _final_loss(h, w, target, *, name, tm=512):
    t, d = h.shape
    tm = _tile(t, tm)

    def body(h_ref, w_ref, tg_ref, loss_ref, dh_ref, dw_ref):
        i = pl.program_id(0)
        x = h_ref[...]
        r = lax.rsqrt(jnp.mean(x * x, axis=-1, keepdims=True) + EPS)
        nh = x * r
        err = nh * w_ref[...] - tg_ref[...]
        lpart = (0.5 / d) * jnp.sum(jnp.sum(err * err, axis=-1, keepdims=True), axis=0, keepdims=True)
        g = err * (1.0 / d)
        gw = g * w_ref[...]
        dh_ref[...] = r * (gw - nh * jnp.mean(gw * nh, axis=-1, keepdims=True))
        part = jnp.sum(g * nh, axis=0, keepdims=True)
        lrow = jnp.broadcast_to(lpart, (1, LANES))

        @pl.when(i == 0)
        def _():
            dw_ref[...] = part
            loss_ref[...] = lrow

        @pl.when(i > 0)
        def _():
            dw_ref[...] += part
            loss_ref[...] += lrow

    row = pl.BlockSpec((tm, d), lambda i: (i, 0))
    vec = pl.BlockSpec((1, d), lambda i: (0, 0))
    return pl.pallas_call(
        body,
        name=name,
        grid=(t // tm,),
        in_specs=[row, vec, row],
        out_specs=[pl.BlockSpec((1, LANES), lambda i: (0, 0)), row, vec],
        out_shape=[jax.ShapeDtypeStruct((1, LANES), F32), jax.ShapeDtypeStruct((t, d), F32), jax.ShapeDtypeStruct((1, d), F32)],
        compiler_params=_params(("arbitrary",)),
    )(h, w.reshape(1, d), target)


def _conv_from_ext(ext_ref, cw_ref, kw, tm):
    y = cw_ref[kw - 1:kw, :] * ext_ref[pl.ds(HALO, tm), :]
    for i in range(kw - 1):
        y = y + cw_ref[i:i + 1, :] * ext_ref[pl.ds(HALO - (kw - 1) + i, tm), :]
    return y


def _conv_bwd_from_ext(xext_ref, dyext_ref, cw_ref, dcw_ref, kw, tm, first):
    dy = dyext_ref[pl.ds(0, tm), :]
    dx = cw_ref[kw - 1:kw, :] * dy
    for i in range(kw - 1):
        dx = dx + cw_ref[i:i + 1, :] * dyext_ref[pl.ds(kw - 1 - i, tm), :]
    for i in range(kw):
        part = jnp.sum(dy * xext_ref[pl.ds(HALO - (kw - 1) + i, tm), :], axis=0, keepdims=True)

        @pl.when(first)
        def _():
            dcw_ref[i:i + 1, :] = part

        @pl.when(jnp.logical_not(first))
        def _():
            dcw_ref[i:i + 1, :] += part

    return dx


def _delta_pre_fwd(pqkv, pba, conv_w, alog_row, dtb_row, *, name, tm=256):
    t, c3 = pqkv.shape
    hk = N_HEADS_A * HEAD_DIM_A
    tm = _tile(t, tm)

    def body(x_ref, ba_ref, cw_ref, al_ref, db_ref, q_ref, k_ref, v_ref, g_ref, b_ref, ext):
        i = pl.program_id(0)

        @pl.when(i == 0)
        def _():
            ext[0:HALO, :] = jnp.zeros((HALO, c3), F32)

        ext[pl.ds(HALO, tm), :] = x_ref[...]
        y = _conv_from_ext(ext, cw_ref, CONV_A, tm)
        ext[0:HALO, :] = ext[pl.ds(tm, HALO), :]
        s = y * _sigmoid(y)
        for h in range(N_HEADS_A):
            lo = h * HEAD_DIM_A
            for dst, off in ((q_ref, 0), (k_ref, hk)):
                sh = s[:, off + lo:off + lo + HEAD_DIM_A]
                dst[:, lo:lo + HEAD_DIM_A] = sh * lax.rsqrt(jnp.sum(sh * sh, axis=-1, keepdims=True) + EPS)
        v_ref[...] = s[:, 2 * hk:3 * hk]
        ba = ba_ref[...]
        beta = _sigmoid(ba)
        gfull = -jnp.exp(al_ref[...]) * _softplus(ba + db_ref[...])
        for h in range(N_HEADS_A):
            lo = h * HEAD_DIM_A
            b_ref[:, lo:lo + HEAD_DIM_A] = jnp.broadcast_to(beta[:, h:h + 1], (tm, HEAD_DIM_A))
            g_ref[:, lo:lo + HEAD_DIM_A] = jnp.broadcast_to(gfull[:, N_HEADS_A + h:N_HEADS_A + h + 1], (tm, HEAD_DIM_A))

    row = lambda w: pl.BlockSpec((tm, w), lambda i: (i, 0))
    fixed = lambda r, w: pl.BlockSpec((r, w), lambda i: (0, 0))
    out = jax.ShapeDtypeStruct((t, hk), F32)
    return pl.pallas_call(
        body,
        name=name,
        grid=(t // tm,),
        in_specs=[row(c3), row(LANES), fixed(CONV_A, c3), fixed(1, LANES), fixed(1, LANES)],
        out_specs=[row(hk)] * 5,
        out_shape=[out] * 5,
        scratch_shapes=[pltpu.VMEM((HALO + tm, c3), F32)],
        compiler_params=_params(("arbitrary",)),
    )(pqkv, pba, conv_w, alog_row, dtb_row)


def _delta_pre_bwd(pqkv, pba, conv_w, alog_row, dtb_row, dq, dk, dv, dg, db, *, name, tm=256):
    t, c3 = pqkv.shape
    hk = N_HEADS_A * HEAD_DIM_A
    tm = _tile(t, tm)
    nt = t // tm
    hb = tm // HALO

    def body(x_ref, xp_ref, ba_ref, cw_ref, al_ref, db_ref, dq_ref, dk_ref, dv_ref, dg_ref, dbt_ref,
             dx_ref, dba_ref, dcw_ref, dal_ref, ddb_ref, xext, dyext, carry):
        i = pl.program_id(0)
        first = i == 0
        tile = nt - 1 - i

        @pl.when(tile == 0)
        def _():
            xext[0:HALO, :] = jnp.zeros((HALO, c3), F32)

        @pl.when(tile > 0)
        def _():
            xext[0:HALO, :] = xp_ref[...]

        xext[pl.ds(HALO, tm), :] = x_ref[...]
        y = _conv_from_ext(xext, cw_ref, CONV_A, tm)
        sg = _sigmoid(y)
        s = y * sg
        dsilu = sg * (1.0 + y * (1.0 - sg))
        for h in range(N_HEADS_A):
            lo = h * HEAD_DIM_A
            for src, off in ((dq_ref, 0), (dk_ref, hk)):
                sh = s[:, off + lo:off + lo + HEAD_DIM_A]
                r = lax.rsqrt(jnp.sum(sh * sh, axis=-1, keepdims=True) + EPS)
                qn = sh * r
                gq = src[:, lo:lo + HEAD_DIM_A]
                dsh = r * (gq - qn * jnp.sum(gq * qn, axis=-1, keepdims=True))
                dyext[pl.ds(0, tm), off + lo:off + lo + HEAD_DIM_A] = dsh * dsilu[:, off + lo:off + lo + HEAD_DIM_A]
        dyext[pl.ds(0, tm), 2 * hk:3 * hk] = dv_ref[...] * dsilu[:, 2 * hk:3 * hk]

        @pl.when(first)
        def _():
            dyext[pl.ds(tm, HALO), :] = jnp.zeros((HALO, c3), F32)

        @pl.when(jnp.logical_not(first))
        def _():
            dyext[pl.ds(tm, HALO), :] = carry[...]

        dx = _conv_bwd_from_ext(xext, dyext, cw_ref, dcw_ref, CONV_A, tm, first)
        carry[...] = dyext[0:HALO, :]
        dx_ref[...] = dx.astype(BF16)

        lane = lax.broadcasted_iota(jnp.int32, (tm, LANES), 1)
        gcol = jnp.zeros((tm, LANES), F32)
        for h in range(N_HEADS_A):
            lo = h * HEAD_DIM_A
            dbh = jnp.sum(dbt_ref[:, lo:lo + HEAD_DIM_A], axis=-1, keepdims=True)
            dgh = jnp.sum(dg_ref[:, lo:lo + HEAD_DIM_A], axis=-1, keepdims=True)
            gcol = gcol + jnp.where(lane == h, dbh, 0.0) + jnp.where(lane == N_HEADS_A + h, dgh, 0.0)
        ba = ba_ref[...]
        beta = _sigmoid(ba)
        a_neg = -jnp.exp(al_ref[...])
        z = ba + db_ref[...]
        dz = gcol * a_neg * _sigmoid(z)
        is_g = jnp.logical_and(lane >= N_HEADS_A, lane < 2 * N_HEADS_A)
        dba = jnp.where(lane < N_HEADS_A, gcol * beta * (1.0 - beta), jnp.where(is_g, dz, 0.0))
        dba_ref[...] = dba.astype(BF16)
        dal = jnp.sum(jnp.where(is_g, gcol * a_neg * _softplus(z), 0.0), axis=0, keepdims=True)
        ddb = jnp.sum(jnp.where(is_g, dz, 0.0), axis=0, keepdims=True)

        @pl.when(first)
        def _():
            dal_ref[...] = dal
            ddb_ref[...] = ddb

        @pl.when(jnp.logical_not(first))
        def _():
            dal_ref[...] += dal
            ddb_ref[...] += ddb

    rev = lambda w: pl.BlockSpec((tm, w), lambda i: (nt - 1 - i, 0))
    prev = pl.BlockSpec((HALO, c3), lambda i: (jnp.maximum((nt - 1 - i) * hb - 1, 0), 0))
    fixed = lambda r, w: pl.BlockSpec((r, w), lambda i: (0, 0))
    return pl.pallas_call(
        body,
        name=name,
        grid=(nt,),
        in_specs=[rev(c3), prev, rev(LANES), fixed(CONV_A, c3), fixed(1, LANES), fixed(1, LANES)] + [rev(hk)] * 5,
        out_specs=[rev(c3), rev(LANES), fixed(CONV_A, c3), fixed(1, LANES), fixed(1, LANES)],
        out_shape=[jax.ShapeDtypeStruct((t, c3), BF16), jax.ShapeDtypeStruct((t, LANES), BF16),
                   jax.ShapeDtypeStruct((CONV_A, c3), F32), jax.ShapeDtypeStruct((1, LANES), F32),
                   jax.ShapeDtypeStruct((1, LANES), F32)],
        scratch_shapes=[pltpu.VMEM((HALO + tm, c3), F32), pltpu.VMEM((tm + HALO, c3), F32), pltpu.VMEM((HALO, c3), F32)],
        compiler_params=_params(("arbitrary",)),
    )(pqkv, pqkv, pba, conv_w, alog_row, dtb_row, dq, dk, dv, dg, db)


def _gated_norm_fwd(o, z, w, *, name, tm=512):
    t, d = o.shape
    tm = _tile(t, tm)

    def body(o_ref, z_ref, w_ref, y_ref):
        for h in range(N_HEADS_A):
            sl = slice(h * HEAD_DIM_A, (h + 1) * HEAD_DIM_A)
            oh = o_ref[:, sl]
            zh = z_ref[:, sl]
            r = lax.rsqrt(jnp.mean(oh * oh, axis=-1, keepdims=True) + EPS)
            y_ref[:, sl] = (oh * r * w_ref[...] * (zh * _sigmoid(zh))).astype(BF16)

    row = pl.BlockSpec((tm, d), lambda i: (i, 0))
    return pl.pallas_call(
        body,
        name=name,
        grid=(t // tm,),
        in_specs=[row, row, pl.BlockSpec((1, HEAD_DIM_A), lambda i: (0, 0))],
        out_specs=row,
        out_shape=jax.ShapeDtypeStruct((t, d), BF16),
        compiler_params=_params(("parallel",)),
    )(o, z, w)


def _gated_norm_bwd(o, z, w, dy, *, name, tm=512):
    t, d = o.shape
    tm = _tile(t, tm)

    def body(o_ref, z_ref, w_ref, dy_ref, do_ref, dz_ref, dw_ref):
        i = pl.program_id(0)
        dw = jnp.zeros((1, HEAD_DIM_A), F32)
        for h in range(N_HEADS_A):
            sl = slice(h * HEAD_DIM_A, (h + 1) * HEAD_DIM_A)
            oh = o_ref[:, sl]
            zh = z_ref[:, sl]
            g = dy_ref[:, sl]
            r = lax.rsqrt(jnp.mean(oh * oh, axis=-1, keepdims=True) + EPS)
            nh = oh * r
            sg = _sigmoid(zh)
            dz_ref[:, sl] = (g * nh * w_ref[...] * (sg * (1.0 + zh * (1.0 - sg)))).astype(BF16)
            dt = g * (zh * sg)
            dw = dw + jnp.sum(dt * nh, axis=0, keepdims=True)
            dnh = dt * w_ref[...]
            do_ref[:, sl] = r * (dnh - nh * jnp.mean(dnh * nh, axis=-1, keepdims=True))

        @pl.when(i == 0)
        def _():
            dw_ref[...] = dw

        @pl.when(i > 0)
        def _():
            dw_ref[...] += dw

    row = pl.BlockSpec((tm, d), lambda i: (i, 0))
    vec = pl.BlockSpec((1, HEAD_DIM_A), lambda i: (0, 0))
    return pl.pallas_call(
        body,
        name=name,
        grid=(t // tm,),
        in_specs=[row, row, vec, row],
        out_specs=[row, row, vec],
        out_shape=[jax.ShapeDtypeStruct((t, d), F32), jax.ShapeDtypeStruct((t, d), BF16),
                   jax.ShapeDtypeStruct((1, HEAD_DIM_A), F32)],
        compiler_params=_params(("arbitrary",)),
    )(o, z, w, dy)


_NN = (((1,), (0,)), ((), ()))
_NT = (((1,), (1,)), ((), ()))
_TN = (((0,), (0,)), ((), ()))
_DIMS = {"nn": _NN, "nt": _NT, "tn": _TN}


def _raw_dot(a, b, kind, prec):
    dims = _DIMS[kind]
    if prec == "f32":
        return lax.dot_general(a, b, dims, precision=lax.Precision.HIGHEST, preferred_element_type=F32)
    a_hi, b_hi = a.astype(BF16), b.astype(BF16)
    out = lax.dot_general(a_hi, b_hi, dims, preferred_element_type=F32)
    if prec == "x3":
        a_lo = (a - a_hi.astype(F32)).astype(BF16)
        b_lo = (b - b_hi.astype(F32)).astype(BF16)
        out = out + lax.dot_general(a_hi, b_lo, dims, preferred_element_type=F32)
        out = out + lax.dot_general(a_lo, b_hi, dims, preferred_element_type=F32)
    return out


@functools.partial(jax.custom_vjp, nondiff_argnums=(2, 3))
def _dot(a, b, kind, prec):
    return _raw_dot(a, b, kind, prec)


def _dot_fwd(a, b, kind, prec):
    return _raw_dot(a, b, kind, prec), (a, b)


def _dot_bwd(kind, prec, saved, g):
    a, b = saved
    if kind == "nn":
        return _raw_dot(g, b, "nt", prec), _raw_dot(a, g, "tn", prec)
    if kind == "nt":
        return _raw_dot(g, b, "nn", prec), _raw_dot(g, a, "tn", prec)
    return _raw_dot(b, g, "nt", prec), _raw_dot(a, g, "nn", prec)


_dot.defvjp(_dot_fwd, _dot_bwd)

INV_PREC = "x3"


def _inv_unit_lower_raw(lmat):
    c = lmat.shape[0]
    eye = (lax.broadcasted_iota(jnp.int32, (c, c), 0) == lax.broadcasted_iota(jnp.int32, (c, c), 1)).astype(F32)
    x = eye - lmat
    p = lmat
    for _ in range(int(math.log2(c)) - 1):
        p = _raw_dot(p, p, "nn", INV_PREC)
        x = x + _raw_dot(x, p, "nn", INV_PREC)
    return x


@jax.custom_vjp
def _inv_unit_lower(lmat, hint):
    return _inv_unit_lower_raw(lmat) if hint is None else hint


def _inv_fwd(lmat, hint):
    tm = _inv_unit_lower_raw(lmat) if hint is None else hint
    return tm, (tm, hint)


def _inv_bwd(saved, g):
    tm, hint = saved
    d = -_raw_dot(_raw_dot(tm, g, "tn", INV_PREC), tm, "nt", INV_PREC)
    return d, (None if hint is None else jnp.zeros_like(hint))


_inv_unit_lower.defvjp(_inv_fwd, _inv_bwd)


def _delta_prep(q, k, v, gbc, bbc, hint=None):
    c = q.shape[0]
    ii = lax.broadcasted_iota(jnp.int32, (c, c), 0)
    jj = lax.broadcasted_iota(jnp.int32, (c, c), 1)
    incl = ii >= jj
    strict = ii > jj
    ltri = incl.astype(F32)
    m1 = _dot(ltri, gbc, "nn", "f32")
    gtot = jnp.sum(gbc, axis=0, keepdims=True)
    decay = jnp.exp(jnp.where(incl, m1 - m1.T, NEG_BIG))
    eg = jnp.exp(m1)
    kk = _dot(k, k, "nt", "bf16")
    lmat = jnp.where(strict, bbc * kk * decay, 0.0)
    tinv = _inv_unit_lower(lmat, hint)
    eye = (ii == jj).astype(F32)
    toff = tinv - eye
    bv = bbc * v
    bk = bbc * eg * k
    u0 = bv + _dot(toff, bv, "nn", "bf16")
    wk = bk + _dot(toff, bk, "nn", "bf16")
    qs = q * (HEAD_DIM_A ** -0.5)
    qk = _dot(qs, k, "nt", "bf16") * decay
    q_dec = qs * eg
    k_dec = k * jnp.exp(gtot - m1)
    glast = jnp.broadcast_to(jnp.exp(gtot), (c, c))
    return (u0, wk, qk, q_dec, k_dec, glast), tinv


def _delta_step(s, u0, wk, qk, q_dec, k_dec, glast):
    u = u0 - _dot(wk, s, "nn", "bf16")
    o = _dot(q_dec, s, "nn", "bf16") + _dot(qk, u, "nn", "bf16")
    s_new = glast * s + _dot(k_dec, u, "tn", "bf16")
    return o, s_new


def _chunk_spec(nc, reverse=False):
    if reverse:
        return pl.BlockSpec((CHUNK, HEAD_DIM_A), lambda h, n: (nc - 1 - n, h))
    return pl.BlockSpec((CHUNK, HEAD_DIM_A), lambda h, n: (n, h))


def _delta_prep_fwd(q, k, v, gbc, bbc, *, name):
    t, d = q.shape
    nc = t // CHUNK

    def body(q_ref, k_ref, v_ref, g_ref, b_ref, *outs):
        res, tinv = _delta_prep(q_ref[...], k_ref[...], v_ref[...], g_ref[...], b_ref[...])
        for ref, val in zip(outs, res + (tinv,)):
            ref[...] = val

    spec = _chunk_spec(nc)
    return pl.pallas_call(
        body,
        name=name,
        grid=(N_HEADS_A, nc),
        in_specs=[spec] * 5,
        out_specs=[spec] * 7,
        out_shape=[jax.ShapeDtypeStruct((t, d), F32)] * 7,
        compiler_params=_params(("parallel", "parallel")),
    )(q, k, v, gbc, bbc)


def _delta_prep_bwd(q, k, v, gbc, bbc, tinv, cts, *, name):
    t, d = q.shape
    nc = t // CHUNK

    def body(q_ref, k_ref, v_ref, g_ref, b_ref, t_ref, c0, c1, c2, c3, c4, c5, *outs):
        def f(q_, k_, v_, g_, b_):
            return _delta_prep(q_, k_, v_, g_, b_, hint=t_ref[...])[0]

        _, vjp = jax.vjp(f, q_ref[...], k_ref[...], v_ref[...], g_ref[...], b_ref[...])
        grads = vjp((c0[...], c1[...], c2[...], c3[...], c4[...], c5[...]))
        for ref, val in zip(outs, grads):
            ref[...] = val

    spec = _chunk_spec(nc)
    return pl.pallas_call(
        body,
        name=name,
        grid=(N_HEADS_A, nc),
        in_specs=[spec] * 12,
        out_specs=[spec] * 5,
        out_shape=[jax.ShapeDtypeStruct((t, d), F32)] * 5,
        compiler_params=_params(("parallel", "parallel")),
    )(q, k, v, gbc, bbc, tinv, *cts)


def _delta_scan_fwd(prep, *, name):
    t, d = prep[0].shape
    nc = t // CHUNK

    def body(u0, wk, qk, qd, kd, gl, o_ref, st_ref, s_ref):
        n = pl.program_id(1)

        @pl.when(n == 0)
        def _():
            s_ref[...] = jnp.zeros((HEAD_DIM_A, HEAD_DIM_A), F32)

        s = s_ref[...]
        st_ref[...] = s
        o, s_new = _delta_step(s, u0[...], wk[...], qk[...], qd[...], kd[...], gl[...])
        o_ref[...] = o
        s_ref[...] = s_new

    spec = _chunk_spec(nc)
    return pl.pallas_call(
        body,
        name=name,
        grid=(N_HEADS_A, nc),
        in_specs=[spec] * 6,
        out_specs=[spec] * 2,
        out_shape=[jax.ShapeDtypeStruct((t, d), F32)] * 2,
        scratch_shapes=[pltpu.VMEM((HEAD_DIM_A, HEAD_DIM_A), F32)],
        compiler_params=_params(("parallel", "arbitrary")),
    )(*prep)


def _delta_scan_bwd(prep, states, do, *, name):
    t, d = do.shape
    nc = t // CHUNK

    def body(u0, wk, qk, qd, kd, gl, st_ref, do_ref, *rest):
        outs, ds_ref = rest[:6], rest[6]
        n = pl.program_id(1)

        @pl.when(n == 0)
        def _():
            ds_ref[...] = jnp.zeros((HEAD_DIM_A, HEAD_DIM_A), F32)

        _, vjp = jax.vjp(_delta_step, st_ref[...], u0[...], wk[...], qk[...], qd[...], kd[...], gl[...])
        grads = vjp((do_ref[...], ds_ref[...]))
        ds_ref[...] = grads[0]
        for ref, val in zip(outs, grads[1:]):
            ref[...] = val

    spec = _chunk_spec(nc, reverse=True)
    return pl.pallas_call(
        body,
        name=name,
        grid=(N_HEADS_A, nc),
        in_specs=[spec] * 8,
        out_specs=[spec] * 6,
        out_shape=[jax.ShapeDtypeStruct((t, d), F32)] * 6,
        scratch_shapes=[pltpu.VMEM((HEAD_DIM_A, HEAD_DIM_A), F32)],
        compiler_params=_params(("parallel", "arbitrary")),
    )(*prep, states, do)


def _alibi_slope(h):
    return 2.0 ** (-8.0 * (h + 1) / N_HEADS_B)


def _swa_probs(qh, kcat, slope, sink, blk):
    s = lax.dot_general(qh.astype(BF16), kcat.astype(BF16), _NT, preferred_element_type=F32) * (HEAD_DIM_B ** -0.5)
    qi = lax.broadcasted_iota(jnp.int32, (WINDOW, 2 * WINDOW), 0)
    kj = lax.broadcasted_iota(jnp.int32, (WINDOW, 2 * WINDOW), 1)
    dist = qi + WINDOW - kj
    valid = (dist >= 0) & (dist < WINDOW) & (blk * WINDOW - WINDOW + kj >= 0)
    logits = jnp.where(valid, s - slope * dist.astype(F32), NEG_BIG)
    m = jnp.maximum(jnp.max(logits, axis=-1, keepdims=True), sink)
    e = jnp.exp(logits - m)
    es = jnp.exp(sink - m)
    inv = 1.0 / (jnp.sum(e, axis=-1, keepdims=True) + es)
    return e * inv, es * inv


def _swa_fwd(proj, sinks, *, name):
    t = proj.shape[0]
    nb = t // WINDOW
    qd = N_HEADS_B * HEAD_DIM_B
    kd = N_KV_B * HEAD_DIM_B

    def body(sink_ref, q_ref, kp_ref, kc_ref, vp_ref, vc_ref, o_ref):
        blk = pl.program_id(0)
        for hk in range(N_KV_B):
            ks = slice(hk * HEAD_DIM_B, (hk + 1) * HEAD_DIM_B)
            kcat = jnp.concatenate([kp_ref[:, ks], kc_ref[:, ks]], axis=0)
            vcat = jnp.concatenate([vp_ref[:, ks], vc_ref[:, ks]], axis=0).astype(BF16)
            for g in range(GROUP_B):
                h = hk * GROUP_B + g
                hs = slice(h * HEAD_DIM_B, (h + 1) * HEAD_DIM_B)
                p, _ = _swa_probs(q_ref[:, hs], kcat, _alibi_slope(h), sink_ref[0, h], blk)
                o_ref[:, hs] = jnp.dot(p.astype(BF16), vcat, preferred_element_type=F32).astype(BF16)

    q_spec = pl.BlockSpec((WINDOW, qd), lambda i: (i, 0))
    kv = lambda col, prev: pl.BlockSpec((WINDOW, kd), (lambda i: (jnp.maximum(i - 1, 0), col)) if prev else (lambda i: (i, col)))
    kcol, vcol = qd // kd, qd // kd + 1
    return pl.pallas_call(
        body,
        name=name,
        grid=(nb,),
        in_specs=[pl.BlockSpec(memory_space=pltpu.SMEM), q_spec, kv(kcol, True), kv(kcol, False), kv(vcol, True), kv(vcol, False)],
        out_specs=q_spec,
        out_shape=jax.ShapeDtypeStruct((t, qd), BF16),
        compiler_params=_params(("parallel",)),
    )(sinks, proj, proj, proj, proj, proj)


def _swa_bwd(proj, sinks, dout, *, name):
    t = proj.shape[0]
    nb = t // WINDOW
    qd = N_HEADS_B * HEAD_DIM_B
    kd = N_KV_B * HEAD_DIM_B
    scale = HEAD_DIM_B ** -0.5

    def body(sink_ref, q_ref, kp_ref, kc_ref, vp_ref, vc_ref, do_ref, dq_ref, dk_ref, dv_ref, dsk_ref):
        blk = pl.program_id(0)
        lane = lax.broadcasted_iota(jnp.int32, (1, LANES), 1)

        @pl.when(blk == 0)
        def _():
            dk_ref[...] = jnp.zeros((t, kd), F32)
            dv_ref[...] = jnp.zeros((t, kd), F32)
            dsk_ref[...] = jnp.zeros((1, LANES), F32)

        cur = pl.ds(pl.multiple_of(blk * WINDOW, WINDOW), WINDOW)
        prv = pl.ds(pl.multiple_of(jnp.maximum(blk - 1, 0) * WINDOW, WINDOW), WINDOW)
        dsk = jnp.zeros((1, LANES), F32)
        for hk in range(N_KV_B):
            ks = slice(hk * HEAD_DIM_B, (hk + 1) * HEAD_DIM_B)
            kcat = jnp.concatenate([kp_ref[:, ks], kc_ref[:, ks]], axis=0)
            vcat = jnp.concatenate([vp_ref[:, ks], vc_ref[:, ks]], axis=0).astype(BF16)
            dkc = jnp.zeros((2 * WINDOW, HEAD_DIM_B), F32)
            dvc = jnp.zeros((2 * WINDOW, HEAD_DIM_B), F32)
            for g in range(GROUP_B):
                h = hk * GROUP_B + g
                hs = slice(h * HEAD_DIM_B, (h + 1) * HEAD_DIM_B)
                qh = q_ref[:, hs]
                p, psink = _swa_probs(qh, kcat, _alibi_slope(h), sink_ref[0, h], blk)
                doh = do_ref[:, hs].astype(BF16)
                dp = lax.dot_general(doh, vcat, _NT, preferred_element_type=F32)
                delta = jnp.sum(p * dp, axis=-1, keepdims=True)
                ds = (p * (dp - delta) * scale).astype(BF16)
                dsk = dsk + jnp.where(lane == h, jnp.sum(-psink * delta, axis=0, keepdims=True), 0.0)
                dq_ref[:, hs] = jnp.dot(ds, kcat.astype(BF16), preferred_element_type=F32).astype(BF16)
                dkc = dkc + lax.dot_general(ds, qh.astype(BF16), _TN, preferred_element_type=F32)
                dvc = dvc + lax.dot_general(p.astype(BF16), doh, _TN, preferred_element_type=F32)
            dk_ref[cur, ks] += dkc[WINDOW:, :]
            dv_ref[cur, ks] += dvc[WINDOW:, :]

            @pl.when(blk > 0)
            def _():
                dk_ref[prv, ks] += dkc[:WINDOW, :]
                dv_ref[prv, ks] += dvc[:WINDOW, :]

        dsk_ref[...] += dsk

    q_spec = pl.BlockSpec((WINDOW, qd), lambda i: (i, 0))
    kv = lambda col, prev: pl.BlockSpec((WINDOW, kd), (lambda i: (jnp.maximum(i - 1, 0), col)) if prev else (lambda i: (i, col)))
    kcol, vcol = qd // kd, qd // kd + 1
    full = pl.BlockSpec((t, kd), lambda i: (0, 0))
    return pl.pallas_call(
        body,
        name=name,
        grid=(nb,),
        in_specs=[pl.BlockSpec(memory_space=pltpu.SMEM), q_spec, kv(kcol, True), kv(kcol, False), kv(vcol, True), kv(vcol, False), q_spec],
        out_specs=[q_spec, full, full, pl.BlockSpec((1, LANES), lambda i: (0, 0))],
        out_shape=[jax.ShapeDtypeStruct((t, qd), BF16), jax.ShapeDtypeStruct((t, kd), F32),
                   jax.ShapeDtypeStruct((t, kd), F32), jax.ShapeDtypeStruct((1, LANES), F32)],
        compiler_params=_params(("arbitrary",)),
    )(sinks, proj, proj, proj, proj, proj, dout)


def _ffn_act_fwd(up_g, up_v, cw_g, cw_v, *, name, tm=512, cb=256):
    t, f = up_g.shape
    tm, cb = _tile(t, tm), _tile(f, cb)

    def body(ug_ref, uv_ref, cg_ref, cv_ref, a_ref, eg, ev):
        i = pl.program_id(1)

        @pl.when(i == 0)
        def _():
            eg[0:HALO, :] = jnp.zeros((HALO, cb), F32)
            ev[0:HALO, :] = jnp.zeros((HALO, cb), F32)

        eg[pl.ds(HALO, tm), :] = ug_ref[...]
        ev[pl.ds(HALO, tm), :] = uv_ref[...]
        yg = _conv_from_ext(eg, cg_ref, FFN_CONV, tm)
        yv = _conv_from_ext(ev, cv_ref, FFN_CONV, tm)
        eg[0:HALO, :] = eg[pl.ds(tm, HALO), :]
        ev[0:HALO, :] = ev[pl.ds(tm, HALO), :]
        a_ref[...] = (yg * _sigmoid(yg) * yv).astype(BF16)

    blk = pl.BlockSpec((tm, cb), lambda c, i: (i, c))
    cw = pl.BlockSpec((FFN_CONV, cb), lambda c, i: (0, c))
    return pl.pallas_call(
        body,
        name=name,
        grid=(f // cb, t // tm),
        in_specs=[blk, blk, cw, cw],
        out_specs=blk,
        out_shape=jax.ShapeDtypeStruct((t, f), BF16),
        scratch_shapes=[pltpu.VMEM((HALO + tm, cb), F32)] * 2,
        compiler_params=_params(("parallel", "arbitrary")),
    )(up_g, up_v, cw_g, cw_v)


def _ffn_act_bwd(up_g, up_v, cw_g, cw_v, dact, *, name, tm=512, cb=256):
    t, f = up_g.shape
    tm, cb = _tile(t, tm), _tile(f, cb)
    nt = t // tm
    hb = tm // HALO

    def body(ug_ref, uv_ref, pg_ref, pv_ref, cg_ref, cv_ref, da_ref, dg_ref, dv_ref, dcg_ref, dcv_ref,
             xg, xv, dyg, dyv, carry_g, carry_v):
        i = pl.program_id(1)
        first = i == 0
        tile = nt - 1 - i

        @pl.when(tile == 0)
        def _():
            xg[0:HALO, :] = jnp.zeros((HALO, cb), F32)
            xv[0:HALO, :] = jnp.zeros((HALO, cb), F32)

        @pl.when(tile > 0)
        def _():
            xg[0:HALO, :] = pg_ref[...]
            xv[0:HALO, :] = pv_ref[...]

        xg[pl.ds(HALO, tm), :] = ug_ref[...]
        xv[pl.ds(HALO, tm), :] = uv_ref[...]
        yg = _conv_from_ext(xg, cg_ref, FFN_CONV, tm)
        yv = _conv_from_ext(xv, cv_ref, FFN_CONV, tm)
        sg = _sigmoid(yg)
        da = da_ref[...]
        dyg[pl.ds(0, tm), :] = da * yv * (sg * (1.0 + yg * (1.0 - sg)))
        dyv[pl.ds(0, tm), :] = da * (yg * sg)

        @pl.when(first)
        def _():
            dyg[pl.ds(tm, HALO), :] = jnp.zeros((HALO, cb), F32)
            dyv[pl.ds(tm, HALO), :] = jnp.zeros((HALO, cb), F32)

        @pl.when(jnp.logical_not(first))
        def _():
            dyg[pl.ds(tm, HALO), :] = carry_g[...]
            dyv[pl.ds(tm, HALO), :] = carry_v[...]

        dg_ref[...] = _conv_bwd_from_ext(xg, dyg, cg_ref, dcg_ref, FFN_CONV, tm, first).astype(BF16)
        dv_ref[...] = _conv_bwd_from_ext(xv, dyv, cv_ref, dcv_ref, FFN_CONV, tm, first).astype(BF16)
        carry_g[...] = dyg[0:HALO, :]
        carry_v[...] = dyv[0:HALO, :]

    blk = pl.BlockSpec((tm, cb), lambda c, i: (nt - 1 - i, c))
    prev = pl.BlockSpec((HALO, cb), lambda c, i: (jnp.maximum((nt - 1 - i) * hb - 1, 0), c))
    cw = pl.BlockSpec((FFN_CONV, cb), lambda c, i: (0, c))
    return pl.pallas_call(
        body,
        name=name,
        grid=(f // cb, nt),
        in_specs=[blk, blk, prev, prev, cw, cw, blk],
        out_specs=[blk, blk, cw, cw],
        out_shape=[jax.ShapeDtypeStruct((t, f), BF16), jax.ShapeDtypeStruct((t, f), BF16),
                   jax.ShapeDtypeStruct((FFN_CONV, f), F32), jax.ShapeDtypeStruct((FFN_CONV, f), F32)],
        scratch_shapes=[pltpu.VMEM((HALO + tm, cb), F32)] * 2 + [pltpu.VMEM((tm + HALO, cb), F32)] * 2
        + [pltpu.VMEM((HALO, cb), F32)] * 2,
        compiler_params=_params(("parallel", "arbitrary")),
    )(up_g, up_v, up_g, up_v, cw_g, cw_v, dact)


def _ple_fwd(h, zg, pe, *, name, tm=512):
    t, d = h.shape
    tm = _tile(t, tm)

    def body(h_ref, z_ref, p_ref, o_ref):
        o_ref[...] = h_ref[...] + _sigmoid(z_ref[...]) * p_ref[...]

    row = pl.BlockSpec((tm, d), lambda i: (i, 0))
    return pl.pallas_call(
        body, name=name, grid=(t // tm,), in_specs=[row] * 3, out_specs=row,
        out_shape=jax.ShapeDtypeStruct((t, d), F32), compiler_params=_params(("parallel",)),
    )(h, zg, pe)


def _ple_bwd(dh, zg, pe, *, name, tm=512):
    t, d = dh.shape
    tm = _tile(t, tm)

    def body(g_ref, z_ref, p_ref, dz_ref, dp_ref):
        g = g_ref[...]
        sg = _sigmoid(z_ref[...])
        dz_ref[...] = (g * p_ref[...] * sg * (1.0 - sg)).astype(BF16)
        dp_ref[...] = (g * sg).astype(BF16)

    row = pl.BlockSpec((tm, d), lambda i: (i, 0))
    return pl.pallas_call(
        body, name=name, grid=(t // tm,), in_specs=[row] * 3, out_specs=[row] * 2,
        out_shape=[jax.ShapeDtypeStruct((t, d), BF16)] * 2, compiler_params=_params(("parallel",)),
    )(dh, zg, pe)


def _my_pos():
    return lax.axis_index("x"), lax.axis_index("y"), lax.axis_index("c")


def _all_gather(block, *, name):
    r, w = block.shape

    def body(x_ref, out_ref, send_sems, recv_sems, local_sem):
        x, y, c = _my_pos()
        me, sibling = (x, y, c), (x, y, 1 - c)
        chips = [(1 - x, y), (x, 1 - y), (1 - x, 1 - y)]

        def slot(px, py, pc):
            return out_ref.at[4 * px + 2 * py + pc]

        def copy(k, blk, to, src=None):
            return pltpu.make_async_remote_copy(
                src_ref=slot(*blk) if src is None else src, dst_ref=slot(*blk),
                send_sem=send_sems.at[k], recv_sem=recv_sems.at[k],
                device_id=to, device_id_type=pl.DeviceIdType.MESH)

        mine = pltpu.make_async_copy(x_ref, slot(*me), local_sem)
        mine.start()
        first = [copy(0, me, sibling, src=x_ref)]
        first += [copy(1 + j, me, (*chip, c), src=x_ref) for j, chip in enumerate(chips)]
        for cp in first:
            cp.start()
        passed = [copy(4 + j, (*chip, c), sibling) for j, chip in enumerate(chips)]
        for j, chip in enumerate(chips):
            copy(1 + j, (*chip, c), me).wait_recv()
            passed[j].start()
        copy(0, sibling, me).wait_recv()
        for j, chip in enumerate(chips):
            copy(4 + j, (*chip, 1 - c), me).wait_recv()
        for cp in first + passed:
            cp.wait_send()
        mine.wait()

    return pl.pallas_call(
        body,
        name=name,
        out_shape=jax.ShapeDtypeStruct((N_DEV, r, w), block.dtype),
        in_specs=[pl.BlockSpec(memory_space=pl.ANY)],
        out_specs=pl.BlockSpec(memory_space=pl.ANY),
        scratch_shapes=[pltpu.SemaphoreType.DMA((7,)), pltpu.SemaphoreType.DMA((7,)), pltpu.SemaphoreType.DMA],
    )(block)


def _all_to_all(slabs, *, name):
    n, r, w = slabs.shape

    def body(x_ref, out_ref, send_sems, recv_sems, local_sem):
        x, y, c = _my_pos()
        my_idx = 4 * x + 2 * y + c
        mine = pltpu.make_async_copy(x_ref.at[my_idx], out_ref.at[my_idx], local_sem)
        mine.start()
        copies = []
        for k in range(1, N_DEV):
            fx, fy, fc = (k >> 2) & 1, (k >> 1) & 1, k & 1
            px = (1 - x) if fx else x
            py = (1 - y) if fy else y
            pc = (1 - c) if fc else c
            cp = pltpu.make_async_remote_copy(
                src_ref=x_ref.at[4 * px + 2 * py + pc], dst_ref=out_ref.at[my_idx],
                send_sem=send_sems.at[k - 1], recv_sem=recv_sems.at[k - 1],
                device_id=(px, py, pc), device_id_type=pl.DeviceIdType.MESH)
            cp.start()
            copies.append(cp)
        for cp in copies:
            cp.wait_recv()
        for cp in copies:
            cp.wait_send()
        mine.wait()

    return pl.pallas_call(
        body,
        name=name,
        out_shape=jax.ShapeDtypeStruct((n, r, w), slabs.dtype),
        in_specs=[pl.BlockSpec(memory_space=pl.ANY)],
        out_specs=pl.BlockSpec(memory_space=pl.ANY),
        scratch_shapes=[pltpu.SemaphoreType.DMA((7,)), pltpu.SemaphoreType.DMA((7,)), pltpu.SemaphoreType.DMA],
    )(slabs)


def _adamw_packed(parts, w, m, v, *, name, tr=512):
    n, r, lanes = parts.shape
    tr = tr if r % tr == 0 else r
    c1 = 1.0 / (1.0 - ADAM_B1 ** ADAM_STEP)
    c2 = 1.0 / (1.0 - ADAM_B2 ** ADAM_STEP)

    def body(p_ref, w_ref, m_ref, v_ref, g_ref, d_ref, nm_ref, nv_ref):
        g = p_ref[0].astype(F32)
        for j in range(1, n):
            g = g + p_ref[j].astype(F32)
        nm = ADAM_B1 * m_ref[...] + (1.0 - ADAM_B1) * g
        nv = ADAM_B2 * v_ref[...] + (1.0 - ADAM_B2) * (g * g)
        g_ref[...] = g
        nm_ref[...] = nm
        nv_ref[...] = nv
        d_ref[...] = -ADAM_LR * ((nm * c1) / (jnp.sqrt(nv * c2) + ADAM_EPS) + ADAM_WD * w_ref[...])

    row = pl.BlockSpec((tr, lanes), lambda i: (i, 0))
    out = jax.ShapeDtypeStruct((r, lanes), F32)
    return pl.pallas_call(
        body,
        name=name,
        grid=(r // tr,),
        in_specs=[pl.BlockSpec((n, tr, lanes), lambda i: (0, i, 0)), row, row, row],
        out_specs=[row] * 4,
        out_shape=[out] * 4,
        compiler_params=_params(("parallel",)),
    )(parts, w, m, v)


BIG = ("a_w_in", "a_w_out", "b_w_in", "b_w_out", "f_w_up", "f_w_down", "ple_w_proj", "ple_w_gate")
CONVS = ("a_conv", "f_conv")
SMALL = ("norm_mix", "norm_ffn", "norm_ple", "norm_final", "a_log", "a_dt_bias", "a_norm", "b_sinks")
WEIGHTS = ("norm_mix", "norm_ffn", "norm_ple", "norm_final", "a_w_in", "a_conv", "a_log", "a_dt_bias", "a_norm",
           "a_w_out", "b_w_in", "b_sinks", "b_w_out", "f_w_up", "f_conv", "f_w_down", "ple_w_proj", "ple_w_gate")
SLAB_ROW_MULTIPLE = 512


def _pack(arrs, dtype, row_multiple):
    flat = jnp.concatenate([a.reshape(-1).astype(dtype) for a in arrs])
    rows = -(-flat.shape[0] // LANES)
    rows = -(-rows // row_multiple) * row_multiple
    return jnp.pad(flat, (0, rows * LANES - flat.shape[0])).reshape(rows, LANES)


def _unpack(slab, shapes):
    lead = slab.shape[:-2]
    flat = slab.reshape(lead + (-1,))
    out, off = [], 0
    for s in shapes:
        size = math.prod(s)
        out.append(flat[..., off:off + size].reshape(lead + tuple(s)))
        off += size
    return out


def _cols_full(g):
    g = jnp.moveaxis(g, 0, -2)
    return g.reshape(g.shape[:-2] + (g.shape[-2] * g.shape[-1],))


def _rows_full(g):
    g = jnp.moveaxis(g, 0, -3)
    return g.reshape(g.shape[:-3] + (g.shape[-3] * g.shape[-2], g.shape[-1]))


def _cols_split(wfull):
    n = wfull.shape[-1] // N_DEV
    g = wfull.reshape(wfull.shape[:-1] + (N_DEV, n))
    return jnp.moveaxis(g, -2, 0)


def _rows_split(wfull):
    k = wfull.shape[-2] // N_DEV
    g = wfull.reshape(wfull.shape[:-2] + (N_DEV, k, wfull.shape[-1]))
    return jnp.moveaxis(g, -3, 0)


COL_SHARDED = ("a_w_in", "b_w_in", "f_w_up", "ple_w_proj", "a_conv", "f_conv")


def _full(name, g):
    return _cols_full(g) if name in COL_SHARDED else _rows_full(g)


def _split(name, wfull):
    return _cols_split(wfull) if name in COL_SHARDED else _rows_split(wfull)


def _pad_cols(a, width):
    return jnp.pad(a, ((0, 0), (0, width - a.shape[1])))


def kernel(x, p, norm_mix, norm_ffn, norm_ple, norm_final, a_w_in, a_conv, a_log, a_dt_bias, a_norm, a_w_out, b_w_in, b_sinks, b_w_out, f_w_up, f_conv, f_w_down, ple_w_proj, ple_w_gate, loss_target, m_norm_mix, m_norm_ffn, m_norm_ple, m_norm_final, m_a_w_in, m_a_conv, m_a_log, m_a_dt_bias, m_a_norm, m_a_w_out, m_b_w_in, m_b_sinks, m_b_w_out, m_f_w_up, m_f_conv, m_f_w_down, m_ple_w_proj, m_ple_w_gate, v_norm_mix, v_norm_ffn, v_norm_ple, v_norm_final, v_a_w_in, v_a_conv, v_a_log, v_a_dt_bias, v_a_norm, v_a_w_out, v_b_w_in, v_b_sinks, v_b_w_out, v_f_w_up, v_f_conv, v_f_w_down, v_ple_w_proj, v_ple_w_gate):
    wts = dict(norm_mix=norm_mix, norm_ffn=norm_ffn, norm_ple=norm_ple, norm_final=norm_final, a_w_in=a_w_in,
               a_conv=a_conv, a_log=a_log, a_dt_bias=a_dt_bias, a_norm=a_norm, a_w_out=a_w_out, b_w_in=b_w_in,
               b_sinks=b_sinks, b_w_out=b_w_out, f_w_up=f_w_up, f_conv=f_conv, f_w_down=f_w_down,
               ple_w_proj=ple_w_proj, ple_w_gate=ple_w_gate)
    mom = dict(norm_mix=m_norm_mix, norm_ffn=m_norm_ffn, norm_ple=m_norm_ple, norm_final=m_norm_final,
               a_w_in=m_a_w_in, a_conv=m_a_conv, a_log=m_a_log, a_dt_bias=m_a_dt_bias, a_norm=m_a_norm,
               a_w_out=m_a_w_out, b_w_in=m_b_w_in, b_sinks=m_b_sinks, b_w_out=m_b_w_out, f_w_up=m_f_w_up,
               f_conv=m_f_conv, f_w_down=m_f_w_down, ple_w_proj=m_ple_w_proj, ple_w_gate=m_ple_w_gate)
    var = dict(norm_mix=v_norm_mix, norm_ffn=v_norm_ffn, norm_ple=v_norm_ple, norm_final=v_norm_final,
               a_w_in=v_a_w_in, a_conv=v_a_conv, a_log=v_a_log, a_dt_bias=v_a_dt_bias, a_norm=v_a_norm,
               a_w_out=v_a_w_out, b_w_in=v_b_w_in, b_sinks=v_b_sinks, b_w_out=v_b_w_out, f_w_up=v_f_w_up,
               f_conv=v_f_conv, f_w_down=v_f_w_down, ple_w_proj=v_ple_w_proj, ple_w_gate=v_ple_w_gate)
    hk = N_HEADS_A * HEAD_DIM_A
    qd = N_HEADS_B * HEAD_DIM_B
    kd = N_KV_B * HEAD_DIM_B
    xs = x[0]
    tgt = loss_target[0]

    big_shapes = [wts[n].shape for n in BIG]
    conv_shapes = [wts[n].shape for n in CONVS]
    gbig = _all_gather(_pack([wts[n] for n in BIG], BF16, SLAB_ROW_MULTIPLE), name="gather_matrices")
    gconv = _all_gather(_pack([wts[n] for n in CONVS], F32, 8), name="gather_convs")
    full = {n: _full(n, g) for n, g in zip(BIG, _unpack(gbig, big_shapes))}
    full.update({n: _full(n, g) for n, g in zip(CONVS, _unpack(gconv, conv_shapes))})

    wa_in = full["a_w_in"][0]
    wa_qkv, wa_z, wa_ba = wa_in[:, :3 * hk], wa_in[:, 3 * hk:4 * hk], _pad_cols(wa_in[:, 4 * hk:], LANES)
    wa_out = full["a_w_out"][0]
    wb_in, wb_out = full["b_w_in"][0], full["b_w_out"][0]
    w_up_g = [full["f_w_up"][i][:, :D_FF] for i in range(2)]
    w_up_v = [full["f_w_up"][i][:, D_FF:] for i in range(2)]
    w_down = [full["f_w_down"][i] for i in range(2)]
    w_pp = [full["ple_w_proj"][i] for i in range(2)]
    w_pg = [full["ple_w_gate"][i] for i in range(2)]
    cv_a = full["a_conv"][0]
    cv_g = [full["f_conv"][i][:, :D_FF] for i in range(2)]
    cv_v = [full["f_conv"][i][:, D_FF:] for i in range(2)]
    alog_row = jnp.pad(a_log, ((0, 0), (N_HEADS_A, LANES - 2 * N_HEADS_A)))
    dtb_row = jnp.pad(a_dt_bias, ((0, 0), (N_HEADS_A, LANES - 2 * N_HEADS_A)))
    p_bf = [p[i, 0].astype(BF16) for i in range(2)]

    def ffn_ple_fwd(i, h_a):
        n_f = _rms_fwd(h_a, norm_ffn[i], name=f"l{i}_ffn_norm")
        up_g = _matmul(n_f, w_up_g[i], name=f"l{i}_ffn_up_gate")
        up_v = _matmul(n_f, w_up_v[i], name=f"l{i}_ffn_up_val")
        act = _ffn_act_fwd(up_g, up_v, cv_g[i], cv_v[i], name=f"l{i}_ffn_act")
        h_b = _matmul(act, w_down[i], res=h_a, name=f"l{i}_ffn_down")
        n_p = _rms_fwd(h_b, norm_ple[i], name=f"l{i}_ple_norm")
        zg = _matmul(n_p, w_pg[i], name=f"l{i}_ple_gate")
        pe = _matmul(p_bf[i], w_pp[i], name=f"l{i}_ple_proj")
        h_c = _ple_fwd(h_b, zg, pe, name=f"l{i}_ple_mix")
        return h_c, dict(n_f=n_f, up_g=up_g, up_v=up_v, act=act, h_b=h_b, n_p=n_p, zg=zg, pe=pe)

    n0 = _rms_fwd(xs, norm_mix[0], name="l0_mix_norm")
    pqkv = _matmul(n0, wa_qkv, name="l0_in_qkv")
    z0 = _matmul(n0, wa_z, name="l0_in_z")
    pba = _matmul(n0, wa_ba, name="l0_in_ba")
    q, k, v, gbc, bbc = _delta_pre_fwd(pqkv, pba, cv_a, alog_row, dtb_row, name="l0_delta_pre")
    *prep, tinv = _delta_prep_fwd(q, k, v, gbc, bbc, name="l0_delta_prep")
    o, states = _delta_scan_fwd(prep, name="l0_delta_scan")
    og = _gated_norm_fwd(o, z0, a_norm, name="l0_gated_norm")
    h1 = _matmul(og, wa_out, res=xs, name="l0_mix_out")
    h3, sv0 = ffn_ple_fwd(0, h1)

    n1 = _rms_fwd(h3, norm_mix[1], name="l1_mix_norm")
    pb = _matmul(n1, wb_in, name="l1_in_qkv")
    att = _swa_fwd(pb, b_sinks, name="l1_swa")
    h4 = _matmul(att, wb_out, res=h3, name="l1_mix_out")
    h6, sv1 = ffn_ple_fwd(1, h4)

    loss_row, dh6, d_norm_final = _final_loss(h6, norm_final, tgt, name="final_loss")
    loss = lax.psum(loss_row[0, 0], MESH_AXES)

    gw = {}

    def ffn_ple_bwd(i, dh_c, h_a, sv):
        dzg, dpe = _ple_bwd(dh_c, sv["zg"], sv["pe"], name=f"l{i}_ple_mix_bwd")
        d_pg = _matmul(sv["n_p"], dzg, ta=True, name=f"l{i}_ple_gate_dw")
        d_pp = _matmul(p_bf[i], dpe, ta=True, name=f"l{i}_ple_proj_dw")
        dn_p = _matmul(dzg, w_pg[i], tb=True, name=f"l{i}_ple_gate_dx")
        dh_b, d_np = _rms_bwd(sv["h_b"], norm_ple[i], dn_p, dh_c, name=f"l{i}_ple_norm_bwd")
        dact = _matmul(dh_b, w_down[i], tb=True, name=f"l{i}_ffn_down_dx")
        d_down = _matmul(sv["act"], dh_b, ta=True, name=f"l{i}_ffn_down_dw")
        dup_g, dup_v, d_cg, d_cv = _ffn_act_bwd(sv["up_g"], sv["up_v"], cv_g[i], cv_v[i], dact, name=f"l{i}_ffn_act_bwd")
        d_up_g = _matmul(sv["n_f"], dup_g, ta=True, name=f"l{i}_ffn_up_gate_dw")
        d_up_v = _matmul(sv["n_f"], dup_v, ta=True, name=f"l{i}_ffn_up_val_dw")
        dn_f = _matmul(dup_g, w_up_g[i], tb=True, name=f"l{i}_ffn_up_gate_dx")
        dn_f = _matmul(dup_v, w_up_v[i], tb=True, res=dn_f, name=f"l{i}_ffn_up_val_dx")
        dh_a, d_nf = _rms_bwd(h_a, norm_ffn[i], dn_f, dh_b, name=f"l{i}_ffn_norm_bwd")
        g = dict(ple_w_gate=d_pg, ple_w_proj=d_pp, norm_ple=d_np, f_w_down=d_down,
                 f_w_up=jnp.concatenate([d_up_g, d_up_v], axis=1), f_conv=jnp.concatenate([d_cg, d_cv], axis=1),
                 norm_ffn=d_nf)
        return dh_a, g

    dh4, g1 = ffn_ple_bwd(1, dh6, h4, sv1)
    datt = _matmul(dh4, wb_out, tb=True, out_dtype=BF16, name="l1_mix_out_dx")
    gw["b_w_out"] = _matmul(att, dh4, ta=True, name="l1_mix_out_dw")
    dq_b, dk_b, dv_b, dsinks = _swa_bwd(pb, b_sinks, datt, name="l1_swa_bwd")
    dpb = jnp.concatenate([dq_b, dk_b.astype(BF16), dv_b.astype(BF16)], axis=1)
    gw["b_w_in"] = _matmul(n1, dpb, ta=True, name="l1_in_qkv_dw")
    dn1 = _matmul(dpb, wb_in, tb=True, name="l1_in_qkv_dx")
    dh3, d_nm1 = _rms_bwd(h3, norm_mix[1], dn1, dh4, name="l1_mix_norm_bwd")

    dh1, g0 = ffn_ple_bwd(0, dh3, h1, sv0)
    dog = _matmul(dh1, wa_out, tb=True, name="l0_mix_out_dx")
    gw["a_w_out"] = _matmul(og, dh1, ta=True, name="l0_mix_out_dw")
    do, dz0, d_anorm = _gated_norm_bwd(o, z0, a_norm, dog, name="l0_gated_norm_bwd")
    cts = _delta_scan_bwd(prep, states, do, name="l0_delta_scan_bwd")
    dq, dk, dv, dgbc, dbbc = _delta_prep_bwd(q, k, v, gbc, bbc, tinv, cts, name="l0_delta_prep_bwd")
    dpqkv, dpba, d_aconv, d_alog, d_dtb = _delta_pre_bwd(pqkv, pba, cv_a, alog_row, dtb_row, dq, dk, dv, dgbc, dbbc,
                                                         name="l0_delta_pre_bwd")
    d_wqkv = _matmul(n0, dpqkv, ta=True, name="l0_in_qkv_dw")
    d_wz = _matmul(n0, dz0, ta=True, name="l0_in_z_dw")
    d_wba = _matmul(n0, dpba, ta=True, name="l0_in_ba_dw")
    gw["a_w_in"] = jnp.concatenate([d_wqkv, d_wz, d_wba[:, :2 * N_HEADS_A]], axis=1)
    dn0 = _matmul(dpqkv, wa_qkv, tb=True, name="l0_in_qkv_dx")
    dn0 = _matmul(dz0, wa_z, tb=True, res=dn0, name="l0_in_z_dx")
    dn0 = _matmul(dpba, wa_ba, tb=True, res=dn0, name="l0_in_ba_dx")
    dx, d_nm0 = _rms_bwd(xs, norm_mix[0], dn0, dh1, name="l0_mix_norm_bwd")

    for n in ("f_w_up", "f_w_down", "ple_w_proj", "ple_w_gate", "f_conv"):
        gw[n] = jnp.stack([g0[n], g1[n]])
    for n in ("a_w_in", "a_w_out", "b_w_in", "b_w_out"):
        gw[n] = gw[n][None]
    gw["a_conv"] = d_aconv[None]
    small_g = dict(norm_mix=jnp.concatenate([d_nm0, d_nm1]), norm_ffn=jnp.concatenate([g0["norm_ffn"], g1["norm_ffn"]]),
                   norm_ple=jnp.concatenate([g0["norm_ple"], g1["norm_ple"]]), norm_final=d_norm_final[0],
                   a_log=d_alog[:, N_HEADS_A:2 * N_HEADS_A], a_dt_bias=d_dtb[:, N_HEADS_A:2 * N_HEADS_A],
                   a_norm=d_anorm, b_sinks=dsinks[:, :N_HEADS_B])

    def pack_split(names, dtype, row_multiple):
        per_dev = [[_split(n, gw[n])[j] for n in names] for j in range(N_DEV)]
        return jnp.stack([_pack(arrs, dtype, row_multiple) for arrs in per_dev])

    recv_big = _all_to_all(pack_split(BIG, BF16, SLAB_ROW_MULTIPLE), name="exchange_matrix_grads")
    recv_conv = _all_to_all(pack_split(CONVS, F32, 8), name="exchange_conv_grads")
    recv_small = _all_gather(_pack([small_g[n] for n in SMALL], F32, 8), name="gather_small_grads")

    outs = {}
    for names, recv, dtype, mult, tag in ((BIG, recv_big, F32, SLAB_ROW_MULTIPLE, "matrices"),
                                          (CONVS, recv_conv, F32, 8, "convs"), (SMALL, recv_small, F32, 8, "small")):
        shapes = [wts[n].shape for n in names]
        packed = [_pack([d[n] for n in names], dtype, mult) for d in (wts, mom, var)]
        res = _adamw_packed(recv, *packed, name=f"adamw_{tag}")
        for kind, slab in zip(("grad", "delta", "new_m", "new_v"), res):
            for n, arr in zip(names, _unpack(slab, shapes)):
                outs[(kind, n)] = arr

    result = [loss, dx[None]]
    for kind in ("grad", "delta", "new_m", "new_v"):
        result += [outs[(kind, n)] for n in WEIGHTS]
    return tuple(result)
```

```python
import functools
import math

import jax
import jax.numpy as jnp
from jax import lax
from jax.experimental import pallas as pl
from jax.experimental.pallas import tpu as pltpu

F32 = jnp.float32
BF16 = jnp.bfloat16

D_MODEL = 1024
N_HEADS_A = 8
HEAD_DIM_A = 128
CONV_A = 4
CHUNK = 128
N_HEADS_B = 16
N_KV_B = 4
GROUP_B = N_HEADS_B // N_KV_B
HEAD_DIM_B = 64
WINDOW = 128
D_FF = 2816
FFN_CONV = 3
PLE_DIM = 256
EPS = 1e-6
N_DEV = 8
HALO = 8

ADAM_LR = 0.001
ADAM_B1 = 0.9
ADAM_B2 = 0.999
ADAM_EPS = 1e-08
ADAM_WD = 0.01
ADAM_STEP = 10

LANES = 128
VMEM_LIMIT_BYTES = 56 * 1024 * 1024
NEG_BIG = -1e30

MESH_AXES = ("x", "y", "c")


def _params(sem=None):
    return pltpu.CompilerParams(dimension_semantics=sem, vmem_limit_bytes=VMEM_LIMIT_BYTES)


def _tile(n, target):
    best = None
    for t in range(LANES, min(n, target) + 1, LANES):
        if n % t == 0:
            best = t
    return best or n


def _sigmoid(x):
    return 1.0 / (1.0 + jnp.exp(-x))


def _softplus(x):
    return jnp.maximum(x, 0.0) + jnp.log1p(jnp.exp(-jnp.abs(x)))


def _matmul(a, b, *, name, ta=False, tb=False, res=None, out_dtype=F32, tm=1408, tn=1408, tk=None):
    m = a.shape[1] if ta else a.shape[0]
    k = a.shape[0] if ta else a.shape[1]
    n = b.shape[0] if tb else b.shape[1]
    assert (b.shape[1] if tb else b.shape[0]) == k, (a.shape, b.shape, ta, tb)
    if tk is None:
        tk = 1024 if ta else 2816
    tm, tn, tk = _tile(m, tm), _tile(n, tn), _tile(k, tk)
    nk = k // tk
    dims = (((0 if ta else 1,), (1 if tb else 0,)), ((), ()))
    a_spec = pl.BlockSpec((tk, tm), lambda i, j, kk: (kk, i)) if ta else pl.BlockSpec((tm, tk), lambda i, j, kk: (i, kk))
    b_spec = pl.BlockSpec((tn, tk), lambda i, j, kk: (j, kk)) if tb else pl.BlockSpec((tk, tn), lambda i, j, kk: (kk, j))
    o_spec = pl.BlockSpec((tm, tn), lambda i, j, kk: (i, j))
    has_res = res is not None

    def body(*refs):
        a_ref, b_ref = refs[0], refs[1]
        r_ref = refs[2] if has_res else None
        o_ref = refs[3] if has_res else refs[2]
        part = lax.dot_general(a_ref[...].astype(BF16), b_ref[...].astype(BF16), dims, preferred_element_type=F32)

        def finish(acc):
            if has_res:
                acc = acc + r_ref[...].astype(F32)
            o_ref[...] = acc.astype(out_dtype)

        if nk == 1:
            finish(part)
        else:
            acc_ref = refs[-1]
            kk = pl.program_id(2)

            @pl.when(kk == 0)
            def _():
                acc_ref[...] = part

            @pl.when(kk > 0)
            def _():
                acc_ref[...] += part

            @pl.when(kk == nk - 1)
            def _():
                finish(acc_ref[...])

    in_specs = [a_spec, b_spec] + ([o_spec] if has_res else [])
    args = (a, b) + ((res,) if has_res else ())
    return pl.pallas_call(
        body,
        name=name,
        grid=(m // tm, n // tn, nk),
        in_specs=in_specs,
        out_specs=o_spec,
        out_shape=jax.ShapeDtypeStruct((m, n), out_dtype),
        scratch_shapes=[pltpu.VMEM((tm, tn), F32)] if nk > 1 else [],
        compiler_params=_params(("parallel", "parallel", "arbitrary")),
    )(*args)


def _rms_fwd(h, w, *, name, tm=512):
    t, d = h.shape
    tm = _tile(t, tm)

    def body(h_ref, w_ref, o_ref):
        x = h_ref[...]
        r = lax.rsqrt(jnp.mean(x * x, axis=-1, keepdims=True) + EPS)
        o_ref[...] = (x * r * w_ref[...]).astype(BF16)

    return pl.pallas_call(
        body,
        name=name,
        grid=(t // tm,),
        in_specs=[pl.BlockSpec((tm, d), lambda i: (i, 0)), pl.BlockSpec((1, d), lambda i: (0, 0))],
        out_specs=pl.BlockSpec((tm, d), lambda i: (i, 0)),
        out_shape=jax.ShapeDtypeStruct((t, d), BF16),
        compiler_params=_params(("parallel",)),
    )(h, w.reshape(1, d))


def _rms_bwd(h, w, dn, skip, *, name, tm=512):
    t, d = h.shape
    tm = _tile(t, tm)

    def body(h_ref, w_ref, dn_ref, skip_ref, dh_ref, dw_ref):
        i = pl.program_id(0)
        x = h_ref[...]
        r = lax.rsqrt(jnp.mean(x * x, axis=-1, keepdims=True) + EPS)
        nh = x * r
        g = dn_ref[...].astype(F32)
        gw = g * w_ref[...]
        dh_ref[...] = r * (gw - nh * jnp.mean(gw * nh, axis=-1, keepdims=True)) + skip_ref[...]
        part = jnp.sum(g * nh, axis=0, keepdims=True)

        @pl.when(i == 0)
        def _():
            dw_ref[...] = part

        @pl.when(i > 0)
        def _():
            dw_ref[...] += part

    row = pl.BlockSpec((tm, d), lambda i: (i, 0))
    vec = pl.BlockSpec((1, d), lambda i: (0, 0))
    return pl.pallas_call(
        body,
        name=name,
        grid=(t // tm,),
        in_specs=[row, vec, row, row],
        out_specs=[row, vec],
        out_shape=[jax.ShapeDtypeStruct((t, d), F32), jax.ShapeDtypeStruct((1, d), F32)],
        compiler_params=_params(("arbitrary",)),
    )(h, w.reshape(1, d), dn, skip)


def _final_loss(h, w, target, *, name, tm=512):
    t, d = h.shape
    tm = _tile(t, tm)

    def body(h_ref, w_ref, tg_ref, loss_ref, dh_ref, dw_ref):
        i = pl.program_id(0)
        x = h_ref[...]
        r = lax.rsqrt(jnp.mean(x * x, axis=-1, keepdims=True) + EPS)
        nh = x * r
        err = nh * w_ref[...] - tg_ref[...]
        lpart = (0.5 / d) * jnp.sum(jnp.sum(err * err, axis=-1, keepdims=True), axis=0, keepdims=True)
        g = err * (1.0 / d)
        gw = g * w_ref[...]
        dh_ref[...] = r * (gw - nh * jnp.mean(gw * nh, axis=-1, keepdims=True))
        part = jnp.sum(g * nh, axis=0, keepdims=True)
        lrow = jnp.broadcast_to(lpart, (1, LANES))

        @pl.when(i == 0)
        def _():
            dw_ref[...] = part
            loss_ref[...] = lrow

        @pl.when(i > 0)
        def _():
            dw_ref[...] += part
            loss_ref[...] += lrow

    row = pl.BlockSpec((tm, d), lambda i: (i, 0))
    vec = pl.BlockSpec((1, d), lambda i: (0, 0))
    return pl.pallas_call(
        body,
        name=name,
        grid=(t // tm,),
        in_specs=[row, vec, row],
        out_specs=[pl.BlockSpec((1, LANES), lambda i: (0, 0)), row, vec],
        out_shape=[jax.ShapeDtypeStruct((1, LANES), F32), jax.ShapeDtypeStruct((t, d), F32), jax.ShapeDtypeStruct((1, d), F32)],
        compiler_params=_params(("arbitrary",)),
    )(h, w.reshape(1, d), target)


def _conv_from_ext(ext_ref, cw_ref, kw, tm):
    y = cw_ref[kw - 1:kw, :] * ext_ref[pl.ds(HALO, tm), :]
    for i in range(kw - 1):
        y = y + cw_ref[i:i + 1, :] * ext_ref[pl.ds(HALO - (kw - 1) + i, tm), :]
    return y


def _conv_bwd_from_ext(xext_ref, dyext_ref, cw_ref, dcw_ref, kw, tm, first):
    dy = dyext_ref[pl.ds(0, tm), :]
    dx = cw_ref[kw - 1:kw, :] * dy
    for i in range(kw - 1):
        dx = dx + cw_ref[i:i + 1, :] * dyext_ref[pl.ds(kw - 1 - i, tm), :]
    for i in range(kw):
        part = jnp.sum(dy * xext_ref[pl.ds(HALO - (kw - 1) + i, tm), :], axis=0, keepdims=True)

        @pl.when(first)
        def _():
            dcw_ref[i:i + 1, :] = part

        @pl.when(jnp.logical_not(first))
        def _():
            dcw_ref[i:i + 1, :] += part

    return dx


def _delta_pre_fwd(pqkv, pba, conv_w, alog_row, dtb_row, *, name, tm=256):
    t, c3 = pqkv.shape
    hk = N_HEADS_A * HEAD_DIM_A
    tm = _tile(t, tm)

    def body(x_ref, ba_ref, cw_ref, al_ref, db_ref, q_ref, k_ref, v_ref, g_ref, b_ref, ext):
        i = pl.program_id(0)

        @pl.when(i == 0)
        def _():
            ext[0:HALO, :] = jnp.zeros((HALO, c3), F32)

        ext[pl.ds(HALO, tm), :] = x_ref[...]
        y = _conv_from_ext(ext, cw_ref, CONV_A, tm)
        ext[0:HALO, :] = ext[pl.ds(tm, HALO), :]
        s = y * _sigmoid(y)
        for h in range(N_HEADS_A):
            lo = h * HEAD_DIM_A
            for dst, off in ((q_ref, 0), (k_ref, hk)):
                sh = s[:, off + lo:off + lo + HEAD_DIM_A]
                dst[:, lo:lo + HEAD_DIM_A] = sh * lax.rsqrt(jnp.sum(sh * sh, axis=-1, keepdims=True) + EPS)
        v_ref[...] = s[:, 2 * hk:3 * hk]
        ba = ba_ref[...]
        beta = _sigmoid(ba)
        gfull = -jnp.exp(al_ref[...]) * _softplus(ba + db_ref[...])
        for h in range(N_HEADS_A):
            lo = h * HEAD_DIM_A
            b_ref[:, lo:lo + HEAD_DIM_A] = jnp.broadcast_to(beta[:, h:h + 1], (tm, HEAD_DIM_A))
            g_ref[:, lo:lo + HEAD_DIM_A] = jnp.broadcast_to(gfull[:, N_HEADS_A + h:N_HEADS_A + h + 1], (tm, HEAD_DIM_A))

    row = lambda w: pl.BlockSpec((tm, w), lambda i: (i, 0))
    fixed = lambda r, w: pl.BlockSpec((r, w), lambda i: (0, 0))
    out = jax.ShapeDtypeStruct((t, hk), F32)
    return pl.pallas_call(
        body,
        name=name,
        grid=(t // tm,),
        in_specs=[row(c3), row(LANES), fixed(CONV_A, c3), fixed(1, LANES), fixed(1, LANES)],
        out_specs=[row(hk)] * 5,
        out_shape=[out] * 5,
        scratch_shapes=[pltpu.VMEM((HALO + tm, c3), F32)],
        compiler_params=_params(("arbitrary",)),
    )(pqkv, pba, conv_w, alog_row, dtb_row)


def _delta_pre_bwd(pqkv, pba, conv_w, alog_row, dtb_row, dq, dk, dv, dg, db, *, name, tm=256):
    t, c3 = pqkv.shape
    hk = N_HEADS_A * HEAD_DIM_A
    tm = _tile(t, tm)
    nt = t // tm
    hb = tm // HALO

    def body(x_ref, xp_ref, ba_ref, cw_ref, al_ref, db_ref, dq_ref, dk_ref, dv_ref, dg_ref, dbt_ref,
             dx_ref, dba_ref, dcw_ref, dal_ref, ddb_ref, xext, dyext, carry):
        i = pl.program_id(0)
        first = i == 0
        tile = nt - 1 - i

        @pl.when(tile == 0)
        def _():
            xext[0:HALO, :] = jnp.zeros((HALO, c3), F32)

        @pl.when(tile > 0)
        def _():
            xext[0:HALO, :] = xp_ref[...]

        xext[pl.ds(HALO, tm), :] = x_ref[...]
        y = _conv_from_ext(xext, cw_ref, CONV_A, tm)
        sg = _sigmoid(y)
        s = y * sg
        dsilu = sg * (1.0 + y * (1.0 - sg))
        for h in range(N_HEADS_A):
            lo = h * HEAD_DIM_A
            for src, off in ((dq_ref, 0), (dk_ref, hk)):
                sh = s[:, off + lo:off + lo + HEAD_DIM_A]
                r = lax.rsqrt(jnp.sum(sh * sh, axis=-1, keepdims=True) + EPS)
                qn = sh * r
                gq = src[:, lo:lo + HEAD_DIM_A]
                dsh = r * (gq - qn * jnp.sum(gq * qn, axis=-1, keepdims=True))
                dyext[pl.ds(0, tm), off + lo:off + lo + HEAD_DIM_A] = dsh * dsilu[:, off + lo:off + lo + HEAD_DIM_A]
        dyext[pl.ds(0, tm), 2 * hk:3 * hk] = dv_ref[...] * dsilu[:, 2 * hk:3 * hk]

        @pl.when(first)
        def _():
            dyext[pl.ds(tm, HALO), :] = jnp.zeros((HALO, c3), F32)

        @pl.when(jnp.logical_not(first))
        def _():
            dyext[pl.ds(tm, HALO), :] = carry[...]

        dx = _conv_bwd_from_ext(xext, dyext, cw_ref, dcw_ref, CONV_A, tm, first)
        carry[...] = dyext[0:HALO, :]
        dx_ref[...] = dx.astype(BF16)

        lane = lax.broadcasted_iota(jnp.int32, (tm, LANES), 1)
        gcol = jnp.zeros((tm, LANES), F32)
        for h in range(N_HEADS_A):
            lo = h * HEAD_DIM_A
            dbh = jnp.sum(dbt_ref[:, lo:lo + HEAD_DIM_A], axis=-1, keepdims=True)
            dgh = jnp.sum(dg_ref[:, lo:lo + HEAD_DIM_A], axis=-1, keepdims=True)
            gcol = gcol + jnp.where(lane == h, dbh, 0.0) + jnp.where(lane == N_HEADS_A + h, dgh, 0.0)
        ba = ba_ref[...]
        beta = _sigmoid(ba)
        a_neg = -jnp.exp(al_ref[...])
        z = ba + db_ref[...]
        dz = gcol * a_neg * _sigmoid(z)
        is_g = jnp.logical_and(lane >= N_HEADS_A, lane < 2 * N_HEADS_A)
        dba = jnp.where(lane < N_HEADS_A, gcol * beta * (1.0 - beta), jnp.where(is_g, dz, 0.0))
        dba_ref[...] = dba.astype(BF16)
        dal = jnp.sum(jnp.where(is_g, gcol * a_neg * _softplus(z), 0.0), axis=0, keepdims=True)
        ddb = jnp.sum(jnp.where(is_g, dz, 0.0), axis=0, keepdims=True)

        @pl.when(first)
        def _():
            dal_ref[...] = dal
            ddb_ref[...] = ddb

        @pl.when(jnp.logical_not(first))
        def _():
            dal_ref[...] += dal
            ddb_ref[...] += ddb

    rev = lambda w: pl.BlockSpec((tm, w), lambda i: (nt - 1 - i, 0))
    prev = pl.BlockSpec((HALO, c3), lambda i: (jnp.maximum((nt - 1 - i) * hb - 1, 0), 0))
    fixed = lambda r, w: pl.BlockSpec((r, w), lambda i: (0, 0))
    return pl.pallas_call(
        body,
        name=name,
        grid=(nt,),
        in_specs=[rev(c3), prev, rev(LANES), fixed(CONV_A, c3), fixed(1, LANES), fixed(1, LANES)] + [rev(hk)] * 5,
        out_specs=[rev(c3), rev(LANES), fixed(CONV_A, c3), fixed(1, LANES), fixed(1, LANES)],
        out_shape=[jax.ShapeDtypeStruct((t, c3), BF16), jax.ShapeDtypeStruct((t, LANES), BF16),
                   jax.ShapeDtypeStruct((CONV_A, c3), F32), jax.ShapeDtypeStruct((1, LANES), F32),
                   jax.ShapeDtypeStruct((1, LANES), F32)],
        scratch_shapes=[pltpu.VMEM((HALO + tm, c3), F32), pltpu.VMEM((tm + HALO, c3), F32), pltpu.VMEM((HALO, c3), F32)],
        compiler_params=_params(("arbitrary",)),
    )(pqkv, pqkv, pba, conv_w, alog_row, dtb_row, dq, dk, dv, dg, db)


def _gated_norm_fwd(o, z, w, *, name, tm=512):
    t, d = o.shape
    tm = _tile(t, tm)

    def body(o_ref, z_ref, w_ref, y_ref):
        for h in range(N_HEADS_A):
            sl = slice(h * HEAD_DIM_A, (h + 1) * HEAD_DIM_A)
            oh = o_ref[:, sl]
            zh = z_ref[:, sl]
            r = lax.rsqrt(jnp.mean(oh * oh, axis=-1, keepdims=True) + EPS)
            y_ref[:, sl] = (oh * r * w_ref[...] * (zh * _sigmoid(zh))).astype(BF16)

    row = pl.BlockSpec((tm, d), lambda i: (i, 0))
    return pl.pallas_call(
        body,
        name=name,
        grid=(t // tm,),
        in_specs=[row, row, pl.BlockSpec((1, HEAD_DIM_A), lambda i: (0, 0))],
        out_specs=row,
        out_shape=jax.ShapeDtypeStruct((t, d), BF16),
        compiler_params=_params(("parallel",)),
    )(o, z, w)


def _gated_norm_bwd(o, z, w, dy, *, name, tm=512):
    t, d = o.shape
    tm = _tile(t, tm)

    def body(o_ref, z_ref, w_ref, dy_ref, do_ref, dz_ref, dw_ref):
        i = pl.program_id(0)
        dw = jnp.zeros((1, HEAD_DIM_A), F32)
        for h in range(N_HEADS_A):
            sl = slice(h * HEAD_DIM_A, (h + 1) * HEAD_DIM_A)
            oh = o_ref[:, sl]
            zh = z_ref[:, sl]
            g = dy_ref[:, sl]
            r = lax.rsqrt(jnp.mean(oh * oh, axis=-1, keepdims=True) + EPS)
            nh = oh * r
            sg = _sigmoid(zh)
            dz_ref[:, sl] = (g * nh * w_ref[...] * (sg * (1.0 + zh * (1.0 - sg)))).astype(BF16)
            dt = g * (zh * sg)
            dw = dw + jnp.sum(dt * nh, axis=0, keepdims=True)
            dnh = dt * w_ref[...]
            do_ref[:, sl] = r * (dnh - nh * jnp.mean(dnh * nh, axis=-1, keepdims=True))

        @pl.when(i == 0)
        def _():
            dw_ref[...] = dw

        @pl.when(i > 0)
        def _():
            dw_ref[...] += dw

    row = pl.BlockSpec((tm, d), lambda i: (i, 0))
    vec = pl.BlockSpec((1, HEAD_DIM_A), lambda i: (0, 0))
    return pl.pallas_call(
        body,
        name=name,
        grid=(t // tm,),
        in_specs=[row, row, vec, row],
        out_specs=[row, row, vec],
        out_shape=[jax.ShapeDtypeStruct((t, d), F32), jax.ShapeDtypeStruct((t, d), BF16),
                   jax.ShapeDtypeStruct((1, HEAD_DIM_A), F32)],
        compiler_params=_params(("arbitrary",)),
    )(o, z, w, dy)


_NN = (((1,), (0,)), ((), ()))
_NT = (((1,), (1,)), ((), ()))
_TN = (((0,), (0,)), ((), ()))
_DIMS = {"nn": _NN, "nt": _NT, "tn": _TN}


def _raw_dot(a, b, kind, prec):
    dims = _DIMS[kind]
    if prec == "f32":
        return lax.dot_general(a, b, dims, precision=lax.Precision.HIGHEST, preferred_element_type=F32)
    a_hi, b_hi = a.astype(BF16), b.astype(BF16)
    out = lax.dot_general(a_hi, b_hi, dims, preferred_element_type=F32)
    if prec == "x3":
        a_lo = (a - a_hi.astype(F32)).astype(BF16)
        b_lo = (b - b_hi.astype(F32)).astype(BF16)
        out = out + lax.dot_general(a_hi, b_lo, dims, preferred_element_type=F32)
        out = out + lax.dot_general(a_lo, b_hi, dims, preferred_element_type=F32)
    return out


@functools.partial(jax.custom_vjp, nondiff_argnums=(2, 3))
def _dot(a, b, kind, prec):
    return _raw_dot(a, b, kind, prec)


def _dot_fwd(a, b, kind, prec):
    return _raw_dot(a, b, kind, prec), (a, b)


def _dot_bwd(kind, prec, saved, g):
    a, b = saved
    if kind == "nn":
        return _raw_dot(g, b, "nt", prec), _raw_dot(a, g, "tn", prec)
    if kind == "nt":
        return _raw_dot(g, b, "nn", prec), _raw_dot(g, a, "tn", prec)
    return _raw_dot(b, g, "nt", prec), _raw_dot(a, g, "nn", prec)


_dot.defvjp(_dot_fwd, _dot_bwd)

INV_PREC = "x3"


def _inv_unit_lower_raw(lmat):
    c = lmat.shape[0]
    eye = (lax.broadcasted_iota(jnp.int32, (c, c), 0) == lax.broadcasted_iota(jnp.int32, (c, c), 1)).astype(F32)
    x = eye - lmat
    p = lmat
    for _ in range(int(math.log2(c)) - 1):
        p = _raw_dot(p, p, "nn", INV_PREC)
        x = x + _raw_dot(x, p, "nn", INV_PREC)
    return x


@jax.custom_vjp
def _inv_unit_lower(lmat, hint):
    return _inv_unit_lower_raw(lmat) if hint is None else hint


def _inv_fwd(lmat, hint):
    tm = _inv_unit_lower_raw(lmat) if hint is None else hint
    return tm, (tm, hint)


def _inv_bwd(saved, g):
    tm, hint = saved
    d = -_raw_dot(_raw_dot(tm, g, "tn", INV_PREC), tm, "nt", INV_PREC)
    return d, (None if hint is None else jnp.zeros_like(hint))


_inv_unit_lower.defvjp(_inv_fwd, _inv_bwd)


def _delta_prep(q, k, v, gbc, bbc, hint=None):
    c = q.shape[0]
    ii = lax.broadcasted_iota(jnp.int32, (c, c), 0)
    jj = lax.broadcasted_iota(jnp.int32, (c, c), 1)
    incl = ii >= jj
    strict = ii > jj
    ltri = incl.astype(F32)
    m1 = _dot(ltri, gbc, "nn", "f32")
    gtot = jnp.sum(gbc, axis=0, keepdims=True)
    decay = jnp.exp(jnp.where(incl, m1 - m1.T, NEG_BIG))
    eg = jnp.exp(m1)
    kk = _dot(k, k, "nt", "bf16")
    lmat = jnp.where(strict, bbc * kk * decay, 0.0)
    tinv = _inv_unit_lower(lmat, hint)
    eye = (ii == jj).astype(F32)
    toff = tinv - eye
    bv = bbc * v
    bk = bbc * eg * k
    u0 = bv + _dot(toff, bv, "nn", "bf16")
    wk = bk + _dot(toff, bk, "nn", "bf16")
    qs = q * (HEAD_DIM_A ** -0.5)
    qk = _dot(qs, k, "nt", "bf16") * decay
    q_dec = qs * eg
    k_dec = k * jnp.exp(gtot - m1)
    glast = jnp.broadcast_to(jnp.exp(gtot), (c, c))
    return (u0, wk, qk, q_dec, k_dec, glast), tinv


def _delta_step(s, u0, wk, qk, q_dec, k_dec, glast):
    u = u0 - _dot(wk, s, "nn", "bf16")
    o = _dot(q_dec, s, "nn", "bf16") + _dot(qk, u, "nn", "bf16")
    s_new = glast * s + _dot(k_dec, u, "tn", "bf16")
    return o, s_new


HEADS_PER_STEP = 4


def _chunk_spec(nc, reverse=False):
    w = HEADS_PER_STEP * HEAD_DIM_A
    if reverse:
        return pl.BlockSpec((CHUNK, w), lambda h, n: (nc - 1 - n, h))
    return pl.BlockSpec((CHUNK, w), lambda h, n: (n, h))


def _head_slices():
    return [slice(j * HEAD_DIM_A, (j + 1) * HEAD_DIM_A) for j in range(HEADS_PER_STEP)]


def _delta_prep_fwd(q, k, v, gbc, bbc, *, name):
    t, d = q.shape
    nc = t // CHUNK

    def body(q_ref, k_ref, v_ref, g_ref, b_ref, *outs):
        for sl in _head_slices():
            res, tinv = _delta_prep(q_ref[:, sl], k_ref[:, sl], v_ref[:, sl], g_ref[:, sl], b_ref[:, sl])
            for ref, val in zip(outs, res + (tinv,)):
                ref[:, sl] = val

    spec = _chunk_spec(nc)
    return pl.pallas_call(
        body,
        name=name,
        grid=(N_HEADS_A // HEADS_PER_STEP, nc),
        in_specs=[spec] * 5,
        out_specs=[spec] * 7,
        out_shape=[jax.ShapeDtypeStruct((t, d), F32)] * 7,
        compiler_params=_params(("parallel", "parallel")),
    )(q, k, v, gbc, bbc)


def _delta_prep_bwd(q, k, v, gbc, bbc, tinv, cts, *, name):
    t, d = q.shape
    nc = t // CHUNK

    def body(q_ref, k_ref, v_ref, g_ref, b_ref, t_ref, c0, c1, c2, c3, c4, c5, *outs):
        for sl in _head_slices():
            def f(q_, k_, v_, g_, b_, sl=sl):
                return _delta_prep(q_, k_, v_, g_, b_, hint=t_ref[:, sl])[0]

            _, vjp = jax.vjp(f, q_ref[:, sl], k_ref[:, sl], v_ref[:, sl], g_ref[:, sl], b_ref[:, sl])
            grads = vjp((c0[:, sl], c1[:, sl], c2[:, sl], c3[:, sl], c4[:, sl], c5[:, sl]))
            for ref, val in zip(outs, grads):
                ref[:, sl] = val

    spec = _chunk_spec(nc)
    return pl.pallas_call(
        body,
        name=name,
        grid=(N_HEADS_A // HEADS_PER_STEP, nc),
        in_specs=[spec] * 12,
        out_specs=[spec] * 5,
        out_shape=[jax.ShapeDtypeStruct((t, d), F32)] * 5,
        compiler_params=_params(("parallel", "parallel")),
    )(q, k, v, gbc, bbc, tinv, *cts)


def _delta_scan_fwd(prep, *, name):
    t, d = prep[0].shape
    nc = t // CHUNK

    def body(u0, wk, qk, qd, kd, gl, o_ref, st_ref, s_ref):
        n = pl.program_id(1)

        @pl.when(n == 0)
        def _():
            s_ref[...] = jnp.zeros(s_ref.shape, F32)

        for j, sl in enumerate(_head_slices()):
            s = s_ref[j]
            st_ref[:, sl] = s
            o, s_new = _delta_step(s, u0[:, sl], wk[:, sl], qk[:, sl], qd[:, sl], kd[:, sl], gl[:, sl])
            o_ref[:, sl] = o
            s_ref[j] = s_new

    spec = _chunk_spec(nc)
    return pl.pallas_call(
        body,
        name=name,
        grid=(N_HEADS_A // HEADS_PER_STEP, nc),
        in_specs=[spec] * 6,
        out_specs=[spec] * 2,
        out_shape=[jax.ShapeDtypeStruct((t, d), F32)] * 2,
        scratch_shapes=[pltpu.VMEM((HEADS_PER_STEP, HEAD_DIM_A, HEAD_DIM_A), F32)],
        compiler_params=_params(("parallel", "arbitrary")),
    )(*prep)


def _delta_scan_bwd(prep, states, do, *, name):
    t, d = do.shape
    nc = t // CHUNK

    def body(u0, wk, qk, qd, kd, gl, st_ref, do_ref, *rest):
        outs, ds_ref = rest[:6], rest[6]
        n = pl.program_id(1)

        @pl.when(n == 0)
        def _():
            ds_ref[...] = jnp.zeros(ds_ref.shape, F32)

        for j, sl in enumerate(_head_slices()):
            _, vjp = jax.vjp(_delta_step, st_ref[:, sl], u0[:, sl], wk[:, sl], qk[:, sl], qd[:, sl], kd[:, sl], gl[:, sl])
            grads = vjp((do_ref[:, sl], ds_ref[j]))
            ds_ref[j] = grads[0]
            for ref, val in zip(outs, grads[1:]):
                ref[:, sl] = val

    spec = _chunk_spec(nc, reverse=True)
    return pl.pallas_call(
        body,
        name=name,
        grid=(N_HEADS_A // HEADS_PER_STEP, nc),
        in_specs=[spec] * 8,
        out_specs=[spec] * 6,
        out_shape=[jax.ShapeDtypeStruct((t, d), F32)] * 6,
        scratch_shapes=[pltpu.VMEM((HEADS_PER_STEP, HEAD_DIM_A, HEAD_DIM_A), F32)],
        compiler_params=_params(("parallel", "arbitrary")),
    )(*prep, states, do)


def _alibi_slope(h):
    return 2.0 ** (-8.0 * (h + 1) / N_HEADS_B)


def _swa_probs(qh, kcat, slope, sink, blk):
    s = lax.dot_general(qh.astype(BF16), kcat.astype(BF16), _NT, preferred_element_type=F32) * (HEAD_DIM_B ** -0.5)
    qi = lax.broadcasted_iota(jnp.int32, (WINDOW, 2 * WINDOW), 0)
    kj = lax.broadcasted_iota(jnp.int32, (WINDOW, 2 * WINDOW), 1)
    dist = qi + WINDOW - kj
    valid = (dist >= 0) & (dist < WINDOW) & (blk * WINDOW - WINDOW + kj >= 0)
    logits = jnp.where(valid, s - slope * dist.astype(F32), NEG_BIG)
    m = jnp.maximum(jnp.max(logits, axis=-1, keepdims=True), sink)
    e = jnp.exp(logits - m)
    es = jnp.exp(sink - m)
    inv = 1.0 / (jnp.sum(e, axis=-1, keepdims=True) + es)
    return e * inv, es * inv


def _swa_fwd(proj, sinks, *, name):
    t = proj.shape[0]
    nb = t // WINDOW
    qd = N_HEADS_B * HEAD_DIM_B
    kd = N_KV_B * HEAD_DIM_B

    def body(sink_ref, q_ref, kp_ref, kc_ref, vp_ref, vc_ref, o_ref):
        blk = pl.program_id(0)
        for hk in range(N_KV_B):
            ks = slice(hk * HEAD_DIM_B, (hk + 1) * HEAD_DIM_B)
            kcat = jnp.concatenate([kp_ref[:, ks], kc_ref[:, ks]], axis=0)
            vcat = jnp.concatenate([vp_ref[:, ks], vc_ref[:, ks]], axis=0).astype(BF16)
            for g in range(GROUP_B):
                h = hk * GROUP_B + g
                hs = slice(h * HEAD_DIM_B, (h + 1) * HEAD_DIM_B)
                p, _ = _swa_probs(q_ref[:, hs], kcat, _alibi_slope(h), sink_ref[0, h], blk)
                o_ref[:, hs] = jnp.dot(p.astype(BF16), vcat, preferred_element_type=F32).astype(BF16)

    q_spec = pl.BlockSpec((WINDOW, qd), lambda i: (i, 0))
    kv = lambda col, prev: pl.BlockSpec((WINDOW, kd), (lambda i: (jnp.maximum(i - 1, 0), col)) if prev else (lambda i: (i, col)))
    kcol, vcol = qd // kd, qd // kd + 1
    return pl.pallas_call(
        body,
        name=name,
        grid=(nb,),
        in_specs=[pl.BlockSpec(memory_space=pltpu.SMEM), q_spec, kv(kcol, True), kv(kcol, False), kv(vcol, True), kv(vcol, False)],
        out_specs=q_spec,
        out_shape=jax.ShapeDtypeStruct((t, qd), BF16),
        compiler_params=_params(("parallel",)),
    )(sinks, proj, proj, proj, proj, proj)


def _swa_bwd(proj, sinks, dout, *, name):
    t = proj.shape[0]
    nb = t // WINDOW
    qd = N_HEADS_B * HEAD_DIM_B
    kd = N_KV_B * HEAD_DIM_B
    scale = HEAD_DIM_B ** -0.5

    def body(sink_ref, q_ref, kp_ref, kc_ref, vp_ref, vc_ref, do_ref, dq_ref, dk_ref, dv_ref, dsk_ref):
        blk = pl.program_id(0)
        lane = lax.broadcasted_iota(jnp.int32, (1, LANES), 1)

        @pl.when(blk == 0)
        def _():
            dk_ref[...] = jnp.zeros((t, kd), F32)
            dv_ref[...] = jnp.zeros((t, kd), F32)
            dsk_ref[...] = jnp.zeros((1, LANES), F32)

        cur = pl.ds(pl.multiple_of(blk * WINDOW, WINDOW), WINDOW)
        prv = pl.ds(pl.multiple_of(jnp.maximum(blk - 1, 0) * WINDOW, WINDOW), WINDOW)
        dsk = jnp.zeros((1, LANES), F32)
        for hk in range(N_KV_B):
            ks = slice(hk * HEAD_DIM_B, (hk + 1) * HEAD_DIM_B)
            kcat = jnp.concatenate([kp_ref[:, ks], kc_ref[:, ks]], axis=0)
            vcat = jnp.concatenate([vp_ref[:, ks], vc_ref[:, ks]], axis=0).astype(BF16)
            dkc = jnp.zeros((2 * WINDOW, HEAD_DIM_B), F32)
            dvc = jnp.zeros((2 * WINDOW, HEAD_DIM_B), F32)
            for g in range(GROUP_B):
                h = hk * GROUP_B + g
                hs = slice(h * HEAD_DIM_B, (h + 1) * HEAD_DIM_B)
                qh = q_ref[:, hs]
                p, psink = _swa_probs(qh, kcat, _alibi_slope(h), sink_ref[0, h], blk)
                doh = do_ref[:, hs].astype(BF16)
                dp = lax.dot_general(doh, vcat, _NT, preferred_element_type=F32)
                delta = jnp.sum(p * dp, axis=-1, keepdims=True)
                ds = (p * (dp - delta) * scale).astype(BF16)
                dsk = dsk + jnp.where(lane == h, jnp.sum(-psink * delta, axis=0, keepdims=True), 0.0)
                dq_ref[:, hs] = jnp.dot(ds, kcat.astype(BF16), preferred_element_type=F32).astype(BF16)
                dkc = dkc + lax.dot_general(ds, qh.astype(BF16), _TN, preferred_element_type=F32)
                dvc = dvc + lax.dot_general(p.astype(BF16), doh, _TN, preferred_element_type=F32)
            dk_ref[cur, ks] += dkc[WINDOW:, :]
            dv_ref[cur, ks] += dvc[WINDOW:, :]

            @pl.when(blk > 0)
            def _():
                dk_ref[prv, ks] += dkc[:WINDOW, :]
                dv_ref[prv, ks] += dvc[:WINDOW, :]

        dsk_ref[...] += dsk

    q_spec = pl.BlockSpec((WINDOW, qd), lambda i: (i, 0))
    kv = lambda col, prev: pl.BlockSpec((WINDOW, kd), (lambda i: (jnp.maximum(i - 1, 0), col)) if prev else (lambda i: (i, col)))
    kcol, vcol = qd // kd, qd // kd + 1
    full = pl.BlockSpec((t, kd), lambda i: (0, 0))
    return pl.pallas_call(
        body,
        name=name,
        grid=(nb,),
        in_specs=[pl.BlockSpec(memory_space=pltpu.SMEM), q_spec, kv(kcol, True), kv(kcol, False), kv(vcol, True), kv(vcol, False), q_spec],
        out_specs=[q_spec, full, full, pl.BlockSpec((1, LANES), lambda i: (0, 0))],
        out_shape=[jax.ShapeDtypeStruct((t, qd), BF16), jax.ShapeDtypeStruct((t, kd), F32),
                   jax.ShapeDtypeStruct((t, kd), F32), jax.ShapeDtypeStruct((1, LANES), F32)],
        compiler_params=_params(("arbitrary",)),
    )(sinks, proj, proj, proj, proj, proj, dout)


def _ffn_act_fwd(up_g, up_v, cw_g, cw_v, *, name, tm=512, cb=256):
    t, f = up_g.shape
    tm, cb = _tile(t, tm), _tile(f, cb)

    def body(ug_ref, uv_ref, cg_ref, cv_ref, a_ref, eg, ev):
        i = pl.program_id(1)

        @pl.when(i == 0)
        def _():
            eg[0:HALO, :] = jnp.zeros((HALO, cb), F32)
            ev[0:HALO, :] = jnp.zeros((HALO, cb), F32)

        eg[pl.ds(HALO, tm), :] = ug_ref[...]
        ev[pl.ds(HALO, tm), :] = uv_ref[...]
        yg = _conv_from_ext(eg, cg_ref, FFN_CONV, tm)
        yv = _conv_from_ext(ev, cv_ref, FFN_CONV, tm)
        eg[0:HALO, :] = eg[pl.ds(tm, HALO), :]
        ev[0:HALO, :] = ev[pl.ds(tm, HALO), :]
        a_ref[...] = (yg * _sigmoid(yg) * yv).astype(BF16)

    blk = pl.BlockSpec((tm, cb), lambda c, i: (i, c))
    cw = pl.BlockSpec((FFN_CONV, cb), lambda c, i: (0, c))
    return pl.pallas_call(
        body,
        name=name,
        grid=(f // cb, t // tm),
        in_specs=[blk, blk, cw, cw],
        out_specs=blk,
        out_shape=jax.ShapeDtypeStruct((t, f), BF16),
        scratch_shapes=[pltpu.VMEM((HALO + tm, cb), F32)] * 2,
        compiler_params=_params(("parallel", "arbitrary")),
    )(up_g, up_v, cw_g, cw_v)


def _ffn_act_bwd(up_g, up_v, cw_g, cw_v, dact, *, name, tm=512, cb=256):
    t, f = up_g.shape
    tm, cb = _tile(t, tm), _tile(f, cb)
    nt = t // tm
    hb = tm // HALO

    def body(ug_ref, uv_ref, pg_ref, pv_ref, cg_ref, cv_ref, da_ref, dg_ref, dv_ref, dcg_ref, dcv_ref,
             xg, xv, dyg, dyv, carry_g, carry_v):
        i = pl.program_id(1)
        first = i == 0
        tile = nt - 1 - i

        @pl.when(tile == 0)
        def _():
            xg[0:HALO, :] = jnp.zeros((HALO, cb), F32)
            xv[0:HALO, :] = jnp.zeros((HALO, cb), F32)

        @pl.when(tile > 0)
        def _():
            xg[0:HALO, :] = pg_ref[...]
            xv[0:HALO, :] = pv_ref[...]

        xg[pl.ds(HALO, tm), :] = ug_ref[...]
        xv[pl.ds(HALO, tm), :] = uv_ref[...]
        yg = _conv_from_ext(xg, cg_ref, FFN_CONV, tm)
        yv = _conv_from_ext(xv, cv_ref, FFN_CONV, tm)
        sg = _sigmoid(yg)
        da = da_ref[...]
        dyg[pl.ds(0, tm), :] = da * yv * (sg * (1.0 + yg * (1.0 - sg)))
        dyv[pl.ds(0, tm), :] = da * (yg * sg)

        @pl.when(first)
        def _():
            dyg[pl.ds(tm, HALO), :] = jnp.zeros((HALO, cb), F32)
            dyv[pl.ds(tm, HALO), :] = jnp.zeros((HALO, cb), F32)

        @pl.when(jnp.logical_not(first))
        def _():
            dyg[pl.ds(tm, HALO), :] = carry_g[...]
            dyv[pl.ds(tm, HALO), :] = carry_v[...]

        dg_ref[...] = _conv_bwd_from_ext(xg, dyg, cg_ref, dcg_ref, FFN_CONV, tm, first).astype(BF16)
        dv_ref[...] = _conv_bwd_from_ext(xv, dyv, cv_ref, dcv_ref, FFN_CONV, tm, first).astype(BF16)
        carry_g[...] = dyg[0:HALO, :]
        carry_v[...] = dyv[0:HALO, :]

    blk = pl.BlockSpec((tm, cb), lambda c, i: (nt - 1 - i, c))
    prev = pl.BlockSpec((HALO, cb), lambda c, i: (jnp.maximum((nt - 1 - i) * hb - 1, 0), c))
    cw = pl.BlockSpec((FFN_CONV, cb), lambda c, i: (0, c))
    return pl.pallas_call(
        body,
        name=name,
        grid=(f // cb, nt),
        in_specs=[blk, blk, prev, prev, cw, cw, blk],
        out_specs=[blk, blk, cw, cw],
        out_shape=[jax.ShapeDtypeStruct((t, f), BF16), jax.ShapeDtypeStruct((t, f), BF16),
                   jax.ShapeDtypeStruct((FFN_CONV, f), F32), jax.ShapeDtypeStruct((FFN_CONV, f), F32)],
        scratch_shapes=[pltpu.VMEM((HALO + tm, cb), F32)] * 2 + [pltpu.VMEM((tm + HALO, cb), F32)] * 2
        + [pltpu.VMEM((HALO, cb), F32)] * 2,
        compiler_params=_params(("parallel", "arbitrary")),
    )(up_g, up_v, up_g, up_v, cw_g, cw_v, dact)


def _ple_fwd(h, zg, pe, *, name, tm=512):
    t, d = h.shape
    tm = _tile(t, tm)

    def body(h_ref, z_ref, p_ref, o_ref):
        o_ref[...] = h_ref[...] + _sigmoid(z_ref[...]) * p_ref[...]

    row = pl.BlockSpec((tm, d), lambda i: (i, 0))
    return pl.pallas_call(
        body, name=name, grid=(t // tm,), in_specs=[row] * 3, out_specs=row,
        out_shape=jax.ShapeDtypeStruct((t, d), F32), compiler_params=_params(("parallel",)),
    )(h, zg, pe)


def _ple_bwd(dh, zg, pe, *, name, tm=512):
    t, d = dh.shape
    tm = _tile(t, tm)

    def body(g_ref, z_ref, p_ref, dz_ref, dp_ref):
        g = g_ref[...]
        sg = _sigmoid(z_ref[...])
        dz_ref[...] = (g * p_ref[...] * sg * (1.0 - sg)).astype(BF16)
        dp_ref[...] = (g * sg).astype(BF16)

    row = pl.BlockSpec((tm, d), lambda i: (i, 0))
    return pl.pallas_call(
        body, name=name, grid=(t // tm,), in_specs=[row] * 3, out_specs=[row] * 2,
        out_shape=[jax.ShapeDtypeStruct((t, d), BF16)] * 2, compiler_params=_params(("parallel",)),
    )(dh, zg, pe)


def _my_pos():
    return lax.axis_index("x"), lax.axis_index("y"), lax.axis_index("c")


def _all_gather(block, *, name):
    r, w = block.shape

    def body(x_ref, out_ref, send_sems, recv_sems, local_sem):
        x, y, c = _my_pos()
        me, sibling = (x, y, c), (x, y, 1 - c)
        chips = [(1 - x, y), (x, 1 - y), (1 - x, 1 - y)]

        def slot(px, py, pc):
            return out_ref.at[4 * px + 2 * py + pc]

        def copy(k, blk, to, src=None):
            return pltpu.make_async_remote_copy(
                src_ref=slot(*blk) if src is None else src, dst_ref=slot(*blk),
                send_sem=send_sems.at[k], recv_sem=recv_sems.at[k],
                device_id=to, device_id_type=pl.DeviceIdType.MESH)

        mine = pltpu.make_async_copy(x_ref, slot(*me), local_sem)
        mine.start()
        first = [copy(0, me, sibling, src=x_ref)]
        first += [copy(1 + j, me, (*chip, c), src=x_ref) for j, chip in enumerate(chips)]
        for cp in first:
            cp.start()
        passed = [copy(4 + j, (*chip, c), sibling) for j, chip in enumerate(chips)]
        for j, chip in enumerate(chips):
            copy(1 + j, (*chip, c), me).wait_recv()
            passed[j].start()
        copy(0, sibling, me).wait_recv()
        for j, chip in enumerate(chips):
            copy(4 + j, (*chip, 1 - c), me).wait_recv()
        for cp in first + passed:
            cp.wait_send()
        mine.wait()

    return pl.pallas_call(
        body,
        name=name,
        out_shape=jax.ShapeDtypeStruct((N_DEV, r, w), block.dtype),
        in_specs=[pl.BlockSpec(memory_space=pl.ANY)],
        out_specs=pl.BlockSpec(memory_space=pl.ANY),
        scratch_shapes=[pltpu.SemaphoreType.DMA((7,)), pltpu.SemaphoreType.DMA((7,)), pltpu.SemaphoreType.DMA],
    )(block)


def _all_to_all(slabs, *, name):
    n, r, w = slabs.shape

    def body(x_ref, out_ref, send_sems, recv_sems, local_sem):
        x, y, c = _my_pos()
        my_idx = 4 * x + 2 * y + c
        mine = pltpu.make_async_copy(x_ref.at[my_idx], out_ref.at[my_idx], local_sem)
        mine.start()
        copies = []
        for k in range(1, N_DEV):
            fx, fy, fc = (k >> 2) & 1, (k >> 1) & 1, k & 1
            px = (1 - x) if fx else x
            py = (1 - y) if fy else y
            pc = (1 - c) if fc else c
            cp = pltpu.make_async_remote_copy(
                src_ref=x_ref.at[4 * px + 2 * py + pc], dst_ref=out_ref.at[my_idx],
                send_sem=send_sems.at[k - 1], recv_sem=recv_sems.at[k - 1],
                device_id=(px, py, pc), device_id_type=pl.DeviceIdType.MESH)
            cp.start()
            copies.append(cp)
        for cp in copies:
            cp.wait_recv()
        for cp in copies:
            cp.wait_send()
        mine.wait()

    return pl.pallas_call(
        body,
        name=name,
        out_shape=jax.ShapeDtypeStruct((n, r, w), slabs.dtype),
        in_specs=[pl.BlockSpec(memory_space=pl.ANY)],
        out_specs=pl.BlockSpec(memory_space=pl.ANY),
        scratch_shapes=[pltpu.SemaphoreType.DMA((7,)), pltpu.SemaphoreType.DMA((7,)), pltpu.SemaphoreType.DMA],
    )(slabs)


def _sum_parts(parts, *, name, tr=512):
    n, r, lanes = parts.shape
    tr = tr if r % tr == 0 else r

    def body(p_ref, g_ref):
        g = p_ref[0].astype(F32)
        for j in range(1, n):
            g = g + p_ref[j].astype(F32)
        g_ref[...] = g

    row = pl.BlockSpec((tr, lanes), lambda i: (i, 0))
    return pl.pallas_call(
        body,
        name=name,
        grid=(r // tr,),
        in_specs=[pl.BlockSpec((n, tr, lanes), lambda i: (0, i, 0))],
        out_specs=row,
        out_shape=jax.ShapeDtypeStruct((r, lanes), F32),
        compiler_params=_params(("parallel",)),
    )(parts)


def _adamw_packed(g, w, m, v, *, name, tr=512):
    r, lanes = g.shape
    tr = tr if r % tr == 0 else r
    c1 = 1.0 / (1.0 - ADAM_B1 ** ADAM_STEP)
    c2 = 1.0 / (1.0 - ADAM_B2 ** ADAM_STEP)

    def body(g_ref, w_ref, m_ref, v_ref, d_ref, nm_ref, nv_ref):
        g = g_ref[...]
        nm = ADAM_B1 * m_ref[...] + (1.0 - ADAM_B1) * g
        nv = ADAM_B2 * v_ref[...] + (1.0 - ADAM_B2) * (g * g)
        nm_ref[...] = nm
        nv_ref[...] = nv
        d_ref[...] = -ADAM_LR * ((nm * c1) / (jnp.sqrt(nv * c2) + ADAM_EPS) + ADAM_WD * w_ref[...])

    row = pl.BlockSpec((tr, lanes), lambda i: (i, 0))
    out = jax.ShapeDtypeStruct((r, lanes), F32)
    return pl.pallas_call(
        body,
        name=name,
        grid=(r // tr,),
        in_specs=[row] * 4,
        out_specs=[row] * 3,
        out_shape=[out] * 3,
        compiler_params=_params(("parallel",)),
    )(g, w, m, v)


BIG = ("a_w_in", "a_w_out", "b_w_in", "b_w_out", "f_w_up", "f_w_down", "ple_w_proj", "ple_w_gate")
CONVS = ("a_conv", "f_conv")
SMALL = ("norm_mix", "norm_ffn", "norm_ple", "norm_final", "a_log", "a_dt_bias", "a_norm", "b_sinks")
WEIGHTS = ("norm_mix", "norm_ffn", "norm_ple", "norm_final", "a_w_in", "a_conv", "a_log", "a_dt_bias", "a_norm",
           "a_w_out", "b_w_in", "b_sinks", "b_w_out", "f_w_up", "f_conv", "f_w_down", "ple_w_proj", "ple_w_gate")
SLAB_ROW_MULTIPLE = 512


def _pack(arrs, dtype, row_multiple):
    flat = jnp.concatenate([a.reshape(-1).astype(dtype) for a in arrs])
    rows = -(-flat.shape[0] // LANES)
    rows = -(-rows // row_multiple) * row_multiple
    return jnp.pad(flat, (0, rows * LANES - flat.shape[0])).reshape(rows, LANES)


def _unpack(slab, shapes):
    lead = slab.shape[:-2]
    flat = slab.reshape(lead + (-1,))
    out, off = [], 0
    for s in shapes:
        size = math.prod(s)
        out.append(flat[..., off:off + size].reshape(lead + tuple(s)))
        off += size
    return out


def _cols_full(g):
    g = jnp.moveaxis(g, 0, -2)
    return g.reshape(g.shape[:-2] + (g.shape[-2] * g.shape[-1],))


def _rows_full(g):
    g = jnp.moveaxis(g, 0, -3)
    return g.reshape(g.shape[:-3] + (g.shape[-3] * g.shape[-2], g.shape[-1]))


def _cols_split(wfull):
    n = wfull.shape[-1] // N_DEV
    g = wfull.reshape(wfull.shape[:-1] + (N_DEV, n))
    return jnp.moveaxis(g, -2, 0)


def _rows_split(wfull):
    k = wfull.shape[-2] // N_DEV
    g = wfull.reshape(wfull.shape[:-2] + (N_DEV, k, wfull.shape[-1]))
    return jnp.moveaxis(g, -3, 0)


TRANSPOSED = ("a_w_in", "b_w_in", "f_w_up", "ple_w_proj")


def _wire(name, a):
    return jnp.swapaxes(a, -1, -2) if name in TRANSPOSED else a


def _wire_shape(name, shape):
    return shape[:-2] + (shape[-1], shape[-2]) if name in TRANSPOSED else tuple(shape)


def _full(name, g):
    return _cols_full(g) if name in CONVS else _rows_full(g)


def _split(name, wfull):
    return _cols_split(wfull) if name in CONVS else _rows_split(wfull)


def _pack_split(grads, names, dtype, row_multiple):
    flat = jnp.concatenate([_split(n, grads[n]).reshape(N_DEV, -1).astype(dtype) for n in names], axis=1)
    rows = -(-flat.shape[1] // LANES)
    rows = -(-rows // row_multiple) * row_multiple
    return jnp.pad(flat, ((0, 0), (0, rows * LANES - flat.shape[1]))).reshape(N_DEV, rows, LANES)


def _pad_cols(a, width):
    return jnp.pad(a, ((0, 0), (0, width - a.shape[1])))


def kernel(x, p, norm_mix, norm_ffn, norm_ple, norm_final, a_w_in, a_conv, a_log, a_dt_bias, a_norm, a_w_out, b_w_in, b_sinks, b_w_out, f_w_up, f_conv, f_w_down, ple_w_proj, ple_w_gate, loss_target, m_norm_mix, m_norm_ffn, m_norm_ple, m_norm_final, m_a_w_in, m_a_conv, m_a_log, m_a_dt_bias, m_a_norm, m_a_w_out, m_b_w_in, m_b_sinks, m_b_w_out, m_f_w_up, m_f_conv, m_f_w_down, m_ple_w_proj, m_ple_w_gate, v_norm_mix, v_norm_ffn, v_norm_ple, v_norm_final, v_a_w_in, v_a_conv, v_a_log, v_a_dt_bias, v_a_norm, v_a_w_out, v_b_w_in, v_b_sinks, v_b_w_out, v_f_w_up, v_f_conv, v_f_w_down, v_ple_w_proj, v_ple_w_gate):
    wts = dict(norm_mix=norm_mix, norm_ffn=norm_ffn, norm_ple=norm_ple, norm_final=norm_final, a_w_in=a_w_in,
               a_conv=a_conv, a_log=a_log, a_dt_bias=a_dt_bias, a_norm=a_norm, a_w_out=a_w_out, b_w_in=b_w_in,
               b_sinks=b_sinks, b_w_out=b_w_out, f_w_up=f_w_up, f_conv=f_conv, f_w_down=f_w_down,
               ple_w_proj=ple_w_proj, ple_w_gate=ple_w_gate)
    mom = dict(norm_mix=m_norm_mix, norm_ffn=m_norm_ffn, norm_ple=m_norm_ple, norm_final=m_norm_final,
               a_w_in=m_a_w_in, a_conv=m_a_conv, a_log=m_a_log, a_dt_bias=m_a_dt_bias, a_norm=m_a_norm,
               a_w_out=m_a_w_out, b_w_in=m_b_w_in, b_sinks=m_b_sinks, b_w_out=m_b_w_out, f_w_up=m_f_w_up,
               f_conv=m_f_conv, f_w_down=m_f_w_down, ple_w_proj=m_ple_w_proj, ple_w_gate=m_ple_w_gate)
    var = dict(norm_mix=v_norm_mix, norm_ffn=v_norm_ffn, norm_ple=v_norm_ple, norm_final=v_norm_final,
               a_w_in=v_a_w_in, a_conv=v_a_conv, a_log=v_a_log, a_dt_bias=v_a_dt_bias, a_norm=v_a_norm,
               a_w_out=v_a_w_out, b_w_in=v_b_w_in, b_sinks=v_b_sinks, b_w_out=v_b_w_out, f_w_up=v_f_w_up,
               f_conv=v_f_conv, f_w_down=v_f_w_down, ple_w_proj=v_ple_w_proj, ple_w_gate=v_ple_w_gate)
    hk = N_HEADS_A * HEAD_DIM_A
    qd = N_HEADS_B * HEAD_DIM_B
    kd = N_KV_B * HEAD_DIM_B
    xs = x[0]
    tgt = loss_target[0]

    big_shapes = [_wire_shape(n, wts[n].shape) for n in BIG]
    conv_shapes = [wts[n].shape for n in CONVS]
    gbig = _all_gather(_pack([_wire(n, wts[n]) for n in BIG], BF16, SLAB_ROW_MULTIPLE), name="gather_matrices")
    gconv = _all_gather(_pack([wts[n] for n in CONVS], F32, 8), name="gather_convs")
    full = {n: _full(n, g) for n, g in zip(BIG, _unpack(gbig, big_shapes))}
    full.update({n: _full(n, g) for n, g in zip(CONVS, _unpack(gconv, conv_shapes))})

    wa_in_t = full["a_w_in"][0]
    wa_qkv_t, wa_z_t = wa_in_t[:3 * hk], wa_in_t[3 * hk:4 * hk]
    wa_ba_t = jnp.pad(wa_in_t[4 * hk:], ((0, LANES - 2 * N_HEADS_A), (0, 0)))
    wa_out = full["a_w_out"][0]
    wb_in_t, wb_out = full["b_w_in"][0], full["b_w_out"][0]
    w_up_g_t = [full["f_w_up"][i][:D_FF] for i in range(2)]
    w_up_v_t = [full["f_w_up"][i][D_FF:] for i in range(2)]
    w_down = [full["f_w_down"][i] for i in range(2)]
    w_pp_t = [full["ple_w_proj"][i] for i in range(2)]
    w_pg = [full["ple_w_gate"][i] for i in range(2)]
    cv_a = full["a_conv"][0]
    cv_g = [full["f_conv"][i][:, :D_FF] for i in range(2)]
    cv_v = [full["f_conv"][i][:, D_FF:] for i in range(2)]
    alog_row = jnp.pad(a_log, ((0, 0), (N_HEADS_A, LANES - 2 * N_HEADS_A)))
    dtb_row = jnp.pad(a_dt_bias, ((0, 0), (N_HEADS_A, LANES - 2 * N_HEADS_A)))
    p_bf = [p[i, 0].astype(BF16) for i in range(2)]

    def ffn_ple_fwd(i, h_a):
        n_f = _rms_fwd(h_a, norm_ffn[i], name=f"l{i}_ffn_norm")
        up_g = _matmul(n_f, w_up_g_t[i], tb=True, name=f"l{i}_ffn_up_gate")
        up_v = _matmul(n_f, w_up_v_t[i], tb=True, name=f"l{i}_ffn_up_val")
        act = _ffn_act_fwd(up_g, up_v, cv_g[i], cv_v[i], name=f"l{i}_ffn_act")
        h_b = _matmul(act, w_down[i], res=h_a, name=f"l{i}_ffn_down")
        n_p = _rms_fwd(h_b, norm_ple[i], name=f"l{i}_ple_norm")
        zg = _matmul(n_p, w_pg[i], name=f"l{i}_ple_gate")
        pe = _matmul(p_bf[i], w_pp_t[i], tb=True, name=f"l{i}_ple_proj")
        h_c = _ple_fwd(h_b, zg, pe, name=f"l{i}_ple_mix")
        return h_c, dict(n_f=n_f, up_g=up_g, up_v=up_v, act=act, h_b=h_b, n_p=n_p, zg=zg, pe=pe)

    n0 = _rms_fwd(xs, norm_mix[0], name="l0_mix_norm")
    pqkv = _matmul(n0, wa_qkv_t, tb=True, name="l0_in_qkv")
    z0 = _matmul(n0, wa_z_t, tb=True, name="l0_in_z")
    pba = _matmul(n0, wa_ba_t, tb=True, name="l0_in_ba")
    q, k, v, gbc, bbc = _delta_pre_fwd(pqkv, pba, cv_a, alog_row, dtb_row, name="l0_delta_pre")
    *prep, tinv = _delta_prep_fwd(q, k, v, gbc, bbc, name="l0_delta_prep")
    o, states = _delta_scan_fwd(prep, name="l0_delta_scan")
    og = _gated_norm_fwd(o, z0, a_norm, name="l0_gated_norm")
    h1 = _matmul(og, wa_out, res=xs, name="l0_mix_out")
    h3, sv0 = ffn_ple_fwd(0, h1)

    n1 = _rms_fwd(h3, norm_mix[1], name="l1_mix_norm")
    pb = _matmul(n1, wb_in_t, tb=True, name="l1_in_qkv")
    att = _swa_fwd(pb, b_sinks, name="l1_swa")
    h4 = _matmul(att, wb_out, res=h3, name="l1_mix_out")
    h6, sv1 = ffn_ple_fwd(1, h4)

    loss_row, dh6, d_norm_final = _final_loss(h6, norm_final, tgt, name="final_loss")
    loss = lax.psum(loss_row[0, 0], MESH_AXES)

    gw = {}

    def ffn_ple_bwd(i, dh_c, h_a, sv):
        dzg, dpe = _ple_bwd(dh_c, sv["zg"], sv["pe"], name=f"l{i}_ple_mix_bwd")
        d_pg = _matmul(sv["n_p"], dzg, ta=True, name=f"l{i}_ple_gate_dw")
        d_pp_t = _matmul(dpe, p_bf[i], ta=True, name=f"l{i}_ple_proj_dw")
        dn_p = _matmul(dzg, w_pg[i], tb=True, name=f"l{i}_ple_gate_dx")
        dh_b, d_np = _rms_bwd(sv["h_b"], norm_ple[i], dn_p, dh_c, name=f"l{i}_ple_norm_bwd")
        dact = _matmul(dh_b, w_down[i], tb=True, name=f"l{i}_ffn_down_dx")
        d_down = _matmul(sv["act"], dh_b, ta=True, name=f"l{i}_ffn_down_dw")
        dup_g, dup_v, d_cg, d_cv = _ffn_act_bwd(sv["up_g"], sv["up_v"], cv_g[i], cv_v[i], dact, name=f"l{i}_ffn_act_bwd")
        d_up_g_t = _matmul(dup_g, sv["n_f"], ta=True, name=f"l{i}_ffn_up_gate_dw")
        d_up_v_t = _matmul(dup_v, sv["n_f"], ta=True, name=f"l{i}_ffn_up_val_dw")
        dn_f = _matmul(dup_g, w_up_g_t[i], name=f"l{i}_ffn_up_gate_dx")
        dn_f = _matmul(dup_v, w_up_v_t[i], res=dn_f, name=f"l{i}_ffn_up_val_dx")
        dh_a, d_nf = _rms_bwd(h_a, norm_ffn[i], dn_f, dh_b, name=f"l{i}_ffn_norm_bwd")
        g = dict(ple_w_gate=d_pg, ple_w_proj=d_pp_t, norm_ple=d_np, f_w_down=d_down,
                 f_w_up=jnp.concatenate([d_up_g_t, d_up_v_t], axis=0), f_conv=jnp.concatenate([d_cg, d_cv], axis=1),
                 norm_ffn=d_nf)
        return dh_a, g

    dh4, g1 = ffn_ple_bwd(1, dh6, h4, sv1)
    datt = _matmul(dh4, wb_out, tb=True, out_dtype=BF16, name="l1_mix_out_dx")
    gw["b_w_out"] = _matmul(att, dh4, ta=True, name="l1_mix_out_dw")
    dq_b, dk_b, dv_b, dsinks = _swa_bwd(pb, b_sinks, datt, name="l1_swa_bwd")
    dpb = jnp.concatenate([dq_b, dk_b.astype(BF16), dv_b.astype(BF16)], axis=1)
    gw["b_w_in"] = _matmul(dpb, n1, ta=True, name="l1_in_qkv_dw")
    dn1 = _matmul(dpb, wb_in_t, name="l1_in_qkv_dx")
    dh3, d_nm1 = _rms_bwd(h3, norm_mix[1], dn1, dh4, name="l1_mix_norm_bwd")

    dh1, g0 = ffn_ple_bwd(0, dh3, h1, sv0)
    dog = _matmul(dh1, wa_out, tb=True, name="l0_mix_out_dx")
    gw["a_w_out"] = _matmul(og, dh1, ta=True, name="l0_mix_out_dw")
    do, dz0, d_anorm = _gated_norm_bwd(o, z0, a_norm, dog, name="l0_gated_norm_bwd")
    cts = _delta_scan_bwd(prep, states, do, name="l0_delta_scan_bwd")
    dq, dk, dv, dgbc, dbbc = _delta_prep_bwd(q, k, v, gbc, bbc, tinv, cts, name="l0_delta_prep_bwd")
    dpqkv, dpba, d_aconv, d_alog, d_dtb = _delta_pre_bwd(pqkv, pba, cv_a, alog_row, dtb_row, dq, dk, dv, dgbc, dbbc,
                                                         name="l0_delta_pre_bwd")
    d_wqkv_t = _matmul(dpqkv, n0, ta=True, name="l0_in_qkv_dw")
    d_wz_t = _matmul(dz0, n0, ta=True, name="l0_in_z_dw")
    d_wba_t = _matmul(dpba, n0, ta=True, name="l0_in_ba_dw")
    gw["a_w_in"] = jnp.concatenate([d_wqkv_t, d_wz_t, d_wba_t[:2 * N_HEADS_A]], axis=0)
    dn0 = _matmul(dpqkv, wa_qkv_t, name="l0_in_qkv_dx")
    dn0 = _matmul(dz0, wa_z_t, res=dn0, name="l0_in_z_dx")
    dn0 = _matmul(dpba, wa_ba_t, res=dn0, name="l0_in_ba_dx")
    dx, d_nm0 = _rms_bwd(xs, norm_mix[0], dn0, dh1, name="l0_mix_norm_bwd")

    for n in ("f_w_up", "f_w_down", "ple_w_proj", "ple_w_gate", "f_conv"):
        gw[n] = jnp.stack([g0[n], g1[n]])
    for n in ("a_w_in", "a_w_out", "b_w_in", "b_w_out"):
        gw[n] = gw[n][None]
    gw["a_conv"] = d_aconv[None]
    small_g = dict(norm_mix=jnp.concatenate([d_nm0, d_nm1]), norm_ffn=jnp.concatenate([g0["norm_ffn"], g1["norm_ffn"]]),
                   norm_ple=jnp.concatenate([g0["norm_ple"], g1["norm_ple"]]), norm_final=d_norm_final[0],
                   a_log=d_alog[:, N_HEADS_A:2 * N_HEADS_A], a_dt_bias=d_dtb[:, N_HEADS_A:2 * N_HEADS_A],
                   a_norm=d_anorm, b_sinks=dsinks[:, :N_HEADS_B])

    recv_big = _all_to_all(_pack_split(gw, BIG, BF16, SLAB_ROW_MULTIPLE), name="exchange_matrix_grads")
    recv_conv = _all_to_all(_pack_split(gw, CONVS, F32, 8), name="exchange_conv_grads")
    recv_small = _all_gather(_pack([small_g[n] for n in SMALL], F32, 8), name="gather_small_grads")

    outs = {}
    for names, recv, mult, tag in ((BIG, recv_big, SLAB_ROW_MULTIPLE, "matrices"), (CONVS, recv_conv, 8, "convs"),
                                   (SMALL, recv_small, 8, "small")):
        shapes = [wts[n].shape for n in names]
        g_wire = _unpack(_sum_parts(recv, name=f"sum_{tag}"), [_wire_shape(n, s) for n, s in zip(names, shapes)])
        grads = [_wire(n, g) for n, g in zip(names, g_wire)]
        packed = [_pack([d[n] for n in names], F32, mult) for d in (wts, mom, var)]
        res = _adamw_packed(_pack(grads, F32, mult), *packed, name=f"adamw_{tag}")
        for n, g in zip(names, grads):
            outs[("grad", n)] = g
        for kind, slab in zip(("delta", "new_m", "new_v"), res):
            for n, arr in zip(names, _unpack(slab, shapes)):
                outs[(kind, n)] = arr

    result = [loss, dx[None]]
    for kind in ("grad", "delta", "new_m", "new_v"):
        result += [outs[(kind, n)] for n in WEIGHTS]
    return tuple(result)
```

```python
import functools
import math

import jax
import jax.numpy as jnp
from jax import lax
from jax.experimental import pallas as pl
from jax.experimental.pallas import tpu as pltpu

F32 = jnp.float32
BF16 = jnp.bfloat16

D_MODEL = 1024
N_HEADS_A = 8
HEAD_DIM_A = 128
CONV_A = 4
CHUNK = 128
N_HEADS_B = 16
N_KV_B = 4
GROUP_B = N_HEADS_B // N_KV_B
HEAD_DIM_B = 64
WINDOW = 128
D_FF = 2816
FFN_CONV = 3
PLE_DIM = 256
EPS = 1e-6
N_DEV = 8
HALO = 8

ADAM_LR = 0.001
ADAM_B1 = 0.9
ADAM_B2 = 0.999
ADAM_EPS = 1e-08
ADAM_WD = 0.01
ADAM_STEP = 10

LANES = 128
VMEM_LIMIT_BYTES = 56 * 1024 * 1024
NEG_BIG = -1e30

MESH_AXES = ("x", "y", "c")


def _params(sem=None):
    return pltpu.CompilerParams(dimension_semantics=sem, vmem_limit_bytes=VMEM_LIMIT_BYTES)


def _tile(n, target):
    best = None
    for t in range(LANES, min(n, target) + 1, LANES):
        if n % t == 0:
            best = t
    return best or n


def _sigmoid(x):
    return 1.0 / (1.0 + jnp.exp(-x))


def _softplus(x):
    return jnp.maximum(x, 0.0) + jnp.log1p(jnp.exp(-jnp.abs(x)))


def _matmul(a, b, *, name, ta=False, tb=False, res=None, out_dtype=F32, tm=1408, tn=1408, tk=None):
    m = a.shape[1] if ta else a.shape[0]
    k = a.shape[0] if ta else a.shape[1]
    n = b.shape[0] if tb else b.shape[1]
    assert (b.shape[1] if tb else b.shape[0]) == k, (a.shape, b.shape, ta, tb)
    if tk is None:
        tk = 1024 if ta else 2816
    tm, tn, tk = _tile(m, tm), _tile(n, tn), _tile(k, tk)
    nk = k // tk
    dims = (((0 if ta else 1,), (1 if tb else 0,)), ((), ()))
    a_spec = pl.BlockSpec((tk, tm), lambda i, j, kk: (kk, i)) if ta else pl.BlockSpec((tm, tk), lambda i, j, kk: (i, kk))
    b_spec = pl.BlockSpec((tn, tk), lambda i, j, kk: (j, kk)) if tb else pl.BlockSpec((tk, tn), lambda i, j, kk: (kk, j))
    o_spec = pl.BlockSpec((tm, tn), lambda i, j, kk: (i, j))
    has_res = res is not None

    def body(*refs):
        a_ref, b_ref = refs[0], refs[1]
        r_ref = refs[2] if has_res else None
        o_ref = refs[3] if has_res else refs[2]
        part = lax.dot_general(a_ref[...].astype(BF16), b_ref[...].astype(BF16), dims, preferred_element_type=F32)

        def finish(acc):
            if has_res:
                acc = acc + r_ref[...].astype(F32)
            o_ref[...] = acc.astype(out_dtype)

        if nk == 1:
            finish(part)
        else:
            acc_ref = refs[-1]
            kk = pl.program_id(2)

            @pl.when(kk == 0)
            def _():
                acc_ref[...] = part

            @pl.when(kk > 0)
            def _():
                acc_ref[...] += part

            @pl.when(kk == nk - 1)
            def _():
                finish(acc_ref[...])

    in_specs = [a_spec, b_spec] + ([o_spec] if has_res else [])
    args = (a, b) + ((res,) if has_res else ())
    return pl.pallas_call(
        body,
        name=name,
        grid=(m // tm, n // tn, nk),
        in_specs=in_specs,
        out_specs=o_spec,
        out_shape=jax.ShapeDtypeStruct((m, n), out_dtype),
        scratch_shapes=[pltpu.VMEM((tm, tn), F32)] if nk > 1 else [],
        compiler_params=_params(("parallel", "parallel", "arbitrary")),
    )(*args)


def _rms_fwd(h, w, *, name, tm=512):
    t, d = h.shape
    tm = _tile(t, tm)

    def body(h_ref, w_ref, o_ref):
        x = h_ref[...]
        r = lax.rsqrt(jnp.mean(x * x, axis=-1, keepdims=True) + EPS)
        o_ref[...] = (x * r * w_ref[...]).astype(BF16)

    return pl.pallas_call(
        body,
        name=name,
        grid=(t // tm,),
        in_specs=[pl.BlockSpec((tm, d), lambda i: (i, 0)), pl.BlockSpec((1, d), lambda i: (0, 0))],
        out_specs=pl.BlockSpec((tm, d), lambda i: (i, 0)),
        out_shape=jax.ShapeDtypeStruct((t, d), BF16),
        compiler_params=_params(("parallel",)),
    )(h, w.reshape(1, d))


def _rms_bwd(h, w, dn, skip, *, name, tm=512):
    t, d = h.shape
    tm = _tile(t, tm)

    def body(h_ref, w_ref, dn_ref, skip_ref, dh_ref, dw_ref):
        i = pl.program_id(0)
        x = h_ref[...]
        r = lax.rsqrt(jnp.mean(x * x, axis=-1, keepdims=True) + EPS)
        nh = x * r
        g = dn_ref[...].astype(F32)
        gw = g * w_ref[...]
        dh_ref[...] = r * (gw - nh * jnp.mean(gw * nh, axis=-1, keepdims=True)) + skip_ref[...]
        part = jnp.sum(g * nh, axis=0, keepdims=True)

        @pl.when(i == 0)
        def _():
            dw_ref[...] = part

        @pl.when(i > 0)
        def _():
            dw_ref[...] += part

    row = pl.BlockSpec((tm, d), lambda i: (i, 0))
    vec = pl.BlockSpec((1, d), lambda i: (0, 0))
    return pl.pallas_call(
        body,
        name=name,
        grid=(t // tm,),
        in_specs=[row, vec, row, row],
        out_specs=[row, vec],
        out_shape=[jax.ShapeDtypeStruct((t, d), F32), jax.ShapeDtypeStruct((1, d), F32)],
        compiler_params=_params(("arbitrary",)),
    )(h, w.reshape(1, d), dn, skip)


def _final_loss(h, w, target, *, name, tm=512):
    t, d = h.shape
    tm = _tile(t, tm)

    def body(h_ref, w_ref, tg_ref, loss_ref, dh_ref, dw_ref):
        i = pl.program_id(0)
        x = h_ref[...]
        r = lax.rsqrt(jnp.mean(x * x, axis=-1, keepdims=True) + EPS)
        nh = x * r
        err = nh * w_ref[...] - tg_ref[...]
        lpart = (0.5 / d) * jnp.sum(jnp.sum(err * err, axis=-1, keepdims=True), axis=0, keepdims=True)
        g = err * (1.0 / d)
        gw = g * w_ref[...]
        dh_ref[...] = r * (gw - nh * jnp.mean(gw * nh, axis=-1, keepdims=True))
        part = jnp.sum(g * nh, axis=0, keepdims=True)
        lrow = jnp.broadcast_to(lpart, (1, LANES))

        @pl.when(i == 0)
        def _():
            dw_ref[...] = part
            loss_ref[...] = lrow

        @pl.when(i > 0)
        def _():
            dw_ref[...] += part
            loss_ref[...] += lrow

    row = pl.BlockSpec((tm, d), lambda i: (i, 0))
    vec = pl.BlockSpec((1, d), lambda i: (0, 0))
    return pl.pallas_call(
        body,
        name=name,
        grid=(t // tm,),
        in_specs=[row, vec, row],
        out_specs=[pl.BlockSpec((1, LANES), lambda i: (0, 0)), row, vec],
        out_shape=[jax.ShapeDtypeStruct((1, LANES), F32), jax.ShapeDtypeStruct((t, d), F32), jax.ShapeDtypeStruct((1, d), F32)],
        compiler_params=_params(("arbitrary",)),
    )(h, w.reshape(1, d), target)


def _conv_from_ext(ext_ref, cw_ref, kw, tm):
    y = cw_ref[kw - 1:kw, :] * ext_ref[pl.ds(HALO, tm), :]
    for i in range(kw - 1):
        y = y + cw_ref[i:i + 1, :] * ext_ref[pl.ds(HALO - (kw - 1) + i, tm), :]
    return y


ROW_CHUNK = 64


def _conv_rows(src_ref, base, rows, cw_ref, kw):
    y = cw_ref[kw - 1:kw, :] * src_ref[pl.ds(base, rows), :]
    for i in range(kw - 1):
        y = y + cw_ref[i:i + 1, :] * src_ref[pl.ds(base - (kw - 1) + i, rows), :]
    return y


def _conv_t_rows(dy_ref, base, rows, cw_ref, kw):
    dx = cw_ref[kw - 1:kw, :] * dy_ref[pl.ds(base, rows), :]
    for i in range(kw - 1):
        dx = dx + cw_ref[i:i + 1, :] * dy_ref[pl.ds(base + kw - 1 - i, rows), :]
    return dx


def _conv_bwd_from_ext(xext_ref, dyext_ref, cw_ref, dcw_ref, kw, tm, first):
    dy = dyext_ref[pl.ds(0, tm), :]
    dx = cw_ref[kw - 1:kw, :] * dy
    for i in range(kw - 1):
        dx = dx + cw_ref[i:i + 1, :] * dyext_ref[pl.ds(kw - 1 - i, tm), :]
    for i in range(kw):
        part = jnp.sum(dy * xext_ref[pl.ds(HALO - (kw - 1) + i, tm), :], axis=0, keepdims=True)

        @pl.when(first)
        def _():
            dcw_ref[i:i + 1, :] = part

        @pl.when(jnp.logical_not(first))
        def _():
            dcw_ref[i:i + 1, :] += part

    return dx


def _delta_pre_fwd(pqkv, pba, conv_w, alog_row, dtb_row, *, name, tm=256):
    t, c3 = pqkv.shape
    hk = N_HEADS_A * HEAD_DIM_A
    tm = _tile(t, tm)

    def body(x_ref, ba_ref, cw_ref, al_ref, db_ref, q_ref, k_ref, v_ref, g_ref, b_ref, ext):
        i = pl.program_id(0)

        @pl.when(i == 0)
        def _():
            ext[0:HALO, :] = jnp.zeros((HALO, c3), F32)

        ext[pl.ds(HALO, tm), :] = x_ref[...]
        y = _conv_from_ext(ext, cw_ref, CONV_A, tm)
        ext[0:HALO, :] = ext[pl.ds(tm, HALO), :]
        s = y * _sigmoid(y)
        for h in range(N_HEADS_A):
            lo = h * HEAD_DIM_A
            for dst, off in ((q_ref, 0), (k_ref, hk)):
                sh = s[:, off + lo:off + lo + HEAD_DIM_A]
                dst[:, lo:lo + HEAD_DIM_A] = sh * lax.rsqrt(jnp.sum(sh * sh, axis=-1, keepdims=True) + EPS)
        v_ref[...] = s[:, 2 * hk:3 * hk]
        ba = ba_ref[...]
        beta = _sigmoid(ba)
        gfull = -jnp.exp(al_ref[...]) * _softplus(ba + db_ref[...])
        for h in range(N_HEADS_A):
            lo = h * HEAD_DIM_A
            b_ref[:, lo:lo + HEAD_DIM_A] = jnp.broadcast_to(beta[:, h:h + 1], (tm, HEAD_DIM_A))
            g_ref[:, lo:lo + HEAD_DIM_A] = jnp.broadcast_to(gfull[:, N_HEADS_A + h:N_HEADS_A + h + 1], (tm, HEAD_DIM_A))

    row = lambda w: pl.BlockSpec((tm, w), lambda i: (i, 0))
    fixed = lambda r, w: pl.BlockSpec((r, w), lambda i: (0, 0))
    out = jax.ShapeDtypeStruct((t, hk), F32)
    return pl.pallas_call(
        body,
        name=name,
        grid=(t // tm,),
        in_specs=[row(c3), row(LANES), fixed(CONV_A, c3), fixed(1, LANES), fixed(1, LANES)],
        out_specs=[row(hk)] * 5,
        out_shape=[out] * 5,
        scratch_shapes=[pltpu.VMEM((HALO + tm, c3), F32)],
        compiler_params=_params(("arbitrary",)),
    )(pqkv, pba, conv_w, alog_row, dtb_row)


def _delta_pre_bwd(pqkv, pba, conv_w, alog_row, dtb_row, dq, dk, dv, dg, db, *, name, tm=256):
    t, c3 = pqkv.shape
    hk = N_HEADS_A * HEAD_DIM_A
    tm = _tile(t, tm)
    nt = t // tm
    hb = tm // HALO

    def body(x_ref, xp_ref, ba_ref, cw_ref, al_ref, db_ref, dq_ref, dk_ref, dv_ref, dg_ref, dbt_ref,
             dx_ref, dba_ref, dcw_ref, dal_ref, ddb_ref, xext, dyext, carry):
        i = pl.program_id(0)
        first = i == 0
        tile = nt - 1 - i

        @pl.when(tile == 0)
        def _():
            xext[0:HALO, :] = jnp.zeros((HALO, c3), F32)

        @pl.when(tile > 0)
        def _():
            xext[0:HALO, :] = xp_ref[...]

        xext[pl.ds(HALO, tm), :] = x_ref[...]
        y = _conv_from_ext(xext, cw_ref, CONV_A, tm)
        sg = _sigmoid(y)
        s = y * sg
        dsilu = sg * (1.0 + y * (1.0 - sg))
        for h in range(N_HEADS_A):
            lo = h * HEAD_DIM_A
            for src, off in ((dq_ref, 0), (dk_ref, hk)):
                sh = s[:, off + lo:off + lo + HEAD_DIM_A]
                r = lax.rsqrt(jnp.sum(sh * sh, axis=-1, keepdims=True) + EPS)
                qn = sh * r
                gq = src[:, lo:lo + HEAD_DIM_A]
                dsh = r * (gq - qn * jnp.sum(gq * qn, axis=-1, keepdims=True))
                dyext[pl.ds(0, tm), off + lo:off + lo + HEAD_DIM_A] = dsh * dsilu[:, off + lo:off + lo + HEAD_DIM_A]
        dyext[pl.ds(0, tm), 2 * hk:3 * hk] = dv_ref[...] * dsilu[:, 2 * hk:3 * hk]

        @pl.when(first)
        def _():
            dyext[pl.ds(tm, HALO), :] = jnp.zeros((HALO, c3), F32)

        @pl.when(jnp.logical_not(first))
        def _():
            dyext[pl.ds(tm, HALO), :] = carry[...]

        dx = _conv_bwd_from_ext(xext, dyext, cw_ref, dcw_ref, CONV_A, tm, first)
        carry[...] = dyext[0:HALO, :]
        dx_ref[...] = dx.astype(BF16)

        lane = lax.broadcasted_iota(jnp.int32, (tm, LANES), 1)
        gcol = jnp.zeros((tm, LANES), F32)
        for h in range(N_HEADS_A):
            lo = h * HEAD_DIM_A
            dbh = jnp.sum(dbt_ref[:, lo:lo + HEAD_DIM_A], axis=-1, keepdims=True)
            dgh = jnp.sum(dg_ref[:, lo:lo + HEAD_DIM_A], axis=-1, keepdims=True)
            gcol = gcol + jnp.where(lane == h, dbh, 0.0) + jnp.where(lane == N_HEADS_A + h, dgh, 0.0)
        ba = ba_ref[...]
        beta = _sigmoid(ba)
        a_neg = -jnp.exp(al_ref[...])
        z = ba + db_ref[...]
        dz = gcol * a_neg * _sigmoid(z)
        is_g = jnp.logical_and(lane >= N_HEADS_A, lane < 2 * N_HEADS_A)
        dba = jnp.where(lane < N_HEADS_A, gcol * beta * (1.0 - beta), jnp.where(is_g, dz, 0.0))
        dba_ref[...] = dba.astype(BF16)
        dal = jnp.sum(jnp.where(is_g, gcol * a_neg * _softplus(z), 0.0), axis=0, keepdims=True)
        ddb = jnp.sum(jnp.where(is_g, dz, 0.0), axis=0, keepdims=True)

        @pl.when(first)
        def _():
            dal_ref[...] = dal
            ddb_ref[...] = ddb

        @pl.when(jnp.logical_not(first))
        def _():
            dal_ref[...] += dal
            ddb_ref[...] += ddb

    rev = lambda w: pl.BlockSpec((tm, w), lambda i: (nt - 1 - i, 0))
    prev = pl.BlockSpec((HALO, c3), lambda i: (jnp.maximum((nt - 1 - i) * hb - 1, 0), 0))
    fixed = lambda r, w: pl.BlockSpec((r, w), lambda i: (0, 0))
    return pl.pallas_call(
        body,
        name=name,
        grid=(nt,),
        in_specs=[rev(c3), prev, rev(LANES), fixed(CONV_A, c3), fixed(1, LANES), fixed(1, LANES)] + [rev(hk)] * 5,
        out_specs=[rev(c3), rev(LANES), fixed(CONV_A, c3), fixed(1, LANES), fixed(1, LANES)],
        out_shape=[jax.ShapeDtypeStruct((t, c3), BF16), jax.ShapeDtypeStruct((t, LANES), BF16),
                   jax.ShapeDtypeStruct((CONV_A, c3), F32), jax.ShapeDtypeStruct((1, LANES), F32),
                   jax.ShapeDtypeStruct((1, LANES), F32)],
        scratch_shapes=[pltpu.VMEM((HALO + tm, c3), F32), pltpu.VMEM((tm + HALO, c3), F32), pltpu.VMEM((HALO, c3), F32)],
        compiler_params=_params(("arbitrary",)),
    )(pqkv, pqkv, pba, conv_w, alog_row, dtb_row, dq, dk, dv, dg, db)


def _gated_norm_fwd(o, z, w, *, name, tm=512):
    t, d = o.shape
    tm = _tile(t, tm)

    def body(o_ref, z_ref, w_ref, y_ref):
        for h in range(N_HEADS_A):
            sl = slice(h * HEAD_DIM_A, (h + 1) * HEAD_DIM_A)
            oh = o_ref[:, sl]
            zh = z_ref[:, sl]
            r = lax.rsqrt(jnp.mean(oh * oh, axis=-1, keepdims=True) + EPS)
            y_ref[:, sl] = (oh * r * w_ref[...] * (zh * _sigmoid(zh))).astype(BF16)

    row = pl.BlockSpec((tm, d), lambda i: (i, 0))
    return pl.pallas_call(
        body,
        name=name,
        grid=(t // tm,),
        in_specs=[row, row, pl.BlockSpec((1, HEAD_DIM_A), lambda i: (0, 0))],
        out_specs=row,
        out_shape=jax.ShapeDtypeStruct((t, d), BF16),
        compiler_params=_params(("parallel",)),
    )(o, z, w)


def _gated_norm_bwd(o, z, w, dy, *, name, tm=512):
    t, d = o.shape
    tm = _tile(t, tm)

    def body(o_ref, z_ref, w_ref, dy_ref, do_ref, dz_ref, dw_ref):
        i = pl.program_id(0)
        dw = jnp.zeros((1, HEAD_DIM_A), F32)
        for h in range(N_HEADS_A):
            sl = slice(h * HEAD_DIM_A, (h + 1) * HEAD_DIM_A)
            oh = o_ref[:, sl]
            zh = z_ref[:, sl]
            g = dy_ref[:, sl]
            r = lax.rsqrt(jnp.mean(oh * oh, axis=-1, keepdims=True) + EPS)
            nh = oh * r
            sg = _sigmoid(zh)
            dz_ref[:, sl] = (g * nh * w_ref[...] * (sg * (1.0 + zh * (1.0 - sg)))).astype(BF16)
            dt = g * (zh * sg)
            dw = dw + jnp.sum(dt * nh, axis=0, keepdims=True)
            dnh = dt * w_ref[...]
            do_ref[:, sl] = r * (dnh - nh * jnp.mean(dnh * nh, axis=-1, keepdims=True))

        @pl.when(i == 0)
        def _():
            dw_ref[...] = dw

        @pl.when(i > 0)
        def _():
            dw_ref[...] += dw

    row = pl.BlockSpec((tm, d), lambda i: (i, 0))
    vec = pl.BlockSpec((1, HEAD_DIM_A), lambda i: (0, 0))
    return pl.pallas_call(
        body,
        name=name,
        grid=(t // tm,),
        in_specs=[row, row, vec, row],
        out_specs=[row, row, vec],
        out_shape=[jax.ShapeDtypeStruct((t, d), F32), jax.ShapeDtypeStruct((t, d), BF16),
                   jax.ShapeDtypeStruct((1, HEAD_DIM_A), F32)],
        compiler_params=_params(("arbitrary",)),
    )(o, z, w, dy)


_NN = (((1,), (0,)), ((), ()))
_NT = (((1,), (1,)), ((), ()))
_TN = (((0,), (0,)), ((), ()))
_DIMS = {"nn": _NN, "nt": _NT, "tn": _TN}


def _raw_dot(a, b, kind, prec):
    dims = _DIMS[kind]
    a_hi, b_hi = a.astype(BF16), b.astype(BF16)
    out = lax.dot_general(a_hi, b_hi, dims, preferred_element_type=F32)
    if prec == "x3":
        a_lo = (a - a_hi.astype(F32)).astype(BF16)
        b_lo = (b - b_hi.astype(F32)).astype(BF16)
        out = out + lax.dot_general(a_hi, b_lo, dims, preferred_element_type=F32)
        out = out + lax.dot_general(a_lo, b_hi, dims, preferred_element_type=F32)
    elif prec == "s3":
        r1 = b - b_hi.astype(F32)
        b_mid = r1.astype(BF16)
        b_lo = (r1 - b_mid.astype(F32)).astype(BF16)
        out = out + lax.dot_general(a_hi, b_mid, dims, preferred_element_type=F32)
        out = out + lax.dot_general(a_hi, b_lo, dims, preferred_element_type=F32)
    return out


def _raw_dots(xs, ys, kind, prec):
    return [_raw_dot(x, y, kind, prec) for x, y in zip(xs, ys)]


@functools.partial(jax.custom_vjp, nondiff_argnums=(2, 3))
def _dots(xs, ys, kind, prec):
    return _raw_dots(xs, ys, kind, prec)


def _dots_fwd(xs, ys, kind, prec):
    return _raw_dots(xs, ys, kind, prec), (xs, ys)


def _dots_bwd(kind, prec, saved, gs):
    xs, ys = saved
    if kind == "nn":
        return _raw_dots(gs, ys, "nt", prec), _raw_dots(xs, gs, "tn", prec)
    if kind == "nt":
        return _raw_dots(gs, ys, "nn", prec), _raw_dots(gs, xs, "tn", prec)
    return _raw_dots(ys, gs, "nt", prec), _raw_dots(xs, gs, "nn", prec)


_dots.defvjp(_dots_fwd, _dots_bwd)


def _eye(c):
    return (lax.broadcasted_iota(jnp.int32, (c, c), 0) == lax.broadcasted_iota(jnp.int32, (c, c), 1)).astype(F32)


def _inv_unit_lower_raw(lmats):
    c = lmats[0].shape[0]
    eye = _eye(c)
    xs = [eye - l for l in lmats]
    ps = lmats
    for _ in range(int(math.log2(c)) - 1):
        ps = _raw_dots(ps, ps, "nn", "bf16")
        xs = [x + d for x, d in zip(xs, _raw_dots(xs, ps, "nn", "bf16"))]
    rs = [x - eye + d for x, d in zip(xs, _raw_dots(lmats, xs, "nn", "x3"))]
    return [x - d for x, d in zip(xs, _raw_dots(xs, rs, "nn", "bf16"))]


@jax.custom_vjp
def _inv_unit_lower(lmats, hints):
    return _inv_unit_lower_raw(lmats) if hints is None else hints


def _inv_fwd(lmats, hints):
    tms = _inv_unit_lower_raw(lmats) if hints is None else hints
    return tms, (tms, hints)


def _inv_bwd(saved, gs):
    tms, hints = saved
    ds = [-d for d in _raw_dots(_raw_dots(tms, gs, "tn", "x3"), tms, "nt", "x3")]
    return ds, (None if hints is None else [jnp.zeros_like(h) for h in hints])


_inv_unit_lower.defvjp(_inv_fwd, _inv_bwd)


def _delta_prep(qs, ks, vs, gs, bs, hints=None):
    c = qs[0].shape[0]
    nh = len(qs)
    ii = lax.broadcasted_iota(jnp.int32, (c, c), 0)
    jj = lax.broadcasted_iota(jnp.int32, (c, c), 1)
    incl = ii >= jj
    strict = ii > jj
    ltri = incl.astype(F32)
    eye = _eye(c)
    m1 = _dots([ltri] * nh, gs, "nn", "s3")
    gtot = [jnp.sum(g, axis=0, keepdims=True) for g in gs]
    decay = [jnp.exp(jnp.where(incl, m - m.T, NEG_BIG)) for m in m1]
    eg = [jnp.exp(m) for m in m1]
    kk = _dots(ks, ks, "nt", "bf16")
    lmats = [jnp.where(strict, b * x * d, 0.0) for b, x, d in zip(bs, kk, decay)]
    tinv = _inv_unit_lower(lmats, hints)
    toff = [t - eye for t in tinv]
    bv = [b * v for b, v in zip(bs, vs)]
    bk = [b * e * k for b, e, k in zip(bs, eg, ks)]
    u0 = [x + d for x, d in zip(bv, _dots(toff, bv, "nn", "bf16"))]
    wk = [x + d for x, d in zip(bk, _dots(toff, bk, "nn", "bf16"))]
    qsc = [q * (HEAD_DIM_A ** -0.5) for q in qs]
    qk = [x * d for x, d in zip(_dots(qsc, ks, "nt", "bf16"), decay)]
    q_dec = [q * e for q, e in zip(qsc, eg)]
    k_dec = [k * jnp.exp(t - m) for k, t, m in zip(ks, gtot, m1)]
    glast = [jnp.broadcast_to(jnp.exp(t), (c, c)) for t in gtot]
    return (u0, wk, qk, q_dec, k_dec, glast), tinv


def _delta_step(ss, u0, wk, qk, q_dec, k_dec, glast):
    us = [a - d for a, d in zip(u0, _dots(wk, ss, "nn", "bf16"))]
    os_ = [a + d for a, d in zip(_dots(q_dec, ss, "nn", "bf16"), _dots(qk, us, "nn", "bf16"))]
    s_new = [g * s + d for g, s, d in zip(glast, ss, _dots(k_dec, us, "tn", "bf16"))]
    return os_, s_new


HEADS_PER_STEP = 8


def _chunk_spec(nc, reverse=False):
    w = HEADS_PER_STEP * HEAD_DIM_A
    if reverse:
        return pl.BlockSpec((CHUNK, w), lambda h, n: (nc - 1 - n, h))
    return pl.BlockSpec((CHUNK, w), lambda h, n: (n, h))


def _head_slices():
    return [slice(j * HEAD_DIM_A, (j + 1) * HEAD_DIM_A) for j in range(HEADS_PER_STEP)]


def _heads(ref):
    return [ref[:, sl] for sl in _head_slices()]


def _delta_prep_fwd(q, k, v, gbc, bbc, *, name):
    t, d = q.shape
    nc = t // CHUNK

    def body(q_ref, k_ref, v_ref, g_ref, b_ref, *outs):
        res, tinv = _delta_prep(*[_heads(r) for r in (q_ref, k_ref, v_ref, g_ref, b_ref)])
        for ref, vals in zip(outs, res + (tinv,)):
            for sl, val in zip(_head_slices(), vals):
                ref[:, sl] = val

    spec = _chunk_spec(nc)
    return pl.pallas_call(
        body,
        name=name,
        grid=(N_HEADS_A // HEADS_PER_STEP, nc),
        in_specs=[spec] * 5,
        out_specs=[spec] * 7,
        out_shape=[jax.ShapeDtypeStruct((t, d), F32)] * 7,
        compiler_params=_params(("parallel", "parallel")),
    )(q, k, v, gbc, bbc)


def _delta_prep_bwd(q, k, v, gbc, bbc, tinv, cts, *, name):
    t, d = q.shape
    nc = t // CHUNK

    def body(q_ref, k_ref, v_ref, g_ref, b_ref, t_ref, c0, c1, c2, c3, c4, c5, *outs):
        def f(q_, k_, v_, g_, b_):
            return _delta_prep(q_, k_, v_, g_, b_, hints=_heads(t_ref))[0]

        _, vjp = jax.vjp(f, *[_heads(r) for r in (q_ref, k_ref, v_ref, g_ref, b_ref)])
        grads = vjp(tuple(_heads(c) for c in (c0, c1, c2, c3, c4, c5)))
        for ref, vals in zip(outs, grads):
            for sl, val in zip(_head_slices(), vals):
                ref[:, sl] = val

    spec = _chunk_spec(nc)
    return pl.pallas_call(
        body,
        name=name,
        grid=(N_HEADS_A // HEADS_PER_STEP, nc),
        in_specs=[spec] * 12,
        out_specs=[spec] * 5,
        out_shape=[jax.ShapeDtypeStruct((t, d), F32)] * 5,
        compiler_params=_params(("parallel", "parallel")),
    )(q, k, v, gbc, bbc, tinv, *cts)


def _delta_scan_fwd(prep, *, name):
    t, d = prep[0].shape
    nc = t // CHUNK

    def body(u0, wk, qk, qd, kd, gl, o_ref, st_ref, s_ref):
        n = pl.program_id(1)

        @pl.when(n == 0)
        def _():
            s_ref[...] = jnp.zeros(s_ref.shape, F32)

        ss = [s_ref[j] for j in range(HEADS_PER_STEP)]
        os_, s_new = _delta_step(ss, *[_heads(r) for r in (u0, wk, qk, qd, kd, gl)])
        for j, sl in enumerate(_head_slices()):
            st_ref[:, sl] = ss[j]
            o_ref[:, sl] = os_[j]
            s_ref[j] = s_new[j]

    spec = _chunk_spec(nc)
    return pl.pallas_call(
        body,
        name=name,
        grid=(N_HEADS_A // HEADS_PER_STEP, nc),
        in_specs=[spec] * 6,
        out_specs=[spec] * 2,
        out_shape=[jax.ShapeDtypeStruct((t, d), F32)] * 2,
        scratch_shapes=[pltpu.VMEM((HEADS_PER_STEP, HEAD_DIM_A, HEAD_DIM_A), F32)],
        compiler_params=_params(("parallel", "arbitrary")),
    )(*prep)


def _delta_scan_bwd(prep, states, do, *, name):
    t, d = do.shape
    nc = t // CHUNK

    def body(u0, wk, qk, qd, kd, gl, st_ref, do_ref, *rest):
        outs, ds_ref = rest[:6], rest[6]
        n = pl.program_id(1)

        @pl.when(n == 0)
        def _():
            ds_ref[...] = jnp.zeros(ds_ref.shape, F32)

        _, vjp = jax.vjp(_delta_step, *[_heads(r) for r in (st_ref, u0, wk, qk, qd, kd, gl)])
        grads = vjp((_heads(do_ref), [ds_ref[j] for j in range(HEADS_PER_STEP)]))
        for j, sl in enumerate(_head_slices()):
            ds_ref[j] = grads[0][j]
            for ref, vals in zip(outs, grads[1:]):
                ref[:, sl] = vals[j]

    spec = _chunk_spec(nc, reverse=True)
    return pl.pallas_call(
        body,
        name=name,
        grid=(N_HEADS_A // HEADS_PER_STEP, nc),
        in_specs=[spec] * 8,
        out_specs=[spec] * 6,
        out_shape=[jax.ShapeDtypeStruct((t, d), F32)] * 6,
        scratch_shapes=[pltpu.VMEM((HEADS_PER_STEP, HEAD_DIM_A, HEAD_DIM_A), F32)],
        compiler_params=_params(("parallel", "arbitrary")),
    )(*prep, states, do)


def _alibi_slope(h):
    return 2.0 ** (-8.0 * (h + 1) / N_HEADS_B)


def _swa_probs(qh, kcat, slope, sink, blk):
    s = lax.dot_general(qh.astype(BF16), kcat.astype(BF16), _NT, preferred_element_type=F32) * (HEAD_DIM_B ** -0.5)
    qi = lax.broadcasted_iota(jnp.int32, (WINDOW, 2 * WINDOW), 0)
    kj = lax.broadcasted_iota(jnp.int32, (WINDOW, 2 * WINDOW), 1)
    dist = qi + WINDOW - kj
    valid = (dist >= 0) & (dist < WINDOW) & (blk * WINDOW - WINDOW + kj >= 0)
    logits = jnp.where(valid, s - slope * dist.astype(F32), NEG_BIG)
    m = jnp.maximum(jnp.max(logits, axis=-1, keepdims=True), sink)
    e = jnp.exp(logits - m)
    es = jnp.exp(sink - m)
    inv = 1.0 / (jnp.sum(e, axis=-1, keepdims=True) + es)
    return e * inv, es * inv


def _swa_fwd(proj, sinks, *, name):
    t = proj.shape[0]
    nb = t // WINDOW
    qd = N_HEADS_B * HEAD_DIM_B
    kd = N_KV_B * HEAD_DIM_B

    def body(sink_ref, q_ref, kp_ref, kc_ref, vp_ref, vc_ref, o_ref):
        blk = pl.program_id(0)
        for hk in range(N_KV_B):
            ks = slice(hk * HEAD_DIM_B, (hk + 1) * HEAD_DIM_B)
            kcat = jnp.concatenate([kp_ref[:, ks], kc_ref[:, ks]], axis=0)
            vcat = jnp.concatenate([vp_ref[:, ks], vc_ref[:, ks]], axis=0).astype(BF16)
            for g in range(GROUP_B):
                h = hk * GROUP_B + g
                hs = slice(h * HEAD_DIM_B, (h + 1) * HEAD_DIM_B)
                p, _ = _swa_probs(q_ref[:, hs], kcat, _alibi_slope(h), sink_ref[0, h], blk)
                o_ref[:, hs] = jnp.dot(p.astype(BF16), vcat, preferred_element_type=F32).astype(BF16)

    q_spec = pl.BlockSpec((WINDOW, qd), lambda i: (i, 0))
    kv = lambda col, prev: pl.BlockSpec((WINDOW, kd), (lambda i: (jnp.maximum(i - 1, 0), col)) if prev else (lambda i: (i, col)))
    kcol, vcol = qd // kd, qd // kd + 1
    return pl.pallas_call(
        body,
        name=name,
        grid=(nb,),
        in_specs=[pl.BlockSpec(memory_space=pltpu.SMEM), q_spec, kv(kcol, True), kv(kcol, False), kv(vcol, True), kv(vcol, False)],
        out_specs=q_spec,
        out_shape=jax.ShapeDtypeStruct((t, qd), BF16),
        compiler_params=_params(("parallel",)),
    )(sinks, proj, proj, proj, proj, proj)


def _swa_bwd(proj, sinks, dout, *, name):
    t = proj.shape[0]
    nb = t // WINDOW
    qd = N_HEADS_B * HEAD_DIM_B
    kd = N_KV_B * HEAD_DIM_B
    scale = HEAD_DIM_B ** -0.5

    def body(sink_ref, q_ref, kp_ref, kc_ref, vp_ref, vc_ref, do_ref, dq_ref, dk_ref, dv_ref, dsk_ref):
        blk = pl.program_id(0)
        lane = lax.broadcasted_iota(jnp.int32, (1, LANES), 1)

        @pl.when(blk == 0)
        def _():
            dk_ref[...] = jnp.zeros((t, kd), F32)
            dv_ref[...] = jnp.zeros((t, kd), F32)
            dsk_ref[...] = jnp.zeros((1, LANES), F32)

        cur = pl.ds(pl.multiple_of(blk * WINDOW, WINDOW), WINDOW)
        prv = pl.ds(pl.multiple_of(jnp.maximum(blk - 1, 0) * WINDOW, WINDOW), WINDOW)
        dsk = jnp.zeros((1, LANES), F32)
        for hk in range(N_KV_B):
            ks = slice(hk * HEAD_DIM_B, (hk + 1) * HEAD_DIM_B)
            kcat = jnp.concatenate([kp_ref[:, ks], kc_ref[:, ks]], axis=0)
            vcat = jnp.concatenate([vp_ref[:, ks], vc_ref[:, ks]], axis=0).astype(BF16)
            dkc = jnp.zeros((2 * WINDOW, HEAD_DIM_B), F32)
            dvc = jnp.zeros((2 * WINDOW, HEAD_DIM_B), F32)
            for g in range(GROUP_B):
                h = hk * GROUP_B + g
                hs = slice(h * HEAD_DIM_B, (h + 1) * HEAD_DIM_B)
                qh = q_ref[:, hs]
                p, psink = _swa_probs(qh, kcat, _alibi_slope(h), sink_ref[0, h], blk)
                doh = do_ref[:, hs].astype(BF16)
                dp = lax.dot_general(doh, vcat, _NT, preferred_element_type=F32)
                delta = jnp.sum(p * dp, axis=-1, keepdims=True)
                ds = (p * (dp - delta) * scale).astype(BF16)
                dsk = dsk + jnp.where(lane == h, jnp.sum(-psink * delta, axis=0, keepdims=True), 0.0)
                dq_ref[:, hs] = jnp.dot(ds, kcat.astype(BF16), preferred_element_type=F32).astype(BF16)
                dkc = dkc + lax.dot_general(ds, qh.astype(BF16), _TN, preferred_element_type=F32)
                dvc = dvc + lax.dot_general(p.astype(BF16), doh, _TN, preferred_element_type=F32)
            dk_ref[cur, ks] += dkc[WINDOW:, :]
            dv_ref[cur, ks] += dvc[WINDOW:, :]

            @pl.when(blk > 0)
            def _():
                dk_ref[prv, ks] += dkc[:WINDOW, :]
                dv_ref[prv, ks] += dvc[:WINDOW, :]

        dsk_ref[...] += dsk

    q_spec = pl.BlockSpec((WINDOW, qd), lambda i: (i, 0))
    kv = lambda col, prev: pl.BlockSpec((WINDOW, kd), (lambda i: (jnp.maximum(i - 1, 0), col)) if prev else (lambda i: (i, col)))
    kcol, vcol = qd // kd, qd // kd + 1
    full = pl.BlockSpec((t, kd), lambda i: (0, 0))
    return pl.pallas_call(
        body,
        name=name,
        grid=(nb,),
        in_specs=[pl.BlockSpec(memory_space=pltpu.SMEM), q_spec, kv(kcol, True), kv(kcol, False), kv(vcol, True), kv(vcol, False), q_spec],
        out_specs=[q_spec, full, full, pl.BlockSpec((1, LANES), lambda i: (0, 0))],
        out_shape=[jax.ShapeDtypeStruct((t, qd), BF16), jax.ShapeDtypeStruct((t, kd), F32),
                   jax.ShapeDtypeStruct((t, kd), F32), jax.ShapeDtypeStruct((1, LANES), F32)],
        compiler_params=_params(("arbitrary",)),
    )(sinks, proj, proj, proj, proj, proj, dout)


def _ffn_act_fwd(up_g, up_v, cw_g, cw_v, *, name, tm=512, cb=256):
    t, f = up_g.shape
    tm, cb = _tile(t, tm), _tile(f, cb)

    rc = min(ROW_CHUNK, tm)

    def body(ug_ref, uv_ref, cg_ref, cv_ref, a_ref, hg, hv):
        i = pl.program_id(1)

        @pl.when(i == 0)
        def _():
            hg[0:HALO, :] = jnp.zeros((HALO, cb), F32)
            hv[0:HALO, :] = jnp.zeros((HALO, cb), F32)

        hg[pl.ds(HALO, rc), :] = ug_ref[0:rc, :]
        hv[pl.ds(HALO, rc), :] = uv_ref[0:rc, :]
        for r in range(tm // rc):
            if r == 0:
                yg = _conv_rows(hg, HALO, rc, cg_ref, FFN_CONV)
                yv = _conv_rows(hv, HALO, rc, cv_ref, FFN_CONV)
            else:
                yg = _conv_rows(ug_ref, r * rc, rc, cg_ref, FFN_CONV)
                yv = _conv_rows(uv_ref, r * rc, rc, cv_ref, FFN_CONV)
            a_ref[r * rc:(r + 1) * rc, :] = (yg * _sigmoid(yg) * yv).astype(BF16)
        hg[0:HALO, :] = ug_ref[tm - HALO:tm, :]
        hv[0:HALO, :] = uv_ref[tm - HALO:tm, :]

    blk = pl.BlockSpec((tm, cb), lambda c, i: (i, c))
    cw = pl.BlockSpec((FFN_CONV, cb), lambda c, i: (0, c))
    return pl.pallas_call(
        body,
        name=name,
        grid=(f // cb, t // tm),
        in_specs=[blk, blk, cw, cw],
        out_specs=blk,
        out_shape=jax.ShapeDtypeStruct((t, f), BF16),
        scratch_shapes=[pltpu.VMEM((HALO + rc, cb), F32)] * 2,
        compiler_params=_params(("parallel", "arbitrary")),
    )(up_g, up_v, cw_g, cw_v)


def _ffn_act_bwd(up_g, up_v, cw_g, cw_v, dact, *, name, tm=512, cb=256):
    t, f = up_g.shape
    tm, cb = _tile(t, tm), _tile(f, cb)
    nt = t // tm
    hb = tm // HALO

    rc = min(ROW_CHUNK, tm)
    nr = tm // rc
    kw = FFN_CONV

    def body(ug_ref, uv_ref, pg_ref, pv_ref, cg_ref, cv_ref, da_ref, dg_ref, dv_ref, dcg_ref, dcv_ref,
             hg, hv, dyg, dyv):
        i = pl.program_id(1)
        first = i == 0
        tile = nt - 1 - i

        @pl.when(tile == 0)
        def _():
            hg[0:HALO, :] = jnp.zeros((HALO, cb), F32)
            hv[0:HALO, :] = jnp.zeros((HALO, cb), F32)

        @pl.when(tile > 0)
        def _():
            hg[0:HALO, :] = pg_ref[...]
            hv[0:HALO, :] = pv_ref[...]

        @pl.when(first)
        def _():
            dyg[pl.ds(tm, HALO), :] = jnp.zeros((HALO, cb), F32)
            dyv[pl.ds(tm, HALO), :] = jnp.zeros((HALO, cb), F32)

        hg[pl.ds(HALO, rc), :] = ug_ref[0:rc, :]
        hv[pl.ds(HALO, rc), :] = uv_ref[0:rc, :]
        dcg = [jnp.zeros((1, cb), F32) for _ in range(kw)]
        dcv = [jnp.zeros((1, cb), F32) for _ in range(kw)]
        for r in reversed(range(nr)):
            rows = slice(r * rc, (r + 1) * rc)
            src_g, src_v, base = (hg, hv, HALO) if r == 0 else (ug_ref, uv_ref, r * rc)
            yg = _conv_rows(src_g, base, rc, cg_ref, kw)
            yv = _conv_rows(src_v, base, rc, cv_ref, kw)
            sg = _sigmoid(yg)
            da = da_ref[rows, :]
            dy_g = da * yv * (sg * (1.0 + yg * (1.0 - sg)))
            dy_v = da * (yg * sg)
            dyg[rows, :] = dy_g
            dyv[rows, :] = dy_v
            dg_ref[rows, :] = _conv_t_rows(dyg, r * rc, rc, cg_ref, kw).astype(BF16)
            dv_ref[rows, :] = _conv_t_rows(dyv, r * rc, rc, cv_ref, kw).astype(BF16)
            for j in range(kw):
                dcg[j] = dcg[j] + jnp.sum(dy_g * src_g[pl.ds(base - (kw - 1) + j, rc), :], axis=0, keepdims=True)
                dcv[j] = dcv[j] + jnp.sum(dy_v * src_v[pl.ds(base - (kw - 1) + j, rc), :], axis=0, keepdims=True)
        dyg[pl.ds(tm, HALO), :] = dyg[0:HALO, :]
        dyv[pl.ds(tm, HALO), :] = dyv[0:HALO, :]
        for j in range(kw):
            @pl.when(first)
            def _():
                dcg_ref[j:j + 1, :] = dcg[j]
                dcv_ref[j:j + 1, :] = dcv[j]

            @pl.when(jnp.logical_not(first))
            def _():
                dcg_ref[j:j + 1, :] += dcg[j]
                dcv_ref[j:j + 1, :] += dcv[j]

    blk = pl.BlockSpec((tm, cb), lambda c, i: (nt - 1 - i, c))
    prev = pl.BlockSpec((HALO, cb), lambda c, i: (jnp.maximum((nt - 1 - i) * hb - 1, 0), c))
    cw = pl.BlockSpec((FFN_CONV, cb), lambda c, i: (0, c))
    return pl.pallas_call(
        body,
        name=name,
        grid=(f // cb, nt),
        in_specs=[blk, blk, prev, prev, cw, cw, blk],
        out_specs=[blk, blk, cw, cw],
        out_shape=[jax.ShapeDtypeStruct((t, f), BF16), jax.ShapeDtypeStruct((t, f), BF16),
                   jax.ShapeDtypeStruct((FFN_CONV, f), F32), jax.ShapeDtypeStruct((FFN_CONV, f), F32)],
        scratch_shapes=[pltpu.VMEM((HALO + rc, cb), F32)] * 2 + [pltpu.VMEM((tm + HALO, cb), F32)] * 2,
        compiler_params=_params(("parallel", "arbitrary")),
    )(up_g, up_v, up_g, up_v, cw_g, cw_v, dact)


def _ple_fwd(h, zg, pe, *, name, tm=512):
    t, d = h.shape
    tm = _tile(t, tm)

    def body(h_ref, z_ref, p_ref, o_ref):
        o_ref[...] = h_ref[...] + _sigmoid(z_ref[...]) * p_ref[...]

    row = pl.BlockSpec((tm, d), lambda i: (i, 0))
    return pl.pallas_call(
        body, name=name, grid=(t // tm,), in_specs=[row] * 3, out_specs=row,
        out_shape=jax.ShapeDtypeStruct((t, d), F32), compiler_params=_params(("parallel",)),
    )(h, zg, pe)


def _ple_bwd(dh, zg, pe, *, name, tm=512):
    t, d = dh.shape
    tm = _tile(t, tm)

    def body(g_ref, z_ref, p_ref, dz_ref, dp_ref):
        g = g_ref[...]
        sg = _sigmoid(z_ref[...])
        dz_ref[...] = (g * p_ref[...] * sg * (1.0 - sg)).astype(BF16)
        dp_ref[...] = (g * sg).astype(BF16)

    row = pl.BlockSpec((tm, d), lambda i: (i, 0))
    return pl.pallas_call(
        body, name=name, grid=(t // tm,), in_specs=[row] * 3, out_specs=[row] * 2,
        out_shape=[jax.ShapeDtypeStruct((t, d), BF16)] * 2, compiler_params=_params(("parallel",)),
    )(dh, zg, pe)


def _my_pos():
    return lax.axis_index("x"), lax.axis_index("y"), lax.axis_index("c")


def _all_gather(block, *, name):
    r, w = block.shape

    def body(x_ref, out_ref, send_sems, recv_sems, local_sem):
        x, y, c = _my_pos()
        me, sibling = (x, y, c), (x, y, 1 - c)
        chips = [(1 - x, y), (x, 1 - y), (1 - x, 1 - y)]

        def slot(px, py, pc):
            return out_ref.at[4 * px + 2 * py + pc]

        def copy(k, blk, to, src=None):
            return pltpu.make_async_remote_copy(
                src_ref=slot(*blk) if src is None else src, dst_ref=slot(*blk),
                send_sem=send_sems.at[k], recv_sem=recv_sems.at[k],
                device_id=to, device_id_type=pl.DeviceIdType.MESH)

        mine = pltpu.make_async_copy(x_ref, slot(*me), local_sem)
        mine.start()
        first = [copy(0, me, sibling, src=x_ref)]
        first += [copy(1 + j, me, (*chip, c), src=x_ref) for j, chip in enumerate(chips)]
        for cp in first:
            cp.start()
        passed = [copy(4 + j, (*chip, c), sibling) for j, chip in enumerate(chips)]
        for j, chip in enumerate(chips):
            copy(1 + j, (*chip, c), me).wait_recv()
            passed[j].start()
        copy(0, sibling, me).wait_recv()
        for j, chip in enumerate(chips):
            copy(4 + j, (*chip, 1 - c), me).wait_recv()
        for cp in first + passed:
            cp.wait_send()
        mine.wait()

    return pl.pallas_call(
        body,
        name=name,
        out_shape=jax.ShapeDtypeStruct((N_DEV, r, w), block.dtype),
        in_specs=[pl.BlockSpec(memory_space=pl.ANY)],
        out_specs=pl.BlockSpec(memory_space=pl.ANY),
        scratch_shapes=[pltpu.SemaphoreType.DMA((7,)), pltpu.SemaphoreType.DMA((7,)), pltpu.SemaphoreType.DMA],
    )(block)


def _all_to_all(slabs, *, name):
    n, r, w = slabs.shape

    def body(x_ref, out_ref, send_sems, recv_sems, local_sem):
        x, y, c = _my_pos()
        my_idx = 4 * x + 2 * y + c
        mine = pltpu.make_async_copy(x_ref.at[my_idx], out_ref.at[my_idx], local_sem)
        mine.start()
        copies = []
        for k in range(1, N_DEV):
            fx, fy, fc = (k >> 2) & 1, (k >> 1) & 1, k & 1
            px = (1 - x) if fx else x
            py = (1 - y) if fy else y
            pc = (1 - c) if fc else c
            cp = pltpu.make_async_remote_copy(
                src_ref=x_ref.at[4 * px + 2 * py + pc], dst_ref=out_ref.at[my_idx],
                send_sem=send_sems.at[k - 1], recv_sem=recv_sems.at[k - 1],
                device_id=(px, py, pc), device_id_type=pl.DeviceIdType.MESH)
            cp.start()
            copies.append(cp)
        for cp in copies:
            cp.wait_recv()
        for cp in copies:
            cp.wait_send()
        mine.wait()

    return pl.pallas_call(
        body,
        name=name,
        out_shape=jax.ShapeDtypeStruct((n, r, w), slabs.dtype),
        in_specs=[pl.BlockSpec(memory_space=pl.ANY)],
        out_specs=pl.BlockSpec(memory_space=pl.ANY),
        scratch_shapes=[pltpu.SemaphoreType.DMA((7,)), pltpu.SemaphoreType.DMA((7,)), pltpu.SemaphoreType.DMA],
    )(slabs)


def _sum_parts(parts, *, name, tr=512):
    n, r, lanes = parts.shape
    tr = tr if r % tr == 0 else r

    def body(p_ref, g_ref):
        g = p_ref[0].astype(F32)
        for j in range(1, n):
            g = g + p_ref[j].astype(F32)
        g_ref[...] = g

    row = pl.BlockSpec((tr, lanes), lambda i: (i, 0))
    return pl.pallas_call(
        body,
        name=name,
        grid=(r // tr,),
        in_specs=[pl.BlockSpec((n, tr, lanes), lambda i: (0, i, 0))],
        out_specs=row,
        out_shape=jax.ShapeDtypeStruct((r, lanes), F32),
        compiler_params=_params(("parallel",)),
    )(parts)


def _adamw_packed(g, w, m, v, *, name, tr=512):
    r, lanes = g.shape
    tr = tr if r % tr == 0 else r
    c1 = 1.0 / (1.0 - ADAM_B1 ** ADAM_STEP)
    c2 = 1.0 / (1.0 - ADAM_B2 ** ADAM_STEP)

    def body(g_ref, w_ref, m_ref, v_ref, d_ref, nm_ref, nv_ref):
        g = g_ref[...]
        nm = ADAM_B1 * m_ref[...] + (1.0 - ADAM_B1) * g
        nv = ADAM_B2 * v_ref[...] + (1.0 - ADAM_B2) * (g * g)
        nm_ref[...] = nm
        nv_ref[...] = nv
        d_ref[...] = -ADAM_LR * ((nm * c1) / (jnp.sqrt(nv * c2) + ADAM_EPS) + ADAM_WD * w_ref[...])

    row = pl.BlockSpec((tr, lanes), lambda i: (i, 0))
    out = jax.ShapeDtypeStruct((r, lanes), F32)
    return pl.pallas_call(
        body,
        name=name,
        grid=(r // tr,),
        in_specs=[row] * 4,
        out_specs=[row] * 3,
        out_shape=[out] * 3,
        compiler_params=_params(("parallel",)),
    )(g, w, m, v)


BIG = ("a_w_in", "a_w_out", "b_w_in", "b_w_out", "f_w_up", "f_w_down", "ple_w_proj", "ple_w_gate")
CONVS = ("a_conv", "f_conv")
SMALL = ("norm_mix", "norm_ffn", "norm_ple", "norm_final", "a_log", "a_dt_bias", "a_norm", "b_sinks")
WEIGHTS = ("norm_mix", "norm_ffn", "norm_ple", "norm_final", "a_w_in", "a_conv", "a_log", "a_dt_bias", "a_norm",
           "a_w_out", "b_w_in", "b_sinks", "b_w_out", "f_w_up", "f_conv", "f_w_down", "ple_w_proj", "ple_w_gate")
SLAB_ROW_MULTIPLE = 512


def _pack(arrs, dtype, row_multiple):
    flat = jnp.concatenate([a.reshape(-1).astype(dtype) for a in arrs])
    rows = -(-flat.shape[0] // LANES)
    rows = -(-rows // row_multiple) * row_multiple
    return jnp.pad(flat, (0, rows * LANES - flat.shape[0])).reshape(rows, LANES)


def _unpack(slab, shapes):
    lead = slab.shape[:-2]
    flat = slab.reshape(lead + (-1,))
    out, off = [], 0
    for s in shapes:
        size = math.prod(s)
        out.append(flat[..., off:off + size].reshape(lead + tuple(s)))
        off += size
    return out


def _cols_full(g):
    g = jnp.moveaxis(g, 0, -2)
    return g.reshape(g.shape[:-2] + (g.shape[-2] * g.shape[-1],))


def _rows_full(g):
    g = jnp.moveaxis(g, 0, -3)
    return g.reshape(g.shape[:-3] + (g.shape[-3] * g.shape[-2], g.shape[-1]))


def _cols_split(wfull):
    n = wfull.shape[-1] // N_DEV
    g = wfull.reshape(wfull.shape[:-1] + (N_DEV, n))
    return jnp.moveaxis(g, -2, 0)


def _rows_split(wfull):
    k = wfull.shape[-2] // N_DEV
    g = wfull.reshape(wfull.shape[:-2] + (N_DEV, k, wfull.shape[-1]))
    return jnp.moveaxis(g, -3, 0)


TRANSPOSED = ("a_w_in", "b_w_in", "f_w_up", "ple_w_proj")


def _wire(name, a):
    return jnp.swapaxes(a, -1, -2) if name in TRANSPOSED else a


def _wire_shape(name, shape):
    return shape[:-2] + (shape[-1], shape[-2]) if name in TRANSPOSED else tuple(shape)


def _full(name, g):
    return _cols_full(g) if name in CONVS else _rows_full(g)


def _split(name, wfull):
    return _cols_split(wfull) if name in CONVS else _rows_split(wfull)


def _pack_split(grads, names, dtype, row_multiple):
    flat = jnp.concatenate([_split(n, grads[n]).reshape(N_DEV, -1).astype(dtype) for n in names], axis=1)
    rows = -(-flat.shape[1] // LANES)
    rows = -(-rows // row_multiple) * row_multiple
    return jnp.pad(flat, ((0, 0), (0, rows * LANES - flat.shape[1]))).reshape(N_DEV, rows, LANES)


def _pad_cols(a, width):
    return jnp.pad(a, ((0, 0), (0, width - a.shape[1])))


def kernel(x, p, norm_mix, norm_ffn, norm_ple, norm_final, a_w_in, a_conv, a_log, a_dt_bias, a_norm, a_w_out, b_w_in, b_sinks, b_w_out, f_w_up, f_conv, f_w_down, ple_w_proj, ple_w_gate, loss_target, m_norm_mix, m_norm_ffn, m_norm_ple, m_norm_final, m_a_w_in, m_a_conv, m_a_log, m_a_dt_bias, m_a_norm, m_a_w_out, m_b_w_in, m_b_sinks, m_b_w_out, m_f_w_up, m_f_conv, m_f_w_down, m_ple_w_proj, m_ple_w_gate, v_norm_mix, v_norm_ffn, v_norm_ple, v_norm_final, v_a_w_in, v_a_conv, v_a_log, v_a_dt_bias, v_a_norm, v_a_w_out, v_b_w_in, v_b_sinks, v_b_w_out, v_f_w_up, v_f_conv, v_f_w_down, v_ple_w_proj, v_ple_w_gate):
    wts = dict(norm_mix=norm_mix, norm_ffn=norm_ffn, norm_ple=norm_ple, norm_final=norm_final, a_w_in=a_w_in,
               a_conv=a_conv, a_log=a_log, a_dt_bias=a_dt_bias, a_norm=a_norm, a_w_out=a_w_out, b_w_in=b_w_in,
               b_sinks=b_sinks, b_w_out=b_w_out, f_w_up=f_w_up, f_conv=f_conv, f_w_down=f_w_down,
               ple_w_proj=ple_w_proj, ple_w_gate=ple_w_gate)
    mom = dict(norm_mix=m_norm_mix, norm_ffn=m_norm_ffn, norm_ple=m_norm_ple, norm_final=m_norm_final,
               a_w_in=m_a_w_in, a_conv=m_a_conv, a_log=m_a_log, a_dt_bias=m_a_dt_bias, a_norm=m_a_norm,
               a_w_out=m_a_w_out, b_w_in=m_b_w_in, b_sinks=m_b_sinks, b_w_out=m_b_w_out, f_w_up=m_f_w_up,
               f_conv=m_f_conv, f_w_down=m_f_w_down, ple_w_proj=m_ple_w_proj, ple_w_gate=m_ple_w_gate)
    var = dict(norm_mix=v_norm_mix, norm_ffn=v_norm_ffn, norm_ple=v_norm_ple, norm_final=v_norm_final,
               a_w_in=v_a_w_in, a_conv=v_a_conv, a_log=v_a_log, a_dt_bias=v_a_dt_bias, a_norm=v_a_norm,
               a_w_out=v_a_w_out, b_w_in=v_b_w_in, b_sinks=v_b_sinks, b_w_out=v_b_w_out, f_w_up=v_f_w_up,
               f_conv=v_f_conv, f_w_down=v_f_w_down, ple_w_proj=v_ple_w_proj, ple_w_gate=v_ple_w_gate)
    hk = N_HEADS_A * HEAD_DIM_A
    qd = N_HEADS_B * HEAD_DIM_B
    kd = N_KV_B * HEAD_DIM_B
    xs = x[0]
    tgt = loss_target[0]

    big_shapes = [_wire_shape(n, wts[n].shape) for n in BIG]
    conv_shapes = [wts[n].shape for n in CONVS]
    gbig = _all_gather(_pack([_wire(n, wts[n]) for n in BIG], BF16, SLAB_ROW_MULTIPLE), name="gather_matrices")
    gconv = _all_gather(_pack([wts[n] for n in CONVS], F32, 8), name="gather_convs")
    full = {n: _full(n, g) for n, g in zip(BIG, _unpack(gbig, big_shapes))}
    full.update({n: _full(n, g) for n, g in zip(CONVS, _unpack(gconv, conv_shapes))})

    wa_in_t = full["a_w_in"][0]
    wa_qkv_t, wa_z_t = wa_in_t[:3 * hk], wa_in_t[3 * hk:4 * hk]
    wa_ba_t = jnp.pad(wa_in_t[4 * hk:], ((0, LANES - 2 * N_HEADS_A), (0, 0)))
    wa_out = full["a_w_out"][0]
    wb_in_t, wb_out = full["b_w_in"][0], full["b_w_out"][0]
    w_up_g_t = [full["f_w_up"][i][:D_FF] for i in range(2)]
    w_up_v_t = [full["f_w_up"][i][D_FF:] for i in range(2)]
    w_down = [full["f_w_down"][i] for i in range(2)]
    w_pp_t = [full["ple_w_proj"][i] for i in range(2)]
    w_pg = [full["ple_w_gate"][i] for i in range(2)]
    cv_a = full["a_conv"][0]
    cv_g = [full["f_conv"][i][:, :D_FF] for i in range(2)]
    cv_v = [full["f_conv"][i][:, D_FF:] for i in range(2)]
    alog_row = jnp.pad(a_log, ((0, 0), (N_HEADS_A, LANES - 2 * N_HEADS_A)))
    dtb_row = jnp.pad(a_dt_bias, ((0, 0), (N_HEADS_A, LANES - 2 * N_HEADS_A)))
    p_bf = [p[i, 0].astype(BF16) for i in range(2)]

    def ffn_ple_fwd(i, h_a):
        n_f = _rms_fwd(h_a, norm_ffn[i], name=f"l{i}_ffn_norm")
        up_g = _matmul(n_f, w_up_g_t[i], tb=True, name=f"l{i}_ffn_up_gate")
        up_v = _matmul(n_f, w_up_v_t[i], tb=True, name=f"l{i}_ffn_up_val")
        act = _ffn_act_fwd(up_g, up_v, cv_g[i], cv_v[i], name=f"l{i}_ffn_act")
        h_b = _matmul(act, w_down[i], res=h_a, name=f"l{i}_ffn_down")
        n_p = _rms_fwd(h_b, norm_ple[i], name=f"l{i}_ple_norm")
        zg = _matmul(n_p, w_pg[i], name=f"l{i}_ple_gate")
        pe = _matmul(p_bf[i], w_pp_t[i], tb=True, name=f"l{i}_ple_proj")
        h_c = _ple_fwd(h_b, zg, pe, name=f"l{i}_ple_mix")
        return h_c, dict(n_f=n_f, up_g=up_g, up_v=up_v, act=act, h_b=h_b, n_p=n_p, zg=zg, pe=pe)

    n0 = _rms_fwd(xs, norm_mix[0], name="l0_mix_norm")
    pqkv = _matmul(n0, wa_qkv_t, tb=True, name="l0_in_qkv")
    z0 = _matmul(n0, wa_z_t, tb=True, name="l0_in_z")
    pba = _matmul(n0, wa_ba_t, tb=True, name="l0_in_ba")
    q, k, v, gbc, bbc = _delta_pre_fwd(pqkv, pba, cv_a, alog_row, dtb_row, name="l0_delta_pre")
    *prep, tinv = _delta_prep_fwd(q, k, v, gbc, bbc, name="l0_delta_prep")
    o, states = _delta_scan_fwd(prep, name="l0_delta_scan")
    og = _gated_norm_fwd(o, z0, a_norm, name="l0_gated_norm")
    h1 = _matmul(og, wa_out, res=xs, name="l0_mix_out")
    h3, sv0 = ffn_ple_fwd(0, h1)

    n1 = _rms_fwd(h3, norm_mix[1], name="l1_mix_norm")
    pb = _matmul(n1, wb_in_t, tb=True, name="l1_in_qkv")
    att = _swa_fwd(pb, b_sinks, name="l1_swa")
    h4 = _matmul(att, wb_out, res=h3, name="l1_mix_out")
    h6, sv1 = ffn_ple_fwd(1, h4)

    loss_row, dh6, d_norm_final = _final_loss(h6, norm_final, tgt, name="final_loss")
    loss = lax.psum(loss_row[0, 0], MESH_AXES)

    gw = {}

    def ffn_ple_bwd(i, dh_c, h_a, sv):
        dzg, dpe = _ple_bwd(dh_c, sv["zg"], sv["pe"], name=f"l{i}_ple_mix_bwd")
        d_pg = _matmul(sv["n_p"], dzg, ta=True, name=f"l{i}_ple_gate_dw")
        d_pp_t = _matmul(dpe, p_bf[i], ta=True, name=f"l{i}_ple_proj_dw")
        dn_p = _matmul(dzg, w_pg[i], tb=True, name=f"l{i}_ple_gate_dx")
        dh_b, d_np = _rms_bwd(sv["h_b"], norm_ple[i], dn_p, dh_c, name=f"l{i}_ple_norm_bwd")
        dact = _matmul(dh_b, w_down[i], tb=True, name=f"l{i}_ffn_down_dx")
        d_down = _matmul(sv["act"], dh_b, ta=True, name=f"l{i}_ffn_down_dw")
        dup_g, dup_v, d_cg, d_cv = _ffn_act_bwd(sv["up_g"], sv["up_v"], cv_g[i], cv_v[i], dact, name=f"l{i}_ffn_act_bwd")
        d_up_g_t = _matmul(dup_g, sv["n_f"], ta=True, name=f"l{i}_ffn_up_gate_dw")
        d_up_v_t = _matmul(dup_v, sv["n_f"], ta=True, name=f"l{i}_ffn_up_val_dw")
        dn_f = _matmul(dup_g, w_up_g_t[i], name=f"l{i}_ffn_up_gate_dx")
        dn_f = _matmul(dup_v, w_up_v_t[i], res=dn_f, name=f"l{i}_ffn_up_val_dx")
        dh_a, d_nf = _rms_bwd(h_a, norm_ffn[i], dn_f, dh_b, name=f"l{i}_ffn_norm_bwd")
        g = dict(ple_w_gate=d_pg, ple_w_proj=d_pp_t, norm_ple=d_np, f_w_down=d_down,
                 f_w_up=jnp.concatenate([d_up_g_t, d_up_v_t], axis=0), f_conv=jnp.concatenate([d_cg, d_cv], axis=1),
                 norm_ffn=d_nf)
        return dh_a, g

    dh4, g1 = ffn_ple_bwd(1, dh6, h4, sv1)
    datt = _matmul(dh4, wb_out, tb=True, out_dtype=BF16, name="l1_mix_out_dx")
    gw["b_w_out"] = _matmul(att, dh4, ta=True, name="l1_mix_out_dw")
    dq_b, dk_b, dv_b, dsinks = _swa_bwd(pb, b_sinks, datt, name="l1_swa_bwd")
    dpb = jnp.concatenate([dq_b, dk_b.astype(BF16), dv_b.astype(BF16)], axis=1)
    gw["b_w_in"] = _matmul(dpb, n1, ta=True, name="l1_in_qkv_dw")
    dn1 = _matmul(dpb, wb_in_t, name="l1_in_qkv_dx")
    dh3, d_nm1 = _rms_bwd(h3, norm_mix[1], dn1, dh4, name="l1_mix_norm_bwd")

    dh1, g0 = ffn_ple_bwd(0, dh3, h1, sv0)
    dog = _matmul(dh1, wa_out, tb=True, name="l0_mix_out_dx")
    gw["a_w_out"] = _matmul(og, dh1, ta=True, name="l0_mix_out_dw")
    do, dz0, d_anorm = _gated_norm_bwd(o, z0, a_norm, dog, name="l0_gated_norm_bwd")
    cts = _delta_scan_bwd(prep, states, do, name="l0_delta_scan_bwd")
    dq, dk, dv, dgbc, dbbc = _delta_prep_bwd(q, k, v, gbc, bbc, tinv, cts, name="l0_delta_prep_bwd")
    dpqkv, dpba, d_aconv, d_alog, d_dtb = _delta_pre_bwd(pqkv, pba, cv_a, alog_row, dtb_row, dq, dk, dv, dgbc, dbbc,
                                                         name="l0_delta_pre_bwd")
    d_wqkv_t = _matmul(dpqkv, n0, ta=True, name="l0_in_qkv_dw")
    d_wz_t = _matmul(dz0, n0, ta=True, name="l0_in_z_dw")
    d_wba_t = _matmul(dpba, n0, ta=True, name="l0_in_ba_dw")
    gw["a_w_in"] = jnp.concatenate([d_wqkv_t, d_wz_t, d_wba_t[:2 * N_HEADS_A]], axis=0)
    dn0 = _matmul(dpqkv, wa_qkv_t, name="l0_in_qkv_dx")
    dn0 = _matmul(dz0, wa_z_t, res=dn0, name="l0_in_z_dx")
    dn0 = _matmul(dpba, wa_ba_t, res=dn0, name="l0_in_ba_dx")
    dx, d_nm0 = _rms_bwd(xs, norm_mix[0], dn0, dh1, name="l0_mix_norm_bwd")

    for n in ("f_w_up", "f_w_down", "ple_w_proj", "ple_w_gate", "f_conv"):
        gw[n] = jnp.stack([g0[n], g1[n]])
    for n in ("a_w_in", "a_w_out", "b_w_in", "b_w_out"):
        gw[n] = gw[n][None]
    gw["a_conv"] = d_aconv[None]
    small_g = dict(norm_mix=jnp.concatenate([d_nm0, d_nm1]), norm_ffn=jnp.concatenate([g0["norm_ffn"], g1["norm_ffn"]]),
                   norm_ple=jnp.concatenate([g0["norm_ple"], g1["norm_ple"]]), norm_final=d_norm_final[0],
                   a_log=d_alog[:, N_HEADS_A:2 * N_HEADS_A], a_dt_bias=d_dtb[:, N_HEADS_A:2 * N_HEADS_A],
                   a_norm=d_anorm, b_sinks=dsinks[:, :N_HEADS_B])

    recv_big = _all_to_all(_pack_split(gw, BIG, BF16, SLAB_ROW_MULTIPLE), name="exchange_matrix_grads")
    recv_conv = _all_to_all(_pack_split(gw, CONVS, F32, 8), name="exchange_conv_grads")
    recv_small = _all_gather(_pack([small_g[n] for n in SMALL], F32, 8), name="gather_small_grads")

    outs = {}
    for names, recv, mult, tag in ((BIG, recv_big, SLAB_ROW_MULTIPLE, "matrices"), (CONVS, recv_conv, 8, "convs"),
                                   (SMALL, recv_small, 8, "small")):
        shapes = [wts[n].shape for n in names]
        g_wire = _unpack(_sum_parts(recv, name=f"sum_{tag}"), [_wire_shape(n, s) for n, s in zip(names, shapes)])
        grads = [_wire(n, g) for n, g in zip(names, g_wire)]
        packed = [_pack([d[n] for n in names], F32, mult) for d in (wts, mom, var)]
        res = _adamw_packed(_pack(grads, F32, mult), *packed, name=f"adamw_{tag}")
        for n, g in zip(names, grads):
            outs[("grad", n)] = g
        for kind, slab in zip(("delta", "new_m", "new_v"), res):
            for n, arr in zip(names, _unpack(slab, shapes)):
                outs[(kind, n)] = arr

    result = [loss, dx[None]]
    for kind in ("grad", "delta", "new_m", "new_v"):
        result += [outs[(kind, n)] for n in WEIGHTS]
    return tuple(result)
```

```python
import functools
import math

import jax
import jax.numpy as jnp
from jax import lax
from jax.experimental import pallas as pl
from jax.experimental.pallas import tpu as pltpu

F32 = jnp.float32
BF16 = jnp.bfloat16

D_MODEL = 1024
N_HEADS_A = 8
HEAD_DIM_A = 128
CONV_A = 4
CHUNK = 128
N_HEADS_B = 16
N_KV_B = 4
GROUP_B = N_HEADS_B // N_KV_B
HEAD_DIM_B = 64
WINDOW = 128
D_FF = 2816
FFN_CONV = 3
PLE_DIM = 256
EPS = 1e-6
N_DEV = 8
HALO = 8
PROJ_A_REAL = 4 * N_HEADS_A * HEAD_DIM_A + 2 * N_HEADS_A
PROJ_A = 4 * N_HEADS_A * HEAD_DIM_A + 128
Z_COL_BLOCK = 3
BA_COL_BLOCK = 32

ADAM_LR = 0.001
ADAM_B1 = 0.9
ADAM_B2 = 0.999
ADAM_EPS = 1e-08
ADAM_WD = 0.01
ADAM_STEP = 10

LANES = 128
VMEM_LIMIT_BYTES = 56 * 1024 * 1024
NEG_BIG = -1e30

MESH_AXES = ("x", "y", "c")


def _params(sem=None):
    return pltpu.CompilerParams(dimension_semantics=sem, vmem_limit_bytes=VMEM_LIMIT_BYTES)


def _tile(n, target):
    best = None
    for t in range(LANES, min(n, target) + 1, LANES):
        if n % t == 0:
            best = t
    return best or n


def _sigmoid(x):
    return 1.0 / (1.0 + jnp.exp(-x))


def _softplus(x):
    return jnp.maximum(x, 0.0) + jnp.log1p(jnp.exp(-jnp.abs(x)))


def _matmul(a, b, *, name, ta=False, tb=False, res=None, out_dtype=F32, tm=1408, tn=1408, tk=None, dep=None):
    sa, sb = a.ndim == 3, b.ndim == 3
    ns = a.shape[0] if sa else (b.shape[0] if sb else 1)
    contract_stack = sa and sb
    out_stacked = sa != sb
    m = a.shape[-1] if ta else a.shape[-2]
    k = a.shape[-2] if ta else a.shape[-1]
    n = b.shape[-2] if tb else b.shape[-1]
    assert (b.shape[-1] if tb else b.shape[-2]) == k, (a.shape, b.shape, ta, tb)
    if tk is None:
        tk = 1024 if ta else 2816
    tm, tn, tk = _tile(m, tm), _tile(n, tn), _tile(k, tk)
    nk = k // tk
    nsteps = nk * (ns if contract_stack else 1)
    dims = (((0 if ta else 1,), (1 if tb else 0,)), ((), ()))

    def spec(block, stacked, order):
        def index(g, i, j, kk):
            two = order(i, j, kk % nk)
            if not stacked:
                return two
            return (kk // nk if contract_stack else g,) + two
        return pl.BlockSpec(((None,) if stacked else ()) + block, index)

    a_spec = spec((tk, tm), sa, lambda i, j, kq: (kq, i)) if ta else spec((tm, tk), sa, lambda i, j, kq: (i, kq))
    b_spec = spec((tn, tk), sb, lambda i, j, kq: (j, kq)) if tb else spec((tk, tn), sb, lambda i, j, kq: (kq, j))
    o_spec = spec((tm, tn), out_stacked, lambda i, j, kq: (i, j))
    has_res = res is not None
    has_dep = dep is not None

    def body(*refs):
        a_ref, b_ref = refs[0], refs[1]
        r_ref = refs[2] if has_res else None
        o_ref = refs[2 + has_res + has_dep]
        part = lax.dot_general(a_ref[...].astype(BF16), b_ref[...].astype(BF16), dims, preferred_element_type=F32)

        def finish(acc):
            if has_res:
                acc = acc + r_ref[...].astype(F32)
            o_ref[...] = acc.astype(out_dtype)

        if nsteps == 1:
            finish(part)
        else:
            acc_ref = refs[-1]
            kk = pl.program_id(3)

            @pl.when(kk == 0)
            def _():
                acc_ref[...] = part

            @pl.when(kk > 0)
            def _():
                acc_ref[...] += part

            @pl.when(kk == nsteps - 1)
            def _():
                finish(acc_ref[...])

    in_specs = [a_spec, b_spec] + ([o_spec] if has_res else []) + ([pl.BlockSpec(memory_space=pl.ANY)] if has_dep else [])
    args = (a, b) + ((res,) if has_res else ()) + ((dep,) if has_dep else ())
    return pl.pallas_call(
        body,
        name=name,
        grid=(ns if out_stacked else 1, m // tm, n // tn, nsteps),
        in_specs=in_specs,
        out_specs=o_spec,
        out_shape=jax.ShapeDtypeStruct(((ns,) if out_stacked else ()) + (m, n), out_dtype),
        scratch_shapes=[pltpu.VMEM((tm, tn), F32)] if nsteps > 1 else [],
        compiler_params=_params(("parallel", "parallel", "parallel", "arbitrary")),
    )(*args)


def _rms_fwd(h, w, *, name, tm=512):
    t, d = h.shape
    tm = _tile(t, tm)

    def body(h_ref, w_ref, o_ref):
        x = h_ref[...]
        r = lax.rsqrt(jnp.mean(x * x, axis=-1, keepdims=True) + EPS)
        o_ref[...] = (x * r * w_ref[...]).astype(BF16)

    return pl.pallas_call(
        body,
        name=name,
        grid=(t // tm,),
        in_specs=[pl.BlockSpec((tm, d), lambda i: (i, 0)), pl.BlockSpec((1, d), lambda i: (0, 0))],
        out_specs=pl.BlockSpec((tm, d), lambda i: (i, 0)),
        out_shape=jax.ShapeDtypeStruct((t, d), BF16),
        compiler_params=_params(("parallel",)),
    )(h, w.reshape(1, d))


def _rms_bwd(h, w, dn, skip, *, name, tm=512):
    t, d = h.shape
    tm = _tile(t, tm)

    def body(h_ref, w_ref, dn_ref, skip_ref, dh_ref, dw_ref):
        i = pl.program_id(0)
        x = h_ref[...]
        r = lax.rsqrt(jnp.mean(x * x, axis=-1, keepdims=True) + EPS)
        nh = x * r
        g = dn_ref[...].astype(F32)
        gw = g * w_ref[...]
        dh_ref[...] = r * (gw - nh * jnp.mean(gw * nh, axis=-1, keepdims=True)) + skip_ref[...]
        part = jnp.sum(g * nh, axis=0, keepdims=True)

        @pl.when(i == 0)
        def _():
            dw_ref[...] = part

        @pl.when(i > 0)
        def _():
            dw_ref[...] += part

    row = pl.BlockSpec((tm, d), lambda i: (i, 0))
    vec = pl.BlockSpec((1, d), lambda i: (0, 0))
    return pl.pallas_call(
        body,
        name=name,
        grid=(t // tm,),
        in_specs=[row, vec, row, row],
        out_specs=[row, vec],
        out_shape=[jax.ShapeDtypeStruct((t, d), F32), jax.ShapeDtypeStruct((1, d), F32)],
        compiler_params=_params(("arbitrary",)),
    )(h, w.reshape(1, d), dn, skip)


def _final_loss(h, w, target, *, name, tm=512):
    t, d = h.shape
    tm = _tile(t, tm)

    def body(h_ref, w_ref, tg_ref, loss_ref, dh_ref, dw_ref):
        i = pl.program_id(0)
        x = h_ref[...]
        r = lax.rsqrt(jnp.mean(x * x, axis=-1, keepdims=True) + EPS)
        nh = x * r
        err = nh * w_ref[...] - tg_ref[...]
        lpart = (0.5 / d) * jnp.sum(jnp.sum(err * err, axis=-1, keepdims=True), axis=0, keepdims=True)
        g = err * (1.0 / d)
        gw = g * w_ref[...]
        dh_ref[...] = r * (gw - nh * jnp.mean(gw * nh, axis=-1, keepdims=True))
        part = jnp.sum(g * nh, axis=0, keepdims=True)
        lrow = jnp.broadcast_to(lpart, (1, LANES))

        @pl.when(i == 0)
        def _():
            dw_ref[...] = part
            loss_ref[...] = lrow

        @pl.when(i > 0)
        def _():
            dw_ref[...] += part
            loss_ref[...] += lrow

    row = pl.BlockSpec((tm, d), lambda i: (i, 0))
    vec = pl.BlockSpec((1, d), lambda i: (0, 0))
    return pl.pallas_call(
        body,
        name=name,
        grid=(t // tm,),
        in_specs=[row, vec, row],
        out_specs=[pl.BlockSpec((1, LANES), lambda i: (0, 0)), row, vec],
        out_shape=[jax.ShapeDtypeStruct((1, LANES), F32), jax.ShapeDtypeStruct((t, d), F32), jax.ShapeDtypeStruct((1, d), F32)],
        compiler_params=_params(("arbitrary",)),
    )(h, w.reshape(1, d), target)


def _conv_from_ext(ext_ref, cw_ref, kw, tm):
    y = cw_ref[kw - 1:kw, :] * ext_ref[pl.ds(HALO, tm), :]
    for i in range(kw - 1):
        y = y + cw_ref[i:i + 1, :] * ext_ref[pl.ds(HALO - (kw - 1) + i, tm), :]
    return y


ROW_CHUNK = 64


def _conv_rows(src_ref, base, rows, cw_ref, kw):
    y = cw_ref[kw - 1:kw, :] * src_ref[pl.ds(base, rows), :]
    for i in range(kw - 1):
        y = y + cw_ref[i:i + 1, :] * src_ref[pl.ds(base - (kw - 1) + i, rows), :]
    return y


def _conv_t_rows(dy_ref, base, rows, cw_ref, kw):
    dx = cw_ref[kw - 1:kw, :] * dy_ref[pl.ds(base, rows), :]
    for i in range(kw - 1):
        dx = dx + cw_ref[i:i + 1, :] * dy_ref[pl.ds(base + kw - 1 - i, rows), :]
    return dx


def _conv_bwd_from_ext(xext_ref, dyext_ref, cw_ref, dcw_ref, kw, tm, first):
    dy = dyext_ref[pl.ds(0, tm), :]
    dx = cw_ref[kw - 1:kw, :] * dy
    for i in range(kw - 1):
        dx = dx + cw_ref[i:i + 1, :] * dyext_ref[pl.ds(kw - 1 - i, tm), :]
    for i in range(kw):
        part = jnp.sum(dy * xext_ref[pl.ds(HALO - (kw - 1) + i, tm), :], axis=0, keepdims=True)

        @pl.when(first)
        def _():
            dcw_ref[i:i + 1, :] = part

        @pl.when(jnp.logical_not(first))
        def _():
            dcw_ref[i:i + 1, :] += part

    return dx


def _delta_pre_fwd(proj, conv_w, alog_row, dtb_row, *, name, tm=256):
    t = proj.shape[0]
    c3 = 3 * N_HEADS_A * HEAD_DIM_A
    hk = N_HEADS_A * HEAD_DIM_A
    tm = _tile(t, tm)

    def body(x_ref, ba_ref, cw_ref, al_ref, db_ref, q_ref, k_ref, v_ref, g_ref, b_ref, ext):
        i = pl.program_id(0)

        @pl.when(i == 0)
        def _():
            ext[0:HALO, :] = jnp.zeros((HALO, c3), F32)

        ext[pl.ds(HALO, tm), :] = x_ref[...]
        y = _conv_from_ext(ext, cw_ref, CONV_A, tm)
        ext[0:HALO, :] = ext[pl.ds(tm, HALO), :]
        s = y * _sigmoid(y)
        for h in range(N_HEADS_A):
            lo = h * HEAD_DIM_A
            for dst, off in ((q_ref, 0), (k_ref, hk)):
                sh = s[:, off + lo:off + lo + HEAD_DIM_A]
                dst[:, lo:lo + HEAD_DIM_A] = sh * lax.rsqrt(jnp.sum(sh * sh, axis=-1, keepdims=True) + EPS)
        v_ref[...] = s[:, 2 * hk:3 * hk]
        ba = ba_ref[...]
        beta = _sigmoid(ba)
        gfull = -jnp.exp(al_ref[...]) * _softplus(ba + db_ref[...])
        for h in range(N_HEADS_A):
            lo = h * HEAD_DIM_A
            b_ref[:, lo:lo + HEAD_DIM_A] = jnp.broadcast_to(beta[:, h:h + 1], (tm, HEAD_DIM_A))
            g_ref[:, lo:lo + HEAD_DIM_A] = jnp.broadcast_to(gfull[:, N_HEADS_A + h:N_HEADS_A + h + 1], (tm, HEAD_DIM_A))

    row = lambda w: pl.BlockSpec((tm, w), lambda i: (i, 0))
    fixed = lambda r, w: pl.BlockSpec((r, w), lambda i: (0, 0))
    out = jax.ShapeDtypeStruct((t, hk), F32)
    return pl.pallas_call(
        body,
        name=name,
        grid=(t // tm,),
        in_specs=[row(c3), pl.BlockSpec((tm, LANES), lambda i: (i, BA_COL_BLOCK)), fixed(CONV_A, c3), fixed(1, LANES),
                  fixed(1, LANES)],
        out_specs=[row(hk)] * 5,
        out_shape=[out] * 5,
        scratch_shapes=[pltpu.VMEM((HALO + tm, c3), F32)],
        compiler_params=_params(("arbitrary",)),
    )(proj, proj, conv_w, alog_row, dtb_row)


def _delta_pre_bwd(proj, conv_w, alog_row, dtb_row, dq, dk, dv, dg, db, dz, *, name, tm=256):
    t, pw = proj.shape
    c3 = 3 * N_HEADS_A * HEAD_DIM_A
    hk = N_HEADS_A * HEAD_DIM_A
    tm = _tile(t, tm)
    nt = t // tm
    hb = tm // HALO

    def body(x_ref, xp_ref, ba_ref, cw_ref, al_ref, db_ref, dq_ref, dk_ref, dv_ref, dg_ref, dbt_ref, dz_ref,
             dp_ref, dcw_ref, dal_ref, ddb_ref, xext, dyext, carry):
        i = pl.program_id(0)
        first = i == 0
        tile = nt - 1 - i

        @pl.when(tile == 0)
        def _():
            xext[0:HALO, :] = jnp.zeros((HALO, c3), F32)

        @pl.when(tile > 0)
        def _():
            xext[0:HALO, :] = xp_ref[...]

        xext[pl.ds(HALO, tm), :] = x_ref[...]
        y = _conv_from_ext(xext, cw_ref, CONV_A, tm)
        sg = _sigmoid(y)
        s = y * sg
        dsilu = sg * (1.0 + y * (1.0 - sg))
        for h in range(N_HEADS_A):
            lo = h * HEAD_DIM_A
            for src, off in ((dq_ref, 0), (dk_ref, hk)):
                sh = s[:, off + lo:off + lo + HEAD_DIM_A]
                r = lax.rsqrt(jnp.sum(sh * sh, axis=-1, keepdims=True) + EPS)
                qn = sh * r
                gq = src[:, lo:lo + HEAD_DIM_A]
                dsh = r * (gq - qn * jnp.sum(gq * qn, axis=-1, keepdims=True))
                dyext[pl.ds(0, tm), off + lo:off + lo + HEAD_DIM_A] = dsh * dsilu[:, off + lo:off + lo + HEAD_DIM_A]
        dyext[pl.ds(0, tm), 2 * hk:3 * hk] = dv_ref[...] * dsilu[:, 2 * hk:3 * hk]

        @pl.when(first)
        def _():
            dyext[pl.ds(tm, HALO), :] = jnp.zeros((HALO, c3), F32)

        @pl.when(jnp.logical_not(first))
        def _():
            dyext[pl.ds(tm, HALO), :] = carry[...]

        dx = _conv_bwd_from_ext(xext, dyext, cw_ref, dcw_ref, CONV_A, tm, first)
        carry[...] = dyext[0:HALO, :]
        dp_ref[:, 0:c3] = dx.astype(BF16)
        dp_ref[:, c3:c3 + hk] = dz_ref[...]

        lane = lax.broadcasted_iota(jnp.int32, (tm, LANES), 1)
        gcol = jnp.zeros((tm, LANES), F32)
        for h in range(N_HEADS_A):
            lo = h * HEAD_DIM_A
            dbh = jnp.sum(dbt_ref[:, lo:lo + HEAD_DIM_A], axis=-1, keepdims=True)
            dgh = jnp.sum(dg_ref[:, lo:lo + HEAD_DIM_A], axis=-1, keepdims=True)
            gcol = gcol + jnp.where(lane == h, dbh, 0.0) + jnp.where(lane == N_HEADS_A + h, dgh, 0.0)
        ba = ba_ref[...]
        beta = _sigmoid(ba)
        a_neg = -jnp.exp(al_ref[...])
        z = ba + db_ref[...]
        dz = gcol * a_neg * _sigmoid(z)
        is_g = jnp.logical_and(lane >= N_HEADS_A, lane < 2 * N_HEADS_A)
        dba = jnp.where(lane < N_HEADS_A, gcol * beta * (1.0 - beta), jnp.where(is_g, dz, 0.0))
        dp_ref[:, c3 + hk:pw] = dba.astype(BF16)
        dal = jnp.sum(jnp.where(is_g, gcol * a_neg * _softplus(z), 0.0), axis=0, keepdims=True)
        ddb = jnp.sum(jnp.where(is_g, dz, 0.0), axis=0, keepdims=True)

        @pl.when(first)
        def _():
            dal_ref[...] = dal
            ddb_ref[...] = ddb

        @pl.when(jnp.logical_not(first))
        def _():
            dal_ref[...] += dal
            ddb_ref[...] += ddb

    rev = lambda w: pl.BlockSpec((tm, w), lambda i: (nt - 1 - i, 0))
    prev = pl.BlockSpec((HALO, c3), lambda i: (jnp.maximum((nt - 1 - i) * hb - 1, 0), 0))
    fixed = lambda r, w: pl.BlockSpec((r, w), lambda i: (0, 0))
    return pl.pallas_call(
        body,
        name=name,
        grid=(nt,),
        in_specs=[rev(c3), prev, pl.BlockSpec((tm, LANES), lambda i: (nt - 1 - i, BA_COL_BLOCK)), fixed(CONV_A, c3),
                  fixed(1, LANES), fixed(1, LANES)] + [rev(hk)] * 6,
        out_specs=[rev(pw), fixed(CONV_A, c3), fixed(1, LANES), fixed(1, LANES)],
        out_shape=[jax.ShapeDtypeStruct((t, pw), BF16), jax.ShapeDtypeStruct((CONV_A, c3), F32),
                   jax.ShapeDtypeStruct((1, LANES), F32), jax.ShapeDtypeStruct((1, LANES), F32)],
        scratch_shapes=[pltpu.VMEM((HALO + tm, c3), F32), pltpu.VMEM((tm + HALO, c3), F32), pltpu.VMEM((HALO, c3), F32)],
        compiler_params=_params(("arbitrary",)),
    )(proj, proj, proj, conv_w, alog_row, dtb_row, dq, dk, dv, dg, db, dz)


def _gated_norm_fwd(o, proj, w, *, name, tm=512):
    t, d = o.shape
    tm = _tile(t, tm)

    def body(o_ref, z_ref, w_ref, y_ref):
        for h in range(N_HEADS_A):
            sl = slice(h * HEAD_DIM_A, (h + 1) * HEAD_DIM_A)
            oh = o_ref[:, sl]
            zh = z_ref[:, sl]
            r = lax.rsqrt(jnp.mean(oh * oh, axis=-1, keepdims=True) + EPS)
            y_ref[:, sl] = (oh * r * w_ref[...] * (zh * _sigmoid(zh))).astype(BF16)

    row = pl.BlockSpec((tm, d), lambda i: (i, 0))
    return pl.pallas_call(
        body,
        name=name,
        grid=(t // tm,),
        in_specs=[row, pl.BlockSpec((tm, d), lambda i: (i, Z_COL_BLOCK)), pl.BlockSpec((1, HEAD_DIM_A), lambda i: (0, 0))],
        out_specs=row,
        out_shape=jax.ShapeDtypeStruct((t, d), BF16),
        compiler_params=_params(("parallel",)),
    )(o, proj, w)


def _gated_norm_bwd(o, proj, w, dy, *, name, tm=512):
    t, d = o.shape
    tm = _tile(t, tm)

    def body(o_ref, z_ref, w_ref, dy_ref, do_ref, dz_ref, dw_ref):
        i = pl.program_id(0)
        dw = jnp.zeros((1, HEAD_DIM_A), F32)
        for h in range(N_HEADS_A):
            sl = slice(h * HEAD_DIM_A, (h + 1) * HEAD_DIM_A)
            oh = o_ref[:, sl]
            zh = z_ref[:, sl]
            g = dy_ref[:, sl]
            r = lax.rsqrt(jnp.mean(oh * oh, axis=-1, keepdims=True) + EPS)
            nh = oh * r
            sg = _sigmoid(zh)
            dz_ref[:, sl] = (g * nh * w_ref[...] * (sg * (1.0 + zh * (1.0 - sg)))).astype(BF16)
            dt = g * (zh * sg)
            dw = dw + jnp.sum(dt * nh, axis=0, keepdims=True)
            dnh = dt * w_ref[...]
            do_ref[:, sl] = r * (dnh - nh * jnp.mean(dnh * nh, axis=-1, keepdims=True))

        @pl.when(i == 0)
        def _():
            dw_ref[...] = dw

        @pl.when(i > 0)
        def _():
            dw_ref[...] += dw

    row = pl.BlockSpec((tm, d), lambda i: (i, 0))
    vec = pl.BlockSpec((1, HEAD_DIM_A), lambda i: (0, 0))
    return pl.pallas_call(
        body,
        name=name,
        grid=(t // tm,),
        in_specs=[row, pl.BlockSpec((tm, d), lambda i: (i, Z_COL_BLOCK)), vec, row],
        out_specs=[row, row, vec],
        out_shape=[jax.ShapeDtypeStruct((t, d), F32), jax.ShapeDtypeStruct((t, d), BF16),
                   jax.ShapeDtypeStruct((1, HEAD_DIM_A), F32)],
        compiler_params=_params(("arbitrary",)),
    )(o, proj, w, dy)


_NN = (((1,), (0,)), ((), ()))
_NT = (((1,), (1,)), ((), ()))
_TN = (((0,), (0,)), ((), ()))
_DIMS = {"nn": _NN, "nt": _NT, "tn": _TN}


def _raw_dot(a, b, kind, prec):
    dims = _DIMS[kind]
    a_hi, b_hi = a.astype(BF16), b.astype(BF16)
    out = lax.dot_general(a_hi, b_hi, dims, preferred_element_type=F32)
    if prec == "x3":
        a_lo = (a - a_hi.astype(F32)).astype(BF16)
        b_lo = (b - b_hi.astype(F32)).astype(BF16)
        out = out + lax.dot_general(a_hi, b_lo, dims, preferred_element_type=F32)
        out = out + lax.dot_general(a_lo, b_hi, dims, preferred_element_type=F32)
    elif prec == "s3":
        r1 = b - b_hi.astype(F32)
        b_mid = r1.astype(BF16)
        b_lo = (r1 - b_mid.astype(F32)).astype(BF16)
        out = out + lax.dot_general(a_hi, b_mid, dims, preferred_element_type=F32)
        out = out + lax.dot_general(a_hi, b_lo, dims, preferred_element_type=F32)
    return out


def _raw_dots(xs, ys, kind, prec):
    return [_raw_dot(x, y, kind, prec) for x, y in zip(xs, ys)]


@functools.partial(jax.custom_vjp, nondiff_argnums=(2, 3))
def _dots(xs, ys, kind, prec):
    return _raw_dots(xs, ys, kind, prec)


def _dots_fwd(xs, ys, kind, prec):
    return _raw_dots(xs, ys, kind, prec), (xs, ys)


def _dots_bwd(kind, prec, saved, gs):
    xs, ys = saved
    if kind == "nn":
        return _raw_dots(gs, ys, "nt", prec), _raw_dots(xs, gs, "tn", prec)
    if kind == "nt":
        return _raw_dots(gs, ys, "nn", prec), _raw_dots(gs, xs, "tn", prec)
    return _raw_dots(ys, gs, "nt", prec), _raw_dots(xs, gs, "nn", prec)


_dots.defvjp(_dots_fwd, _dots_bwd)


def _eye(c):
    return (lax.broadcasted_iota(jnp.int32, (c, c), 0) == lax.broadcasted_iota(jnp.int32, (c, c), 1)).astype(F32)


def _inv_unit_lower_raw(lmats):
    c = lmats[0].shape[0]
    eye = _eye(c)
    xs = [eye - l for l in lmats]
    ps = lmats
    for _ in range(int(math.log2(c)) - 1):
        ps = _raw_dots(ps, ps, "nn", "bf16")
        xs = [x + d for x, d in zip(xs, _raw_dots(xs, ps, "nn", "bf16"))]
    rs = [x - eye + d for x, d in zip(xs, _raw_dots(lmats, xs, "nn", "x3"))]
    return [x - d for x, d in zip(xs, _raw_dots(xs, rs, "nn", "bf16"))]


@jax.custom_vjp
def _inv_unit_lower(lmats, hints):
    return _inv_unit_lower_raw(lmats) if hints is None else hints


def _inv_fwd(lmats, hints):
    tms = _inv_unit_lower_raw(lmats) if hints is None else hints
    return tms, (tms, hints)


def _inv_bwd(saved, gs):
    tms, hints = saved
    ds = [-d for d in _raw_dots(_raw_dots(tms, gs, "tn", "x3"), tms, "nt", "x3")]
    return ds, (None if hints is None else [jnp.zeros_like(h) for h in hints])


_inv_unit_lower.defvjp(_inv_fwd, _inv_bwd)


def _delta_prep(qs, ks, vs, gs, bs, hints=None):
    c = qs[0].shape[0]
    nh = len(qs)
    ii = lax.broadcasted_iota(jnp.int32, (c, c), 0)
    jj = lax.broadcasted_iota(jnp.int32, (c, c), 1)
    incl = ii >= jj
    strict = ii > jj
    ltri = incl.astype(F32)
    eye = _eye(c)
    m1 = _dots([ltri] * nh, gs, "nn", "s3")
    gtot = [jnp.sum(g, axis=0, keepdims=True) for g in gs]
    decay = [jnp.exp(jnp.where(incl, m - m.T, NEG_BIG)) for m in m1]
    eg = [jnp.exp(m) for m in m1]
    kk = _dots(ks, ks, "nt", "bf16")
    lmats = [jnp.where(strict, b * x * d, 0.0) for b, x, d in zip(bs, kk, decay)]
    tinv = _inv_unit_lower(lmats, hints)
    toff = [t - eye for t in tinv]
    bv = [b * v for b, v in zip(bs, vs)]
    bk = [b * e * k for b, e, k in zip(bs, eg, ks)]
    u0 = [x + d for x, d in zip(bv, _dots(toff, bv, "nn", "bf16"))]
    wk = [x + d for x, d in zip(bk, _dots(toff, bk, "nn", "bf16"))]
    qsc = [q * (HEAD_DIM_A ** -0.5) for q in qs]
    qk = [x * d for x, d in zip(_dots(qsc, ks, "nt", "bf16"), decay)]
    q_dec = [q * e for q, e in zip(qsc, eg)]
    k_dec = [k * jnp.exp(t - m) for k, t, m in zip(ks, gtot, m1)]
    glast = [jnp.broadcast_to(jnp.exp(t), (c, c)) for t in gtot]
    return (u0, wk, qk, q_dec, k_dec, glast), tinv


def _delta_step(ss, u0, wk, qk, q_dec, k_dec, glast):
    us = [a - d for a, d in zip(u0, _dots(wk, ss, "nn", "bf16"))]
    os_ = [a + d for a, d in zip(_dots(q_dec, ss, "nn", "bf16"), _dots(qk, us, "nn", "bf16"))]
    s_new = [g * s + d for g, s, d in zip(glast, ss, _dots(k_dec, us, "tn", "bf16"))]
    return os_, s_new


HEADS_PER_STEP = 8


def _chunk_spec(nc, reverse=False):
    w = HEADS_PER_STEP * HEAD_DIM_A
    if reverse:
        return pl.BlockSpec((CHUNK, w), lambda h, n: (nc - 1 - n, h))
    return pl.BlockSpec((CHUNK, w), lambda h, n: (n, h))


def _head_slices():
    return [slice(j * HEAD_DIM_A, (j + 1) * HEAD_DIM_A) for j in range(HEADS_PER_STEP)]


def _heads(ref):
    return [ref[:, sl] for sl in _head_slices()]


def _delta_prep_fwd(q, k, v, gbc, bbc, *, name):
    t, d = q.shape
    nc = t // CHUNK

    def body(q_ref, k_ref, v_ref, g_ref, b_ref, *outs):
        res, tinv = _delta_prep(*[_heads(r) for r in (q_ref, k_ref, v_ref, g_ref, b_ref)])
        for ref, vals in zip(outs, res + (tinv,)):
            for sl, val in zip(_head_slices(), vals):
                ref[:, sl] = val

    spec = _chunk_spec(nc)
    return pl.pallas_call(
        body,
        name=name,
        grid=(N_HEADS_A // HEADS_PER_STEP, nc),
        in_specs=[spec] * 5,
        out_specs=[spec] * 7,
        out_shape=[jax.ShapeDtypeStruct((t, d), F32)] * 7,
        compiler_params=_params(("parallel", "parallel")),
    )(q, k, v, gbc, bbc)


def _delta_prep_bwd(q, k, v, gbc, bbc, tinv, cts, *, name):
    t, d = q.shape
    nc = t // CHUNK

    def body(q_ref, k_ref, v_ref, g_ref, b_ref, t_ref, c0, c1, c2, c3, c4, c5, *outs):
        def f(q_, k_, v_, g_, b_):
            return _delta_prep(q_, k_, v_, g_, b_, hints=_heads(t_ref))[0]

        _, vjp = jax.vjp(f, *[_heads(r) for r in (q_ref, k_ref, v_ref, g_ref, b_ref)])
        grads = vjp(tuple(_heads(c) for c in (c0, c1, c2, c3, c4, c5)))
        for ref, vals in zip(outs, grads):
            for sl, val in zip(_head_slices(), vals):
                ref[:, sl] = val

    spec = _chunk_spec(nc)
    return pl.pallas_call(
        body,
        name=name,
        grid=(N_HEADS_A // HEADS_PER_STEP, nc),
        in_specs=[spec] * 12,
        out_specs=[spec] * 5,
        out_shape=[jax.ShapeDtypeStruct((t, d), F32)] * 5,
        compiler_params=_params(("parallel", "parallel")),
    )(q, k, v, gbc, bbc, tinv, *cts)


def _delta_scan_fwd(prep, *, name):
    t, d = prep[0].shape
    nc = t // CHUNK

    def body(u0, wk, qk, qd, kd, gl, o_ref, st_ref, s_ref):
        n = pl.program_id(1)

        @pl.when(n == 0)
        def _():
            s_ref[...] = jnp.zeros(s_ref.shape, F32)

        ss = [s_ref[j] for j in range(HEADS_PER_STEP)]
        os_, s_new = _delta_step(ss, *[_heads(r) for r in (u0, wk, qk, qd, kd, gl)])
        for j, sl in enumerate(_head_slices()):
            st_ref[:, sl] = ss[j]
            o_ref[:, sl] = os_[j]
            s_ref[j] = s_new[j]

    spec = _chunk_spec(nc)
    return pl.pallas_call(
        body,
        name=name,
        grid=(N_HEADS_A // HEADS_PER_STEP, nc),
        in_specs=[spec] * 6,
        out_specs=[spec] * 2,
        out_shape=[jax.ShapeDtypeStruct((t, d), F32)] * 2,
        scratch_shapes=[pltpu.VMEM((HEADS_PER_STEP, HEAD_DIM_A, HEAD_DIM_A), F32)],
        compiler_params=_params(("parallel", "arbitrary")),
    )(*prep)


def _delta_scan_bwd(prep, states, do, *, name):
    t, d = do.shape
    nc = t // CHUNK

    def body(u0, wk, qk, qd, kd, gl, st_ref, do_ref, *rest):
        outs, ds_ref = rest[:6], rest[6]
        n = pl.program_id(1)

        @pl.when(n == 0)
        def _():
            ds_ref[...] = jnp.zeros(ds_ref.shape, F32)

        _, vjp = jax.vjp(_delta_step, *[_heads(r) for r in (st_ref, u0, wk, qk, qd, kd, gl)])
        grads = vjp((_heads(do_ref), [ds_ref[j] for j in range(HEADS_PER_STEP)]))
        for j, sl in enumerate(_head_slices()):
            ds_ref[j] = grads[0][j]
            for ref, vals in zip(outs, grads[1:]):
                ref[:, sl] = vals[j]

    spec = _chunk_spec(nc, reverse=True)
    return pl.pallas_call(
        body,
        name=name,
        grid=(N_HEADS_A // HEADS_PER_STEP, nc),
        in_specs=[spec] * 8,
        out_specs=[spec] * 6,
        out_shape=[jax.ShapeDtypeStruct((t, d), F32)] * 6,
        scratch_shapes=[pltpu.VMEM((HEADS_PER_STEP, HEAD_DIM_A, HEAD_DIM_A), F32)],
        compiler_params=_params(("parallel", "arbitrary")),
    )(*prep, states, do)


def _alibi_slope(h):
    return 2.0 ** (-8.0 * (h + 1) / N_HEADS_B)


def _swa_probs(qh, kcat, slope, sink, blk):
    s = lax.dot_general(qh.astype(BF16), kcat.astype(BF16), _NT, preferred_element_type=F32) * (HEAD_DIM_B ** -0.5)
    qi = lax.broadcasted_iota(jnp.int32, (WINDOW, 2 * WINDOW), 0)
    kj = lax.broadcasted_iota(jnp.int32, (WINDOW, 2 * WINDOW), 1)
    dist = qi + WINDOW - kj
    valid = (dist >= 0) & (dist < WINDOW) & (blk * WINDOW - WINDOW + kj >= 0)
    logits = jnp.where(valid, s - slope * dist.astype(F32), NEG_BIG)
    m = jnp.maximum(jnp.max(logits, axis=-1, keepdims=True), sink)
    e = jnp.exp(logits - m)
    es = jnp.exp(sink - m)
    inv = 1.0 / (jnp.sum(e, axis=-1, keepdims=True) + es)
    return e * inv, es * inv


def _swa_fwd(proj, sinks, *, name):
    t = proj.shape[0]
    nb = t // WINDOW
    qd = N_HEADS_B * HEAD_DIM_B
    kd = N_KV_B * HEAD_DIM_B

    def body(sink_ref, q_ref, kp_ref, kc_ref, vp_ref, vc_ref, o_ref):
        blk = pl.program_id(0)
        for hk in range(N_KV_B):
            ks = slice(hk * HEAD_DIM_B, (hk + 1) * HEAD_DIM_B)
            kcat = jnp.concatenate([kp_ref[:, ks], kc_ref[:, ks]], axis=0)
            vcat = jnp.concatenate([vp_ref[:, ks], vc_ref[:, ks]], axis=0).astype(BF16)
            for g in range(GROUP_B):
                h = hk * GROUP_B + g
                hs = slice(h * HEAD_DIM_B, (h + 1) * HEAD_DIM_B)
                p, _ = _swa_probs(q_ref[:, hs], kcat, _alibi_slope(h), sink_ref[0, h], blk)
                o_ref[:, hs] = jnp.dot(p.astype(BF16), vcat, preferred_element_type=F32).astype(BF16)

    q_spec = pl.BlockSpec((WINDOW, qd), lambda i: (i, 0))
    kv = lambda col, prev: pl.BlockSpec((WINDOW, kd), (lambda i: (jnp.maximum(i - 1, 0), col)) if prev else (lambda i: (i, col)))
    kcol, vcol = qd // kd, qd // kd + 1
    return pl.pallas_call(
        body,
        name=name,
        grid=(nb,),
        in_specs=[pl.BlockSpec(memory_space=pltpu.SMEM), q_spec, kv(kcol, True), kv(kcol, False), kv(vcol, True), kv(vcol, False)],
        out_specs=q_spec,
        out_shape=jax.ShapeDtypeStruct((t, qd), BF16),
        compiler_params=_params(("parallel",)),
    )(sinks, proj, proj, proj, proj, proj)


def _swa_bwd(proj, sinks, dout, *, name):
    t = proj.shape[0]
    nb = t // WINDOW
    qd = N_HEADS_B * HEAD_DIM_B
    kd = N_KV_B * HEAD_DIM_B
    scale = HEAD_DIM_B ** -0.5

    def body(sink_ref, q_ref, kp_ref, kc_ref, vp_ref, vc_ref, do_ref, dq_ref, dk_ref, dv_ref, dsk_ref):
        blk = pl.program_id(0)
        lane = lax.broadcasted_iota(jnp.int32, (1, LANES), 1)

        @pl.when(blk == 0)
        def _():
            dk_ref[...] = jnp.zeros((t, kd), F32)
            dv_ref[...] = jnp.zeros((t, kd), F32)
            dsk_ref[...] = jnp.zeros((1, LANES), F32)

        cur = pl.ds(pl.multiple_of(blk * WINDOW, WINDOW), WINDOW)
        prv = pl.ds(pl.multiple_of(jnp.maximum(blk - 1, 0) * WINDOW, WINDOW), WINDOW)
        dsk = jnp.zeros((1, LANES), F32)
        for hk in range(N_KV_B):
            ks = slice(hk * HEAD_DIM_B, (hk + 1) * HEAD_DIM_B)
            kcat = jnp.concatenate([kp_ref[:, ks], kc_ref[:, ks]], axis=0)
            vcat = jnp.concatenate([vp_ref[:, ks], vc_ref[:, ks]], axis=0).astype(BF16)
            dkc = jnp.zeros((2 * WINDOW, HEAD_DIM_B), F32)
            dvc = jnp.zeros((2 * WINDOW, HEAD_DIM_B), F32)
            for g in range(GROUP_B):
                h = hk * GROUP_B + g
                hs = slice(h * HEAD_DIM_B, (h + 1) * HEAD_DIM_B)
                qh = q_ref[:, hs]
                p, psink = _swa_probs(qh, kcat, _alibi_slope(h), sink_ref[0, h], blk)
                doh = do_ref[:, hs].astype(BF16)
                dp = lax.dot_general(doh, vcat, _NT, preferred_element_type=F32)
                delta = jnp.sum(p * dp, axis=-1, keepdims=True)
                ds = (p * (dp - delta) * scale).astype(BF16)
                dsk = dsk + jnp.where(lane == h, jnp.sum(-psink * delta, axis=0, keepdims=True), 0.0)
                dq_ref[:, hs] = jnp.dot(ds, kcat.astype(BF16), preferred_element_type=F32).astype(BF16)
                dkc = dkc + lax.dot_general(ds, qh.astype(BF16), _TN, preferred_element_type=F32)
                dvc = dvc + lax.dot_general(p.astype(BF16), doh, _TN, preferred_element_type=F32)
            dk_ref[cur, ks] += dkc[WINDOW:, :]
            dv_ref[cur, ks] += dvc[WINDOW:, :]

            @pl.when(blk > 0)
            def _():
                dk_ref[prv, ks] += dkc[:WINDOW, :]
                dv_ref[prv, ks] += dvc[:WINDOW, :]

        dsk_ref[...] += dsk

    q_spec = pl.BlockSpec((WINDOW, qd), lambda i: (i, 0))
    kv = lambda col, prev: pl.BlockSpec((WINDOW, kd), (lambda i: (jnp.maximum(i - 1, 0), col)) if prev else (lambda i: (i, col)))
    kcol, vcol = qd // kd, qd // kd + 1
    full = pl.BlockSpec((t, kd), lambda i: (0, 0))
    return pl.pallas_call(
        body,
        name=name,
        grid=(nb,),
        in_specs=[pl.BlockSpec(memory_space=pltpu.SMEM), q_spec, kv(kcol, True), kv(kcol, False), kv(vcol, True), kv(vcol, False), q_spec],
        out_specs=[q_spec, full, full, pl.BlockSpec((1, LANES), lambda i: (0, 0))],
        out_shape=[jax.ShapeDtypeStruct((t, qd), BF16), jax.ShapeDtypeStruct((t, kd), F32),
                   jax.ShapeDtypeStruct((t, kd), F32), jax.ShapeDtypeStruct((1, LANES), F32)],
        compiler_params=_params(("arbitrary",)),
    )(sinks, proj, proj, proj, proj, proj, dout)


def _ffn_act_fwd(up, cw, *, name, tm=512, cb=256):
    _, t, f = up.shape
    tm, cb = _tile(t, tm), _tile(f, cb)

    rc = min(ROW_CHUNK, tm)

    def body(ug_ref, uv_ref, cg_ref, cv_ref, a_ref, hg, hv):
        i = pl.program_id(1)

        @pl.when(i == 0)
        def _():
            hg[0:HALO, :] = jnp.zeros((HALO, cb), F32)
            hv[0:HALO, :] = jnp.zeros((HALO, cb), F32)

        hg[pl.ds(HALO, rc), :] = ug_ref[0:rc, :]
        hv[pl.ds(HALO, rc), :] = uv_ref[0:rc, :]
        for r in range(tm // rc):
            if r == 0:
                yg = _conv_rows(hg, HALO, rc, cg_ref, FFN_CONV)
                yv = _conv_rows(hv, HALO, rc, cv_ref, FFN_CONV)
            else:
                yg = _conv_rows(ug_ref, r * rc, rc, cg_ref, FFN_CONV)
                yv = _conv_rows(uv_ref, r * rc, rc, cv_ref, FFN_CONV)
            a_ref[r * rc:(r + 1) * rc, :] = (yg * _sigmoid(yg) * yv).astype(BF16)
        hg[0:HALO, :] = ug_ref[tm - HALO:tm, :]
        hv[0:HALO, :] = uv_ref[tm - HALO:tm, :]

    ncb = f // cb
    half = lambda s: pl.BlockSpec((None, tm, cb), lambda c, i: (s, i, c))
    taps = lambda s: pl.BlockSpec((FFN_CONV, cb), lambda c, i: (0, c + s * ncb))
    return pl.pallas_call(
        body,
        name=name,
        grid=(ncb, t // tm),
        in_specs=[half(0), half(1), taps(0), taps(1)],
        out_specs=pl.BlockSpec((tm, cb), lambda c, i: (i, c)),
        out_shape=jax.ShapeDtypeStruct((t, f), BF16),
        scratch_shapes=[pltpu.VMEM((HALO + rc, cb), F32)] * 2,
        compiler_params=_params(("parallel", "arbitrary")),
    )(up, up, cw, cw)


def _ffn_act_bwd(up, cw, dact, *, name, tm=512, cb=256):
    _, t, f = up.shape
    tm, cb = _tile(t, tm), _tile(f, cb)
    nt = t // tm
    hb = tm // HALO

    rc = min(ROW_CHUNK, tm)
    nr = tm // rc
    kw = FFN_CONV

    def body(ug_ref, uv_ref, pg_ref, pv_ref, cg_ref, cv_ref, da_ref, du_ref, dcg_ref, dcv_ref,
             hg, hv, dyg, dyv):
        i = pl.program_id(1)
        first = i == 0
        tile = nt - 1 - i

        @pl.when(tile == 0)
        def _():
            hg[0:HALO, :] = jnp.zeros((HALO, cb), F32)
            hv[0:HALO, :] = jnp.zeros((HALO, cb), F32)

        @pl.when(tile > 0)
        def _():
            hg[0:HALO, :] = pg_ref[...]
            hv[0:HALO, :] = pv_ref[...]

        @pl.when(first)
        def _():
            dyg[pl.ds(tm, HALO), :] = jnp.zeros((HALO, cb), F32)
            dyv[pl.ds(tm, HALO), :] = jnp.zeros((HALO, cb), F32)

        hg[pl.ds(HALO, rc), :] = ug_ref[0:rc, :]
        hv[pl.ds(HALO, rc), :] = uv_ref[0:rc, :]
        dcg = [jnp.zeros((1, cb), F32) for _ in range(kw)]
        dcv = [jnp.zeros((1, cb), F32) for _ in range(kw)]
        for r in reversed(range(nr)):
            rows = slice(r * rc, (r + 1) * rc)
            src_g, src_v, base = (hg, hv, HALO) if r == 0 else (ug_ref, uv_ref, r * rc)
            yg = _conv_rows(src_g, base, rc, cg_ref, kw)
            yv = _conv_rows(src_v, base, rc, cv_ref, kw)
            sg = _sigmoid(yg)
            da = da_ref[rows, :]
            dy_g = da * yv * (sg * (1.0 + yg * (1.0 - sg)))
            dy_v = da * (yg * sg)
            dyg[rows, :] = dy_g
            dyv[rows, :] = dy_v
            du_ref[0, rows, :] = _conv_t_rows(dyg, r * rc, rc, cg_ref, kw).astype(BF16)
            du_ref[1, rows, :] = _conv_t_rows(dyv, r * rc, rc, cv_ref, kw).astype(BF16)
            for j in range(kw):
                dcg[j] = dcg[j] + jnp.sum(dy_g * src_g[pl.ds(base - (kw - 1) + j, rc), :], axis=0, keepdims=True)
                dcv[j] = dcv[j] + jnp.sum(dy_v * src_v[pl.ds(base - (kw - 1) + j, rc), :], axis=0, keepdims=True)
        dyg[pl.ds(tm, HALO), :] = dyg[0:HALO, :]
        dyv[pl.ds(tm, HALO), :] = dyv[0:HALO, :]
        for j in range(kw):
            @pl.when(first)
            def _():
                dcg_ref[j:j + 1, :] = dcg[j]
                dcv_ref[j:j + 1, :] = dcv[j]

            @pl.when(jnp.logical_not(first))
            def _():
                dcg_ref[j:j + 1, :] += dcg[j]
                dcv_ref[j:j + 1, :] += dcv[j]

    ncb = f // cb
    half = lambda s: pl.BlockSpec((None, tm, cb), lambda c, i: (s, nt - 1 - i, c))
    prev = lambda s: pl.BlockSpec((None, HALO, cb), lambda c, i: (s, jnp.maximum((nt - 1 - i) * hb - 1, 0), c))
    taps = lambda s: pl.BlockSpec((FFN_CONV, cb), lambda c, i: (0, c + s * ncb))
    dtaps = pl.BlockSpec((FFN_CONV, cb), lambda c, i: (0, c))
    return pl.pallas_call(
        body,
        name=name,
        grid=(ncb, nt),
        in_specs=[half(0), half(1), prev(0), prev(1), taps(0), taps(1), pl.BlockSpec((tm, cb), lambda c, i: (nt - 1 - i, c))],
        out_specs=[pl.BlockSpec((2, tm, cb), lambda c, i: (0, nt - 1 - i, c)), dtaps, dtaps],
        out_shape=[jax.ShapeDtypeStruct((2, t, f), BF16), jax.ShapeDtypeStruct((FFN_CONV, f), F32),
                   jax.ShapeDtypeStruct((FFN_CONV, f), F32)],
        scratch_shapes=[pltpu.VMEM((HALO + rc, cb), F32)] * 2 + [pltpu.VMEM((tm + HALO, cb), F32)] * 2,
        compiler_params=_params(("parallel", "arbitrary")),
    )(up, up, up, up, cw, cw, dact)


def _ple_fwd(h, zg, pe, *, name, tm=512):
    t, d = h.shape
    tm = _tile(t, tm)

    def body(h_ref, z_ref, p_ref, o_ref):
        o_ref[...] = h_ref[...] + _sigmoid(z_ref[...]) * p_ref[...]

    row = pl.BlockSpec((tm, d), lambda i: (i, 0))
    return pl.pallas_call(
        body, name=name, grid=(t // tm,), in_specs=[row] * 3, out_specs=row,
        out_shape=jax.ShapeDtypeStruct((t, d), F32), compiler_params=_params(("parallel",)),
    )(h, zg, pe)


def _ple_bwd(dh, zg, pe, *, name, tm=512):
    t, d = dh.shape
    tm = _tile(t, tm)

    def body(g_ref, z_ref, p_ref, dz_ref, dp_ref):
        g = g_ref[...]
        sg = _sigmoid(z_ref[...])
        dz_ref[...] = (g * p_ref[...] * sg * (1.0 - sg)).astype(BF16)
        dp_ref[...] = (g * sg).astype(BF16)

    row = pl.BlockSpec((tm, d), lambda i: (i, 0))
    return pl.pallas_call(
        body, name=name, grid=(t // tm,), in_specs=[row] * 3, out_specs=[row] * 2,
        out_shape=[jax.ShapeDtypeStruct((t, d), BF16)] * 2, compiler_params=_params(("parallel",)),
    )(dh, zg, pe)


def _my_pos():
    return lax.axis_index("x"), lax.axis_index("y"), lax.axis_index("c")


def _all_gather(block, *, name):
    r, w = block.shape

    def body(x_ref, out_ref, send_sems, recv_sems, local_sem):
        x, y, c = _my_pos()
        me, sibling = (x, y, c), (x, y, 1 - c)
        chips = [(1 - x, y), (x, 1 - y), (1 - x, 1 - y)]

        def slot(px, py, pc):
            return out_ref.at[4 * px + 2 * py + pc]

        def copy(k, blk, to, src=None):
            return pltpu.make_async_remote_copy(
                src_ref=slot(*blk) if src is None else src, dst_ref=slot(*blk),
                send_sem=send_sems.at[k], recv_sem=recv_sems.at[k],
                device_id=to, device_id_type=pl.DeviceIdType.MESH)

        mine = pltpu.make_async_copy(x_ref, slot(*me), local_sem)
        mine.start()
        first = [copy(0, me, sibling, src=x_ref)]
        first += [copy(1 + j, me, (*chip, c), src=x_ref) for j, chip in enumerate(chips)]
        for cp in first:
            cp.start()
        passed = [copy(4 + j, (*chip, c), sibling) for j, chip in enumerate(chips)]
        for j, chip in enumerate(chips):
            copy(1 + j, (*chip, c), me).wait_recv()
            passed[j].start()
        copy(0, sibling, me).wait_recv()
        for j, chip in enumerate(chips):
            copy(4 + j, (*chip, 1 - c), me).wait_recv()
        for cp in first + passed:
            cp.wait_send()
        mine.wait()

    return pl.pallas_call(
        body,
        name=name,
        out_shape=jax.ShapeDtypeStruct((N_DEV, r, w), block.dtype),
        in_specs=[pl.BlockSpec(memory_space=pl.ANY)],
        out_specs=pl.BlockSpec(memory_space=pl.ANY),
        scratch_shapes=[pltpu.SemaphoreType.DMA((7,)), pltpu.SemaphoreType.DMA((7,)), pltpu.SemaphoreType.DMA],
    )(block)


def _all_to_all(slabs, *, name):
    n, r, w = slabs.shape

    def body(x_ref, out_ref, send_sems, recv_sems, local_sem):
        x, y, c = _my_pos()
        my_idx = 4 * x + 2 * y + c
        mine = pltpu.make_async_copy(x_ref.at[my_idx], out_ref.at[my_idx], local_sem)
        mine.start()
        copies = []
        for k in range(1, N_DEV):
            fx, fy, fc = (k >> 2) & 1, (k >> 1) & 1, k & 1
            px = (1 - x) if fx else x
            py = (1 - y) if fy else y
            pc = (1 - c) if fc else c
            cp = pltpu.make_async_remote_copy(
                src_ref=x_ref.at[4 * px + 2 * py + pc], dst_ref=out_ref.at[my_idx],
                send_sem=send_sems.at[k - 1], recv_sem=recv_sems.at[k - 1],
                device_id=(px, py, pc), device_id_type=pl.DeviceIdType.MESH)
            cp.start()
            copies.append(cp)
        for cp in copies:
            cp.wait_recv()
        for cp in copies:
            cp.wait_send()
        mine.wait()

    return pl.pallas_call(
        body,
        name=name,
        out_shape=jax.ShapeDtypeStruct((n, r, w), slabs.dtype),
        in_specs=[pl.BlockSpec(memory_space=pl.ANY)],
        out_specs=pl.BlockSpec(memory_space=pl.ANY),
        scratch_shapes=[pltpu.SemaphoreType.DMA((7,)), pltpu.SemaphoreType.DMA((7,)), pltpu.SemaphoreType.DMA],
    )(slabs)


def _exchange_copies(scatter, src_refs, land_refs, send_sems, recv_sems, local_sems):
    x, y, c = _my_pos()
    me = 4 * x + 2 * y + c
    local, remote = [], []
    for i, (s, l) in enumerate(zip(src_refs, land_refs)):
        local.append(pltpu.make_async_copy(s.at[me] if scatter else s, l.at[me], local_sems.at[i]))
        for k in range(1, N_DEV):
            px = (1 - x) if (k >> 2) & 1 else x
            py = (1 - y) if (k >> 1) & 1 else y
            pc = (1 - c) if k & 1 else c
            remote.append(pltpu.make_async_remote_copy(
                src_ref=s.at[4 * px + 2 * py + pc] if scatter else s, dst_ref=l.at[me],
                send_sem=send_sems.at[(N_DEV - 1) * i + k - 1], recv_sem=recv_sems.at[(N_DEV - 1) * i + k - 1],
                device_id=(px, py, pc), device_id_type=pl.DeviceIdType.MESH))
    return local, remote


def _exchange(arrays, *, scatter, name):
    n = len(arrays)

    def body(*refs):
        srcs, lands = refs[:n], refs[n:2 * n]
        local, remote = _exchange_copies(scatter, srcs, lands, *refs[2 * n:])
        for cp in local + remote:
            cp.start()
        for cp in remote:
            cp.wait_recv()
        for cp in remote:
            cp.wait_send()
        for cp in local:
            cp.wait()

    hbm = pl.BlockSpec(memory_space=pl.ANY)
    out = pl.pallas_call(
        body,
        name=name,
        out_shape=[jax.ShapeDtypeStruct(a.shape if scatter else (N_DEV,) + a.shape, a.dtype) for a in arrays],
        in_specs=[hbm] * n,
        out_specs=[hbm] * n,
        scratch_shapes=[pltpu.SemaphoreType.DMA(((N_DEV - 1) * n,)), pltpu.SemaphoreType.DMA(((N_DEV - 1) * n,)),
                        pltpu.SemaphoreType.DMA((n,))],
    )(*arrays)
    return list(out)


_HBM_SPEC = pl.BlockSpec(memory_space=pltpu.HBM)
_SEM_SPEC = pl.BlockSpec(memory_space=pltpu.SEMAPHORE)
_EFFECT = pltpu.SideEffectType.DATAFLOW_SIDE_EFFECTING


def _exchange_start(arrays, *, scatter, name, dep):
    n = len(arrays)
    srcs = [pltpu.with_memory_space_constraint(a, pltpu.HBM) for a in arrays]
    lands = [pltpu.with_memory_space_constraint(lax.empty(a.shape if scatter else (N_DEV,) + a.shape, a.dtype), pltpu.HBM)
             for a in arrays]

    def body(*refs):
        src_refs, land_refs = refs[:n], refs[n:2 * n]
        send_sems, recv_sems, local_sems = refs[2 * n + 1:2 * n + 4]
        token = refs[-1]
        local, remote = _exchange_copies(scatter, src_refs, land_refs, send_sems, recv_sems, local_sems)
        for cp in local + remote:
            cp.start()
        token[...] = jnp.zeros_like(token)

    sems = (pltpu.SemaphoreType.DMA(((N_DEV - 1) * n,)), pltpu.SemaphoreType.DMA(((N_DEV - 1) * n,)),
            pltpu.SemaphoreType.DMA((n,)))
    out = pl.pallas_call(
        body,
        name=name,
        out_shape=sems + tuple(pltpu.HBM(a.shape, a.dtype) for a in srcs + lands) + (jax.ShapeDtypeStruct((8, LANES), F32),),
        in_specs=[_HBM_SPEC] * (2 * n) + [pl.BlockSpec(memory_space=pl.ANY)],
        out_specs=(_SEM_SPEC,) * 3 + (_HBM_SPEC,) * (2 * n) + (pl.BlockSpec(memory_space=pltpu.VMEM),),
        input_output_aliases={i: 3 + i for i in range(2 * n)},
        compiler_params=pltpu.CompilerParams(has_side_effects=_EFFECT),
    )(*srcs, *lands, dep)
    return (out[:3], list(out[3:3 + n]), list(out[3 + n:3 + 2 * n])), out[-1]


def _exchange_wait(handle, after, *, scatter, name):
    sems, srcs, lands = handle
    n = len(srcs)

    def body(*refs):
        src_refs, land_refs = refs[:n], refs[n:2 * n]
        send_sems, recv_sems, local_sems = refs[2 * n:2 * n + 3]
        local, remote = _exchange_copies(scatter, src_refs, land_refs, send_sems, recv_sems, local_sems)
        for cp in remote:
            cp.wait_send()
            cp.wait_recv()
        for cp in local:
            cp.wait()

    out = pl.pallas_call(
        body,
        name=name,
        out_shape=tuple(pltpu.HBM(a.shape, a.dtype) for a in srcs + lands),
        in_specs=[_HBM_SPEC] * (2 * n) + [_SEM_SPEC] * 3 + [pl.BlockSpec(memory_space=pl.ANY)],
        out_specs=(_HBM_SPEC,) * (2 * n),
        input_output_aliases={i: i for i in range(2 * n)},
        compiler_params=pltpu.CompilerParams(has_side_effects=_EFFECT),
    )(*srcs, *lands, *sems, after)
    return list(out[n:])


def _sum_parts(parts, *, name, tr=512):
    n, r, lanes = parts.shape
    tr = tr if (r % tr == 0 and r > 1024) else r

    def body(p_ref, g_ref):
        g = p_ref[0].astype(F32)
        for j in range(1, n):
            g = g + p_ref[j].astype(F32)
        g_ref[...] = g

    row = pl.BlockSpec((tr, lanes), lambda i: (i, 0))
    return pl.pallas_call(
        body,
        name=name,
        grid=(r // tr,),
        in_specs=[pl.BlockSpec((n, tr, lanes), lambda i: (0, i, 0))],
        out_specs=row,
        out_shape=jax.ShapeDtypeStruct((r, lanes), F32),
        compiler_params=_params(("parallel",)),
    )(parts)


def _adamw_update(g, w, m, v):
    c1 = 1.0 / (1.0 - ADAM_B1 ** ADAM_STEP)
    c2 = 1.0 / (1.0 - ADAM_B2 ** ADAM_STEP)
    nm = ADAM_B1 * m + (1.0 - ADAM_B1) * g
    nv = ADAM_B2 * v + (1.0 - ADAM_B2) * (g * g)
    return -ADAM_LR * ((nm * c1) / (jnp.sqrt(nv * c2) + ADAM_EPS) + ADAM_WD * w), nm, nv


def _adamw_layer(g, w, m, v, layer, prev, *, name):
    nl, k, n = w.shape
    tr = max(d for d in range(8, min(k, 256) + 1, 8) if k % d == 0)
    in_parts = g.ndim == 3

    def body(g_ref, w_ref, m_ref, v_ref, *rest):
        go_ref, d_ref, nm_ref, nv_ref = rest[-4:]
        if in_parts:
            gg = g_ref[0].astype(F32)
            for j in range(1, g_ref.shape[0]):
                gg = gg + g_ref[j].astype(F32)
        else:
            gg = g_ref[...]
        d, nm, nv = _adamw_update(gg, w_ref[...], m_ref[...], v_ref[...])
        go_ref[...] = gg
        d_ref[...] = d
        nm_ref[...] = nm
        nv_ref[...] = nv

    lay = pl.BlockSpec((None, tr, n), lambda i: (layer, i, 0))
    n_prev = 0 if prev is None else 4
    out = jax.ShapeDtypeStruct((nl, k, n), F32)
    return pl.pallas_call(
        body,
        name=name,
        grid=(k // tr,),
        in_specs=[pl.BlockSpec((g.shape[0], tr, n), lambda i: (0, i, 0)) if in_parts else pl.BlockSpec((tr, n), lambda i: (i, 0)),
                  lay, lay, lay] + [pl.BlockSpec(memory_space=pl.ANY)] * n_prev,
        out_specs=[lay] * 4,
        out_shape=[out] * 4,
        input_output_aliases={4 + j: j for j in range(n_prev)},
        compiler_params=_params(("parallel",)),
    )(g, w, m, v, *(prev or ()))


def _adamw_packed(g, w, m, v, *, name, tr=512):
    r, lanes = g.shape
    tr = tr if r % tr == 0 else r
    c1 = 1.0 / (1.0 - ADAM_B1 ** ADAM_STEP)
    c2 = 1.0 / (1.0 - ADAM_B2 ** ADAM_STEP)

    def body(g_ref, w_ref, m_ref, v_ref, d_ref, nm_ref, nv_ref):
        g = g_ref[...]
        nm = ADAM_B1 * m_ref[...] + (1.0 - ADAM_B1) * g
        nv = ADAM_B2 * v_ref[...] + (1.0 - ADAM_B2) * (g * g)
        nm_ref[...] = nm
        nv_ref[...] = nv
        d_ref[...] = -ADAM_LR * ((nm * c1) / (jnp.sqrt(nv * c2) + ADAM_EPS) + ADAM_WD * w_ref[...])

    row = pl.BlockSpec((tr, lanes), lambda i: (i, 0))
    out = jax.ShapeDtypeStruct((r, lanes), F32)
    return pl.pallas_call(
        body,
        name=name,
        grid=(r // tr,),
        in_specs=[row] * 4,
        out_specs=[row] * 3,
        out_shape=[out] * 3,
        compiler_params=_params(("parallel",)),
    )(g, w, m, v)


BIG = ("a_w_in", "a_w_out", "b_w_in", "b_w_out", "f_w_up", "f_w_down", "ple_w_proj", "ple_w_gate")
CONVS = ("a_conv", "f_conv")
SMALL = ("norm_mix", "norm_ffn", "norm_ple", "norm_final", "a_log", "a_dt_bias", "a_norm", "b_sinks")
WEIGHTS = ("norm_mix", "norm_ffn", "norm_ple", "norm_final", "a_w_in", "a_conv", "a_log", "a_dt_bias", "a_norm",
           "a_w_out", "b_w_in", "b_sinks", "b_w_out", "f_w_up", "f_conv", "f_w_down", "ple_w_proj", "ple_w_gate")
SLAB_ROW_MULTIPLE = 512


def _pack(arrs, dtype, row_multiple):
    flat = jnp.concatenate([a.reshape(-1).astype(dtype) for a in arrs])
    rows = -(-flat.shape[0] // LANES)
    rows = -(-rows // row_multiple) * row_multiple
    return jnp.pad(flat, (0, rows * LANES - flat.shape[0])).reshape(rows, LANES)


def _unpack(slab, shapes):
    lead = slab.shape[:-2]
    flat = slab.reshape(lead + (-1,))
    out, off = [], 0
    for s in shapes:
        size = math.prod(s)
        out.append(flat[..., off:off + size].reshape(lead + tuple(s)))
        off += size
    return out


def _cols_full(g):
    g = jnp.moveaxis(g, 0, -2)
    return g.reshape(g.shape[:-2] + (g.shape[-2] * g.shape[-1],))


def _rows_full(g):
    g = jnp.moveaxis(g, 0, -3)
    return g.reshape(g.shape[:-3] + (g.shape[-3] * g.shape[-2], g.shape[-1]))


def _cols_split(wfull):
    n = wfull.shape[-1] // N_DEV
    g = wfull.reshape(wfull.shape[:-1] + (N_DEV, n))
    return jnp.moveaxis(g, -2, 0)


def _rows_split(wfull):
    k = wfull.shape[-2] // N_DEV
    g = wfull.reshape(wfull.shape[:-2] + (N_DEV, k, wfull.shape[-1]))
    return jnp.moveaxis(g, -3, 0)


TRANSPOSED = ("a_w_in", "b_w_in", "f_w_up", "ple_w_proj")


def _wire(name, a):
    return jnp.swapaxes(a, -1, -2) if name in TRANSPOSED else a


def _wire_shape(name, shape):
    return shape[:-2] + (shape[-1], shape[-2]) if name in TRANSPOSED else tuple(shape)


def _full(name, g):
    return _cols_full(g) if name in CONVS else _rows_full(g)


def _split(name, wfull):
    return _cols_split(wfull) if name in CONVS else _rows_split(wfull)


def _pack_split(grads, names, dtype, row_multiple):
    flat = jnp.concatenate([_split(n, grads[n]).reshape(N_DEV, -1).astype(dtype) for n in names], axis=1)
    rows = -(-flat.shape[1] // LANES)
    rows = -(-rows // row_multiple) * row_multiple
    return jnp.pad(flat, ((0, 0), (0, rows * LANES - flat.shape[1]))).reshape(N_DEV, rows, LANES)


def _pad_cols(a, width):
    return jnp.pad(a, ((0, 0), (0, width - a.shape[1])))


def kernel(x, p, norm_mix, norm_ffn, norm_ple, norm_final, a_w_in, a_conv, a_log, a_dt_bias, a_norm, a_w_out, b_w_in, b_sinks, b_w_out, f_w_up, f_conv, f_w_down, ple_w_proj, ple_w_gate, loss_target, m_norm_mix, m_norm_ffn, m_norm_ple, m_norm_final, m_a_w_in, m_a_conv, m_a_log, m_a_dt_bias, m_a_norm, m_a_w_out, m_b_w_in, m_b_sinks, m_b_w_out, m_f_w_up, m_f_conv, m_f_w_down, m_ple_w_proj, m_ple_w_gate, v_norm_mix, v_norm_ffn, v_norm_ple, v_norm_final, v_a_w_in, v_a_conv, v_a_log, v_a_dt_bias, v_a_norm, v_a_w_out, v_b_w_in, v_b_sinks, v_b_w_out, v_f_w_up, v_f_conv, v_f_w_down, v_ple_w_proj, v_ple_w_gate):
    wts = dict(norm_mix=norm_mix, norm_ffn=norm_ffn, norm_ple=norm_ple, norm_final=norm_final, a_w_in=a_w_in,
               a_conv=a_conv, a_log=a_log, a_dt_bias=a_dt_bias, a_norm=a_norm, a_w_out=a_w_out, b_w_in=b_w_in,
               b_sinks=b_sinks, b_w_out=b_w_out, f_w_up=f_w_up, f_conv=f_conv, f_w_down=f_w_down,
               ple_w_proj=ple_w_proj, ple_w_gate=ple_w_gate)
    mom = dict(norm_mix=m_norm_mix, norm_ffn=m_norm_ffn, norm_ple=m_norm_ple, norm_final=m_norm_final,
               a_w_in=m_a_w_in, a_conv=m_a_conv, a_log=m_a_log, a_dt_bias=m_a_dt_bias, a_norm=m_a_norm,
               a_w_out=m_a_w_out, b_w_in=m_b_w_in, b_sinks=m_b_sinks, b_w_out=m_b_w_out, f_w_up=m_f_w_up,
               f_conv=m_f_conv, f_w_down=m_f_w_down, ple_w_proj=m_ple_w_proj, ple_w_gate=m_ple_w_gate)
    var = dict(norm_mix=v_norm_mix, norm_ffn=v_norm_ffn, norm_ple=v_norm_ple, norm_final=v_norm_final,
               a_w_in=v_a_w_in, a_conv=v_a_conv, a_log=v_a_log, a_dt_bias=v_a_dt_bias, a_norm=v_a_norm,
               a_w_out=v_a_w_out, b_w_in=v_b_w_in, b_sinks=v_b_sinks, b_w_out=v_b_w_out, f_w_up=v_f_w_up,
               f_conv=v_f_conv, f_w_down=v_f_w_down, ple_w_proj=v_ple_w_proj, ple_w_gate=v_ple_w_gate)
    hk = N_HEADS_A * HEAD_DIM_A
    xs = x[0]
    tgt = loss_target[0]
    p_bf = p.astype(BF16)

    def shard(name, layer):
        return _wire(name, wts[name][layer]).astype(BF16)

    def stacked_rows(g):
        return g.reshape(g.shape[0] * g.shape[1], g.shape[2])

    n_in = a_w_in.shape[-1]
    first = _all_gather(jnp.concatenate([shard("a_w_in", 0), shard("a_w_out", 0)]), name="gather_mixer0")
    wa_in_t = jnp.pad(stacked_rows(first[:, :n_in]), ((0, PROJ_A - PROJ_A_REAL), (0, 0)))
    wa_out = stacked_rows(first[:, n_in:])
    gconv = _all_gather(_pack([wts[n] for n in CONVS], F32, 8), name="gather_convs")
    conv_full = {n: _cols_full(g) for n, g in zip(CONVS, _unpack(gconv, [wts[n].shape for n in CONVS]))}
    cv_a, cv_f = conv_full["a_conv"][0], conv_full["f_conv"]
    layer_names = ("f_w_up", "f_w_down", "ple_w_proj", "ple_w_gate")
    gather0, tok = _exchange_start([shard(n, 0) for n in layer_names], scatter=False, name="gather_layer0_start", dep=first)
    gather1, tok = _exchange_start([shard(n, 0) for n in ("b_w_in", "b_w_out")] + [shard(n, 1) for n in layer_names],
                                   scatter=False, name="gather_layer1_start", dep=tok)

    alog_row = jnp.pad(a_log, ((0, 0), (N_HEADS_A, LANES - 2 * N_HEADS_A)))
    dtb_row = jnp.pad(a_dt_bias, ((0, 0), (N_HEADS_A, LANES - 2 * N_HEADS_A)))

    def ffn_ple_fwd(i, h_a, w_up_t, w_down, w_pp_t, w_pg):
        n_f = _rms_fwd(h_a, norm_ffn[i], name=f"l{i}_ffn_norm")
        up = _matmul(n_f, w_up_t, tb=True, name=f"l{i}_ffn_up")
        act = _ffn_act_fwd(up, cv_f[i], name=f"l{i}_ffn_act")
        h_b = _matmul(act, w_down, res=h_a, name=f"l{i}_ffn_down")
        n_p = _rms_fwd(h_b, norm_ple[i], name=f"l{i}_ple_norm")
        zg = _matmul(n_p, w_pg, name=f"l{i}_ple_gate")
        pe = _matmul(p_bf[i, 0], w_pp_t, tb=True, name=f"l{i}_ple_proj")
        h_c = _ple_fwd(h_b, zg, pe, name=f"l{i}_ple_mix")
        return h_c, dict(n_f=n_f, up=up, act=act, h_b=h_b, n_p=n_p, zg=zg, pe=pe)

    def layer_weights(lands):
        up_t, down, pp_t, pg = (stacked_rows(g) for g in lands)
        return up_t.reshape(2, D_FF, D_MODEL), down, pp_t, pg

    n0 = _rms_fwd(xs, norm_mix[0], name="l0_mix_norm")
    proj = _matmul(n0, wa_in_t, tb=True, tm=512, dep=tok, name="l0_in_proj")
    q, k, v, gbc, bbc = _delta_pre_fwd(proj, cv_a, alog_row, dtb_row, name="l0_delta_pre")
    *prep, tinv = _delta_prep_fwd(q, k, v, gbc, bbc, name="l0_delta_prep")
    o, states = _delta_scan_fwd(prep, name="l0_delta_scan")
    og = _gated_norm_fwd(o, proj, a_norm, name="l0_gated_norm")
    h1 = _matmul(og, wa_out, res=xs, name="l0_mix_out")
    lw0 = layer_weights(_exchange_wait(gather0, h1, scatter=False, name="gather_layer0_wait"))
    h3, sv0 = ffn_ple_fwd(0, h1, *lw0)

    lands1 = _exchange_wait(gather1, h3, scatter=False, name="gather_layer1_wait")
    wb_in_t, wb_out = stacked_rows(lands1[0]), stacked_rows(lands1[1])
    lw1 = layer_weights(lands1[2:])
    n1 = _rms_fwd(h3, norm_mix[1], name="l1_mix_norm")
    pb = _matmul(n1, wb_in_t, tb=True, name="l1_in_qkv")
    att = _swa_fwd(pb, b_sinks, name="l1_swa")
    h4 = _matmul(att, wb_out, res=h3, name="l1_mix_out")
    h6, sv1 = ffn_ple_fwd(1, h4, *lw1)

    loss_row, dh6, d_norm_final = _final_loss(h6, norm_final, tgt, name="final_loss")
    loss = lax.psum(loss_row[0, 0], MESH_AXES)

    def ffn_ple_bwd(i, dh_c, h_a, sv, lw, dep):
        w_up_t, w_down, _, w_pg = lw
        dzg, dpe = _ple_bwd(dh_c, sv["zg"], sv["pe"], name=f"l{i}_ple_mix_bwd")
        d_pg = _matmul(sv["n_p"], dzg, ta=True, out_dtype=BF16, dep=dep, name=f"l{i}_ple_gate_dw")
        d_pp_t = _matmul(dpe, p_bf[i, 0], ta=True, out_dtype=BF16, name=f"l{i}_ple_proj_dw")
        dn_p = _matmul(dzg, w_pg, tb=True, name=f"l{i}_ple_gate_dx")
        dh_b, d_np = _rms_bwd(sv["h_b"], norm_ple[i], dn_p, dh_c, name=f"l{i}_ple_norm_bwd")
        dact = _matmul(dh_b, w_down, tb=True, name=f"l{i}_ffn_down_dx")
        d_down = _matmul(sv["act"], dh_b, ta=True, out_dtype=BF16, name=f"l{i}_ffn_down_dw")
        dup, d_cg, d_cv = _ffn_act_bwd(sv["up"], cv_f[i], dact, name=f"l{i}_ffn_act_bwd")
        d_up_t = _matmul(dup, sv["n_f"], ta=True, out_dtype=BF16, name=f"l{i}_ffn_up_dw")
        dn_f = _matmul(dup, w_up_t, name=f"l{i}_ffn_up_dx")
        dh_a, d_nf = _rms_bwd(h_a, norm_ffn[i], dn_f, dh_b, name=f"l{i}_ffn_norm_bwd")
        mats = [d_up_t.reshape(2 * D_FF, D_MODEL), d_down, d_pp_t, d_pg]
        return dh_a, mats, dict(norm_ple=d_np, norm_ffn=d_nf, f_conv=jnp.concatenate([d_cg, d_cv], axis=1))

    def slabs(g):
        return g.reshape(N_DEV, g.shape[0] // N_DEV, g.shape[1])

    dh4, mats1, g1 = ffn_ple_bwd(1, dh6, h4, sv1, lw1, None)
    datt = _matmul(dh4, wb_out, tb=True, out_dtype=BF16, name="l1_mix_out_dx")
    d_wb_out = _matmul(att, dh4, ta=True, out_dtype=BF16, name="l1_mix_out_dw")
    dq_b, dk_b, dv_b, dsinks = _swa_bwd(pb, b_sinks, datt, name="l1_swa_bwd")
    dpb = jnp.concatenate([dq_b, dk_b.astype(BF16), dv_b.astype(BF16)], axis=1)
    d_wb_in_t = _matmul(dpb, n1, ta=True, out_dtype=BF16, name="l1_in_qkv_dw")
    send1, tok = _exchange_start([slabs(g) for g in [d_wb_in_t, d_wb_out] + mats1], scatter=True,
                                 name="exchange_layer1_start", dep=d_wb_in_t)
    dn1 = _matmul(dpb, wb_in_t, name="l1_in_qkv_dx")
    dh3, d_nm1 = _rms_bwd(h3, norm_mix[1], dn1, dh4, name="l1_mix_norm_bwd")

    dh1, mats0, g0 = ffn_ple_bwd(0, dh3, h1, sv0, lw0, tok)
    send0, tok = _exchange_start([slabs(g) for g in mats0], scatter=True, name="exchange_layer0_start", dep=mats0[0])
    dog = _matmul(dh1, wa_out, tb=True, dep=tok, name="l0_mix_out_dx")
    d_wa_out = _matmul(og, dh1, ta=True, out_dtype=BF16, name="l0_mix_out_dw")
    do, dz0, d_anorm = _gated_norm_bwd(o, proj, a_norm, dog, name="l0_gated_norm_bwd")
    cts = _delta_scan_bwd(prep, states, do, name="l0_delta_scan_bwd")
    dq, dk, dv, dgbc, dbbc = _delta_prep_bwd(q, k, v, gbc, bbc, tinv, cts, name="l0_delta_prep_bwd")
    dproj, d_aconv, d_alog, d_dtb = _delta_pre_bwd(proj, cv_a, alog_row, dtb_row, dq, dk, dv, dgbc, dbbc, dz0,
                                                   name="l0_delta_pre_bwd")
    d_wa_in_t = _matmul(dproj, n0, ta=True, out_dtype=BF16, name="l0_in_proj_dw")
    sendm, tok = _exchange_start([slabs(d_wa_in_t[:PROJ_A_REAL]), slabs(d_wa_out)], scatter=True,
                                 name="exchange_mixer0_start", dep=d_wa_in_t)
    dn0 = _matmul(dproj, wa_in_t, tm=512, dep=tok, name="l0_in_proj_dx")
    dx, d_nm0 = _rms_bwd(xs, norm_mix[0], dn0, dh1, name="l0_mix_norm_bwd")

    recv1 = _exchange_wait(send1, dx, scatter=True, name="exchange_layer1_wait")
    recv0 = _exchange_wait(send0, recv1[0], scatter=True, name="exchange_layer0_wait")
    parts = {("b_w_in", 0): recv1[0], ("b_w_out", 0): recv1[1]}
    parts.update({(n, 1): r for n, r in zip(layer_names, recv1[2:])})
    parts.update({(n, 0): r for n, r in zip(layer_names, recv0)})

    outs = {}

    def update_matrix(name):
        res = None
        for layer in range(wts[name].shape[0]):
            g = parts[(name, layer)]
            if name in TRANSPOSED:
                g = _sum_parts(g, name=f"sum_{name}_{layer}").T
            res = _adamw_layer(g, wts[name], mom[name], var[name], layer, res, name=f"adamw_{name}_{layer}")
        for kind, arr in zip(("grad", "delta", "new_m", "new_v"), res):
            outs[(kind, name)] = arr
        return res

    last = [update_matrix(n) for n in ("b_w_in", "b_w_out") + layer_names][-1]
    recvm = _exchange_wait(sendm, last[0], scatter=True, name="exchange_mixer0_wait")
    parts.update({("a_w_in", 0): recvm[0], ("a_w_out", 0): recvm[1]})
    update_matrix("a_w_in")
    update_matrix("a_w_out")

    gconvs = dict(a_conv=d_aconv[None], f_conv=jnp.stack([g0["f_conv"], g1["f_conv"]]))
    small_g = dict(norm_mix=jnp.concatenate([d_nm0, d_nm1]), norm_ffn=jnp.concatenate([g0["norm_ffn"], g1["norm_ffn"]]),
                   norm_ple=jnp.concatenate([g0["norm_ple"], g1["norm_ple"]]), norm_final=d_norm_final[0],
                   a_log=d_alog[:, N_HEADS_A:2 * N_HEADS_A], a_dt_bias=d_dtb[:, N_HEADS_A:2 * N_HEADS_A],
                   a_norm=d_anorm, b_sinks=dsinks[:, :N_HEADS_B])
    recv_conv = _all_to_all(_pack_split(gconvs, CONVS, F32, 8), name="exchange_conv_grads")
    recv_small = _all_gather(_pack([small_g[n] for n in SMALL], F32, 8), name="gather_small_grads")
    for names, recv, tag in ((CONVS, recv_conv, "convs"), (SMALL, recv_small, "small")):
        shapes = [wts[n].shape for n in names]
        g_slab = _sum_parts(recv, name=f"sum_{tag}")
        packed = [_pack([d[n] for n in names], F32, 8) for d in (wts, mom, var)]
        res = _adamw_packed(g_slab, *packed, name=f"adamw_{tag}")
        for kind, slab in zip(("grad", "delta", "new_m", "new_v"), (g_slab,) + tuple(res)):
            for n, arr in zip(names, _unpack(slab, shapes)):
                outs[(kind, n)] = arr

    result = [loss, dx[None]]
    for kind in ("grad", "delta", "new_m", "new_v"):
        result += [outs[(kind, n)] for n in WEIGHTS]
    return tuple(result)
```

```python
import functools
import math

import jax
import jax.numpy as jnp
from jax import lax
from jax.experimental import pallas as pl
from jax.experimental.pallas import tpu as pltpu

F32 = jnp.float32
BF16 = jnp.bfloat16

D_MODEL = 1024
N_HEADS_A = 8
HEAD_DIM_A = 128
CONV_A = 4
CHUNK = 128
N_HEADS_B = 16
N_KV_B = 4
GROUP_B = N_HEADS_B // N_KV_B
HEAD_DIM_B = 64
WINDOW = 128
D_FF = 2816
FFN_CONV = 3
PLE_DIM = 256
EPS = 1e-6
N_DEV = 8
HALO = 8
PROJ_A_REAL = 4 * N_HEADS_A * HEAD_DIM_A + 2 * N_HEADS_A
PROJ_A = 4 * N_HEADS_A * HEAD_DIM_A + 128
Z_COL_BLOCK = 3
BA_COL_BLOCK = 32

ADAM_LR = 0.001
ADAM_B1 = 0.9
ADAM_B2 = 0.999
ADAM_EPS = 1e-08
ADAM_WD = 0.01
ADAM_STEP = 10

LANES = 128
VMEM_LIMIT_BYTES = 56 * 1024 * 1024
NEG_BIG = -1e30

MESH_AXES = ("x", "y", "c")


def _params(sem=None):
    return pltpu.CompilerParams(dimension_semantics=sem, vmem_limit_bytes=VMEM_LIMIT_BYTES)


def _tile(n, target):
    best = None
    for t in range(LANES, min(n, target) + 1, LANES):
        if n % t == 0:
            best = t
    return best or n


def _sigmoid(x):
    return 1.0 / (1.0 + jnp.exp(-x))


def _softplus(x):
    return jnp.maximum(x, 0.0) + jnp.log1p(jnp.exp(-jnp.abs(x)))


def _matmul(a, b, *, name, ta=False, tb=False, res=None, out_dtype=F32, tm=1408, tn=1408, tk=None, dep=None):
    sa, sb = a.ndim == 3, b.ndim == 3
    ns = a.shape[0] if sa else (b.shape[0] if sb else 1)
    contract_stack = sa and sb
    out_stacked = sa != sb
    m = a.shape[-1] if ta else a.shape[-2]
    k = a.shape[-2] if ta else a.shape[-1]
    n = b.shape[-2] if tb else b.shape[-1]
    assert (b.shape[-1] if tb else b.shape[-2]) == k, (a.shape, b.shape, ta, tb)
    if tk is None:
        tk = 1024 if ta else 2816
    tm, tn, tk = _tile(m, tm), _tile(n, tn), _tile(k, tk)
    nk = k // tk
    nsteps = nk * (ns if contract_stack else 1)
    dims = (((0 if ta else 1,), (1 if tb else 0,)), ((), ()))

    def spec(block, stacked, order):
        def index(g, i, j, kk):
            two = order(i, j, kk % nk)
            if not stacked:
                return two
            return (kk // nk if contract_stack else g,) + two
        return pl.BlockSpec(((None,) if stacked else ()) + block, index)

    a_spec = spec((tk, tm), sa, lambda i, j, kq: (kq, i)) if ta else spec((tm, tk), sa, lambda i, j, kq: (i, kq))
    b_spec = spec((tn, tk), sb, lambda i, j, kq: (j, kq)) if tb else spec((tk, tn), sb, lambda i, j, kq: (kq, j))
    o_spec = spec((tm, tn), out_stacked, lambda i, j, kq: (i, j))
    has_res = res is not None
    has_dep = dep is not None

    def body(*refs):
        a_ref, b_ref = refs[0], refs[1]
        r_ref = refs[2] if has_res else None
        o_ref = refs[2 + has_res + has_dep]
        part = lax.dot_general(a_ref[...].astype(BF16), b_ref[...].astype(BF16), dims, preferred_element_type=F32)

        def finish(acc):
            if has_res:
                acc = acc + r_ref[...].astype(F32)
            o_ref[...] = acc.astype(out_dtype)

        if nsteps == 1:
            finish(part)
        else:
            acc_ref = refs[-1]
            kk = pl.program_id(3)

            @pl.when(kk == 0)
            def _():
                acc_ref[...] = part

            @pl.when(kk > 0)
            def _():
                acc_ref[...] += part

            @pl.when(kk == nsteps - 1)
            def _():
                finish(acc_ref[...])

    in_specs = [a_spec, b_spec] + ([o_spec] if has_res else []) + ([pl.BlockSpec(memory_space=pl.ANY)] if has_dep else [])
    args = (a, b) + ((res,) if has_res else ()) + ((dep,) if has_dep else ())
    return pl.pallas_call(
        body,
        name=name,
        grid=(ns if out_stacked else 1, m // tm, n // tn, nsteps),
        in_specs=in_specs,
        out_specs=o_spec,
        out_shape=jax.ShapeDtypeStruct(((ns,) if out_stacked else ()) + (m, n), out_dtype),
        scratch_shapes=[pltpu.VMEM((tm, tn), F32)] if nsteps > 1 else [],
        compiler_params=_params(("parallel", "parallel", "parallel", "arbitrary")),
    )(*args)


def _rms_fwd(h, w, *, name, tm=512):
    t, d = h.shape
    tm = _tile(t, tm)

    def body(h_ref, w_ref, o_ref):
        x = h_ref[...]
        r = lax.rsqrt(jnp.mean(x * x, axis=-1, keepdims=True) + EPS)
        o_ref[...] = (x * r * w_ref[...]).astype(BF16)

    return pl.pallas_call(
        body,
        name=name,
        grid=(t // tm,),
        in_specs=[pl.BlockSpec((tm, d), lambda i: (i, 0)), pl.BlockSpec((1, d), lambda i: (0, 0))],
        out_specs=pl.BlockSpec((tm, d), lambda i: (i, 0)),
        out_shape=jax.ShapeDtypeStruct((t, d), BF16),
        compiler_params=_params(("parallel",)),
    )(h, w.reshape(1, d))


def _rms_bwd(h, w, dn, skip, *, name, tm=512):
    t, d = h.shape
    tm = _tile(t, tm)

    def body(h_ref, w_ref, dn_ref, skip_ref, dh_ref, dw_ref):
        i = pl.program_id(0)
        x = h_ref[...]
        r = lax.rsqrt(jnp.mean(x * x, axis=-1, keepdims=True) + EPS)
        nh = x * r
        g = dn_ref[...].astype(F32)
        gw = g * w_ref[...]
        dh_ref[...] = r * (gw - nh * jnp.mean(gw * nh, axis=-1, keepdims=True)) + skip_ref[...]
        part = jnp.sum(g * nh, axis=0, keepdims=True)

        @pl.when(i == 0)
        def _():
            dw_ref[...] = part

        @pl.when(i > 0)
        def _():
            dw_ref[...] += part

    row = pl.BlockSpec((tm, d), lambda i: (i, 0))
    vec = pl.BlockSpec((1, d), lambda i: (0, 0))
    return pl.pallas_call(
        body,
        name=name,
        grid=(t // tm,),
        in_specs=[row, vec, row, row],
        out_specs=[row, vec],
        out_shape=[jax.ShapeDtypeStruct((t, d), F32), jax.ShapeDtypeStruct((1, d), F32)],
        compiler_params=_params(("arbitrary",)),
    )(h, w.reshape(1, d), dn, skip)


def _final_loss(h, w, target, *, name, tm=512):
    t, d = h.shape
    tm = _tile(t, tm)

    def body(h_ref, w_ref, tg_ref, loss_ref, dh_ref, dw_ref):
        i = pl.program_id(0)
        x = h_ref[...]
        r = lax.rsqrt(jnp.mean(x * x, axis=-1, keepdims=True) + EPS)
        nh = x * r
        err = nh * w_ref[...] - tg_ref[...]
        lpart = (0.5 / d) * jnp.sum(jnp.sum(err * err, axis=-1, keepdims=True), axis=0, keepdims=True)
        g = err * (1.0 / d)
        gw = g * w_ref[...]
        dh_ref[...] = r * (gw - nh * jnp.mean(gw * nh, axis=-1, keepdims=True))
        part = jnp.sum(g * nh, axis=0, keepdims=True)
        lrow = jnp.broadcast_to(lpart, (1, LANES))

        @pl.when(i == 0)
        def _():
            dw_ref[...] = part
            loss_ref[...] = lrow

        @pl.when(i > 0)
        def _():
            dw_ref[...] += part
            loss_ref[...] += lrow

    row = pl.BlockSpec((tm, d), lambda i: (i, 0))
    vec = pl.BlockSpec((1, d), lambda i: (0, 0))
    return pl.pallas_call(
        body,
        name=name,
        grid=(t // tm,),
        in_specs=[row, vec, row],
        out_specs=[pl.BlockSpec((1, LANES), lambda i: (0, 0)), row, vec],
        out_shape=[jax.ShapeDtypeStruct((1, LANES), F32), jax.ShapeDtypeStruct((t, d), F32), jax.ShapeDtypeStruct((1, d), F32)],
        compiler_params=_params(("arbitrary",)),
    )(h, w.reshape(1, d), target)


def _conv_from_ext(ext_ref, cw_ref, kw, tm):
    y = cw_ref[kw - 1:kw, :] * ext_ref[pl.ds(HALO, tm), :]
    for i in range(kw - 1):
        y = y + cw_ref[i:i + 1, :] * ext_ref[pl.ds(HALO - (kw - 1) + i, tm), :]
    return y


ROW_CHUNK = 64


def _conv_rows(src_ref, base, rows, cw_ref, kw):
    y = cw_ref[kw - 1:kw, :] * src_ref[pl.ds(base, rows), :]
    for i in range(kw - 1):
        y = y + cw_ref[i:i + 1, :] * src_ref[pl.ds(base - (kw - 1) + i, rows), :]
    return y


def _conv_t_rows(dy_ref, base, rows, cw_ref, kw):
    dx = cw_ref[kw - 1:kw, :] * dy_ref[pl.ds(base, rows), :]
    for i in range(kw - 1):
        dx = dx + cw_ref[i:i + 1, :] * dy_ref[pl.ds(base + kw - 1 - i, rows), :]
    return dx


def _conv_bwd_from_ext(xext_ref, dyext_ref, cw_ref, dcw_ref, kw, tm, first):
    dy = dyext_ref[pl.ds(0, tm), :]
    dx = cw_ref[kw - 1:kw, :] * dy
    for i in range(kw - 1):
        dx = dx + cw_ref[i:i + 1, :] * dyext_ref[pl.ds(kw - 1 - i, tm), :]
    for i in range(kw):
        part = jnp.sum(dy * xext_ref[pl.ds(HALO - (kw - 1) + i, tm), :], axis=0, keepdims=True)

        @pl.when(first)
        def _():
            dcw_ref[i:i + 1, :] = part

        @pl.when(jnp.logical_not(first))
        def _():
            dcw_ref[i:i + 1, :] += part

    return dx


def _delta_pre_fwd(proj, conv_w, alog_row, dtb_row, *, name, tm=256):
    t = proj.shape[0]
    c3 = 3 * N_HEADS_A * HEAD_DIM_A
    hk = N_HEADS_A * HEAD_DIM_A
    tm = _tile(t, tm)

    def body(x_ref, ba_ref, cw_ref, al_ref, db_ref, q_ref, k_ref, v_ref, g_ref, b_ref, ext):
        i = pl.program_id(0)

        @pl.when(i == 0)
        def _():
            ext[0:HALO, :] = jnp.zeros((HALO, c3), F32)

        ext[pl.ds(HALO, tm), :] = x_ref[...]
        y = _conv_from_ext(ext, cw_ref, CONV_A, tm)
        ext[0:HALO, :] = ext[pl.ds(tm, HALO), :]
        s = y * _sigmoid(y)
        for h in range(N_HEADS_A):
            lo = h * HEAD_DIM_A
            for dst, off in ((q_ref, 0), (k_ref, hk)):
                sh = s[:, off + lo:off + lo + HEAD_DIM_A]
                dst[:, lo:lo + HEAD_DIM_A] = sh * lax.rsqrt(jnp.sum(sh * sh, axis=-1, keepdims=True) + EPS)
        v_ref[...] = s[:, 2 * hk:3 * hk]
        ba = ba_ref[...]
        beta = _sigmoid(ba)
        gfull = -jnp.exp(al_ref[...]) * _softplus(ba + db_ref[...])
        for h in range(N_HEADS_A):
            lo = h * HEAD_DIM_A
            b_ref[:, lo:lo + HEAD_DIM_A] = jnp.broadcast_to(beta[:, h:h + 1], (tm, HEAD_DIM_A))
            g_ref[:, lo:lo + HEAD_DIM_A] = jnp.broadcast_to(gfull[:, N_HEADS_A + h:N_HEADS_A + h + 1], (tm, HEAD_DIM_A))

    row = lambda w: pl.BlockSpec((tm, w), lambda i: (i, 0))
    fixed = lambda r, w: pl.BlockSpec((r, w), lambda i: (0, 0))
    out = jax.ShapeDtypeStruct((t, hk), F32)
    return pl.pallas_call(
        body,
        name=name,
        grid=(t // tm,),
        in_specs=[row(c3), pl.BlockSpec((tm, LANES), lambda i: (i, BA_COL_BLOCK)), fixed(CONV_A, c3), fixed(1, LANES),
                  fixed(1, LANES)],
        out_specs=[row(hk)] * 5,
        out_shape=[out] * 5,
        scratch_shapes=[pltpu.VMEM((HALO + tm, c3), F32)],
        compiler_params=_params(("arbitrary",)),
    )(proj, proj, conv_w, alog_row, dtb_row)


def _delta_pre_bwd(proj, conv_w, alog_row, dtb_row, dq, dk, dv, dg, db, dz, *, name, tm=256):
    t, pw = proj.shape
    c3 = 3 * N_HEADS_A * HEAD_DIM_A
    hk = N_HEADS_A * HEAD_DIM_A
    tm = _tile(t, tm)
    nt = t // tm
    hb = tm // HALO

    def body(x_ref, xp_ref, ba_ref, cw_ref, al_ref, db_ref, dq_ref, dk_ref, dv_ref, dg_ref, dbt_ref, dz_ref,
             dp_ref, dcw_ref, dal_ref, ddb_ref, xext, dyext, carry):
        i = pl.program_id(0)
        first = i == 0
        tile = nt - 1 - i

        @pl.when(tile == 0)
        def _():
            xext[0:HALO, :] = jnp.zeros((HALO, c3), F32)

        @pl.when(tile > 0)
        def _():
            xext[0:HALO, :] = xp_ref[...]

        xext[pl.ds(HALO, tm), :] = x_ref[...]
        y = _conv_from_ext(xext, cw_ref, CONV_A, tm)
        sg = _sigmoid(y)
        s = y * sg
        dsilu = sg * (1.0 + y * (1.0 - sg))
        for h in range(N_HEADS_A):
            lo = h * HEAD_DIM_A
            for src, off in ((dq_ref, 0), (dk_ref, hk)):
                sh = s[:, off + lo:off + lo + HEAD_DIM_A]
                r = lax.rsqrt(jnp.sum(sh * sh, axis=-1, keepdims=True) + EPS)
                qn = sh * r
                gq = src[:, lo:lo + HEAD_DIM_A]
                dsh = r * (gq - qn * jnp.sum(gq * qn, axis=-1, keepdims=True))
                dyext[pl.ds(0, tm), off + lo:off + lo + HEAD_DIM_A] = dsh * dsilu[:, off + lo:off + lo + HEAD_DIM_A]
        dyext[pl.ds(0, tm), 2 * hk:3 * hk] = dv_ref[...] * dsilu[:, 2 * hk:3 * hk]

        @pl.when(first)
        def _():
            dyext[pl.ds(tm, HALO), :] = jnp.zeros((HALO, c3), F32)

        @pl.when(jnp.logical_not(first))
        def _():
            dyext[pl.ds(tm, HALO), :] = carry[...]

        dx = _conv_bwd_from_ext(xext, dyext, cw_ref, dcw_ref, CONV_A, tm, first)
        carry[...] = dyext[0:HALO, :]
        dp_ref[:, 0:c3] = dx.astype(BF16)
        dp_ref[:, c3:c3 + hk] = dz_ref[...]

        lane = lax.broadcasted_iota(jnp.int32, (tm, LANES), 1)
        gcol = jnp.zeros((tm, LANES), F32)
        for h in range(N_HEADS_A):
            lo = h * HEAD_DIM_A
            dbh = jnp.sum(dbt_ref[:, lo:lo + HEAD_DIM_A], axis=-1, keepdims=True)
            dgh = jnp.sum(dg_ref[:, lo:lo + HEAD_DIM_A], axis=-1, keepdims=True)
            gcol = gcol + jnp.where(lane == h, dbh, 0.0) + jnp.where(lane == N_HEADS_A + h, dgh, 0.0)
        ba = ba_ref[...]
        beta = _sigmoid(ba)
        a_neg = -jnp.exp(al_ref[...])
        z = ba + db_ref[...]
        dz = gcol * a_neg * _sigmoid(z)
        is_g = jnp.logical_and(lane >= N_HEADS_A, lane < 2 * N_HEADS_A)
        dba = jnp.where(lane < N_HEADS_A, gcol * beta * (1.0 - beta), jnp.where(is_g, dz, 0.0))
        dp_ref[:, c3 + hk:pw] = dba.astype(BF16)
        dal = jnp.sum(jnp.where(is_g, gcol * a_neg * _softplus(z), 0.0), axis=0, keepdims=True)
        ddb = jnp.sum(jnp.where(is_g, dz, 0.0), axis=0, keepdims=True)

        @pl.when(first)
        def _():
            dal_ref[...] = dal
            ddb_ref[...] = ddb

        @pl.when(jnp.logical_not(first))
        def _():
            dal_ref[...] += dal
            ddb_ref[...] += ddb

    rev = lambda w: pl.BlockSpec((tm, w), lambda i: (nt - 1 - i, 0))
    prev = pl.BlockSpec((HALO, c3), lambda i: (jnp.maximum((nt - 1 - i) * hb - 1, 0), 0))
    fixed = lambda r, w: pl.BlockSpec((r, w), lambda i: (0, 0))
    return pl.pallas_call(
        body,
        name=name,
        grid=(nt,),
        in_specs=[rev(c3), prev, pl.BlockSpec((tm, LANES), lambda i: (nt - 1 - i, BA_COL_BLOCK)), fixed(CONV_A, c3),
                  fixed(1, LANES), fixed(1, LANES)] + [rev(hk)] * 6,
        out_specs=[rev(pw), fixed(CONV_A, c3), fixed(1, LANES), fixed(1, LANES)],
        out_shape=[jax.ShapeDtypeStruct((t, pw), BF16), jax.ShapeDtypeStruct((CONV_A, c3), F32),
                   jax.ShapeDtypeStruct((1, LANES), F32), jax.ShapeDtypeStruct((1, LANES), F32)],
        scratch_shapes=[pltpu.VMEM((HALO + tm, c3), F32), pltpu.VMEM((tm + HALO, c3), F32), pltpu.VMEM((HALO, c3), F32)],
        compiler_params=_params(("arbitrary",)),
    )(proj, proj, proj, conv_w, alog_row, dtb_row, dq, dk, dv, dg, db, dz)


def _gated_norm_fwd(o, proj, w, *, name, tm=512):
    t, d = o.shape
    tm = _tile(t, tm)

    def body(o_ref, z_ref, w_ref, y_ref):
        for h in range(N_HEADS_A):
            sl = slice(h * HEAD_DIM_A, (h + 1) * HEAD_DIM_A)
            oh = o_ref[:, sl]
            zh = z_ref[:, sl]
            r = lax.rsqrt(jnp.mean(oh * oh, axis=-1, keepdims=True) + EPS)
            y_ref[:, sl] = (oh * r * w_ref[...] * (zh * _sigmoid(zh))).astype(BF16)

    row = pl.BlockSpec((tm, d), lambda i: (i, 0))
    return pl.pallas_call(
        body,
        name=name,
        grid=(t // tm,),
        in_specs=[row, pl.BlockSpec((tm, d), lambda i: (i, Z_COL_BLOCK)), pl.BlockSpec((1, HEAD_DIM_A), lambda i: (0, 0))],
        out_specs=row,
        out_shape=jax.ShapeDtypeStruct((t, d), BF16),
        compiler_params=_params(("parallel",)),
    )(o, proj, w)


def _gated_norm_bwd(o, proj, w, dy, *, name, tm=512):
    t, d = o.shape
    tm = _tile(t, tm)

    def body(o_ref, z_ref, w_ref, dy_ref, do_ref, dz_ref, dw_ref):
        i = pl.program_id(0)
        dw = jnp.zeros((1, HEAD_DIM_A), F32)
        for h in range(N_HEADS_A):
            sl = slice(h * HEAD_DIM_A, (h + 1) * HEAD_DIM_A)
            oh = o_ref[:, sl]
            zh = z_ref[:, sl]
            g = dy_ref[:, sl]
            r = lax.rsqrt(jnp.mean(oh * oh, axis=-1, keepdims=True) + EPS)
            nh = oh * r
            sg = _sigmoid(zh)
            dz_ref[:, sl] = (g * nh * w_ref[...] * (sg * (1.0 + zh * (1.0 - sg)))).astype(BF16)
            dt = g * (zh * sg)
            dw = dw + jnp.sum(dt * nh, axis=0, keepdims=True)
            dnh = dt * w_ref[...]
            do_ref[:, sl] = r * (dnh - nh * jnp.mean(dnh * nh, axis=-1, keepdims=True))

        @pl.when(i == 0)
        def _():
            dw_ref[...] = dw

        @pl.when(i > 0)
        def _():
            dw_ref[...] += dw

    row = pl.BlockSpec((tm, d), lambda i: (i, 0))
    vec = pl.BlockSpec((1, HEAD_DIM_A), lambda i: (0, 0))
    return pl.pallas_call(
        body,
        name=name,
        grid=(t // tm,),
        in_specs=[row, pl.BlockSpec((tm, d), lambda i: (i, Z_COL_BLOCK)), vec, row],
        out_specs=[row, row, vec],
        out_shape=[jax.ShapeDtypeStruct((t, d), F32), jax.ShapeDtypeStruct((t, d), BF16),
                   jax.ShapeDtypeStruct((1, HEAD_DIM_A), F32)],
        compiler_params=_params(("arbitrary",)),
    )(o, proj, w, dy)


_NN = (((1,), (0,)), ((), ()))
_NT = (((1,), (1,)), ((), ()))
_TN = (((0,), (0,)), ((), ()))
_DIMS = {"nn": _NN, "nt": _NT, "tn": _TN}


def _raw_dot(a, b, kind, prec):
    dims = _DIMS[kind]
    a_hi, b_hi = a.astype(BF16), b.astype(BF16)
    out = lax.dot_general(a_hi, b_hi, dims, preferred_element_type=F32)
    if prec == "x3":
        a_lo = (a - a_hi.astype(F32)).astype(BF16)
        b_lo = (b - b_hi.astype(F32)).astype(BF16)
        out = out + lax.dot_general(a_hi, b_lo, dims, preferred_element_type=F32)
        out = out + lax.dot_general(a_lo, b_hi, dims, preferred_element_type=F32)
    elif prec == "s3":
        r1 = b - b_hi.astype(F32)
        b_mid = r1.astype(BF16)
        b_lo = (r1 - b_mid.astype(F32)).astype(BF16)
        out = out + lax.dot_general(a_hi, b_mid, dims, preferred_element_type=F32)
        out = out + lax.dot_general(a_hi, b_lo, dims, preferred_element_type=F32)
    return out


def _raw_dots(xs, ys, kind, prec):
    return [_raw_dot(x, y, kind, prec) for x, y in zip(xs, ys)]


@functools.partial(jax.custom_vjp, nondiff_argnums=(2, 3))
def _dots(xs, ys, kind, prec):
    return _raw_dots(xs, ys, kind, prec)


def _dots_fwd(xs, ys, kind, prec):
    return _raw_dots(xs, ys, kind, prec), (xs, ys)


def _dots_bwd(kind, prec, saved, gs):
    xs, ys = saved
    if kind == "nn":
        return _raw_dots(gs, ys, "nt", prec), _raw_dots(xs, gs, "tn", prec)
    if kind == "nt":
        return _raw_dots(gs, ys, "nn", prec), _raw_dots(gs, xs, "tn", prec)
    return _raw_dots(ys, gs, "nt", prec), _raw_dots(xs, gs, "nn", prec)


_dots.defvjp(_dots_fwd, _dots_bwd)


def _eye(c):
    return (lax.broadcasted_iota(jnp.int32, (c, c), 0) == lax.broadcasted_iota(jnp.int32, (c, c), 1)).astype(F32)


def _inv_unit_lower_raw(lmats):
    c = lmats[0].shape[0]
    eye = _eye(c)
    xs = [eye - l for l in lmats]
    ps = lmats
    for _ in range(int(math.log2(c)) - 1):
        ps = _raw_dots(ps, ps, "nn", "bf16")
        xs = [x + d for x, d in zip(xs, _raw_dots(xs, ps, "nn", "bf16"))]
    rs = [x - eye + d for x, d in zip(xs, _raw_dots(lmats, xs, "nn", "x3"))]
    return [x - d for x, d in zip(xs, _raw_dots(xs, rs, "nn", "bf16"))]


@jax.custom_vjp
def _inv_unit_lower(lmats, hints):
    return _inv_unit_lower_raw(lmats) if hints is None else hints


def _inv_fwd(lmats, hints):
    tms = _inv_unit_lower_raw(lmats) if hints is None else hints
    return tms, (tms, hints)


def _inv_bwd(saved, gs):
    tms, hints = saved
    ds = [-d for d in _raw_dots(_raw_dots(tms, gs, "tn", "x3"), tms, "nt", "x3")]
    return ds, (None if hints is None else [jnp.zeros_like(h) for h in hints])


_inv_unit_lower.defvjp(_inv_fwd, _inv_bwd)


def _delta_prep(qs, ks, vs, gs, bs, hints=None):
    c = qs[0].shape[0]
    nh = len(qs)
    ii = lax.broadcasted_iota(jnp.int32, (c, c), 0)
    jj = lax.broadcasted_iota(jnp.int32, (c, c), 1)
    incl = ii >= jj
    strict = ii > jj
    ltri = incl.astype(F32)
    eye = _eye(c)
    m1 = _dots([ltri] * nh, gs, "nn", "s3")
    gtot = [jnp.sum(g, axis=0, keepdims=True) for g in gs]
    decay = [jnp.exp(jnp.where(incl, m - m.T, NEG_BIG)) for m in m1]
    eg = [jnp.exp(m) for m in m1]
    kk = _dots(ks, ks, "nt", "bf16")
    lmats = [jnp.where(strict, b * x * d, 0.0) for b, x, d in zip(bs, kk, decay)]
    tinv = _inv_unit_lower(lmats, hints)
    toff = [t - eye for t in tinv]
    bv = [b * v for b, v in zip(bs, vs)]
    bk = [b * e * k for b, e, k in zip(bs, eg, ks)]
    u0 = [x + d for x, d in zip(bv, _dots(toff, bv, "nn", "bf16"))]
    wk = [x + d for x, d in zip(bk, _dots(toff, bk, "nn", "bf16"))]
    qsc = [q * (HEAD_DIM_A ** -0.5) for q in qs]
    qk = [x * d for x, d in zip(_dots(qsc, ks, "nt", "bf16"), decay)]
    q_dec = [q * e for q, e in zip(qsc, eg)]
    k_dec = [k * jnp.exp(t - m) for k, t, m in zip(ks, gtot, m1)]
    glast = [jnp.broadcast_to(jnp.exp(t), (c, c)) for t in gtot]
    return (u0, wk, qk, q_dec, k_dec, glast), tinv


def _delta_step(ss, u0, wk, qk, q_dec, k_dec, glast):
    us = [a - d for a, d in zip(u0, _dots(wk, ss, "nn", "bf16"))]
    os_ = [a + d for a, d in zip(_dots(q_dec, ss, "nn", "bf16"), _dots(qk, us, "nn", "bf16"))]
    s_new = [g * s + d for g, s, d in zip(glast, ss, _dots(k_dec, us, "tn", "bf16"))]
    return os_, s_new


HEADS_PER_STEP = 8


def _chunk_spec(nc, reverse=False):
    w = HEADS_PER_STEP * HEAD_DIM_A
    if reverse:
        return pl.BlockSpec((CHUNK, w), lambda h, n: (nc - 1 - n, h))
    return pl.BlockSpec((CHUNK, w), lambda h, n: (n, h))


def _head_slices():
    return [slice(j * HEAD_DIM_A, (j + 1) * HEAD_DIM_A) for j in range(HEADS_PER_STEP)]


def _heads(ref):
    return [ref[:, sl] for sl in _head_slices()]


def _delta_prep_fwd(q, k, v, gbc, bbc, *, name):
    t, d = q.shape
    nc = t // CHUNK

    def body(q_ref, k_ref, v_ref, g_ref, b_ref, *outs):
        res, tinv = _delta_prep(*[_heads(r) for r in (q_ref, k_ref, v_ref, g_ref, b_ref)])
        for ref, vals in zip(outs, res + (tinv,)):
            for sl, val in zip(_head_slices(), vals):
                ref[:, sl] = val

    spec = _chunk_spec(nc)
    return pl.pallas_call(
        body,
        name=name,
        grid=(N_HEADS_A // HEADS_PER_STEP, nc),
        in_specs=[spec] * 5,
        out_specs=[spec] * 7,
        out_shape=[jax.ShapeDtypeStruct((t, d), F32)] * 7,
        compiler_params=_params(("parallel", "parallel")),
    )(q, k, v, gbc, bbc)


def _delta_prep_bwd(q, k, v, gbc, bbc, tinv, cts, *, name):
    t, d = q.shape
    nc = t // CHUNK

    def body(q_ref, k_ref, v_ref, g_ref, b_ref, t_ref, c0, c1, c2, c3, c4, c5, *outs):
        def f(q_, k_, v_, g_, b_):
            return _delta_prep(q_, k_, v_, g_, b_, hints=_heads(t_ref))[0]

        _, vjp = jax.vjp(f, *[_heads(r) for r in (q_ref, k_ref, v_ref, g_ref, b_ref)])
        grads = vjp(tuple(_heads(c) for c in (c0, c1, c2, c3, c4, c5)))
        for ref, vals in zip(outs, grads):
            for sl, val in zip(_head_slices(), vals):
                ref[:, sl] = val

    spec = _chunk_spec(nc)
    return pl.pallas_call(
        body,
        name=name,
        grid=(N_HEADS_A // HEADS_PER_STEP, nc),
        in_specs=[spec] * 12,
        out_specs=[spec] * 5,
        out_shape=[jax.ShapeDtypeStruct((t, d), F32)] * 5,
        compiler_params=_params(("parallel", "parallel")),
    )(q, k, v, gbc, bbc, tinv, *cts)


def _delta_scan_fwd(prep, *, name):
    t, d = prep[0].shape
    nc = t // CHUNK

    def body(u0, wk, qk, qd, kd, gl, o_ref, st_ref, s_ref):
        n = pl.program_id(1)

        @pl.when(n == 0)
        def _():
            s_ref[...] = jnp.zeros(s_ref.shape, F32)

        ss = [s_ref[j] for j in range(HEADS_PER_STEP)]
        os_, s_new = _delta_step(ss, *[_heads(r) for r in (u0, wk, qk, qd, kd, gl)])
        for j, sl in enumerate(_head_slices()):
            st_ref[:, sl] = ss[j]
            o_ref[:, sl] = os_[j]
            s_ref[j] = s_new[j]

    spec = _chunk_spec(nc)
    return pl.pallas_call(
        body,
        name=name,
        grid=(N_HEADS_A // HEADS_PER_STEP, nc),
        in_specs=[spec] * 6,
        out_specs=[spec] * 2,
        out_shape=[jax.ShapeDtypeStruct((t, d), F32)] * 2,
        scratch_shapes=[pltpu.VMEM((HEADS_PER_STEP, HEAD_DIM_A, HEAD_DIM_A), F32)],
        compiler_params=_params(("parallel", "arbitrary")),
    )(*prep)


def _delta_scan_bwd(prep, states, do, *, name):
    t, d = do.shape
    nc = t // CHUNK

    def body(u0, wk, qk, qd, kd, gl, st_ref, do_ref, *rest):
        outs, ds_ref = rest[:6], rest[6]
        n = pl.program_id(1)

        @pl.when(n == 0)
        def _():
            ds_ref[...] = jnp.zeros(ds_ref.shape, F32)

        _, vjp = jax.vjp(_delta_step, *[_heads(r) for r in (st_ref, u0, wk, qk, qd, kd, gl)])
        grads = vjp((_heads(do_ref), [ds_ref[j] for j in range(HEADS_PER_STEP)]))
        for j, sl in enumerate(_head_slices()):
            ds_ref[j] = grads[0][j]
            for ref, vals in zip(outs, grads[1:]):
                ref[:, sl] = vals[j]

    spec = _chunk_spec(nc, reverse=True)
    return pl.pallas_call(
        body,
        name=name,
        grid=(N_HEADS_A // HEADS_PER_STEP, nc),
        in_specs=[spec] * 8,
        out_specs=[spec] * 6,
        out_shape=[jax.ShapeDtypeStruct((t, d), F32)] * 6,
        scratch_shapes=[pltpu.VMEM((HEADS_PER_STEP, HEAD_DIM_A, HEAD_DIM_A), F32)],
        compiler_params=_params(("parallel", "arbitrary")),
    )(*prep, states, do)


def _alibi_slope(h):
    return 2.0 ** (-8.0 * (h + 1) / N_HEADS_B)


def _swa_load(sink_ref, q_ref, kp_ref, kc_ref, vp_ref, vc_ref):
    rg = lax.broadcasted_iota(jnp.int32, (GROUP_B * WINDOW, 1), 0) // WINDOW
    q4s, kcats, vcats, slopes, sinkcols = [], [], [], [], []
    for hk in range(N_KV_B):
        ks = slice(hk * HEAD_DIM_B, (hk + 1) * HEAD_DIM_B)
        heads = [hk * GROUP_B + g for g in range(GROUP_B)]
        q4s.append(jnp.concatenate([q_ref[:, h * HEAD_DIM_B:(h + 1) * HEAD_DIM_B] for h in heads], axis=0).astype(BF16))
        kcats.append(jnp.concatenate([kp_ref[:, ks], kc_ref[:, ks]], axis=0).astype(BF16))
        vcats.append(jnp.concatenate([vp_ref[:, ks], vc_ref[:, ks]], axis=0).astype(BF16))
        slope = jnp.zeros((GROUP_B * WINDOW, 1), F32)
        sink = jnp.zeros((GROUP_B * WINDOW, 1), F32)
        for g, h in enumerate(heads):
            slope = jnp.where(rg == g, _alibi_slope(h), slope)
            sink = jnp.where(rg == g, sink_ref[0, h], sink)
        slopes.append(slope)
        sinkcols.append(sink)
    return q4s, kcats, vcats, slopes, sinkcols


def _swa_probs(q4s, kcats, slopes, sinkcols, blk):
    rows = GROUP_B * WINDOW
    qi = lax.broadcasted_iota(jnp.int32, (rows, 2 * WINDOW), 0) % WINDOW
    kj = lax.broadcasted_iota(jnp.int32, (rows, 2 * WINDOW), 1)
    dist = qi + WINDOW - kj
    valid = (dist >= 0) & (dist < WINDOW) & (blk * WINDOW - WINDOW + kj >= 0)
    distf = dist.astype(F32)
    ss = [lax.dot_general(q, kc, _NT, preferred_element_type=F32) for q, kc in zip(q4s, kcats)]
    logits = [jnp.where(valid, s * (HEAD_DIM_B ** -0.5) - sl * distf, NEG_BIG) for s, sl in zip(ss, slopes)]
    ms = [jnp.maximum(jnp.max(l, axis=-1, keepdims=True), sk) for l, sk in zip(logits, sinkcols)]
    es = [jnp.exp(l - m) for l, m in zip(logits, ms)]
    esk = [jnp.exp(sk - m) for sk, m in zip(sinkcols, ms)]
    invs = [1.0 / (jnp.sum(e, axis=-1, keepdims=True) + k) for e, k in zip(es, esk)]
    return [e * i for e, i in zip(es, invs)], [k * i for k, i in zip(esk, invs)]


def _swa_fwd(proj, sinks, *, name):
    t = proj.shape[0]
    nb = t // WINDOW
    qd = N_HEADS_B * HEAD_DIM_B
    kd = N_KV_B * HEAD_DIM_B

    def body(sink_ref, q_ref, kp_ref, kc_ref, vp_ref, vc_ref, o_ref):
        blk = pl.program_id(0)
        q4s, kcats, vcats, slopes, sinkcols = _swa_load(sink_ref, q_ref, kp_ref, kc_ref, vp_ref, vc_ref)
        ps, _ = _swa_probs(q4s, kcats, slopes, sinkcols, blk)
        outs = [jnp.dot(p.astype(BF16), vc, preferred_element_type=F32) for p, vc in zip(ps, vcats)]
        for hk, out in enumerate(outs):
            for g in range(GROUP_B):
                h = hk * GROUP_B + g
                o_ref[:, h * HEAD_DIM_B:(h + 1) * HEAD_DIM_B] = out[g * WINDOW:(g + 1) * WINDOW, :].astype(BF16)

    q_spec = pl.BlockSpec((WINDOW, qd), lambda i: (i, 0))
    kv = lambda col, prev: pl.BlockSpec((WINDOW, kd), (lambda i: (jnp.maximum(i - 1, 0), col)) if prev else (lambda i: (i, col)))
    kcol, vcol = qd // kd, qd // kd + 1
    return pl.pallas_call(
        body,
        name=name,
        grid=(nb,),
        in_specs=[pl.BlockSpec(memory_space=pltpu.SMEM), q_spec, kv(kcol, True), kv(kcol, False), kv(vcol, True), kv(vcol, False)],
        out_specs=q_spec,
        out_shape=jax.ShapeDtypeStruct((t, qd), BF16),
        compiler_params=_params(("parallel",)),
    )(sinks, proj, proj, proj, proj, proj)


def _swa_bwd(proj, sinks, dout, *, name):
    t = proj.shape[0]
    nb = t // WINDOW
    qd = N_HEADS_B * HEAD_DIM_B
    kd = N_KV_B * HEAD_DIM_B
    scale = HEAD_DIM_B ** -0.5

    def body(sink_ref, q_ref, kp_ref, kc_ref, vp_ref, vc_ref, do_ref, dq_ref, dk_ref, dv_ref, dsk_ref):
        blk = pl.program_id(0)
        lane = lax.broadcasted_iota(jnp.int32, (1, LANES), 1)

        @pl.when(blk == 0)
        def _():
            dk_ref[...] = jnp.zeros((t, kd), F32)
            dv_ref[...] = jnp.zeros((t, kd), F32)
            dsk_ref[...] = jnp.zeros((1, LANES), F32)

        cur = pl.ds(pl.multiple_of(blk * WINDOW, WINDOW), WINDOW)
        prv = pl.ds(pl.multiple_of(jnp.maximum(blk - 1, 0) * WINDOW, WINDOW), WINDOW)
        q4s, kcats, vcats, slopes, sinkcols = _swa_load(sink_ref, q_ref, kp_ref, kc_ref, vp_ref, vc_ref)
        ps, psinks = _swa_probs(q4s, kcats, slopes, sinkcols, blk)
        do4s = [jnp.concatenate([do_ref[:, (hk * GROUP_B + g) * HEAD_DIM_B:(hk * GROUP_B + g + 1) * HEAD_DIM_B]
                                 for g in range(GROUP_B)], axis=0).astype(BF16) for hk in range(N_KV_B)]
        dps = [lax.dot_general(d, vc, _NT, preferred_element_type=F32) for d, vc in zip(do4s, vcats)]
        deltas = [jnp.sum(p * dp, axis=-1, keepdims=True) for p, dp in zip(ps, dps)]
        dss = [(p * (dp - dl) * scale).astype(BF16) for p, dp, dl in zip(ps, dps, deltas)]
        dq4s = [jnp.dot(ds, kc, preferred_element_type=F32) for ds, kc in zip(dss, kcats)]
        dkcs = [lax.dot_general(ds, q, _TN, preferred_element_type=F32) for ds, q in zip(dss, q4s)]
        dvcs = [lax.dot_general(p.astype(BF16), d, _TN, preferred_element_type=F32) for p, d in zip(ps, do4s)]
        dsk = jnp.zeros((1, LANES), F32)
        for hk in range(N_KV_B):
            ks = slice(hk * HEAD_DIM_B, (hk + 1) * HEAD_DIM_B)
            dsink = -psinks[hk] * deltas[hk]
            for g in range(GROUP_B):
                h = hk * GROUP_B + g
                rows = slice(g * WINDOW, (g + 1) * WINDOW)
                dq_ref[:, h * HEAD_DIM_B:(h + 1) * HEAD_DIM_B] = dq4s[hk][rows, :].astype(BF16)
                dsk = dsk + jnp.where(lane == h, jnp.sum(dsink[rows, :], axis=0, keepdims=True), 0.0)
            dk_ref[cur, ks] += dkcs[hk][WINDOW:, :]
            dv_ref[cur, ks] += dvcs[hk][WINDOW:, :]

            @pl.when(blk > 0)
            def _():
                dk_ref[prv, ks] += dkcs[hk][:WINDOW, :]
                dv_ref[prv, ks] += dvcs[hk][:WINDOW, :]

        dsk_ref[...] += dsk

    q_spec = pl.BlockSpec((WINDOW, qd), lambda i: (i, 0))
    kv = lambda col, prev: pl.BlockSpec((WINDOW, kd), (lambda i: (jnp.maximum(i - 1, 0), col)) if prev else (lambda i: (i, col)))
    kcol, vcol = qd // kd, qd // kd + 1
    full = pl.BlockSpec((t, kd), lambda i: (0, 0))
    return pl.pallas_call(
        body,
        name=name,
        grid=(nb,),
        in_specs=[pl.BlockSpec(memory_space=pltpu.SMEM), q_spec, kv(kcol, True), kv(kcol, False), kv(vcol, True), kv(vcol, False), q_spec],
        out_specs=[q_spec, full, full, pl.BlockSpec((1, LANES), lambda i: (0, 0))],
        out_shape=[jax.ShapeDtypeStruct((t, qd), BF16), jax.ShapeDtypeStruct((t, kd), F32),
                   jax.ShapeDtypeStruct((t, kd), F32), jax.ShapeDtypeStruct((1, LANES), F32)],
        compiler_params=_params(("arbitrary",)),
    )(sinks, proj, proj, proj, proj, proj, dout)


def _ffn_act_fwd(up, cw, *, name, tm=512, cb=256):
    _, t, f = up.shape
    tm, cb = _tile(t, tm), _tile(f, cb)

    rc = min(ROW_CHUNK, tm)

    def body(ug_ref, uv_ref, cg_ref, cv_ref, a_ref, hg, hv):
        i = pl.program_id(1)

        @pl.when(i == 0)
        def _():
            hg[0:HALO, :] = jnp.zeros((HALO, cb), F32)
            hv[0:HALO, :] = jnp.zeros((HALO, cb), F32)

        hg[pl.ds(HALO, rc), :] = ug_ref[0:rc, :]
        hv[pl.ds(HALO, rc), :] = uv_ref[0:rc, :]
        for r in range(tm // rc):
            if r == 0:
                yg = _conv_rows(hg, HALO, rc, cg_ref, FFN_CONV)
                yv = _conv_rows(hv, HALO, rc, cv_ref, FFN_CONV)
            else:
                yg = _conv_rows(ug_ref, r * rc, rc, cg_ref, FFN_CONV)
                yv = _conv_rows(uv_ref, r * rc, rc, cv_ref, FFN_CONV)
            a_ref[r * rc:(r + 1) * rc, :] = (yg * _sigmoid(yg) * yv).astype(BF16)
        hg[0:HALO, :] = ug_ref[tm - HALO:tm, :]
        hv[0:HALO, :] = uv_ref[tm - HALO:tm, :]

    ncb = f // cb
    half = lambda s: pl.BlockSpec((None, tm, cb), lambda c, i: (s, i, c))
    taps = lambda s: pl.BlockSpec((FFN_CONV, cb), lambda c, i: (0, c + s * ncb))
    return pl.pallas_call(
        body,
        name=name,
        grid=(ncb, t // tm),
        in_specs=[half(0), half(1), taps(0), taps(1)],
        out_specs=pl.BlockSpec((tm, cb), lambda c, i: (i, c)),
        out_shape=jax.ShapeDtypeStruct((t, f), BF16),
        scratch_shapes=[pltpu.VMEM((HALO + rc, cb), F32)] * 2,
        compiler_params=_params(("parallel", "arbitrary")),
    )(up, up, cw, cw)


def _ffn_act_bwd(up, cw, dact, *, name, tm=512, cb=256):
    _, t, f = up.shape
    tm, cb = _tile(t, tm), _tile(f, cb)
    nt = t // tm
    hb = tm // HALO

    rc = min(ROW_CHUNK, tm)
    nr = tm // rc
    kw = FFN_CONV

    def body(ug_ref, uv_ref, pg_ref, pv_ref, cg_ref, cv_ref, da_ref, du_ref, dcg_ref, dcv_ref,
             hg, hv, dyg, dyv):
        i = pl.program_id(1)
        first = i == 0
        tile = nt - 1 - i

        @pl.when(tile == 0)
        def _():
            hg[0:HALO, :] = jnp.zeros((HALO, cb), F32)
            hv[0:HALO, :] = jnp.zeros((HALO, cb), F32)

        @pl.when(tile > 0)
        def _():
            hg[0:HALO, :] = pg_ref[...]
            hv[0:HALO, :] = pv_ref[...]

        @pl.when(first)
        def _():
            dyg[pl.ds(tm, HALO), :] = jnp.zeros((HALO, cb), F32)
            dyv[pl.ds(tm, HALO), :] = jnp.zeros((HALO, cb), F32)

        hg[pl.ds(HALO, rc), :] = ug_ref[0:rc, :]
        hv[pl.ds(HALO, rc), :] = uv_ref[0:rc, :]
        dcg = [jnp.zeros((1, cb), F32) for _ in range(kw)]
        dcv = [jnp.zeros((1, cb), F32) for _ in range(kw)]
        for r in reversed(range(nr)):
            rows = slice(r * rc, (r + 1) * rc)
            src_g, src_v, base = (hg, hv, HALO) if r == 0 else (ug_ref, uv_ref, r * rc)
            yg = _conv_rows(src_g, base, rc, cg_ref, kw)
            yv = _conv_rows(src_v, base, rc, cv_ref, kw)
            sg = _sigmoid(yg)
            da = da_ref[rows, :]
            dy_g = da * yv * (sg * (1.0 + yg * (1.0 - sg)))
            dy_v = da * (yg * sg)
            dyg[rows, :] = dy_g
            dyv[rows, :] = dy_v
            du_ref[0, rows, :] = _conv_t_rows(dyg, r * rc, rc, cg_ref, kw).astype(BF16)
            du_ref[1, rows, :] = _conv_t_rows(dyv, r * rc, rc, cv_ref, kw).astype(BF16)
            for j in range(kw):
                dcg[j] = dcg[j] + jnp.sum(dy_g * src_g[pl.ds(base - (kw - 1) + j, rc), :], axis=0, keepdims=True)
                dcv[j] = dcv[j] + jnp.sum(dy_v * src_v[pl.ds(base - (kw - 1) + j, rc), :], axis=0, keepdims=True)
        dyg[pl.ds(tm, HALO), :] = dyg[0:HALO, :]
        dyv[pl.ds(tm, HALO), :] = dyv[0:HALO, :]
        for j in range(kw):
            @pl.when(first)
            def _():
                dcg_ref[j:j + 1, :] = dcg[j]
                dcv_ref[j:j + 1, :] = dcv[j]

            @pl.when(jnp.logical_not(first))
            def _():
                dcg_ref[j:j + 1, :] += dcg[j]
                dcv_ref[j:j + 1, :] += dcv[j]

    ncb = f // cb
    half = lambda s: pl.BlockSpec((None, tm, cb), lambda c, i: (s, nt - 1 - i, c))
    prev = lambda s: pl.BlockSpec((None, HALO, cb), lambda c, i: (s, jnp.maximum((nt - 1 - i) * hb - 1, 0), c))
    taps = lambda s: pl.BlockSpec((FFN_CONV, cb), lambda c, i: (0, c + s * ncb))
    dtaps = pl.BlockSpec((FFN_CONV, cb), lambda c, i: (0, c))
    return pl.pallas_call(
        body,
        name=name,
        grid=(ncb, nt),
        in_specs=[half(0), half(1), prev(0), prev(1), taps(0), taps(1), pl.BlockSpec((tm, cb), lambda c, i: (nt - 1 - i, c))],
        out_specs=[pl.BlockSpec((2, tm, cb), lambda c, i: (0, nt - 1 - i, c)), dtaps, dtaps],
        out_shape=[jax.ShapeDtypeStruct((2, t, f), BF16), jax.ShapeDtypeStruct((FFN_CONV, f), F32),
                   jax.ShapeDtypeStruct((FFN_CONV, f), F32)],
        scratch_shapes=[pltpu.VMEM((HALO + rc, cb), F32)] * 2 + [pltpu.VMEM((tm + HALO, cb), F32)] * 2,
        compiler_params=_params(("parallel", "arbitrary")),
    )(up, up, up, up, cw, cw, dact)


def _ple_fwd(h, zg, pe, *, name, tm=512):
    t, d = h.shape
    tm = _tile(t, tm)

    def body(h_ref, z_ref, p_ref, o_ref):
        o_ref[...] = h_ref[...] + _sigmoid(z_ref[...]) * p_ref[...]

    row = pl.BlockSpec((tm, d), lambda i: (i, 0))
    return pl.pallas_call(
        body, name=name, grid=(t // tm,), in_specs=[row] * 3, out_specs=row,
        out_shape=jax.ShapeDtypeStruct((t, d), F32), compiler_params=_params(("parallel",)),
    )(h, zg, pe)


def _ple_bwd(dh, zg, pe, *, name, tm=512):
    t, d = dh.shape
    tm = _tile(t, tm)

    def body(g_ref, z_ref, p_ref, dz_ref, dp_ref):
        g = g_ref[...]
        sg = _sigmoid(z_ref[...])
        dz_ref[...] = (g * p_ref[...] * sg * (1.0 - sg)).astype(BF16)
        dp_ref[...] = (g * sg).astype(BF16)

    row = pl.BlockSpec((tm, d), lambda i: (i, 0))
    return pl.pallas_call(
        body, name=name, grid=(t // tm,), in_specs=[row] * 3, out_specs=[row] * 2,
        out_shape=[jax.ShapeDtypeStruct((t, d), BF16)] * 2, compiler_params=_params(("parallel",)),
    )(dh, zg, pe)


def _my_pos():
    return lax.axis_index("x"), lax.axis_index("y"), lax.axis_index("c")


def _all_gather(block, *, name, dep=None):
    r, w = block.shape
    has_dep = dep is not None

    def body(*refs):
        x_ref, out_ref, send_sems, recv_sems, local_sem = refs[:1] + refs[1 + has_dep:]
        x, y, c = _my_pos()
        me, sibling = (x, y, c), (x, y, 1 - c)
        chips = [(1 - x, y), (x, 1 - y), (1 - x, 1 - y)]

        def slot(px, py, pc):
            return out_ref.at[4 * px + 2 * py + pc]

        def copy(k, blk, to, src=None):
            return pltpu.make_async_remote_copy(
                src_ref=slot(*blk) if src is None else src, dst_ref=slot(*blk),
                send_sem=send_sems.at[k], recv_sem=recv_sems.at[k],
                device_id=to, device_id_type=pl.DeviceIdType.MESH)

        mine = pltpu.make_async_copy(x_ref, slot(*me), local_sem)
        mine.start()
        first = [copy(0, me, sibling, src=x_ref)]
        first += [copy(1 + j, me, (*chip, c), src=x_ref) for j, chip in enumerate(chips)]
        for cp in first:
            cp.start()
        passed = [copy(4 + j, (*chip, c), sibling) for j, chip in enumerate(chips)]
        for j, chip in enumerate(chips):
            copy(1 + j, (*chip, c), me).wait_recv()
            passed[j].start()
        copy(0, sibling, me).wait_recv()
        for j, chip in enumerate(chips):
            copy(4 + j, (*chip, 1 - c), me).wait_recv()
        for cp in first + passed:
            cp.wait_send()
        mine.wait()

    return pl.pallas_call(
        body,
        name=name,
        out_shape=jax.ShapeDtypeStruct((N_DEV, r, w), block.dtype),
        in_specs=[pl.BlockSpec(memory_space=pl.ANY)] * (1 + has_dep),
        out_specs=pl.BlockSpec(memory_space=pl.ANY),
        scratch_shapes=[pltpu.SemaphoreType.DMA((7,)), pltpu.SemaphoreType.DMA((7,)), pltpu.SemaphoreType.DMA],
    )(*((block, dep) if has_dep else (block,)))


def _all_to_all(slabs, *, name):
    n, r, w = slabs.shape

    def body(x_ref, out_ref, send_sems, recv_sems, local_sem):
        x, y, c = _my_pos()
        my_idx = 4 * x + 2 * y + c
        mine = pltpu.make_async_copy(x_ref.at[my_idx], out_ref.at[my_idx], local_sem)
        mine.start()
        copies = []
        for k in range(1, N_DEV):
            fx, fy, fc = (k >> 2) & 1, (k >> 1) & 1, k & 1
            px = (1 - x) if fx else x
            py = (1 - y) if fy else y
            pc = (1 - c) if fc else c
            cp = pltpu.make_async_remote_copy(
                src_ref=x_ref.at[4 * px + 2 * py + pc], dst_ref=out_ref.at[my_idx],
                send_sem=send_sems.at[k - 1], recv_sem=recv_sems.at[k - 1],
                device_id=(px, py, pc), device_id_type=pl.DeviceIdType.MESH)
            cp.start()
            copies.append(cp)
        for cp in copies:
            cp.wait_recv()
        for cp in copies:
            cp.wait_send()
        mine.wait()

    return pl.pallas_call(
        body,
        name=name,
        out_shape=jax.ShapeDtypeStruct((n, r, w), slabs.dtype),
        in_specs=[pl.BlockSpec(memory_space=pl.ANY)],
        out_specs=pl.BlockSpec(memory_space=pl.ANY),
        scratch_shapes=[pltpu.SemaphoreType.DMA((7,)), pltpu.SemaphoreType.DMA((7,)), pltpu.SemaphoreType.DMA],
    )(slabs)


def _exchange_copies(scatter, src_refs, land_refs, send_sems, recv_sems, local_sems):
    x, y, c = _my_pos()
    me = 4 * x + 2 * y + c
    local, remote = [], []
    for i, (s, l) in enumerate(zip(src_refs, land_refs)):
        local.append(pltpu.make_async_copy(s.at[me] if scatter else s, l.at[me], local_sems.at[i]))
        for k in range(1, N_DEV):
            px = (1 - x) if (k >> 2) & 1 else x
            py = (1 - y) if (k >> 1) & 1 else y
            pc = (1 - c) if k & 1 else c
            remote.append(pltpu.make_async_remote_copy(
                src_ref=s.at[4 * px + 2 * py + pc] if scatter else s, dst_ref=l.at[me],
                send_sem=send_sems.at[(N_DEV - 1) * i + k - 1], recv_sem=recv_sems.at[(N_DEV - 1) * i + k - 1],
                device_id=(px, py, pc), device_id_type=pl.DeviceIdType.MESH))
    return local, remote


def _exchange(arrays, *, scatter, name):
    n = len(arrays)

    def body(*refs):
        srcs, lands = refs[:n], refs[n:2 * n]
        local, remote = _exchange_copies(scatter, srcs, lands, *refs[2 * n:])
        for cp in local + remote:
            cp.start()
        for cp in remote:
            cp.wait_recv()
        for cp in remote:
            cp.wait_send()
        for cp in local:
            cp.wait()

    hbm = pl.BlockSpec(memory_space=pl.ANY)
    out = pl.pallas_call(
        body,
        name=name,
        out_shape=[jax.ShapeDtypeStruct(a.shape if scatter else (N_DEV,) + a.shape, a.dtype) for a in arrays],
        in_specs=[hbm] * n,
        out_specs=[hbm] * n,
        scratch_shapes=[pltpu.SemaphoreType.DMA(((N_DEV - 1) * n,)), pltpu.SemaphoreType.DMA(((N_DEV - 1) * n,)),
                        pltpu.SemaphoreType.DMA((n,))],
    )(*arrays)
    return list(out)


_HBM_SPEC = pl.BlockSpec(memory_space=pltpu.HBM)
_SEM_SPEC = pl.BlockSpec(memory_space=pltpu.SEMAPHORE)
_EFFECT = pltpu.SideEffectType.DATAFLOW_SIDE_EFFECTING


def _exchange_start(arrays, *, scatter, name, dep):
    n = len(arrays)
    srcs = [pltpu.with_memory_space_constraint(a, pltpu.HBM) for a in arrays]
    lands = [pltpu.with_memory_space_constraint(lax.empty(a.shape if scatter else (N_DEV,) + a.shape, a.dtype), pltpu.HBM)
             for a in arrays]

    def body(*refs):
        src_refs, land_refs = refs[:n], refs[n:2 * n]
        send_sems, recv_sems, local_sems = refs[2 * n + 1:2 * n + 4]
        token = refs[-1]
        local, remote = _exchange_copies(scatter, src_refs, land_refs, send_sems, recv_sems, local_sems)
        for cp in local + remote:
            cp.start()
        token[...] = jnp.zeros_like(token)

    sems = (pltpu.SemaphoreType.DMA(((N_DEV - 1) * n,)), pltpu.SemaphoreType.DMA(((N_DEV - 1) * n,)),
            pltpu.SemaphoreType.DMA((n,)))
    out = pl.pallas_call(
        body,
        name=name,
        out_shape=sems + tuple(pltpu.HBM(a.shape, a.dtype) for a in srcs + lands) + (jax.ShapeDtypeStruct((8, LANES), F32),),
        in_specs=[_HBM_SPEC] * (2 * n) + [pl.BlockSpec(memory_space=pl.ANY)],
        out_specs=(_SEM_SPEC,) * 3 + (_HBM_SPEC,) * (2 * n) + (pl.BlockSpec(memory_space=pltpu.VMEM),),
        input_output_aliases={i: 3 + i for i in range(2 * n)},
        compiler_params=pltpu.CompilerParams(has_side_effects=_EFFECT),
    )(*srcs, *lands, dep)
    return (out[:3], list(out[3:3 + n]), list(out[3 + n:3 + 2 * n])), out[-1]


def _exchange_wait(handle, after, *, scatter, name):
    sems, srcs, lands = handle
    n = len(srcs)

    def body(*refs):
        src_refs, land_refs = refs[:n], refs[n:2 * n]
        send_sems, recv_sems, local_sems = refs[2 * n:2 * n + 3]
        local, remote = _exchange_copies(scatter, src_refs, land_refs, send_sems, recv_sems, local_sems)
        for cp in remote:
            cp.wait_send()
            cp.wait_recv()
        for cp in local:
            cp.wait()

    out = pl.pallas_call(
        body,
        name=name,
        out_shape=tuple(pltpu.HBM(a.shape, a.dtype) for a in srcs + lands),
        in_specs=[_HBM_SPEC] * (2 * n) + [_SEM_SPEC] * 3 + [pl.BlockSpec(memory_space=pl.ANY)],
        out_specs=(_HBM_SPEC,) * (2 * n),
        input_output_aliases={i: i for i in range(2 * n)},
        compiler_params=pltpu.CompilerParams(has_side_effects=_EFFECT),
    )(*srcs, *lands, *sems, after)
    return list(out[n:])


def _sum_parts(parts, *, name, tr=512):
    n, r, lanes = parts.shape
    tr = tr if (r % tr == 0 and r > 1024) else r

    def body(p_ref, g_ref):
        g = p_ref[0].astype(F32)
        for j in range(1, n):
            g = g + p_ref[j].astype(F32)
        g_ref[...] = g

    row = pl.BlockSpec((tr, lanes), lambda i: (i, 0))
    return pl.pallas_call(
        body,
        name=name,
        grid=(r // tr,),
        in_specs=[pl.BlockSpec((n, tr, lanes), lambda i: (0, i, 0))],
        out_specs=row,
        out_shape=jax.ShapeDtypeStruct((r, lanes), F32),
        compiler_params=_params(("parallel",)),
    )(parts)


def _adamw_update(g, w, m, v):
    c1 = 1.0 / (1.0 - ADAM_B1 ** ADAM_STEP)
    c2 = 1.0 / (1.0 - ADAM_B2 ** ADAM_STEP)
    nm = ADAM_B1 * m + (1.0 - ADAM_B1) * g
    nv = ADAM_B2 * v + (1.0 - ADAM_B2) * (g * g)
    return -ADAM_LR * ((nm * c1) / (jnp.sqrt(nv * c2) + ADAM_EPS) + ADAM_WD * w), nm, nv


def _adamw_layer(g, w, m, v, layer, prev, *, name):
    nl, k, n = w.shape
    tr = max(d for d in range(8, min(k, 256) + 1, 8) if k % d == 0)
    in_parts = g.ndim == 3

    def body(g_ref, w_ref, m_ref, v_ref, *rest):
        go_ref, d_ref, nm_ref, nv_ref = rest[-4:]
        if in_parts:
            gg = g_ref[0].astype(F32)
            for j in range(1, g_ref.shape[0]):
                gg = gg + g_ref[j].astype(F32)
        else:
            gg = g_ref[...]
        d, nm, nv = _adamw_update(gg, w_ref[...], m_ref[...], v_ref[...])
        go_ref[...] = gg
        d_ref[...] = d
        nm_ref[...] = nm
        nv_ref[...] = nv

    lay = pl.BlockSpec((None, tr, n), lambda i: (layer, i, 0))
    n_prev = 0 if prev is None else 4
    out = jax.ShapeDtypeStruct((nl, k, n), F32)
    return pl.pallas_call(
        body,
        name=name,
        grid=(k // tr,),
        in_specs=[pl.BlockSpec((g.shape[0], tr, n), lambda i: (0, i, 0)) if in_parts else pl.BlockSpec((tr, n), lambda i: (i, 0)),
                  lay, lay, lay] + [pl.BlockSpec(memory_space=pl.ANY)] * n_prev,
        out_specs=[lay] * 4,
        out_shape=[out] * 4,
        input_output_aliases={4 + j: j for j in range(n_prev)},
        compiler_params=_params(("parallel",)),
    )(g, w, m, v, *(prev or ()))


def _adamw_packed(g, w, m, v, *, name, tr=512):
    r, lanes = g.shape
    tr = tr if r % tr == 0 else r
    c1 = 1.0 / (1.0 - ADAM_B1 ** ADAM_STEP)
    c2 = 1.0 / (1.0 - ADAM_B2 ** ADAM_STEP)

    def body(g_ref, w_ref, m_ref, v_ref, d_ref, nm_ref, nv_ref):
        g = g_ref[...]
        nm = ADAM_B1 * m_ref[...] + (1.0 - ADAM_B1) * g
        nv = ADAM_B2 * v_ref[...] + (1.0 - ADAM_B2) * (g * g)
        nm_ref[...] = nm
        nv_ref[...] = nv
        d_ref[...] = -ADAM_LR * ((nm * c1) / (jnp.sqrt(nv * c2) + ADAM_EPS) + ADAM_WD * w_ref[...])

    row = pl.BlockSpec((tr, lanes), lambda i: (i, 0))
    out = jax.ShapeDtypeStruct((r, lanes), F32)
    return pl.pallas_call(
        body,
        name=name,
        grid=(r // tr,),
        in_specs=[row] * 4,
        out_specs=[row] * 3,
        out_shape=[out] * 3,
        compiler_params=_params(("parallel",)),
    )(g, w, m, v)


BIG = ("a_w_in", "a_w_out", "b_w_in", "b_w_out", "f_w_up", "f_w_down", "ple_w_proj", "ple_w_gate")
CONVS = ("a_conv", "f_conv")
SMALL = ("norm_mix", "norm_ffn", "norm_ple", "norm_final", "a_log", "a_dt_bias", "a_norm", "b_sinks")
WEIGHTS = ("norm_mix", "norm_ffn", "norm_ple", "norm_final", "a_w_in", "a_conv", "a_log", "a_dt_bias", "a_norm",
           "a_w_out", "b_w_in", "b_sinks", "b_w_out", "f_w_up", "f_conv", "f_w_down", "ple_w_proj", "ple_w_gate")
SLAB_ROW_MULTIPLE = 512


def _pack(arrs, dtype, row_multiple):
    flat = jnp.concatenate([a.reshape(-1).astype(dtype) for a in arrs])
    rows = -(-flat.shape[0] // LANES)
    rows = -(-rows // row_multiple) * row_multiple
    return jnp.pad(flat, (0, rows * LANES - flat.shape[0])).reshape(rows, LANES)


def _unpack(slab, shapes):
    lead = slab.shape[:-2]
    flat = slab.reshape(lead + (-1,))
    out, off = [], 0
    for s in shapes:
        size = math.prod(s)
        out.append(flat[..., off:off + size].reshape(lead + tuple(s)))
        off += size
    return out


def _cols_full(g):
    g = jnp.moveaxis(g, 0, -2)
    return g.reshape(g.shape[:-2] + (g.shape[-2] * g.shape[-1],))


def _rows_full(g):
    g = jnp.moveaxis(g, 0, -3)
    return g.reshape(g.shape[:-3] + (g.shape[-3] * g.shape[-2], g.shape[-1]))


def _cols_split(wfull):
    n = wfull.shape[-1] // N_DEV
    g = wfull.reshape(wfull.shape[:-1] + (N_DEV, n))
    return jnp.moveaxis(g, -2, 0)


def _rows_split(wfull):
    k = wfull.shape[-2] // N_DEV
    g = wfull.reshape(wfull.shape[:-2] + (N_DEV, k, wfull.shape[-1]))
    return jnp.moveaxis(g, -3, 0)


TRANSPOSED = ("a_w_in", "b_w_in", "f_w_up", "ple_w_proj")


def _wire(name, a):
    return jnp.swapaxes(a, -1, -2) if name in TRANSPOSED else a


def _wire_shape(name, shape):
    return shape[:-2] + (shape[-1], shape[-2]) if name in TRANSPOSED else tuple(shape)


def _full(name, g):
    return _cols_full(g) if name in CONVS else _rows_full(g)


def _split(name, wfull):
    return _cols_split(wfull) if name in CONVS else _rows_split(wfull)


def _pack_split(grads, names, dtype, row_multiple):
    flat = jnp.concatenate([_split(n, grads[n]).reshape(N_DEV, -1).astype(dtype) for n in names], axis=1)
    rows = -(-flat.shape[1] // LANES)
    rows = -(-rows // row_multiple) * row_multiple
    return jnp.pad(flat, ((0, 0), (0, rows * LANES - flat.shape[1]))).reshape(N_DEV, rows, LANES)


def _pad_cols(a, width):
    return jnp.pad(a, ((0, 0), (0, width - a.shape[1])))


def kernel(x, p, norm_mix, norm_ffn, norm_ple, norm_final, a_w_in, a_conv, a_log, a_dt_bias, a_norm, a_w_out, b_w_in, b_sinks, b_w_out, f_w_up, f_conv, f_w_down, ple_w_proj, ple_w_gate, loss_target, m_norm_mix, m_norm_ffn, m_norm_ple, m_norm_final, m_a_w_in, m_a_conv, m_a_log, m_a_dt_bias, m_a_norm, m_a_w_out, m_b_w_in, m_b_sinks, m_b_w_out, m_f_w_up, m_f_conv, m_f_w_down, m_ple_w_proj, m_ple_w_gate, v_norm_mix, v_norm_ffn, v_norm_ple, v_norm_final, v_a_w_in, v_a_conv, v_a_log, v_a_dt_bias, v_a_norm, v_a_w_out, v_b_w_in, v_b_sinks, v_b_w_out, v_f_w_up, v_f_conv, v_f_w_down, v_ple_w_proj, v_ple_w_gate):
    wts = dict(norm_mix=norm_mix, norm_ffn=norm_ffn, norm_ple=norm_ple, norm_final=norm_final, a_w_in=a_w_in,
               a_conv=a_conv, a_log=a_log, a_dt_bias=a_dt_bias, a_norm=a_norm, a_w_out=a_w_out, b_w_in=b_w_in,
               b_sinks=b_sinks, b_w_out=b_w_out, f_w_up=f_w_up, f_conv=f_conv, f_w_down=f_w_down,
               ple_w_proj=ple_w_proj, ple_w_gate=ple_w_gate)
    mom = dict(norm_mix=m_norm_mix, norm_ffn=m_norm_ffn, norm_ple=m_norm_ple, norm_final=m_norm_final,
               a_w_in=m_a_w_in, a_conv=m_a_conv, a_log=m_a_log, a_dt_bias=m_a_dt_bias, a_norm=m_a_norm,
               a_w_out=m_a_w_out, b_w_in=m_b_w_in, b_sinks=m_b_sinks, b_w_out=m_b_w_out, f_w_up=m_f_w_up,
               f_conv=m_f_conv, f_w_down=m_f_w_down, ple_w_proj=m_ple_w_proj, ple_w_gate=m_ple_w_gate)
    var = dict(norm_mix=v_norm_mix, norm_ffn=v_norm_ffn, norm_ple=v_norm_ple, norm_final=v_norm_final,
               a_w_in=v_a_w_in, a_conv=v_a_conv, a_log=v_a_log, a_dt_bias=v_a_dt_bias, a_norm=v_a_norm,
               a_w_out=v_a_w_out, b_w_in=v_b_w_in, b_sinks=v_b_sinks, b_w_out=v_b_w_out, f_w_up=v_f_w_up,
               f_conv=v_f_conv, f_w_down=v_f_w_down, ple_w_proj=v_ple_w_proj, ple_w_gate=v_ple_w_gate)
    hk = N_HEADS_A * HEAD_DIM_A
    xs = x[0]
    tgt = loss_target[0]
    p_bf = p.astype(BF16)

    def shard(name, layer):
        return _wire(name, wts[name][layer]).astype(BF16)

    def stacked_rows(g):
        return g.reshape(g.shape[0] * g.shape[1], g.shape[2])

    n_in = a_w_in.shape[-1]
    first = _all_gather(jnp.concatenate([shard("a_w_in", 0), shard("a_w_out", 0)]), name="gather_mixer0")
    wa_in_t = jnp.pad(stacked_rows(first[:, :n_in]), ((0, PROJ_A - PROJ_A_REAL), (0, 0)))
    wa_out = stacked_rows(first[:, n_in:])
    gconv = _all_gather(_pack([wts[n] for n in CONVS], F32, 8), dep=first, name="gather_convs")
    conv_full = {n: _cols_full(g) for n, g in zip(CONVS, _unpack(gconv, [wts[n].shape for n in CONVS]))}
    cv_a, cv_f = conv_full["a_conv"][0], conv_full["f_conv"]
    layer_names = ("f_w_up", "f_w_down", "ple_w_proj", "ple_w_gate")
    gather0, tok = _exchange_start([shard(n, 0) for n in layer_names], scatter=False, name="gather_layer0_start", dep=gconv)
    gather1, tok = _exchange_start([shard(n, 0) for n in ("b_w_in", "b_w_out")] + [shard(n, 1) for n in layer_names],
                                   scatter=False, name="gather_layer1_start", dep=tok)

    alog_row = jnp.pad(a_log, ((0, 0), (N_HEADS_A, LANES - 2 * N_HEADS_A)))
    dtb_row = jnp.pad(a_dt_bias, ((0, 0), (N_HEADS_A, LANES - 2 * N_HEADS_A)))

    def ffn_ple_fwd(i, h_a, w_up_t, w_down, w_pp_t, w_pg):
        n_f = _rms_fwd(h_a, norm_ffn[i], name=f"l{i}_ffn_norm")
        up = _matmul(n_f, w_up_t, tb=True, name=f"l{i}_ffn_up")
        act = _ffn_act_fwd(up, cv_f[i], name=f"l{i}_ffn_act")
        h_b = _matmul(act, w_down, res=h_a, name=f"l{i}_ffn_down")
        n_p = _rms_fwd(h_b, norm_ple[i], name=f"l{i}_ple_norm")
        zg = _matmul(n_p, w_pg, name=f"l{i}_ple_gate")
        pe = _matmul(p_bf[i, 0], w_pp_t, tb=True, name=f"l{i}_ple_proj")
        h_c = _ple_fwd(h_b, zg, pe, name=f"l{i}_ple_mix")
        return h_c, dict(n_f=n_f, up=up, act=act, h_b=h_b, n_p=n_p, zg=zg, pe=pe)

    def layer_weights(lands):
        up_t, down, pp_t, pg = (stacked_rows(g) for g in lands)
        return up_t.reshape(2, D_FF, D_MODEL), down, pp_t, pg

    n0 = _rms_fwd(xs, norm_mix[0], name="l0_mix_norm")
    proj = _matmul(n0, wa_in_t, tb=True, tm=512, dep=tok, name="l0_in_proj")
    q, k, v, gbc, bbc = _delta_pre_fwd(proj, cv_a, alog_row, dtb_row, name="l0_delta_pre")
    *prep, tinv = _delta_prep_fwd(q, k, v, gbc, bbc, name="l0_delta_prep")
    o, states = _delta_scan_fwd(prep, name="l0_delta_scan")
    og = _gated_norm_fwd(o, proj, a_norm, name="l0_gated_norm")
    h1 = _matmul(og, wa_out, res=xs, name="l0_mix_out")
    lw0 = layer_weights(_exchange_wait(gather0, h1, scatter=False, name="gather_layer0_wait"))
    h3, sv0 = ffn_ple_fwd(0, h1, *lw0)

    lands1 = _exchange_wait(gather1, h3, scatter=False, name="gather_layer1_wait")
    wb_in_t, wb_out = stacked_rows(lands1[0]), stacked_rows(lands1[1])
    lw1 = layer_weights(lands1[2:])
    n1 = _rms_fwd(h3, norm_mix[1], name="l1_mix_norm")
    pb = _matmul(n1, wb_in_t, tb=True, name="l1_in_qkv")
    att = _swa_fwd(pb, b_sinks, name="l1_swa")
    h4 = _matmul(att, wb_out, res=h3, name="l1_mix_out")
    h6, sv1 = ffn_ple_fwd(1, h4, *lw1)

    loss_row, dh6, d_norm_final = _final_loss(h6, norm_final, tgt, name="final_loss")
    loss = lax.psum(loss_row[0, 0], MESH_AXES)

    def ffn_ple_bwd(i, dh_c, h_a, sv, lw, dep):
        w_up_t, w_down, _, w_pg = lw
        dzg, dpe = _ple_bwd(dh_c, sv["zg"], sv["pe"], name=f"l{i}_ple_mix_bwd")
        d_pg = _matmul(sv["n_p"], dzg, ta=True, out_dtype=BF16, dep=dep, name=f"l{i}_ple_gate_dw")
        d_pp_t = _matmul(dpe, p_bf[i, 0], ta=True, out_dtype=BF16, name=f"l{i}_ple_proj_dw")
        dn_p = _matmul(dzg, w_pg, tb=True, name=f"l{i}_ple_gate_dx")
        dh_b, d_np = _rms_bwd(sv["h_b"], norm_ple[i], dn_p, dh_c, name=f"l{i}_ple_norm_bwd")
        dact = _matmul(dh_b, w_down, tb=True, name=f"l{i}_ffn_down_dx")
        d_down = _matmul(sv["act"], dh_b, ta=True, out_dtype=BF16, name=f"l{i}_ffn_down_dw")
        dup, d_cg, d_cv = _ffn_act_bwd(sv["up"], cv_f[i], dact, name=f"l{i}_ffn_act_bwd")
        d_up_t = _matmul(dup, sv["n_f"], ta=True, out_dtype=BF16, name=f"l{i}_ffn_up_dw")
        dn_f = _matmul(dup, w_up_t, name=f"l{i}_ffn_up_dx")
        dh_a, d_nf = _rms_bwd(h_a, norm_ffn[i], dn_f, dh_b, name=f"l{i}_ffn_norm_bwd")
        mats = [d_up_t.reshape(2 * D_FF, D_MODEL), d_down, d_pp_t, d_pg]
        return dh_a, mats, dict(norm_ple=d_np, norm_ffn=d_nf, f_conv=jnp.concatenate([d_cg, d_cv], axis=1))

    def slabs(g):
        return g.reshape(N_DEV, g.shape[0] // N_DEV, g.shape[1])

    dh4, mats1, g1 = ffn_ple_bwd(1, dh6, h4, sv1, lw1, None)
    datt = _matmul(dh4, wb_out, tb=True, out_dtype=BF16, name="l1_mix_out_dx")
    d_wb_out = _matmul(att, dh4, ta=True, out_dtype=BF16, name="l1_mix_out_dw")
    dq_b, dk_b, dv_b, dsinks = _swa_bwd(pb, b_sinks, datt, name="l1_swa_bwd")
    dpb = jnp.concatenate([dq_b, dk_b.astype(BF16), dv_b.astype(BF16)], axis=1)
    d_wb_in_t = _matmul(dpb, n1, ta=True, out_dtype=BF16, name="l1_in_qkv_dw")
    send1, tok = _exchange_start([slabs(g) for g in [d_wb_in_t, d_wb_out] + mats1], scatter=True,
                                 name="exchange_layer1_start", dep=d_wb_in_t)
    dn1 = _matmul(dpb, wb_in_t, name="l1_in_qkv_dx")
    dh3, d_nm1 = _rms_bwd(h3, norm_mix[1], dn1, dh4, name="l1_mix_norm_bwd")

    dh1, mats0, g0 = ffn_ple_bwd(0, dh3, h1, sv0, lw0, tok)
    send0, tok = _exchange_start([slabs(g) for g in mats0], scatter=True, name="exchange_layer0_start", dep=mats0[0])
    dog = _matmul(dh1, wa_out, tb=True, dep=tok, name="l0_mix_out_dx")
    d_wa_out = _matmul(og, dh1, ta=True, out_dtype=BF16, name="l0_mix_out_dw")
    do, dz0, d_anorm = _gated_norm_bwd(o, proj, a_norm, dog, name="l0_gated_norm_bwd")
    cts = _delta_scan_bwd(prep, states, do, name="l0_delta_scan_bwd")
    dq, dk, dv, dgbc, dbbc = _delta_prep_bwd(q, k, v, gbc, bbc, tinv, cts, name="l0_delta_prep_bwd")
    dproj, d_aconv, d_alog, d_dtb = _delta_pre_bwd(proj, cv_a, alog_row, dtb_row, dq, dk, dv, dgbc, dbbc, dz0,
                                                   name="l0_delta_pre_bwd")
    d_wa_in_t = _matmul(dproj, n0, ta=True, out_dtype=BF16, name="l0_in_proj_dw")
    sendm, tok = _exchange_start([slabs(d_wa_in_t[:PROJ_A_REAL]), slabs(d_wa_out)], scatter=True,
                                 name="exchange_mixer0_start", dep=d_wa_in_t)
    dn0 = _matmul(dproj, wa_in_t, tm=512, dep=tok, name="l0_in_proj_dx")
    dx, d_nm0 = _rms_bwd(xs, norm_mix[0], dn0, dh1, name="l0_mix_norm_bwd")

    recv1 = _exchange_wait(send1, dx, scatter=True, name="exchange_layer1_wait")
    recv0 = _exchange_wait(send0, recv1[0], scatter=True, name="exchange_layer0_wait")
    parts = {("b_w_in", 0): recv1[0], ("b_w_out", 0): recv1[1]}
    parts.update({(n, 1): r for n, r in zip(layer_names, recv1[2:])})
    parts.update({(n, 0): r for n, r in zip(layer_names, recv0)})

    outs = {}

    def update_matrix(name):
        res = None
        for layer in range(wts[name].shape[0]):
            g = parts[(name, layer)]
            if name in TRANSPOSED:
                g = _sum_parts(g, name=f"sum_{name}_{layer}").T
            res = _adamw_layer(g, wts[name], mom[name], var[name], layer, res, name=f"adamw_{name}_{layer}")
        for kind, arr in zip(("grad", "delta", "new_m", "new_v"), res):
            outs[(kind, name)] = arr
        return res

    last = [update_matrix(n) for n in ("b_w_in", "b_w_out") + layer_names][-1]
    recvm = _exchange_wait(sendm, last[0], scatter=True, name="exchange_mixer0_wait")
    parts.update({("a_w_in", 0): recvm[0], ("a_w_out", 0): recvm[1]})
    update_matrix("a_w_in")
    update_matrix("a_w_out")

    gconvs = dict(a_conv=d_aconv[None], f_conv=jnp.stack([g0["f_conv"], g1["f_conv"]]))
    small_g = dict(norm_mix=jnp.concatenate([d_nm0, d_nm1]), norm_ffn=jnp.concatenate([g0["norm_ffn"], g1["norm_ffn"]]),
                   norm_ple=jnp.concatenate([g0["norm_ple"], g1["norm_ple"]]), norm_final=d_norm_final[0],
                   a_log=d_alog[:, N_HEADS_A:2 * N_HEADS_A], a_dt_bias=d_dtb[:, N_HEADS_A:2 * N_HEADS_A],
                   a_norm=d_anorm, b_sinks=dsinks[:, :N_HEADS_B])
    recv_conv = _all_to_all(_pack_split(gconvs, CONVS, F32, 8), name="exchange_conv_grads")
    recv_small = _all_gather(_pack([small_g[n] for n in SMALL], F32, 8), name="gather_small_grads")
    for names, recv, tag in ((CONVS, recv_conv, "convs"), (SMALL, recv_small, "small")):
        shapes = [wts[n].shape for n in names]
        g_slab = _sum_parts(recv, name=f"sum_{tag}")
        packed = [_pack([d[n] for n in names], F32, 8) for d in (wts, mom, var)]
        res = _adamw_packed(g_slab, *packed, name=f"adamw_{tag}")
        for kind, slab in zip(("grad", "delta", "new_m", "new_v"), (g_slab,) + tuple(res)):
            for n, arr in zip(names, _unpack(slab, shapes)):
                outs[(kind, n)] = arr

    result = [loss, dx[None]]
    for kind in ("grad", "delta", "new_m", "new_v"):
        result += [outs[(kind, n)] for n in WEIGHTS]
    return tuple(result)
```

```python
import functools
import math

import jax
import jax.numpy as jnp
from jax import lax
from jax.experimental import pallas as pl
from jax.experimental.pallas import tpu as pltpu

F32 = jnp.float32
BF16 = jnp.bfloat16

D_MODEL = 1024
N_HEADS_A = 8
HEAD_DIM_A = 128
CONV_A = 4
CHUNK = 128
N_HEADS_B = 16
N_KV_B = 4
GROUP_B = N_HEADS_B // N_KV_B
HEAD_DIM_B = 64
WINDOW = 128
D_FF = 2816
FFN_CONV = 3
PLE_DIM = 256
EPS = 1e-6
N_DEV = 8
HALO = 8
PROJ_A_REAL = 4 * N_HEADS_A * HEAD_DIM_A + 2 * N_HEADS_A
PROJ_A = 4 * N_HEADS_A * HEAD_DIM_A + 128
Z_COL_BLOCK = 3
BA_COL_BLOCK = 32

ADAM_LR = 0.001
ADAM_B1 = 0.9
ADAM_B2 = 0.999
ADAM_EPS = 1e-08
ADAM_WD = 0.01
ADAM_STEP = 10

LANES = 128
VMEM_LIMIT_BYTES = 56 * 1024 * 1024
NEG_BIG = -1e30

MESH_AXES = ("x", "y", "c")


def _params(sem=None):
    return pltpu.CompilerParams(dimension_semantics=sem, vmem_limit_bytes=VMEM_LIMIT_BYTES)


def _tile(n, target):
    best = None
    for t in range(LANES, min(n, target) + 1, LANES):
        if n % t == 0:
            best = t
    return best or n


def _sigmoid(x):
    return 1.0 / (1.0 + jnp.exp(-x))


def _softplus(x):
    return jnp.maximum(x, 0.0) + jnp.log1p(jnp.exp(-jnp.abs(x)))


def _matmul(a, b, *, name, ta=False, tb=False, res=None, out_dtype=F32, tm=1408, tn=1408, tk=None, dep=None):
    sa, sb = a.ndim == 3, b.ndim == 3
    ns = a.shape[0] if sa else (b.shape[0] if sb else 1)
    contract_stack = sa and sb
    out_stacked = sa != sb
    m = a.shape[-1] if ta else a.shape[-2]
    k = a.shape[-2] if ta else a.shape[-1]
    n = b.shape[-2] if tb else b.shape[-1]
    assert (b.shape[-1] if tb else b.shape[-2]) == k, (a.shape, b.shape, ta, tb)
    if tk is None:
        tk = 1024 if ta else 2816
    tm, tn, tk = _tile(m, tm), _tile(n, tn), _tile(k, tk)
    nk = k // tk
    nsteps = nk * (ns if contract_stack else 1)
    dims = (((0 if ta else 1,), (1 if tb else 0,)), ((), ()))

    def spec(block, stacked, order):
        def index(g, i, j, kk):
            two = order(i, j, kk % nk)
            if not stacked:
                return two
            return (kk // nk if contract_stack else g,) + two
        return pl.BlockSpec(((None,) if stacked else ()) + block, index)

    a_spec = spec((tk, tm), sa, lambda i, j, kq: (kq, i)) if ta else spec((tm, tk), sa, lambda i, j, kq: (i, kq))
    b_spec = spec((tn, tk), sb, lambda i, j, kq: (j, kq)) if tb else spec((tk, tn), sb, lambda i, j, kq: (kq, j))
    o_spec = spec((tm, tn), out_stacked, lambda i, j, kq: (i, j))
    has_res = res is not None
    has_dep = dep is not None

    def body(*refs):
        a_ref, b_ref = refs[0], refs[1]
        r_ref = refs[2] if has_res else None
        o_ref = refs[2 + has_res + has_dep]
        part = lax.dot_general(a_ref[...].astype(BF16), b_ref[...].astype(BF16), dims, preferred_element_type=F32)

        def finish(acc):
            if has_res:
                acc = acc + r_ref[...].astype(F32)
            o_ref[...] = acc.astype(out_dtype)

        if nsteps == 1:
            finish(part)
        else:
            acc_ref = refs[-1]
            kk = pl.program_id(3)

            @pl.when(kk == 0)
            def _():
                acc_ref[...] = part

            @pl.when(kk > 0)
            def _():
                acc_ref[...] += part

            @pl.when(kk == nsteps - 1)
            def _():
                finish(acc_ref[...])

    in_specs = [a_spec, b_spec] + ([o_spec] if has_res else []) + ([pl.BlockSpec(memory_space=pl.ANY)] if has_dep else [])
    args = (a, b) + ((res,) if has_res else ()) + ((dep,) if has_dep else ())
    return pl.pallas_call(
        body,
        name=name,
        grid=(ns if out_stacked else 1, m // tm, n // tn, nsteps),
        in_specs=in_specs,
        out_specs=o_spec,
        out_shape=jax.ShapeDtypeStruct(((ns,) if out_stacked else ()) + (m, n), out_dtype),
        scratch_shapes=[pltpu.VMEM((tm, tn), F32)] if nsteps > 1 else [],
        compiler_params=_params(("parallel", "parallel", "parallel", "arbitrary")),
    )(*args)


EPI_ROWS = 32


def _matmul_rows(a, b, epilogue, tiles_in, rows_in, outs, *, name, tb=False, tm=512, tk=None, dep=None):
    stacked = a.ndim == 3
    ns = a.shape[0] if stacked else 1
    m, k = a.shape[-2], a.shape[-1]
    n = b.shape[-2] if tb else b.shape[-1]
    assert (b.shape[-1] if tb else b.shape[-2]) == k and (b.ndim == 3) == stacked, (a.shape, b.shape, tb)
    tm, tk = _tile(m, tm), _tile(k, 2816 if tk is None else tk)
    nk = k // tk
    nsteps = nk * ns
    dims = (((1,), (1 if tb else 0,)), ((), ()))
    lead = (None,) if stacked else ()
    front = (lambda kk: (kk // nk,)) if stacked else (lambda kk: ())
    a_spec = pl.BlockSpec(lead + (tm, tk), lambda i, kk: front(kk) + (i, kk % nk))
    if tb:
        b_spec = pl.BlockSpec(lead + (n, tk), lambda i, kk: front(kk) + (0, kk % nk))
    else:
        b_spec = pl.BlockSpec(lead + (tk, n), lambda i, kk: front(kk) + (kk % nk, 0))
    tile_spec = pl.BlockSpec((tm, n), lambda i, kk: (i, 0))
    row_spec = pl.BlockSpec((1, n), lambda i, kk: (0, 0))
    n_t, n_r, has_dep = len(tiles_in), len(rows_in), dep is not None

    def body(*refs):
        a_ref, b_ref = refs[:2]
        tile_refs = refs[2:2 + n_t]
        row_refs = refs[2 + n_t:2 + n_t + n_r]
        out_refs = refs[2 + n_t + n_r + has_dep:-1]
        acc_ref = refs[-1]
        part = lax.dot_general(a_ref[...].astype(BF16), b_ref[...].astype(BF16), dims, preferred_element_type=F32)
        kk = pl.program_id(1)
        if nsteps == 1:
            acc_ref[...] = part
        else:
            @pl.when(kk == 0)
            def _():
                acc_ref[...] = part

            @pl.when(kk > 0)
            def _():
                acc_ref[...] += part

        @pl.when(kk == nsteps - 1)
        def _():
            epilogue(acc_ref, tile_refs, row_refs, out_refs, pl.program_id(0) == 0)

    return pl.pallas_call(
        body,
        name=name,
        grid=(m // tm, nsteps),
        in_specs=[a_spec, b_spec] + [tile_spec] * n_t + [row_spec] * n_r + ([pl.BlockSpec(memory_space=pl.ANY)] if has_dep else []),
        out_specs=[tile_spec if kind == "tile" else row_spec for _, kind in outs],
        out_shape=[jax.ShapeDtypeStruct((m, n) if kind == "tile" else (1, n), dt) for dt, kind in outs],
        scratch_shapes=[pltpu.VMEM((tm, n), F32)],
        compiler_params=_params(("arbitrary", "arbitrary")),
    )(a, b, *tiles_in, *[r.reshape(1, n) for r in rows_in], *((dep,) if has_dep else ()))


def _row_chunks(ref):
    return [pl.ds(r, EPI_ROWS) for r in range(0, ref.shape[0], EPI_ROWS)]


def _epi_res_norm(acc, tiles, rows, outs, first):
    (res,), (w,), (h_out, n_out) = tiles, rows, outs
    for rs in _row_chunks(acc):
        h = acc[rs, :] + res[rs, :]
        h_out[rs, :] = h
        r = lax.rsqrt(jnp.mean(h * h, axis=-1, keepdims=True) + EPS)
        n_out[rs, :] = (h * r * w[...]).astype(BF16)


def _epi_ple(acc, tiles, rows, outs, first):
    hb, pe = tiles
    for rs in _row_chunks(acc):
        zg = acc[rs, :]
        outs[0][rs, :] = zg
        h = hb[rs, :] + _sigmoid(zg) * pe[rs, :]
        outs[1][rs, :] = h
        if rows:
            r = lax.rsqrt(jnp.mean(h * h, axis=-1, keepdims=True) + EPS)
            outs[2][rs, :] = (h * r * rows[0][...]).astype(BF16)


def _epi_rms_bwd(acc, tiles, rows, outs, first):
    (h_ref, skip), (w,), (dh_out, dw_out) = tiles, rows, outs
    dw = jnp.zeros((1, acc.shape[1]), F32)
    for rs in _row_chunks(acc):
        x = h_ref[rs, :]
        r = lax.rsqrt(jnp.mean(x * x, axis=-1, keepdims=True) + EPS)
        nh = x * r
        g = acc[rs, :]
        gw = g * w[...]
        dh_out[rs, :] = r * (gw - nh * jnp.mean(gw * nh, axis=-1, keepdims=True)) + skip[rs, :]
        dw = dw + jnp.sum(g * nh, axis=0, keepdims=True)

    @pl.when(first)
    def _():
        dw_out[...] = dw

    @pl.when(jnp.logical_not(first))
    def _():
        dw_out[...] += dw


def _rms_fwd(h, w, *, name, tm=512):
    t, d = h.shape
    tm = _tile(t, tm)

    def body(h_ref, w_ref, o_ref):
        x = h_ref[...]
        r = lax.rsqrt(jnp.mean(x * x, axis=-1, keepdims=True) + EPS)
        o_ref[...] = (x * r * w_ref[...]).astype(BF16)

    return pl.pallas_call(
        body,
        name=name,
        grid=(t // tm,),
        in_specs=[pl.BlockSpec((tm, d), lambda i: (i, 0)), pl.BlockSpec((1, d), lambda i: (0, 0))],
        out_specs=pl.BlockSpec((tm, d), lambda i: (i, 0)),
        out_shape=jax.ShapeDtypeStruct((t, d), BF16),
        compiler_params=_params(("parallel",)),
    )(h, w.reshape(1, d))


def _rms_bwd(h, w, dn, skip, *, name, tm=512):
    t, d = h.shape
    tm = _tile(t, tm)

    def body(h_ref, w_ref, dn_ref, skip_ref, dh_ref, dw_ref):
        i = pl.program_id(0)
        x = h_ref[...]
        r = lax.rsqrt(jnp.mean(x * x, axis=-1, keepdims=True) + EPS)
        nh = x * r
        g = dn_ref[...].astype(F32)
        gw = g * w_ref[...]
        dh_ref[...] = r * (gw - nh * jnp.mean(gw * nh, axis=-1, keepdims=True)) + skip_ref[...]
        part = jnp.sum(g * nh, axis=0, keepdims=True)

        @pl.when(i == 0)
        def _():
            dw_ref[...] = part

        @pl.when(i > 0)
        def _():
            dw_ref[...] += part

    row = pl.BlockSpec((tm, d), lambda i: (i, 0))
    vec = pl.BlockSpec((1, d), lambda i: (0, 0))
    return pl.pallas_call(
        body,
        name=name,
        grid=(t // tm,),
        in_specs=[row, vec, row, row],
        out_specs=[row, vec],
        out_shape=[jax.ShapeDtypeStruct((t, d), F32), jax.ShapeDtypeStruct((1, d), F32)],
        compiler_params=_params(("arbitrary",)),
    )(h, w.reshape(1, d), dn, skip)


def _final_loss(h, w, target, *, name, tm=512):
    t, d = h.shape
    tm = _tile(t, tm)

    def body(h_ref, w_ref, tg_ref, loss_ref, dh_ref, dw_ref):
        i = pl.program_id(0)
        x = h_ref[...]
        r = lax.rsqrt(jnp.mean(x * x, axis=-1, keepdims=True) + EPS)
        nh = x * r
        err = nh * w_ref[...] - tg_ref[...]
        lpart = (0.5 / d) * jnp.sum(jnp.sum(err * err, axis=-1, keepdims=True), axis=0, keepdims=True)
        g = err * (1.0 / d)
        gw = g * w_ref[...]
        dh_ref[...] = r * (gw - nh * jnp.mean(gw * nh, axis=-1, keepdims=True))
        part = jnp.sum(g * nh, axis=0, keepdims=True)
        lrow = jnp.broadcast_to(lpart, (1, LANES))

        @pl.when(i == 0)
        def _():
            dw_ref[...] = part
            loss_ref[...] = lrow

        @pl.when(i > 0)
        def _():
            dw_ref[...] += part
            loss_ref[...] += lrow

    row = pl.BlockSpec((tm, d), lambda i: (i, 0))
    vec = pl.BlockSpec((1, d), lambda i: (0, 0))
    return pl.pallas_call(
        body,
        name=name,
        grid=(t // tm,),
        in_specs=[row, vec, row],
        out_specs=[pl.BlockSpec((1, LANES), lambda i: (0, 0)), row, vec],
        out_shape=[jax.ShapeDtypeStruct((1, LANES), F32), jax.ShapeDtypeStruct((t, d), F32), jax.ShapeDtypeStruct((1, d), F32)],
        compiler_params=_params(("arbitrary",)),
    )(h, w.reshape(1, d), target)


def _conv_from_ext(ext_ref, cw_ref, kw, tm):
    y = cw_ref[kw - 1:kw, :] * ext_ref[pl.ds(HALO, tm), :]
    for i in range(kw - 1):
        y = y + cw_ref[i:i + 1, :] * ext_ref[pl.ds(HALO - (kw - 1) + i, tm), :]
    return y


ROW_CHUNK = 64


def _conv_rows(src_ref, base, rows, cw_ref, kw):
    y = cw_ref[kw - 1:kw, :] * src_ref[pl.ds(base, rows), :]
    for i in range(kw - 1):
        y = y + cw_ref[i:i + 1, :] * src_ref[pl.ds(base - (kw - 1) + i, rows), :]
    return y


def _conv_t_rows(dy_ref, base, rows, cw_ref, kw):
    dx = cw_ref[kw - 1:kw, :] * dy_ref[pl.ds(base, rows), :]
    for i in range(kw - 1):
        dx = dx + cw_ref[i:i + 1, :] * dy_ref[pl.ds(base + kw - 1 - i, rows), :]
    return dx


def _conv_bwd_from_ext(xext_ref, dyext_ref, cw_ref, dcw_ref, kw, tm, first):
    dy = dyext_ref[pl.ds(0, tm), :]
    dx = cw_ref[kw - 1:kw, :] * dy
    for i in range(kw - 1):
        dx = dx + cw_ref[i:i + 1, :] * dyext_ref[pl.ds(kw - 1 - i, tm), :]
    for i in range(kw):
        part = jnp.sum(dy * xext_ref[pl.ds(HALO - (kw - 1) + i, tm), :], axis=0, keepdims=True)

        @pl.when(first)
        def _():
            dcw_ref[i:i + 1, :] = part

        @pl.when(jnp.logical_not(first))
        def _():
            dcw_ref[i:i + 1, :] += part

    return dx


def _delta_pre_fwd(proj, conv_w, alog_row, dtb_row, *, name, tm=256):
    t = proj.shape[0]
    c3 = 3 * N_HEADS_A * HEAD_DIM_A
    hk = N_HEADS_A * HEAD_DIM_A
    tm = _tile(t, tm)

    def body(x_ref, ba_ref, cw_ref, al_ref, db_ref, q_ref, k_ref, v_ref, g_ref, b_ref, ext):
        i = pl.program_id(0)

        @pl.when(i == 0)
        def _():
            ext[0:HALO, :] = jnp.zeros((HALO, c3), F32)

        ext[pl.ds(HALO, tm), :] = x_ref[...]
        y = _conv_from_ext(ext, cw_ref, CONV_A, tm)
        ext[0:HALO, :] = ext[pl.ds(tm, HALO), :]
        s = y * _sigmoid(y)
        for h in range(N_HEADS_A):
            lo = h * HEAD_DIM_A
            for dst, off in ((q_ref, 0), (k_ref, hk)):
                sh = s[:, off + lo:off + lo + HEAD_DIM_A]
                dst[:, lo:lo + HEAD_DIM_A] = sh * lax.rsqrt(jnp.sum(sh * sh, axis=-1, keepdims=True) + EPS)
        v_ref[...] = s[:, 2 * hk:3 * hk]
        ba = ba_ref[...]
        beta = _sigmoid(ba)
        gfull = -jnp.exp(al_ref[...]) * _softplus(ba + db_ref[...])
        for h in range(N_HEADS_A):
            lo = h * HEAD_DIM_A
            b_ref[:, lo:lo + HEAD_DIM_A] = jnp.broadcast_to(beta[:, h:h + 1], (tm, HEAD_DIM_A))
            g_ref[:, lo:lo + HEAD_DIM_A] = jnp.broadcast_to(gfull[:, N_HEADS_A + h:N_HEADS_A + h + 1], (tm, HEAD_DIM_A))

    row = lambda w: pl.BlockSpec((tm, w), lambda i: (i, 0))
    fixed = lambda r, w: pl.BlockSpec((r, w), lambda i: (0, 0))
    out = jax.ShapeDtypeStruct((t, hk), F32)
    return pl.pallas_call(
        body,
        name=name,
        grid=(t // tm,),
        in_specs=[row(c3), pl.BlockSpec((tm, LANES), lambda i: (i, BA_COL_BLOCK)), fixed(CONV_A, c3), fixed(1, LANES),
                  fixed(1, LANES)],
        out_specs=[row(hk)] * 5,
        out_shape=[out] * 5,
        scratch_shapes=[pltpu.VMEM((HALO + tm, c3), F32)],
        compiler_params=_params(("arbitrary",)),
    )(proj, proj, conv_w, alog_row, dtb_row)


def _delta_pre_bwd(proj, conv_w, alog_row, dtb_row, dq, dk, dv, dg, db, dz, *, name, tm=256):
    t, pw = proj.shape
    c3 = 3 * N_HEADS_A * HEAD_DIM_A
    hk = N_HEADS_A * HEAD_DIM_A
    tm = _tile(t, tm)
    nt = t // tm
    hb = tm // HALO

    def body(x_ref, xp_ref, ba_ref, cw_ref, al_ref, db_ref, dq_ref, dk_ref, dv_ref, dg_ref, dbt_ref, dz_ref,
             dp_ref, dcw_ref, dal_ref, ddb_ref, xext, dyext, carry):
        i = pl.program_id(0)
        first = i == 0
        tile = nt - 1 - i

        @pl.when(tile == 0)
        def _():
            xext[0:HALO, :] = jnp.zeros((HALO, c3), F32)

        @pl.when(tile > 0)
        def _():
            xext[0:HALO, :] = xp_ref[...]

        xext[pl.ds(HALO, tm), :] = x_ref[...]
        y = _conv_from_ext(xext, cw_ref, CONV_A, tm)
        sg = _sigmoid(y)
        s = y * sg
        dsilu = sg * (1.0 + y * (1.0 - sg))
        for h in range(N_HEADS_A):
            lo = h * HEAD_DIM_A
            for src, off in ((dq_ref, 0), (dk_ref, hk)):
                sh = s[:, off + lo:off + lo + HEAD_DIM_A]
                r = lax.rsqrt(jnp.sum(sh * sh, axis=-1, keepdims=True) + EPS)
                qn = sh * r
                gq = src[:, lo:lo + HEAD_DIM_A]
                dsh = r * (gq - qn * jnp.sum(gq * qn, axis=-1, keepdims=True))
                dyext[pl.ds(0, tm), off + lo:off + lo + HEAD_DIM_A] = dsh * dsilu[:, off + lo:off + lo + HEAD_DIM_A]
        dyext[pl.ds(0, tm), 2 * hk:3 * hk] = dv_ref[...] * dsilu[:, 2 * hk:3 * hk]

        @pl.when(first)
        def _():
            dyext[pl.ds(tm, HALO), :] = jnp.zeros((HALO, c3), F32)

        @pl.when(jnp.logical_not(first))
        def _():
            dyext[pl.ds(tm, HALO), :] = carry[...]

        dx = _conv_bwd_from_ext(xext, dyext, cw_ref, dcw_ref, CONV_A, tm, first)
        carry[...] = dyext[0:HALO, :]
        dp_ref[:, 0:c3] = dx.astype(BF16)
        dp_ref[:, c3:c3 + hk] = dz_ref[...]

        lane = lax.broadcasted_iota(jnp.int32, (tm, LANES), 1)
        gcol = jnp.zeros((tm, LANES), F32)
        for h in range(N_HEADS_A):
            lo = h * HEAD_DIM_A
            dbh = jnp.sum(dbt_ref[:, lo:lo + HEAD_DIM_A], axis=-1, keepdims=True)
            dgh = jnp.sum(dg_ref[:, lo:lo + HEAD_DIM_A], axis=-1, keepdims=True)
            gcol = gcol + jnp.where(lane == h, dbh, 0.0) + jnp.where(lane == N_HEADS_A + h, dgh, 0.0)
        ba = ba_ref[...]
        beta = _sigmoid(ba)
        a_neg = -jnp.exp(al_ref[...])
        z = ba + db_ref[...]
        dz = gcol * a_neg * _sigmoid(z)
        is_g = jnp.logical_and(lane >= N_HEADS_A, lane < 2 * N_HEADS_A)
        dba = jnp.where(lane < N_HEADS_A, gcol * beta * (1.0 - beta), jnp.where(is_g, dz, 0.0))
        dp_ref[:, c3 + hk:pw] = dba.astype(BF16)
        dal = jnp.sum(jnp.where(is_g, gcol * a_neg * _softplus(z), 0.0), axis=0, keepdims=True)
        ddb = jnp.sum(jnp.where(is_g, dz, 0.0), axis=0, keepdims=True)

        @pl.when(first)
        def _():
            dal_ref[...] = dal
            ddb_ref[...] = ddb

        @pl.when(jnp.logical_not(first))
        def _():
            dal_ref[...] += dal
            ddb_ref[...] += ddb

    rev = lambda w: pl.BlockSpec((tm, w), lambda i: (nt - 1 - i, 0))
    prev = pl.BlockSpec((HALO, c3), lambda i: (jnp.maximum((nt - 1 - i) * hb - 1, 0), 0))
    fixed = lambda r, w: pl.BlockSpec((r, w), lambda i: (0, 0))
    return pl.pallas_call(
        body,
        name=name,
        grid=(nt,),
        in_specs=[rev(c3), prev, pl.BlockSpec((tm, LANES), lambda i: (nt - 1 - i, BA_COL_BLOCK)), fixed(CONV_A, c3),
                  fixed(1, LANES), fixed(1, LANES)] + [rev(hk)] * 6,
        out_specs=[rev(pw), fixed(CONV_A, c3), fixed(1, LANES), fixed(1, LANES)],
        out_shape=[jax.ShapeDtypeStruct((t, pw), BF16), jax.ShapeDtypeStruct((CONV_A, c3), F32),
                   jax.ShapeDtypeStruct((1, LANES), F32), jax.ShapeDtypeStruct((1, LANES), F32)],
        scratch_shapes=[pltpu.VMEM((HALO + tm, c3), F32), pltpu.VMEM((tm + HALO, c3), F32), pltpu.VMEM((HALO, c3), F32)],
        compiler_params=_params(("arbitrary",)),
    )(proj, proj, proj, conv_w, alog_row, dtb_row, dq, dk, dv, dg, db, dz)


def _gated_norm_fwd(o, proj, w, *, name, tm=512):
    t, d = o.shape
    tm = _tile(t, tm)

    def body(o_ref, z_ref, w_ref, y_ref):
        for h in range(N_HEADS_A):
            sl = slice(h * HEAD_DIM_A, (h + 1) * HEAD_DIM_A)
            oh = o_ref[:, sl]
            zh = z_ref[:, sl]
            r = lax.rsqrt(jnp.mean(oh * oh, axis=-1, keepdims=True) + EPS)
            y_ref[:, sl] = (oh * r * w_ref[...] * (zh * _sigmoid(zh))).astype(BF16)

    row = pl.BlockSpec((tm, d), lambda i: (i, 0))
    return pl.pallas_call(
        body,
        name=name,
        grid=(t // tm,),
        in_specs=[row, pl.BlockSpec((tm, d), lambda i: (i, Z_COL_BLOCK)), pl.BlockSpec((1, HEAD_DIM_A), lambda i: (0, 0))],
        out_specs=row,
        out_shape=jax.ShapeDtypeStruct((t, d), BF16),
        compiler_params=_params(("parallel",)),
    )(o, proj, w)


def _gated_norm_bwd(o, proj, w, dy, *, name, tm=512):
    t, d = o.shape
    tm = _tile(t, tm)

    def body(o_ref, z_ref, w_ref, dy_ref, do_ref, dz_ref, dw_ref):
        i = pl.program_id(0)
        dw = jnp.zeros((1, HEAD_DIM_A), F32)
        for h in range(N_HEADS_A):
            sl = slice(h * HEAD_DIM_A, (h + 1) * HEAD_DIM_A)
            oh = o_ref[:, sl]
            zh = z_ref[:, sl]
            g = dy_ref[:, sl]
            r = lax.rsqrt(jnp.mean(oh * oh, axis=-1, keepdims=True) + EPS)
            nh = oh * r
            sg = _sigmoid(zh)
            dz_ref[:, sl] = (g * nh * w_ref[...] * (sg * (1.0 + zh * (1.0 - sg)))).astype(BF16)
            dt = g * (zh * sg)
            dw = dw + jnp.sum(dt * nh, axis=0, keepdims=True)
            dnh = dt * w_ref[...]
            do_ref[:, sl] = r * (dnh - nh * jnp.mean(dnh * nh, axis=-1, keepdims=True))

        @pl.when(i == 0)
        def _():
            dw_ref[...] = dw

        @pl.when(i > 0)
        def _():
            dw_ref[...] += dw

    row = pl.BlockSpec((tm, d), lambda i: (i, 0))
    vec = pl.BlockSpec((1, HEAD_DIM_A), lambda i: (0, 0))
    return pl.pallas_call(
        body,
        name=name,
        grid=(t // tm,),
        in_specs=[row, pl.BlockSpec((tm, d), lambda i: (i, Z_COL_BLOCK)), vec, row],
        out_specs=[row, row, vec],
        out_shape=[jax.ShapeDtypeStruct((t, d), F32), jax.ShapeDtypeStruct((t, d), BF16),
                   jax.ShapeDtypeStruct((1, HEAD_DIM_A), F32)],
        compiler_params=_params(("arbitrary",)),
    )(o, proj, w, dy)


_NN = (((1,), (0,)), ((), ()))
_NT = (((1,), (1,)), ((), ()))
_TN = (((0,), (0,)), ((), ()))
_DIMS = {"nn": _NN, "nt": _NT, "tn": _TN}


def _raw_dot(a, b, kind, prec):
    dims = _DIMS[kind]
    a_hi, b_hi = a.astype(BF16), b.astype(BF16)
    out = lax.dot_general(a_hi, b_hi, dims, preferred_element_type=F32)
    if prec == "x3":
        a_lo = (a - a_hi.astype(F32)).astype(BF16)
        b_lo = (b - b_hi.astype(F32)).astype(BF16)
        out = out + lax.dot_general(a_hi, b_lo, dims, preferred_element_type=F32)
        out = out + lax.dot_general(a_lo, b_hi, dims, preferred_element_type=F32)
    elif prec == "s3":
        r1 = b - b_hi.astype(F32)
        b_mid = r1.astype(BF16)
        b_lo = (r1 - b_mid.astype(F32)).astype(BF16)
        out = out + lax.dot_general(a_hi, b_mid, dims, preferred_element_type=F32)
        out = out + lax.dot_general(a_hi, b_lo, dims, preferred_element_type=F32)
    return out


def _raw_dots(xs, ys, kind, prec):
    return [_raw_dot(x, y, kind, prec) for x, y in zip(xs, ys)]


@functools.partial(jax.custom_vjp, nondiff_argnums=(2, 3))
def _dots(xs, ys, kind, prec):
    return _raw_dots(xs, ys, kind, prec)


def _dots_fwd(xs, ys, kind, prec):
    return _raw_dots(xs, ys, kind, prec), (xs, ys)


def _dots_bwd(kind, prec, saved, gs):
    xs, ys = saved
    if kind == "nn":
        return _raw_dots(gs, ys, "nt", prec), _raw_dots(xs, gs, "tn", prec)
    if kind == "nt":
        return _raw_dots(gs, ys, "nn", prec), _raw_dots(gs, xs, "tn", prec)
    return _raw_dots(ys, gs, "nt", prec), _raw_dots(xs, gs, "nn", prec)


_dots.defvjp(_dots_fwd, _dots_bwd)


def _eye(c):
    return (lax.broadcasted_iota(jnp.int32, (c, c), 0) == lax.broadcasted_iota(jnp.int32, (c, c), 1)).astype(F32)


def _inv_unit_lower_raw(lmats):
    c = lmats[0].shape[0]
    eye = _eye(c)
    xs = [eye - l for l in lmats]
    ps = lmats
    for _ in range(int(math.log2(c)) - 1):
        ps = _raw_dots(ps, ps, "nn", "bf16")
        xs = [x + d for x, d in zip(xs, _raw_dots(xs, ps, "nn", "bf16"))]
    rs = [x - eye + d for x, d in zip(xs, _raw_dots(lmats, xs, "nn", "x3"))]
    return [x - d for x, d in zip(xs, _raw_dots(xs, rs, "nn", "bf16"))]


@jax.custom_vjp
def _inv_unit_lower(lmats, hints):
    return _inv_unit_lower_raw(lmats) if hints is None else hints


def _inv_fwd(lmats, hints):
    tms = _inv_unit_lower_raw(lmats) if hints is None else hints
    return tms, (tms, hints)


def _inv_bwd(saved, gs):
    tms, hints = saved
    ds = [-d for d in _raw_dots(_raw_dots(tms, gs, "tn", "x3"), tms, "nt", "x3")]
    return ds, (None if hints is None else [jnp.zeros_like(h) for h in hints])


_inv_unit_lower.defvjp(_inv_fwd, _inv_bwd)


def _delta_prep(qs, ks, vs, gs, bs, hints=None):
    c = qs[0].shape[0]
    nh = len(qs)
    ii = lax.broadcasted_iota(jnp.int32, (c, c), 0)
    jj = lax.broadcasted_iota(jnp.int32, (c, c), 1)
    incl = ii >= jj
    strict = ii > jj
    ltri = incl.astype(F32)
    eye = _eye(c)
    m1 = _dots([ltri] * nh, gs, "nn", "s3")
    gtot = [jnp.sum(g, axis=0, keepdims=True) for g in gs]
    decay = [jnp.exp(jnp.where(incl, m - m.T, NEG_BIG)) for m in m1]
    eg = [jnp.exp(m) for m in m1]
    kk = _dots(ks, ks, "nt", "bf16")
    lmats = [jnp.where(strict, b * x * d, 0.0) for b, x, d in zip(bs, kk, decay)]
    tinv = _inv_unit_lower(lmats, hints)
    toff = [t - eye for t in tinv]
    bv = [b * v for b, v in zip(bs, vs)]
    bk = [b * e * k for b, e, k in zip(bs, eg, ks)]
    u0 = [x + d for x, d in zip(bv, _dots(toff, bv, "nn", "bf16"))]
    wk = [x + d for x, d in zip(bk, _dots(toff, bk, "nn", "bf16"))]
    qsc = [q * (HEAD_DIM_A ** -0.5) for q in qs]
    qk = [x * d for x, d in zip(_dots(qsc, ks, "nt", "bf16"), decay)]
    q_dec = [q * e for q, e in zip(qsc, eg)]
    k_dec = [k * jnp.exp(t - m) for k, t, m in zip(ks, gtot, m1)]
    glast = [jnp.broadcast_to(jnp.exp(t), (c, c)) for t in gtot]
    return (u0, wk, qk, q_dec, k_dec, glast), tinv


def _delta_step(ss, u0, wk, qk, q_dec, k_dec, glast):
    us = [a - d for a, d in zip(u0, _dots(wk, ss, "nn", "bf16"))]
    os_ = [a + d for a, d in zip(_dots(q_dec, ss, "nn", "bf16"), _dots(qk, us, "nn", "bf16"))]
    s_new = [g * s + d for g, s, d in zip(glast, ss, _dots(k_dec, us, "tn", "bf16"))]
    return os_, s_new


HEADS_PER_STEP = 8


def _chunk_spec(nc, reverse=False):
    w = HEADS_PER_STEP * HEAD_DIM_A
    if reverse:
        return pl.BlockSpec((CHUNK, w), lambda h, n: (nc - 1 - n, h))
    return pl.BlockSpec((CHUNK, w), lambda h, n: (n, h))


def _head_slices():
    return [slice(j * HEAD_DIM_A, (j + 1) * HEAD_DIM_A) for j in range(HEADS_PER_STEP)]


def _heads(ref):
    return [ref[:, sl] for sl in _head_slices()]


def _delta_prep_fwd(q, k, v, gbc, bbc, *, name):
    t, d = q.shape
    nc = t // CHUNK

    def body(q_ref, k_ref, v_ref, g_ref, b_ref, *outs):
        res, tinv = _delta_prep(*[_heads(r) for r in (q_ref, k_ref, v_ref, g_ref, b_ref)])
        for ref, vals in zip(outs, res + (tinv,)):
            for sl, val in zip(_head_slices(), vals):
                ref[:, sl] = val

    spec = _chunk_spec(nc)
    return pl.pallas_call(
        body,
        name=name,
        grid=(N_HEADS_A // HEADS_PER_STEP, nc),
        in_specs=[spec] * 5,
        out_specs=[spec] * 7,
        out_shape=[jax.ShapeDtypeStruct((t, d), F32)] * 7,
        compiler_params=_params(("parallel", "parallel")),
    )(q, k, v, gbc, bbc)


def _delta_prep_bwd(q, k, v, gbc, bbc, tinv, cts, *, name):
    t, d = q.shape
    nc = t // CHUNK

    def body(q_ref, k_ref, v_ref, g_ref, b_ref, t_ref, c0, c1, c2, c3, c4, c5, *outs):
        def f(q_, k_, v_, g_, b_):
            return _delta_prep(q_, k_, v_, g_, b_, hints=_heads(t_ref))[0]

        _, vjp = jax.vjp(f, *[_heads(r) for r in (q_ref, k_ref, v_ref, g_ref, b_ref)])
        grads = vjp(tuple(_heads(c) for c in (c0, c1, c2, c3, c4, c5)))
        for ref, vals in zip(outs, grads):
            for sl, val in zip(_head_slices(), vals):
                ref[:, sl] = val

    spec = _chunk_spec(nc)
    return pl.pallas_call(
        body,
        name=name,
        grid=(N_HEADS_A // HEADS_PER_STEP, nc),
        in_specs=[spec] * 12,
        out_specs=[spec] * 5,
        out_shape=[jax.ShapeDtypeStruct((t, d), F32)] * 5,
        compiler_params=_params(("parallel", "parallel")),
    )(q, k, v, gbc, bbc, tinv, *cts)


def _delta_scan_fwd(prep, *, name):
    t, d = prep[0].shape
    nc = t // CHUNK

    def body(u0, wk, qk, qd, kd, gl, o_ref, st_ref, s_ref):
        n = pl.program_id(1)

        @pl.when(n == 0)
        def _():
            s_ref[...] = jnp.zeros(s_ref.shape, F32)

        ss = [s_ref[j] for j in range(HEADS_PER_STEP)]
        os_, s_new = _delta_step(ss, *[_heads(r) for r in (u0, wk, qk, qd, kd, gl)])
        for j, sl in enumerate(_head_slices()):
            st_ref[:, sl] = ss[j]
            o_ref[:, sl] = os_[j]
            s_ref[j] = s_new[j]

    spec = _chunk_spec(nc)
    return pl.pallas_call(
        body,
        name=name,
        grid=(N_HEADS_A // HEADS_PER_STEP, nc),
        in_specs=[spec] * 6,
        out_specs=[spec] * 2,
        out_shape=[jax.ShapeDtypeStruct((t, d), F32)] * 2,
        scratch_shapes=[pltpu.VMEM((HEADS_PER_STEP, HEAD_DIM_A, HEAD_DIM_A), F32)],
        compiler_params=_params(("parallel", "arbitrary")),
    )(*prep)


def _delta_scan_bwd(prep, states, do, *, name):
    t, d = do.shape
    nc = t // CHUNK

    def body(u0, wk, qk, qd, kd, gl, st_ref, do_ref, *rest):
        outs, ds_ref = rest[:6], rest[6]
        n = pl.program_id(1)

        @pl.when(n == 0)
        def _():
            ds_ref[...] = jnp.zeros(ds_ref.shape, F32)

        _, vjp = jax.vjp(_delta_step, *[_heads(r) for r in (st_ref, u0, wk, qk, qd, kd, gl)])
        grads = vjp((_heads(do_ref), [ds_ref[j] for j in range(HEADS_PER_STEP)]))
        for j, sl in enumerate(_head_slices()):
            ds_ref[j] = grads[0][j]
            for ref, vals in zip(outs, grads[1:]):
                ref[:, sl] = vals[j]

    spec = _chunk_spec(nc, reverse=True)
    return pl.pallas_call(
        body,
        name=name,
        grid=(N_HEADS_A // HEADS_PER_STEP, nc),
        in_specs=[spec] * 8,
        out_specs=[spec] * 6,
        out_shape=[jax.ShapeDtypeStruct((t, d), F32)] * 6,
        scratch_shapes=[pltpu.VMEM((HEADS_PER_STEP, HEAD_DIM_A, HEAD_DIM_A), F32)],
        compiler_params=_params(("parallel", "arbitrary")),
    )(*prep, states, do)


def _alibi_slope(h):
    return 2.0 ** (-8.0 * (h + 1) / N_HEADS_B)


def _swa_load(sink_ref, q_ref, kp_ref, kc_ref, vp_ref, vc_ref):
    rg = lax.broadcasted_iota(jnp.int32, (GROUP_B * WINDOW, 1), 0) // WINDOW
    q4s, kcats, vcats, slopes, sinkcols = [], [], [], [], []
    for hk in range(N_KV_B):
        ks = slice(hk * HEAD_DIM_B, (hk + 1) * HEAD_DIM_B)
        heads = [hk * GROUP_B + g for g in range(GROUP_B)]
        q4s.append(jnp.concatenate([q_ref[:, h * HEAD_DIM_B:(h + 1) * HEAD_DIM_B] for h in heads], axis=0).astype(BF16))
        kcats.append(jnp.concatenate([kp_ref[:, ks], kc_ref[:, ks]], axis=0).astype(BF16))
        vcats.append(jnp.concatenate([vp_ref[:, ks], vc_ref[:, ks]], axis=0).astype(BF16))
        slope = jnp.zeros((GROUP_B * WINDOW, 1), F32)
        sink = jnp.zeros((GROUP_B * WINDOW, 1), F32)
        for g, h in enumerate(heads):
            slope = jnp.where(rg == g, _alibi_slope(h), slope)
            sink = jnp.where(rg == g, sink_ref[0, h], sink)
        slopes.append(slope)
        sinkcols.append(sink)
    return q4s, kcats, vcats, slopes, sinkcols


def _swa_probs(q4s, kcats, slopes, sinkcols, blk):
    rows = GROUP_B * WINDOW
    qi = lax.broadcasted_iota(jnp.int32, (rows, 2 * WINDOW), 0) % WINDOW
    kj = lax.broadcasted_iota(jnp.int32, (rows, 2 * WINDOW), 1)
    dist = qi + WINDOW - kj
    valid = (dist >= 0) & (dist < WINDOW) & (blk * WINDOW - WINDOW + kj >= 0)
    distf = dist.astype(F32)
    ss = [lax.dot_general(q, kc, _NT, preferred_element_type=F32) for q, kc in zip(q4s, kcats)]
    logits = [jnp.where(valid, s * (HEAD_DIM_B ** -0.5) - sl * distf, NEG_BIG) for s, sl in zip(ss, slopes)]
    ms = [jnp.maximum(jnp.max(l, axis=-1, keepdims=True), sk) for l, sk in zip(logits, sinkcols)]
    es = [jnp.exp(l - m) for l, m in zip(logits, ms)]
    esk = [jnp.exp(sk - m) for sk, m in zip(sinkcols, ms)]
    invs = [1.0 / (jnp.sum(e, axis=-1, keepdims=True) + k) for e, k in zip(es, esk)]
    return [e * i for e, i in zip(es, invs)], [k * i for k, i in zip(esk, invs)]


def _swa_fwd(proj, sinks, *, name):
    t = proj.shape[0]
    nb = t // WINDOW
    qd = N_HEADS_B * HEAD_DIM_B
    kd = N_KV_B * HEAD_DIM_B

    def body(sink_ref, q_ref, kp_ref, kc_ref, vp_ref, vc_ref, o_ref):
        blk = pl.program_id(0)
        q4s, kcats, vcats, slopes, sinkcols = _swa_load(sink_ref, q_ref, kp_ref, kc_ref, vp_ref, vc_ref)
        ps, _ = _swa_probs(q4s, kcats, slopes, sinkcols, blk)
        outs = [jnp.dot(p.astype(BF16), vc, preferred_element_type=F32) for p, vc in zip(ps, vcats)]
        for hk, out in enumerate(outs):
            for g in range(GROUP_B):
                h = hk * GROUP_B + g
                o_ref[:, h * HEAD_DIM_B:(h + 1) * HEAD_DIM_B] = out[g * WINDOW:(g + 1) * WINDOW, :].astype(BF16)

    q_spec = pl.BlockSpec((WINDOW, qd), lambda i: (i, 0))
    kv = lambda col, prev: pl.BlockSpec((WINDOW, kd), (lambda i: (jnp.maximum(i - 1, 0), col)) if prev else (lambda i: (i, col)))
    kcol, vcol = qd // kd, qd // kd + 1
    return pl.pallas_call(
        body,
        name=name,
        grid=(nb,),
        in_specs=[pl.BlockSpec(memory_space=pltpu.SMEM), q_spec, kv(kcol, True), kv(kcol, False), kv(vcol, True), kv(vcol, False)],
        out_specs=q_spec,
        out_shape=jax.ShapeDtypeStruct((t, qd), BF16),
        compiler_params=_params(("parallel",)),
    )(sinks, proj, proj, proj, proj, proj)


def _swa_bwd(proj, sinks, dout, *, name):
    t = proj.shape[0]
    nb = t // WINDOW
    qd = N_HEADS_B * HEAD_DIM_B
    kd = N_KV_B * HEAD_DIM_B
    scale = HEAD_DIM_B ** -0.5

    def body(sink_ref, q_ref, kp_ref, kc_ref, vp_ref, vc_ref, do_ref, dq_ref, dk_ref, dv_ref, dsk_ref):
        blk = pl.program_id(0)
        lane = lax.broadcasted_iota(jnp.int32, (1, LANES), 1)

        @pl.when(blk == 0)
        def _():
            dk_ref[...] = jnp.zeros((t, kd), F32)
            dv_ref[...] = jnp.zeros((t, kd), F32)
            dsk_ref[...] = jnp.zeros((1, LANES), F32)

        cur = pl.ds(pl.multiple_of(blk * WINDOW, WINDOW), WINDOW)
        prv = pl.ds(pl.multiple_of(jnp.maximum(blk - 1, 0) * WINDOW, WINDOW), WINDOW)
        q4s, kcats, vcats, slopes, sinkcols = _swa_load(sink_ref, q_ref, kp_ref, kc_ref, vp_ref, vc_ref)
        ps, psinks = _swa_probs(q4s, kcats, slopes, sinkcols, blk)
        do4s = [jnp.concatenate([do_ref[:, (hk * GROUP_B + g) * HEAD_DIM_B:(hk * GROUP_B + g + 1) * HEAD_DIM_B]
                                 for g in range(GROUP_B)], axis=0).astype(BF16) for hk in range(N_KV_B)]
        dps = [lax.dot_general(d, vc, _NT, preferred_element_type=F32) for d, vc in zip(do4s, vcats)]
        deltas = [jnp.sum(p * dp, axis=-1, keepdims=True) for p, dp in zip(ps, dps)]
        dss = [(p * (dp - dl) * scale).astype(BF16) for p, dp, dl in zip(ps, dps, deltas)]
        dq4s = [jnp.dot(ds, kc, preferred_element_type=F32) for ds, kc in zip(dss, kcats)]
        dkcs = [lax.dot_general(ds, q, _TN, preferred_element_type=F32) for ds, q in zip(dss, q4s)]
        dvcs = [lax.dot_general(p.astype(BF16), d, _TN, preferred_element_type=F32) for p, d in zip(ps, do4s)]
        dsk = jnp.zeros((1, LANES), F32)
        for hk in range(N_KV_B):
            ks = slice(hk * HEAD_DIM_B, (hk + 1) * HEAD_DIM_B)
            dsink = -psinks[hk] * deltas[hk]
            for g in range(GROUP_B):
                h = hk * GROUP_B + g
                rows = slice(g * WINDOW, (g + 1) * WINDOW)
                dq_ref[:, h * HEAD_DIM_B:(h + 1) * HEAD_DIM_B] = dq4s[hk][rows, :].astype(BF16)
                dsk = dsk + jnp.where(lane == h, jnp.sum(dsink[rows, :], axis=0, keepdims=True), 0.0)
            dk_ref[cur, ks] += dkcs[hk][WINDOW:, :]
            dv_ref[cur, ks] += dvcs[hk][WINDOW:, :]

            @pl.when(blk > 0)
            def _():
                dk_ref[prv, ks] += dkcs[hk][:WINDOW, :]
                dv_ref[prv, ks] += dvcs[hk][:WINDOW, :]

        dsk_ref[...] += dsk

    q_spec = pl.BlockSpec((WINDOW, qd), lambda i: (i, 0))
    kv = lambda col, prev: pl.BlockSpec((WINDOW, kd), (lambda i: (jnp.maximum(i - 1, 0), col)) if prev else (lambda i: (i, col)))
    kcol, vcol = qd // kd, qd // kd + 1
    full = pl.BlockSpec((t, kd), lambda i: (0, 0))
    return pl.pallas_call(
        body,
        name=name,
        grid=(nb,),
        in_specs=[pl.BlockSpec(memory_space=pltpu.SMEM), q_spec, kv(kcol, True), kv(kcol, False), kv(vcol, True), kv(vcol, False), q_spec],
        out_specs=[q_spec, full, full, pl.BlockSpec((1, LANES), lambda i: (0, 0))],
        out_shape=[jax.ShapeDtypeStruct((t, qd), BF16), jax.ShapeDtypeStruct((t, kd), F32),
                   jax.ShapeDtypeStruct((t, kd), F32), jax.ShapeDtypeStruct((1, LANES), F32)],
        compiler_params=_params(("arbitrary",)),
    )(sinks, proj, proj, proj, proj, proj, dout)


def _ffn_act_fwd(up, cw, *, name, tm=512, cb=256):
    _, t, f = up.shape
    tm, cb = _tile(t, tm), _tile(f, cb)

    rc = min(ROW_CHUNK, tm)

    def body(ug_ref, uv_ref, cg_ref, cv_ref, a_ref, hg, hv):
        i = pl.program_id(1)

        @pl.when(i == 0)
        def _():
            hg[0:HALO, :] = jnp.zeros((HALO, cb), F32)
            hv[0:HALO, :] = jnp.zeros((HALO, cb), F32)

        hg[pl.ds(HALO, rc), :] = ug_ref[0:rc, :]
        hv[pl.ds(HALO, rc), :] = uv_ref[0:rc, :]
        for r in range(tm // rc):
            if r == 0:
                yg = _conv_rows(hg, HALO, rc, cg_ref, FFN_CONV)
                yv = _conv_rows(hv, HALO, rc, cv_ref, FFN_CONV)
            else:
                yg = _conv_rows(ug_ref, r * rc, rc, cg_ref, FFN_CONV)
                yv = _conv_rows(uv_ref, r * rc, rc, cv_ref, FFN_CONV)
            a_ref[r * rc:(r + 1) * rc, :] = (yg * _sigmoid(yg) * yv).astype(BF16)
        hg[0:HALO, :] = ug_ref[tm - HALO:tm, :]
        hv[0:HALO, :] = uv_ref[tm - HALO:tm, :]

    ncb = f // cb
    half = lambda s: pl.BlockSpec((None, tm, cb), lambda c, i: (s, i, c))
    taps = lambda s: pl.BlockSpec((FFN_CONV, cb), lambda c, i: (0, c + s * ncb))
    return pl.pallas_call(
        body,
        name=name,
        grid=(ncb, t // tm),
        in_specs=[half(0), half(1), taps(0), taps(1)],
        out_specs=pl.BlockSpec((tm, cb), lambda c, i: (i, c)),
        out_shape=jax.ShapeDtypeStruct((t, f), BF16),
        scratch_shapes=[pltpu.VMEM((HALO + rc, cb), F32)] * 2,
        compiler_params=_params(("parallel", "arbitrary")),
    )(up, up, cw, cw)


def _ffn_act_bwd(up, cw, dact, *, name, tm=512, cb=256):
    _, t, f = up.shape
    tm, cb = _tile(t, tm), _tile(f, cb)
    nt = t // tm
    hb = tm // HALO

    rc = min(ROW_CHUNK, tm)
    nr = tm // rc
    kw = FFN_CONV

    def body(ug_ref, uv_ref, pg_ref, pv_ref, cg_ref, cv_ref, da_ref, du_ref, dcg_ref, dcv_ref,
             hg, hv, dyg, dyv):
        i = pl.program_id(1)
        first = i == 0
        tile = nt - 1 - i

        @pl.when(tile == 0)
        def _():
            hg[0:HALO, :] = jnp.zeros((HALO, cb), F32)
            hv[0:HALO, :] = jnp.zeros((HALO, cb), F32)

        @pl.when(tile > 0)
        def _():
            hg[0:HALO, :] = pg_ref[...]
            hv[0:HALO, :] = pv_ref[...]

        @pl.when(first)
        def _():
            dyg[pl.ds(tm, HALO), :] = jnp.zeros((HALO, cb), F32)
            dyv[pl.ds(tm, HALO), :] = jnp.zeros((HALO, cb), F32)

        hg[pl.ds(HALO, rc), :] = ug_ref[0:rc, :]
        hv[pl.ds(HALO, rc), :] = uv_ref[0:rc, :]
        dcg = [jnp.zeros((1, cb), F32) for _ in range(kw)]
        dcv = [jnp.zeros((1, cb), F32) for _ in range(kw)]
        for r in reversed(range(nr)):
            rows = slice(r * rc, (r + 1) * rc)
            src_g, src_v, base = (hg, hv, HALO) if r == 0 else (ug_ref, uv_ref, r * rc)
            yg = _conv_rows(src_g, base, rc, cg_ref, kw)
            yv = _conv_rows(src_v, base, rc, cv_ref, kw)
            sg = _sigmoid(yg)
            da = da_ref[rows, :]
            dy_g = da * yv * (sg * (1.0 + yg * (1.0 - sg)))
            dy_v = da * (yg * sg)
            dyg[rows, :] = dy_g
            dyv[rows, :] = dy_v
            du_ref[0, rows, :] = _conv_t_rows(dyg, r * rc, rc, cg_ref, kw).astype(BF16)
            du_ref[1, rows, :] = _conv_t_rows(dyv, r * rc, rc, cv_ref, kw).astype(BF16)
            for j in range(kw):
                dcg[j] = dcg[j] + jnp.sum(dy_g * src_g[pl.ds(base - (kw - 1) + j, rc), :], axis=0, keepdims=True)
                dcv[j] = dcv[j] + jnp.sum(dy_v * src_v[pl.ds(base - (kw - 1) + j, rc), :], axis=0, keepdims=True)
        dyg[pl.ds(tm, HALO), :] = dyg[0:HALO, :]
        dyv[pl.ds(tm, HALO), :] = dyv[0:HALO, :]
        for j in range(kw):
            @pl.when(first)
            def _():
                dcg_ref[j:j + 1, :] = dcg[j]
                dcv_ref[j:j + 1, :] = dcv[j]

            @pl.when(jnp.logical_not(first))
            def _():
                dcg_ref[j:j + 1, :] += dcg[j]
                dcv_ref[j:j + 1, :] += dcv[j]

    ncb = f // cb
    half = lambda s: pl.BlockSpec((None, tm, cb), lambda c, i: (s, nt - 1 - i, c))
    prev = lambda s: pl.BlockSpec((None, HALO, cb), lambda c, i: (s, jnp.maximum((nt - 1 - i) * hb - 1, 0), c))
    taps = lambda s: pl.BlockSpec((FFN_CONV, cb), lambda c, i: (0, c + s * ncb))
    dtaps = pl.BlockSpec((FFN_CONV, cb), lambda c, i: (0, c))
    return pl.pallas_call(
        body,
        name=name,
        grid=(ncb, nt),
        in_specs=[half(0), half(1), prev(0), prev(1), taps(0), taps(1), pl.BlockSpec((tm, cb), lambda c, i: (nt - 1 - i, c))],
        out_specs=[pl.BlockSpec((2, tm, cb), lambda c, i: (0, nt - 1 - i, c)), dtaps, dtaps],
        out_shape=[jax.ShapeDtypeStruct((2, t, f), BF16), jax.ShapeDtypeStruct((FFN_CONV, f), F32),
                   jax.ShapeDtypeStruct((FFN_CONV, f), F32)],
        scratch_shapes=[pltpu.VMEM((HALO + rc, cb), F32)] * 2 + [pltpu.VMEM((tm + HALO, cb), F32)] * 2,
        compiler_params=_params(("parallel", "arbitrary")),
    )(up, up, up, up, cw, cw, dact)


def _ple_fwd(h, zg, pe, *, name, tm=512):
    t, d = h.shape
    tm = _tile(t, tm)

    def body(h_ref, z_ref, p_ref, o_ref):
        o_ref[...] = h_ref[...] + _sigmoid(z_ref[...]) * p_ref[...]

    row = pl.BlockSpec((tm, d), lambda i: (i, 0))
    return pl.pallas_call(
        body, name=name, grid=(t // tm,), in_specs=[row] * 3, out_specs=row,
        out_shape=jax.ShapeDtypeStruct((t, d), F32), compiler_params=_params(("parallel",)),
    )(h, zg, pe)


def _ple_bwd(dh, zg, pe, *, name, tm=512):
    t, d = dh.shape
    tm = _tile(t, tm)

    def body(g_ref, z_ref, p_ref, dz_ref, dp_ref):
        g = g_ref[...]
        sg = _sigmoid(z_ref[...])
        dz_ref[...] = (g * p_ref[...] * sg * (1.0 - sg)).astype(BF16)
        dp_ref[...] = (g * sg).astype(BF16)

    row = pl.BlockSpec((tm, d), lambda i: (i, 0))
    return pl.pallas_call(
        body, name=name, grid=(t // tm,), in_specs=[row] * 3, out_specs=[row] * 2,
        out_shape=[jax.ShapeDtypeStruct((t, d), BF16)] * 2, compiler_params=_params(("parallel",)),
    )(dh, zg, pe)


def _my_pos():
    return lax.axis_index("x"), lax.axis_index("y"), lax.axis_index("c")


def _all_gather(block, *, name, dep=None):
    r, w = block.shape
    has_dep = dep is not None

    def body(*refs):
        x_ref, out_ref, send_sems, recv_sems, local_sem = refs[:1] + refs[1 + has_dep:]
        x, y, c = _my_pos()
        me, sibling = (x, y, c), (x, y, 1 - c)
        chips = [(1 - x, y), (x, 1 - y), (1 - x, 1 - y)]

        def slot(px, py, pc):
            return out_ref.at[4 * px + 2 * py + pc]

        def copy(k, blk, to, src=None):
            return pltpu.make_async_remote_copy(
                src_ref=slot(*blk) if src is None else src, dst_ref=slot(*blk),
                send_sem=send_sems.at[k], recv_sem=recv_sems.at[k],
                device_id=to, device_id_type=pl.DeviceIdType.MESH)

        mine = pltpu.make_async_copy(x_ref, slot(*me), local_sem)
        mine.start()
        first = [copy(0, me, sibling, src=x_ref)]
        first += [copy(1 + j, me, (*chip, c), src=x_ref) for j, chip in enumerate(chips)]
        for cp in first:
            cp.start()
        passed = [copy(4 + j, (*chip, c), sibling) for j, chip in enumerate(chips)]
        for j, chip in enumerate(chips):
            copy(1 + j, (*chip, c), me).wait_recv()
            passed[j].start()
        copy(0, sibling, me).wait_recv()
        for j, chip in enumerate(chips):
            copy(4 + j, (*chip, 1 - c), me).wait_recv()
        for cp in first + passed:
            cp.wait_send()
        mine.wait()

    return pl.pallas_call(
        body,
        name=name,
        out_shape=jax.ShapeDtypeStruct((N_DEV, r, w), block.dtype),
        in_specs=[pl.BlockSpec(memory_space=pl.ANY)] * (1 + has_dep),
        out_specs=pl.BlockSpec(memory_space=pl.ANY),
        scratch_shapes=[pltpu.SemaphoreType.DMA((7,)), pltpu.SemaphoreType.DMA((7,)), pltpu.SemaphoreType.DMA],
    )(*((block, dep) if has_dep else (block,)))


def _all_to_all(slabs, *, name):
    n, r, w = slabs.shape

    def body(x_ref, out_ref, send_sems, recv_sems, local_sem):
        x, y, c = _my_pos()
        my_idx = 4 * x + 2 * y + c
        mine = pltpu.make_async_copy(x_ref.at[my_idx], out_ref.at[my_idx], local_sem)
        mine.start()
        copies = []
        for k in range(1, N_DEV):
            fx, fy, fc = (k >> 2) & 1, (k >> 1) & 1, k & 1
            px = (1 - x) if fx else x
            py = (1 - y) if fy else y
            pc = (1 - c) if fc else c
            cp = pltpu.make_async_remote_copy(
                src_ref=x_ref.at[4 * px + 2 * py + pc], dst_ref=out_ref.at[my_idx],
                send_sem=send_sems.at[k - 1], recv_sem=recv_sems.at[k - 1],
                device_id=(px, py, pc), device_id_type=pl.DeviceIdType.MESH)
            cp.start()
            copies.append(cp)
        for cp in copies:
            cp.wait_recv()
        for cp in copies:
            cp.wait_send()
        mine.wait()

    return pl.pallas_call(
        body,
        name=name,
        out_shape=jax.ShapeDtypeStruct((n, r, w), slabs.dtype),
        in_specs=[pl.BlockSpec(memory_space=pl.ANY)],
        out_specs=pl.BlockSpec(memory_space=pl.ANY),
        scratch_shapes=[pltpu.SemaphoreType.DMA((7,)), pltpu.SemaphoreType.DMA((7,)), pltpu.SemaphoreType.DMA],
    )(slabs)


def _exchange_copies(scatter, src_refs, land_refs, send_sems, recv_sems, local_sems):
    x, y, c = _my_pos()
    me = 4 * x + 2 * y + c
    local, remote = [], []
    for i, (s, l) in enumerate(zip(src_refs, land_refs)):
        local.append(pltpu.make_async_copy(s.at[me] if scatter else s, l.at[me], local_sems.at[i]))
        for k in range(1, N_DEV):
            px = (1 - x) if (k >> 2) & 1 else x
            py = (1 - y) if (k >> 1) & 1 else y
            pc = (1 - c) if k & 1 else c
            remote.append(pltpu.make_async_remote_copy(
                src_ref=s.at[4 * px + 2 * py + pc] if scatter else s, dst_ref=l.at[me],
                send_sem=send_sems.at[(N_DEV - 1) * i + k - 1], recv_sem=recv_sems.at[(N_DEV - 1) * i + k - 1],
                device_id=(px, py, pc), device_id_type=pl.DeviceIdType.MESH))
    return local, remote


def _exchange(arrays, *, scatter, name):
    n = len(arrays)

    def body(*refs):
        srcs, lands = refs[:n], refs[n:2 * n]
        local, remote = _exchange_copies(scatter, srcs, lands, *refs[2 * n:])
        for cp in local + remote:
            cp.start()
        for cp in remote:
            cp.wait_recv()
        for cp in remote:
            cp.wait_send()
        for cp in local:
            cp.wait()

    hbm = pl.BlockSpec(memory_space=pl.ANY)
    out = pl.pallas_call(
        body,
        name=name,
        out_shape=[jax.ShapeDtypeStruct(a.shape if scatter else (N_DEV,) + a.shape, a.dtype) for a in arrays],
        in_specs=[hbm] * n,
        out_specs=[hbm] * n,
        scratch_shapes=[pltpu.SemaphoreType.DMA(((N_DEV - 1) * n,)), pltpu.SemaphoreType.DMA(((N_DEV - 1) * n,)),
                        pltpu.SemaphoreType.DMA((n,))],
    )(*arrays)
    return list(out)


_HBM_SPEC = pl.BlockSpec(memory_space=pltpu.HBM)
_SEM_SPEC = pl.BlockSpec(memory_space=pltpu.SEMAPHORE)
_EFFECT = pltpu.SideEffectType.DATAFLOW_SIDE_EFFECTING


def _exchange_start(arrays, *, scatter, name, dep):
    n = len(arrays)
    srcs = [pltpu.with_memory_space_constraint(a, pltpu.HBM) for a in arrays]
    lands = [pltpu.with_memory_space_constraint(lax.empty(a.shape if scatter else (N_DEV,) + a.shape, a.dtype), pltpu.HBM)
             for a in arrays]

    def body(*refs):
        src_refs, land_refs = refs[:n], refs[n:2 * n]
        send_sems, recv_sems, local_sems = refs[2 * n + 1:2 * n + 4]
        token = refs[-1]
        local, remote = _exchange_copies(scatter, src_refs, land_refs, send_sems, recv_sems, local_sems)
        for cp in local + remote:
            cp.start()
        token[...] = jnp.zeros_like(token)

    sems = (pltpu.SemaphoreType.DMA(((N_DEV - 1) * n,)), pltpu.SemaphoreType.DMA(((N_DEV - 1) * n,)),
            pltpu.SemaphoreType.DMA((n,)))
    out = pl.pallas_call(
        body,
        name=name,
        out_shape=sems + tuple(pltpu.HBM(a.shape, a.dtype) for a in srcs + lands) + (jax.ShapeDtypeStruct((8, LANES), F32),),
        in_specs=[_HBM_SPEC] * (2 * n) + [pl.BlockSpec(memory_space=pl.ANY)],
        out_specs=(_SEM_SPEC,) * 3 + (_HBM_SPEC,) * (2 * n) + (pl.BlockSpec(memory_space=pltpu.VMEM),),
        input_output_aliases={i: 3 + i for i in range(2 * n)},
        compiler_params=pltpu.CompilerParams(has_side_effects=_EFFECT),
    )(*srcs, *lands, dep)
    return (out[:3], list(out[3:3 + n]), list(out[3 + n:3 + 2 * n])), out[-1]


def _exchange_wait(handle, after, *, scatter, name):
    sems, srcs, lands = handle
    n = len(srcs)

    def body(*refs):
        src_refs, land_refs = refs[:n], refs[n:2 * n]
        send_sems, recv_sems, local_sems = refs[2 * n:2 * n + 3]
        local, remote = _exchange_copies(scatter, src_refs, land_refs, send_sems, recv_sems, local_sems)
        for cp in remote:
            cp.wait_send()
            cp.wait_recv()
        for cp in local:
            cp.wait()

    out = pl.pallas_call(
        body,
        name=name,
        out_shape=tuple(pltpu.HBM(a.shape, a.dtype) for a in srcs + lands),
        in_specs=[_HBM_SPEC] * (2 * n) + [_SEM_SPEC] * 3 + [pl.BlockSpec(memory_space=pl.ANY)],
        out_specs=(_HBM_SPEC,) * (2 * n),
        input_output_aliases={i: i for i in range(2 * n)},
        compiler_params=pltpu.CompilerParams(has_side_effects=_EFFECT),
    )(*srcs, *lands, *sems, after)
    return list(out[n:])


def _sum_parts(parts, *, name, tr=512):
    n, r, lanes = parts.shape
    tr = tr if (r % tr == 0 and r > 1024) else r

    def body(p_ref, g_ref):
        g = p_ref[0].astype(F32)
        for j in range(1, n):
            g = g + p_ref[j].astype(F32)
        g_ref[...] = g

    row = pl.BlockSpec((tr, lanes), lambda i: (i, 0))
    return pl.pallas_call(
        body,
        name=name,
        grid=(r // tr,),
        in_specs=[pl.BlockSpec((n, tr, lanes), lambda i: (0, i, 0))],
        out_specs=row,
        out_shape=jax.ShapeDtypeStruct((r, lanes), F32),
        compiler_params=_params(("parallel",)),
    )(parts)


def _adamw_update(g, w, m, v):
    c1 = 1.0 / (1.0 - ADAM_B1 ** ADAM_STEP)
    c2 = 1.0 / (1.0 - ADAM_B2 ** ADAM_STEP)
    nm = ADAM_B1 * m + (1.0 - ADAM_B1) * g
    nv = ADAM_B2 * v + (1.0 - ADAM_B2) * (g * g)
    return -ADAM_LR * ((nm * c1) / (jnp.sqrt(nv * c2) + ADAM_EPS) + ADAM_WD * w), nm, nv


def _adamw_layer(g, w, m, v, layer, prev, *, name):
    nl, k, n = w.shape
    tr = max(d for d in range(8, min(k, 256) + 1, 8) if k % d == 0)
    in_parts = g.ndim == 3

    def body(g_ref, w_ref, m_ref, v_ref, *rest):
        go_ref, d_ref, nm_ref, nv_ref = rest[-4:]
        if in_parts:
            gg = g_ref[0].astype(F32)
            for j in range(1, g_ref.shape[0]):
                gg = gg + g_ref[j].astype(F32)
        else:
            gg = g_ref[...]
        d, nm, nv = _adamw_update(gg, w_ref[...], m_ref[...], v_ref[...])
        go_ref[...] = gg
        d_ref[...] = d
        nm_ref[...] = nm
        nv_ref[...] = nv

    lay = pl.BlockSpec((None, tr, n), lambda i: (layer, i, 0))
    n_prev = 0 if prev is None else 4
    out = jax.ShapeDtypeStruct((nl, k, n), F32)
    return pl.pallas_call(
        body,
        name=name,
        grid=(k // tr,),
        in_specs=[pl.BlockSpec((g.shape[0], tr, n), lambda i: (0, i, 0)) if in_parts else pl.BlockSpec((tr, n), lambda i: (i, 0)),
                  lay, lay, lay] + [pl.BlockSpec(memory_space=pl.ANY)] * n_prev,
        out_specs=[lay] * 4,
        out_shape=[out] * 4,
        input_output_aliases={4 + j: j for j in range(n_prev)},
        compiler_params=_params(("parallel",)),
    )(g, w, m, v, *(prev or ()))


def _adamw_packed(g, w, m, v, *, name, tr=512):
    r, lanes = g.shape
    tr = tr if r % tr == 0 else r
    c1 = 1.0 / (1.0 - ADAM_B1 ** ADAM_STEP)
    c2 = 1.0 / (1.0 - ADAM_B2 ** ADAM_STEP)

    def body(g_ref, w_ref, m_ref, v_ref, d_ref, nm_ref, nv_ref):
        g = g_ref[...]
        nm = ADAM_B1 * m_ref[...] + (1.0 - ADAM_B1) * g
        nv = ADAM_B2 * v_ref[...] + (1.0 - ADAM_B2) * (g * g)
        nm_ref[...] = nm
        nv_ref[...] = nv
        d_ref[...] = -ADAM_LR * ((nm * c1) / (jnp.sqrt(nv * c2) + ADAM_EPS) + ADAM_WD * w_ref[...])

    row = pl.BlockSpec((tr, lanes), lambda i: (i, 0))
    out = jax.ShapeDtypeStruct((r, lanes), F32)
    return pl.pallas_call(
        body,
        name=name,
        grid=(r // tr,),
        in_specs=[row] * 4,
        out_specs=[row] * 3,
        out_shape=[out] * 3,
        compiler_params=_params(("parallel",)),
    )(g, w, m, v)


BIG = ("a_w_in", "a_w_out", "b_w_in", "b_w_out", "f_w_up", "f_w_down", "ple_w_proj", "ple_w_gate")
CONVS = ("a_conv", "f_conv")
SMALL = ("norm_mix", "norm_ffn", "norm_ple", "norm_final", "a_log", "a_dt_bias", "a_norm", "b_sinks")
WEIGHTS = ("norm_mix", "norm_ffn", "norm_ple", "norm_final", "a_w_in", "a_conv", "a_log", "a_dt_bias", "a_norm",
           "a_w_out", "b_w_in", "b_sinks", "b_w_out", "f_w_up", "f_conv", "f_w_down", "ple_w_proj", "ple_w_gate")
SLAB_ROW_MULTIPLE = 512


def _pack(arrs, dtype, row_multiple):
    flat = jnp.concatenate([a.reshape(-1).astype(dtype) for a in arrs])
    rows = -(-flat.shape[0] // LANES)
    rows = -(-rows // row_multiple) * row_multiple
    return jnp.pad(flat, (0, rows * LANES - flat.shape[0])).reshape(rows, LANES)


def _unpack(slab, shapes):
    lead = slab.shape[:-2]
    flat = slab.reshape(lead + (-1,))
    out, off = [], 0
    for s in shapes:
        size = math.prod(s)
        out.append(flat[..., off:off + size].reshape(lead + tuple(s)))
        off += size
    return out


def _cols_full(g):
    g = jnp.moveaxis(g, 0, -2)
    return g.reshape(g.shape[:-2] + (g.shape[-2] * g.shape[-1],))


def _rows_full(g):
    g = jnp.moveaxis(g, 0, -3)
    return g.reshape(g.shape[:-3] + (g.shape[-3] * g.shape[-2], g.shape[-1]))


def _cols_split(wfull):
    n = wfull.shape[-1] // N_DEV
    g = wfull.reshape(wfull.shape[:-1] + (N_DEV, n))
    return jnp.moveaxis(g, -2, 0)


def _rows_split(wfull):
    k = wfull.shape[-2] // N_DEV
    g = wfull.reshape(wfull.shape[:-2] + (N_DEV, k, wfull.shape[-1]))
    return jnp.moveaxis(g, -3, 0)


TRANSPOSED = ("a_w_in", "b_w_in", "f_w_up", "ple_w_proj")


def _wire(name, a):
    return jnp.swapaxes(a, -1, -2) if name in TRANSPOSED else a


def _wire_shape(name, shape):
    return shape[:-2] + (shape[-1], shape[-2]) if name in TRANSPOSED else tuple(shape)


def _full(name, g):
    return _cols_full(g) if name in CONVS else _rows_full(g)


def _split(name, wfull):
    return _cols_split(wfull) if name in CONVS else _rows_split(wfull)


def _pack_split(grads, names, dtype, row_multiple):
    flat = jnp.concatenate([_split(n, grads[n]).reshape(N_DEV, -1).astype(dtype) for n in names], axis=1)
    rows = -(-flat.shape[1] // LANES)
    rows = -(-rows // row_multiple) * row_multiple
    return jnp.pad(flat, ((0, 0), (0, rows * LANES - flat.shape[1]))).reshape(N_DEV, rows, LANES)


def _pad_cols(a, width):
    return jnp.pad(a, ((0, 0), (0, width - a.shape[1])))


def kernel(x, p, norm_mix, norm_ffn, norm_ple, norm_final, a_w_in, a_conv, a_log, a_dt_bias, a_norm, a_w_out, b_w_in, b_sinks, b_w_out, f_w_up, f_conv, f_w_down, ple_w_proj, ple_w_gate, loss_target, m_norm_mix, m_norm_ffn, m_norm_ple, m_norm_final, m_a_w_in, m_a_conv, m_a_log, m_a_dt_bias, m_a_norm, m_a_w_out, m_b_w_in, m_b_sinks, m_b_w_out, m_f_w_up, m_f_conv, m_f_w_down, m_ple_w_proj, m_ple_w_gate, v_norm_mix, v_norm_ffn, v_norm_ple, v_norm_final, v_a_w_in, v_a_conv, v_a_log, v_a_dt_bias, v_a_norm, v_a_w_out, v_b_w_in, v_b_sinks, v_b_w_out, v_f_w_up, v_f_conv, v_f_w_down, v_ple_w_proj, v_ple_w_gate):
    wts = dict(norm_mix=norm_mix, norm_ffn=norm_ffn, norm_ple=norm_ple, norm_final=norm_final, a_w_in=a_w_in,
               a_conv=a_conv, a_log=a_log, a_dt_bias=a_dt_bias, a_norm=a_norm, a_w_out=a_w_out, b_w_in=b_w_in,
               b_sinks=b_sinks, b_w_out=b_w_out, f_w_up=f_w_up, f_conv=f_conv, f_w_down=f_w_down,
               ple_w_proj=ple_w_proj, ple_w_gate=ple_w_gate)
    mom = dict(norm_mix=m_norm_mix, norm_ffn=m_norm_ffn, norm_ple=m_norm_ple, norm_final=m_norm_final,
               a_w_in=m_a_w_in, a_conv=m_a_conv, a_log=m_a_log, a_dt_bias=m_a_dt_bias, a_norm=m_a_norm,
               a_w_out=m_a_w_out, b_w_in=m_b_w_in, b_sinks=m_b_sinks, b_w_out=m_b_w_out, f_w_up=m_f_w_up,
               f_conv=m_f_conv, f_w_down=m_f_w_down, ple_w_proj=m_ple_w_proj, ple_w_gate=m_ple_w_gate)
    var = dict(norm_mix=v_norm_mix, norm_ffn=v_norm_ffn, norm_ple=v_norm_ple, norm_final=v_norm_final,
               a_w_in=v_a_w_in, a_conv=v_a_conv, a_log=v_a_log, a_dt_bias=v_a_dt_bias, a_norm=v_a_norm,
               a_w_out=v_a_w_out, b_w_in=v_b_w_in, b_sinks=v_b_sinks, b_w_out=v_b_w_out, f_w_up=v_f_w_up,
               f_conv=v_f_conv, f_w_down=v_f_w_down, ple_w_proj=v_ple_w_proj, ple_w_gate=v_ple_w_gate)
    hk = N_HEADS_A * HEAD_DIM_A
    xs = x[0]
    tgt = loss_target[0]
    p_bf = p.astype(BF16)

    def shard(name, layer):
        return _wire(name, wts[name][layer]).astype(BF16)

    def stacked_rows(g):
        return g.reshape(g.shape[0] * g.shape[1], g.shape[2])

    n_in = a_w_in.shape[-1]
    first = _all_gather(jnp.concatenate([shard("a_w_in", 0), shard("a_w_out", 0)]), name="gather_mixer0")
    wa_in_t = jnp.pad(stacked_rows(first[:, :n_in]), ((0, PROJ_A - PROJ_A_REAL), (0, 0)))
    wa_out = stacked_rows(first[:, n_in:])
    gconv = _all_gather(_pack([wts[n] for n in CONVS], F32, 8), dep=first, name="gather_convs")
    conv_full = {n: _cols_full(g) for n, g in zip(CONVS, _unpack(gconv, [wts[n].shape for n in CONVS]))}
    cv_a, cv_f = conv_full["a_conv"][0], conv_full["f_conv"]
    layer_names = ("f_w_up", "f_w_down", "ple_w_proj", "ple_w_gate")
    gather0, tok = _exchange_start([shard(n, 0) for n in layer_names], scatter=False, name="gather_layer0_start", dep=gconv)
    gather1, tok = _exchange_start([shard(n, 0) for n in ("b_w_in", "b_w_out")] + [shard(n, 1) for n in layer_names],
                                   scatter=False, name="gather_layer1_start", dep=tok)

    alog_row = jnp.pad(a_log, ((0, 0), (N_HEADS_A, LANES - 2 * N_HEADS_A)))
    dtb_row = jnp.pad(a_dt_bias, ((0, 0), (N_HEADS_A, LANES - 2 * N_HEADS_A)))

    tile_f32, tile_bf16, rowsum = (F32, "tile"), (BF16, "tile"), (F32, "rowsum")

    def ffn_ple_fwd(i, h_a, n_f, next_norm, w_up_t, w_down, w_pp_t, w_pg):
        up = _matmul(n_f, w_up_t, tb=True, name=f"l{i}_ffn_up")
        act = _ffn_act_fwd(up, cv_f[i], name=f"l{i}_ffn_act")
        h_b, n_p = _matmul_rows(act, w_down, _epi_res_norm, [h_a], [norm_ple[i]], [tile_f32, tile_bf16],
                                name=f"l{i}_ffn_down")
        pe = _matmul(p_bf[i, 0], w_pp_t, tb=True, name=f"l{i}_ple_proj")
        res = _matmul_rows(n_p, w_pg, _epi_ple, [h_b, pe], [] if next_norm is None else [next_norm],
                           [tile_f32, tile_f32] + ([] if next_norm is None else [tile_bf16]), name=f"l{i}_ple_gate")
        return res[1], (None if next_norm is None else res[2]), dict(n_f=n_f, up=up, act=act, h_b=h_b, n_p=n_p, zg=res[0], pe=pe)

    def layer_weights(lands):
        up_t, down, pp_t, pg = (stacked_rows(g) for g in lands)
        return up_t.reshape(2, D_FF, D_MODEL), down, pp_t, pg

    n0 = _rms_fwd(xs, norm_mix[0], name="l0_mix_norm")
    proj = _matmul(n0, wa_in_t, tb=True, tm=512, dep=tok, name="l0_in_proj")
    q, k, v, gbc, bbc = _delta_pre_fwd(proj, cv_a, alog_row, dtb_row, name="l0_delta_pre")
    *prep, tinv = _delta_prep_fwd(q, k, v, gbc, bbc, name="l0_delta_prep")
    o, states = _delta_scan_fwd(prep, name="l0_delta_scan")
    og = _gated_norm_fwd(o, proj, a_norm, name="l0_gated_norm")
    h1, nf0 = _matmul_rows(og, wa_out, _epi_res_norm, [xs], [norm_ffn[0]], [tile_f32, tile_bf16], name="l0_mix_out")
    lw0 = layer_weights(_exchange_wait(gather0, h1, scatter=False, name="gather_layer0_wait"))
    h3, n1, sv0 = ffn_ple_fwd(0, h1, nf0, norm_mix[1], *lw0)

    lands1 = _exchange_wait(gather1, h3, scatter=False, name="gather_layer1_wait")
    wb_in_t, wb_out = stacked_rows(lands1[0]), stacked_rows(lands1[1])
    lw1 = layer_weights(lands1[2:])
    pb = _matmul(n1, wb_in_t, tb=True, name="l1_in_qkv")
    att = _swa_fwd(pb, b_sinks, name="l1_swa")
    h4, nf1 = _matmul_rows(att, wb_out, _epi_res_norm, [h3], [norm_ffn[1]], [tile_f32, tile_bf16], name="l1_mix_out")
    h6, _, sv1 = ffn_ple_fwd(1, h4, nf1, None, *lw1)

    loss_row, dh6, d_norm_final = _final_loss(h6, norm_final, tgt, name="final_loss")
    loss = lax.psum(loss_row[0, 0], MESH_AXES)

    def ffn_ple_bwd(i, dh_c, h_a, sv, lw, dep):
        w_up_t, w_down, _, w_pg = lw
        dzg, dpe = _ple_bwd(dh_c, sv["zg"], sv["pe"], name=f"l{i}_ple_mix_bwd")
        d_pg = _matmul(sv["n_p"], dzg, ta=True, out_dtype=BF16, dep=dep, name=f"l{i}_ple_gate_dw")
        d_pp_t = _matmul(dpe, p_bf[i, 0], ta=True, out_dtype=BF16, name=f"l{i}_ple_proj_dw")
        dh_b, d_np = _matmul_rows(dzg, w_pg, _epi_rms_bwd, [sv["h_b"], dh_c], [norm_ple[i]], [tile_f32, rowsum], tb=True,
                                  name=f"l{i}_ple_gate_dx")
        dact = _matmul(dh_b, w_down, tb=True, name=f"l{i}_ffn_down_dx")
        d_down = _matmul(sv["act"], dh_b, ta=True, out_dtype=BF16, name=f"l{i}_ffn_down_dw")
        dup, d_cg, d_cv = _ffn_act_bwd(sv["up"], cv_f[i], dact, name=f"l{i}_ffn_act_bwd")
        d_up_t = _matmul(dup, sv["n_f"], ta=True, out_dtype=BF16, name=f"l{i}_ffn_up_dw")
        dh_a, d_nf = _matmul_rows(dup, w_up_t, _epi_rms_bwd, [h_a, dh_b], [norm_ffn[i]], [tile_f32, rowsum],
                                  name=f"l{i}_ffn_up_dx")
        mats = [d_up_t.reshape(2 * D_FF, D_MODEL), d_down, d_pp_t, d_pg]
        return dh_a, mats, dict(norm_ple=d_np, norm_ffn=d_nf, f_conv=jnp.concatenate([d_cg, d_cv], axis=1))

    def slabs(g):
        return g.reshape(N_DEV, g.shape[0] // N_DEV, g.shape[1])

    dh4, mats1, g1 = ffn_ple_bwd(1, dh6, h4, sv1, lw1, None)
    datt = _matmul(dh4, wb_out, tb=True, out_dtype=BF16, name="l1_mix_out_dx")
    d_wb_out = _matmul(att, dh4, ta=True, out_dtype=BF16, name="l1_mix_out_dw")
    dq_b, dk_b, dv_b, dsinks = _swa_bwd(pb, b_sinks, datt, name="l1_swa_bwd")
    dpb = jnp.concatenate([dq_b, dk_b.astype(BF16), dv_b.astype(BF16)], axis=1)
    d_wb_in_t = _matmul(dpb, n1, ta=True, out_dtype=BF16, name="l1_in_qkv_dw")
    send1, tok = _exchange_start([slabs(g) for g in [d_wb_in_t, d_wb_out] + mats1], scatter=True,
                                 name="exchange_layer1_start", dep=d_wb_in_t)
    dh3, d_nm1 = _matmul_rows(dpb, wb_in_t, _epi_rms_bwd, [h3, dh4], [norm_mix[1]], [tile_f32, rowsum], name="l1_in_qkv_dx")

    dh1, mats0, g0 = ffn_ple_bwd(0, dh3, h1, sv0, lw0, tok)
    send0, tok = _exchange_start([slabs(g) for g in mats0], scatter=True, name="exchange_layer0_start", dep=mats0[0])
    dog = _matmul(dh1, wa_out, tb=True, dep=tok, name="l0_mix_out_dx")
    d_wa_out = _matmul(og, dh1, ta=True, out_dtype=BF16, name="l0_mix_out_dw")
    do, dz0, d_anorm = _gated_norm_bwd(o, proj, a_norm, dog, name="l0_gated_norm_bwd")
    cts = _delta_scan_bwd(prep, states, do, name="l0_delta_scan_bwd")
    dq, dk, dv, dgbc, dbbc = _delta_prep_bwd(q, k, v, gbc, bbc, tinv, cts, name="l0_delta_prep_bwd")
    dproj, d_aconv, d_alog, d_dtb = _delta_pre_bwd(proj, cv_a, alog_row, dtb_row, dq, dk, dv, dgbc, dbbc, dz0,
                                                   name="l0_delta_pre_bwd")
    d_wa_in_t = _matmul(dproj, n0, ta=True, out_dtype=BF16, name="l0_in_proj_dw")
    sendm, tok = _exchange_start([slabs(d_wa_in_t[:PROJ_A_REAL]), slabs(d_wa_out)], scatter=True,
                                 name="exchange_mixer0_start", dep=d_wa_in_t)
    dx, d_nm0 = _matmul_rows(dproj, wa_in_t, _epi_rms_bwd, [xs, dh1], [norm_mix[0]], [tile_f32, rowsum], dep=tok,
                             name="l0_in_proj_dx")

    recv1 = _exchange_wait(send1, dx, scatter=True, name="exchange_layer1_wait")
    recv0 = _exchange_wait(send0, recv1[0], scatter=True, name="exchange_layer0_wait")
    parts = {("b_w_in", 0): recv1[0], ("b_w_out", 0): recv1[1]}
    parts.update({(n, 1): r for n, r in zip(layer_names, recv1[2:])})
    parts.update({(n, 0): r for n, r in zip(layer_names, recv0)})

    outs = {}

    def update_matrix(name):
        res = None
        for layer in range(wts[name].shape[0]):
            g = parts[(name, layer)]
            if name in TRANSPOSED:
                g = _sum_parts(g, name=f"sum_{name}_{layer}").T
            res = _adamw_layer(g, wts[name], mom[name], var[name], layer, res, name=f"adamw_{name}_{layer}")
        for kind, arr in zip(("grad", "delta", "new_m", "new_v"), res):
            outs[(kind, name)] = arr
        return res

    last = [update_matrix(n) for n in ("b_w_in", "b_w_out") + layer_names][-1]
    recvm = _exchange_wait(sendm, last[0], scatter=True, name="exchange_mixer0_wait")
    parts.update({("a_w_in", 0): recvm[0], ("a_w_out", 0): recvm[1]})
    update_matrix("a_w_in")
    update_matrix("a_w_out")

    gconvs = dict(a_conv=d_aconv[None], f_conv=jnp.stack([g0["f_conv"], g1["f_conv"]]))
    small_g = dict(norm_mix=jnp.concatenate([d_nm0, d_nm1]), norm_ffn=jnp.concatenate([g0["norm_ffn"], g1["norm_ffn"]]),
                   norm_ple=jnp.concatenate([g0["norm_ple"], g1["norm_ple"]]), norm_final=d_norm_final[0],
                   a_log=d_alog[:, N_HEADS_A:2 * N_HEADS_A], a_dt_bias=d_dtb[:, N_HEADS_A:2 * N_HEADS_A],
                   a_norm=d_anorm, b_sinks=dsinks[:, :N_HEADS_B])
    recv_conv = _all_to_all(_pack_split(gconvs, CONVS, F32, 8), name="exchange_conv_grads")
    recv_small = _all_gather(_pack([small_g[n] for n in SMALL], F32, 8), name="gather_small_grads")
    for names, recv, tag in ((CONVS, recv_conv, "convs"), (SMALL, recv_small, "small")):
        shapes = [wts[n].shape for n in names]
        g_slab = _sum_parts(recv, name=f"sum_{tag}")
        packed = [_pack([d[n] for n in names], F32, 8) for d in (wts, mom, var)]
        res = _adamw_packed(g_slab, *packed, name=f"adamw_{tag}")
        for kind, slab in zip(("grad", "delta", "new_m", "new_v"), (g_slab,) + tuple(res)):
            for n, arr in zip(names, _unpack(slab, shapes)):
                outs[(kind, n)] = arr

    result = [loss, dx[None]]
    for kind in ("grad", "delta", "new_m", "new_v"):
        result += [outs[(kind, n)] for n in WEIGHTS]
    return tuple(result)
```

```python
import functools
import math

import jax
import jax.numpy as jnp
from jax import lax
from jax.experimental import pallas as pl
from jax.experimental.pallas import tpu as pltpu

F32 = jnp.float32
BF16 = jnp.bfloat16

D_MODEL = 1024
N_HEADS_A = 8
HEAD_DIM_A = 128
CONV_A = 4
CHUNK = 128
N_HEADS_B = 16
N_KV_B = 4
GROUP_B = N_HEADS_B // N_KV_B
HEAD_DIM_B = 64
WINDOW = 128
D_FF = 2816
FFN_CONV = 3
PLE_DIM = 256
EPS = 1e-6
N_DEV = 8
HALO = 8
PROJ_A_REAL = 4 * N_HEADS_A * HEAD_DIM_A + 2 * N_HEADS_A
PROJ_A = 4 * N_HEADS_A * HEAD_DIM_A + 128
Z_COL_BLOCK = 3
BA_COL_BLOCK = 32

ADAM_LR = 0.001
ADAM_B1 = 0.9
ADAM_B2 = 0.999
ADAM_EPS = 1e-08
ADAM_WD = 0.01
ADAM_STEP = 10

LANES = 128
VMEM_LIMIT_BYTES = 56 * 1024 * 1024
NEG_BIG = -1e30

MESH_AXES = ("x", "y", "c")


def _params(sem=None):
    return pltpu.CompilerParams(dimension_semantics=sem, vmem_limit_bytes=VMEM_LIMIT_BYTES)


def _tile(n, target):
    best = None
    for t in range(LANES, min(n, target) + 1, LANES):
        if n % t == 0:
            best = t
    return best or n


def _sigmoid(x):
    return 0.5 * jnp.tanh(0.5 * x) + 0.5


def _softplus(x):
    return jnp.maximum(x, 0.0) + jnp.log1p(jnp.exp(-jnp.abs(x)))


def _matmul(a, b, *, name, ta=False, tb=False, res=None, out_dtype=F32, tm=1408, tn=1408, tk=None, dep=None):
    sa, sb = a.ndim == 3, b.ndim == 3
    ns = a.shape[0] if sa else (b.shape[0] if sb else 1)
    contract_stack = sa and sb
    out_stacked = sa != sb
    m = a.shape[-1] if ta else a.shape[-2]
    k = a.shape[-2] if ta else a.shape[-1]
    n = b.shape[-2] if tb else b.shape[-1]
    assert (b.shape[-1] if tb else b.shape[-2]) == k, (a.shape, b.shape, ta, tb)
    if tk is None:
        tk = 1024 if ta else 2816
    tm, tn, tk = _tile(m, tm), _tile(n, tn), _tile(k, tk)
    nk = k // tk
    nsteps = nk * (ns if contract_stack else 1)
    dims = (((0 if ta else 1,), (1 if tb else 0,)), ((), ()))

    def spec(block, stacked, order):
        def index(g, i, j, kk):
            two = order(i, j, kk % nk)
            if not stacked:
                return two
            return (kk // nk if contract_stack else g,) + two
        return pl.BlockSpec(((None,) if stacked else ()) + block, index)

    a_spec = spec((tk, tm), sa, lambda i, j, kq: (kq, i)) if ta else spec((tm, tk), sa, lambda i, j, kq: (i, kq))
    b_spec = spec((tn, tk), sb, lambda i, j, kq: (j, kq)) if tb else spec((tk, tn), sb, lambda i, j, kq: (kq, j))
    o_spec = spec((tm, tn), out_stacked, lambda i, j, kq: (i, j))
    has_res = res is not None
    has_dep = dep is not None

    def body(*refs):
        a_ref, b_ref = refs[0], refs[1]
        r_ref = refs[2] if has_res else None
        o_ref = refs[2 + has_res + has_dep]
        part = lax.dot_general(a_ref[...].astype(BF16), b_ref[...].astype(BF16), dims, preferred_element_type=F32)

        def finish(acc):
            if has_res:
                acc = acc + r_ref[...].astype(F32)
            o_ref[...] = acc.astype(out_dtype)

        if nsteps == 1:
            finish(part)
        else:
            acc_ref = refs[-1]
            kk = pl.program_id(3)

            @pl.when(kk == 0)
            def _():
                acc_ref[...] = part

            @pl.when(kk > 0)
            def _():
                acc_ref[...] += part

            @pl.when(kk == nsteps - 1)
            def _():
                finish(acc_ref[...])

    in_specs = [a_spec, b_spec] + ([o_spec] if has_res else []) + ([pl.BlockSpec(memory_space=pl.ANY)] if has_dep else [])
    args = (a, b) + ((res,) if has_res else ()) + ((dep,) if has_dep else ())
    return pl.pallas_call(
        body,
        name=name,
        grid=(ns if out_stacked else 1, m // tm, n // tn, nsteps),
        in_specs=in_specs,
        out_specs=o_spec,
        out_shape=jax.ShapeDtypeStruct(((ns,) if out_stacked else ()) + (m, n), out_dtype),
        scratch_shapes=[pltpu.VMEM((tm, tn), F32)] if nsteps > 1 else [],
        compiler_params=_params(("parallel", "parallel", "parallel", "arbitrary")),
    )(*args)


EPI_ROWS = 32


def _matmul_rows(a, b, epilogue, tiles_in, rows_in, outs, *, name, tb=False, tm=512, tk=None, dep=None):
    stacked = a.ndim == 3
    ns = a.shape[0] if stacked else 1
    m, k = a.shape[-2], a.shape[-1]
    n = b.shape[-2] if tb else b.shape[-1]
    assert (b.shape[-1] if tb else b.shape[-2]) == k and (b.ndim == 3) == stacked, (a.shape, b.shape, tb)
    tm, tk = _tile(m, tm), _tile(k, 2816 if tk is None else tk)
    nk = k // tk
    nsteps = nk * ns
    dims = (((1,), (1 if tb else 0,)), ((), ()))
    lead = (None,) if stacked else ()
    front = (lambda kk: (kk // nk,)) if stacked else (lambda kk: ())
    a_spec = pl.BlockSpec(lead + (tm, tk), lambda i, kk: front(kk) + (i, kk % nk))
    if tb:
        b_spec = pl.BlockSpec(lead + (n, tk), lambda i, kk: front(kk) + (0, kk % nk))
    else:
        b_spec = pl.BlockSpec(lead + (tk, n), lambda i, kk: front(kk) + (kk % nk, 0))
    tile_spec = pl.BlockSpec((tm, n), lambda i, kk: (i, 0))
    row_spec = pl.BlockSpec((1, n), lambda i, kk: (0, 0))
    n_t, n_r, has_dep = len(tiles_in), len(rows_in), dep is not None

    def body(*refs):
        a_ref, b_ref = refs[:2]
        tile_refs = refs[2:2 + n_t]
        row_refs = refs[2 + n_t:2 + n_t + n_r]
        out_refs = refs[2 + n_t + n_r + has_dep:-1]
        acc_ref = refs[-1]
        part = lax.dot_general(a_ref[...].astype(BF16), b_ref[...].astype(BF16), dims, preferred_element_type=F32)
        kk = pl.program_id(1)
        if nsteps == 1:
            acc_ref[...] = part
        else:
            @pl.when(kk == 0)
            def _():
                acc_ref[...] = part

            @pl.when(kk > 0)
            def _():
                acc_ref[...] += part

        @pl.when(kk == nsteps - 1)
        def _():
            epilogue(acc_ref, tile_refs, row_refs, out_refs, pl.program_id(0) == 0)

    return pl.pallas_call(
        body,
        name=name,
        grid=(m // tm, nsteps),
        in_specs=[a_spec, b_spec] + [tile_spec] * n_t + [row_spec] * n_r + ([pl.BlockSpec(memory_space=pl.ANY)] if has_dep else []),
        out_specs=[tile_spec if kind == "tile" else row_spec for _, kind in outs],
        out_shape=[jax.ShapeDtypeStruct((m, n) if kind == "tile" else (1, n), dt) for dt, kind in outs],
        scratch_shapes=[pltpu.VMEM((tm, n), F32)],
        compiler_params=_params(("arbitrary", "arbitrary")),
    )(a, b, *tiles_in, *[r.reshape(1, n) for r in rows_in], *((dep,) if has_dep else ()))


def _row_chunks(ref):
    return [pl.ds(r, EPI_ROWS) for r in range(0, ref.shape[0], EPI_ROWS)]


def _epi_res_norm(acc, tiles, rows, outs, first):
    (res,), (w,), (h_out, n_out) = tiles, rows, outs
    for rs in _row_chunks(acc):
        h = acc[rs, :] + res[rs, :]
        h_out[rs, :] = h
        r = lax.rsqrt(jnp.mean(h * h, axis=-1, keepdims=True) + EPS)
        n_out[rs, :] = (h * r * w[...]).astype(BF16)


def _epi_ple(acc, tiles, rows, outs, first):
    hb, pe = tiles
    for rs in _row_chunks(acc):
        zg = acc[rs, :]
        outs[0][rs, :] = zg
        h = hb[rs, :] + _sigmoid(zg) * pe[rs, :]
        outs[1][rs, :] = h
        if rows:
            r = lax.rsqrt(jnp.mean(h * h, axis=-1, keepdims=True) + EPS)
            outs[2][rs, :] = (h * r * rows[0][...]).astype(BF16)


def _epi_rms_bwd(acc, tiles, rows, outs, first):
    (h_ref, skip), (w,), (dh_out, dw_out) = tiles, rows, outs
    dw = jnp.zeros((1, acc.shape[1]), F32)
    for rs in _row_chunks(acc):
        x = h_ref[rs, :]
        r = lax.rsqrt(jnp.mean(x * x, axis=-1, keepdims=True) + EPS)
        nh = x * r
        g = acc[rs, :]
        gw = g * w[...]
        dh_out[rs, :] = r * (gw - nh * jnp.mean(gw * nh, axis=-1, keepdims=True)) + skip[rs, :]
        dw = dw + jnp.sum(g * nh, axis=0, keepdims=True)

    @pl.when(first)
    def _():
        dw_out[...] = dw

    @pl.when(jnp.logical_not(first))
    def _():
        dw_out[...] += dw


def _rms_fwd(h, w, *, name, tm=512):
    t, d = h.shape
    tm = _tile(t, tm)

    def body(h_ref, w_ref, o_ref):
        x = h_ref[...]
        r = lax.rsqrt(jnp.mean(x * x, axis=-1, keepdims=True) + EPS)
        o_ref[...] = (x * r * w_ref[...]).astype(BF16)

    return pl.pallas_call(
        body,
        name=name,
        grid=(t // tm,),
        in_specs=[pl.BlockSpec((tm, d), lambda i: (i, 0)), pl.BlockSpec((1, d), lambda i: (0, 0))],
        out_specs=pl.BlockSpec((tm, d), lambda i: (i, 0)),
        out_shape=jax.ShapeDtypeStruct((t, d), BF16),
        compiler_params=_params(("parallel",)),
    )(h, w.reshape(1, d))


def _rms_bwd(h, w, dn, skip, *, name, tm=512):
    t, d = h.shape
    tm = _tile(t, tm)

    def body(h_ref, w_ref, dn_ref, skip_ref, dh_ref, dw_ref):
        i = pl.program_id(0)
        x = h_ref[...]
        r = lax.rsqrt(jnp.mean(x * x, axis=-1, keepdims=True) + EPS)
        nh = x * r
        g = dn_ref[...].astype(F32)
        gw = g * w_ref[...]
        dh_ref[...] = r * (gw - nh * jnp.mean(gw * nh, axis=-1, keepdims=True)) + skip_ref[...]
        part = jnp.sum(g * nh, axis=0, keepdims=True)

        @pl.when(i == 0)
        def _():
            dw_ref[...] = part

        @pl.when(i > 0)
        def _():
            dw_ref[...] += part

    row = pl.BlockSpec((tm, d), lambda i: (i, 0))
    vec = pl.BlockSpec((1, d), lambda i: (0, 0))
    return pl.pallas_call(
        body,
        name=name,
        grid=(t // tm,),
        in_specs=[row, vec, row, row],
        out_specs=[row, vec],
        out_shape=[jax.ShapeDtypeStruct((t, d), F32), jax.ShapeDtypeStruct((1, d), F32)],
        compiler_params=_params(("arbitrary",)),
    )(h, w.reshape(1, d), dn, skip)


def _final_loss(h, w, target, *, name, tm=512):
    t, d = h.shape
    tm = _tile(t, tm)

    def body(h_ref, w_ref, tg_ref, loss_ref, dh_ref, dw_ref):
        i = pl.program_id(0)
        x = h_ref[...]
        r = lax.rsqrt(jnp.mean(x * x, axis=-1, keepdims=True) + EPS)
        nh = x * r
        err = nh * w_ref[...] - tg_ref[...]
        lpart = (0.5 / d) * jnp.sum(jnp.sum(err * err, axis=-1, keepdims=True), axis=0, keepdims=True)
        g = err * (1.0 / d)
        gw = g * w_ref[...]
        dh_ref[...] = r * (gw - nh * jnp.mean(gw * nh, axis=-1, keepdims=True))
        part = jnp.sum(g * nh, axis=0, keepdims=True)
        lrow = jnp.broadcast_to(lpart, (1, LANES))

        @pl.when(i == 0)
        def _():
            dw_ref[...] = part
            loss_ref[...] = lrow

        @pl.when(i > 0)
        def _():
            dw_ref[...] += part
            loss_ref[...] += lrow

    row = pl.BlockSpec((tm, d), lambda i: (i, 0))
    vec = pl.BlockSpec((1, d), lambda i: (0, 0))
    return pl.pallas_call(
        body,
        name=name,
        grid=(t // tm,),
        in_specs=[row, vec, row],
        out_specs=[pl.BlockSpec((1, LANES), lambda i: (0, 0)), row, vec],
        out_shape=[jax.ShapeDtypeStruct((1, LANES), F32), jax.ShapeDtypeStruct((t, d), F32), jax.ShapeDtypeStruct((1, d), F32)],
        compiler_params=_params(("arbitrary",)),
    )(h, w.reshape(1, d), target)


def _conv_from_ext(ext_ref, cw_ref, kw, tm):
    y = cw_ref[kw - 1:kw, :] * ext_ref[pl.ds(HALO, tm), :]
    for i in range(kw - 1):
        y = y + cw_ref[i:i + 1, :] * ext_ref[pl.ds(HALO - (kw - 1) + i, tm), :]
    return y


ROW_CHUNK = 64


def _shifted_rows(src_ref, base, rows, shifts, cols=slice(None)):
    ext = src_ref[pl.ds(base - HALO, rows + HALO), cols]
    return [ext[HALO:, :] if s == 0 else pltpu.roll(ext, s, 0)[HALO:, :] for s in shifts]


def _conv_rows(src_ref, base, rows, cw_ref, kw, cols=slice(None)):
    wins = _shifted_rows(src_ref, base, rows, range(kw), cols)
    y = cw_ref[kw - 1:kw, cols] * wins[0]
    for s in range(1, kw):
        y = y + cw_ref[kw - 1 - s:kw - s, cols] * wins[s]
    return y


def _ahead_rows(src_ref, base, rows, shifts, cols=slice(None)):
    ext = src_ref[pl.ds(base, rows + HALO), cols]
    return [ext[:rows, :] if s == 0 else pltpu.roll(ext, rows + HALO - s, 0)[:rows, :] for s in shifts]


def _conv_t_rows(dy_ref, base, rows, cw_ref, kw, cols=slice(None)):
    wins = _ahead_rows(dy_ref, base, rows, range(kw), cols)
    dx = cw_ref[kw - 1:kw, cols] * wins[0]
    for s in range(1, kw):
        dx = dx + cw_ref[kw - 1 - s:kw - s, cols] * wins[s]
    return dx


def _conv_bwd_from_ext(xext_ref, dyext_ref, cw_ref, dcw_ref, kw, tm, first):
    dy = dyext_ref[pl.ds(0, tm), :]
    dx = cw_ref[kw - 1:kw, :] * dy
    for i in range(kw - 1):
        dx = dx + cw_ref[i:i + 1, :] * dyext_ref[pl.ds(kw - 1 - i, tm), :]
    for i in range(kw):
        part = jnp.sum(dy * xext_ref[pl.ds(HALO - (kw - 1) + i, tm), :], axis=0, keepdims=True)

        @pl.when(first)
        def _():
            dcw_ref[i:i + 1, :] = part

        @pl.when(jnp.logical_not(first))
        def _():
            dcw_ref[i:i + 1, :] += part

    return dx


def _delta_pre_fwd(proj, conv_w, alog_row, dtb_row, *, name, tm=256):
    t = proj.shape[0]
    c3 = 3 * N_HEADS_A * HEAD_DIM_A
    hk = N_HEADS_A * HEAD_DIM_A
    tm = _tile(t, tm)

    def body(x_ref, ba_ref, cw_ref, al_ref, db_ref, q_ref, k_ref, v_ref, g_ref, b_ref, ext):
        i = pl.program_id(0)

        @pl.when(i == 0)
        def _():
            ext[0:HALO, :] = jnp.zeros((HALO, c3), F32)

        ext[pl.ds(HALO, tm), :] = x_ref[...]
        y = _conv_from_ext(ext, cw_ref, CONV_A, tm)
        ext[0:HALO, :] = ext[pl.ds(tm, HALO), :]
        s = y * _sigmoid(y)
        for h in range(N_HEADS_A):
            lo = h * HEAD_DIM_A
            for dst, off in ((q_ref, 0), (k_ref, hk)):
                sh = s[:, off + lo:off + lo + HEAD_DIM_A]
                dst[:, lo:lo + HEAD_DIM_A] = sh * lax.rsqrt(jnp.sum(sh * sh, axis=-1, keepdims=True) + EPS)
        v_ref[...] = s[:, 2 * hk:3 * hk]
        ba = ba_ref[...]
        beta = _sigmoid(ba)
        gfull = -jnp.exp(al_ref[...]) * _softplus(ba + db_ref[...])
        for h in range(N_HEADS_A):
            lo = h * HEAD_DIM_A
            b_ref[:, lo:lo + HEAD_DIM_A] = jnp.broadcast_to(beta[:, h:h + 1], (tm, HEAD_DIM_A))
            g_ref[:, lo:lo + HEAD_DIM_A] = jnp.broadcast_to(gfull[:, N_HEADS_A + h:N_HEADS_A + h + 1], (tm, HEAD_DIM_A))

    row = lambda w: pl.BlockSpec((tm, w), lambda i: (i, 0))
    fixed = lambda r, w: pl.BlockSpec((r, w), lambda i: (0, 0))
    out = jax.ShapeDtypeStruct((t, hk), F32)
    return pl.pallas_call(
        body,
        name=name,
        grid=(t // tm,),
        in_specs=[row(c3), pl.BlockSpec((tm, LANES), lambda i: (i, BA_COL_BLOCK)), fixed(CONV_A, c3), fixed(1, LANES),
                  fixed(1, LANES)],
        out_specs=[row(hk)] * 5,
        out_shape=[out] * 5,
        scratch_shapes=[pltpu.VMEM((HALO + tm, c3), F32)],
        compiler_params=_params(("arbitrary",)),
    )(proj, proj, conv_w, alog_row, dtb_row)


def _delta_pre_bwd(proj, conv_w, alog_row, dtb_row, dq, dk, dv, dg, db, dz, *, name, tm=256):
    t, pw = proj.shape
    c3 = 3 * N_HEADS_A * HEAD_DIM_A
    hk = N_HEADS_A * HEAD_DIM_A
    tm = _tile(t, tm)
    nt = t // tm
    hb = tm // HALO

    def body(x_ref, xp_ref, ba_ref, cw_ref, al_ref, db_ref, dq_ref, dk_ref, dv_ref, dg_ref, dbt_ref, dz_ref,
             dp_ref, dcw_ref, dal_ref, ddb_ref, xext, dyext, carry):
        i = pl.program_id(0)
        first = i == 0
        tile = nt - 1 - i

        @pl.when(tile == 0)
        def _():
            xext[0:HALO, :] = jnp.zeros((HALO, c3), F32)

        @pl.when(tile > 0)
        def _():
            xext[0:HALO, :] = xp_ref[...]

        xext[pl.ds(HALO, tm), :] = x_ref[...]
        y = _conv_from_ext(xext, cw_ref, CONV_A, tm)
        sg = _sigmoid(y)
        s = y * sg
        dsilu = sg * (1.0 + y * (1.0 - sg))
        for h in range(N_HEADS_A):
            lo = h * HEAD_DIM_A
            for src, off in ((dq_ref, 0), (dk_ref, hk)):
                sh = s[:, off + lo:off + lo + HEAD_DIM_A]
                r = lax.rsqrt(jnp.sum(sh * sh, axis=-1, keepdims=True) + EPS)
                qn = sh * r
                gq = src[:, lo:lo + HEAD_DIM_A]
                dsh = r * (gq - qn * jnp.sum(gq * qn, axis=-1, keepdims=True))
                dyext[pl.ds(0, tm), off + lo:off + lo + HEAD_DIM_A] = dsh * dsilu[:, off + lo:off + lo + HEAD_DIM_A]
        dyext[pl.ds(0, tm), 2 * hk:3 * hk] = dv_ref[...] * dsilu[:, 2 * hk:3 * hk]

        @pl.when(first)
        def _():
            dyext[pl.ds(tm, HALO), :] = jnp.zeros((HALO, c3), F32)

        @pl.when(jnp.logical_not(first))
        def _():
            dyext[pl.ds(tm, HALO), :] = carry[...]

        dx = _conv_bwd_from_ext(xext, dyext, cw_ref, dcw_ref, CONV_A, tm, first)
        carry[...] = dyext[0:HALO, :]
        dp_ref[:, 0:c3] = dx.astype(BF16)
        dp_ref[:, c3:c3 + hk] = dz_ref[...]

        lane = lax.broadcasted_iota(jnp.int32, (tm, LANES), 1)
        gcol = jnp.zeros((tm, LANES), F32)
        for h in range(N_HEADS_A):
            lo = h * HEAD_DIM_A
            dbh = jnp.sum(dbt_ref[:, lo:lo + HEAD_DIM_A], axis=-1, keepdims=True)
            dgh = jnp.sum(dg_ref[:, lo:lo + HEAD_DIM_A], axis=-1, keepdims=True)
            gcol = gcol + jnp.where(lane == h, dbh, 0.0) + jnp.where(lane == N_HEADS_A + h, dgh, 0.0)
        ba = ba_ref[...]
        beta = _sigmoid(ba)
        a_neg = -jnp.exp(al_ref[...])
        z = ba + db_ref[...]
        dz = gcol * a_neg * _sigmoid(z)
        is_g = jnp.logical_and(lane >= N_HEADS_A, lane < 2 * N_HEADS_A)
        dba = jnp.where(lane < N_HEADS_A, gcol * beta * (1.0 - beta), jnp.where(is_g, dz, 0.0))
        dp_ref[:, c3 + hk:pw] = dba.astype(BF16)
        dal = jnp.sum(jnp.where(is_g, gcol * a_neg * _softplus(z), 0.0), axis=0, keepdims=True)
        ddb = jnp.sum(jnp.where(is_g, dz, 0.0), axis=0, keepdims=True)

        @pl.when(first)
        def _():
            dal_ref[...] = dal
            ddb_ref[...] = ddb

        @pl.when(jnp.logical_not(first))
        def _():
            dal_ref[...] += dal
            ddb_ref[...] += ddb

    rev = lambda w: pl.BlockSpec((tm, w), lambda i: (nt - 1 - i, 0))
    prev = pl.BlockSpec((HALO, c3), lambda i: (jnp.maximum((nt - 1 - i) * hb - 1, 0), 0))
    fixed = lambda r, w: pl.BlockSpec((r, w), lambda i: (0, 0))
    return pl.pallas_call(
        body,
        name=name,
        grid=(nt,),
        in_specs=[rev(c3), prev, pl.BlockSpec((tm, LANES), lambda i: (nt - 1 - i, BA_COL_BLOCK)), fixed(CONV_A, c3),
                  fixed(1, LANES), fixed(1, LANES)] + [rev(hk)] * 6,
        out_specs=[rev(pw), fixed(CONV_A, c3), fixed(1, LANES), fixed(1, LANES)],
        out_shape=[jax.ShapeDtypeStruct((t, pw), BF16), jax.ShapeDtypeStruct((CONV_A, c3), F32),
                   jax.ShapeDtypeStruct((1, LANES), F32), jax.ShapeDtypeStruct((1, LANES), F32)],
        scratch_shapes=[pltpu.VMEM((HALO + tm, c3), F32), pltpu.VMEM((tm + HALO, c3), F32), pltpu.VMEM((HALO, c3), F32)],
        compiler_params=_params(("arbitrary",)),
    )(proj, proj, proj, conv_w, alog_row, dtb_row, dq, dk, dv, dg, db, dz)


def _gated_norm_fwd(o, proj, w, *, name, tm=512):
    t, d = o.shape
    tm = _tile(t, tm)

    def body(o_ref, z_ref, w_ref, y_ref):
        for h in range(N_HEADS_A):
            sl = slice(h * HEAD_DIM_A, (h + 1) * HEAD_DIM_A)
            oh = o_ref[:, sl]
            zh = z_ref[:, sl]
            r = lax.rsqrt(jnp.mean(oh * oh, axis=-1, keepdims=True) + EPS)
            y_ref[:, sl] = (oh * r * w_ref[...] * (zh * _sigmoid(zh))).astype(BF16)

    row = pl.BlockSpec((tm, d), lambda i: (i, 0))
    return pl.pallas_call(
        body,
        name=name,
        grid=(t // tm,),
        in_specs=[row, pl.BlockSpec((tm, d), lambda i: (i, Z_COL_BLOCK)), pl.BlockSpec((1, HEAD_DIM_A), lambda i: (0, 0))],
        out_specs=row,
        out_shape=jax.ShapeDtypeStruct((t, d), BF16),
        compiler_params=_params(("parallel",)),
    )(o, proj, w)


def _gated_norm_bwd(o, proj, w, dy, *, name, tm=512):
    t, d = o.shape
    tm = _tile(t, tm)

    def body(o_ref, z_ref, w_ref, dy_ref, do_ref, dz_ref, dw_ref):
        i = pl.program_id(0)
        dw = jnp.zeros((1, HEAD_DIM_A), F32)
        for h in range(N_HEADS_A):
            sl = slice(h * HEAD_DIM_A, (h + 1) * HEAD_DIM_A)
            oh = o_ref[:, sl]
            zh = z_ref[:, sl]
            g = dy_ref[:, sl]
            r = lax.rsqrt(jnp.mean(oh * oh, axis=-1, keepdims=True) + EPS)
            nh = oh * r
            sg = _sigmoid(zh)
            dz_ref[:, sl] = (g * nh * w_ref[...] * (sg * (1.0 + zh * (1.0 - sg)))).astype(BF16)
            dt = g * (zh * sg)
            dw = dw + jnp.sum(dt * nh, axis=0, keepdims=True)
            dnh = dt * w_ref[...]
            do_ref[:, sl] = r * (dnh - nh * jnp.mean(dnh * nh, axis=-1, keepdims=True))

        @pl.when(i == 0)
        def _():
            dw_ref[...] = dw

        @pl.when(i > 0)
        def _():
            dw_ref[...] += dw

    row = pl.BlockSpec((tm, d), lambda i: (i, 0))
    vec = pl.BlockSpec((1, HEAD_DIM_A), lambda i: (0, 0))
    return pl.pallas_call(
        body,
        name=name,
        grid=(t // tm,),
        in_specs=[row, pl.BlockSpec((tm, d), lambda i: (i, Z_COL_BLOCK)), vec, row],
        out_specs=[row, row, vec],
        out_shape=[jax.ShapeDtypeStruct((t, d), F32), jax.ShapeDtypeStruct((t, d), BF16),
                   jax.ShapeDtypeStruct((1, HEAD_DIM_A), F32)],
        compiler_params=_params(("arbitrary",)),
    )(o, proj, w, dy)


_NN = (((1,), (0,)), ((), ()))
_NT = (((1,), (1,)), ((), ()))
_TN = (((0,), (0,)), ((), ()))
_DIMS = {"nn": _NN, "nt": _NT, "tn": _TN}


def _raw_dot(a, b, kind, prec):
    dims = _DIMS[kind]
    a_hi, b_hi = a.astype(BF16), b.astype(BF16)
    out = lax.dot_general(a_hi, b_hi, dims, preferred_element_type=F32)
    if prec == "x3":
        a_lo = (a - a_hi.astype(F32)).astype(BF16)
        b_lo = (b - b_hi.astype(F32)).astype(BF16)
        out = out + lax.dot_general(a_hi, b_lo, dims, preferred_element_type=F32)
        out = out + lax.dot_general(a_lo, b_hi, dims, preferred_element_type=F32)
    elif prec == "s3":
        r1 = b - b_hi.astype(F32)
        b_mid = r1.astype(BF16)
        b_lo = (r1 - b_mid.astype(F32)).astype(BF16)
        out = out + lax.dot_general(a_hi, b_mid, dims, preferred_element_type=F32)
        out = out + lax.dot_general(a_hi, b_lo, dims, preferred_element_type=F32)
    return out


def _raw_dots(xs, ys, kind, prec):
    return [_raw_dot(x, y, kind, prec) for x, y in zip(xs, ys)]


@functools.partial(jax.custom_vjp, nondiff_argnums=(2, 3))
def _dots(xs, ys, kind, prec):
    return _raw_dots(xs, ys, kind, prec)


def _dots_fwd(xs, ys, kind, prec):
    return _raw_dots(xs, ys, kind, prec), (xs, ys)


def _dots_bwd(kind, prec, saved, gs):
    xs, ys = saved
    if kind == "nn":
        return _raw_dots(gs, ys, "nt", prec), _raw_dots(xs, gs, "tn", prec)
    if kind == "nt":
        return _raw_dots(gs, ys, "nn", prec), _raw_dots(gs, xs, "tn", prec)
    return _raw_dots(ys, gs, "nt", prec), _raw_dots(xs, gs, "nn", prec)


_dots.defvjp(_dots_fwd, _dots_bwd)


def _eye(c):
    return (lax.broadcasted_iota(jnp.int32, (c, c), 0) == lax.broadcasted_iota(jnp.int32, (c, c), 1)).astype(F32)


def _inv_unit_lower_raw(lmats):
    c = lmats[0].shape[0]
    eye = _eye(c)
    xs = [eye - l for l in lmats]
    ps = lmats
    for _ in range(int(math.log2(c)) - 1):
        ps = _raw_dots(ps, ps, "nn", "bf16")
        xs = [x + d for x, d in zip(xs, _raw_dots(xs, ps, "nn", "bf16"))]
    rs = [x - eye + d for x, d in zip(xs, _raw_dots(lmats, xs, "nn", "x3"))]
    return [x - d for x, d in zip(xs, _raw_dots(xs, rs, "nn", "bf16"))]


@jax.custom_vjp
def _inv_unit_lower(lmats, hints):
    return _inv_unit_lower_raw(lmats) if hints is None else hints


def _inv_fwd(lmats, hints):
    tms = _inv_unit_lower_raw(lmats) if hints is None else hints
    return tms, (tms, hints)


def _inv_bwd(saved, gs):
    tms, hints = saved
    ds = [-d for d in _raw_dots(_raw_dots(tms, gs, "tn", "x3"), tms, "nt", "x3")]
    return ds, (None if hints is None else [jnp.zeros_like(h) for h in hints])


_inv_unit_lower.defvjp(_inv_fwd, _inv_bwd)


def _delta_prep(qs, ks, vs, gs, bs, hints=None):
    c = qs[0].shape[0]
    nh = len(qs)
    ii = lax.broadcasted_iota(jnp.int32, (c, c), 0)
    jj = lax.broadcasted_iota(jnp.int32, (c, c), 1)
    incl = ii >= jj
    strict = ii > jj
    ltri = incl.astype(F32)
    eye = _eye(c)
    m1 = _dots([ltri] * nh, gs, "nn", "s3")
    gtot = [jnp.sum(g, axis=0, keepdims=True) for g in gs]
    decay = [jnp.exp(jnp.where(incl, m - m.T, NEG_BIG)) for m in m1]
    eg = [jnp.exp(m) for m in m1]
    kk = _dots(ks, ks, "nt", "bf16")
    lmats = [jnp.where(strict, b * x * d, 0.0) for b, x, d in zip(bs, kk, decay)]
    tinv = _inv_unit_lower(lmats, hints)
    toff = [t - eye for t in tinv]
    bv = [b * v for b, v in zip(bs, vs)]
    bk = [b * e * k for b, e, k in zip(bs, eg, ks)]
    u0 = [x + d for x, d in zip(bv, _dots(toff, bv, "nn", "bf16"))]
    wk = [x + d for x, d in zip(bk, _dots(toff, bk, "nn", "bf16"))]
    qsc = [q * (HEAD_DIM_A ** -0.5) for q in qs]
    qk = [x * d for x, d in zip(_dots(qsc, ks, "nt", "bf16"), decay)]
    q_dec = [q * e for q, e in zip(qsc, eg)]
    k_dec = [k * jnp.exp(t - m) for k, t, m in zip(ks, gtot, m1)]
    glast = [jnp.broadcast_to(jnp.exp(t), (c, c)) for t in gtot]
    return (u0, wk, qk, q_dec, k_dec, glast), tinv


def _delta_step(ss, u0, wk, qk, q_dec, k_dec, glast):
    us = [a - d for a, d in zip(u0, _dots(wk, ss, "nn", "bf16"))]
    os_ = [a + d for a, d in zip(_dots(q_dec, ss, "nn", "bf16"), _dots(qk, us, "nn", "bf16"))]
    s_new = [g * s + d for g, s, d in zip(glast, ss, _dots(k_dec, us, "tn", "bf16"))]
    return os_, s_new


HEADS_PER_STEP = 8


def _chunk_spec(nc, reverse=False):
    w = HEADS_PER_STEP * HEAD_DIM_A
    if reverse:
        return pl.BlockSpec((CHUNK, w), lambda h, n: (nc - 1 - n, h))
    return pl.BlockSpec((CHUNK, w), lambda h, n: (n, h))


def _head_slices():
    return [slice(j * HEAD_DIM_A, (j + 1) * HEAD_DIM_A) for j in range(HEADS_PER_STEP)]


def _heads(ref):
    return [ref[:, sl] for sl in _head_slices()]


def _delta_prep_fwd(q, k, v, gbc, bbc, *, name):
    t, d = q.shape
    nc = t // CHUNK

    def body(q_ref, k_ref, v_ref, g_ref, b_ref, *outs):
        res, tinv = _delta_prep(*[_heads(r) for r in (q_ref, k_ref, v_ref, g_ref, b_ref)])
        for ref, vals in zip(outs, res + (tinv,)):
            for sl, val in zip(_head_slices(), vals):
                ref[:, sl] = val

    spec = _chunk_spec(nc)
    return pl.pallas_call(
        body,
        name=name,
        grid=(N_HEADS_A // HEADS_PER_STEP, nc),
        in_specs=[spec] * 5,
        out_specs=[spec] * 7,
        out_shape=[jax.ShapeDtypeStruct((t, d), F32)] * 7,
        compiler_params=_params(("parallel", "parallel")),
    )(q, k, v, gbc, bbc)


def _delta_prep_bwd(q, k, v, gbc, bbc, tinv, cts, *, name):
    t, d = q.shape
    nc = t // CHUNK

    def body(q_ref, k_ref, v_ref, g_ref, b_ref, t_ref, c0, c1, c2, c3, c4, c5, *outs):
        def f(q_, k_, v_, g_, b_):
            return _delta_prep(q_, k_, v_, g_, b_, hints=_heads(t_ref))[0]

        _, vjp = jax.vjp(f, *[_heads(r) for r in (q_ref, k_ref, v_ref, g_ref, b_ref)])
        grads = vjp(tuple(_heads(c) for c in (c0, c1, c2, c3, c4, c5)))
        for ref, vals in zip(outs, grads):
            for sl, val in zip(_head_slices(), vals):
                ref[:, sl] = val

    spec = _chunk_spec(nc)
    return pl.pallas_call(
        body,
        name=name,
        grid=(N_HEADS_A // HEADS_PER_STEP, nc),
        in_specs=[spec] * 12,
        out_specs=[spec] * 5,
        out_shape=[jax.ShapeDtypeStruct((t, d), F32)] * 5,
        compiler_params=_params(("parallel", "parallel")),
    )(q, k, v, gbc, bbc, tinv, *cts)


def _delta_scan_fwd(prep, *, name):
    t, d = prep[0].shape
    nc = t // CHUNK

    def body(u0, wk, qk, qd, kd, gl, o_ref, st_ref, s_ref):
        n = pl.program_id(1)

        @pl.when(n == 0)
        def _():
            s_ref[...] = jnp.zeros(s_ref.shape, F32)

        ss = [s_ref[j] for j in range(HEADS_PER_STEP)]
        os_, s_new = _delta_step(ss, *[_heads(r) for r in (u0, wk, qk, qd, kd, gl)])
        for j, sl in enumerate(_head_slices()):
            st_ref[:, sl] = ss[j]
            o_ref[:, sl] = os_[j]
            s_ref[j] = s_new[j]

    spec = _chunk_spec(nc)
    return pl.pallas_call(
        body,
        name=name,
        grid=(N_HEADS_A // HEADS_PER_STEP, nc),
        in_specs=[spec] * 6,
        out_specs=[spec] * 2,
        out_shape=[jax.ShapeDtypeStruct((t, d), F32)] * 2,
        scratch_shapes=[pltpu.VMEM((HEADS_PER_STEP, HEAD_DIM_A, HEAD_DIM_A), F32)],
        compiler_params=_params(("parallel", "arbitrary")),
    )(*prep)


def _delta_scan_bwd(prep, states, do, *, name):
    t, d = do.shape
    nc = t // CHUNK

    def body(u0, wk, qk, qd, kd, gl, st_ref, do_ref, *rest):
        outs, ds_ref = rest[:6], rest[6]
        n = pl.program_id(1)

        @pl.when(n == 0)
        def _():
            ds_ref[...] = jnp.zeros(ds_ref.shape, F32)

        _, vjp = jax.vjp(_delta_step, *[_heads(r) for r in (st_ref, u0, wk, qk, qd, kd, gl)])
        grads = vjp((_heads(do_ref), [ds_ref[j] for j in range(HEADS_PER_STEP)]))
        for j, sl in enumerate(_head_slices()):
            ds_ref[j] = grads[0][j]
            for ref, vals in zip(outs, grads[1:]):
                ref[:, sl] = vals[j]

    spec = _chunk_spec(nc, reverse=True)
    return pl.pallas_call(
        body,
        name=name,
        grid=(N_HEADS_A // HEADS_PER_STEP, nc),
        in_specs=[spec] * 8,
        out_specs=[spec] * 6,
        out_shape=[jax.ShapeDtypeStruct((t, d), F32)] * 6,
        scratch_shapes=[pltpu.VMEM((HEADS_PER_STEP, HEAD_DIM_A, HEAD_DIM_A), F32)],
        compiler_params=_params(("parallel", "arbitrary")),
    )(*prep, states, do)


def _alibi_slope(h):
    return 2.0 ** (-8.0 * (h + 1) / N_HEADS_B)


def _swa_load(sink_ref, q_ref, kp_ref, kc_ref, vp_ref, vc_ref):
    rg = lax.broadcasted_iota(jnp.int32, (GROUP_B * WINDOW, 1), 0) // WINDOW
    q4s, kcats, vcats, slopes, sinkcols = [], [], [], [], []
    for hk in range(N_KV_B):
        ks = slice(hk * HEAD_DIM_B, (hk + 1) * HEAD_DIM_B)
        heads = [hk * GROUP_B + g for g in range(GROUP_B)]
        q4s.append(jnp.concatenate([q_ref[:, h * HEAD_DIM_B:(h + 1) * HEAD_DIM_B] for h in heads], axis=0).astype(BF16))
        kcats.append(jnp.concatenate([kp_ref[:, ks], kc_ref[:, ks]], axis=0).astype(BF16))
        vcats.append(jnp.concatenate([vp_ref[:, ks], vc_ref[:, ks]], axis=0).astype(BF16))
        slope = jnp.zeros((GROUP_B * WINDOW, 1), F32)
        sink = jnp.zeros((GROUP_B * WINDOW, 1), F32)
        for g, h in enumerate(heads):
            slope = jnp.where(rg == g, _alibi_slope(h), slope)
            sink = jnp.where(rg == g, sink_ref[0, h], sink)
        slopes.append(slope)
        sinkcols.append(sink)
    return q4s, kcats, vcats, slopes, sinkcols


def _swa_probs(q4s, kcats, slopes, sinkcols, blk):
    rows = GROUP_B * WINDOW
    qi = lax.broadcasted_iota(jnp.int32, (rows, 2 * WINDOW), 0) % WINDOW
    kj = lax.broadcasted_iota(jnp.int32, (rows, 2 * WINDOW), 1)
    dist = qi + WINDOW - kj
    valid = (dist >= 0) & (dist < WINDOW) & (blk * WINDOW - WINDOW + kj >= 0)
    distf = dist.astype(F32)
    ss = [lax.dot_general(q, kc, _NT, preferred_element_type=F32) for q, kc in zip(q4s, kcats)]
    logits = [jnp.where(valid, s * (HEAD_DIM_B ** -0.5) - sl * distf, NEG_BIG) for s, sl in zip(ss, slopes)]
    ms = [jnp.maximum(jnp.max(l, axis=-1, keepdims=True), sk) for l, sk in zip(logits, sinkcols)]
    es = [jnp.exp(l - m) for l, m in zip(logits, ms)]
    esk = [jnp.exp(sk - m) for sk, m in zip(sinkcols, ms)]
    invs = [1.0 / (jnp.sum(e, axis=-1, keepdims=True) + k) for e, k in zip(es, esk)]
    return [e * i for e, i in zip(es, invs)], [k * i for k, i in zip(esk, invs)]


def _swa_fwd(proj, sinks, *, name):
    t = proj.shape[0]
    nb = t // WINDOW
    qd = N_HEADS_B * HEAD_DIM_B
    kd = N_KV_B * HEAD_DIM_B

    def body(sink_ref, q_ref, kp_ref, kc_ref, vp_ref, vc_ref, o_ref):
        blk = pl.program_id(0)
        q4s, kcats, vcats, slopes, sinkcols = _swa_load(sink_ref, q_ref, kp_ref, kc_ref, vp_ref, vc_ref)
        ps, _ = _swa_probs(q4s, kcats, slopes, sinkcols, blk)
        outs = [jnp.dot(p.astype(BF16), vc, preferred_element_type=F32) for p, vc in zip(ps, vcats)]
        for hk, out in enumerate(outs):
            for g in range(GROUP_B):
                h = hk * GROUP_B + g
                o_ref[:, h * HEAD_DIM_B:(h + 1) * HEAD_DIM_B] = out[g * WINDOW:(g + 1) * WINDOW, :].astype(BF16)

    q_spec = pl.BlockSpec((WINDOW, qd), lambda i: (i, 0))
    kv = lambda col, prev: pl.BlockSpec((WINDOW, kd), (lambda i: (jnp.maximum(i - 1, 0), col)) if prev else (lambda i: (i, col)))
    kcol, vcol = qd // kd, qd // kd + 1
    return pl.pallas_call(
        body,
        name=name,
        grid=(nb,),
        in_specs=[pl.BlockSpec(memory_space=pltpu.SMEM), q_spec, kv(kcol, True), kv(kcol, False), kv(vcol, True), kv(vcol, False)],
        out_specs=q_spec,
        out_shape=jax.ShapeDtypeStruct((t, qd), BF16),
        compiler_params=_params(("parallel",)),
    )(sinks, proj, proj, proj, proj, proj)


def _swa_bwd(proj, sinks, dout, *, name):
    t = proj.shape[0]
    nb = t // WINDOW
    qd = N_HEADS_B * HEAD_DIM_B
    kd = N_KV_B * HEAD_DIM_B
    scale = HEAD_DIM_B ** -0.5

    def body(sink_ref, q_ref, kp_ref, kc_ref, vp_ref, vc_ref, do_ref, dq_ref, dk_ref, dv_ref, dsk_ref):
        blk = pl.program_id(0)
        lane = lax.broadcasted_iota(jnp.int32, (1, LANES), 1)

        @pl.when(blk == 0)
        def _():
            dk_ref[...] = jnp.zeros((t, kd), F32)
            dv_ref[...] = jnp.zeros((t, kd), F32)
            dsk_ref[...] = jnp.zeros((1, LANES), F32)

        cur = pl.ds(pl.multiple_of(blk * WINDOW, WINDOW), WINDOW)
        prv = pl.ds(pl.multiple_of(jnp.maximum(blk - 1, 0) * WINDOW, WINDOW), WINDOW)
        q4s, kcats, vcats, slopes, sinkcols = _swa_load(sink_ref, q_ref, kp_ref, kc_ref, vp_ref, vc_ref)
        ps, psinks = _swa_probs(q4s, kcats, slopes, sinkcols, blk)
        do4s = [jnp.concatenate([do_ref[:, (hk * GROUP_B + g) * HEAD_DIM_B:(hk * GROUP_B + g + 1) * HEAD_DIM_B]
                                 for g in range(GROUP_B)], axis=0).astype(BF16) for hk in range(N_KV_B)]
        dps = [lax.dot_general(d, vc, _NT, preferred_element_type=F32) for d, vc in zip(do4s, vcats)]
        deltas = [jnp.sum(p * dp, axis=-1, keepdims=True) for p, dp in zip(ps, dps)]
        dss = [(p * (dp - dl) * scale).astype(BF16) for p, dp, dl in zip(ps, dps, deltas)]
        dq4s = [jnp.dot(ds, kc, preferred_element_type=F32) for ds, kc in zip(dss, kcats)]
        dkcs = [lax.dot_general(ds, q, _TN, preferred_element_type=F32) for ds, q in zip(dss, q4s)]
        dvcs = [lax.dot_general(p.astype(BF16), d, _TN, preferred_element_type=F32) for p, d in zip(ps, do4s)]
        dsk = jnp.zeros((1, LANES), F32)
        for hk in range(N_KV_B):
            ks = slice(hk * HEAD_DIM_B, (hk + 1) * HEAD_DIM_B)
            dsink = -psinks[hk] * deltas[hk]
            for g in range(GROUP_B):
                h = hk * GROUP_B + g
                rows = slice(g * WINDOW, (g + 1) * WINDOW)
                dq_ref[:, h * HEAD_DIM_B:(h + 1) * HEAD_DIM_B] = dq4s[hk][rows, :].astype(BF16)
                dsk = dsk + jnp.where(lane == h, jnp.sum(dsink[rows, :], axis=0, keepdims=True), 0.0)
            dk_ref[cur, ks] += dkcs[hk][WINDOW:, :]
            dv_ref[cur, ks] += dvcs[hk][WINDOW:, :]

            @pl.when(blk > 0)
            def _():
                dk_ref[prv, ks] += dkcs[hk][:WINDOW, :]
                dv_ref[prv, ks] += dvcs[hk][:WINDOW, :]

        dsk_ref[...] += dsk

    q_spec = pl.BlockSpec((WINDOW, qd), lambda i: (i, 0))
    kv = lambda col, prev: pl.BlockSpec((WINDOW, kd), (lambda i: (jnp.maximum(i - 1, 0), col)) if prev else (lambda i: (i, col)))
    kcol, vcol = qd // kd, qd // kd + 1
    full = pl.BlockSpec((t, kd), lambda i: (0, 0))
    return pl.pallas_call(
        body,
        name=name,
        grid=(nb,),
        in_specs=[pl.BlockSpec(memory_space=pltpu.SMEM), q_spec, kv(kcol, True), kv(kcol, False), kv(vcol, True), kv(vcol, False), q_spec],
        out_specs=[q_spec, full, full, pl.BlockSpec((1, LANES), lambda i: (0, 0))],
        out_shape=[jax.ShapeDtypeStruct((t, qd), BF16), jax.ShapeDtypeStruct((t, kd), F32),
                   jax.ShapeDtypeStruct((t, kd), F32), jax.ShapeDtypeStruct((1, LANES), F32)],
        compiler_params=_params(("arbitrary",)),
    )(sinks, proj, proj, proj, proj, proj, dout)


def _ffn_act_fwd(up, cw, *, name, tm=512, cb=256):
    _, t, f = up.shape
    tm, cb = _tile(t, tm), _tile(f, cb)

    rc = min(ROW_CHUNK, tm)

    def body(ug_ref, uv_ref, cg_ref, cv_ref, a_ref, hg, hv):
        i = pl.program_id(1)

        @pl.when(i == 0)
        def _():
            hg[0:HALO, :] = jnp.zeros((HALO, cb), F32)
            hv[0:HALO, :] = jnp.zeros((HALO, cb), F32)

        hg[pl.ds(HALO, rc), :] = ug_ref[0:rc, :]
        hv[pl.ds(HALO, rc), :] = uv_ref[0:rc, :]
        for r in range(tm // rc):
            if r == 0:
                yg = _conv_rows(hg, HALO, rc, cg_ref, FFN_CONV)
                yv = _conv_rows(hv, HALO, rc, cv_ref, FFN_CONV)
            else:
                yg = _conv_rows(ug_ref, r * rc, rc, cg_ref, FFN_CONV)
                yv = _conv_rows(uv_ref, r * rc, rc, cv_ref, FFN_CONV)
            a_ref[r * rc:(r + 1) * rc, :] = (yg * _sigmoid(yg) * yv).astype(BF16)
        hg[0:HALO, :] = ug_ref[tm - HALO:tm, :]
        hv[0:HALO, :] = uv_ref[tm - HALO:tm, :]

    ncb = f // cb
    half = lambda s: pl.BlockSpec((None, tm, cb), lambda c, i: (s, i, c))
    taps = lambda s: pl.BlockSpec((FFN_CONV, cb), lambda c, i: (0, c + s * ncb))
    return pl.pallas_call(
        body,
        name=name,
        grid=(ncb, t // tm),
        in_specs=[half(0), half(1), taps(0), taps(1)],
        out_specs=pl.BlockSpec((tm, cb), lambda c, i: (i, c)),
        out_shape=jax.ShapeDtypeStruct((t, f), BF16),
        scratch_shapes=[pltpu.VMEM((HALO + rc, cb), F32)] * 2,
        compiler_params=_params(("parallel", "arbitrary")),
    )(up, up, cw, cw)


def _ffn_act_bwd(up, cw, dact, *, name, tm=512, cb=256):
    _, t, f = up.shape
    tm, cb = _tile(t, tm), _tile(f, cb)
    nt = t // tm
    hb = tm // HALO

    rc = min(ROW_CHUNK, tm)
    nr = tm // rc
    kw = FFN_CONV

    def body(ug_ref, uv_ref, pg_ref, pv_ref, cg_ref, cv_ref, da_ref, du_ref, dcg_ref, dcv_ref,
             hg, hv, dyg, dyv):
        i = pl.program_id(1)
        first = i == 0
        tile = nt - 1 - i

        @pl.when(tile == 0)
        def _():
            hg[0:HALO, :] = jnp.zeros((HALO, cb), F32)
            hv[0:HALO, :] = jnp.zeros((HALO, cb), F32)

        @pl.when(tile > 0)
        def _():
            hg[0:HALO, :] = pg_ref[...]
            hv[0:HALO, :] = pv_ref[...]

        @pl.when(first)
        def _():
            dyg[pl.ds(tm, HALO), :] = jnp.zeros((HALO, cb), F32)
            dyv[pl.ds(tm, HALO), :] = jnp.zeros((HALO, cb), F32)

        hg[pl.ds(HALO, rc), :] = ug_ref[0:rc, :]
        hv[pl.ds(HALO, rc), :] = uv_ref[0:rc, :]
        dcg = [jnp.zeros((1, cb), F32) for _ in range(kw)]
        dcv = [jnp.zeros((1, cb), F32) for _ in range(kw)]
        for r in reversed(range(nr)):
            rows = slice(r * rc, (r + 1) * rc)
            src_g, src_v, base = (hg, hv, HALO) if r == 0 else (ug_ref, uv_ref, r * rc)
            yg = _conv_rows(src_g, base, rc, cg_ref, kw)
            yv = _conv_rows(src_v, base, rc, cv_ref, kw)
            sg = _sigmoid(yg)
            da = da_ref[rows, :]
            dy_g = da * yv * (sg * (1.0 + yg * (1.0 - sg)))
            dy_v = da * (yg * sg)
            dyg[rows, :] = dy_g
            dyv[rows, :] = dy_v
            du_ref[0, rows, :] = _conv_t_rows(dyg, r * rc, rc, cg_ref, kw).astype(BF16)
            du_ref[1, rows, :] = _conv_t_rows(dyv, r * rc, rc, cv_ref, kw).astype(BF16)
            for j in range(kw):
                dcg[j] = dcg[j] + jnp.sum(dy_g * src_g[pl.ds(base - (kw - 1) + j, rc), :], axis=0, keepdims=True)
                dcv[j] = dcv[j] + jnp.sum(dy_v * src_v[pl.ds(base - (kw - 1) + j, rc), :], axis=0, keepdims=True)
        dyg[pl.ds(tm, HALO), :] = dyg[0:HALO, :]
        dyv[pl.ds(tm, HALO), :] = dyv[0:HALO, :]
        for j in range(kw):
            @pl.when(first)
            def _():
                dcg_ref[j:j + 1, :] = dcg[j]
                dcv_ref[j:j + 1, :] = dcv[j]

            @pl.when(jnp.logical_not(first))
            def _():
                dcg_ref[j:j + 1, :] += dcg[j]
                dcv_ref[j:j + 1, :] += dcv[j]

    ncb = f // cb
    half = lambda s: pl.BlockSpec((None, tm, cb), lambda c, i: (s, nt - 1 - i, c))
    prev = lambda s: pl.BlockSpec((None, HALO, cb), lambda c, i: (s, jnp.maximum((nt - 1 - i) * hb - 1, 0), c))
    taps = lambda s: pl.BlockSpec((FFN_CONV, cb), lambda c, i: (0, c + s * ncb))
    dtaps = pl.BlockSpec((FFN_CONV, cb), lambda c, i: (0, c))
    return pl.pallas_call(
        body,
        name=name,
        grid=(ncb, nt),
        in_specs=[half(0), half(1), prev(0), prev(1), taps(0), taps(1), pl.BlockSpec((tm, cb), lambda c, i: (nt - 1 - i, c))],
        out_specs=[pl.BlockSpec((2, tm, cb), lambda c, i: (0, nt - 1 - i, c)), dtaps, dtaps],
        out_shape=[jax.ShapeDtypeStruct((2, t, f), BF16), jax.ShapeDtypeStruct((FFN_CONV, f), F32),
                   jax.ShapeDtypeStruct((FFN_CONV, f), F32)],
        scratch_shapes=[pltpu.VMEM((HALO + rc, cb), F32)] * 2 + [pltpu.VMEM((tm + HALO, cb), F32)] * 2,
        compiler_params=_params(("parallel", "arbitrary")),
    )(up, up, up, up, cw, cw, dact)


FFN_COL_TILE = 1408
FFN_SUB = 256
FFN_ROW_CHUNK = 32


def _sub_blocks(width):
    return [slice(c, min(c + FFN_SUB, width)) for c in range(0, width, FFN_SUB)]


def _ffn_up_act(n_f, w_up_t, cw, *, name, tm=512):
    t, d = n_f.shape
    f = w_up_t.shape[1]
    tm, tn = _tile(t, tm), _tile(f, FFN_COL_TILE)
    nj = f // tn
    rc = min(FFN_ROW_CHUNK, tm)
    kw = FFN_CONV

    def body(n_ref, wg_ref, wv_ref, cg_ref, cv_ref, up_ref, a_ref, hg, hv):
        i = pl.program_id(1)

        @pl.when(i == 0)
        def _():
            hg[0:HALO, :] = jnp.zeros((HALO, tn), F32)
            hv[0:HALO, :] = jnp.zeros((HALO, tn), F32)

        def products(cs):
            up_ref[0, :, cs] = lax.dot_general(n_ref[...], wg_ref[cs, :], _NT, preferred_element_type=F32)
            up_ref[1, :, cs] = lax.dot_general(n_ref[...], wv_ref[cs, :], _NT, preferred_element_type=F32)

        subs = _sub_blocks(tn)
        ug, uv = up_ref.at[0], up_ref.at[1]
        products(subs[0])
        for ci, cs in enumerate(subs):
            if ci + 1 < len(subs):
                products(subs[ci + 1])
            hg[pl.ds(HALO, rc), cs] = ug[0:rc, cs]
            hv[pl.ds(HALO, rc), cs] = uv[0:rc, cs]
            for r in range(tm // rc):
                src_g, src_v, base = (hg, hv, HALO) if r == 0 else (ug, uv, r * rc)
                yg = _conv_rows(src_g, base, rc, cg_ref, kw, cs)
                yv = _conv_rows(src_v, base, rc, cv_ref, kw, cs)
                a_ref[r * rc:(r + 1) * rc, cs] = (yg * _sigmoid(yg) * yv).astype(BF16)
            hg[0:HALO, cs] = ug[tm - HALO:tm, cs]
            hv[0:HALO, cs] = uv[tm - HALO:tm, cs]

    half = lambda s: pl.BlockSpec((None, tn, d), lambda j, i: (s, j, 0))
    taps = lambda s: pl.BlockSpec((kw, tn), lambda j, i: (0, j + s * nj))
    return pl.pallas_call(
        body,
        name=name,
        grid=(nj, t // tm),
        in_specs=[pl.BlockSpec((tm, d), lambda j, i: (i, 0)), half(0), half(1), taps(0), taps(1)],
        out_specs=[pl.BlockSpec((2, tm, tn), lambda j, i: (0, i, j)), pl.BlockSpec((tm, tn), lambda j, i: (i, j))],
        out_shape=[jax.ShapeDtypeStruct((2, t, f), F32), jax.ShapeDtypeStruct((t, f), BF16)],
        scratch_shapes=[pltpu.VMEM((HALO + rc, tn), F32)] * 2,
        compiler_params=_params(("parallel", "arbitrary")),
    )(n_f, w_up_t, w_up_t, cw, cw)


def _ffn_down_dx_act_bwd(dh, w_down, up, cw, *, name, tm=512):
    t, d = dh.shape
    f = w_down.shape[0]
    tm, tn = _tile(t, tm), _tile(f, FFN_COL_TILE)
    nj, nt = f // tn, t // tm
    hb = tm // HALO
    rc = min(FFN_ROW_CHUNK, tm)
    nr = tm // rc
    kw = FFN_CONV

    def body(dh_ref, wd_ref, ug_ref, uv_ref, pg_ref, pv_ref, cg_ref, cv_ref, du_ref, dcg_ref, dcv_ref,
             hg, hv, dyg, dyv, da_s, dh_s):
        i = pl.program_id(1)
        first = i == 0
        tile = nt - 1 - i

        @pl.when(tile == 0)
        def _():
            hg[0:HALO, :] = jnp.zeros((HALO, tn), F32)
            hv[0:HALO, :] = jnp.zeros((HALO, tn), F32)

        @pl.when(tile > 0)
        def _():
            hg[0:HALO, :] = pg_ref[...]
            hv[0:HALO, :] = pv_ref[...]

        @pl.when(first)
        def _():
            dyg[pl.ds(tm, HALO), :] = jnp.zeros((HALO, tn), F32)
            dyv[pl.ds(tm, HALO), :] = jnp.zeros((HALO, tn), F32)

        dh_s[...] = dh_ref[...].astype(BF16)

        def fold(x):
            out = x[0:HALO, :]
            for g in range(1, rc // HALO):
                out = out + x[g * HALO:(g + 1) * HALO, :]
            return out

        def product(cs):
            da_s[:, cs] = lax.dot_general(dh_s[...], wd_ref[cs, :], _NT, preferred_element_type=F32)

        subs = _sub_blocks(tn)
        product(subs[0])
        for ci, cs in enumerate(subs):
            width = cs.stop - cs.start
            if ci + 1 < len(subs):
                product(subs[ci + 1])
            hg[pl.ds(HALO, rc), cs] = ug_ref[0:rc, cs]
            hv[pl.ds(HALO, rc), cs] = uv_ref[0:rc, cs]
            dcg = [jnp.zeros((HALO, width), F32) for _ in range(kw)]
            dcv = [jnp.zeros((HALO, width), F32) for _ in range(kw)]
            for r in reversed(range(nr)):
                rows = slice(r * rc, (r + 1) * rc)
                src_g, src_v, base = (hg, hv, HALO) if r == 0 else (ug_ref, uv_ref, r * rc)
                win_g = _shifted_rows(src_g, base, rc, [kw - 1 - j for j in range(kw)], cs)
                win_v = _shifted_rows(src_v, base, rc, [kw - 1 - j for j in range(kw)], cs)
                yg = sum(cg_ref[j:j + 1, cs] * win_g[j] for j in range(kw))
                yv = sum(cv_ref[j:j + 1, cs] * win_v[j] for j in range(kw))
                sg = _sigmoid(yg)
                da = da_s[rows, cs]
                dy_g = da * yv * (sg * (1.0 + yg * (1.0 - sg)))
                dy_v = da * (yg * sg)
                dyg[rows, cs] = dy_g
                dyv[rows, cs] = dy_v
                du_ref[0, rows, cs] = _conv_t_rows(dyg, r * rc, rc, cg_ref, kw, cs).astype(BF16)
                du_ref[1, rows, cs] = _conv_t_rows(dyv, r * rc, rc, cv_ref, kw, cs).astype(BF16)
                for j in range(kw):
                    dcg[j] = dcg[j] + fold(dy_g * win_g[j])
                    dcv[j] = dcv[j] + fold(dy_v * win_v[j])
            dyg[pl.ds(tm, HALO), cs] = dyg[0:HALO, cs]
            dyv[pl.ds(tm, HALO), cs] = dyv[0:HALO, cs]
            for j in range(kw):
                tg = jnp.sum(dcg[j], axis=0, keepdims=True)
                tv = jnp.sum(dcv[j], axis=0, keepdims=True)

                @pl.when(first)
                def _():
                    dcg_ref[j:j + 1, cs] = tg
                    dcv_ref[j:j + 1, cs] = tv

                @pl.when(jnp.logical_not(first))
                def _():
                    dcg_ref[j:j + 1, cs] += tg
                    dcv_ref[j:j + 1, cs] += tv

    half = lambda s: pl.BlockSpec((None, tm, tn), lambda j, i: (s, nt - 1 - i, j))
    prev = lambda s: pl.BlockSpec((None, HALO, tn), lambda j, i: (s, jnp.maximum((nt - 1 - i) * hb - 1, 0), j))
    taps = lambda s: pl.BlockSpec((kw, tn), lambda j, i: (0, j + s * nj))
    dtaps = pl.BlockSpec((kw, tn), lambda j, i: (0, j))
    return pl.pallas_call(
        body,
        name=name,
        grid=(nj, nt),
        in_specs=[pl.BlockSpec((tm, d), lambda j, i: (nt - 1 - i, 0)), pl.BlockSpec((tn, d), lambda j, i: (j, 0)),
                  half(0), half(1), prev(0), prev(1), taps(0), taps(1)],
        out_specs=[pl.BlockSpec((2, tm, tn), lambda j, i: (0, nt - 1 - i, j)), dtaps, dtaps],
        out_shape=[jax.ShapeDtypeStruct((2, t, f), BF16), jax.ShapeDtypeStruct((kw, f), F32),
                   jax.ShapeDtypeStruct((kw, f), F32)],
        scratch_shapes=[pltpu.VMEM((HALO + rc, tn), F32)] * 2 + [pltpu.VMEM((tm + HALO, tn), F32)] * 2
        + [pltpu.VMEM((tm, tn), F32), pltpu.VMEM((tm, d), BF16)],
        compiler_params=_params(("parallel", "arbitrary")),
    )(dh, w_down, up, up, up, up, cw, cw)


def _ple_fwd(h, zg, pe, *, name, tm=512):
    t, d = h.shape
    tm = _tile(t, tm)

    def body(h_ref, z_ref, p_ref, o_ref):
        o_ref[...] = h_ref[...] + _sigmoid(z_ref[...]) * p_ref[...]

    row = pl.BlockSpec((tm, d), lambda i: (i, 0))
    return pl.pallas_call(
        body, name=name, grid=(t // tm,), in_specs=[row] * 3, out_specs=row,
        out_shape=jax.ShapeDtypeStruct((t, d), F32), compiler_params=_params(("parallel",)),
    )(h, zg, pe)


def _ple_bwd(dh, zg, pe, *, name, tm=512):
    t, d = dh.shape
    tm = _tile(t, tm)

    def body(g_ref, z_ref, p_ref, dz_ref, dp_ref):
        g = g_ref[...]
        sg = _sigmoid(z_ref[...])
        dz_ref[...] = (g * p_ref[...] * sg * (1.0 - sg)).astype(BF16)
        dp_ref[...] = (g * sg).astype(BF16)

    row = pl.BlockSpec((tm, d), lambda i: (i, 0))
    return pl.pallas_call(
        body, name=name, grid=(t // tm,), in_specs=[row] * 3, out_specs=[row] * 2,
        out_shape=[jax.ShapeDtypeStruct((t, d), BF16)] * 2, compiler_params=_params(("parallel",)),
    )(dh, zg, pe)


def _my_pos():
    return lax.axis_index("x"), lax.axis_index("y"), lax.axis_index("c")


def _all_gather(block, *, name, dep=None):
    r, w = block.shape
    has_dep = dep is not None

    def body(*refs):
        x_ref, out_ref, send_sems, recv_sems, local_sem = refs[:1] + refs[1 + has_dep:]
        x, y, c = _my_pos()
        me, sibling = (x, y, c), (x, y, 1 - c)
        chips = [(1 - x, y), (x, 1 - y), (1 - x, 1 - y)]

        def slot(px, py, pc):
            return out_ref.at[4 * px + 2 * py + pc]

        def copy(k, blk, to, src=None):
            return pltpu.make_async_remote_copy(
                src_ref=slot(*blk) if src is None else src, dst_ref=slot(*blk),
                send_sem=send_sems.at[k], recv_sem=recv_sems.at[k],
                device_id=to, device_id_type=pl.DeviceIdType.MESH)

        mine = pltpu.make_async_copy(x_ref, slot(*me), local_sem)
        mine.start()
        first = [copy(0, me, sibling, src=x_ref)]
        first += [copy(1 + j, me, (*chip, c), src=x_ref) for j, chip in enumerate(chips)]
        for cp in first:
            cp.start()
        passed = [copy(4 + j, (*chip, c), sibling) for j, chip in enumerate(chips)]
        for j, chip in enumerate(chips):
            copy(1 + j, (*chip, c), me).wait_recv()
            passed[j].start()
        copy(0, sibling, me).wait_recv()
        for j, chip in enumerate(chips):
            copy(4 + j, (*chip, 1 - c), me).wait_recv()
        for cp in first + passed:
            cp.wait_send()
        mine.wait()

    return pl.pallas_call(
        body,
        name=name,
        out_shape=jax.ShapeDtypeStruct((N_DEV, r, w), block.dtype),
        in_specs=[pl.BlockSpec(memory_space=pl.ANY)] * (1 + has_dep),
        out_specs=pl.BlockSpec(memory_space=pl.ANY),
        scratch_shapes=[pltpu.SemaphoreType.DMA((7,)), pltpu.SemaphoreType.DMA((7,)), pltpu.SemaphoreType.DMA],
    )(*((block, dep) if has_dep else (block,)))


def _all_to_all(slabs, *, name):
    n, r, w = slabs.shape

    def body(x_ref, out_ref, send_sems, recv_sems, local_sem):
        x, y, c = _my_pos()
        my_idx = 4 * x + 2 * y + c
        mine = pltpu.make_async_copy(x_ref.at[my_idx], out_ref.at[my_idx], local_sem)
        mine.start()
        copies = []
        for k in range(1, N_DEV):
            fx, fy, fc = (k >> 2) & 1, (k >> 1) & 1, k & 1
            px = (1 - x) if fx else x
            py = (1 - y) if fy else y
            pc = (1 - c) if fc else c
            cp = pltpu.make_async_remote_copy(
                src_ref=x_ref.at[4 * px + 2 * py + pc], dst_ref=out_ref.at[my_idx],
                send_sem=send_sems.at[k - 1], recv_sem=recv_sems.at[k - 1],
                device_id=(px, py, pc), device_id_type=pl.DeviceIdType.MESH)
            cp.start()
            copies.append(cp)
        for cp in copies:
            cp.wait_recv()
        for cp in copies:
            cp.wait_send()
        mine.wait()

    return pl.pallas_call(
        body,
        name=name,
        out_shape=jax.ShapeDtypeStruct((n, r, w), slabs.dtype),
        in_specs=[pl.BlockSpec(memory_space=pl.ANY)],
        out_specs=pl.BlockSpec(memory_space=pl.ANY),
        scratch_shapes=[pltpu.SemaphoreType.DMA((7,)), pltpu.SemaphoreType.DMA((7,)), pltpu.SemaphoreType.DMA],
    )(slabs)


def _exchange_copies(scatter, src_refs, land_refs, send_sems, recv_sems, local_sems):
    x, y, c = _my_pos()
    me = 4 * x + 2 * y + c
    local, remote = [], []
    for i, (s, l) in enumerate(zip(src_refs, land_refs)):
        local.append(pltpu.make_async_copy(s.at[me] if scatter else s, l.at[me], local_sems.at[i]))
        for k in range(1, N_DEV):
            px = (1 - x) if (k >> 2) & 1 else x
            py = (1 - y) if (k >> 1) & 1 else y
            pc = (1 - c) if k & 1 else c
            remote.append(pltpu.make_async_remote_copy(
                src_ref=s.at[4 * px + 2 * py + pc] if scatter else s, dst_ref=l.at[me],
                send_sem=send_sems.at[(N_DEV - 1) * i + k - 1], recv_sem=recv_sems.at[(N_DEV - 1) * i + k - 1],
                device_id=(px, py, pc), device_id_type=pl.DeviceIdType.MESH))
    return local, remote


def _exchange(arrays, *, scatter, name):
    n = len(arrays)

    def body(*refs):
        srcs, lands = refs[:n], refs[n:2 * n]
        local, remote = _exchange_copies(scatter, srcs, lands, *refs[2 * n:])
        for cp in local + remote:
            cp.start()
        for cp in remote:
            cp.wait_recv()
        for cp in remote:
            cp.wait_send()
        for cp in local:
            cp.wait()

    hbm = pl.BlockSpec(memory_space=pl.ANY)
    out = pl.pallas_call(
        body,
        name=name,
        out_shape=[jax.ShapeDtypeStruct(a.shape if scatter else (N_DEV,) + a.shape, a.dtype) for a in arrays],
        in_specs=[hbm] * n,
        out_specs=[hbm] * n,
        scratch_shapes=[pltpu.SemaphoreType.DMA(((N_DEV - 1) * n,)), pltpu.SemaphoreType.DMA(((N_DEV - 1) * n,)),
                        pltpu.SemaphoreType.DMA((n,))],
    )(*arrays)
    return list(out)


_HBM_SPEC = pl.BlockSpec(memory_space=pltpu.HBM)
_SEM_SPEC = pl.BlockSpec(memory_space=pltpu.SEMAPHORE)
_EFFECT = pltpu.SideEffectType.DATAFLOW_SIDE_EFFECTING


def _exchange_start(arrays, *, scatter, name, dep):
    n = len(arrays)
    srcs = [pltpu.with_memory_space_constraint(a, pltpu.HBM) for a in arrays]
    lands = [pltpu.with_memory_space_constraint(lax.empty(a.shape if scatter else (N_DEV,) + a.shape, a.dtype), pltpu.HBM)
             for a in arrays]

    def body(*refs):
        src_refs, land_refs = refs[:n], refs[n:2 * n]
        send_sems, recv_sems, local_sems = refs[2 * n + 1:2 * n + 4]
        token = refs[-1]
        local, remote = _exchange_copies(scatter, src_refs, land_refs, send_sems, recv_sems, local_sems)
        for cp in local + remote:
            cp.start()
        token[...] = jnp.zeros_like(token)

    sems = (pltpu.SemaphoreType.DMA(((N_DEV - 1) * n,)), pltpu.SemaphoreType.DMA(((N_DEV - 1) * n,)),
            pltpu.SemaphoreType.DMA((n,)))
    out = pl.pallas_call(
        body,
        name=name,
        out_shape=sems + tuple(pltpu.HBM(a.shape, a.dtype) for a in srcs + lands) + (jax.ShapeDtypeStruct((8, LANES), F32),),
        in_specs=[_HBM_SPEC] * (2 * n) + [pl.BlockSpec(memory_space=pl.ANY)],
        out_specs=(_SEM_SPEC,) * 3 + (_HBM_SPEC,) * (2 * n) + (pl.BlockSpec(memory_space=pltpu.VMEM),),
        input_output_aliases={i: 3 + i for i in range(2 * n)},
        compiler_params=pltpu.CompilerParams(has_side_effects=_EFFECT),
    )(*srcs, *lands, dep)
    return (out[:3], list(out[3:3 + n]), list(out[3 + n:3 + 2 * n])), out[-1]


def _exchange_wait(handle, after, *, scatter, name):
    sems, srcs, lands = handle
    n = len(srcs)

    def body(*refs):
        src_refs, land_refs = refs[:n], refs[n:2 * n]
        send_sems, recv_sems, local_sems = refs[2 * n:2 * n + 3]
        local, remote = _exchange_copies(scatter, src_refs, land_refs, send_sems, recv_sems, local_sems)
        for cp in remote:
            cp.wait_send()
            cp.wait_recv()
        for cp in local:
            cp.wait()

    out = pl.pallas_call(
        body,
        name=name,
        out_shape=tuple(pltpu.HBM(a.shape, a.dtype) for a in srcs + lands),
        in_specs=[_HBM_SPEC] * (2 * n) + [_SEM_SPEC] * 3 + [pl.BlockSpec(memory_space=pl.ANY)],
        out_specs=(_HBM_SPEC,) * (2 * n),
        input_output_aliases={i: i for i in range(2 * n)},
        compiler_params=pltpu.CompilerParams(has_side_effects=_EFFECT),
    )(*srcs, *lands, *sems, after)
    return list(out[n:])


def _sum_parts(parts, *, name, tr=512):
    n, r, lanes = parts.shape
    tr = tr if (r % tr == 0 and r > 1024) else r

    def body(p_ref, g_ref):
        g = p_ref[0].astype(F32)
        for j in range(1, n):
            g = g + p_ref[j].astype(F32)
        g_ref[...] = g

    row = pl.BlockSpec((tr, lanes), lambda i: (i, 0))
    return pl.pallas_call(
        body,
        name=name,
        grid=(r // tr,),
        in_specs=[pl.BlockSpec((n, tr, lanes), lambda i: (0, i, 0))],
        out_specs=row,
        out_shape=jax.ShapeDtypeStruct((r, lanes), F32),
        compiler_params=_params(("parallel",)),
    )(parts)


def _adamw_update(g, w, m, v):
    c1 = 1.0 / (1.0 - ADAM_B1 ** ADAM_STEP)
    c2 = 1.0 / (1.0 - ADAM_B2 ** ADAM_STEP)
    nm = ADAM_B1 * m + (1.0 - ADAM_B1) * g
    nv = ADAM_B2 * v + (1.0 - ADAM_B2) * (g * g)
    return -ADAM_LR * ((nm * c1) / (jnp.sqrt(nv * c2) + ADAM_EPS) + ADAM_WD * w), nm, nv


def _adamw_layer(g, w, m, v, layer, prev, *, name):
    nl, k, n = w.shape
    tr = max(d for d in range(8, min(k, 256) + 1, 8) if k % d == 0)
    in_parts = g.ndim == 3

    def body(g_ref, w_ref, m_ref, v_ref, *rest):
        go_ref, d_ref, nm_ref, nv_ref = rest[-4:]
        if in_parts:
            gg = g_ref[0].astype(F32)
            for j in range(1, g_ref.shape[0]):
                gg = gg + g_ref[j].astype(F32)
        else:
            gg = g_ref[...]
        d, nm, nv = _adamw_update(gg, w_ref[...], m_ref[...], v_ref[...])
        go_ref[...] = gg
        d_ref[...] = d
        nm_ref[...] = nm
        nv_ref[...] = nv

    lay = pl.BlockSpec((None, tr, n), lambda i: (layer, i, 0))
    n_prev = 0 if prev is None else 4
    out = jax.ShapeDtypeStruct((nl, k, n), F32)
    return pl.pallas_call(
        body,
        name=name,
        grid=(k // tr,),
        in_specs=[pl.BlockSpec((g.shape[0], tr, n), lambda i: (0, i, 0)) if in_parts else pl.BlockSpec((tr, n), lambda i: (i, 0)),
                  lay, lay, lay] + [pl.BlockSpec(memory_space=pl.ANY)] * n_prev,
        out_specs=[lay] * 4,
        out_shape=[out] * 4,
        input_output_aliases={4 + j: j for j in range(n_prev)},
        compiler_params=_params(("parallel",)),
    )(g, w, m, v, *(prev or ()))


def _adamw_packed(g, w, m, v, *, name, tr=512):
    r, lanes = g.shape
    tr = tr if r % tr == 0 else r
    c1 = 1.0 / (1.0 - ADAM_B1 ** ADAM_STEP)
    c2 = 1.0 / (1.0 - ADAM_B2 ** ADAM_STEP)

    def body(g_ref, w_ref, m_ref, v_ref, d_ref, nm_ref, nv_ref):
        g = g_ref[...]
        nm = ADAM_B1 * m_ref[...] + (1.0 - ADAM_B1) * g
        nv = ADAM_B2 * v_ref[...] + (1.0 - ADAM_B2) * (g * g)
        nm_ref[...] = nm
        nv_ref[...] = nv
        d_ref[...] = -ADAM_LR * ((nm * c1) / (jnp.sqrt(nv * c2) + ADAM_EPS) + ADAM_WD * w_ref[...])

    row = pl.BlockSpec((tr, lanes), lambda i: (i, 0))
    out = jax.ShapeDtypeStruct((r, lanes), F32)
    return pl.pallas_call(
        body,
        name=name,
        grid=(r // tr,),
        in_specs=[row] * 4,
        out_specs=[row] * 3,
        out_shape=[out] * 3,
        compiler_params=_params(("parallel",)),
    )(g, w, m, v)


BIG = ("a_w_in", "a_w_out", "b_w_in", "b_w_out", "f_w_up", "f_w_down", "ple_w_proj", "ple_w_gate")
CONVS = ("a_conv", "f_conv")
SMALL = ("norm_mix", "norm_ffn", "norm_ple", "norm_final", "a_log", "a_dt_bias", "a_norm", "b_sinks")
WEIGHTS = ("norm_mix", "norm_ffn", "norm_ple", "norm_final", "a_w_in", "a_conv", "a_log", "a_dt_bias", "a_norm",
           "a_w_out", "b_w_in", "b_sinks", "b_w_out", "f_w_up", "f_conv", "f_w_down", "ple_w_proj", "ple_w_gate")
SLAB_ROW_MULTIPLE = 512


def _pack(arrs, dtype, row_multiple):
    flat = jnp.concatenate([a.reshape(-1).astype(dtype) for a in arrs])
    rows = -(-flat.shape[0] // LANES)
    rows = -(-rows // row_multiple) * row_multiple
    return jnp.pad(flat, (0, rows * LANES - flat.shape[0])).reshape(rows, LANES)


def _unpack(slab, shapes):
    lead = slab.shape[:-2]
    flat = slab.reshape(lead + (-1,))
    out, off = [], 0
    for s in shapes:
        size = math.prod(s)
        out.append(flat[..., off:off + size].reshape(lead + tuple(s)))
        off += size
    return out


def _cols_full(g):
    g = jnp.moveaxis(g, 0, -2)
    return g.reshape(g.shape[:-2] + (g.shape[-2] * g.shape[-1],))


def _rows_full(g):
    g = jnp.moveaxis(g, 0, -3)
    return g.reshape(g.shape[:-3] + (g.shape[-3] * g.shape[-2], g.shape[-1]))


def _cols_split(wfull):
    n = wfull.shape[-1] // N_DEV
    g = wfull.reshape(wfull.shape[:-1] + (N_DEV, n))
    return jnp.moveaxis(g, -2, 0)


def _rows_split(wfull):
    k = wfull.shape[-2] // N_DEV
    g = wfull.reshape(wfull.shape[:-2] + (N_DEV, k, wfull.shape[-1]))
    return jnp.moveaxis(g, -3, 0)


TRANSPOSED = ("a_w_in", "b_w_in", "f_w_up", "ple_w_proj")


def _wire(name, a):
    return jnp.swapaxes(a, -1, -2) if name in TRANSPOSED else a


def _wire_shape(name, shape):
    return shape[:-2] + (shape[-1], shape[-2]) if name in TRANSPOSED else tuple(shape)


def _full(name, g):
    return _cols_full(g) if name in CONVS else _rows_full(g)


def _split(name, wfull):
    return _cols_split(wfull) if name in CONVS else _rows_split(wfull)


def _pack_split(grads, names, dtype, row_multiple):
    flat = jnp.concatenate([_split(n, grads[n]).reshape(N_DEV, -1).astype(dtype) for n in names], axis=1)
    rows = -(-flat.shape[1] // LANES)
    rows = -(-rows // row_multiple) * row_multiple
    return jnp.pad(flat, ((0, 0), (0, rows * LANES - flat.shape[1]))).reshape(N_DEV, rows, LANES)


def _pad_cols(a, width):
    return jnp.pad(a, ((0, 0), (0, width - a.shape[1])))


def kernel(x, p, norm_mix, norm_ffn, norm_ple, norm_final, a_w_in, a_conv, a_log, a_dt_bias, a_norm, a_w_out, b_w_in, b_sinks, b_w_out, f_w_up, f_conv, f_w_down, ple_w_proj, ple_w_gate, loss_target, m_norm_mix, m_norm_ffn, m_norm_ple, m_norm_final, m_a_w_in, m_a_conv, m_a_log, m_a_dt_bias, m_a_norm, m_a_w_out, m_b_w_in, m_b_sinks, m_b_w_out, m_f_w_up, m_f_conv, m_f_w_down, m_ple_w_proj, m_ple_w_gate, v_norm_mix, v_norm_ffn, v_norm_ple, v_norm_final, v_a_w_in, v_a_conv, v_a_log, v_a_dt_bias, v_a_norm, v_a_w_out, v_b_w_in, v_b_sinks, v_b_w_out, v_f_w_up, v_f_conv, v_f_w_down, v_ple_w_proj, v_ple_w_gate):
    wts = dict(norm_mix=norm_mix, norm_ffn=norm_ffn, norm_ple=norm_ple, norm_final=norm_final, a_w_in=a_w_in,
               a_conv=a_conv, a_log=a_log, a_dt_bias=a_dt_bias, a_norm=a_norm, a_w_out=a_w_out, b_w_in=b_w_in,
               b_sinks=b_sinks, b_w_out=b_w_out, f_w_up=f_w_up, f_conv=f_conv, f_w_down=f_w_down,
               ple_w_proj=ple_w_proj, ple_w_gate=ple_w_gate)
    mom = dict(norm_mix=m_norm_mix, norm_ffn=m_norm_ffn, norm_ple=m_norm_ple, norm_final=m_norm_final,
               a_w_in=m_a_w_in, a_conv=m_a_conv, a_log=m_a_log, a_dt_bias=m_a_dt_bias, a_norm=m_a_norm,
               a_w_out=m_a_w_out, b_w_in=m_b_w_in, b_sinks=m_b_sinks, b_w_out=m_b_w_out, f_w_up=m_f_w_up,
               f_conv=m_f_conv, f_w_down=m_f_w_down, ple_w_proj=m_ple_w_proj, ple_w_gate=m_ple_w_gate)
    var = dict(norm_mix=v_norm_mix, norm_ffn=v_norm_ffn, norm_ple=v_norm_ple, norm_final=v_norm_final,
               a_w_in=v_a_w_in, a_conv=v_a_conv, a_log=v_a_log, a_dt_bias=v_a_dt_bias, a_norm=v_a_norm,
               a_w_out=v_a_w_out, b_w_in=v_b_w_in, b_sinks=v_b_sinks, b_w_out=v_b_w_out, f_w_up=v_f_w_up,
               f_conv=v_f_conv, f_w_down=v_f_w_down, ple_w_proj=v_ple_w_proj, ple_w_gate=v_ple_w_gate)
    hk = N_HEADS_A * HEAD_DIM_A
    xs = x[0]
    tgt = loss_target[0]
    p_bf = p.astype(BF16)

    def shard(name, layer):
        return _wire(name, wts[name][layer]).astype(BF16)

    def stacked_rows(g):
        return g.reshape(g.shape[0] * g.shape[1], g.shape[2])

    n_in = a_w_in.shape[-1]
    first = _all_gather(jnp.concatenate([shard("a_w_in", 0), shard("a_w_out", 0)]), name="gather_mixer0")
    wa_in_t = jnp.pad(stacked_rows(first[:, :n_in]), ((0, PROJ_A - PROJ_A_REAL), (0, 0)))
    wa_out = stacked_rows(first[:, n_in:])
    gconv = _all_gather(_pack([wts[n] for n in CONVS], F32, 8), dep=first, name="gather_convs")
    conv_full = {n: _cols_full(g) for n, g in zip(CONVS, _unpack(gconv, [wts[n].shape for n in CONVS]))}
    cv_a, cv_f = conv_full["a_conv"][0], conv_full["f_conv"]
    layer_names = ("f_w_up", "f_w_down", "ple_w_proj", "ple_w_gate")
    gather0, tok = _exchange_start([shard(n, 0) for n in layer_names], scatter=False, name="gather_layer0_start", dep=gconv)
    gather1, tok = _exchange_start([shard(n, 0) for n in ("b_w_in", "b_w_out")] + [shard(n, 1) for n in layer_names],
                                   scatter=False, name="gather_layer1_start", dep=tok)

    alog_row = jnp.pad(a_log, ((0, 0), (N_HEADS_A, LANES - 2 * N_HEADS_A)))
    dtb_row = jnp.pad(a_dt_bias, ((0, 0), (N_HEADS_A, LANES - 2 * N_HEADS_A)))

    tile_f32, tile_bf16, rowsum = (F32, "tile"), (BF16, "tile"), (F32, "rowsum")

    def ffn_ple_fwd(i, h_a, n_f, next_norm, w_up_t, w_down, w_pp_t, w_pg):
        up, act = _ffn_up_act(n_f, w_up_t, cv_f[i], name=f"l{i}_ffn_up")
        h_b, n_p = _matmul_rows(act, w_down, _epi_res_norm, [h_a], [norm_ple[i]], [tile_f32, tile_bf16],
                                name=f"l{i}_ffn_down")
        pe = _matmul(p_bf[i, 0], w_pp_t, tb=True, name=f"l{i}_ple_proj")
        res = _matmul_rows(n_p, w_pg, _epi_ple, [h_b, pe], [] if next_norm is None else [next_norm],
                           [tile_f32, tile_f32] + ([] if next_norm is None else [tile_bf16]), name=f"l{i}_ple_gate")
        return res[1], (None if next_norm is None else res[2]), dict(n_f=n_f, up=up, act=act, h_b=h_b, n_p=n_p, zg=res[0], pe=pe)

    def layer_weights(lands):
        up_t, down, pp_t, pg = (stacked_rows(g) for g in lands)
        return up_t.reshape(2, D_FF, D_MODEL), down, pp_t, pg

    n0 = _rms_fwd(xs, norm_mix[0], name="l0_mix_norm")
    proj = _matmul(n0, wa_in_t, tb=True, tm=512, dep=tok, name="l0_in_proj")
    q, k, v, gbc, bbc = _delta_pre_fwd(proj, cv_a, alog_row, dtb_row, name="l0_delta_pre")
    *prep, tinv = _delta_prep_fwd(q, k, v, gbc, bbc, name="l0_delta_prep")
    o, states = _delta_scan_fwd(prep, name="l0_delta_scan")
    og = _gated_norm_fwd(o, proj, a_norm, name="l0_gated_norm")
    h1, nf0 = _matmul_rows(og, wa_out, _epi_res_norm, [xs], [norm_ffn[0]], [tile_f32, tile_bf16], name="l0_mix_out")
    lw0 = layer_weights(_exchange_wait(gather0, h1, scatter=False, name="gather_layer0_wait"))
    h3, n1, sv0 = ffn_ple_fwd(0, h1, nf0, norm_mix[1], *lw0)

    lands1 = _exchange_wait(gather1, h3, scatter=False, name="gather_layer1_wait")
    wb_in_t, wb_out = stacked_rows(lands1[0]), stacked_rows(lands1[1])
    lw1 = layer_weights(lands1[2:])
    pb = _matmul(n1, wb_in_t, tb=True, name="l1_in_qkv")
    att = _swa_fwd(pb, b_sinks, name="l1_swa")
    h4, nf1 = _matmul_rows(att, wb_out, _epi_res_norm, [h3], [norm_ffn[1]], [tile_f32, tile_bf16], name="l1_mix_out")
    h6, _, sv1 = ffn_ple_fwd(1, h4, nf1, None, *lw1)

    loss_row, dh6, d_norm_final = _final_loss(h6, norm_final, tgt, name="final_loss")
    loss = lax.psum(loss_row[0, 0], MESH_AXES)

    def ffn_ple_bwd(i, dh_c, h_a, sv, lw, dep):
        w_up_t, w_down, _, w_pg = lw
        dzg, dpe = _ple_bwd(dh_c, sv["zg"], sv["pe"], name=f"l{i}_ple_mix_bwd")
        d_pg = _matmul(sv["n_p"], dzg, ta=True, out_dtype=BF16, dep=dep, name=f"l{i}_ple_gate_dw")
        d_pp_t = _matmul(dpe, p_bf[i, 0], ta=True, out_dtype=BF16, name=f"l{i}_ple_proj_dw")
        dh_b, d_np = _matmul_rows(dzg, w_pg, _epi_rms_bwd, [sv["h_b"], dh_c], [norm_ple[i]], [tile_f32, rowsum], tb=True,
                                  name=f"l{i}_ple_gate_dx")
        d_down = _matmul(sv["act"], dh_b, ta=True, out_dtype=BF16, name=f"l{i}_ffn_down_dw")
        dup, d_cg, d_cv = _ffn_down_dx_act_bwd(dh_b, w_down, sv["up"], cv_f[i], name=f"l{i}_ffn_down_dx")
        d_up_t = _matmul(dup, sv["n_f"], ta=True, out_dtype=BF16, name=f"l{i}_ffn_up_dw")
        dh_a, d_nf = _matmul_rows(dup, w_up_t, _epi_rms_bwd, [h_a, dh_b], [norm_ffn[i]], [tile_f32, rowsum],
                                  name=f"l{i}_ffn_up_dx")
        mats = [d_up_t.reshape(2 * D_FF, D_MODEL), d_down, d_pp_t, d_pg]
        return dh_a, mats, dict(norm_ple=d_np, norm_ffn=d_nf, f_conv=jnp.concatenate([d_cg, d_cv], axis=1))

    def slabs(g):
        return g.reshape(N_DEV, g.shape[0] // N_DEV, g.shape[1])

    dh4, mats1, g1 = ffn_ple_bwd(1, dh6, h4, sv1, lw1, None)
    datt = _matmul(dh4, wb_out, tb=True, out_dtype=BF16, name="l1_mix_out_dx")
    d_wb_out = _matmul(att, dh4, ta=True, out_dtype=BF16, name="l1_mix_out_dw")
    dq_b, dk_b, dv_b, dsinks = _swa_bwd(pb, b_sinks, datt, name="l1_swa_bwd")
    dpb = jnp.concatenate([dq_b, dk_b.astype(BF16), dv_b.astype(BF16)], axis=1)
    d_wb_in_t = _matmul(dpb, n1, ta=True, out_dtype=BF16, name="l1_in_qkv_dw")
    send1, tok = _exchange_start([slabs(g) for g in [d_wb_in_t, d_wb_out] + mats1], scatter=True,
                                 name="exchange_layer1_start", dep=d_wb_in_t)
    dh3, d_nm1 = _matmul_rows(dpb, wb_in_t, _epi_rms_bwd, [h3, dh4], [norm_mix[1]], [tile_f32, rowsum], name="l1_in_qkv_dx")

    dh1, mats0, g0 = ffn_ple_bwd(0, dh3, h1, sv0, lw0, tok)
    send0, tok = _exchange_start([slabs(g) for g in mats0], scatter=True, name="exchange_layer0_start", dep=mats0[0])
    dog = _matmul(dh1, wa_out, tb=True, dep=tok, name="l0_mix_out_dx")
    d_wa_out = _matmul(og, dh1, ta=True, out_dtype=BF16, name="l0_mix_out_dw")
    do, dz0, d_anorm = _gated_norm_bwd(o, proj, a_norm, dog, name="l0_gated_norm_bwd")
    cts = _delta_scan_bwd(prep, states, do, name="l0_delta_scan_bwd")
    dq, dk, dv, dgbc, dbbc = _delta_prep_bwd(q, k, v, gbc, bbc, tinv, cts, name="l0_delta_prep_bwd")
    dproj, d_aconv, d_alog, d_dtb = _delta_pre_bwd(proj, cv_a, alog_row, dtb_row, dq, dk, dv, dgbc, dbbc, dz0,
                                                   name="l0_delta_pre_bwd")
    d_wa_in_t = _matmul(dproj, n0, ta=True, out_dtype=BF16, name="l0_in_proj_dw")
    sendm, tok = _exchange_start([slabs(d_wa_in_t[:PROJ_A_REAL]), slabs(d_wa_out)], scatter=True,
                                 name="exchange_mixer0_start", dep=d_wa_in_t)
    dx, d_nm0 = _matmul_rows(dproj, wa_in_t, _epi_rms_bwd, [xs, dh1], [norm_mix[0]], [tile_f32, rowsum], dep=tok,
                             name="l0_in_proj_dx")

    recv1 = _exchange_wait(send1, dx, scatter=True, name="exchange_layer1_wait")
    recv0 = _exchange_wait(send0, recv1[0], scatter=True, name="exchange_layer0_wait")
    parts = {("b_w_in", 0): recv1[0], ("b_w_out", 0): recv1[1]}
    parts.update({(n, 1): r for n, r in zip(layer_names, recv1[2:])})
    parts.update({(n, 0): r for n, r in zip(layer_names, recv0)})

    outs = {}

    def update_matrix(name):
        res = None
        for layer in range(wts[name].shape[0]):
            g = parts[(name, layer)]
            if name in TRANSPOSED:
                g = _sum_parts(g, name=f"sum_{name}_{layer}").T
            res = _adamw_layer(g, wts[name], mom[name], var[name], layer, res, name=f"adamw_{name}_{layer}")
        for kind, arr in zip(("grad", "delta", "new_m", "new_v"), res):
            outs[(kind, name)] = arr
        return res

    last = [update_matrix(n) for n in ("b_w_in", "b_w_out") + layer_names][-1]
    recvm = _exchange_wait(sendm, last[0], scatter=True, name="exchange_mixer0_wait")
    parts.update({("a_w_in", 0): recvm[0], ("a_w_out", 0): recvm[1]})
    update_matrix("a_w_in")
    update_matrix("a_w_out")

    gconvs = dict(a_conv=d_aconv[None], f_conv=jnp.stack([g0["f_conv"], g1["f_conv"]]))
    small_g = dict(norm_mix=jnp.concatenate([d_nm0, d_nm1]), norm_ffn=jnp.concatenate([g0["norm_ffn"], g1["norm_ffn"]]),
                   norm_ple=jnp.concatenate([g0["norm_ple"], g1["norm_ple"]]), norm_final=d_norm_final[0],
                   a_log=d_alog[:, N_HEADS_A:2 * N_HEADS_A], a_dt_bias=d_dtb[:, N_HEADS_A:2 * N_HEADS_A],
                   a_norm=d_anorm, b_sinks=dsinks[:, :N_HEADS_B])
    recv_conv = _all_to_all(_pack_split(gconvs, CONVS, F32, 8), name="exchange_conv_grads")
    recv_small = _all_gather(_pack([small_g[n] for n in SMALL], F32, 8), name="gather_small_grads")
    for names, recv, tag in ((CONVS, recv_conv, "convs"), (SMALL, recv_small, "small")):
        shapes = [wts[n].shape for n in names]
        g_slab = _sum_parts(recv, name=f"sum_{tag}")
        packed = [_pack([d[n] for n in names], F32, 8) for d in (wts, mom, var)]
        res = _adamw_packed(g_slab, *packed, name=f"adamw_{tag}")
        for kind, slab in zip(("grad", "delta", "new_m", "new_v"), (g_slab,) + tuple(res)):
            for n, arr in zip(names, _unpack(slab, shapes)):
                outs[(kind, n)] = arr

    result = [loss, dx[None]]
    for kind in ("grad", "delta", "new_m", "new_v"):
        result += [outs[(kind, n)] for n in WEIGHTS]
    return tuple(result)
```

```python
import functools
import math

import jax
import jax.numpy as jnp
from jax import lax
from jax.experimental import pallas as pl
from jax.experimental.pallas import tpu as pltpu

F32 = jnp.float32
BF16 = jnp.bfloat16

D_MODEL = 1024
N_HEADS_A = 8
HEAD_DIM_A = 128
CONV_A = 4
CHUNK = 128
N_HEADS_B = 16
N_KV_B = 4
GROUP_B = N_HEADS_B // N_KV_B
HEAD_DIM_B = 64
WINDOW = 128
D_FF = 2816
FFN_CONV = 3
PLE_DIM = 256
EPS = 1e-6
N_DEV = 8
HALO = 8
PROJ_A_REAL = 4 * N_HEADS_A * HEAD_DIM_A + 2 * N_HEADS_A
PROJ_A = 4 * N_HEADS_A * HEAD_DIM_A + 128
Z_COL_BLOCK = 3
BA_COL_BLOCK = 32

ADAM_LR = 0.001
ADAM_B1 = 0.9
ADAM_B2 = 0.999
ADAM_EPS = 1e-08
ADAM_WD = 0.01
ADAM_STEP = 10

LANES = 128
VMEM_LIMIT_BYTES = 56 * 1024 * 1024
NEG_BIG = -1e30

MESH_AXES = ("x", "y", "c")


def _params(sem=None):
    return pltpu.CompilerParams(dimension_semantics=sem, vmem_limit_bytes=VMEM_LIMIT_BYTES)


def _tile(n, target):
    best = None
    for t in range(LANES, min(n, target) + 1, LANES):
        if n % t == 0:
            best = t
    return best or n


def _sigmoid(x):
    return 0.5 * jnp.tanh(0.5 * x) + 0.5


def _softplus(x):
    return jnp.maximum(x, 0.0) + jnp.log1p(jnp.exp(-jnp.abs(x)))


def _matmul(a, b, *, name, ta=False, tb=False, res=None, out_dtype=F32, tm=1408, tn=1408, tk=None, dep=None):
    sa, sb = a.ndim == 3, b.ndim == 3
    ns = a.shape[0] if sa else (b.shape[0] if sb else 1)
    contract_stack = sa and sb
    out_stacked = sa != sb
    m = a.shape[-1] if ta else a.shape[-2]
    k = a.shape[-2] if ta else a.shape[-1]
    n = b.shape[-2] if tb else b.shape[-1]
    assert (b.shape[-1] if tb else b.shape[-2]) == k, (a.shape, b.shape, ta, tb)
    if tk is None:
        tk = 1024 if ta else 2816
    tm, tn, tk = _tile(m, tm), _tile(n, tn), _tile(k, tk)
    nk = k // tk
    nsteps = nk * (ns if contract_stack else 1)
    dims = (((0 if ta else 1,), (1 if tb else 0,)), ((), ()))

    def spec(block, stacked, order):
        def index(g, i, j, kk):
            two = order(i, j, kk % nk)
            if not stacked:
                return two
            return (kk // nk if contract_stack else g,) + two
        return pl.BlockSpec(((None,) if stacked else ()) + block, index)

    a_spec = spec((tk, tm), sa, lambda i, j, kq: (kq, i)) if ta else spec((tm, tk), sa, lambda i, j, kq: (i, kq))
    b_spec = spec((tn, tk), sb, lambda i, j, kq: (j, kq)) if tb else spec((tk, tn), sb, lambda i, j, kq: (kq, j))
    o_spec = spec((tm, tn), out_stacked, lambda i, j, kq: (i, j))
    has_res = res is not None
    has_dep = dep is not None

    def body(*refs):
        a_ref, b_ref = refs[0], refs[1]
        r_ref = refs[2] if has_res else None
        o_ref = refs[2 + has_res + has_dep]
        part = lax.dot_general(a_ref[...].astype(BF16), b_ref[...].astype(BF16), dims, preferred_element_type=F32)

        def finish(acc):
            if has_res:
                acc = acc + r_ref[...].astype(F32)
            o_ref[...] = acc.astype(out_dtype)

        if nsteps == 1:
            finish(part)
        else:
            acc_ref = refs[-1]
            kk = pl.program_id(3)

            @pl.when(kk == 0)
            def _():
                acc_ref[...] = part

            @pl.when(kk > 0)
            def _():
                acc_ref[...] += part

            @pl.when(kk == nsteps - 1)
            def _():
                finish(acc_ref[...])

    in_specs = [a_spec, b_spec] + ([o_spec] if has_res else []) + ([pl.BlockSpec(memory_space=pl.ANY)] if has_dep else [])
    args = (a, b) + ((res,) if has_res else ()) + ((dep,) if has_dep else ())
    return pl.pallas_call(
        body,
        name=name,
        grid=(ns if out_stacked else 1, m // tm, n // tn, nsteps),
        in_specs=in_specs,
        out_specs=o_spec,
        out_shape=jax.ShapeDtypeStruct(((ns,) if out_stacked else ()) + (m, n), out_dtype),
        scratch_shapes=[pltpu.VMEM((tm, tn), F32)] if nsteps > 1 else [],
        compiler_params=_params(("parallel", "parallel", "parallel", "arbitrary")),
    )(*args)


EPI_ROWS = 32


def _matmul_rows(a, b, epilogue, tiles_in, rows_in, outs, *, name, tb=False, tm=512, tk=None, dep=None):
    stacked = a.ndim == 3
    ns = a.shape[0] if stacked else 1
    m, k = a.shape[-2], a.shape[-1]
    n = b.shape[-2] if tb else b.shape[-1]
    assert (b.shape[-1] if tb else b.shape[-2]) == k and (b.ndim == 3) == stacked, (a.shape, b.shape, tb)
    tm, tk = _tile(m, tm), _tile(k, 2816 if tk is None else tk)
    nk = k // tk
    nsteps = nk * ns
    dims = (((1,), (1 if tb else 0,)), ((), ()))
    lead = (None,) if stacked else ()
    front = (lambda kk: (kk // nk,)) if stacked else (lambda kk: ())
    a_spec = pl.BlockSpec(lead + (tm, tk), lambda i, kk: front(kk) + (i, kk % nk))
    if tb:
        b_spec = pl.BlockSpec(lead + (n, tk), lambda i, kk: front(kk) + (0, kk % nk))
    else:
        b_spec = pl.BlockSpec(lead + (tk, n), lambda i, kk: front(kk) + (kk % nk, 0))
    tile_spec = pl.BlockSpec((tm, n), lambda i, kk: (i, 0))
    row_spec = pl.BlockSpec((1, n), lambda i, kk: (0, 0))
    n_t, n_r, has_dep = len(tiles_in), len(rows_in), dep is not None

    def body(*refs):
        a_ref, b_ref = refs[:2]
        tile_refs = refs[2:2 + n_t]
        row_refs = refs[2 + n_t:2 + n_t + n_r]
        out_refs = refs[2 + n_t + n_r + has_dep:-1]
        acc_ref = refs[-1]
        part = lax.dot_general(a_ref[...].astype(BF16), b_ref[...].astype(BF16), dims, preferred_element_type=F32)
        kk = pl.program_id(1)
        if nsteps == 1:
            acc_ref[...] = part
        else:
            @pl.when(kk == 0)
            def _():
                acc_ref[...] = part

            @pl.when(kk > 0)
            def _():
                acc_ref[...] += part

        @pl.when(kk == nsteps - 1)
        def _():
            epilogue(acc_ref, tile_refs, row_refs, out_refs, pl.program_id(0) == 0)

    return pl.pallas_call(
        body,
        name=name,
        grid=(m // tm, nsteps),
        in_specs=[a_spec, b_spec] + [tile_spec] * n_t + [row_spec] * n_r + ([pl.BlockSpec(memory_space=pl.ANY)] if has_dep else []),
        out_specs=[tile_spec if kind == "tile" else row_spec for _, kind in outs],
        out_shape=[jax.ShapeDtypeStruct((m, n) if kind == "tile" else (1, n), dt) for dt, kind in outs],
        scratch_shapes=[pltpu.VMEM((tm, n), F32)],
        compiler_params=_params(("arbitrary", "arbitrary")),
    )(a, b, *tiles_in, *[r.reshape(1, n) for r in rows_in], *((dep,) if has_dep else ()))


def _row_chunks(ref):
    return [pl.ds(r, EPI_ROWS) for r in range(0, ref.shape[0], EPI_ROWS)]


def _epi_res_norm(acc, tiles, rows, outs, first):
    (res,), (w,), (h_out, n_out) = tiles, rows, outs
    for rs in _row_chunks(acc):
        h = acc[rs, :] + res[rs, :]
        h_out[rs, :] = h
        r = lax.rsqrt(jnp.mean(h * h, axis=-1, keepdims=True) + EPS)
        n_out[rs, :] = (h * r * w[...]).astype(BF16)


def _epi_ple(acc, tiles, rows, outs, first):
    hb, pe = tiles
    for rs in _row_chunks(acc):
        zg = acc[rs, :]
        outs[0][rs, :] = zg
        h = hb[rs, :] + _sigmoid(zg) * pe[rs, :]
        outs[1][rs, :] = h
        if rows:
            r = lax.rsqrt(jnp.mean(h * h, axis=-1, keepdims=True) + EPS)
            outs[2][rs, :] = (h * r * rows[0][...]).astype(BF16)


def _epi_rms_bwd(acc, tiles, rows, outs, first):
    (h_ref, skip), (w,), (dh_out, dw_out) = tiles, rows, outs
    dw = jnp.zeros((1, acc.shape[1]), F32)
    for rs in _row_chunks(acc):
        x = h_ref[rs, :]
        r = lax.rsqrt(jnp.mean(x * x, axis=-1, keepdims=True) + EPS)
        nh = x * r
        g = acc[rs, :]
        gw = g * w[...]
        dh_out[rs, :] = r * (gw - nh * jnp.mean(gw * nh, axis=-1, keepdims=True)) + skip[rs, :]
        dw = dw + jnp.sum(g * nh, axis=0, keepdims=True)

    @pl.when(first)
    def _():
        dw_out[...] = dw

    @pl.when(jnp.logical_not(first))
    def _():
        dw_out[...] += dw


def _rms_fwd(h, w, *, name, tm=512):
    t, d = h.shape
    tm = _tile(t, tm)

    def body(h_ref, w_ref, o_ref):
        x = h_ref[...]
        r = lax.rsqrt(jnp.mean(x * x, axis=-1, keepdims=True) + EPS)
        o_ref[...] = (x * r * w_ref[...]).astype(BF16)

    return pl.pallas_call(
        body,
        name=name,
        grid=(t // tm,),
        in_specs=[pl.BlockSpec((tm, d), lambda i: (i, 0)), pl.BlockSpec((1, d), lambda i: (0, 0))],
        out_specs=pl.BlockSpec((tm, d), lambda i: (i, 0)),
        out_shape=jax.ShapeDtypeStruct((t, d), BF16),
        compiler_params=_params(("parallel",)),
    )(h, w.reshape(1, d))


def _rms_bwd(h, w, dn, skip, *, name, tm=512):
    t, d = h.shape
    tm = _tile(t, tm)

    def body(h_ref, w_ref, dn_ref, skip_ref, dh_ref, dw_ref):
        i = pl.program_id(0)
        x = h_ref[...]
        r = lax.rsqrt(jnp.mean(x * x, axis=-1, keepdims=True) + EPS)
        nh = x * r
        g = dn_ref[...].astype(F32)
        gw = g * w_ref[...]
        dh_ref[...] = r * (gw - nh * jnp.mean(gw * nh, axis=-1, keepdims=True)) + skip_ref[...]
        part = jnp.sum(g * nh, axis=0, keepdims=True)

        @pl.when(i == 0)
        def _():
            dw_ref[...] = part

        @pl.when(i > 0)
        def _():
            dw_ref[...] += part

    row = pl.BlockSpec((tm, d), lambda i: (i, 0))
    vec = pl.BlockSpec((1, d), lambda i: (0, 0))
    return pl.pallas_call(
        body,
        name=name,
        grid=(t // tm,),
        in_specs=[row, vec, row, row],
        out_specs=[row, vec],
        out_shape=[jax.ShapeDtypeStruct((t, d), F32), jax.ShapeDtypeStruct((1, d), F32)],
        compiler_params=_params(("arbitrary",)),
    )(h, w.reshape(1, d), dn, skip)


def _final_loss(h, w, target, *, name, tm=512):
    t, d = h.shape
    tm = _tile(t, tm)

    def body(h_ref, w_ref, tg_ref, loss_ref, dh_ref, dw_ref):
        i = pl.program_id(0)
        x = h_ref[...]
        r = lax.rsqrt(jnp.mean(x * x, axis=-1, keepdims=True) + EPS)
        nh = x * r
        err = nh * w_ref[...] - tg_ref[...]
        lpart = (0.5 / d) * jnp.sum(jnp.sum(err * err, axis=-1, keepdims=True), axis=0, keepdims=True)
        g = err * (1.0 / d)
        gw = g * w_ref[...]
        dh_ref[...] = r * (gw - nh * jnp.mean(gw * nh, axis=-1, keepdims=True))
        part = jnp.sum(g * nh, axis=0, keepdims=True)
        lrow = jnp.broadcast_to(lpart, (1, LANES))

        @pl.when(i == 0)
        def _():
            dw_ref[...] = part
            loss_ref[...] = lrow

        @pl.when(i > 0)
        def _():
            dw_ref[...] += part
            loss_ref[...] += lrow

    row = pl.BlockSpec((tm, d), lambda i: (i, 0))
    vec = pl.BlockSpec((1, d), lambda i: (0, 0))
    return pl.pallas_call(
        body,
        name=name,
        grid=(t // tm,),
        in_specs=[row, vec, row],
        out_specs=[pl.BlockSpec((1, LANES), lambda i: (0, 0)), row, vec],
        out_shape=[jax.ShapeDtypeStruct((1, LANES), F32), jax.ShapeDtypeStruct((t, d), F32), jax.ShapeDtypeStruct((1, d), F32)],
        compiler_params=_params(("arbitrary",)),
    )(h, w.reshape(1, d), target)


def _conv_from_ext(ext_ref, cw_ref, kw, tm):
    y = cw_ref[kw - 1:kw, :] * ext_ref[pl.ds(HALO, tm), :]
    for i in range(kw - 1):
        y = y + cw_ref[i:i + 1, :] * ext_ref[pl.ds(HALO - (kw - 1) + i, tm), :]
    return y


ROW_CHUNK = 64


def _shifted_rows(src_ref, base, rows, shifts, cols=slice(None)):
    ext = src_ref[pl.ds(base - HALO, rows + HALO), cols]
    return [ext[HALO:, :] if s == 0 else pltpu.roll(ext, s, 0)[HALO:, :] for s in shifts]


def _conv_rows(src_ref, base, rows, cw_ref, kw, cols=slice(None)):
    wins = _shifted_rows(src_ref, base, rows, range(kw), cols)
    y = cw_ref[kw - 1:kw, cols] * wins[0]
    for s in range(1, kw):
        y = y + cw_ref[kw - 1 - s:kw - s, cols] * wins[s]
    return y


def _ahead_rows(src_ref, base, rows, shifts, cols=slice(None)):
    ext = src_ref[pl.ds(base, rows + HALO), cols]
    return [ext[:rows, :] if s == 0 else pltpu.roll(ext, rows + HALO - s, 0)[:rows, :] for s in shifts]


def _conv_t_rows(dy_ref, base, rows, cw_ref, kw, cols=slice(None)):
    wins = _ahead_rows(dy_ref, base, rows, range(kw), cols)
    dx = cw_ref[kw - 1:kw, cols] * wins[0]
    for s in range(1, kw):
        dx = dx + cw_ref[kw - 1 - s:kw - s, cols] * wins[s]
    return dx


def _fold_rows(x):
    out = x[0:HALO, :]
    for g in range(1, x.shape[0] // HALO):
        out = out + x[g * HALO:(g + 1) * HALO, :]
    return out


def _conv_bwd_from_ext(xext_ref, dyext_ref, cw_ref, dcw_ref, kw, tm, first):
    dy = dyext_ref[pl.ds(0, tm), :]
    dx = cw_ref[kw - 1:kw, :] * dy
    for i in range(kw - 1):
        dx = dx + cw_ref[i:i + 1, :] * dyext_ref[pl.ds(kw - 1 - i, tm), :]
    for i in range(kw):
        part = jnp.sum(dy * xext_ref[pl.ds(HALO - (kw - 1) + i, tm), :], axis=0, keepdims=True)

        @pl.when(first)
        def _():
            dcw_ref[i:i + 1, :] = part

        @pl.when(jnp.logical_not(first))
        def _():
            dcw_ref[i:i + 1, :] += part

    return dx


def _delta_pre_fwd(proj, conv_w, alog_row, dtb_row, *, name, tm=256):
    t = proj.shape[0]
    c3 = 3 * N_HEADS_A * HEAD_DIM_A
    hk = N_HEADS_A * HEAD_DIM_A
    tm = _tile(t, tm)

    rc = min(ROW_CHUNK, tm)

    def body(x_ref, ba_ref, cw_ref, al_ref, db_ref, q_ref, k_ref, v_ref, g_ref, b_ref, hx):
        i = pl.program_id(0)

        @pl.when(i == 0)
        def _():
            hx[0:HALO, :] = jnp.zeros((HALO, c3), F32)

        hx[pl.ds(HALO, rc), :] = x_ref[0:rc, :]
        dsts = (q_ref, k_ref, v_ref)
        for r in range(tm // rc):
            rows = slice(r * rc, (r + 1) * rc)
            src, base = (hx, HALO) if r == 0 else (x_ref, r * rc)
            for cb in range(c3 // HEAD_DIM_A):
                y = _conv_rows(src, base, rc, cw_ref, CONV_A, slice(cb * HEAD_DIM_A, (cb + 1) * HEAD_DIM_A))
                s = y * _sigmoid(y)
                kind, h = divmod(cb, N_HEADS_A)
                if kind < 2:
                    s = s * lax.rsqrt(jnp.sum(s * s, axis=-1, keepdims=True) + EPS)
                dsts[kind][rows, h * HEAD_DIM_A:(h + 1) * HEAD_DIM_A] = s
        hx[0:HALO, :] = x_ref[tm - HALO:tm, :]
        ba = ba_ref[...]
        beta = _sigmoid(ba)
        gfull = -jnp.exp(al_ref[...]) * _softplus(ba + db_ref[...])
        for h in range(N_HEADS_A):
            lo = h * HEAD_DIM_A
            b_ref[:, lo:lo + HEAD_DIM_A] = jnp.broadcast_to(beta[:, h:h + 1], (tm, HEAD_DIM_A))
            g_ref[:, lo:lo + HEAD_DIM_A] = jnp.broadcast_to(gfull[:, N_HEADS_A + h:N_HEADS_A + h + 1], (tm, HEAD_DIM_A))

    row = lambda w: pl.BlockSpec((tm, w), lambda i: (i, 0))
    fixed = lambda r, w: pl.BlockSpec((r, w), lambda i: (0, 0))
    out = jax.ShapeDtypeStruct((t, hk), F32)
    return pl.pallas_call(
        body,
        name=name,
        grid=(t // tm,),
        in_specs=[row(c3), pl.BlockSpec((tm, LANES), lambda i: (i, BA_COL_BLOCK)), fixed(CONV_A, c3), fixed(1, LANES),
                  fixed(1, LANES)],
        out_specs=[row(hk)] * 5,
        out_shape=[out] * 5,
        scratch_shapes=[pltpu.VMEM((HALO + rc, c3), F32)],
        compiler_params=_params(("arbitrary",)),
    )(proj, proj, conv_w, alog_row, dtb_row)


def _delta_pre_bwd(proj, conv_w, alog_row, dtb_row, dq, dk, dv, dg, db, dz, *, name, tm=256):
    t, pw = proj.shape
    c3 = 3 * N_HEADS_A * HEAD_DIM_A
    hk = N_HEADS_A * HEAD_DIM_A
    tm = _tile(t, tm)
    nt = t // tm
    hb = tm // HALO

    rc = min(ROW_CHUNK, tm)
    kw = CONV_A

    def body(x_ref, xp_ref, ba_ref, cw_ref, al_ref, db_ref, dq_ref, dk_ref, dv_ref, dg_ref, dbt_ref, dz_ref,
             dp_ref, dcw_ref, dal_ref, ddb_ref, hx, dyext, acc):
        i = pl.program_id(0)
        first = i == 0
        tile = nt - 1 - i

        @pl.when(tile == 0)
        def _():
            hx[0:HALO, :] = jnp.zeros((HALO, c3), F32)

        @pl.when(tile > 0)
        def _():
            hx[0:HALO, :] = xp_ref[...]

        @pl.when(first)
        def _():
            dyext[pl.ds(tm, HALO), :] = jnp.zeros((HALO, c3), F32)

        hx[pl.ds(HALO, rc), :] = x_ref[0:rc, :]
        acc[...] = jnp.zeros(acc.shape, F32)
        srcs = (dq_ref, dk_ref, dv_ref)
        for r in reversed(range(tm // rc)):
            rows = slice(r * rc, (r + 1) * rc)
            src, base = (hx, HALO) if r == 0 else (x_ref, r * rc)
            for cb in range(c3 // HEAD_DIM_A):
                cols = slice(cb * HEAD_DIM_A, (cb + 1) * HEAD_DIM_A)
                kind, h = divmod(cb, N_HEADS_A)
                wins = _shifted_rows(src, base, rc, [kw - 1 - j for j in range(kw)], cols)
                y = sum(cw_ref[j:j + 1, cols] * wins[j] for j in range(kw))
                sg = _sigmoid(y)
                s = y * sg
                ds = srcs[kind][rows, h * HEAD_DIM_A:(h + 1) * HEAD_DIM_A]
                if kind < 2:
                    rn = lax.rsqrt(jnp.sum(s * s, axis=-1, keepdims=True) + EPS)
                    qn = s * rn
                    ds = rn * (ds - qn * jnp.sum(ds * qn, axis=-1, keepdims=True))
                dy = ds * (sg * (1.0 + y * (1.0 - sg)))
                dyext[rows, cols] = dy
                dp_ref[rows, cols] = _conv_t_rows(dyext, r * rc, rc, cw_ref, kw, cols).astype(BF16)
                for j in range(kw):
                    acc[j * HALO:(j + 1) * HALO, cols] += _fold_rows(dy * wins[j])
        dyext[pl.ds(tm, HALO), :] = dyext[0:HALO, :]
        for j in range(kw):
            taps = jnp.sum(acc[j * HALO:(j + 1) * HALO, :], axis=0, keepdims=True)

            @pl.when(first)
            def _():
                dcw_ref[j:j + 1, :] = taps

            @pl.when(jnp.logical_not(first))
            def _():
                dcw_ref[j:j + 1, :] += taps

        dp_ref[:, c3:c3 + hk] = dz_ref[...]

        lane = lax.broadcasted_iota(jnp.int32, (tm, LANES), 1)
        gcol = jnp.zeros((tm, LANES), F32)
        for h in range(N_HEADS_A):
            lo = h * HEAD_DIM_A
            dbh = jnp.sum(dbt_ref[:, lo:lo + HEAD_DIM_A], axis=-1, keepdims=True)
            dgh = jnp.sum(dg_ref[:, lo:lo + HEAD_DIM_A], axis=-1, keepdims=True)
            gcol = gcol + jnp.where(lane == h, dbh, 0.0) + jnp.where(lane == N_HEADS_A + h, dgh, 0.0)
        ba = ba_ref[...]
        beta = _sigmoid(ba)
        a_neg = -jnp.exp(al_ref[...])
        z = ba + db_ref[...]
        dz = gcol * a_neg * _sigmoid(z)
        is_g = jnp.logical_and(lane >= N_HEADS_A, lane < 2 * N_HEADS_A)
        dba = jnp.where(lane < N_HEADS_A, gcol * beta * (1.0 - beta), jnp.where(is_g, dz, 0.0))
        dp_ref[:, c3 + hk:pw] = dba.astype(BF16)
        dal = jnp.sum(jnp.where(is_g, gcol * a_neg * _softplus(z), 0.0), axis=0, keepdims=True)
        ddb = jnp.sum(jnp.where(is_g, dz, 0.0), axis=0, keepdims=True)

        @pl.when(first)
        def _():
            dal_ref[...] = dal
            ddb_ref[...] = ddb

        @pl.when(jnp.logical_not(first))
        def _():
            dal_ref[...] += dal
            ddb_ref[...] += ddb

    rev = lambda w: pl.BlockSpec((tm, w), lambda i: (nt - 1 - i, 0))
    prev = pl.BlockSpec((HALO, c3), lambda i: (jnp.maximum((nt - 1 - i) * hb - 1, 0), 0))
    fixed = lambda r, w: pl.BlockSpec((r, w), lambda i: (0, 0))
    return pl.pallas_call(
        body,
        name=name,
        grid=(nt,),
        in_specs=[rev(c3), prev, pl.BlockSpec((tm, LANES), lambda i: (nt - 1 - i, BA_COL_BLOCK)), fixed(CONV_A, c3),
                  fixed(1, LANES), fixed(1, LANES)] + [rev(hk)] * 6,
        out_specs=[rev(pw), fixed(CONV_A, c3), fixed(1, LANES), fixed(1, LANES)],
        out_shape=[jax.ShapeDtypeStruct((t, pw), BF16), jax.ShapeDtypeStruct((CONV_A, c3), F32),
                   jax.ShapeDtypeStruct((1, LANES), F32), jax.ShapeDtypeStruct((1, LANES), F32)],
        scratch_shapes=[pltpu.VMEM((HALO + rc, c3), F32), pltpu.VMEM((tm + HALO, c3), F32),
                        pltpu.VMEM((CONV_A * HALO, c3), F32)],
        compiler_params=_params(("arbitrary",)),
    )(proj, proj, proj, conv_w, alog_row, dtb_row, dq, dk, dv, dg, db, dz)


def _gated_norm_fwd(o, proj, w, *, name, tm=512):
    t, d = o.shape
    tm = _tile(t, tm)

    def body(o_ref, z_ref, w_ref, y_ref):
        for h in range(N_HEADS_A):
            sl = slice(h * HEAD_DIM_A, (h + 1) * HEAD_DIM_A)
            oh = o_ref[:, sl]
            zh = z_ref[:, sl]
            r = lax.rsqrt(jnp.mean(oh * oh, axis=-1, keepdims=True) + EPS)
            y_ref[:, sl] = (oh * r * w_ref[...] * (zh * _sigmoid(zh))).astype(BF16)

    row = pl.BlockSpec((tm, d), lambda i: (i, 0))
    return pl.pallas_call(
        body,
        name=name,
        grid=(t // tm,),
        in_specs=[row, pl.BlockSpec((tm, d), lambda i: (i, Z_COL_BLOCK)), pl.BlockSpec((1, HEAD_DIM_A), lambda i: (0, 0))],
        out_specs=row,
        out_shape=jax.ShapeDtypeStruct((t, d), BF16),
        compiler_params=_params(("parallel",)),
    )(o, proj, w)


def _gated_norm_bwd(o, proj, w, dy, *, name, tm=512):
    t, d = o.shape
    tm = _tile(t, tm)

    def body(o_ref, z_ref, w_ref, dy_ref, do_ref, dz_ref, dw_ref):
        i = pl.program_id(0)
        dw = jnp.zeros((1, HEAD_DIM_A), F32)
        for h in range(N_HEADS_A):
            sl = slice(h * HEAD_DIM_A, (h + 1) * HEAD_DIM_A)
            oh = o_ref[:, sl]
            zh = z_ref[:, sl]
            g = dy_ref[:, sl]
            r = lax.rsqrt(jnp.mean(oh * oh, axis=-1, keepdims=True) + EPS)
            nh = oh * r
            sg = _sigmoid(zh)
            dz_ref[:, sl] = (g * nh * w_ref[...] * (sg * (1.0 + zh * (1.0 - sg)))).astype(BF16)
            dt = g * (zh * sg)
            dw = dw + jnp.sum(dt * nh, axis=0, keepdims=True)
            dnh = dt * w_ref[...]
            do_ref[:, sl] = r * (dnh - nh * jnp.mean(dnh * nh, axis=-1, keepdims=True))

        @pl.when(i == 0)
        def _():
            dw_ref[...] = dw

        @pl.when(i > 0)
        def _():
            dw_ref[...] += dw

    row = pl.BlockSpec((tm, d), lambda i: (i, 0))
    vec = pl.BlockSpec((1, HEAD_DIM_A), lambda i: (0, 0))
    return pl.pallas_call(
        body,
        name=name,
        grid=(t // tm,),
        in_specs=[row, pl.BlockSpec((tm, d), lambda i: (i, Z_COL_BLOCK)), vec, row],
        out_specs=[row, row, vec],
        out_shape=[jax.ShapeDtypeStruct((t, d), F32), jax.ShapeDtypeStruct((t, d), BF16),
                   jax.ShapeDtypeStruct((1, HEAD_DIM_A), F32)],
        compiler_params=_params(("arbitrary",)),
    )(o, proj, w, dy)


_NN = (((1,), (0,)), ((), ()))
_NT = (((1,), (1,)), ((), ()))
_TN = (((0,), (0,)), ((), ()))
_DIMS = {"nn": _NN, "nt": _NT, "tn": _TN}


def _raw_dot(a, b, kind, prec):
    dims = _DIMS[kind]
    a_hi, b_hi = a.astype(BF16), b.astype(BF16)
    out = lax.dot_general(a_hi, b_hi, dims, preferred_element_type=F32)
    if prec == "x3":
        a_lo = (a - a_hi.astype(F32)).astype(BF16)
        b_lo = (b - b_hi.astype(F32)).astype(BF16)
        out = out + lax.dot_general(a_hi, b_lo, dims, preferred_element_type=F32)
        out = out + lax.dot_general(a_lo, b_hi, dims, preferred_element_type=F32)
    elif prec == "s3":
        r1 = b - b_hi.astype(F32)
        b_mid = r1.astype(BF16)
        b_lo = (r1 - b_mid.astype(F32)).astype(BF16)
        out = out + lax.dot_general(a_hi, b_mid, dims, preferred_element_type=F32)
        out = out + lax.dot_general(a_hi, b_lo, dims, preferred_element_type=F32)
    return out


def _raw_dots(xs, ys, kind, prec):
    return [_raw_dot(x, y, kind, prec) for x, y in zip(xs, ys)]


@functools.partial(jax.custom_vjp, nondiff_argnums=(2, 3))
def _dots(xs, ys, kind, prec):
    return _raw_dots(xs, ys, kind, prec)


def _dots_fwd(xs, ys, kind, prec):
    return _raw_dots(xs, ys, kind, prec), (xs, ys)


def _dots_bwd(kind, prec, saved, gs):
    xs, ys = saved
    if kind == "nn":
        return _raw_dots(gs, ys, "nt", prec), _raw_dots(xs, gs, "tn", prec)
    if kind == "nt":
        return _raw_dots(gs, ys, "nn", prec), _raw_dots(gs, xs, "tn", prec)
    return _raw_dots(ys, gs, "nt", prec), _raw_dots(xs, gs, "nn", prec)


_dots.defvjp(_dots_fwd, _dots_bwd)


def _eye(c):
    return (lax.broadcasted_iota(jnp.int32, (c, c), 0) == lax.broadcasted_iota(jnp.int32, (c, c), 1)).astype(F32)


def _inv_unit_lower_raw(lmats):
    c = lmats[0].shape[0]
    eye = _eye(c)
    xs = [eye - l for l in lmats]
    ps = lmats
    for _ in range(int(math.log2(c)) - 1):
        ps = _raw_dots(ps, ps, "nn", "bf16")
        xs = [x + d for x, d in zip(xs, _raw_dots(xs, ps, "nn", "bf16"))]
    rs = [x - eye + d for x, d in zip(xs, _raw_dots(lmats, xs, "nn", "x3"))]
    return [x - d for x, d in zip(xs, _raw_dots(xs, rs, "nn", "bf16"))]


@jax.custom_vjp
def _inv_unit_lower(lmats, hints):
    return _inv_unit_lower_raw(lmats) if hints is None else hints


def _inv_fwd(lmats, hints):
    tms = _inv_unit_lower_raw(lmats) if hints is None else hints
    return tms, (tms, hints)


def _inv_bwd(saved, gs):
    tms, hints = saved
    ds = [-d for d in _raw_dots(_raw_dots(tms, gs, "tn", "x3"), tms, "nt", "x3")]
    return ds, (None if hints is None else [jnp.zeros_like(h) for h in hints])


_inv_unit_lower.defvjp(_inv_fwd, _inv_bwd)


def _delta_prep(qs, ks, vs, gs, bs, hints=None):
    c = qs[0].shape[0]
    nh = len(qs)
    ii = lax.broadcasted_iota(jnp.int32, (c, c), 0)
    jj = lax.broadcasted_iota(jnp.int32, (c, c), 1)
    incl = ii >= jj
    strict = ii > jj
    ltri = incl.astype(F32)
    eye = _eye(c)
    m1 = _dots([ltri] * nh, gs, "nn", "s3")
    gtot = [jnp.sum(g, axis=0, keepdims=True) for g in gs]
    decay = [jnp.exp(jnp.where(incl, m - m.T, NEG_BIG)) for m in m1]
    eg = [jnp.exp(m) for m in m1]
    kk = _dots(ks, ks, "nt", "bf16")
    lmats = [jnp.where(strict, b * x * d, 0.0) for b, x, d in zip(bs, kk, decay)]
    tinv = _inv_unit_lower(lmats, hints)
    toff = [t - eye for t in tinv]
    bv = [b * v for b, v in zip(bs, vs)]
    bk = [b * e * k for b, e, k in zip(bs, eg, ks)]
    u0 = [x + d for x, d in zip(bv, _dots(toff, bv, "nn", "bf16"))]
    wk = [x + d for x, d in zip(bk, _dots(toff, bk, "nn", "bf16"))]
    qsc = [q * (HEAD_DIM_A ** -0.5) for q in qs]
    qk = [x * d for x, d in zip(_dots(qsc, ks, "nt", "bf16"), decay)]
    q_dec = [q * e for q, e in zip(qsc, eg)]
    k_dec = [k * jnp.exp(t - m) for k, t, m in zip(ks, gtot, m1)]
    glast = [jnp.broadcast_to(jnp.exp(t), (c, c)) for t in gtot]
    return (u0, wk, qk, q_dec, k_dec, glast), tinv


def _delta_step(ss, u0, wk, qk, q_dec, k_dec, glast):
    us = [a - d for a, d in zip(u0, _dots(wk, ss, "nn", "bf16"))]
    os_ = [a + d for a, d in zip(_dots(q_dec, ss, "nn", "bf16"), _dots(qk, us, "nn", "bf16"))]
    s_new = [g * s + d for g, s, d in zip(glast, ss, _dots(k_dec, us, "tn", "bf16"))]
    return os_, s_new


HEADS_PER_STEP = 8


def _chunk_spec(nc, reverse=False):
    w = HEADS_PER_STEP * HEAD_DIM_A
    if reverse:
        return pl.BlockSpec((CHUNK, w), lambda h, n: (nc - 1 - n, h))
    return pl.BlockSpec((CHUNK, w), lambda h, n: (n, h))


def _head_slices():
    return [slice(j * HEAD_DIM_A, (j + 1) * HEAD_DIM_A) for j in range(HEADS_PER_STEP)]


def _heads(ref):
    return [ref[:, sl] for sl in _head_slices()]


def _delta_prep_fwd(q, k, v, gbc, bbc, *, name):
    t, d = q.shape
    nc = t // CHUNK

    def body(q_ref, k_ref, v_ref, g_ref, b_ref, *outs):
        res, tinv = _delta_prep(*[_heads(r) for r in (q_ref, k_ref, v_ref, g_ref, b_ref)])
        for ref, vals in zip(outs, res + (tinv,)):
            for sl, val in zip(_head_slices(), vals):
                ref[:, sl] = val

    spec = _chunk_spec(nc)
    return pl.pallas_call(
        body,
        name=name,
        grid=(N_HEADS_A // HEADS_PER_STEP, nc),
        in_specs=[spec] * 5,
        out_specs=[spec] * 7,
        out_shape=[jax.ShapeDtypeStruct((t, d), F32)] * 7,
        compiler_params=_params(("parallel", "parallel")),
    )(q, k, v, gbc, bbc)


def _delta_prep_bwd(q, k, v, gbc, bbc, tinv, cts, *, name):
    t, d = q.shape
    nc = t // CHUNK

    def body(q_ref, k_ref, v_ref, g_ref, b_ref, t_ref, c0, c1, c2, c3, c4, c5, *outs):
        def f(q_, k_, v_, g_, b_):
            return _delta_prep(q_, k_, v_, g_, b_, hints=_heads(t_ref))[0]

        _, vjp = jax.vjp(f, *[_heads(r) for r in (q_ref, k_ref, v_ref, g_ref, b_ref)])
        grads = vjp(tuple(_heads(c) for c in (c0, c1, c2, c3, c4, c5)))
        for ref, vals in zip(outs, grads):
            for sl, val in zip(_head_slices(), vals):
                ref[:, sl] = val

    spec = _chunk_spec(nc)
    return pl.pallas_call(
        body,
        name=name,
        grid=(N_HEADS_A // HEADS_PER_STEP, nc),
        in_specs=[spec] * 12,
        out_specs=[spec] * 5,
        out_shape=[jax.ShapeDtypeStruct((t, d), F32)] * 5,
        compiler_params=_params(("parallel", "parallel")),
    )(q, k, v, gbc, bbc, tinv, *cts)


def _delta_scan_fwd(prep, *, name):
    t, d = prep[0].shape
    nc = t // CHUNK

    def body(u0, wk, qk, qd, kd, gl, o_ref, st_ref, s_ref):
        n = pl.program_id(1)

        @pl.when(n == 0)
        def _():
            s_ref[...] = jnp.zeros(s_ref.shape, F32)

        ss = [s_ref[j] for j in range(HEADS_PER_STEP)]
        os_, s_new = _delta_step(ss, *[_heads(r) for r in (u0, wk, qk, qd, kd, gl)])
        for j, sl in enumerate(_head_slices()):
            st_ref[:, sl] = ss[j]
            o_ref[:, sl] = os_[j]
            s_ref[j] = s_new[j]

    spec = _chunk_spec(nc)
    return pl.pallas_call(
        body,
        name=name,
        grid=(N_HEADS_A // HEADS_PER_STEP, nc),
        in_specs=[spec] * 6,
        out_specs=[spec] * 2,
        out_shape=[jax.ShapeDtypeStruct((t, d), F32)] * 2,
        scratch_shapes=[pltpu.VMEM((HEADS_PER_STEP, HEAD_DIM_A, HEAD_DIM_A), F32)],
        compiler_params=_params(("parallel", "arbitrary")),
    )(*prep)


def _delta_scan_bwd(prep, states, do, *, name):
    t, d = do.shape
    nc = t // CHUNK

    def body(u0, wk, qk, qd, kd, gl, st_ref, do_ref, *rest):
        outs, ds_ref = rest[:6], rest[6]
        n = pl.program_id(1)

        @pl.when(n == 0)
        def _():
            ds_ref[...] = jnp.zeros(ds_ref.shape, F32)

        _, vjp = jax.vjp(_delta_step, *[_heads(r) for r in (st_ref, u0, wk, qk, qd, kd, gl)])
        grads = vjp((_heads(do_ref), [ds_ref[j] for j in range(HEADS_PER_STEP)]))
        for j, sl in enumerate(_head_slices()):
            ds_ref[j] = grads[0][j]
            for ref, vals in zip(outs, grads[1:]):
                ref[:, sl] = vals[j]

    spec = _chunk_spec(nc, reverse=True)
    return pl.pallas_call(
        body,
        name=name,
        grid=(N_HEADS_A // HEADS_PER_STEP, nc),
        in_specs=[spec] * 8,
        out_specs=[spec] * 6,
        out_shape=[jax.ShapeDtypeStruct((t, d), F32)] * 6,
        scratch_shapes=[pltpu.VMEM((HEADS_PER_STEP, HEAD_DIM_A, HEAD_DIM_A), F32)],
        compiler_params=_params(("parallel", "arbitrary")),
    )(*prep, states, do)


def _delta_fwd(q, k, v, gbc, bbc, proj, norm_w, *, name):
    assert HEADS_PER_STEP == N_HEADS_A
    t, d = q.shape
    nc = t // CHUNK

    def body(q_ref, k_ref, v_ref, g_ref, b_ref, z_ref, w_ref, o_ref, st_ref, t_ref, og_ref, s_ref):
        n = pl.program_id(0)

        @pl.when(n == 0)
        def _():
            s_ref[...] = jnp.zeros(s_ref.shape, F32)

        ss = [s_ref[j] for j in range(N_HEADS_A)]
        res, tinv = _delta_prep(*[_heads(r) for r in (q_ref, k_ref, v_ref, g_ref, b_ref)])
        os_, s_new = _delta_step(ss, *res)
        for j, sl in enumerate(_head_slices()):
            st_ref[:, sl] = ss[j]
            t_ref[:, sl] = tinv[j]
            o_ref[:, sl] = os_[j]
            s_ref[j] = s_new[j]
            zh = z_ref[:, sl]
            r = lax.rsqrt(jnp.mean(os_[j] * os_[j], axis=-1, keepdims=True) + EPS)
            og_ref[:, sl] = (os_[j] * r * w_ref[...] * (zh * _sigmoid(zh))).astype(BF16)

    spec = pl.BlockSpec((CHUNK, d), lambda n: (n, 0))
    f32 = jax.ShapeDtypeStruct((t, d), F32)
    return pl.pallas_call(
        body,
        name=name,
        grid=(nc,),
        in_specs=[spec] * 5 + [pl.BlockSpec((CHUNK, d), lambda n: (n, Z_COL_BLOCK)), pl.BlockSpec((1, HEAD_DIM_A), lambda n: (0, 0))],
        out_specs=[spec] * 4,
        out_shape=[f32, f32, f32, jax.ShapeDtypeStruct((t, d), BF16)],
        scratch_shapes=[pltpu.VMEM((N_HEADS_A, HEAD_DIM_A, HEAD_DIM_A), F32)],
        compiler_params=_params(("arbitrary",)),
    )(q, k, v, gbc, bbc, proj, norm_w)


def _delta_bwd(q, k, v, gbc, bbc, tinv, states, o, proj, norm_w, dog, *, name):
    t, d = q.shape
    nc = t // CHUNK

    def body(q_ref, k_ref, v_ref, g_ref, b_ref, t_ref, st_ref, o_ref, z_ref, w_ref, dog_ref,
             dq_ref, dk_ref, dv_ref, dg_ref, db_ref, dz_ref, dw_ref, ds_ref):
        n = pl.program_id(0)

        @pl.when(n == 0)
        def _():
            ds_ref[...] = jnp.zeros(ds_ref.shape, F32)

        dos = []
        dw = jnp.zeros((1, HEAD_DIM_A), F32)
        for sl in _head_slices():
            oh, zh, g = o_ref[:, sl], z_ref[:, sl], dog_ref[:, sl]
            r = lax.rsqrt(jnp.mean(oh * oh, axis=-1, keepdims=True) + EPS)
            nh = oh * r
            sg = _sigmoid(zh)
            dz_ref[:, sl] = (g * nh * w_ref[...] * (sg * (1.0 + zh * (1.0 - sg)))).astype(BF16)
            dt = g * (zh * sg)
            dw = dw + jnp.sum(dt * nh, axis=0, keepdims=True)
            dnh = dt * w_ref[...]
            dos.append(r * (dnh - nh * jnp.mean(dnh * nh, axis=-1, keepdims=True)))

        @pl.when(n == 0)
        def _():
            dw_ref[...] = dw

        @pl.when(n > 0)
        def _():
            dw_ref[...] += dw

        def chunk(qs, ks, vs, gs, bs, ss):
            return _delta_step(ss, *_delta_prep(qs, ks, vs, gs, bs, hints=_heads(t_ref))[0])

        _, vjp = jax.vjp(chunk, *[_heads(r) for r in (q_ref, k_ref, v_ref, g_ref, b_ref, st_ref)])
        grads = vjp((dos, [ds_ref[j] for j in range(N_HEADS_A)]))
        for j, sl in enumerate(_head_slices()):
            ds_ref[j] = grads[5][j]
            for ref, vals in zip((dq_ref, dk_ref, dv_ref, dg_ref, db_ref), grads[:5]):
                ref[:, sl] = vals[j]

    spec = pl.BlockSpec((CHUNK, d), lambda n: (nc - 1 - n, 0))
    vec = pl.BlockSpec((1, HEAD_DIM_A), lambda n: (0, 0))
    f32 = jax.ShapeDtypeStruct((t, d), F32)
    return pl.pallas_call(
        body,
        name=name,
        grid=(nc,),
        in_specs=[spec] * 8 + [pl.BlockSpec((CHUNK, d), lambda n: (nc - 1 - n, Z_COL_BLOCK)), vec, spec],
        out_specs=[spec] * 6 + [vec],
        out_shape=[f32] * 5 + [jax.ShapeDtypeStruct((t, d), BF16), jax.ShapeDtypeStruct((1, HEAD_DIM_A), F32)],
        scratch_shapes=[pltpu.VMEM((N_HEADS_A, HEAD_DIM_A, HEAD_DIM_A), F32)],
        compiler_params=_params(("arbitrary",)),
    )(q, k, v, gbc, bbc, tinv, states, o, proj, norm_w, dog)


def _alibi_slope(h):
    return 2.0 ** (-8.0 * (h + 1) / N_HEADS_B)


def _swa_load(sink_ref, q_ref, kp_ref, kc_ref, vp_ref, vc_ref):
    rg = lax.broadcasted_iota(jnp.int32, (GROUP_B * WINDOW, 1), 0) // WINDOW
    q4s, kcats, vcats, slopes, sinkcols = [], [], [], [], []
    for hk in range(N_KV_B):
        ks = slice(hk * HEAD_DIM_B, (hk + 1) * HEAD_DIM_B)
        heads = [hk * GROUP_B + g for g in range(GROUP_B)]
        q4s.append(jnp.concatenate([q_ref[:, h * HEAD_DIM_B:(h + 1) * HEAD_DIM_B] for h in heads], axis=0).astype(BF16))
        kcats.append(jnp.concatenate([kp_ref[:, ks], kc_ref[:, ks]], axis=0).astype(BF16))
        vcats.append(jnp.concatenate([vp_ref[:, ks], vc_ref[:, ks]], axis=0).astype(BF16))
        slope = jnp.zeros((GROUP_B * WINDOW, 1), F32)
        sink = jnp.zeros((GROUP_B * WINDOW, 1), F32)
        for g, h in enumerate(heads):
            slope = jnp.where(rg == g, _alibi_slope(h), slope)
            sink = jnp.where(rg == g, sink_ref[0, h], sink)
        slopes.append(slope)
        sinkcols.append(sink)
    return q4s, kcats, vcats, slopes, sinkcols


def _swa_probs(q4s, kcats, slopes, sinkcols, blk):
    rows = GROUP_B * WINDOW
    qi = lax.broadcasted_iota(jnp.int32, (rows, 2 * WINDOW), 0) % WINDOW
    kj = lax.broadcasted_iota(jnp.int32, (rows, 2 * WINDOW), 1)
    dist = qi + WINDOW - kj
    valid = (dist >= 0) & (dist < WINDOW) & (blk * WINDOW - WINDOW + kj >= 0)
    distf = dist.astype(F32)
    ss = [lax.dot_general(q, kc, _NT, preferred_element_type=F32) for q, kc in zip(q4s, kcats)]
    logits = [jnp.where(valid, s * (HEAD_DIM_B ** -0.5) - sl * distf, NEG_BIG) for s, sl in zip(ss, slopes)]
    ms = [jnp.maximum(jnp.max(l, axis=-1, keepdims=True), sk) for l, sk in zip(logits, sinkcols)]
    es = [jnp.exp(l - m) for l, m in zip(logits, ms)]
    esk = [jnp.exp(sk - m) for sk, m in zip(sinkcols, ms)]
    invs = [1.0 / (jnp.sum(e, axis=-1, keepdims=True) + k) for e, k in zip(es, esk)]
    return [e * i for e, i in zip(es, invs)], [k * i for k, i in zip(esk, invs)]


def _swa_fwd(proj, sinks, *, name):
    t = proj.shape[0]
    nb = t // WINDOW
    qd = N_HEADS_B * HEAD_DIM_B
    kd = N_KV_B * HEAD_DIM_B

    def body(sink_ref, q_ref, kp_ref, kc_ref, vp_ref, vc_ref, o_ref):
        blk = pl.program_id(0)
        q4s, kcats, vcats, slopes, sinkcols = _swa_load(sink_ref, q_ref, kp_ref, kc_ref, vp_ref, vc_ref)
        ps, _ = _swa_probs(q4s, kcats, slopes, sinkcols, blk)
        outs = [jnp.dot(p.astype(BF16), vc, preferred_element_type=F32) for p, vc in zip(ps, vcats)]
        for hk, out in enumerate(outs):
            for g in range(GROUP_B):
                h = hk * GROUP_B + g
                o_ref[:, h * HEAD_DIM_B:(h + 1) * HEAD_DIM_B] = out[g * WINDOW:(g + 1) * WINDOW, :].astype(BF16)

    q_spec = pl.BlockSpec((WINDOW, qd), lambda i: (i, 0))
    kv = lambda col, prev: pl.BlockSpec((WINDOW, kd), (lambda i: (jnp.maximum(i - 1, 0), col)) if prev else (lambda i: (i, col)))
    kcol, vcol = qd // kd, qd // kd + 1
    return pl.pallas_call(
        body,
        name=name,
        grid=(nb,),
        in_specs=[pl.BlockSpec(memory_space=pltpu.SMEM), q_spec, kv(kcol, True), kv(kcol, False), kv(vcol, True), kv(vcol, False)],
        out_specs=q_spec,
        out_shape=jax.ShapeDtypeStruct((t, qd), BF16),
        compiler_params=_params(("parallel",)),
    )(sinks, proj, proj, proj, proj, proj)


def _swa_bwd(proj, sinks, dout, *, name):
    t = proj.shape[0]
    nb = t // WINDOW
    qd = N_HEADS_B * HEAD_DIM_B
    kd = N_KV_B * HEAD_DIM_B
    scale = HEAD_DIM_B ** -0.5

    def body(sink_ref, q_ref, kp_ref, kc_ref, vp_ref, vc_ref, do_ref, dq_ref, dk_ref, dv_ref, dsk_ref):
        blk = pl.program_id(0)
        lane = lax.broadcasted_iota(jnp.int32, (1, LANES), 1)

        @pl.when(blk == 0)
        def _():
            dk_ref[...] = jnp.zeros((t, kd), F32)
            dv_ref[...] = jnp.zeros((t, kd), F32)
            dsk_ref[...] = jnp.zeros((1, LANES), F32)

        cur = pl.ds(pl.multiple_of(blk * WINDOW, WINDOW), WINDOW)
        prv = pl.ds(pl.multiple_of(jnp.maximum(blk - 1, 0) * WINDOW, WINDOW), WINDOW)
        q4s, kcats, vcats, slopes, sinkcols = _swa_load(sink_ref, q_ref, kp_ref, kc_ref, vp_ref, vc_ref)
        ps, psinks = _swa_probs(q4s, kcats, slopes, sinkcols, blk)
        do4s = [jnp.concatenate([do_ref[:, (hk * GROUP_B + g) * HEAD_DIM_B:(hk * GROUP_B + g + 1) * HEAD_DIM_B]
                                 for g in range(GROUP_B)], axis=0).astype(BF16) for hk in range(N_KV_B)]
        dps = [lax.dot_general(d, vc, _NT, preferred_element_type=F32) for d, vc in zip(do4s, vcats)]
        deltas = [jnp.sum(p * dp, axis=-1, keepdims=True) for p, dp in zip(ps, dps)]
        dss = [(p * (dp - dl) * scale).astype(BF16) for p, dp, dl in zip(ps, dps, deltas)]
        dq4s = [jnp.dot(ds, kc, preferred_element_type=F32) for ds, kc in zip(dss, kcats)]
        dkcs = [lax.dot_general(ds, q, _TN, preferred_element_type=F32) for ds, q in zip(dss, q4s)]
        dvcs = [lax.dot_general(p.astype(BF16), d, _TN, preferred_element_type=F32) for p, d in zip(ps, do4s)]
        dsk = jnp.zeros((1, LANES), F32)
        for hk in range(N_KV_B):
            ks = slice(hk * HEAD_DIM_B, (hk + 1) * HEAD_DIM_B)
            dsink = -psinks[hk] * deltas[hk]
            for g in range(GROUP_B):
                h = hk * GROUP_B + g
                rows = slice(g * WINDOW, (g + 1) * WINDOW)
                dq_ref[:, h * HEAD_DIM_B:(h + 1) * HEAD_DIM_B] = dq4s[hk][rows, :].astype(BF16)
                dsk = dsk + jnp.where(lane == h, jnp.sum(dsink[rows, :], axis=0, keepdims=True), 0.0)
            dk_ref[cur, ks] += dkcs[hk][WINDOW:, :]
            dv_ref[cur, ks] += dvcs[hk][WINDOW:, :]

            @pl.when(blk > 0)
            def _():
                dk_ref[prv, ks] += dkcs[hk][:WINDOW, :]
                dv_ref[prv, ks] += dvcs[hk][:WINDOW, :]

        dsk_ref[...] += dsk

    q_spec = pl.BlockSpec((WINDOW, qd), lambda i: (i, 0))
    kv = lambda col, prev: pl.BlockSpec((WINDOW, kd), (lambda i: (jnp.maximum(i - 1, 0), col)) if prev else (lambda i: (i, col)))
    kcol, vcol = qd // kd, qd // kd + 1
    full = pl.BlockSpec((t, kd), lambda i: (0, 0))
    return pl.pallas_call(
        body,
        name=name,
        grid=(nb,),
        in_specs=[pl.BlockSpec(memory_space=pltpu.SMEM), q_spec, kv(kcol, True), kv(kcol, False), kv(vcol, True), kv(vcol, False), q_spec],
        out_specs=[q_spec, full, full, pl.BlockSpec((1, LANES), lambda i: (0, 0))],
        out_shape=[jax.ShapeDtypeStruct((t, qd), BF16), jax.ShapeDtypeStruct((t, kd), F32),
                   jax.ShapeDtypeStruct((t, kd), F32), jax.ShapeDtypeStruct((1, LANES), F32)],
        compiler_params=_params(("arbitrary",)),
    )(sinks, proj, proj, proj, proj, proj, dout)


def _ffn_act_fwd(up, cw, *, name, tm=512, cb=256):
    _, t, f = up.shape
    tm, cb = _tile(t, tm), _tile(f, cb)

    rc = min(ROW_CHUNK, tm)

    def body(ug_ref, uv_ref, cg_ref, cv_ref, a_ref, hg, hv):
        i = pl.program_id(1)

        @pl.when(i == 0)
        def _():
            hg[0:HALO, :] = jnp.zeros((HALO, cb), F32)
            hv[0:HALO, :] = jnp.zeros((HALO, cb), F32)

        hg[pl.ds(HALO, rc), :] = ug_ref[0:rc, :]
        hv[pl.ds(HALO, rc), :] = uv_ref[0:rc, :]
        for r in range(tm // rc):
            if r == 0:
                yg = _conv_rows(hg, HALO, rc, cg_ref, FFN_CONV)
                yv = _conv_rows(hv, HALO, rc, cv_ref, FFN_CONV)
            else:
                yg = _conv_rows(ug_ref, r * rc, rc, cg_ref, FFN_CONV)
                yv = _conv_rows(uv_ref, r * rc, rc, cv_ref, FFN_CONV)
            a_ref[r * rc:(r + 1) * rc, :] = (yg * _sigmoid(yg) * yv).astype(BF16)
        hg[0:HALO, :] = ug_ref[tm - HALO:tm, :]
        hv[0:HALO, :] = uv_ref[tm - HALO:tm, :]

    ncb = f // cb
    half = lambda s: pl.BlockSpec((None, tm, cb), lambda c, i: (s, i, c))
    taps = lambda s: pl.BlockSpec((FFN_CONV, cb), lambda c, i: (0, c + s * ncb))
    return pl.pallas_call(
        body,
        name=name,
        grid=(ncb, t // tm),
        in_specs=[half(0), half(1), taps(0), taps(1)],
        out_specs=pl.BlockSpec((tm, cb), lambda c, i: (i, c)),
        out_shape=jax.ShapeDtypeStruct((t, f), BF16),
        scratch_shapes=[pltpu.VMEM((HALO + rc, cb), F32)] * 2,
        compiler_params=_params(("parallel", "arbitrary")),
    )(up, up, cw, cw)


def _ffn_act_bwd(up, cw, dact, *, name, tm=512, cb=256):
    _, t, f = up.shape
    tm, cb = _tile(t, tm), _tile(f, cb)
    nt = t // tm
    hb = tm // HALO

    rc = min(ROW_CHUNK, tm)
    nr = tm // rc
    kw = FFN_CONV

    def body(ug_ref, uv_ref, pg_ref, pv_ref, cg_ref, cv_ref, da_ref, du_ref, dcg_ref, dcv_ref,
             hg, hv, dyg, dyv):
        i = pl.program_id(1)
        first = i == 0
        tile = nt - 1 - i

        @pl.when(tile == 0)
        def _():
            hg[0:HALO, :] = jnp.zeros((HALO, cb), F32)
            hv[0:HALO, :] = jnp.zeros((HALO, cb), F32)

        @pl.when(tile > 0)
        def _():
            hg[0:HALO, :] = pg_ref[...]
            hv[0:HALO, :] = pv_ref[...]

        @pl.when(first)
        def _():
            dyg[pl.ds(tm, HALO), :] = jnp.zeros((HALO, cb), F32)
            dyv[pl.ds(tm, HALO), :] = jnp.zeros((HALO, cb), F32)

        hg[pl.ds(HALO, rc), :] = ug_ref[0:rc, :]
        hv[pl.ds(HALO, rc), :] = uv_ref[0:rc, :]
        dcg = [jnp.zeros((1, cb), F32) for _ in range(kw)]
        dcv = [jnp.zeros((1, cb), F32) for _ in range(kw)]
        for r in reversed(range(nr)):
            rows = slice(r * rc, (r + 1) * rc)
            src_g, src_v, base = (hg, hv, HALO) if r == 0 else (ug_ref, uv_ref, r * rc)
            yg = _conv_rows(src_g, base, rc, cg_ref, kw)
            yv = _conv_rows(src_v, base, rc, cv_ref, kw)
            sg = _sigmoid(yg)
            da = da_ref[rows, :]
            dy_g = da * yv * (sg * (1.0 + yg * (1.0 - sg)))
            dy_v = da * (yg * sg)
            dyg[rows, :] = dy_g
            dyv[rows, :] = dy_v
            du_ref[0, rows, :] = _conv_t_rows(dyg, r * rc, rc, cg_ref, kw).astype(BF16)
            du_ref[1, rows, :] = _conv_t_rows(dyv, r * rc, rc, cv_ref, kw).astype(BF16)
            for j in range(kw):
                dcg[j] = dcg[j] + jnp.sum(dy_g * src_g[pl.ds(base - (kw - 1) + j, rc), :], axis=0, keepdims=True)
                dcv[j] = dcv[j] + jnp.sum(dy_v * src_v[pl.ds(base - (kw - 1) + j, rc), :], axis=0, keepdims=True)
        dyg[pl.ds(tm, HALO), :] = dyg[0:HALO, :]
        dyv[pl.ds(tm, HALO), :] = dyv[0:HALO, :]
        for j in range(kw):
            @pl.when(first)
            def _():
                dcg_ref[j:j + 1, :] = dcg[j]
                dcv_ref[j:j + 1, :] = dcv[j]

            @pl.when(jnp.logical_not(first))
            def _():
                dcg_ref[j:j + 1, :] += dcg[j]
                dcv_ref[j:j + 1, :] += dcv[j]

    ncb = f // cb
    half = lambda s: pl.BlockSpec((None, tm, cb), lambda c, i: (s, nt - 1 - i, c))
    prev = lambda s: pl.BlockSpec((None, HALO, cb), lambda c, i: (s, jnp.maximum((nt - 1 - i) * hb - 1, 0), c))
    taps = lambda s: pl.BlockSpec((FFN_CONV, cb), lambda c, i: (0, c + s * ncb))
    dtaps = pl.BlockSpec((FFN_CONV, cb), lambda c, i: (0, c))
    return pl.pallas_call(
        body,
        name=name,
        grid=(ncb, nt),
        in_specs=[half(0), half(1), prev(0), prev(1), taps(0), taps(1), pl.BlockSpec((tm, cb), lambda c, i: (nt - 1 - i, c))],
        out_specs=[pl.BlockSpec((2, tm, cb), lambda c, i: (0, nt - 1 - i, c)), dtaps, dtaps],
        out_shape=[jax.ShapeDtypeStruct((2, t, f), BF16), jax.ShapeDtypeStruct((FFN_CONV, f), F32),
                   jax.ShapeDtypeStruct((FFN_CONV, f), F32)],
        scratch_shapes=[pltpu.VMEM((HALO + rc, cb), F32)] * 2 + [pltpu.VMEM((tm + HALO, cb), F32)] * 2,
        compiler_params=_params(("parallel", "arbitrary")),
    )(up, up, up, up, cw, cw, dact)


FFN_COL_TILE = 1408
FFN_SUB = 256
FFN_ROW_CHUNK = 32


def _sub_blocks(width):
    return [slice(c, min(c + FFN_SUB, width)) for c in range(0, width, FFN_SUB)]


def _ffn_up_act(n_f, w_up_t, cw, *, name, tm=512):
    t, d = n_f.shape
    f = w_up_t.shape[1]
    tm, tn = _tile(t, tm), _tile(f, FFN_COL_TILE)
    nj = f // tn
    rc = min(FFN_ROW_CHUNK, tm)
    kw = FFN_CONV

    def body(n_ref, wg_ref, wv_ref, cg_ref, cv_ref, up_ref, a_ref, hg, hv):
        i = pl.program_id(1)

        @pl.when(i == 0)
        def _():
            hg[0:HALO, :] = jnp.zeros((HALO, tn), F32)
            hv[0:HALO, :] = jnp.zeros((HALO, tn), F32)

        def products(cs):
            up_ref[0, :, cs] = lax.dot_general(n_ref[...], wg_ref[cs, :], _NT, preferred_element_type=F32)
            up_ref[1, :, cs] = lax.dot_general(n_ref[...], wv_ref[cs, :], _NT, preferred_element_type=F32)

        subs = _sub_blocks(tn)
        ug, uv = up_ref.at[0], up_ref.at[1]
        products(subs[0])
        for ci, cs in enumerate(subs):
            if ci + 1 < len(subs):
                products(subs[ci + 1])
            hg[pl.ds(HALO, rc), cs] = ug[0:rc, cs]
            hv[pl.ds(HALO, rc), cs] = uv[0:rc, cs]
            for r in range(tm // rc):
                src_g, src_v, base = (hg, hv, HALO) if r == 0 else (ug, uv, r * rc)
                yg = _conv_rows(src_g, base, rc, cg_ref, kw, cs)
                yv = _conv_rows(src_v, base, rc, cv_ref, kw, cs)
                a_ref[r * rc:(r + 1) * rc, cs] = (yg * _sigmoid(yg) * yv).astype(BF16)
            hg[0:HALO, cs] = ug[tm - HALO:tm, cs]
            hv[0:HALO, cs] = uv[tm - HALO:tm, cs]

    half = lambda s: pl.BlockSpec((None, tn, d), lambda j, i: (s, j, 0))
    taps = lambda s: pl.BlockSpec((kw, tn), lambda j, i: (0, j + s * nj))
    return pl.pallas_call(
        body,
        name=name,
        grid=(nj, t // tm),
        in_specs=[pl.BlockSpec((tm, d), lambda j, i: (i, 0)), half(0), half(1), taps(0), taps(1)],
        out_specs=[pl.BlockSpec((2, tm, tn), lambda j, i: (0, i, j)), pl.BlockSpec((tm, tn), lambda j, i: (i, j))],
        out_shape=[jax.ShapeDtypeStruct((2, t, f), F32), jax.ShapeDtypeStruct((t, f), BF16)],
        scratch_shapes=[pltpu.VMEM((HALO + rc, tn), F32)] * 2,
        compiler_params=_params(("parallel", "arbitrary")),
    )(n_f, w_up_t, w_up_t, cw, cw)


def _ffn_down_dx_act_bwd(dh, w_down, up, cw, *, name, tm=512):
    t, d = dh.shape
    f = w_down.shape[0]
    tm, tn = _tile(t, tm), _tile(f, FFN_COL_TILE)
    nj, nt = f // tn, t // tm
    hb = tm // HALO
    rc = min(FFN_ROW_CHUNK, tm)
    nr = tm // rc
    kw = FFN_CONV

    def body(dh_ref, wd_ref, ug_ref, uv_ref, pg_ref, pv_ref, cg_ref, cv_ref, du_ref, dcg_ref, dcv_ref,
             hg, hv, dyg, dyv, da_s, dh_s):
        i = pl.program_id(1)
        first = i == 0
        tile = nt - 1 - i

        @pl.when(tile == 0)
        def _():
            hg[0:HALO, :] = jnp.zeros((HALO, tn), F32)
            hv[0:HALO, :] = jnp.zeros((HALO, tn), F32)

        @pl.when(tile > 0)
        def _():
            hg[0:HALO, :] = pg_ref[...]
            hv[0:HALO, :] = pv_ref[...]

        @pl.when(first)
        def _():
            dyg[pl.ds(tm, HALO), :] = jnp.zeros((HALO, tn), F32)
            dyv[pl.ds(tm, HALO), :] = jnp.zeros((HALO, tn), F32)

        dh_s[...] = dh_ref[...].astype(BF16)

        def fold(x):
            out = x[0:HALO, :]
            for g in range(1, rc // HALO):
                out = out + x[g * HALO:(g + 1) * HALO, :]
            return out

        def product(cs):
            da_s[:, cs] = lax.dot_general(dh_s[...], wd_ref[cs, :], _NT, preferred_element_type=F32)

        subs = _sub_blocks(tn)
        product(subs[0])
        for ci, cs in enumerate(subs):
            width = cs.stop - cs.start
            if ci + 1 < len(subs):
                product(subs[ci + 1])
            hg[pl.ds(HALO, rc), cs] = ug_ref[0:rc, cs]
            hv[pl.ds(HALO, rc), cs] = uv_ref[0:rc, cs]
            dcg = [jnp.zeros((HALO, width), F32) for _ in range(kw)]
            dcv = [jnp.zeros((HALO, width), F32) for _ in range(kw)]
            for r in reversed(range(nr)):
                rows = slice(r * rc, (r + 1) * rc)
                src_g, src_v, base = (hg, hv, HALO) if r == 0 else (ug_ref, uv_ref, r * rc)
                win_g = _shifted_rows(src_g, base, rc, [kw - 1 - j for j in range(kw)], cs)
                win_v = _shifted_rows(src_v, base, rc, [kw - 1 - j for j in range(kw)], cs)
                yg = sum(cg_ref[j:j + 1, cs] * win_g[j] for j in range(kw))
                yv = sum(cv_ref[j:j + 1, cs] * win_v[j] for j in range(kw))
                sg = _sigmoid(yg)
                da = da_s[rows, cs]
                dy_g = da * yv * (sg * (1.0 + yg * (1.0 - sg)))
                dy_v = da * (yg * sg)
                dyg[rows, cs] = dy_g
                dyv[rows, cs] = dy_v
                du_ref[0, rows, cs] = _conv_t_rows(dyg, r * rc, rc, cg_ref, kw, cs).astype(BF16)
                du_ref[1, rows, cs] = _conv_t_rows(dyv, r * rc, rc, cv_ref, kw, cs).astype(BF16)
                for j in range(kw):
                    dcg[j] = dcg[j] + fold(dy_g * win_g[j])
                    dcv[j] = dcv[j] + fold(dy_v * win_v[j])
            dyg[pl.ds(tm, HALO), cs] = dyg[0:HALO, cs]
            dyv[pl.ds(tm, HALO), cs] = dyv[0:HALO, cs]
            for j in range(kw):
                tg = jnp.sum(dcg[j], axis=0, keepdims=True)
                tv = jnp.sum(dcv[j], axis=0, keepdims=True)

                @pl.when(first)
                def _():
                    dcg_ref[j:j + 1, cs] = tg
                    dcv_ref[j:j + 1, cs] = tv

                @pl.when(jnp.logical_not(first))
                def _():
                    dcg_ref[j:j + 1, cs] += tg
                    dcv_ref[j:j + 1, cs] += tv

    half = lambda s: pl.BlockSpec((None, tm, tn), lambda j, i: (s, nt - 1 - i, j))
    prev = lambda s: pl.BlockSpec((None, HALO, tn), lambda j, i: (s, jnp.maximum((nt - 1 - i) * hb - 1, 0), j))
    taps = lambda s: pl.BlockSpec((kw, tn), lambda j, i: (0, j + s * nj))
    dtaps = pl.BlockSpec((kw, tn), lambda j, i: (0, j))
    return pl.pallas_call(
        body,
        name=name,
        grid=(nj, nt),
        in_specs=[pl.BlockSpec((tm, d), lambda j, i: (nt - 1 - i, 0)), pl.BlockSpec((tn, d), lambda j, i: (j, 0)),
                  half(0), half(1), prev(0), prev(1), taps(0), taps(1)],
        out_specs=[pl.BlockSpec((2, tm, tn), lambda j, i: (0, nt - 1 - i, j)), dtaps, dtaps],
        out_shape=[jax.ShapeDtypeStruct((2, t, f), BF16), jax.ShapeDtypeStruct((kw, f), F32),
                   jax.ShapeDtypeStruct((kw, f), F32)],
        scratch_shapes=[pltpu.VMEM((HALO + rc, tn), F32)] * 2 + [pltpu.VMEM((tm + HALO, tn), F32)] * 2
        + [pltpu.VMEM((tm, tn), F32), pltpu.VMEM((tm, d), BF16)],
        compiler_params=_params(("parallel", "arbitrary")),
    )(dh, w_down, up, up, up, up, cw, cw)


def _ple_fwd(h, zg, pe, *, name, tm=512):
    t, d = h.shape
    tm = _tile(t, tm)

    def body(h_ref, z_ref, p_ref, o_ref):
        o_ref[...] = h_ref[...] + _sigmoid(z_ref[...]) * p_ref[...]

    row = pl.BlockSpec((tm, d), lambda i: (i, 0))
    return pl.pallas_call(
        body, name=name, grid=(t // tm,), in_specs=[row] * 3, out_specs=row,
        out_shape=jax.ShapeDtypeStruct((t, d), F32), compiler_params=_params(("parallel",)),
    )(h, zg, pe)


def _ple_bwd(dh, zg, pe, *, name, tm=512):
    t, d = dh.shape
    tm = _tile(t, tm)

    def body(g_ref, z_ref, p_ref, dz_ref, dp_ref):
        g = g_ref[...]
        sg = _sigmoid(z_ref[...])
        dz_ref[...] = (g * p_ref[...] * sg * (1.0 - sg)).astype(BF16)
        dp_ref[...] = (g * sg).astype(BF16)

    row = pl.BlockSpec((tm, d), lambda i: (i, 0))
    return pl.pallas_call(
        body, name=name, grid=(t // tm,), in_specs=[row] * 3, out_specs=[row] * 2,
        out_shape=[jax.ShapeDtypeStruct((t, d), BF16)] * 2, compiler_params=_params(("parallel",)),
    )(dh, zg, pe)


def _my_pos():
    return lax.axis_index("x"), lax.axis_index("y"), lax.axis_index("c")


def _all_gather(block, *, name, dep=None):
    r, w = block.shape
    has_dep = dep is not None

    def body(*refs):
        x_ref, out_ref, send_sems, recv_sems, local_sem = refs[:1] + refs[1 + has_dep:]
        x, y, c = _my_pos()
        me, sibling = (x, y, c), (x, y, 1 - c)
        chips = [(1 - x, y), (x, 1 - y), (1 - x, 1 - y)]

        def slot(px, py, pc):
            return out_ref.at[4 * px + 2 * py + pc]

        def copy(k, blk, to, src=None):
            return pltpu.make_async_remote_copy(
                src_ref=slot(*blk) if src is None else src, dst_ref=slot(*blk),
                send_sem=send_sems.at[k], recv_sem=recv_sems.at[k],
                device_id=to, device_id_type=pl.DeviceIdType.MESH)

        mine = pltpu.make_async_copy(x_ref, slot(*me), local_sem)
        mine.start()
        first = [copy(0, me, sibling, src=x_ref)]
        first += [copy(1 + j, me, (*chip, c), src=x_ref) for j, chip in enumerate(chips)]
        for cp in first:
            cp.start()
        passed = [copy(4 + j, (*chip, c), sibling) for j, chip in enumerate(chips)]
        for j, chip in enumerate(chips):
            copy(1 + j, (*chip, c), me).wait_recv()
            passed[j].start()
        copy(0, sibling, me).wait_recv()
        for j, chip in enumerate(chips):
            copy(4 + j, (*chip, 1 - c), me).wait_recv()
        for cp in first + passed:
            cp.wait_send()
        mine.wait()

    return pl.pallas_call(
        body,
        name=name,
        out_shape=jax.ShapeDtypeStruct((N_DEV, r, w), block.dtype),
        in_specs=[pl.BlockSpec(memory_space=pl.ANY)] * (1 + has_dep),
        out_specs=pl.BlockSpec(memory_space=pl.ANY),
        scratch_shapes=[pltpu.SemaphoreType.DMA((7,)), pltpu.SemaphoreType.DMA((7,)), pltpu.SemaphoreType.DMA],
    )(*((block, dep) if has_dep else (block,)))


def _all_to_all(slabs, *, name):
    n, r, w = slabs.shape

    def body(x_ref, out_ref, send_sems, recv_sems, local_sem):
        x, y, c = _my_pos()
        my_idx = 4 * x + 2 * y + c
        mine = pltpu.make_async_copy(x_ref.at[my_idx], out_ref.at[my_idx], local_sem)
        mine.start()
        copies = []
        for k in range(1, N_DEV):
            fx, fy, fc = (k >> 2) & 1, (k >> 1) & 1, k & 1
            px = (1 - x) if fx else x
            py = (1 - y) if fy else y
            pc = (1 - c) if fc else c
            cp = pltpu.make_async_remote_copy(
                src_ref=x_ref.at[4 * px + 2 * py + pc], dst_ref=out_ref.at[my_idx],
                send_sem=send_sems.at[k - 1], recv_sem=recv_sems.at[k - 1],
                device_id=(px, py, pc), device_id_type=pl.DeviceIdType.MESH)
            cp.start()
            copies.append(cp)
        for cp in copies:
            cp.wait_recv()
        for cp in copies:
            cp.wait_send()
        mine.wait()

    return pl.pallas_call(
        body,
        name=name,
        out_shape=jax.ShapeDtypeStruct((n, r, w), slabs.dtype),
        in_specs=[pl.BlockSpec(memory_space=pl.ANY)],
        out_specs=pl.BlockSpec(memory_space=pl.ANY),
        scratch_shapes=[pltpu.SemaphoreType.DMA((7,)), pltpu.SemaphoreType.DMA((7,)), pltpu.SemaphoreType.DMA],
    )(slabs)


def _exchange_copies(scatter, src_refs, land_refs, send_sems, recv_sems, local_sems):
    x, y, c = _my_pos()
    me = 4 * x + 2 * y + c
    local, remote = [], []
    for i, (s, l) in enumerate(zip(src_refs, land_refs)):
        local.append(pltpu.make_async_copy(s.at[me] if scatter else s, l.at[me], local_sems.at[i]))
        for k in range(1, N_DEV):
            px = (1 - x) if (k >> 2) & 1 else x
            py = (1 - y) if (k >> 1) & 1 else y
            pc = (1 - c) if k & 1 else c
            remote.append(pltpu.make_async_remote_copy(
                src_ref=s.at[4 * px + 2 * py + pc] if scatter else s, dst_ref=l.at[me],
                send_sem=send_sems.at[(N_DEV - 1) * i + k - 1], recv_sem=recv_sems.at[(N_DEV - 1) * i + k - 1],
                device_id=(px, py, pc), device_id_type=pl.DeviceIdType.MESH))
    return local, remote


def _exchange(arrays, *, scatter, name):
    n = len(arrays)

    def body(*refs):
        srcs, lands = refs[:n], refs[n:2 * n]
        local, remote = _exchange_copies(scatter, srcs, lands, *refs[2 * n:])
        for cp in local + remote:
            cp.start()
        for cp in remote:
            cp.wait_recv()
        for cp in remote:
            cp.wait_send()
        for cp in local:
            cp.wait()

    hbm = pl.BlockSpec(memory_space=pl.ANY)
    out = pl.pallas_call(
        body,
        name=name,
        out_shape=[jax.ShapeDtypeStruct(a.shape if scatter else (N_DEV,) + a.shape, a.dtype) for a in arrays],
        in_specs=[hbm] * n,
        out_specs=[hbm] * n,
        scratch_shapes=[pltpu.SemaphoreType.DMA(((N_DEV - 1) * n,)), pltpu.SemaphoreType.DMA(((N_DEV - 1) * n,)),
                        pltpu.SemaphoreType.DMA((n,))],
    )(*arrays)
    return list(out)


_HBM_SPEC = pl.BlockSpec(memory_space=pltpu.HBM)
_SEM_SPEC = pl.BlockSpec(memory_space=pltpu.SEMAPHORE)
_EFFECT = pltpu.SideEffectType.DATAFLOW_SIDE_EFFECTING


def _exchange_start(arrays, *, scatter, name, dep):
    n = len(arrays)
    srcs = [pltpu.with_memory_space_constraint(a, pltpu.HBM) for a in arrays]
    lands = [pltpu.with_memory_space_constraint(lax.empty(a.shape if scatter else (N_DEV,) + a.shape, a.dtype), pltpu.HBM)
             for a in arrays]

    def body(*refs):
        src_refs, land_refs = refs[:n], refs[n:2 * n]
        send_sems, recv_sems, local_sems = refs[2 * n + 1:2 * n + 4]
        token = refs[-1]
        local, remote = _exchange_copies(scatter, src_refs, land_refs, send_sems, recv_sems, local_sems)
        for cp in local + remote:
            cp.start()
        token[...] = jnp.zeros_like(token)

    sems = (pltpu.SemaphoreType.DMA(((N_DEV - 1) * n,)), pltpu.SemaphoreType.DMA(((N_DEV - 1) * n,)),
            pltpu.SemaphoreType.DMA((n,)))
    out = pl.pallas_call(
        body,
        name=name,
        out_shape=sems + tuple(pltpu.HBM(a.shape, a.dtype) for a in srcs + lands) + (jax.ShapeDtypeStruct((8, LANES), F32),),
        in_specs=[_HBM_SPEC] * (2 * n) + [pl.BlockSpec(memory_space=pl.ANY)],
        out_specs=(_SEM_SPEC,) * 3 + (_HBM_SPEC,) * (2 * n) + (pl.BlockSpec(memory_space=pltpu.VMEM),),
        input_output_aliases={i: 3 + i for i in range(2 * n)},
        compiler_params=pltpu.CompilerParams(has_side_effects=_EFFECT),
    )(*srcs, *lands, dep)
    return (out[:3], list(out[3:3 + n]), list(out[3 + n:3 + 2 * n])), out[-1]


def _exchange_wait(handle, after, *, scatter, name):
    sems, srcs, lands = handle
    n = len(srcs)

    def body(*refs):
        src_refs, land_refs = refs[:n], refs[n:2 * n]
        send_sems, recv_sems, local_sems = refs[2 * n:2 * n + 3]
        local, remote = _exchange_copies(scatter, src_refs, land_refs, send_sems, recv_sems, local_sems)
        for cp in remote:
            cp.wait_send()
            cp.wait_recv()
        for cp in local:
            cp.wait()

    out = pl.pallas_call(
        body,
        name=name,
        out_shape=tuple(pltpu.HBM(a.shape, a.dtype) for a in srcs + lands),
        in_specs=[_HBM_SPEC] * (2 * n) + [_SEM_SPEC] * 3 + [pl.BlockSpec(memory_space=pl.ANY)],
        out_specs=(_HBM_SPEC,) * (2 * n),
        input_output_aliases={i: i for i in range(2 * n)},
        compiler_params=pltpu.CompilerParams(has_side_effects=_EFFECT),
    )(*srcs, *lands, *sems, after)
    return list(out[n:])


def _sum_parts(parts, *, name, tr=512):
    n, r, lanes = parts.shape
    tr = tr if (r % tr == 0 and r > 1024) else r

    def body(p_ref, g_ref):
        g = p_ref[0].astype(F32)
        for j in range(1, n):
            g = g + p_ref[j].astype(F32)
        g_ref[...] = g

    row = pl.BlockSpec((tr, lanes), lambda i: (i, 0))
    return pl.pallas_call(
        body,
        name=name,
        grid=(r // tr,),
        in_specs=[pl.BlockSpec((n, tr, lanes), lambda i: (0, i, 0))],
        out_specs=row,
        out_shape=jax.ShapeDtypeStruct((r, lanes), F32),
        compiler_params=_params(("parallel",)),
    )(parts)


def _adamw_update(g, w, m, v):
    c1 = 1.0 / (1.0 - ADAM_B1 ** ADAM_STEP)
    c2 = 1.0 / (1.0 - ADAM_B2 ** ADAM_STEP)
    nm = ADAM_B1 * m + (1.0 - ADAM_B1) * g
    nv = ADAM_B2 * v + (1.0 - ADAM_B2) * (g * g)
    return -ADAM_LR * ((nm * c1) / (jnp.sqrt(nv * c2) + ADAM_EPS) + ADAM_WD * w), nm, nv


def _adamw_layer(g, w, m, v, layer, prev, *, name):
    nl, k, n = w.shape
    tr = max(d for d in range(8, min(k, 256) + 1, 8) if k % d == 0)
    in_parts = g.ndim == 3

    def body(g_ref, w_ref, m_ref, v_ref, *rest):
        go_ref, d_ref, nm_ref, nv_ref = rest[-4:]
        if in_parts:
            gg = g_ref[0].astype(F32)
            for j in range(1, g_ref.shape[0]):
                gg = gg + g_ref[j].astype(F32)
        else:
            gg = g_ref[...]
        d, nm, nv = _adamw_update(gg, w_ref[...], m_ref[...], v_ref[...])
        go_ref[...] = gg
        d_ref[...] = d
        nm_ref[...] = nm
        nv_ref[...] = nv

    lay = pl.BlockSpec((None, tr, n), lambda i: (layer, i, 0))
    n_prev = 0 if prev is None else 4
    out = jax.ShapeDtypeStruct((nl, k, n), F32)
    return pl.pallas_call(
        body,
        name=name,
        grid=(k // tr,),
        in_specs=[pl.BlockSpec((g.shape[0], tr, n), lambda i: (0, i, 0)) if in_parts else pl.BlockSpec((tr, n), lambda i: (i, 0)),
                  lay, lay, lay] + [pl.BlockSpec(memory_space=pl.ANY)] * n_prev,
        out_specs=[lay] * 4,
        out_shape=[out] * 4,
        input_output_aliases={4 + j: j for j in range(n_prev)},
        compiler_params=_params(("parallel",)),
    )(g, w, m, v, *(prev or ()))


def _adamw_packed(g, w, m, v, *, name, tr=512):
    r, lanes = g.shape
    tr = tr if r % tr == 0 else r
    c1 = 1.0 / (1.0 - ADAM_B1 ** ADAM_STEP)
    c2 = 1.0 / (1.0 - ADAM_B2 ** ADAM_STEP)

    def body(g_ref, w_ref, m_ref, v_ref, d_ref, nm_ref, nv_ref):
        g = g_ref[...]
        nm = ADAM_B1 * m_ref[...] + (1.0 - ADAM_B1) * g
        nv = ADAM_B2 * v_ref[...] + (1.0 - ADAM_B2) * (g * g)
        nm_ref[...] = nm
        nv_ref[...] = nv
        d_ref[...] = -ADAM_LR * ((nm * c1) / (jnp.sqrt(nv * c2) + ADAM_EPS) + ADAM_WD * w_ref[...])

    row = pl.BlockSpec((tr, lanes), lambda i: (i, 0))
    out = jax.ShapeDtypeStruct((r, lanes), F32)
    return pl.pallas_call(
        body,
        name=name,
        grid=(r // tr,),
        in_specs=[row] * 4,
        out_specs=[row] * 3,
        out_shape=[out] * 3,
        compiler_params=_params(("parallel",)),
    )(g, w, m, v)


BIG = ("a_w_in", "a_w_out", "b_w_in", "b_w_out", "f_w_up", "f_w_down", "ple_w_proj", "ple_w_gate")
CONVS = ("a_conv", "f_conv")
SMALL = ("norm_mix", "norm_ffn", "norm_ple", "norm_final", "a_log", "a_dt_bias", "a_norm", "b_sinks")
WEIGHTS = ("norm_mix", "norm_ffn", "norm_ple", "norm_final", "a_w_in", "a_conv", "a_log", "a_dt_bias", "a_norm",
           "a_w_out", "b_w_in", "b_sinks", "b_w_out", "f_w_up", "f_conv", "f_w_down", "ple_w_proj", "ple_w_gate")
SLAB_ROW_MULTIPLE = 512


def _pack(arrs, dtype, row_multiple):
    flat = jnp.concatenate([a.reshape(-1).astype(dtype) for a in arrs])
    rows = -(-flat.shape[0] // LANES)
    rows = -(-rows // row_multiple) * row_multiple
    return jnp.pad(flat, (0, rows * LANES - flat.shape[0])).reshape(rows, LANES)


def _unpack(slab, shapes):
    lead = slab.shape[:-2]
    flat = slab.reshape(lead + (-1,))
    out, off = [], 0
    for s in shapes:
        size = math.prod(s)
        out.append(flat[..., off:off + size].reshape(lead + tuple(s)))
        off += size
    return out


def _cols_full(g):
    g = jnp.moveaxis(g, 0, -2)
    return g.reshape(g.shape[:-2] + (g.shape[-2] * g.shape[-1],))


def _rows_full(g):
    g = jnp.moveaxis(g, 0, -3)
    return g.reshape(g.shape[:-3] + (g.shape[-3] * g.shape[-2], g.shape[-1]))


def _cols_split(wfull):
    n = wfull.shape[-1] // N_DEV
    g = wfull.reshape(wfull.shape[:-1] + (N_DEV, n))
    return jnp.moveaxis(g, -2, 0)


def _rows_split(wfull):
    k = wfull.shape[-2] // N_DEV
    g = wfull.reshape(wfull.shape[:-2] + (N_DEV, k, wfull.shape[-1]))
    return jnp.moveaxis(g, -3, 0)


TRANSPOSED = ("a_w_in", "b_w_in", "f_w_up", "ple_w_proj")


def _wire(name, a):
    return jnp.swapaxes(a, -1, -2) if name in TRANSPOSED else a


def _wire_shape(name, shape):
    return shape[:-2] + (shape[-1], shape[-2]) if name in TRANSPOSED else tuple(shape)


def _full(name, g):
    return _cols_full(g) if name in CONVS else _rows_full(g)


def _split(name, wfull):
    return _cols_split(wfull) if name in CONVS else _rows_split(wfull)


def _pack_split(grads, names, dtype, row_multiple):
    flat = jnp.concatenate([_split(n, grads[n]).reshape(N_DEV, -1).astype(dtype) for n in names], axis=1)
    rows = -(-flat.shape[1] // LANES)
    rows = -(-rows // row_multiple) * row_multiple
    return jnp.pad(flat, ((0, 0), (0, rows * LANES - flat.shape[1]))).reshape(N_DEV, rows, LANES)


def _pad_cols(a, width):
    return jnp.pad(a, ((0, 0), (0, width - a.shape[1])))


def kernel(x, p, norm_mix, norm_ffn, norm_ple, norm_final, a_w_in, a_conv, a_log, a_dt_bias, a_norm, a_w_out, b_w_in, b_sinks, b_w_out, f_w_up, f_conv, f_w_down, ple_w_proj, ple_w_gate, loss_target, m_norm_mix, m_norm_ffn, m_norm_ple, m_norm_final, m_a_w_in, m_a_conv, m_a_log, m_a_dt_bias, m_a_norm, m_a_w_out, m_b_w_in, m_b_sinks, m_b_w_out, m_f_w_up, m_f_conv, m_f_w_down, m_ple_w_proj, m_ple_w_gate, v_norm_mix, v_norm_ffn, v_norm_ple, v_norm_final, v_a_w_in, v_a_conv, v_a_log, v_a_dt_bias, v_a_norm, v_a_w_out, v_b_w_in, v_b_sinks, v_b_w_out, v_f_w_up, v_f_conv, v_f_w_down, v_ple_w_proj, v_ple_w_gate):
    wts = dict(norm_mix=norm_mix, norm_ffn=norm_ffn, norm_ple=norm_ple, norm_final=norm_final, a_w_in=a_w_in,
               a_conv=a_conv, a_log=a_log, a_dt_bias=a_dt_bias, a_norm=a_norm, a_w_out=a_w_out, b_w_in=b_w_in,
               b_sinks=b_sinks, b_w_out=b_w_out, f_w_up=f_w_up, f_conv=f_conv, f_w_down=f_w_down,
               ple_w_proj=ple_w_proj, ple_w_gate=ple_w_gate)
    mom = dict(norm_mix=m_norm_mix, norm_ffn=m_norm_ffn, norm_ple=m_norm_ple, norm_final=m_norm_final,
               a_w_in=m_a_w_in, a_conv=m_a_conv, a_log=m_a_log, a_dt_bias=m_a_dt_bias, a_norm=m_a_norm,
               a_w_out=m_a_w_out, b_w_in=m_b_w_in, b_sinks=m_b_sinks, b_w_out=m_b_w_out, f_w_up=m_f_w_up,
               f_conv=m_f_conv, f_w_down=m_f_w_down, ple_w_proj=m_ple_w_proj, ple_w_gate=m_ple_w_gate)
    var = dict(norm_mix=v_norm_mix, norm_ffn=v_norm_ffn, norm_ple=v_norm_ple, norm_final=v_norm_final,
               a_w_in=v_a_w_in, a_conv=v_a_conv, a_log=v_a_log, a_dt_bias=v_a_dt_bias, a_norm=v_a_norm,
               a_w_out=v_a_w_out, b_w_in=v_b_w_in, b_sinks=v_b_sinks, b_w_out=v_b_w_out, f_w_up=v_f_w_up,
               f_conv=v_f_conv, f_w_down=v_f_w_down, ple_w_proj=v_ple_w_proj, ple_w_gate=v_ple_w_gate)
    hk = N_HEADS_A * HEAD_DIM_A
    xs = x[0]
    tgt = loss_target[0]
    p_bf = p.astype(BF16)

    def shard(name, layer):
        return _wire(name, wts[name][layer]).astype(BF16)

    def stacked_rows(g):
        return g.reshape(g.shape[0] * g.shape[1], g.shape[2])

    n_in = a_w_in.shape[-1]
    first = _all_gather(jnp.concatenate([shard("a_w_in", 0), shard("a_w_out", 0)]), name="gather_mixer0")
    wa_in_t = jnp.pad(stacked_rows(first[:, :n_in]), ((0, PROJ_A - PROJ_A_REAL), (0, 0)))
    wa_out = stacked_rows(first[:, n_in:])
    gconv = _all_gather(_pack([wts[n] for n in CONVS], F32, 8), dep=first, name="gather_convs")
    conv_full = {n: _cols_full(g) for n, g in zip(CONVS, _unpack(gconv, [wts[n].shape for n in CONVS]))}
    cv_a, cv_f = conv_full["a_conv"][0], conv_full["f_conv"]
    layer_names = ("f_w_up", "f_w_down", "ple_w_proj", "ple_w_gate")
    gather0, tok = _exchange_start([shard(n, 0) for n in layer_names], scatter=False, name="gather_layer0_start", dep=gconv)
    gather1, tok = _exchange_start([shard(n, 0) for n in ("b_w_in", "b_w_out")] + [shard(n, 1) for n in layer_names],
                                   scatter=False, name="gather_layer1_start", dep=tok)

    alog_row = jnp.pad(a_log, ((0, 0), (N_HEADS_A, LANES - 2 * N_HEADS_A)))
    dtb_row = jnp.pad(a_dt_bias, ((0, 0), (N_HEADS_A, LANES - 2 * N_HEADS_A)))

    tile_f32, tile_bf16, rowsum = (F32, "tile"), (BF16, "tile"), (F32, "rowsum")

    def ffn_ple_fwd(i, h_a, n_f, next_norm, w_up_t, w_down, w_pp_t, w_pg):
        up, act = _ffn_up_act(n_f, w_up_t, cv_f[i], name=f"l{i}_ffn_up")
        h_b, n_p = _matmul_rows(act, w_down, _epi_res_norm, [h_a], [norm_ple[i]], [tile_f32, tile_bf16],
                                name=f"l{i}_ffn_down")
        pe = _matmul(p_bf[i, 0], w_pp_t, tb=True, name=f"l{i}_ple_proj")
        res = _matmul_rows(n_p, w_pg, _epi_ple, [h_b, pe], [] if next_norm is None else [next_norm],
                           [tile_f32, tile_f32] + ([] if next_norm is None else [tile_bf16]), name=f"l{i}_ple_gate")
        return res[1], (None if next_norm is None else res[2]), dict(n_f=n_f, up=up, act=act, h_b=h_b, n_p=n_p, zg=res[0], pe=pe)

    def layer_weights(lands):
        up_t, down, pp_t, pg = (stacked_rows(g) for g in lands)
        return up_t.reshape(2, D_FF, D_MODEL), down, pp_t, pg

    n0 = _rms_fwd(xs, norm_mix[0], name="l0_mix_norm")
    proj = _matmul(n0, wa_in_t, tb=True, tm=512, dep=tok, name="l0_in_proj")
    q, k, v, gbc, bbc = _delta_pre_fwd(proj, cv_a, alog_row, dtb_row, name="l0_delta_pre")
    o, states, tinv, og = _delta_fwd(q, k, v, gbc, bbc, proj, a_norm, name="l0_delta")
    h1, nf0 = _matmul_rows(og, wa_out, _epi_res_norm, [xs], [norm_ffn[0]], [tile_f32, tile_bf16], name="l0_mix_out")
    lw0 = layer_weights(_exchange_wait(gather0, h1, scatter=False, name="gather_layer0_wait"))
    h3, n1, sv0 = ffn_ple_fwd(0, h1, nf0, norm_mix[1], *lw0)

    lands1 = _exchange_wait(gather1, h3, scatter=False, name="gather_layer1_wait")
    wb_in_t, wb_out = stacked_rows(lands1[0]), stacked_rows(lands1[1])
    lw1 = layer_weights(lands1[2:])
    pb = _matmul(n1, wb_in_t, tb=True, name="l1_in_qkv")
    att = _swa_fwd(pb, b_sinks, name="l1_swa")
    h4, nf1 = _matmul_rows(att, wb_out, _epi_res_norm, [h3], [norm_ffn[1]], [tile_f32, tile_bf16], name="l1_mix_out")
    h6, _, sv1 = ffn_ple_fwd(1, h4, nf1, None, *lw1)

    loss_row, dh6, d_norm_final = _final_loss(h6, norm_final, tgt, name="final_loss")
    loss = lax.psum(loss_row[0, 0], MESH_AXES)

    def ffn_ple_bwd(i, dh_c, h_a, sv, lw, dep):
        w_up_t, w_down, _, w_pg = lw
        dzg, dpe = _ple_bwd(dh_c, sv["zg"], sv["pe"], name=f"l{i}_ple_mix_bwd")
        d_pg = _matmul(sv["n_p"], dzg, ta=True, out_dtype=BF16, dep=dep, name=f"l{i}_ple_gate_dw")
        d_pp_t = _matmul(dpe, p_bf[i, 0], ta=True, out_dtype=BF16, name=f"l{i}_ple_proj_dw")
        dh_b, d_np = _matmul_rows(dzg, w_pg, _epi_rms_bwd, [sv["h_b"], dh_c], [norm_ple[i]], [tile_f32, rowsum], tb=True,
                                  name=f"l{i}_ple_gate_dx")
        d_down = _matmul(sv["act"], dh_b, ta=True, out_dtype=BF16, name=f"l{i}_ffn_down_dw")
        dup, d_cg, d_cv = _ffn_down_dx_act_bwd(dh_b, w_down, sv["up"], cv_f[i], name=f"l{i}_ffn_down_dx")
        d_up_t = _matmul(dup, sv["n_f"], ta=True, out_dtype=BF16, name=f"l{i}_ffn_up_dw")
        dh_a, d_nf = _matmul_rows(dup, w_up_t, _epi_rms_bwd, [h_a, dh_b], [norm_ffn[i]], [tile_f32, rowsum],
                                  name=f"l{i}_ffn_up_dx")
        mats = [d_up_t.reshape(2 * D_FF, D_MODEL), d_down, d_pp_t, d_pg]
        return dh_a, mats, dict(norm_ple=d_np, norm_ffn=d_nf, f_conv=jnp.concatenate([d_cg, d_cv], axis=1))

    def slabs(g):
        return g.reshape(N_DEV, g.shape[0] // N_DEV, g.shape[1])

    dh4, mats1, g1 = ffn_ple_bwd(1, dh6, h4, sv1, lw1, None)
    datt = _matmul(dh4, wb_out, tb=True, out_dtype=BF16, name="l1_mix_out_dx")
    d_wb_out = _matmul(att, dh4, ta=True, out_dtype=BF16, name="l1_mix_out_dw")
    dq_b, dk_b, dv_b, dsinks = _swa_bwd(pb, b_sinks, datt, name="l1_swa_bwd")
    dpb = jnp.concatenate([dq_b, dk_b.astype(BF16), dv_b.astype(BF16)], axis=1)
    d_wb_in_t = _matmul(dpb, n1, ta=True, out_dtype=BF16, name="l1_in_qkv_dw")
    send1, tok = _exchange_start([slabs(g) for g in [d_wb_in_t, d_wb_out] + mats1], scatter=True,
                                 name="exchange_layer1_start", dep=d_wb_in_t)
    dh3, d_nm1 = _matmul_rows(dpb, wb_in_t, _epi_rms_bwd, [h3, dh4], [norm_mix[1]], [tile_f32, rowsum], name="l1_in_qkv_dx")

    dh1, mats0, g0 = ffn_ple_bwd(0, dh3, h1, sv0, lw0, tok)
    send0, tok = _exchange_start([slabs(g) for g in mats0], scatter=True, name="exchange_layer0_start", dep=mats0[0])
    dog = _matmul(dh1, wa_out, tb=True, dep=tok, name="l0_mix_out_dx")
    d_wa_out = _matmul(og, dh1, ta=True, out_dtype=BF16, name="l0_mix_out_dw")
    dq, dk, dv, dgbc, dbbc, dz0, d_anorm = _delta_bwd(q, k, v, gbc, bbc, tinv, states, o, proj, a_norm, dog,
                                                      name="l0_delta_bwd")
    dproj, d_aconv, d_alog, d_dtb = _delta_pre_bwd(proj, cv_a, alog_row, dtb_row, dq, dk, dv, dgbc, dbbc, dz0,
                                                   name="l0_delta_pre_bwd")
    d_wa_in_t = _matmul(dproj, n0, ta=True, out_dtype=BF16, name="l0_in_proj_dw")
    sendm, tok = _exchange_start([slabs(d_wa_in_t[:PROJ_A_REAL]), slabs(d_wa_out)], scatter=True,
                                 name="exchange_mixer0_start", dep=d_wa_in_t)
    dx, d_nm0 = _matmul_rows(dproj, wa_in_t, _epi_rms_bwd, [xs, dh1], [norm_mix[0]], [tile_f32, rowsum], dep=tok,
                             name="l0_in_proj_dx")

    recv1 = _exchange_wait(send1, dx, scatter=True, name="exchange_layer1_wait")
    recv0 = _exchange_wait(send0, recv1[0], scatter=True, name="exchange_layer0_wait")
    parts = {("b_w_in", 0): recv1[0], ("b_w_out", 0): recv1[1]}
    parts.update({(n, 1): r for n, r in zip(layer_names, recv1[2:])})
    parts.update({(n, 0): r for n, r in zip(layer_names, recv0)})

    outs = {}

    def update_matrix(name):
        res = None
        for layer in range(wts[name].shape[0]):
            g = parts[(name, layer)]
            if name in TRANSPOSED:
                g = _sum_parts(g, name=f"sum_{name}_{layer}").T
            res = _adamw_layer(g, wts[name], mom[name], var[name], layer, res, name=f"adamw_{name}_{layer}")
        for kind, arr in zip(("grad", "delta", "new_m", "new_v"), res):
            outs[(kind, name)] = arr
        return res

    last = [update_matrix(n) for n in ("b_w_in", "b_w_out") + layer_names][-1]
    recvm = _exchange_wait(sendm, last[0], scatter=True, name="exchange_mixer0_wait")
    parts.update({("a_w_in", 0): recvm[0], ("a_w_out", 0): recvm[1]})
    update_matrix("a_w_in")
    update_matrix("a_w_out")

    gconvs = dict(a_conv=d_aconv[None], f_conv=jnp.stack([g0["f_conv"], g1["f_conv"]]))
    small_g = dict(norm_mix=jnp.concatenate([d_nm0, d_nm1]), norm_ffn=jnp.concatenate([g0["norm_ffn"], g1["norm_ffn"]]),
                   norm_ple=jnp.concatenate([g0["norm_ple"], g1["norm_ple"]]), norm_final=d_norm_final[0],
                   a_log=d_alog[:, N_HEADS_A:2 * N_HEADS_A], a_dt_bias=d_dtb[:, N_HEADS_A:2 * N_HEADS_A],
                   a_norm=d_anorm, b_sinks=dsinks[:, :N_HEADS_B])
    recv_conv = _all_to_all(_pack_split(gconvs, CONVS, F32, 8), name="exchange_conv_grads")
    recv_small = _all_gather(_pack([small_g[n] for n in SMALL], F32, 8), name="gather_small_grads")
    for names, recv, tag in ((CONVS, recv_conv, "convs"), (SMALL, recv_small, "small")):
        shapes = [wts[n].shape for n in names]
        g_slab = _sum_parts(recv, name=f"sum_{tag}")
        packed = [_pack([d[n] for n in names], F32, 8) for d in (wts, mom, var)]
        res = _adamw_packed(g_slab, *packed, name=f"adamw_{tag}")
        for kind, slab in zip(("grad", "delta", "new_m", "new_v"), (g_slab,) + tuple(res)):
            for n, arr in zip(names, _unpack(slab, shapes)):
                outs[(kind, n)] = arr

    result = [loss, dx[None]]
    for kind in ("grad", "delta", "new_m", "new_v"):
        result += [outs[(kind, n)] for n in WEIGHTS]
    return tuple(result)
```

```python
import functools
import math

import jax
import jax.numpy as jnp
from jax import lax
from jax.experimental import pallas as pl
from jax.experimental.pallas import tpu as pltpu

F32 = jnp.float32
BF16 = jnp.bfloat16

D_MODEL = 1024
N_HEADS_A = 8
HEAD_DIM_A = 128
CONV_A = 4
CHUNK = 128
N_HEADS_B = 16
N_KV_B = 4
GROUP_B = N_HEADS_B // N_KV_B
HEAD_DIM_B = 64
WINDOW = 128
D_FF = 2816
FFN_CONV = 3
PLE_DIM = 256
EPS = 1e-6
N_DEV = 8
HALO = 8
PROJ_A_REAL = 4 * N_HEADS_A * HEAD_DIM_A + 2 * N_HEADS_A
PROJ_A = 4 * N_HEADS_A * HEAD_DIM_A + 128
Z_COL_BLOCK = 3
BA_COL_BLOCK = 32

ADAM_LR = 0.001
ADAM_B1 = 0.9
ADAM_B2 = 0.999
ADAM_EPS = 1e-08
ADAM_WD = 0.01
ADAM_STEP = 10

LANES = 128
VMEM_LIMIT_BYTES = 56 * 1024 * 1024
NEG_BIG = -1e30

MESH_AXES = ("x", "y", "c")


def _params(sem=None):
    return pltpu.CompilerParams(dimension_semantics=sem, vmem_limit_bytes=VMEM_LIMIT_BYTES)


def _tile(n, target):
    best = None
    for t in range(LANES, min(n, target) + 1, LANES):
        if n % t == 0:
            best = t
    return best or n


def _sigmoid(x):
    return 0.5 * jnp.tanh(0.5 * x) + 0.5


def _softplus(x):
    return jnp.maximum(x, 0.0) + jnp.log1p(jnp.exp(-jnp.abs(x)))


def _matmul(a, b, *, name, ta=False, tb=False, res=None, out_dtype=F32, tm=1408, tn=1408, tk=None, dep=None):
    sa, sb = a.ndim == 3, b.ndim == 3
    ns = a.shape[0] if sa else (b.shape[0] if sb else 1)
    contract_stack = sa and sb
    out_stacked = sa != sb
    m = a.shape[-1] if ta else a.shape[-2]
    k = a.shape[-2] if ta else a.shape[-1]
    n = b.shape[-2] if tb else b.shape[-1]
    assert (b.shape[-1] if tb else b.shape[-2]) == k, (a.shape, b.shape, ta, tb)
    if tk is None:
        tk = 1024 if ta else 2816
    tm, tn, tk = _tile(m, tm), _tile(n, tn), _tile(k, tk)
    nk = k // tk
    nsteps = nk * (ns if contract_stack else 1)
    dims = (((0 if ta else 1,), (1 if tb else 0,)), ((), ()))

    def spec(block, stacked, order):
        def index(g, i, j, kk):
            two = order(i, j, kk % nk)
            if not stacked:
                return two
            return (kk // nk if contract_stack else g,) + two
        return pl.BlockSpec(((None,) if stacked else ()) + block, index)

    a_spec = spec((tk, tm), sa, lambda i, j, kq: (kq, i)) if ta else spec((tm, tk), sa, lambda i, j, kq: (i, kq))
    b_spec = spec((tn, tk), sb, lambda i, j, kq: (j, kq)) if tb else spec((tk, tn), sb, lambda i, j, kq: (kq, j))
    o_spec = spec((tm, tn), out_stacked, lambda i, j, kq: (i, j))
    has_res = res is not None
    has_dep = dep is not None

    def body(*refs):
        a_ref, b_ref = refs[0], refs[1]
        r_ref = refs[2] if has_res else None
        o_ref = refs[2 + has_res + has_dep]
        part = lax.dot_general(a_ref[...].astype(BF16), b_ref[...].astype(BF16), dims, preferred_element_type=F32)

        def finish(acc):
            if has_res:
                acc = acc + r_ref[...].astype(F32)
            o_ref[...] = acc.astype(out_dtype)

        if nsteps == 1:
            finish(part)
        else:
            acc_ref = refs[-1]
            kk = pl.program_id(3)

            @pl.when(kk == 0)
            def _():
                acc_ref[...] = part

            @pl.when(kk > 0)
            def _():
                acc_ref[...] += part

            @pl.when(kk == nsteps - 1)
            def _():
                finish(acc_ref[...])

    in_specs = [a_spec, b_spec] + ([o_spec] if has_res else []) + ([pl.BlockSpec(memory_space=pl.ANY)] if has_dep else [])
    args = (a, b) + ((res,) if has_res else ()) + ((dep,) if has_dep else ())
    return pl.pallas_call(
        body,
        name=name,
        grid=(ns if out_stacked else 1, m // tm, n // tn, nsteps),
        in_specs=in_specs,
        out_specs=o_spec,
        out_shape=jax.ShapeDtypeStruct(((ns,) if out_stacked else ()) + (m, n), out_dtype),
        scratch_shapes=[pltpu.VMEM((tm, tn), F32)] if nsteps > 1 else [],
        compiler_params=_params(("parallel", "parallel", "parallel", "arbitrary")),
    )(*args)


EPI_ROWS = 32


def _matmul_rows(a, b, epilogue, tiles_in, rows_in, outs, *, name, tb=False, tm=512, tk=None, dep=None):
    stacked = a.ndim == 3
    ns = a.shape[0] if stacked else 1
    m, k = a.shape[-2], a.shape[-1]
    n = b.shape[-2] if tb else b.shape[-1]
    assert (b.shape[-1] if tb else b.shape[-2]) == k and (b.ndim == 3) == stacked, (a.shape, b.shape, tb)
    tm, tk = _tile(m, tm), _tile(k, 2816 if tk is None else tk)
    nk = k // tk
    nsteps = nk * ns
    dims = (((1,), (1 if tb else 0,)), ((), ()))
    lead = (None,) if stacked else ()
    front = (lambda kk: (kk // nk,)) if stacked else (lambda kk: ())
    a_spec = pl.BlockSpec(lead + (tm, tk), lambda i, kk: front(kk) + (i, kk % nk))
    if tb:
        b_spec = pl.BlockSpec(lead + (n, tk), lambda i, kk: front(kk) + (0, kk % nk))
    else:
        b_spec = pl.BlockSpec(lead + (tk, n), lambda i, kk: front(kk) + (kk % nk, 0))
    tile_spec = pl.BlockSpec((tm, n), lambda i, kk: (i, 0))
    row_spec = pl.BlockSpec((1, n), lambda i, kk: (0, 0))
    n_t, n_r, has_dep = len(tiles_in), len(rows_in), dep is not None

    def body(*refs):
        a_ref, b_ref = refs[:2]
        tile_refs = refs[2:2 + n_t]
        row_refs = refs[2 + n_t:2 + n_t + n_r]
        out_refs = refs[2 + n_t + n_r + has_dep:-1]
        acc_ref = refs[-1]
        part = lax.dot_general(a_ref[...].astype(BF16), b_ref[...].astype(BF16), dims, preferred_element_type=F32)
        kk = pl.program_id(1)
        if nsteps == 1:
            acc_ref[...] = part
        else:
            @pl.when(kk == 0)
            def _():
                acc_ref[...] = part

            @pl.when(kk > 0)
            def _():
                acc_ref[...] += part

        @pl.when(kk == nsteps - 1)
        def _():
            epilogue(acc_ref, tile_refs, row_refs, out_refs, pl.program_id(0) == 0)

    return pl.pallas_call(
        body,
        name=name,
        grid=(m // tm, nsteps),
        in_specs=[a_spec, b_spec] + [tile_spec] * n_t + [row_spec] * n_r + ([pl.BlockSpec(memory_space=pl.ANY)] if has_dep else []),
        out_specs=[tile_spec if kind == "tile" else row_spec for _, kind in outs],
        out_shape=[jax.ShapeDtypeStruct((m, n) if kind == "tile" else (1, n), dt) for dt, kind in outs],
        scratch_shapes=[pltpu.VMEM((tm, n), F32)],
        compiler_params=_params(("arbitrary", "arbitrary")),
    )(a, b, *tiles_in, *[r.reshape(1, n) for r in rows_in], *((dep,) if has_dep else ()))


def _row_chunks(ref):
    return [pl.ds(r, EPI_ROWS) for r in range(0, ref.shape[0], EPI_ROWS)]


def _epi_res_norm(acc, tiles, rows, outs, first):
    (res,), (w,), (h_out, n_out) = tiles, rows, outs
    for rs in _row_chunks(acc):
        h = acc[rs, :] + res[rs, :]
        h_out[rs, :] = h
        r = lax.rsqrt(jnp.mean(h * h, axis=-1, keepdims=True) + EPS)
        n_out[rs, :] = (h * r * w[...]).astype(BF16)


def _epi_ple(acc, tiles, rows, outs, first):
    hb, pe = tiles
    for rs in _row_chunks(acc):
        zg = acc[rs, :]
        outs[0][rs, :] = zg
        h = hb[rs, :] + _sigmoid(zg) * pe[rs, :]
        outs[1][rs, :] = h
        if rows:
            r = lax.rsqrt(jnp.mean(h * h, axis=-1, keepdims=True) + EPS)
            outs[2][rs, :] = (h * r * rows[0][...]).astype(BF16)


def _epi_rms_bwd(acc, tiles, rows, outs, first):
    (h_ref, skip), (w,), (dh_out, dw_out) = tiles, rows, outs
    dw = jnp.zeros((1, acc.shape[1]), F32)
    for rs in _row_chunks(acc):
        x = h_ref[rs, :]
        r = lax.rsqrt(jnp.mean(x * x, axis=-1, keepdims=True) + EPS)
        nh = x * r
        g = acc[rs, :]
        gw = g * w[...]
        dh_out[rs, :] = r * (gw - nh * jnp.mean(gw * nh, axis=-1, keepdims=True)) + skip[rs, :]
        dw = dw + jnp.sum(g * nh, axis=0, keepdims=True)

    @pl.when(first)
    def _():
        dw_out[...] = dw

    @pl.when(jnp.logical_not(first))
    def _():
        dw_out[...] += dw


def _rms_fwd(h, w, *, name, tm=512):
    t, d = h.shape
    tm = _tile(t, tm)

    def body(h_ref, w_ref, o_ref):
        x = h_ref[...]
        r = lax.rsqrt(jnp.mean(x * x, axis=-1, keepdims=True) + EPS)
        o_ref[...] = (x * r * w_ref[...]).astype(BF16)

    return pl.pallas_call(
        body,
        name=name,
        grid=(t // tm,),
        in_specs=[pl.BlockSpec((tm, d), lambda i: (i, 0)), pl.BlockSpec((1, d), lambda i: (0, 0))],
        out_specs=pl.BlockSpec((tm, d), lambda i: (i, 0)),
        out_shape=jax.ShapeDtypeStruct((t, d), BF16),
        compiler_params=_params(("parallel",)),
    )(h, w.reshape(1, d))


def _rms_bwd(h, w, dn, skip, *, name, tm=512):
    t, d = h.shape
    tm = _tile(t, tm)

    def body(h_ref, w_ref, dn_ref, skip_ref, dh_ref, dw_ref):
        i = pl.program_id(0)
        x = h_ref[...]
        r = lax.rsqrt(jnp.mean(x * x, axis=-1, keepdims=True) + EPS)
        nh = x * r
        g = dn_ref[...].astype(F32)
        gw = g * w_ref[...]
        dh_ref[...] = r * (gw - nh * jnp.mean(gw * nh, axis=-1, keepdims=True)) + skip_ref[...]
        part = jnp.sum(g * nh, axis=0, keepdims=True)

        @pl.when(i == 0)
        def _():
            dw_ref[...] = part

        @pl.when(i > 0)
        def _():
            dw_ref[...] += part

    row = pl.BlockSpec((tm, d), lambda i: (i, 0))
    vec = pl.BlockSpec((1, d), lambda i: (0, 0))
    return pl.pallas_call(
        body,
        name=name,
        grid=(t // tm,),
        in_specs=[row, vec, row, row],
        out_specs=[row, vec],
        out_shape=[jax.ShapeDtypeStruct((t, d), F32), jax.ShapeDtypeStruct((1, d), F32)],
        compiler_params=_params(("arbitrary",)),
    )(h, w.reshape(1, d), dn, skip)


def _final_loss(h, w, target, *, name, tm=512):
    t, d = h.shape
    tm = _tile(t, tm)

    def body(h_ref, w_ref, tg_ref, loss_ref, dh_ref, dw_ref):
        i = pl.program_id(0)
        x = h_ref[...]
        r = lax.rsqrt(jnp.mean(x * x, axis=-1, keepdims=True) + EPS)
        nh = x * r
        err = nh * w_ref[...] - tg_ref[...]
        lpart = (0.5 / d) * jnp.sum(jnp.sum(err * err, axis=-1, keepdims=True), axis=0, keepdims=True)
        g = err * (1.0 / d)
        gw = g * w_ref[...]
        dh_ref[...] = r * (gw - nh * jnp.mean(gw * nh, axis=-1, keepdims=True))
        part = jnp.sum(g * nh, axis=0, keepdims=True)
        lrow = jnp.broadcast_to(lpart, (1, LANES))

        @pl.when(i == 0)
        def _():
            dw_ref[...] = part
            loss_ref[...] = lrow

        @pl.when(i > 0)
        def _():
            dw_ref[...] += part
            loss_ref[...] += lrow

    row = pl.BlockSpec((tm, d), lambda i: (i, 0))
    vec = pl.BlockSpec((1, d), lambda i: (0, 0))
    return pl.pallas_call(
        body,
        name=name,
        grid=(t // tm,),
        in_specs=[row, vec, row],
        out_specs=[pl.BlockSpec((1, LANES), lambda i: (0, 0)), row, vec],
        out_shape=[jax.ShapeDtypeStruct((1, LANES), F32), jax.ShapeDtypeStruct((t, d), F32), jax.ShapeDtypeStruct((1, d), F32)],
        compiler_params=_params(("arbitrary",)),
    )(h, w.reshape(1, d), target)


def _conv_from_ext(ext_ref, cw_ref, kw, tm):
    y = cw_ref[kw - 1:kw, :] * ext_ref[pl.ds(HALO, tm), :]
    for i in range(kw - 1):
        y = y + cw_ref[i:i + 1, :] * ext_ref[pl.ds(HALO - (kw - 1) + i, tm), :]
    return y


ROW_CHUNK = 64


def _shifted_rows(src_ref, base, rows, shifts, cols=slice(None)):
    ext = src_ref[pl.ds(base - HALO, rows + HALO), cols]
    return [ext[HALO:, :] if s == 0 else pltpu.roll(ext, s, 0)[HALO:, :] for s in shifts]


def _conv_rows(src_ref, base, rows, cw_ref, kw, cols=slice(None)):
    wins = _shifted_rows(src_ref, base, rows, range(kw), cols)
    y = cw_ref[kw - 1:kw, cols] * wins[0]
    for s in range(1, kw):
        y = y + cw_ref[kw - 1 - s:kw - s, cols] * wins[s]
    return y


def _ahead_rows(src_ref, base, rows, shifts, cols=slice(None)):
    ext = src_ref[pl.ds(base, rows + HALO), cols]
    return [ext[:rows, :] if s == 0 else pltpu.roll(ext, rows + HALO - s, 0)[:rows, :] for s in shifts]


def _conv_t_rows(dy_ref, base, rows, cw_ref, kw, cols=slice(None)):
    wins = _ahead_rows(dy_ref, base, rows, range(kw), cols)
    dx = cw_ref[kw - 1:kw, cols] * wins[0]
    for s in range(1, kw):
        dx = dx + cw_ref[kw - 1 - s:kw - s, cols] * wins[s]
    return dx


def _fold_rows(x):
    out = x[0:HALO, :]
    for g in range(1, x.shape[0] // HALO):
        out = out + x[g * HALO:(g + 1) * HALO, :]
    return out


def _conv_bwd_from_ext(xext_ref, dyext_ref, cw_ref, dcw_ref, kw, tm, first):
    dy = dyext_ref[pl.ds(0, tm), :]
    dx = cw_ref[kw - 1:kw, :] * dy
    for i in range(kw - 1):
        dx = dx + cw_ref[i:i + 1, :] * dyext_ref[pl.ds(kw - 1 - i, tm), :]
    for i in range(kw):
        part = jnp.sum(dy * xext_ref[pl.ds(HALO - (kw - 1) + i, tm), :], axis=0, keepdims=True)

        @pl.when(first)
        def _():
            dcw_ref[i:i + 1, :] = part

        @pl.when(jnp.logical_not(first))
        def _():
            dcw_ref[i:i + 1, :] += part

    return dx


def _delta_pre_fwd(proj, conv_w, alog_row, dtb_row, *, name, tm=256):
    t = proj.shape[0]
    c3 = 3 * N_HEADS_A * HEAD_DIM_A
    hk = N_HEADS_A * HEAD_DIM_A
    tm = _tile(t, tm)

    rc = min(ROW_CHUNK, tm)

    def body(x_ref, ba_ref, cw_ref, al_ref, db_ref, q_ref, k_ref, v_ref, g_ref, b_ref, hx):
        i = pl.program_id(0)

        @pl.when(i == 0)
        def _():
            hx[0:HALO, :] = jnp.zeros((HALO, c3), F32)

        hx[pl.ds(HALO, rc), :] = x_ref[0:rc, :]
        dsts = (q_ref, k_ref, v_ref)
        for r in range(tm // rc):
            rows = slice(r * rc, (r + 1) * rc)
            src, base = (hx, HALO) if r == 0 else (x_ref, r * rc)
            for cb in range(c3 // HEAD_DIM_A):
                y = _conv_rows(src, base, rc, cw_ref, CONV_A, slice(cb * HEAD_DIM_A, (cb + 1) * HEAD_DIM_A))
                s = y * _sigmoid(y)
                kind, h = divmod(cb, N_HEADS_A)
                if kind < 2:
                    s = s * lax.rsqrt(jnp.sum(s * s, axis=-1, keepdims=True) + EPS)
                dsts[kind][rows, h * HEAD_DIM_A:(h + 1) * HEAD_DIM_A] = s
        hx[0:HALO, :] = x_ref[tm - HALO:tm, :]
        ba = ba_ref[...]
        beta = _sigmoid(ba)
        gfull = -jnp.exp(al_ref[...]) * _softplus(ba + db_ref[...])
        for h in range(N_HEADS_A):
            lo = h * HEAD_DIM_A
            b_ref[:, lo:lo + HEAD_DIM_A] = jnp.broadcast_to(beta[:, h:h + 1], (tm, HEAD_DIM_A))
            g_ref[:, lo:lo + HEAD_DIM_A] = jnp.broadcast_to(gfull[:, N_HEADS_A + h:N_HEADS_A + h + 1], (tm, HEAD_DIM_A))

    row = lambda w: pl.BlockSpec((tm, w), lambda i: (i, 0))
    fixed = lambda r, w: pl.BlockSpec((r, w), lambda i: (0, 0))
    out = jax.ShapeDtypeStruct((t, hk), F32)
    return pl.pallas_call(
        body,
        name=name,
        grid=(t // tm,),
        in_specs=[row(c3), pl.BlockSpec((tm, LANES), lambda i: (i, BA_COL_BLOCK)), fixed(CONV_A, c3), fixed(1, LANES),
                  fixed(1, LANES)],
        out_specs=[row(hk)] * 5,
        out_shape=[out] * 5,
        scratch_shapes=[pltpu.VMEM((HALO + rc, c3), F32)],
        compiler_params=_params(("arbitrary",)),
    )(proj, proj, conv_w, alog_row, dtb_row)


def _delta_pre_bwd(proj, conv_w, alog_row, dtb_row, dq, dk, dv, dg, db, dz, *, name, tm=256):
    t, pw = proj.shape
    c3 = 3 * N_HEADS_A * HEAD_DIM_A
    hk = N_HEADS_A * HEAD_DIM_A
    tm = _tile(t, tm)
    nt = t // tm
    hb = tm // HALO

    rc = min(ROW_CHUNK, tm)
    kw = CONV_A

    def body(x_ref, xp_ref, ba_ref, cw_ref, al_ref, db_ref, dq_ref, dk_ref, dv_ref, dg_ref, dbt_ref, dz_ref,
             dp_ref, dcw_ref, dal_ref, ddb_ref, hx, dyext, acc):
        i = pl.program_id(0)
        first = i == 0
        tile = nt - 1 - i

        @pl.when(tile == 0)
        def _():
            hx[0:HALO, :] = jnp.zeros((HALO, c3), F32)

        @pl.when(tile > 0)
        def _():
            hx[0:HALO, :] = xp_ref[...]

        @pl.when(first)
        def _():
            dyext[pl.ds(tm, HALO), :] = jnp.zeros((HALO, c3), F32)

        hx[pl.ds(HALO, rc), :] = x_ref[0:rc, :]
        acc[...] = jnp.zeros(acc.shape, F32)
        srcs = (dq_ref, dk_ref, dv_ref)
        for r in reversed(range(tm // rc)):
            rows = slice(r * rc, (r + 1) * rc)
            src, base = (hx, HALO) if r == 0 else (x_ref, r * rc)
            for cb in range(c3 // HEAD_DIM_A):
                cols = slice(cb * HEAD_DIM_A, (cb + 1) * HEAD_DIM_A)
                kind, h = divmod(cb, N_HEADS_A)
                wins = _shifted_rows(src, base, rc, [kw - 1 - j for j in range(kw)], cols)
                y = sum(cw_ref[j:j + 1, cols] * wins[j] for j in range(kw))
                sg = _sigmoid(y)
                s = y * sg
                ds = srcs[kind][rows, h * HEAD_DIM_A:(h + 1) * HEAD_DIM_A]
                if kind < 2:
                    rn = lax.rsqrt(jnp.sum(s * s, axis=-1, keepdims=True) + EPS)
                    qn = s * rn
                    ds = rn * (ds - qn * jnp.sum(ds * qn, axis=-1, keepdims=True))
                dy = ds * (sg * (1.0 + y * (1.0 - sg)))
                dyext[rows, cols] = dy
                dp_ref[rows, cols] = _conv_t_rows(dyext, r * rc, rc, cw_ref, kw, cols).astype(BF16)
                for j in range(kw):
                    acc[j * HALO:(j + 1) * HALO, cols] += _fold_rows(dy * wins[j])
        dyext[pl.ds(tm, HALO), :] = dyext[0:HALO, :]
        for j in range(kw):
            taps = jnp.sum(acc[j * HALO:(j + 1) * HALO, :], axis=0, keepdims=True)

            @pl.when(first)
            def _():
                dcw_ref[j:j + 1, :] = taps

            @pl.when(jnp.logical_not(first))
            def _():
                dcw_ref[j:j + 1, :] += taps

        dp_ref[:, c3:c3 + hk] = dz_ref[...]

        lane = lax.broadcasted_iota(jnp.int32, (tm, LANES), 1)
        gcol = jnp.zeros((tm, LANES), F32)
        for h in range(N_HEADS_A):
            lo = h * HEAD_DIM_A
            dbh = jnp.sum(dbt_ref[:, lo:lo + HEAD_DIM_A], axis=-1, keepdims=True)
            dgh = jnp.sum(dg_ref[:, lo:lo + HEAD_DIM_A], axis=-1, keepdims=True)
            gcol = gcol + jnp.where(lane == h, dbh, 0.0) + jnp.where(lane == N_HEADS_A + h, dgh, 0.0)
        ba = ba_ref[...]
        beta = _sigmoid(ba)
        a_neg = -jnp.exp(al_ref[...])
        z = ba + db_ref[...]
        dz = gcol * a_neg * _sigmoid(z)
        is_g = jnp.logical_and(lane >= N_HEADS_A, lane < 2 * N_HEADS_A)
        dba = jnp.where(lane < N_HEADS_A, gcol * beta * (1.0 - beta), jnp.where(is_g, dz, 0.0))
        dp_ref[:, c3 + hk:pw] = dba.astype(BF16)
        dal = jnp.sum(jnp.where(is_g, gcol * a_neg * _softplus(z), 0.0), axis=0, keepdims=True)
        ddb = jnp.sum(jnp.where(is_g, dz, 0.0), axis=0, keepdims=True)

        @pl.when(first)
        def _():
            dal_ref[...] = dal
            ddb_ref[...] = ddb

        @pl.when(jnp.logical_not(first))
        def _():
            dal_ref[...] += dal
            ddb_ref[...] += ddb

    rev = lambda w: pl.BlockSpec((tm, w), lambda i: (nt - 1 - i, 0))
    prev = pl.BlockSpec((HALO, c3), lambda i: (jnp.maximum((nt - 1 - i) * hb - 1, 0), 0))
    fixed = lambda r, w: pl.BlockSpec((r, w), lambda i: (0, 0))
    return pl.pallas_call(
        body,
        name=name,
        grid=(nt,),
        in_specs=[rev(c3), prev, pl.BlockSpec((tm, LANES), lambda i: (nt - 1 - i, BA_COL_BLOCK)), fixed(CONV_A, c3),
                  fixed(1, LANES), fixed(1, LANES)] + [rev(hk)] * 6,
        out_specs=[rev(pw), fixed(CONV_A, c3), fixed(1, LANES), fixed(1, LANES)],
        out_shape=[jax.ShapeDtypeStruct((t, pw), BF16), jax.ShapeDtypeStruct((CONV_A, c3), F32),
                   jax.ShapeDtypeStruct((1, LANES), F32), jax.ShapeDtypeStruct((1, LANES), F32)],
        scratch_shapes=[pltpu.VMEM((HALO + rc, c3), F32), pltpu.VMEM((tm + HALO, c3), F32),
                        pltpu.VMEM((CONV_A * HALO, c3), F32)],
        compiler_params=_params(("arbitrary",)),
    )(proj, proj, proj, conv_w, alog_row, dtb_row, dq, dk, dv, dg, db, dz)


def _gated_norm_fwd(o, proj, w, *, name, tm=512):
    t, d = o.shape
    tm = _tile(t, tm)

    def body(o_ref, z_ref, w_ref, y_ref):
        for h in range(N_HEADS_A):
            sl = slice(h * HEAD_DIM_A, (h + 1) * HEAD_DIM_A)
            oh = o_ref[:, sl]
            zh = z_ref[:, sl]
            r = lax.rsqrt(jnp.mean(oh * oh, axis=-1, keepdims=True) + EPS)
            y_ref[:, sl] = (oh * r * w_ref[...] * (zh * _sigmoid(zh))).astype(BF16)

    row = pl.BlockSpec((tm, d), lambda i: (i, 0))
    return pl.pallas_call(
        body,
        name=name,
        grid=(t // tm,),
        in_specs=[row, pl.BlockSpec((tm, d), lambda i: (i, Z_COL_BLOCK)), pl.BlockSpec((1, HEAD_DIM_A), lambda i: (0, 0))],
        out_specs=row,
        out_shape=jax.ShapeDtypeStruct((t, d), BF16),
        compiler_params=_params(("parallel",)),
    )(o, proj, w)


def _gated_norm_bwd(o, proj, w, dy, *, name, tm=512):
    t, d = o.shape
    tm = _tile(t, tm)

    def body(o_ref, z_ref, w_ref, dy_ref, do_ref, dz_ref, dw_ref):
        i = pl.program_id(0)
        dw = jnp.zeros((1, HEAD_DIM_A), F32)
        for h in range(N_HEADS_A):
            sl = slice(h * HEAD_DIM_A, (h + 1) * HEAD_DIM_A)
            oh = o_ref[:, sl]
            zh = z_ref[:, sl]
            g = dy_ref[:, sl]
            r = lax.rsqrt(jnp.mean(oh * oh, axis=-1, keepdims=True) + EPS)
            nh = oh * r
            sg = _sigmoid(zh)
            dz_ref[:, sl] = (g * nh * w_ref[...] * (sg * (1.0 + zh * (1.0 - sg)))).astype(BF16)
            dt = g * (zh * sg)
            dw = dw + jnp.sum(dt * nh, axis=0, keepdims=True)
            dnh = dt * w_ref[...]
            do_ref[:, sl] = r * (dnh - nh * jnp.mean(dnh * nh, axis=-1, keepdims=True))

        @pl.when(i == 0)
        def _():
            dw_ref[...] = dw

        @pl.when(i > 0)
        def _():
            dw_ref[...] += dw

    row = pl.BlockSpec((tm, d), lambda i: (i, 0))
    vec = pl.BlockSpec((1, HEAD_DIM_A), lambda i: (0, 0))
    return pl.pallas_call(
        body,
        name=name,
        grid=(t // tm,),
        in_specs=[row, pl.BlockSpec((tm, d), lambda i: (i, Z_COL_BLOCK)), vec, row],
        out_specs=[row, row, vec],
        out_shape=[jax.ShapeDtypeStruct((t, d), F32), jax.ShapeDtypeStruct((t, d), BF16),
                   jax.ShapeDtypeStruct((1, HEAD_DIM_A), F32)],
        compiler_params=_params(("arbitrary",)),
    )(o, proj, w, dy)


_NN = (((1,), (0,)), ((), ()))
_NT = (((1,), (1,)), ((), ()))
_TN = (((0,), (0,)), ((), ()))
_DIMS = {"nn": _NN, "nt": _NT, "tn": _TN}


def _raw_dot(a, b, kind, prec):
    dims = _DIMS[kind]
    a_hi, b_hi = a.astype(BF16), b.astype(BF16)
    out = lax.dot_general(a_hi, b_hi, dims, preferred_element_type=F32)
    if prec == "x3":
        a_lo = (a - a_hi.astype(F32)).astype(BF16)
        b_lo = (b - b_hi.astype(F32)).astype(BF16)
        out = out + lax.dot_general(a_hi, b_lo, dims, preferred_element_type=F32)
        out = out + lax.dot_general(a_lo, b_hi, dims, preferred_element_type=F32)
    elif prec == "s3":
        r1 = b - b_hi.astype(F32)
        b_mid = r1.astype(BF16)
        b_lo = (r1 - b_mid.astype(F32)).astype(BF16)
        out = out + lax.dot_general(a_hi, b_mid, dims, preferred_element_type=F32)
        out = out + lax.dot_general(a_hi, b_lo, dims, preferred_element_type=F32)
    return out


def _raw_dots(xs, ys, kind, prec):
    return [_raw_dot(x, y, kind, prec) for x, y in zip(xs, ys)]


@functools.partial(jax.custom_vjp, nondiff_argnums=(2, 3))
def _dots(xs, ys, kind, prec):
    return _raw_dots(xs, ys, kind, prec)


def _dots_fwd(xs, ys, kind, prec):
    return _raw_dots(xs, ys, kind, prec), (xs, ys)


def _dots_bwd(kind, prec, saved, gs):
    xs, ys = saved
    if kind == "nn":
        return _raw_dots(gs, ys, "nt", prec), _raw_dots(xs, gs, "tn", prec)
    if kind == "nt":
        return _raw_dots(gs, ys, "nn", prec), _raw_dots(gs, xs, "tn", prec)
    return _raw_dots(ys, gs, "nt", prec), _raw_dots(xs, gs, "nn", prec)


_dots.defvjp(_dots_fwd, _dots_bwd)


def _eye(c):
    return (lax.broadcasted_iota(jnp.int32, (c, c), 0) == lax.broadcasted_iota(jnp.int32, (c, c), 1)).astype(F32)


def _inv_unit_lower_raw(lmats):
    c = lmats[0].shape[0]
    eye = _eye(c)
    xs = [eye - l for l in lmats]
    ps = lmats
    for _ in range(int(math.log2(c)) - 1):
        ps = _raw_dots(ps, ps, "nn", "bf16")
        xs = [x + d for x, d in zip(xs, _raw_dots(xs, ps, "nn", "bf16"))]
    rs = [x - eye + d for x, d in zip(xs, _raw_dots(lmats, xs, "nn", "x3"))]
    return [x - d for x, d in zip(xs, _raw_dots(xs, rs, "nn", "bf16"))]


@jax.custom_vjp
def _inv_unit_lower(lmats, hints):
    return _inv_unit_lower_raw(lmats) if hints is None else hints


def _inv_fwd(lmats, hints):
    tms = _inv_unit_lower_raw(lmats) if hints is None else hints
    return tms, (tms, hints)


def _inv_bwd(saved, gs):
    tms, hints = saved
    ds = [-d for d in _raw_dots(_raw_dots(tms, gs, "tn", "x3"), tms, "nt", "x3")]
    return ds, (None if hints is None else [jnp.zeros_like(h) for h in hints])


_inv_unit_lower.defvjp(_inv_fwd, _inv_bwd)


def _delta_prep(qs, ks, vs, gs, bs, hints=None):
    c = qs[0].shape[0]
    nh = len(qs)
    ii = lax.broadcasted_iota(jnp.int32, (c, c), 0)
    jj = lax.broadcasted_iota(jnp.int32, (c, c), 1)
    incl = ii >= jj
    strict = ii > jj
    ltri = incl.astype(F32)
    eye = _eye(c)
    m1 = _dots([ltri] * nh, gs, "nn", "s3")
    gtot = [jnp.sum(g, axis=0, keepdims=True) for g in gs]
    decay = [jnp.exp(jnp.where(incl, m - m.T, NEG_BIG)) for m in m1]
    eg = [jnp.exp(m) for m in m1]
    kk = _dots(ks, ks, "nt", "bf16")
    lmats = [jnp.where(strict, b * x * d, 0.0) for b, x, d in zip(bs, kk, decay)]
    tinv = _inv_unit_lower(lmats, hints)
    toff = [t - eye for t in tinv]
    bv = [b * v for b, v in zip(bs, vs)]
    bk = [b * e * k for b, e, k in zip(bs, eg, ks)]
    u0 = [x + d for x, d in zip(bv, _dots(toff, bv, "nn", "bf16"))]
    wk = [x + d for x, d in zip(bk, _dots(toff, bk, "nn", "bf16"))]
    qsc = [q * (HEAD_DIM_A ** -0.5) for q in qs]
    qk = [x * d for x, d in zip(_dots(qsc, ks, "nt", "bf16"), decay)]
    q_dec = [q * e for q, e in zip(qsc, eg)]
    k_dec = [k * jnp.exp(t - m) for k, t, m in zip(ks, gtot, m1)]
    glast = [jnp.broadcast_to(jnp.exp(t), (c, c)) for t in gtot]
    return (u0, wk, qk, q_dec, k_dec, glast), tinv


def _delta_step(ss, u0, wk, qk, q_dec, k_dec, glast):
    us = [a - d for a, d in zip(u0, _dots(wk, ss, "nn", "bf16"))]
    os_ = [a + d for a, d in zip(_dots(q_dec, ss, "nn", "bf16"), _dots(qk, us, "nn", "bf16"))]
    s_new = [g * s + d for g, s, d in zip(glast, ss, _dots(k_dec, us, "tn", "bf16"))]
    return os_, s_new


HEADS_PER_STEP = 8


def _chunk_spec(nc, reverse=False):
    w = HEADS_PER_STEP * HEAD_DIM_A
    if reverse:
        return pl.BlockSpec((CHUNK, w), lambda h, n: (nc - 1 - n, h))
    return pl.BlockSpec((CHUNK, w), lambda h, n: (n, h))


def _head_slices():
    return [slice(j * HEAD_DIM_A, (j + 1) * HEAD_DIM_A) for j in range(HEADS_PER_STEP)]


def _heads(ref):
    return [ref[:, sl] for sl in _head_slices()]


def _delta_prep_fwd(q, k, v, gbc, bbc, *, name):
    t, d = q.shape
    nc = t // CHUNK

    def body(q_ref, k_ref, v_ref, g_ref, b_ref, *outs):
        res, tinv = _delta_prep(*[_heads(r) for r in (q_ref, k_ref, v_ref, g_ref, b_ref)])
        for ref, vals in zip(outs, res + (tinv,)):
            for sl, val in zip(_head_slices(), vals):
                ref[:, sl] = val

    spec = _chunk_spec(nc)
    return pl.pallas_call(
        body,
        name=name,
        grid=(N_HEADS_A // HEADS_PER_STEP, nc),
        in_specs=[spec] * 5,
        out_specs=[spec] * 7,
        out_shape=[jax.ShapeDtypeStruct((t, d), F32)] * 7,
        compiler_params=_params(("parallel", "parallel")),
    )(q, k, v, gbc, bbc)


def _delta_prep_bwd(q, k, v, gbc, bbc, tinv, cts, *, name):
    t, d = q.shape
    nc = t // CHUNK

    def body(q_ref, k_ref, v_ref, g_ref, b_ref, t_ref, c0, c1, c2, c3, c4, c5, *outs):
        def f(q_, k_, v_, g_, b_):
            return _delta_prep(q_, k_, v_, g_, b_, hints=_heads(t_ref))[0]

        _, vjp = jax.vjp(f, *[_heads(r) for r in (q_ref, k_ref, v_ref, g_ref, b_ref)])
        grads = vjp(tuple(_heads(c) for c in (c0, c1, c2, c3, c4, c5)))
        for ref, vals in zip(outs, grads):
            for sl, val in zip(_head_slices(), vals):
                ref[:, sl] = val

    spec = _chunk_spec(nc)
    return pl.pallas_call(
        body,
        name=name,
        grid=(N_HEADS_A // HEADS_PER_STEP, nc),
        in_specs=[spec] * 12,
        out_specs=[spec] * 5,
        out_shape=[jax.ShapeDtypeStruct((t, d), F32)] * 5,
        compiler_params=_params(("parallel", "parallel")),
    )(q, k, v, gbc, bbc, tinv, *cts)


def _delta_scan_fwd(prep, *, name):
    t, d = prep[0].shape
    nc = t // CHUNK

    def body(u0, wk, qk, qd, kd, gl, o_ref, st_ref, s_ref):
        n = pl.program_id(1)

        @pl.when(n == 0)
        def _():
            s_ref[...] = jnp.zeros(s_ref.shape, F32)

        ss = [s_ref[j] for j in range(HEADS_PER_STEP)]
        os_, s_new = _delta_step(ss, *[_heads(r) for r in (u0, wk, qk, qd, kd, gl)])
        for j, sl in enumerate(_head_slices()):
            st_ref[:, sl] = ss[j]
            o_ref[:, sl] = os_[j]
            s_ref[j] = s_new[j]

    spec = _chunk_spec(nc)
    return pl.pallas_call(
        body,
        name=name,
        grid=(N_HEADS_A // HEADS_PER_STEP, nc),
        in_specs=[spec] * 6,
        out_specs=[spec] * 2,
        out_shape=[jax.ShapeDtypeStruct((t, d), F32)] * 2,
        scratch_shapes=[pltpu.VMEM((HEADS_PER_STEP, HEAD_DIM_A, HEAD_DIM_A), F32)],
        compiler_params=_params(("parallel", "arbitrary")),
    )(*prep)


def _delta_scan_bwd(prep, states, do, *, name):
    t, d = do.shape
    nc = t // CHUNK

    def body(u0, wk, qk, qd, kd, gl, st_ref, do_ref, *rest):
        outs, ds_ref = rest[:6], rest[6]
        n = pl.program_id(1)

        @pl.when(n == 0)
        def _():
            ds_ref[...] = jnp.zeros(ds_ref.shape, F32)

        _, vjp = jax.vjp(_delta_step, *[_heads(r) for r in (st_ref, u0, wk, qk, qd, kd, gl)])
        grads = vjp((_heads(do_ref), [ds_ref[j] for j in range(HEADS_PER_STEP)]))
        for j, sl in enumerate(_head_slices()):
            ds_ref[j] = grads[0][j]
            for ref, vals in zip(outs, grads[1:]):
                ref[:, sl] = vals[j]

    spec = _chunk_spec(nc, reverse=True)
    return pl.pallas_call(
        body,
        name=name,
        grid=(N_HEADS_A // HEADS_PER_STEP, nc),
        in_specs=[spec] * 8,
        out_specs=[spec] * 6,
        out_shape=[jax.ShapeDtypeStruct((t, d), F32)] * 6,
        scratch_shapes=[pltpu.VMEM((HEADS_PER_STEP, HEAD_DIM_A, HEAD_DIM_A), F32)],
        compiler_params=_params(("parallel", "arbitrary")),
    )(*prep, states, do)


def _delta_fwd(q, k, v, gbc, bbc, proj, norm_w, *, name):
    assert HEADS_PER_STEP == N_HEADS_A
    t, d = q.shape
    nc = t // CHUNK

    def body(q_ref, k_ref, v_ref, g_ref, b_ref, z_ref, w_ref, o_ref, st_ref, t_ref, og_ref, s_ref):
        n = pl.program_id(0)

        @pl.when(n == 0)
        def _():
            s_ref[...] = jnp.zeros(s_ref.shape, F32)

        ss = [s_ref[j] for j in range(N_HEADS_A)]
        res, tinv = _delta_prep(*[_heads(r) for r in (q_ref, k_ref, v_ref, g_ref, b_ref)])
        os_, s_new = _delta_step(ss, *res)
        for j, sl in enumerate(_head_slices()):
            st_ref[:, sl] = ss[j]
            t_ref[:, sl] = tinv[j]
            o_ref[:, sl] = os_[j]
            s_ref[j] = s_new[j]
            zh = z_ref[:, sl]
            r = lax.rsqrt(jnp.mean(os_[j] * os_[j], axis=-1, keepdims=True) + EPS)
            og_ref[:, sl] = (os_[j] * r * w_ref[...] * (zh * _sigmoid(zh))).astype(BF16)

    spec = pl.BlockSpec((CHUNK, d), lambda n: (n, 0))
    f32 = jax.ShapeDtypeStruct((t, d), F32)
    return pl.pallas_call(
        body,
        name=name,
        grid=(nc,),
        in_specs=[spec] * 5 + [pl.BlockSpec((CHUNK, d), lambda n: (n, Z_COL_BLOCK)), pl.BlockSpec((1, HEAD_DIM_A), lambda n: (0, 0))],
        out_specs=[spec] * 4,
        out_shape=[f32, f32, f32, jax.ShapeDtypeStruct((t, d), BF16)],
        scratch_shapes=[pltpu.VMEM((N_HEADS_A, HEAD_DIM_A, HEAD_DIM_A), F32)],
        compiler_params=_params(("arbitrary",)),
    )(q, k, v, gbc, bbc, proj, norm_w)


def _delta_bwd(q, k, v, gbc, bbc, tinv, states, o, proj, norm_w, dog, *, name):
    t, d = q.shape
    nc = t // CHUNK

    def body(q_ref, k_ref, v_ref, g_ref, b_ref, t_ref, st_ref, o_ref, z_ref, w_ref, dog_ref,
             dq_ref, dk_ref, dv_ref, dg_ref, db_ref, dz_ref, dw_ref, ds_ref):
        n = pl.program_id(0)

        @pl.when(n == 0)
        def _():
            ds_ref[...] = jnp.zeros(ds_ref.shape, F32)

        dos = []
        dw = jnp.zeros((1, HEAD_DIM_A), F32)
        for sl in _head_slices():
            oh, zh, g = o_ref[:, sl], z_ref[:, sl], dog_ref[:, sl]
            r = lax.rsqrt(jnp.mean(oh * oh, axis=-1, keepdims=True) + EPS)
            nh = oh * r
            sg = _sigmoid(zh)
            dz_ref[:, sl] = (g * nh * w_ref[...] * (sg * (1.0 + zh * (1.0 - sg)))).astype(BF16)
            dt = g * (zh * sg)
            dw = dw + jnp.sum(dt * nh, axis=0, keepdims=True)
            dnh = dt * w_ref[...]
            dos.append(r * (dnh - nh * jnp.mean(dnh * nh, axis=-1, keepdims=True)))

        @pl.when(n == 0)
        def _():
            dw_ref[...] = dw

        @pl.when(n > 0)
        def _():
            dw_ref[...] += dw

        def chunk(qs, ks, vs, gs, bs, ss):
            return _delta_step(ss, *_delta_prep(qs, ks, vs, gs, bs, hints=_heads(t_ref))[0])

        _, vjp = jax.vjp(chunk, *[_heads(r) for r in (q_ref, k_ref, v_ref, g_ref, b_ref, st_ref)])
        grads = vjp((dos, [ds_ref[j] for j in range(N_HEADS_A)]))
        for j, sl in enumerate(_head_slices()):
            ds_ref[j] = grads[5][j]
            for ref, vals in zip((dq_ref, dk_ref, dv_ref, dg_ref, db_ref), grads[:5]):
                ref[:, sl] = vals[j]

    spec = pl.BlockSpec((CHUNK, d), lambda n: (nc - 1 - n, 0))
    vec = pl.BlockSpec((1, HEAD_DIM_A), lambda n: (0, 0))
    f32 = jax.ShapeDtypeStruct((t, d), F32)
    return pl.pallas_call(
        body,
        name=name,
        grid=(nc,),
        in_specs=[spec] * 8 + [pl.BlockSpec((CHUNK, d), lambda n: (nc - 1 - n, Z_COL_BLOCK)), vec, spec],
        out_specs=[spec] * 6 + [vec],
        out_shape=[f32] * 5 + [jax.ShapeDtypeStruct((t, d), BF16), jax.ShapeDtypeStruct((1, HEAD_DIM_A), F32)],
        scratch_shapes=[pltpu.VMEM((N_HEADS_A, HEAD_DIM_A, HEAD_DIM_A), F32)],
        compiler_params=_params(("arbitrary",)),
    )(q, k, v, gbc, bbc, tinv, states, o, proj, norm_w, dog)


def _alibi_slope(h):
    return 2.0 ** (-8.0 * (h + 1) / N_HEADS_B)


def _swa_load(sink_ref, q_ref, kp_ref, kc_ref, vp_ref, vc_ref):
    rg = lax.broadcasted_iota(jnp.int32, (GROUP_B * WINDOW, 1), 0) // WINDOW
    q4s, kcats, vcats, slopes, sinkcols = [], [], [], [], []
    for hk in range(N_KV_B):
        ks = slice(hk * HEAD_DIM_B, (hk + 1) * HEAD_DIM_B)
        heads = [hk * GROUP_B + g for g in range(GROUP_B)]
        q4s.append(jnp.concatenate([q_ref[:, h * HEAD_DIM_B:(h + 1) * HEAD_DIM_B] for h in heads], axis=0).astype(BF16))
        kcats.append(jnp.concatenate([kp_ref[:, ks], kc_ref[:, ks]], axis=0).astype(BF16))
        vcats.append(jnp.concatenate([vp_ref[:, ks], vc_ref[:, ks]], axis=0).astype(BF16))
        slope = jnp.zeros((GROUP_B * WINDOW, 1), F32)
        sink = jnp.zeros((GROUP_B * WINDOW, 1), F32)
        for g, h in enumerate(heads):
            slope = jnp.where(rg == g, _alibi_slope(h), slope)
            sink = jnp.where(rg == g, sink_ref[0, h], sink)
        slopes.append(slope)
        sinkcols.append(sink)
    return q4s, kcats, vcats, slopes, sinkcols


def _swa_probs(q4s, kcats, slopes, sinkcols, blk):
    rows = GROUP_B * WINDOW
    qi = lax.broadcasted_iota(jnp.int32, (rows, 2 * WINDOW), 0) % WINDOW
    kj = lax.broadcasted_iota(jnp.int32, (rows, 2 * WINDOW), 1)
    dist = qi + WINDOW - kj
    valid = (dist >= 0) & (dist < WINDOW) & (blk * WINDOW - WINDOW + kj >= 0)
    distf = dist.astype(F32)
    ss = [lax.dot_general(q, kc, _NT, preferred_element_type=F32) for q, kc in zip(q4s, kcats)]
    logits = [jnp.where(valid, s * (HEAD_DIM_B ** -0.5) - sl * distf, NEG_BIG) for s, sl in zip(ss, slopes)]
    ms = [jnp.maximum(jnp.max(l, axis=-1, keepdims=True), sk) for l, sk in zip(logits, sinkcols)]
    es = [jnp.exp(l - m) for l, m in zip(logits, ms)]
    esk = [jnp.exp(sk - m) for sk, m in zip(sinkcols, ms)]
    invs = [1.0 / (jnp.sum(e, axis=-1, keepdims=True) + k) for e, k in zip(es, esk)]
    return [e * i for e, i in zip(es, invs)], [k * i for k, i in zip(esk, invs)]


def _swa_fwd(proj, sinks, *, name):
    t = proj.shape[0]
    nb = t // WINDOW
    qd = N_HEADS_B * HEAD_DIM_B
    kd = N_KV_B * HEAD_DIM_B

    def body(sink_ref, q_ref, kp_ref, kc_ref, vp_ref, vc_ref, o_ref):
        blk = pl.program_id(0)
        q4s, kcats, vcats, slopes, sinkcols = _swa_load(sink_ref, q_ref, kp_ref, kc_ref, vp_ref, vc_ref)
        ps, _ = _swa_probs(q4s, kcats, slopes, sinkcols, blk)
        outs = [jnp.dot(p.astype(BF16), vc, preferred_element_type=F32) for p, vc in zip(ps, vcats)]
        for hk, out in enumerate(outs):
            for g in range(GROUP_B):
                h = hk * GROUP_B + g
                o_ref[:, h * HEAD_DIM_B:(h + 1) * HEAD_DIM_B] = out[g * WINDOW:(g + 1) * WINDOW, :].astype(BF16)

    q_spec = pl.BlockSpec((WINDOW, qd), lambda i: (i, 0))
    kv = lambda col, prev: pl.BlockSpec((WINDOW, kd), (lambda i: (jnp.maximum(i - 1, 0), col)) if prev else (lambda i: (i, col)))
    kcol, vcol = qd // kd, qd // kd + 1
    return pl.pallas_call(
        body,
        name=name,
        grid=(nb,),
        in_specs=[pl.BlockSpec(memory_space=pltpu.SMEM), q_spec, kv(kcol, True), kv(kcol, False), kv(vcol, True), kv(vcol, False)],
        out_specs=q_spec,
        out_shape=jax.ShapeDtypeStruct((t, qd), BF16),
        compiler_params=_params(("parallel",)),
    )(sinks, proj, proj, proj, proj, proj)


def _swa_bwd(proj, sinks, dout, *, name):
    t = proj.shape[0]
    nb = t // WINDOW
    qd = N_HEADS_B * HEAD_DIM_B
    kd = N_KV_B * HEAD_DIM_B
    scale = HEAD_DIM_B ** -0.5

    def body(sink_ref, q_ref, kp_ref, kc_ref, vp_ref, vc_ref, do_ref, dq_ref, dk_ref, dv_ref, dsk_ref):
        blk = pl.program_id(0)
        lane = lax.broadcasted_iota(jnp.int32, (1, LANES), 1)

        @pl.when(blk == 0)
        def _():
            dk_ref[...] = jnp.zeros((t, kd), F32)
            dv_ref[...] = jnp.zeros((t, kd), F32)
            dsk_ref[...] = jnp.zeros((1, LANES), F32)

        cur = pl.ds(pl.multiple_of(blk * WINDOW, WINDOW), WINDOW)
        prv = pl.ds(pl.multiple_of(jnp.maximum(blk - 1, 0) * WINDOW, WINDOW), WINDOW)
        q4s, kcats, vcats, slopes, sinkcols = _swa_load(sink_ref, q_ref, kp_ref, kc_ref, vp_ref, vc_ref)
        ps, psinks = _swa_probs(q4s, kcats, slopes, sinkcols, blk)
        do4s = [jnp.concatenate([do_ref[:, (hk * GROUP_B + g) * HEAD_DIM_B:(hk * GROUP_B + g + 1) * HEAD_DIM_B]
                                 for g in range(GROUP_B)], axis=0).astype(BF16) for hk in range(N_KV_B)]
        dps = [lax.dot_general(d, vc, _NT, preferred_element_type=F32) for d, vc in zip(do4s, vcats)]
        deltas = [jnp.sum(p * dp, axis=-1, keepdims=True) for p, dp in zip(ps, dps)]
        dss = [(p * (dp - dl) * scale).astype(BF16) for p, dp, dl in zip(ps, dps, deltas)]
        dq4s = [jnp.dot(ds, kc, preferred_element_type=F32) for ds, kc in zip(dss, kcats)]
        dkcs = [lax.dot_general(ds, q, _TN, preferred_element_type=F32) for ds, q in zip(dss, q4s)]
        dvcs = [lax.dot_general(p.astype(BF16), d, _TN, preferred_element_type=F32) for p, d in zip(ps, do4s)]
        dsk = jnp.zeros((1, LANES), F32)
        for hk in range(N_KV_B):
            ks = slice(hk * HEAD_DIM_B, (hk + 1) * HEAD_DIM_B)
            dsink = -psinks[hk] * deltas[hk]
            for g in range(GROUP_B):
                h = hk * GROUP_B + g
                rows = slice(g * WINDOW, (g + 1) * WINDOW)
                dq_ref[:, h * HEAD_DIM_B:(h + 1) * HEAD_DIM_B] = dq4s[hk][rows, :].astype(BF16)
                dsk = dsk + jnp.where(lane == h, jnp.sum(dsink[rows, :], axis=0, keepdims=True), 0.0)
            dk_ref[cur, ks] += dkcs[hk][WINDOW:, :]
            dv_ref[cur, ks] += dvcs[hk][WINDOW:, :]

            @pl.when(blk > 0)
            def _():
                dk_ref[prv, ks] += dkcs[hk][:WINDOW, :]
                dv_ref[prv, ks] += dvcs[hk][:WINDOW, :]

        dsk_ref[...] += dsk

    q_spec = pl.BlockSpec((WINDOW, qd), lambda i: (i, 0))
    kv = lambda col, prev: pl.BlockSpec((WINDOW, kd), (lambda i: (jnp.maximum(i - 1, 0), col)) if prev else (lambda i: (i, col)))
    kcol, vcol = qd // kd, qd // kd + 1
    full = pl.BlockSpec((t, kd), lambda i: (0, 0))
    return pl.pallas_call(
        body,
        name=name,
        grid=(nb,),
        in_specs=[pl.BlockSpec(memory_space=pltpu.SMEM), q_spec, kv(kcol, True), kv(kcol, False), kv(vcol, True), kv(vcol, False), q_spec],
        out_specs=[q_spec, full, full, pl.BlockSpec((1, LANES), lambda i: (0, 0))],
        out_shape=[jax.ShapeDtypeStruct((t, qd), BF16), jax.ShapeDtypeStruct((t, kd), F32),
                   jax.ShapeDtypeStruct((t, kd), F32), jax.ShapeDtypeStruct((1, LANES), F32)],
        compiler_params=_params(("arbitrary",)),
    )(sinks, proj, proj, proj, proj, proj, dout)


def _ffn_act_fwd(up, cw, *, name, tm=512, cb=256):
    _, t, f = up.shape
    tm, cb = _tile(t, tm), _tile(f, cb)

    rc = min(ROW_CHUNK, tm)

    def body(ug_ref, uv_ref, cg_ref, cv_ref, a_ref, hg, hv):
        i = pl.program_id(1)

        @pl.when(i == 0)
        def _():
            hg[0:HALO, :] = jnp.zeros((HALO, cb), F32)
            hv[0:HALO, :] = jnp.zeros((HALO, cb), F32)

        hg[pl.ds(HALO, rc), :] = ug_ref[0:rc, :]
        hv[pl.ds(HALO, rc), :] = uv_ref[0:rc, :]
        for r in range(tm // rc):
            if r == 0:
                yg = _conv_rows(hg, HALO, rc, cg_ref, FFN_CONV)
                yv = _conv_rows(hv, HALO, rc, cv_ref, FFN_CONV)
            else:
                yg = _conv_rows(ug_ref, r * rc, rc, cg_ref, FFN_CONV)
                yv = _conv_rows(uv_ref, r * rc, rc, cv_ref, FFN_CONV)
            a_ref[r * rc:(r + 1) * rc, :] = (yg * _sigmoid(yg) * yv).astype(BF16)
        hg[0:HALO, :] = ug_ref[tm - HALO:tm, :]
        hv[0:HALO, :] = uv_ref[tm - HALO:tm, :]

    ncb = f // cb
    half = lambda s: pl.BlockSpec((None, tm, cb), lambda c, i: (s, i, c))
    taps = lambda s: pl.BlockSpec((FFN_CONV, cb), lambda c, i: (0, c + s * ncb))
    return pl.pallas_call(
        body,
        name=name,
        grid=(ncb, t // tm),
        in_specs=[half(0), half(1), taps(0), taps(1)],
        out_specs=pl.BlockSpec((tm, cb), lambda c, i: (i, c)),
        out_shape=jax.ShapeDtypeStruct((t, f), BF16),
        scratch_shapes=[pltpu.VMEM((HALO + rc, cb), F32)] * 2,
        compiler_params=_params(("parallel", "arbitrary")),
    )(up, up, cw, cw)


def _ffn_act_bwd(up, cw, dact, *, name, tm=512, cb=256):
    _, t, f = up.shape
    tm, cb = _tile(t, tm), _tile(f, cb)
    nt = t // tm
    hb = tm // HALO

    rc = min(ROW_CHUNK, tm)
    nr = tm // rc
    kw = FFN_CONV

    def body(ug_ref, uv_ref, pg_ref, pv_ref, cg_ref, cv_ref, da_ref, du_ref, dcg_ref, dcv_ref,
             hg, hv, dyg, dyv):
        i = pl.program_id(1)
        first = i == 0
        tile = nt - 1 - i

        @pl.when(tile == 0)
        def _():
            hg[0:HALO, :] = jnp.zeros((HALO, cb), F32)
            hv[0:HALO, :] = jnp.zeros((HALO, cb), F32)

        @pl.when(tile > 0)
        def _():
            hg[0:HALO, :] = pg_ref[...]
            hv[0:HALO, :] = pv_ref[...]

        @pl.when(first)
        def _():
            dyg[pl.ds(tm, HALO), :] = jnp.zeros((HALO, cb), F32)
            dyv[pl.ds(tm, HALO), :] = jnp.zeros((HALO, cb), F32)

        hg[pl.ds(HALO, rc), :] = ug_ref[0:rc, :]
        hv[pl.ds(HALO, rc), :] = uv_ref[0:rc, :]
        dcg = [jnp.zeros((1, cb), F32) for _ in range(kw)]
        dcv = [jnp.zeros((1, cb), F32) for _ in range(kw)]
        for r in reversed(range(nr)):
            rows = slice(r * rc, (r + 1) * rc)
            src_g, src_v, base = (hg, hv, HALO) if r == 0 else (ug_ref, uv_ref, r * rc)
            yg = _conv_rows(src_g, base, rc, cg_ref, kw)
            yv = _conv_rows(src_v, base, rc, cv_ref, kw)
            sg = _sigmoid(yg)
            da = da_ref[rows, :]
            dy_g = da * yv * (sg * (1.0 + yg * (1.0 - sg)))
            dy_v = da * (yg * sg)
            dyg[rows, :] = dy_g
            dyv[rows, :] = dy_v
            du_ref[0, rows, :] = _conv_t_rows(dyg, r * rc, rc, cg_ref, kw).astype(BF16)
            du_ref[1, rows, :] = _conv_t_rows(dyv, r * rc, rc, cv_ref, kw).astype(BF16)
            for j in range(kw):
                dcg[j] = dcg[j] + jnp.sum(dy_g * src_g[pl.ds(base - (kw - 1) + j, rc), :], axis=0, keepdims=True)
                dcv[j] = dcv[j] + jnp.sum(dy_v * src_v[pl.ds(base - (kw - 1) + j, rc), :], axis=0, keepdims=True)
        dyg[pl.ds(tm, HALO), :] = dyg[0:HALO, :]
        dyv[pl.ds(tm, HALO), :] = dyv[0:HALO, :]
        for j in range(kw):
            @pl.when(first)
            def _():
                dcg_ref[j:j + 1, :] = dcg[j]
                dcv_ref[j:j + 1, :] = dcv[j]

            @pl.when(jnp.logical_not(first))
            def _():
                dcg_ref[j:j + 1, :] += dcg[j]
                dcv_ref[j:j + 1, :] += dcv[j]

    ncb = f // cb
    half = lambda s: pl.BlockSpec((None, tm, cb), lambda c, i: (s, nt - 1 - i, c))
    prev = lambda s: pl.BlockSpec((None, HALO, cb), lambda c, i: (s, jnp.maximum((nt - 1 - i) * hb - 1, 0), c))
    taps = lambda s: pl.BlockSpec((FFN_CONV, cb), lambda c, i: (0, c + s * ncb))
    dtaps = pl.BlockSpec((FFN_CONV, cb), lambda c, i: (0, c))
    return pl.pallas_call(
        body,
        name=name,
        grid=(ncb, nt),
        in_specs=[half(0), half(1), prev(0), prev(1), taps(0), taps(1), pl.BlockSpec((tm, cb), lambda c, i: (nt - 1 - i, c))],
        out_specs=[pl.BlockSpec((2, tm, cb), lambda c, i: (0, nt - 1 - i, c)), dtaps, dtaps],
        out_shape=[jax.ShapeDtypeStruct((2, t, f), BF16), jax.ShapeDtypeStruct((FFN_CONV, f), F32),
                   jax.ShapeDtypeStruct((FFN_CONV, f), F32)],
        scratch_shapes=[pltpu.VMEM((HALO + rc, cb), F32)] * 2 + [pltpu.VMEM((tm + HALO, cb), F32)] * 2,
        compiler_params=_params(("parallel", "arbitrary")),
    )(up, up, up, up, cw, cw, dact)


FFN_COL_TILE = 1408
FFN_SUB = 256
FFN_ROW_CHUNK = 32


def _sub_blocks(width):
    return [slice(c, min(c + FFN_SUB, width)) for c in range(0, width, FFN_SUB)]


def _ffn_up_act(n_f, w_up_t, cw, *, name, tm=512):
    t, d = n_f.shape
    f = w_up_t.shape[1]
    tm, tn = _tile(t, tm), _tile(f, FFN_COL_TILE)
    nj = f // tn
    rc = min(FFN_ROW_CHUNK, tm)
    kw = FFN_CONV

    def body(n_ref, wg_ref, wv_ref, cg_ref, cv_ref, up_ref, y_ref, a_ref, hg, hv):
        i = pl.program_id(1)

        @pl.when(i == 0)
        def _():
            hg[0:HALO, :] = jnp.zeros((HALO, tn), F32)
            hv[0:HALO, :] = jnp.zeros((HALO, tn), F32)

        def products(cs):
            up_ref[0, :, cs] = lax.dot_general(n_ref[...], wg_ref[cs, :], _NT, preferred_element_type=F32)
            up_ref[1, :, cs] = lax.dot_general(n_ref[...], wv_ref[cs, :], _NT, preferred_element_type=F32)

        subs = _sub_blocks(tn)
        ug, uv = up_ref.at[0], up_ref.at[1]
        products(subs[0])
        for ci, cs in enumerate(subs):
            if ci + 1 < len(subs):
                products(subs[ci + 1])
            hg[pl.ds(HALO, rc), cs] = ug[0:rc, cs]
            hv[pl.ds(HALO, rc), cs] = uv[0:rc, cs]
            for r in range(tm // rc):
                src_g, src_v, base = (hg, hv, HALO) if r == 0 else (ug, uv, r * rc)
                yg = _conv_rows(src_g, base, rc, cg_ref, kw, cs)
                yv = _conv_rows(src_v, base, rc, cv_ref, kw, cs)
                y_ref[0, r * rc:(r + 1) * rc, cs] = yg
                y_ref[1, r * rc:(r + 1) * rc, cs] = yv
                a_ref[r * rc:(r + 1) * rc, cs] = (yg * _sigmoid(yg) * yv).astype(BF16)
            hg[0:HALO, cs] = ug[tm - HALO:tm, cs]
            hv[0:HALO, cs] = uv[tm - HALO:tm, cs]

    half = lambda s: pl.BlockSpec((None, tn, d), lambda j, i: (s, j, 0))
    taps = lambda s: pl.BlockSpec((kw, tn), lambda j, i: (0, j + s * nj))
    pair = pl.BlockSpec((2, tm, tn), lambda j, i: (0, i, j))
    return pl.pallas_call(
        body,
        name=name,
        grid=(nj, t // tm),
        in_specs=[pl.BlockSpec((tm, d), lambda j, i: (i, 0)), half(0), half(1), taps(0), taps(1)],
        out_specs=[pair, pair, pl.BlockSpec((tm, tn), lambda j, i: (i, j))],
        out_shape=[jax.ShapeDtypeStruct((2, t, f), F32), jax.ShapeDtypeStruct((2, t, f), F32),
                   jax.ShapeDtypeStruct((t, f), BF16)],
        scratch_shapes=[pltpu.VMEM((HALO + rc, tn), F32)] * 2,
        compiler_params=_params(("parallel", "arbitrary")),
    )(n_f, w_up_t, w_up_t, cw, cw)


def _ffn_down_dx_act_bwd(dh, w_down, up, y, cw, *, name, tm=512):
    t, d = dh.shape
    f = w_down.shape[0]
    tm, tn = _tile(t, tm), _tile(f, FFN_COL_TILE)
    nj, nt = f // tn, t // tm
    rc = min(FFN_ROW_CHUNK, tm)
    nr = tm // rc
    kw = FFN_CONV

    def body(dh_ref, wd_ref, ug_ref, uv_ref, yg_ref, yv_ref, cg_ref, cv_ref, du_ref, dcg_ref, dcv_ref,
             dyg, dyv, da_s, dh_s):
        i = pl.program_id(1)
        first = i == 0

        @pl.when(first)
        def _():
            dyg[pl.ds(tm, HALO), :] = jnp.zeros((HALO, tn), F32)
            dyv[pl.ds(tm, HALO), :] = jnp.zeros((HALO, tn), F32)

        dh_s[...] = dh_ref[...].astype(BF16)

        def product(cs):
            da_s[:, cs] = lax.dot_general(dh_s[...], wd_ref[cs, :], _NT, preferred_element_type=F32)

        subs = _sub_blocks(tn)
        product(subs[0])
        for ci, cs in enumerate(subs):
            width = cs.stop - cs.start
            if ci + 1 < len(subs):
                product(subs[ci + 1])
            dcg = [jnp.zeros((HALO, width), F32) for _ in range(kw)]
            dcv = [jnp.zeros((HALO, width), F32) for _ in range(kw)]
            for r in reversed(range(nr)):
                rows = slice(r * rc, (r + 1) * rc)
                yg, yv = yg_ref[rows, cs], yv_ref[rows, cs]
                sg = _sigmoid(yg)
                da = da_s[rows, cs]
                dyg[rows, cs] = da * yv * (sg * (1.0 + yg * (1.0 - sg)))
                dyv[rows, cs] = da * (yg * sg)
                ahead_g = _ahead_rows(dyg, r * rc, rc, range(kw), cs)
                ahead_v = _ahead_rows(dyv, r * rc, rc, range(kw), cs)
                du_ref[0, rows, cs] = sum(cg_ref[kw - 1 - s:kw - s, cs] * ahead_g[s] for s in range(kw)).astype(BF16)
                du_ref[1, rows, cs] = sum(cv_ref[kw - 1 - s:kw - s, cs] * ahead_v[s] for s in range(kw)).astype(BF16)
                xg, xv = ug_ref[rows, cs], uv_ref[rows, cs]
                for s in range(kw):
                    dcg[kw - 1 - s] = dcg[kw - 1 - s] + _fold_rows(xg * ahead_g[s])
                    dcv[kw - 1 - s] = dcv[kw - 1 - s] + _fold_rows(xv * ahead_v[s])
            dyg[pl.ds(tm, HALO), cs] = dyg[0:HALO, cs]
            dyv[pl.ds(tm, HALO), cs] = dyv[0:HALO, cs]
            for j in range(kw):
                tg = jnp.sum(dcg[j], axis=0, keepdims=True)
                tv = jnp.sum(dcv[j], axis=0, keepdims=True)

                @pl.when(first)
                def _():
                    dcg_ref[j:j + 1, cs] = tg
                    dcv_ref[j:j + 1, cs] = tv

                @pl.when(jnp.logical_not(first))
                def _():
                    dcg_ref[j:j + 1, cs] += tg
                    dcv_ref[j:j + 1, cs] += tv

    half = lambda s: pl.BlockSpec((None, tm, tn), lambda j, i: (s, nt - 1 - i, j))
    taps = lambda s: pl.BlockSpec((kw, tn), lambda j, i: (0, j + s * nj))
    dtaps = pl.BlockSpec((kw, tn), lambda j, i: (0, j))
    return pl.pallas_call(
        body,
        name=name,
        grid=(nj, nt),
        in_specs=[pl.BlockSpec((tm, d), lambda j, i: (nt - 1 - i, 0)), pl.BlockSpec((tn, d), lambda j, i: (j, 0)),
                  half(0), half(1), half(0), half(1), taps(0), taps(1)],
        out_specs=[pl.BlockSpec((2, tm, tn), lambda j, i: (0, nt - 1 - i, j)), dtaps, dtaps],
        out_shape=[jax.ShapeDtypeStruct((2, t, f), BF16), jax.ShapeDtypeStruct((kw, f), F32),
                   jax.ShapeDtypeStruct((kw, f), F32)],
        scratch_shapes=[pltpu.VMEM((tm + HALO, tn), F32)] * 2 + [pltpu.VMEM((tm, tn), F32), pltpu.VMEM((tm, d), BF16)],
        compiler_params=_params(("parallel", "arbitrary")),
    )(dh, w_down, up, up, y, y, cw, cw)


def _ple_fwd(h, zg, pe, *, name, tm=512):
    t, d = h.shape
    tm = _tile(t, tm)

    def body(h_ref, z_ref, p_ref, o_ref):
        o_ref[...] = h_ref[...] + _sigmoid(z_ref[...]) * p_ref[...]

    row = pl.BlockSpec((tm, d), lambda i: (i, 0))
    return pl.pallas_call(
        body, name=name, grid=(t // tm,), in_specs=[row] * 3, out_specs=row,
        out_shape=jax.ShapeDtypeStruct((t, d), F32), compiler_params=_params(("parallel",)),
    )(h, zg, pe)


def _ple_bwd(dh, zg, pe, *, name, tm=512):
    t, d = dh.shape
    tm = _tile(t, tm)

    def body(g_ref, z_ref, p_ref, dz_ref, dp_ref):
        g = g_ref[...]
        sg = _sigmoid(z_ref[...])
        dz_ref[...] = (g * p_ref[...] * sg * (1.0 - sg)).astype(BF16)
        dp_ref[...] = (g * sg).astype(BF16)

    row = pl.BlockSpec((tm, d), lambda i: (i, 0))
    return pl.pallas_call(
        body, name=name, grid=(t // tm,), in_specs=[row] * 3, out_specs=[row] * 2,
        out_shape=[jax.ShapeDtypeStruct((t, d), BF16)] * 2, compiler_params=_params(("parallel",)),
    )(dh, zg, pe)


def _my_pos():
    return lax.axis_index("x"), lax.axis_index("y"), lax.axis_index("c")


def _all_gather(block, *, name, dep=None):
    r, w = block.shape
    has_dep = dep is not None

    def body(*refs):
        x_ref, out_ref, send_sems, recv_sems, local_sem = refs[:1] + refs[1 + has_dep:]
        x, y, c = _my_pos()
        me, sibling = (x, y, c), (x, y, 1 - c)
        chips = [(1 - x, y), (x, 1 - y), (1 - x, 1 - y)]

        def slot(px, py, pc):
            return out_ref.at[4 * px + 2 * py + pc]

        def copy(k, blk, to, src=None):
            return pltpu.make_async_remote_copy(
                src_ref=slot(*blk) if src is None else src, dst_ref=slot(*blk),
                send_sem=send_sems.at[k], recv_sem=recv_sems.at[k],
                device_id=to, device_id_type=pl.DeviceIdType.MESH)

        mine = pltpu.make_async_copy(x_ref, slot(*me), local_sem)
        mine.start()
        first = [copy(0, me, sibling, src=x_ref)]
        first += [copy(1 + j, me, (*chip, c), src=x_ref) for j, chip in enumerate(chips)]
        for cp in first:
            cp.start()
        passed = [copy(4 + j, (*chip, c), sibling) for j, chip in enumerate(chips)]
        for j, chip in enumerate(chips):
            copy(1 + j, (*chip, c), me).wait_recv()
            passed[j].start()
        copy(0, sibling, me).wait_recv()
        for j, chip in enumerate(chips):
            copy(4 + j, (*chip, 1 - c), me).wait_recv()
        for cp in first + passed:
            cp.wait_send()
        mine.wait()

    return pl.pallas_call(
        body,
        name=name,
        out_shape=jax.ShapeDtypeStruct((N_DEV, r, w), block.dtype),
        in_specs=[pl.BlockSpec(memory_space=pl.ANY)] * (1 + has_dep),
        out_specs=pl.BlockSpec(memory_space=pl.ANY),
        scratch_shapes=[pltpu.SemaphoreType.DMA((7,)), pltpu.SemaphoreType.DMA((7,)), pltpu.SemaphoreType.DMA],
    )(*((block, dep) if has_dep else (block,)))


def _all_to_all(slabs, *, name):
    n, r, w = slabs.shape

    def body(x_ref, out_ref, send_sems, recv_sems, local_sem):
        x, y, c = _my_pos()
        my_idx = 4 * x + 2 * y + c
        mine = pltpu.make_async_copy(x_ref.at[my_idx], out_ref.at[my_idx], local_sem)
        mine.start()
        copies = []
        for k in range(1, N_DEV):
            fx, fy, fc = (k >> 2) & 1, (k >> 1) & 1, k & 1
            px = (1 - x) if fx else x
            py = (1 - y) if fy else y
            pc = (1 - c) if fc else c
            cp = pltpu.make_async_remote_copy(
                src_ref=x_ref.at[4 * px + 2 * py + pc], dst_ref=out_ref.at[my_idx],
                send_sem=send_sems.at[k - 1], recv_sem=recv_sems.at[k - 1],
                device_id=(px, py, pc), device_id_type=pl.DeviceIdType.MESH)
            cp.start()
            copies.append(cp)
        for cp in copies:
            cp.wait_recv()
        for cp in copies:
            cp.wait_send()
        mine.wait()

    return pl.pallas_call(
        body,
        name=name,
        out_shape=jax.ShapeDtypeStruct((n, r, w), slabs.dtype),
        in_specs=[pl.BlockSpec(memory_space=pl.ANY)],
        out_specs=pl.BlockSpec(memory_space=pl.ANY),
        scratch_shapes=[pltpu.SemaphoreType.DMA((7,)), pltpu.SemaphoreType.DMA((7,)), pltpu.SemaphoreType.DMA],
    )(slabs)


def _exchange_copies(scatter, src_refs, land_refs, send_sems, recv_sems, local_sems):
    x, y, c = _my_pos()
    me = 4 * x + 2 * y + c
    local, remote = [], []
    for i, (s, l) in enumerate(zip(src_refs, land_refs)):
        local.append(pltpu.make_async_copy(s.at[me] if scatter else s, l.at[me], local_sems.at[i]))
        for k in range(1, N_DEV):
            px = (1 - x) if (k >> 2) & 1 else x
            py = (1 - y) if (k >> 1) & 1 else y
            pc = (1 - c) if k & 1 else c
            remote.append(pltpu.make_async_remote_copy(
                src_ref=s.at[4 * px + 2 * py + pc] if scatter else s, dst_ref=l.at[me],
                send_sem=send_sems.at[(N_DEV - 1) * i + k - 1], recv_sem=recv_sems.at[(N_DEV - 1) * i + k - 1],
                device_id=(px, py, pc), device_id_type=pl.DeviceIdType.MESH))
    return local, remote


def _exchange(arrays, *, scatter, name):
    n = len(arrays)

    def body(*refs):
        srcs, lands = refs[:n], refs[n:2 * n]
        local, remote = _exchange_copies(scatter, srcs, lands, *refs[2 * n:])
        for cp in local + remote:
            cp.start()
        for cp in remote:
            cp.wait_recv()
        for cp in remote:
            cp.wait_send()
        for cp in local:
            cp.wait()

    hbm = pl.BlockSpec(memory_space=pl.ANY)
    out = pl.pallas_call(
        body,
        name=name,
        out_shape=[jax.ShapeDtypeStruct(a.shape if scatter else (N_DEV,) + a.shape, a.dtype) for a in arrays],
        in_specs=[hbm] * n,
        out_specs=[hbm] * n,
        scratch_shapes=[pltpu.SemaphoreType.DMA(((N_DEV - 1) * n,)), pltpu.SemaphoreType.DMA(((N_DEV - 1) * n,)),
                        pltpu.SemaphoreType.DMA((n,))],
    )(*arrays)
    return list(out)


_HBM_SPEC = pl.BlockSpec(memory_space=pltpu.HBM)
_SEM_SPEC = pl.BlockSpec(memory_space=pltpu.SEMAPHORE)
_EFFECT = pltpu.SideEffectType.DATAFLOW_SIDE_EFFECTING


def _exchange_start(arrays, *, scatter, name, dep):
    n = len(arrays)
    srcs = [pltpu.with_memory_space_constraint(a, pltpu.HBM) for a in arrays]
    lands = [pltpu.with_memory_space_constraint(lax.empty(a.shape if scatter else (N_DEV,) + a.shape, a.dtype), pltpu.HBM)
             for a in arrays]

    def body(*refs):
        src_refs, land_refs = refs[:n], refs[n:2 * n]
        send_sems, recv_sems, local_sems = refs[2 * n + 1:2 * n + 4]
        token = refs[-1]
        local, remote = _exchange_copies(scatter, src_refs, land_refs, send_sems, recv_sems, local_sems)
        for cp in local + remote:
            cp.start()
        token[...] = jnp.zeros_like(token)

    sems = (pltpu.SemaphoreType.DMA(((N_DEV - 1) * n,)), pltpu.SemaphoreType.DMA(((N_DEV - 1) * n,)),
            pltpu.SemaphoreType.DMA((n,)))
    out = pl.pallas_call(
        body,
        name=name,
        out_shape=sems + tuple(pltpu.HBM(a.shape, a.dtype) for a in srcs + lands) + (jax.ShapeDtypeStruct((8, LANES), F32),),
        in_specs=[_HBM_SPEC] * (2 * n) + [pl.BlockSpec(memory_space=pl.ANY)],
        out_specs=(_SEM_SPEC,) * 3 + (_HBM_SPEC,) * (2 * n) + (pl.BlockSpec(memory_space=pltpu.VMEM),),
        input_output_aliases={i: 3 + i for i in range(2 * n)},
        compiler_params=pltpu.CompilerParams(has_side_effects=_EFFECT),
    )(*srcs, *lands, dep)
    return (out[:3], list(out[3:3 + n]), list(out[3 + n:3 + 2 * n])), out[-1]


def _exchange_wait(handle, after, *, scatter, name):
    sems, srcs, lands = handle
    n = len(srcs)

    def body(*refs):
        src_refs, land_refs = refs[:n], refs[n:2 * n]
        send_sems, recv_sems, local_sems = refs[2 * n:2 * n + 3]
        local, remote = _exchange_copies(scatter, src_refs, land_refs, send_sems, recv_sems, local_sems)
        for cp in remote:
            cp.wait_send()
            cp.wait_recv()
        for cp in local:
            cp.wait()

    out = pl.pallas_call(
        body,
        name=name,
        out_shape=tuple(pltpu.HBM(a.shape, a.dtype) for a in srcs + lands),
        in_specs=[_HBM_SPEC] * (2 * n) + [_SEM_SPEC] * 3 + [pl.BlockSpec(memory_space=pl.ANY)],
        out_specs=(_HBM_SPEC,) * (2 * n),
        input_output_aliases={i: i for i in range(2 * n)},
        compiler_params=pltpu.CompilerParams(has_side_effects=_EFFECT),
    )(*srcs, *lands, *sems, after)
    return list(out[n:])


def _sum_parts(parts, *, name, tr=512):
    n, r, lanes = parts.shape
    tr = tr if (r % tr == 0 and r > 1024) else r

    def body(p_ref, g_ref):
        g = p_ref[0].astype(F32)
        for j in range(1, n):
            g = g + p_ref[j].astype(F32)
        g_ref[...] = g

    row = pl.BlockSpec((tr, lanes), lambda i: (i, 0))
    return pl.pallas_call(
        body,
        name=name,
        grid=(r // tr,),
        in_specs=[pl.BlockSpec((n, tr, lanes), lambda i: (0, i, 0))],
        out_specs=row,
        out_shape=jax.ShapeDtypeStruct((r, lanes), F32),
        compiler_params=_params(("parallel",)),
    )(parts)


def _adamw_update(g, w, m, v):
    c1 = 1.0 / (1.0 - ADAM_B1 ** ADAM_STEP)
    c2 = 1.0 / (1.0 - ADAM_B2 ** ADAM_STEP)
    nm = ADAM_B1 * m + (1.0 - ADAM_B1) * g
    nv = ADAM_B2 * v + (1.0 - ADAM_B2) * (g * g)
    return -ADAM_LR * ((nm * c1) / (jnp.sqrt(nv * c2) + ADAM_EPS) + ADAM_WD * w), nm, nv


def _adamw_layer(g, w, m, v, layer, prev, *, name):
    nl, k, n = w.shape
    tr = max([d for d in range(8, min(k, 256) + 1, 8) if k % d == 0] or [k])
    in_parts = g.ndim == 3

    def body(g_ref, w_ref, m_ref, v_ref, *rest):
        go_ref, d_ref, nm_ref, nv_ref = rest[-4:]
        if in_parts:
            gg = g_ref[0].astype(F32)
            for j in range(1, g_ref.shape[0]):
                gg = gg + g_ref[j].astype(F32)
        else:
            gg = g_ref[...]
        d, nm, nv = _adamw_update(gg, w_ref[...], m_ref[...], v_ref[...])
        go_ref[...] = gg
        d_ref[...] = d
        nm_ref[...] = nm
        nv_ref[...] = nv

    lay = pl.BlockSpec((None, tr, n), lambda i: (layer, i, 0))
    n_prev = 0 if prev is None else 4
    out = jax.ShapeDtypeStruct((nl, k, n), F32)
    return pl.pallas_call(
        body,
        name=name,
        grid=(k // tr,),
        in_specs=[pl.BlockSpec((g.shape[0], tr, n), lambda i: (0, i, 0)) if in_parts else pl.BlockSpec((tr, n), lambda i: (i, 0)),
                  lay, lay, lay] + [pl.BlockSpec(memory_space=pl.ANY)] * n_prev,
        out_specs=[lay] * 4,
        out_shape=[out] * 4,
        input_output_aliases={4 + j: j for j in range(n_prev)},
        compiler_params=_params(("parallel",)),
    )(g, w, m, v, *(prev or ()))


def _adamw_packed(g, w, m, v, *, name, tr=512):
    r, lanes = g.shape
    tr = tr if r % tr == 0 else r
    c1 = 1.0 / (1.0 - ADAM_B1 ** ADAM_STEP)
    c2 = 1.0 / (1.0 - ADAM_B2 ** ADAM_STEP)

    def body(g_ref, w_ref, m_ref, v_ref, d_ref, nm_ref, nv_ref):
        g = g_ref[...]
        nm = ADAM_B1 * m_ref[...] + (1.0 - ADAM_B1) * g
        nv = ADAM_B2 * v_ref[...] + (1.0 - ADAM_B2) * (g * g)
        nm_ref[...] = nm
        nv_ref[...] = nv
        d_ref[...] = -ADAM_LR * ((nm * c1) / (jnp.sqrt(nv * c2) + ADAM_EPS) + ADAM_WD * w_ref[...])

    row = pl.BlockSpec((tr, lanes), lambda i: (i, 0))
    out = jax.ShapeDtypeStruct((r, lanes), F32)
    return pl.pallas_call(
        body,
        name=name,
        grid=(r // tr,),
        in_specs=[row] * 4,
        out_specs=[row] * 3,
        out_shape=[out] * 3,
        compiler_params=_params(("parallel",)),
    )(g, w, m, v)


BIG = ("a_w_in", "a_w_out", "b_w_in", "b_w_out", "f_w_up", "f_w_down", "ple_w_proj", "ple_w_gate")
CONVS = ("a_conv", "f_conv")
SMALL = ("norm_mix", "norm_ffn", "norm_ple", "norm_final", "a_log", "a_dt_bias", "a_norm", "b_sinks")
WEIGHTS = ("norm_mix", "norm_ffn", "norm_ple", "norm_final", "a_w_in", "a_conv", "a_log", "a_dt_bias", "a_norm",
           "a_w_out", "b_w_in", "b_sinks", "b_w_out", "f_w_up", "f_conv", "f_w_down", "ple_w_proj", "ple_w_gate")
SLAB_ROW_MULTIPLE = 512


def _pack(arrs, dtype, row_multiple):
    flat = jnp.concatenate([a.reshape(-1).astype(dtype) for a in arrs])
    rows = -(-flat.shape[0] // LANES)
    rows = -(-rows // row_multiple) * row_multiple
    return jnp.pad(flat, (0, rows * LANES - flat.shape[0])).reshape(rows, LANES)


def _unpack(slab, shapes):
    lead = slab.shape[:-2]
    flat = slab.reshape(lead + (-1,))
    out, off = [], 0
    for s in shapes:
        size = math.prod(s)
        out.append(flat[..., off:off + size].reshape(lead + tuple(s)))
        off += size
    return out


def _cols_full(g):
    g = jnp.moveaxis(g, 0, -2)
    return g.reshape(g.shape[:-2] + (g.shape[-2] * g.shape[-1],))


def _rows_full(g):
    g = jnp.moveaxis(g, 0, -3)
    return g.reshape(g.shape[:-3] + (g.shape[-3] * g.shape[-2], g.shape[-1]))


def _cols_split(wfull):
    n = wfull.shape[-1] // N_DEV
    g = wfull.reshape(wfull.shape[:-1] + (N_DEV, n))
    return jnp.moveaxis(g, -2, 0)


def _rows_split(wfull):
    k = wfull.shape[-2] // N_DEV
    g = wfull.reshape(wfull.shape[:-2] + (N_DEV, k, wfull.shape[-1]))
    return jnp.moveaxis(g, -3, 0)


TRANSPOSED = ("a_w_in", "b_w_in", "f_w_up", "ple_w_proj")


def _wire(name, a):
    return jnp.swapaxes(a, -1, -2) if name in TRANSPOSED else a


def _wire_shape(name, shape):
    return shape[:-2] + (shape[-1], shape[-2]) if name in TRANSPOSED else tuple(shape)


def _full(name, g):
    return _cols_full(g) if name in CONVS else _rows_full(g)


def _split(name, wfull):
    return _cols_split(wfull) if name in CONVS else _rows_split(wfull)


def _pack_split(grads, names, dtype, row_multiple):
    flat = jnp.concatenate([_split(n, grads[n]).reshape(N_DEV, -1).astype(dtype) for n in names], axis=1)
    rows = -(-flat.shape[1] // LANES)
    rows = -(-rows // row_multiple) * row_multiple
    return jnp.pad(flat, ((0, 0), (0, rows * LANES - flat.shape[1]))).reshape(N_DEV, rows, LANES)


def _pad_cols(a, width):
    return jnp.pad(a, ((0, 0), (0, width - a.shape[1])))


def kernel(x, p, norm_mix, norm_ffn, norm_ple, norm_final, a_w_in, a_conv, a_log, a_dt_bias, a_norm, a_w_out, b_w_in, b_sinks, b_w_out, f_w_up, f_conv, f_w_down, ple_w_proj, ple_w_gate, loss_target, m_norm_mix, m_norm_ffn, m_norm_ple, m_norm_final, m_a_w_in, m_a_conv, m_a_log, m_a_dt_bias, m_a_norm, m_a_w_out, m_b_w_in, m_b_sinks, m_b_w_out, m_f_w_up, m_f_conv, m_f_w_down, m_ple_w_proj, m_ple_w_gate, v_norm_mix, v_norm_ffn, v_norm_ple, v_norm_final, v_a_w_in, v_a_conv, v_a_log, v_a_dt_bias, v_a_norm, v_a_w_out, v_b_w_in, v_b_sinks, v_b_w_out, v_f_w_up, v_f_conv, v_f_w_down, v_ple_w_proj, v_ple_w_gate):
    wts = dict(norm_mix=norm_mix, norm_ffn=norm_ffn, norm_ple=norm_ple, norm_final=norm_final, a_w_in=a_w_in,
               a_conv=a_conv, a_log=a_log, a_dt_bias=a_dt_bias, a_norm=a_norm, a_w_out=a_w_out, b_w_in=b_w_in,
               b_sinks=b_sinks, b_w_out=b_w_out, f_w_up=f_w_up, f_conv=f_conv, f_w_down=f_w_down,
               ple_w_proj=ple_w_proj, ple_w_gate=ple_w_gate)
    mom = dict(norm_mix=m_norm_mix, norm_ffn=m_norm_ffn, norm_ple=m_norm_ple, norm_final=m_norm_final,
               a_w_in=m_a_w_in, a_conv=m_a_conv, a_log=m_a_log, a_dt_bias=m_a_dt_bias, a_norm=m_a_norm,
               a_w_out=m_a_w_out, b_w_in=m_b_w_in, b_sinks=m_b_sinks, b_w_out=m_b_w_out, f_w_up=m_f_w_up,
               f_conv=m_f_conv, f_w_down=m_f_w_down, ple_w_proj=m_ple_w_proj, ple_w_gate=m_ple_w_gate)
    var = dict(norm_mix=v_norm_mix, norm_ffn=v_norm_ffn, norm_ple=v_norm_ple, norm_final=v_norm_final,
               a_w_in=v_a_w_in, a_conv=v_a_conv, a_log=v_a_log, a_dt_bias=v_a_dt_bias, a_norm=v_a_norm,
               a_w_out=v_a_w_out, b_w_in=v_b_w_in, b_sinks=v_b_sinks, b_w_out=v_b_w_out, f_w_up=v_f_w_up,
               f_conv=v_f_conv, f_w_down=v_f_w_down, ple_w_proj=v_ple_w_proj, ple_w_gate=v_ple_w_gate)
    hk = N_HEADS_A * HEAD_DIM_A
    xs = x[0]
    tgt = loss_target[0]
    p_bf = p.astype(BF16)

    def shard(name, layer):
        return _wire(name, wts[name][layer]).astype(BF16)

    def stacked_rows(g):
        return g.reshape(g.shape[0] * g.shape[1], g.shape[2])

    n_in = a_w_in.shape[-1]
    first = _all_gather(jnp.concatenate([shard("a_w_in", 0), shard("a_w_out", 0)]), name="gather_mixer0")
    wa_in_t = jnp.pad(stacked_rows(first[:, :n_in]), ((0, PROJ_A - PROJ_A_REAL), (0, 0)))
    wa_out = stacked_rows(first[:, n_in:])
    gconv = _all_gather(_pack([wts[n] for n in CONVS], F32, 8), dep=first, name="gather_convs")
    conv_full = {n: _cols_full(g) for n, g in zip(CONVS, _unpack(gconv, [wts[n].shape for n in CONVS]))}
    cv_a, cv_f = conv_full["a_conv"][0], conv_full["f_conv"]
    layer_names = ("f_w_up", "f_w_down", "ple_w_proj", "ple_w_gate")
    gather0, tok = _exchange_start([shard(n, 0) for n in layer_names], scatter=False, name="gather_layer0_start", dep=gconv)
    gather1, tok = _exchange_start([shard(n, 0) for n in ("b_w_in", "b_w_out")] + [shard(n, 1) for n in layer_names],
                                   scatter=False, name="gather_layer1_start", dep=tok)

    alog_row = jnp.pad(a_log, ((0, 0), (N_HEADS_A, LANES - 2 * N_HEADS_A)))
    dtb_row = jnp.pad(a_dt_bias, ((0, 0), (N_HEADS_A, LANES - 2 * N_HEADS_A)))

    tile_f32, tile_bf16, rowsum = (F32, "tile"), (BF16, "tile"), (F32, "rowsum")

    def ffn_ple_fwd(i, h_a, n_f, next_norm, w_up_t, w_down, w_pp_t, w_pg):
        up, y, act = _ffn_up_act(n_f, w_up_t, cv_f[i], name=f"l{i}_ffn_up")
        h_b, n_p = _matmul_rows(act, w_down, _epi_res_norm, [h_a], [norm_ple[i]], [tile_f32, tile_bf16],
                                name=f"l{i}_ffn_down")
        pe = _matmul(p_bf[i, 0], w_pp_t, tb=True, name=f"l{i}_ple_proj")
        res = _matmul_rows(n_p, w_pg, _epi_ple, [h_b, pe], [] if next_norm is None else [next_norm],
                           [tile_f32, tile_f32] + ([] if next_norm is None else [tile_bf16]), name=f"l{i}_ple_gate")
        return res[1], (None if next_norm is None else res[2]), dict(n_f=n_f, up=up, y=y, act=act, h_b=h_b, n_p=n_p, zg=res[0], pe=pe)

    def layer_weights(lands):
        up_t, down, pp_t, pg = (stacked_rows(g) for g in lands)
        return up_t.reshape(2, D_FF, D_MODEL), down, pp_t, pg

    n0 = _rms_fwd(xs, norm_mix[0], name="l0_mix_norm")
    proj = _matmul(n0, wa_in_t, tb=True, tm=512, dep=tok, name="l0_in_proj")
    q, k, v, gbc, bbc = _delta_pre_fwd(proj, cv_a, alog_row, dtb_row, name="l0_delta_pre")
    o, states, tinv, og = _delta_fwd(q, k, v, gbc, bbc, proj, a_norm, name="l0_delta")
    h1, nf0 = _matmul_rows(og, wa_out, _epi_res_norm, [xs], [norm_ffn[0]], [tile_f32, tile_bf16], name="l0_mix_out")
    lw0 = layer_weights(_exchange_wait(gather0, h1, scatter=False, name="gather_layer0_wait"))
    h3, n1, sv0 = ffn_ple_fwd(0, h1, nf0, norm_mix[1], *lw0)

    lands1 = _exchange_wait(gather1, h3, scatter=False, name="gather_layer1_wait")
    wb_in_t, wb_out = stacked_rows(lands1[0]), stacked_rows(lands1[1])
    lw1 = layer_weights(lands1[2:])
    pb = _matmul(n1, wb_in_t, tb=True, name="l1_in_qkv")
    att = _swa_fwd(pb, b_sinks, name="l1_swa")
    h4, nf1 = _matmul_rows(att, wb_out, _epi_res_norm, [h3], [norm_ffn[1]], [tile_f32, tile_bf16], name="l1_mix_out")
    h6, _, sv1 = ffn_ple_fwd(1, h4, nf1, None, *lw1)

    loss_row, dh6, d_norm_final = _final_loss(h6, norm_final, tgt, name="final_loss")
    loss = lax.psum(loss_row[0, 0], MESH_AXES)

    def ffn_ple_bwd(i, dh_c, h_a, sv, lw, dep):
        w_up_t, w_down, _, w_pg = lw
        dzg, dpe = _ple_bwd(dh_c, sv["zg"], sv["pe"], name=f"l{i}_ple_mix_bwd")
        d_pg = _matmul(sv["n_p"], dzg, ta=True, out_dtype=BF16, dep=dep, name=f"l{i}_ple_gate_dw")
        d_pp_t = _matmul(dpe, p_bf[i, 0], ta=True, out_dtype=BF16, name=f"l{i}_ple_proj_dw")
        dh_b, d_np = _matmul_rows(dzg, w_pg, _epi_rms_bwd, [sv["h_b"], dh_c], [norm_ple[i]], [tile_f32, rowsum], tb=True,
                                  name=f"l{i}_ple_gate_dx")
        d_down = _matmul(sv["act"], dh_b, ta=True, out_dtype=BF16, name=f"l{i}_ffn_down_dw")
        dup, d_cg, d_cv = _ffn_down_dx_act_bwd(dh_b, w_down, sv["up"], sv["y"], cv_f[i], name=f"l{i}_ffn_down_dx")
        d_up_t = _matmul(dup, sv["n_f"], ta=True, out_dtype=BF16, name=f"l{i}_ffn_up_dw")
        dh_a, d_nf = _matmul_rows(dup, w_up_t, _epi_rms_bwd, [h_a, dh_b], [norm_ffn[i]], [tile_f32, rowsum],
                                  name=f"l{i}_ffn_up_dx")
        mats = [d_up_t.reshape(2 * D_FF, D_MODEL), d_down, d_pp_t, d_pg]
        return dh_a, mats, dict(norm_ple=d_np, norm_ffn=d_nf, f_conv=jnp.concatenate([d_cg, d_cv], axis=1))

    def slabs(g):
        return g.reshape(N_DEV, g.shape[0] // N_DEV, g.shape[1])

    dh4, mats1, g1 = ffn_ple_bwd(1, dh6, h4, sv1, lw1, None)
    datt = _matmul(dh4, wb_out, tb=True, out_dtype=BF16, name="l1_mix_out_dx")
    d_wb_out = _matmul(att, dh4, ta=True, out_dtype=BF16, name="l1_mix_out_dw")
    dq_b, dk_b, dv_b, dsinks = _swa_bwd(pb, b_sinks, datt, name="l1_swa_bwd")
    dpb = jnp.concatenate([dq_b, dk_b.astype(BF16), dv_b.astype(BF16)], axis=1)
    d_wb_in_t = _matmul(dpb, n1, ta=True, out_dtype=BF16, name="l1_in_qkv_dw")
    send1, tok = _exchange_start([slabs(g) for g in [d_wb_in_t, d_wb_out] + mats1], scatter=True,
                                 name="exchange_layer1_start", dep=d_wb_in_t)
    dh3, d_nm1 = _matmul_rows(dpb, wb_in_t, _epi_rms_bwd, [h3, dh4], [norm_mix[1]], [tile_f32, rowsum], name="l1_in_qkv_dx")

    dh1, mats0, g0 = ffn_ple_bwd(0, dh3, h1, sv0, lw0, tok)
    send0, tok = _exchange_start([slabs(g) for g in mats0], scatter=True, name="exchange_layer0_start", dep=mats0[0])
    dog = _matmul(dh1, wa_out, tb=True, dep=tok, name="l0_mix_out_dx")
    d_wa_out = _matmul(og, dh1, ta=True, out_dtype=BF16, name="l0_mix_out_dw")
    dq, dk, dv, dgbc, dbbc, dz0, d_anorm = _delta_bwd(q, k, v, gbc, bbc, tinv, states, o, proj, a_norm, dog,
                                                      name="l0_delta_bwd")
    dproj, d_aconv, d_alog, d_dtb = _delta_pre_bwd(proj, cv_a, alog_row, dtb_row, dq, dk, dv, dgbc, dbbc, dz0,
                                                   name="l0_delta_pre_bwd")
    d_wa_in_t = _matmul(dproj, n0, ta=True, out_dtype=BF16, name="l0_in_proj_dw")
    sendm, tok = _exchange_start([slabs(d_wa_in_t[:PROJ_A_REAL]), slabs(d_wa_out)], scatter=True,
                                 name="exchange_mixer0_start", dep=d_wa_in_t)
    dx, d_nm0 = _matmul_rows(dproj, wa_in_t, _epi_rms_bwd, [xs, dh1], [norm_mix[0]], [tile_f32, rowsum], dep=tok,
                             name="l0_in_proj_dx")

    recv1 = _exchange_wait(send1, dx, scatter=True, name="exchange_layer1_wait")
    recv0 = _exchange_wait(send0, recv1[0], scatter=True, name="exchange_layer0_wait")
    parts = {("b_w_in", 0): recv1[0], ("b_w_out", 0): recv1[1]}
    parts.update({(n, 1): r for n, r in zip(layer_names, recv1[2:])})
    parts.update({(n, 0): r for n, r in zip(layer_names, recv0)})

    outs = {}

    def update_matrix(name):
        w_, m_, v_ = (_wire(name, a) for a in (wts[name], mom[name], var[name]))
        res = None
        for layer in range(w_.shape[0]):
            res = _adamw_layer(parts[(name, layer)], w_, m_, v_, layer, res, name=f"adamw_{name}_{layer}")
        for kind, arr in zip(("grad", "delta", "new_m", "new_v"), res):
            outs[(kind, name)] = _wire(name, arr)
        return res

    last = [update_matrix(n) for n in ("b_w_in", "b_w_out") + layer_names][-1]
    recvm = _exchange_wait(sendm, last[0], scatter=True, name="exchange_mixer0_wait")
    parts.update({("a_w_in", 0): recvm[0], ("a_w_out", 0): recvm[1]})
    update_matrix("a_w_in")
    update_matrix("a_w_out")

    gconvs = dict(a_conv=d_aconv[None], f_conv=jnp.stack([g0["f_conv"], g1["f_conv"]]))
    small_g = dict(norm_mix=jnp.concatenate([d_nm0, d_nm1]), norm_ffn=jnp.concatenate([g0["norm_ffn"], g1["norm_ffn"]]),
                   norm_ple=jnp.concatenate([g0["norm_ple"], g1["norm_ple"]]), norm_final=d_norm_final[0],
                   a_log=d_alog[:, N_HEADS_A:2 * N_HEADS_A], a_dt_bias=d_dtb[:, N_HEADS_A:2 * N_HEADS_A],
                   a_norm=d_anorm, b_sinks=dsinks[:, :N_HEADS_B])
    recv_conv = _all_to_all(_pack_split(gconvs, CONVS, F32, 8), name="exchange_conv_grads")
    recv_small = _all_gather(_pack([small_g[n] for n in SMALL], F32, 8), name="gather_small_grads")
    for names, recv, tag in ((CONVS, recv_conv, "convs"), (SMALL, recv_small, "small")):
        shapes = [wts[n].shape for n in names]
        g_slab = _sum_parts(recv, name=f"sum_{tag}")
        packed = [_pack([d[n] for n in names], F32, 8) for d in (wts, mom, var)]
        res = _adamw_packed(g_slab, *packed, name=f"adamw_{tag}")
        for kind, slab in zip(("grad", "delta", "new_m", "new_v"), (g_slab,) + tuple(res)):
            for n, arr in zip(names, _unpack(slab, shapes)):
                outs[(kind, n)] = arr

    result = [loss, dx[None]]
    for kind in ("grad", "delta", "new_m", "new_v"):
        result += [outs[(kind, n)] for n in WEIGHTS]
    return tuple(result)
```

```python
import functools
import math

import jax
import jax.numpy as jnp
from jax import lax
from jax.experimental import pallas as pl
from jax.experimental.pallas import tpu as pltpu

F32 = jnp.float32
BF16 = jnp.bfloat16

D_MODEL = 1024
N_HEADS_A = 8
HEAD_DIM_A = 128
CONV_A = 4
CHUNK = 128
N_HEADS_B = 16
N_KV_B = 4
GROUP_B = N_HEADS_B // N_KV_B
HEAD_DIM_B = 64
WINDOW = 128
D_FF = 2816
FFN_CONV = 3
PLE_DIM = 256
EPS = 1e-6
N_DEV = 8
HALO = 8
PROJ_A_REAL = 4 * N_HEADS_A * HEAD_DIM_A + 2 * N_HEADS_A
PROJ_A = 4 * N_HEADS_A * HEAD_DIM_A + 128
Z_COL_BLOCK = 3
BA_COL_BLOCK = 32

ADAM_LR = 0.001
ADAM_B1 = 0.9
ADAM_B2 = 0.999
ADAM_EPS = 1e-08
ADAM_WD = 0.01
ADAM_STEP = 10

LANES = 128
VMEM_LIMIT_BYTES = 56 * 1024 * 1024
NEG_BIG = -1e30

MESH_AXES = ("x", "y", "c")


def _params(sem=None):
    return pltpu.CompilerParams(dimension_semantics=sem, vmem_limit_bytes=VMEM_LIMIT_BYTES)


def _tile(n, target):
    best = None
    for t in range(LANES, min(n, target) + 1, LANES):
        if n % t == 0:
            best = t
    return best or n


def _sigmoid(x):
    return 0.5 * jnp.tanh(0.5 * x) + 0.5


def _softplus(x):
    return jnp.maximum(x, 0.0) + jnp.log1p(jnp.exp(-jnp.abs(x)))


def _matmul(a, b, *, name, ta=False, tb=False, res=None, out_dtype=F32, tm=1408, tn=1408, tk=None, dep=None):
    sa, sb = a.ndim == 3, b.ndim == 3
    ns = a.shape[0] if sa else (b.shape[0] if sb else 1)
    contract_stack = sa and sb
    out_stacked = sa != sb
    m = a.shape[-1] if ta else a.shape[-2]
    k = a.shape[-2] if ta else a.shape[-1]
    n = b.shape[-2] if tb else b.shape[-1]
    assert (b.shape[-1] if tb else b.shape[-2]) == k, (a.shape, b.shape, ta, tb)
    if tk is None:
        tk = 1024 if ta else 2816
    tm, tn, tk = _tile(m, tm), _tile(n, tn), _tile(k, tk)
    nk = k // tk
    nsteps = nk * (ns if contract_stack else 1)
    dims = (((0 if ta else 1,), (1 if tb else 0,)), ((), ()))

    def spec(block, stacked, order):
        def index(g, i, j, kk):
            two = order(i, j, kk % nk)
            if not stacked:
                return two
            return (kk // nk if contract_stack else g,) + two
        return pl.BlockSpec(((None,) if stacked else ()) + block, index)

    a_spec = spec((tk, tm), sa, lambda i, j, kq: (kq, i)) if ta else spec((tm, tk), sa, lambda i, j, kq: (i, kq))
    b_spec = spec((tn, tk), sb, lambda i, j, kq: (j, kq)) if tb else spec((tk, tn), sb, lambda i, j, kq: (kq, j))
    o_spec = spec((tm, tn), out_stacked, lambda i, j, kq: (i, j))
    has_res = res is not None
    has_dep = dep is not None

    def body(*refs):
        a_ref, b_ref = refs[0], refs[1]
        r_ref = refs[2] if has_res else None
        o_ref = refs[2 + has_res + has_dep]
        part = lax.dot_general(a_ref[...].astype(BF16), b_ref[...].astype(BF16), dims, preferred_element_type=F32)

        def finish(acc):
            if has_res:
                acc = acc + r_ref[...].astype(F32)
            o_ref[...] = acc.astype(out_dtype)

        if nsteps == 1:
            finish(part)
        else:
            acc_ref = refs[-1]
            kk = pl.program_id(3)

            @pl.when(kk == 0)
            def _():
                acc_ref[...] = part

            @pl.when(kk > 0)
            def _():
                acc_ref[...] += part

            @pl.when(kk == nsteps - 1)
            def _():
                finish(acc_ref[...])

    in_specs = [a_spec, b_spec] + ([o_spec] if has_res else []) + ([pl.BlockSpec(memory_space=pl.ANY)] if has_dep else [])
    args = (a, b) + ((res,) if has_res else ()) + ((dep,) if has_dep else ())
    return pl.pallas_call(
        body,
        name=name,
        grid=(ns if out_stacked else 1, m // tm, n // tn, nsteps),
        in_specs=in_specs,
        out_specs=o_spec,
        out_shape=jax.ShapeDtypeStruct(((ns,) if out_stacked else ()) + (m, n), out_dtype),
        scratch_shapes=[pltpu.VMEM((tm, tn), F32)] if nsteps > 1 else [],
        compiler_params=_params(("parallel", "parallel", "parallel", "arbitrary")),
    )(*args)


EPI_ROWS = 32


def _matmul_rows(a, b, epilogue, tiles_in, rows_in, outs, *, name, tb=False, tm=512, tk=None, dep=None):
    stacked = a.ndim == 3
    ns = a.shape[0] if stacked else 1
    m, k = a.shape[-2], a.shape[-1]
    n = b.shape[-2] if tb else b.shape[-1]
    assert (b.shape[-1] if tb else b.shape[-2]) == k and (b.ndim == 3) == stacked, (a.shape, b.shape, tb)
    tm, tk = _tile(m, tm), _tile(k, 2816 if tk is None else tk)
    nk = k // tk
    nsteps = nk * ns
    dims = (((1,), (1 if tb else 0,)), ((), ()))
    lead = (None,) if stacked else ()
    front = (lambda kk: (kk // nk,)) if stacked else (lambda kk: ())
    a_spec = pl.BlockSpec(lead + (tm, tk), lambda i, kk: front(kk) + (i, kk % nk))
    if tb:
        b_spec = pl.BlockSpec(lead + (n, tk), lambda i, kk: front(kk) + (0, kk % nk))
    else:
        b_spec = pl.BlockSpec(lead + (tk, n), lambda i, kk: front(kk) + (kk % nk, 0))
    tile_spec = pl.BlockSpec((tm, n), lambda i, kk: (i, 0))
    row_spec = pl.BlockSpec((1, n), lambda i, kk: (0, 0))
    n_t, n_r, has_dep = len(tiles_in), len(rows_in), dep is not None

    def body(*refs):
        a_ref, b_ref = refs[:2]
        tile_refs = refs[2:2 + n_t]
        row_refs = refs[2 + n_t:2 + n_t + n_r]
        out_refs = refs[2 + n_t + n_r + has_dep:-1]
        acc_ref = refs[-1]
        part = lax.dot_general(a_ref[...].astype(BF16), b_ref[...].astype(BF16), dims, preferred_element_type=F32)
        kk = pl.program_id(1)
        if nsteps == 1:
            acc_ref[...] = part
        else:
            @pl.when(kk == 0)
            def _():
                acc_ref[...] = part

            @pl.when(kk > 0)
            def _():
                acc_ref[...] += part

        @pl.when(kk == nsteps - 1)
        def _():
            epilogue(acc_ref, tile_refs, row_refs, out_refs, pl.program_id(0) == 0)

    return pl.pallas_call(
        body,
        name=name,
        grid=(m // tm, nsteps),
        in_specs=[a_spec, b_spec] + [tile_spec] * n_t + [row_spec] * n_r + ([pl.BlockSpec(memory_space=pl.ANY)] if has_dep else []),
        out_specs=[tile_spec if kind == "tile" else row_spec for _, kind in outs],
        out_shape=[jax.ShapeDtypeStruct((m, n) if kind == "tile" else (1, n), dt) for dt, kind in outs],
        scratch_shapes=[pltpu.VMEM((tm, n), F32)],
        compiler_params=_params(("arbitrary", "arbitrary")),
    )(a, b, *tiles_in, *[r.reshape(1, n) for r in rows_in], *((dep,) if has_dep else ()))


def _row_chunks(ref):
    return [pl.ds(r, EPI_ROWS) for r in range(0, ref.shape[0], EPI_ROWS)]


def _epi_res_norm(acc, tiles, rows, outs, first):
    (res,), (w,), (h_out, n_out) = tiles, rows, outs
    for rs in _row_chunks(acc):
        h = acc[rs, :] + res[rs, :]
        h_out[rs, :] = h
        r = lax.rsqrt(jnp.mean(h * h, axis=-1, keepdims=True) + EPS)
        n_out[rs, :] = (h * r * w[...]).astype(BF16)


def _epi_ple(acc, tiles, rows, outs, first):
    hb, pe = tiles
    for rs in _row_chunks(acc):
        zg = acc[rs, :]
        outs[0][rs, :] = zg
        h = hb[rs, :] + _sigmoid(zg) * pe[rs, :]
        outs[1][rs, :] = h
        if rows:
            r = lax.rsqrt(jnp.mean(h * h, axis=-1, keepdims=True) + EPS)
            outs[2][rs, :] = (h * r * rows[0][...]).astype(BF16)


def _epi_rms_bwd(acc, tiles, rows, outs, first):
    (h_ref, skip), (w,), (dh_out, dw_out) = tiles, rows, outs
    dw = jnp.zeros((1, acc.shape[1]), F32)
    for rs in _row_chunks(acc):
        x = h_ref[rs, :]
        r = lax.rsqrt(jnp.mean(x * x, axis=-1, keepdims=True) + EPS)
        nh = x * r
        g = acc[rs, :]
        gw = g * w[...]
        dh_out[rs, :] = r * (gw - nh * jnp.mean(gw * nh, axis=-1, keepdims=True)) + skip[rs, :]
        dw = dw + jnp.sum(g * nh, axis=0, keepdims=True)

    @pl.when(first)
    def _():
        dw_out[...] = dw

    @pl.when(jnp.logical_not(first))
    def _():
        dw_out[...] += dw


def _rms_fwd(h, w, *, name, tm=512):
    t, d = h.shape
    tm = _tile(t, tm)

    def body(h_ref, w_ref, o_ref):
        x = h_ref[...]
        r = lax.rsqrt(jnp.mean(x * x, axis=-1, keepdims=True) + EPS)
        o_ref[...] = (x * r * w_ref[...]).astype(BF16)

    return pl.pallas_call(
        body,
        name=name,
        grid=(t // tm,),
        in_specs=[pl.BlockSpec((tm, d), lambda i: (i, 0)), pl.BlockSpec((1, d), lambda i: (0, 0))],
        out_specs=pl.BlockSpec((tm, d), lambda i: (i, 0)),
        out_shape=jax.ShapeDtypeStruct((t, d), BF16),
        compiler_params=_params(("parallel",)),
    )(h, w.reshape(1, d))


def _rms_bwd(h, w, dn, skip, *, name, tm=512):
    t, d = h.shape
    tm = _tile(t, tm)

    def body(h_ref, w_ref, dn_ref, skip_ref, dh_ref, dw_ref):
        i = pl.program_id(0)
        x = h_ref[...]
        r = lax.rsqrt(jnp.mean(x * x, axis=-1, keepdims=True) + EPS)
        nh = x * r
        g = dn_ref[...].astype(F32)
        gw = g * w_ref[...]
        dh_ref[...] = r * (gw - nh * jnp.mean(gw * nh, axis=-1, keepdims=True)) + skip_ref[...]
        part = jnp.sum(g * nh, axis=0, keepdims=True)

        @pl.when(i == 0)
        def _():
            dw_ref[...] = part

        @pl.when(i > 0)
        def _():
            dw_ref[...] += part

    row = pl.BlockSpec((tm, d), lambda i: (i, 0))
    vec = pl.BlockSpec((1, d), lambda i: (0, 0))
    return pl.pallas_call(
        body,
        name=name,
        grid=(t // tm,),
        in_specs=[row, vec, row, row],
        out_specs=[row, vec],
        out_shape=[jax.ShapeDtypeStruct((t, d), F32), jax.ShapeDtypeStruct((1, d), F32)],
        compiler_params=_params(("arbitrary",)),
    )(h, w.reshape(1, d), dn, skip)


def _final_loss(h, w, target, *, name, tm=512):
    t, d = h.shape
    tm = _tile(t, tm)

    def body(h_ref, w_ref, tg_ref, loss_ref, dh_ref, dw_ref):
        i = pl.program_id(0)
        x = h_ref[...]
        r = lax.rsqrt(jnp.mean(x * x, axis=-1, keepdims=True) + EPS)
        nh = x * r
        err = nh * w_ref[...] - tg_ref[...]
        lpart = (0.5 / d) * jnp.sum(jnp.sum(err * err, axis=-1, keepdims=True), axis=0, keepdims=True)
        g = err * (1.0 / d)
        gw = g * w_ref[...]
        dh_ref[...] = r * (gw - nh * jnp.mean(gw * nh, axis=-1, keepdims=True))
        part = jnp.sum(g * nh, axis=0, keepdims=True)
        lrow = jnp.broadcast_to(lpart, (1, LANES))

        @pl.when(i == 0)
        def _():
            dw_ref[...] = part
            loss_ref[...] = lrow

        @pl.when(i > 0)
        def _():
            dw_ref[...] += part
            loss_ref[...] += lrow

    row = pl.BlockSpec((tm, d), lambda i: (i, 0))
    vec = pl.BlockSpec((1, d), lambda i: (0, 0))
    return pl.pallas_call(
        body,
        name=name,
        grid=(t // tm,),
        in_specs=[row, vec, row],
        out_specs=[pl.BlockSpec((1, LANES), lambda i: (0, 0)), row, vec],
        out_shape=[jax.ShapeDtypeStruct((1, LANES), F32), jax.ShapeDtypeStruct((t, d), F32), jax.ShapeDtypeStruct((1, d), F32)],
        compiler_params=_params(("arbitrary",)),
    )(h, w.reshape(1, d), target)


def _conv_from_ext(ext_ref, cw_ref, kw, tm):
    y = cw_ref[kw - 1:kw, :] * ext_ref[pl.ds(HALO, tm), :]
    for i in range(kw - 1):
        y = y + cw_ref[i:i + 1, :] * ext_ref[pl.ds(HALO - (kw - 1) + i, tm), :]
    return y


ROW_CHUNK = 64


def _shifted_rows(src_ref, base, rows, shifts, cols=slice(None)):
    ext = src_ref[pl.ds(base - HALO, rows + HALO), cols]
    return [ext[HALO:, :] if s == 0 else pltpu.roll(ext, s, 0)[HALO:, :] for s in shifts]


def _conv_rows(src_ref, base, rows, cw_ref, kw, cols=slice(None)):
    wins = _shifted_rows(src_ref, base, rows, range(kw), cols)
    y = cw_ref[kw - 1:kw, cols] * wins[0]
    for s in range(1, kw):
        y = y + cw_ref[kw - 1 - s:kw - s, cols] * wins[s]
    return y


def _ahead_rows(src_ref, base, rows, shifts, cols=slice(None)):
    ext = src_ref[pl.ds(base, rows + HALO), cols]
    return [ext[:rows, :] if s == 0 else pltpu.roll(ext, rows + HALO - s, 0)[:rows, :] for s in shifts]


def _conv_t_rows(dy_ref, base, rows, cw_ref, kw, cols=slice(None)):
    wins = _ahead_rows(dy_ref, base, rows, range(kw), cols)
    dx = cw_ref[kw - 1:kw, cols] * wins[0]
    for s in range(1, kw):
        dx = dx + cw_ref[kw - 1 - s:kw - s, cols] * wins[s]
    return dx


def _fold_rows(x):
    out = x[0:HALO, :]
    for g in range(1, x.shape[0] // HALO):
        out = out + x[g * HALO:(g + 1) * HALO, :]
    return out


def _conv_bwd_from_ext(xext_ref, dyext_ref, cw_ref, dcw_ref, kw, tm, first):
    dy = dyext_ref[pl.ds(0, tm), :]
    dx = cw_ref[kw - 1:kw, :] * dy
    for i in range(kw - 1):
        dx = dx + cw_ref[i:i + 1, :] * dyext_ref[pl.ds(kw - 1 - i, tm), :]
    for i in range(kw):
        part = jnp.sum(dy * xext_ref[pl.ds(HALO - (kw - 1) + i, tm), :], axis=0, keepdims=True)

        @pl.when(first)
        def _():
            dcw_ref[i:i + 1, :] = part

        @pl.when(jnp.logical_not(first))
        def _():
            dcw_ref[i:i + 1, :] += part

    return dx


def _delta_pre_fwd(proj, conv_w, alog_row, dtb_row, *, name, tm=256):
    t = proj.shape[0]
    c3 = 3 * N_HEADS_A * HEAD_DIM_A
    hk = N_HEADS_A * HEAD_DIM_A
    tm = _tile(t, tm)

    rc = min(ROW_CHUNK, tm)

    def body(x_ref, ba_ref, cw_ref, al_ref, db_ref, q_ref, k_ref, v_ref, g_ref, b_ref, y_ref, hx):
        i = pl.program_id(0)

        @pl.when(i == 0)
        def _():
            hx[0:HALO, :] = jnp.zeros((HALO, c3), F32)

        hx[pl.ds(HALO, rc), :] = x_ref[0:rc, :]
        dsts = (q_ref, k_ref, v_ref)
        for r in range(tm // rc):
            rows = slice(r * rc, (r + 1) * rc)
            src, base = (hx, HALO) if r == 0 else (x_ref, r * rc)
            for cb in range(c3 // HEAD_DIM_A):
                cols = slice(cb * HEAD_DIM_A, (cb + 1) * HEAD_DIM_A)
                y = _conv_rows(src, base, rc, cw_ref, CONV_A, cols)
                y_ref[rows, cols] = y
                s = y * _sigmoid(y)
                kind, h = divmod(cb, N_HEADS_A)
                if kind < 2:
                    s = s * lax.rsqrt(jnp.sum(s * s, axis=-1, keepdims=True) + EPS)
                dsts[kind][rows, h * HEAD_DIM_A:(h + 1) * HEAD_DIM_A] = s
        hx[0:HALO, :] = x_ref[tm - HALO:tm, :]
        ba = ba_ref[...]
        beta = _sigmoid(ba)
        gfull = -jnp.exp(al_ref[...]) * _softplus(ba + db_ref[...])
        for h in range(N_HEADS_A):
            lo = h * HEAD_DIM_A
            b_ref[:, lo:lo + HEAD_DIM_A] = jnp.broadcast_to(beta[:, h:h + 1], (tm, HEAD_DIM_A))
            g_ref[:, lo:lo + HEAD_DIM_A] = jnp.broadcast_to(gfull[:, N_HEADS_A + h:N_HEADS_A + h + 1], (tm, HEAD_DIM_A))

    row = lambda w: pl.BlockSpec((tm, w), lambda i: (i, 0))
    fixed = lambda r, w: pl.BlockSpec((r, w), lambda i: (0, 0))
    out = jax.ShapeDtypeStruct((t, hk), F32)
    return pl.pallas_call(
        body,
        name=name,
        grid=(t // tm,),
        in_specs=[row(c3), pl.BlockSpec((tm, LANES), lambda i: (i, BA_COL_BLOCK)), fixed(CONV_A, c3), fixed(1, LANES),
                  fixed(1, LANES)],
        out_specs=[row(hk)] * 5 + [row(c3)],
        out_shape=[out] * 5 + [jax.ShapeDtypeStruct((t, c3), F32)],
        scratch_shapes=[pltpu.VMEM((HALO + rc, c3), F32)],
        compiler_params=_params(("arbitrary",)),
    )(proj, proj, conv_w, alog_row, dtb_row)


def _delta_pre_bwd(proj, y, conv_w, alog_row, dtb_row, dq, dk, dv, dg, db, dz, *, name, tm=256):
    t, pw = proj.shape
    c3 = 3 * N_HEADS_A * HEAD_DIM_A
    hk = N_HEADS_A * HEAD_DIM_A
    tm = _tile(t, tm)
    nt = t // tm

    rc = min(ROW_CHUNK, tm)
    kw = CONV_A

    def body(x_ref, y_ref, ba_ref, cw_ref, al_ref, db_ref, dq_ref, dk_ref, dv_ref, dg_ref, dbt_ref, dz_ref,
             dp_ref, dcw_ref, dal_ref, ddb_ref, dyext, acc):
        i = pl.program_id(0)
        first = i == 0

        @pl.when(first)
        def _():
            dyext[pl.ds(tm, HALO), :] = jnp.zeros((HALO, c3), F32)

        acc[...] = jnp.zeros(acc.shape, F32)
        srcs = (dq_ref, dk_ref, dv_ref)
        for r in reversed(range(tm // rc)):
            rows = slice(r * rc, (r + 1) * rc)
            for cb in range(c3 // HEAD_DIM_A):
                cols = slice(cb * HEAD_DIM_A, (cb + 1) * HEAD_DIM_A)
                kind, h = divmod(cb, N_HEADS_A)
                yv = y_ref[rows, cols]
                sg = _sigmoid(yv)
                s = yv * sg
                ds = srcs[kind][rows, h * HEAD_DIM_A:(h + 1) * HEAD_DIM_A]
                if kind < 2:
                    rn = lax.rsqrt(jnp.sum(s * s, axis=-1, keepdims=True) + EPS)
                    qn = s * rn
                    ds = rn * (ds - qn * jnp.sum(ds * qn, axis=-1, keepdims=True))
                dyext[rows, cols] = ds * (sg * (1.0 + yv * (1.0 - sg)))
                ahead = _ahead_rows(dyext, r * rc, rc, range(kw), cols)
                dp_ref[rows, cols] = sum(cw_ref[kw - 1 - s:kw - s, cols] * ahead[s] for s in range(kw)).astype(BF16)
                xv = x_ref[rows, cols]
                for s in range(kw):
                    acc[(kw - 1 - s) * HALO:(kw - s) * HALO, cols] += _fold_rows(xv * ahead[s])
        dyext[pl.ds(tm, HALO), :] = dyext[0:HALO, :]
        for j in range(kw):
            taps = jnp.sum(acc[j * HALO:(j + 1) * HALO, :], axis=0, keepdims=True)

            @pl.when(first)
            def _():
                dcw_ref[j:j + 1, :] = taps

            @pl.when(jnp.logical_not(first))
            def _():
                dcw_ref[j:j + 1, :] += taps

        dp_ref[:, c3:c3 + hk] = dz_ref[...]

        lane = lax.broadcasted_iota(jnp.int32, (tm, LANES), 1)
        gcol = jnp.zeros((tm, LANES), F32)
        for h in range(N_HEADS_A):
            lo = h * HEAD_DIM_A
            dbh = jnp.sum(dbt_ref[:, lo:lo + HEAD_DIM_A], axis=-1, keepdims=True)
            dgh = jnp.sum(dg_ref[:, lo:lo + HEAD_DIM_A], axis=-1, keepdims=True)
            gcol = gcol + jnp.where(lane == h, dbh, 0.0) + jnp.where(lane == N_HEADS_A + h, dgh, 0.0)
        ba = ba_ref[...]
        beta = _sigmoid(ba)
        a_neg = -jnp.exp(al_ref[...])
        z = ba + db_ref[...]
        dz = gcol * a_neg * _sigmoid(z)
        is_g = jnp.logical_and(lane >= N_HEADS_A, lane < 2 * N_HEADS_A)
        dba = jnp.where(lane < N_HEADS_A, gcol * beta * (1.0 - beta), jnp.where(is_g, dz, 0.0))
        dp_ref[:, c3 + hk:pw] = dba.astype(BF16)
        dal = jnp.sum(jnp.where(is_g, gcol * a_neg * _softplus(z), 0.0), axis=0, keepdims=True)
        ddb = jnp.sum(jnp.where(is_g, dz, 0.0), axis=0, keepdims=True)

        @pl.when(first)
        def _():
            dal_ref[...] = dal
            ddb_ref[...] = ddb

        @pl.when(jnp.logical_not(first))
        def _():
            dal_ref[...] += dal
            ddb_ref[...] += ddb

    rev = lambda w: pl.BlockSpec((tm, w), lambda i: (nt - 1 - i, 0))
    fixed = lambda r, w: pl.BlockSpec((r, w), lambda i: (0, 0))
    return pl.pallas_call(
        body,
        name=name,
        grid=(nt,),
        in_specs=[rev(c3), rev(c3), pl.BlockSpec((tm, LANES), lambda i: (nt - 1 - i, BA_COL_BLOCK)), fixed(CONV_A, c3),
                  fixed(1, LANES), fixed(1, LANES)] + [rev(hk)] * 6,
        out_specs=[rev(pw), fixed(CONV_A, c3), fixed(1, LANES), fixed(1, LANES)],
        out_shape=[jax.ShapeDtypeStruct((t, pw), BF16), jax.ShapeDtypeStruct((CONV_A, c3), F32),
                   jax.ShapeDtypeStruct((1, LANES), F32), jax.ShapeDtypeStruct((1, LANES), F32)],
        scratch_shapes=[pltpu.VMEM((tm + HALO, c3), F32), pltpu.VMEM((CONV_A * HALO, c3), F32)],
        compiler_params=_params(("arbitrary",)),
    )(proj, y, proj, conv_w, alog_row, dtb_row, dq, dk, dv, dg, db, dz)


def _gated_norm_fwd(o, proj, w, *, name, tm=512):
    t, d = o.shape
    tm = _tile(t, tm)

    def body(o_ref, z_ref, w_ref, y_ref):
        for h in range(N_HEADS_A):
            sl = slice(h * HEAD_DIM_A, (h + 1) * HEAD_DIM_A)
            oh = o_ref[:, sl]
            zh = z_ref[:, sl]
            r = lax.rsqrt(jnp.mean(oh * oh, axis=-1, keepdims=True) + EPS)
            y_ref[:, sl] = (oh * r * w_ref[...] * (zh * _sigmoid(zh))).astype(BF16)

    row = pl.BlockSpec((tm, d), lambda i: (i, 0))
    return pl.pallas_call(
        body,
        name=name,
        grid=(t // tm,),
        in_specs=[row, pl.BlockSpec((tm, d), lambda i: (i, Z_COL_BLOCK)), pl.BlockSpec((1, HEAD_DIM_A), lambda i: (0, 0))],
        out_specs=row,
        out_shape=jax.ShapeDtypeStruct((t, d), BF16),
        compiler_params=_params(("parallel",)),
    )(o, proj, w)


def _gated_norm_bwd(o, proj, w, dy, *, name, tm=512):
    t, d = o.shape
    tm = _tile(t, tm)

    def body(o_ref, z_ref, w_ref, dy_ref, do_ref, dz_ref, dw_ref):
        i = pl.program_id(0)
        dw = jnp.zeros((1, HEAD_DIM_A), F32)
        for h in range(N_HEADS_A):
            sl = slice(h * HEAD_DIM_A, (h + 1) * HEAD_DIM_A)
            oh = o_ref[:, sl]
            zh = z_ref[:, sl]
            g = dy_ref[:, sl]
            r = lax.rsqrt(jnp.mean(oh * oh, axis=-1, keepdims=True) + EPS)
            nh = oh * r
            sg = _sigmoid(zh)
            dz_ref[:, sl] = (g * nh * w_ref[...] * (sg * (1.0 + zh * (1.0 - sg)))).astype(BF16)
            dt = g * (zh * sg)
            dw = dw + jnp.sum(dt * nh, axis=0, keepdims=True)
            dnh = dt * w_ref[...]
            do_ref[:, sl] = r * (dnh - nh * jnp.mean(dnh * nh, axis=-1, keepdims=True))

        @pl.when(i == 0)
        def _():
            dw_ref[...] = dw

        @pl.when(i > 0)
        def _():
            dw_ref[...] += dw

    row = pl.BlockSpec((tm, d), lambda i: (i, 0))
    vec = pl.BlockSpec((1, HEAD_DIM_A), lambda i: (0, 0))
    return pl.pallas_call(
        body,
        name=name,
        grid=(t // tm,),
        in_specs=[row, pl.BlockSpec((tm, d), lambda i: (i, Z_COL_BLOCK)), vec, row],
        out_specs=[row, row, vec],
        out_shape=[jax.ShapeDtypeStruct((t, d), F32), jax.ShapeDtypeStruct((t, d), BF16),
                   jax.ShapeDtypeStruct((1, HEAD_DIM_A), F32)],
        compiler_params=_params(("arbitrary",)),
    )(o, proj, w, dy)


_NN = (((1,), (0,)), ((), ()))
_NT = (((1,), (1,)), ((), ()))
_TN = (((0,), (0,)), ((), ()))
_DIMS = {"nn": _NN, "nt": _NT, "tn": _TN}


def _raw_dot(a, b, kind, prec):
    dims = _DIMS[kind]
    a_hi, b_hi = a.astype(BF16), b.astype(BF16)
    out = lax.dot_general(a_hi, b_hi, dims, preferred_element_type=F32)
    if prec == "x3":
        a_lo = (a - a_hi.astype(F32)).astype(BF16)
        b_lo = (b - b_hi.astype(F32)).astype(BF16)
        out = out + lax.dot_general(a_hi, b_lo, dims, preferred_element_type=F32)
        out = out + lax.dot_general(a_lo, b_hi, dims, preferred_element_type=F32)
    elif prec == "s3":
        r1 = b - b_hi.astype(F32)
        b_mid = r1.astype(BF16)
        b_lo = (r1 - b_mid.astype(F32)).astype(BF16)
        out = out + lax.dot_general(a_hi, b_mid, dims, preferred_element_type=F32)
        out = out + lax.dot_general(a_hi, b_lo, dims, preferred_element_type=F32)
    return out


def _raw_dots(xs, ys, kind, prec):
    return [_raw_dot(x, y, kind, prec) for x, y in zip(xs, ys)]


@functools.partial(jax.custom_vjp, nondiff_argnums=(2, 3))
def _dots(xs, ys, kind, prec):
    return _raw_dots(xs, ys, kind, prec)


def _dots_fwd(xs, ys, kind, prec):
    return _raw_dots(xs, ys, kind, prec), (xs, ys)


def _dots_bwd(kind, prec, saved, gs):
    xs, ys = saved
    if kind == "nn":
        return _raw_dots(gs, ys, "nt", prec), _raw_dots(xs, gs, "tn", prec)
    if kind == "nt":
        return _raw_dots(gs, ys, "nn", prec), _raw_dots(gs, xs, "tn", prec)
    return _raw_dots(ys, gs, "nt", prec), _raw_dots(xs, gs, "nn", prec)


_dots.defvjp(_dots_fwd, _dots_bwd)


def _eye(c):
    return (lax.broadcasted_iota(jnp.int32, (c, c), 0) == lax.broadcasted_iota(jnp.int32, (c, c), 1)).astype(F32)


def _inv_unit_lower_raw(lmats):
    c = lmats[0].shape[0]
    eye = _eye(c)
    xs = [eye - l for l in lmats]
    ps = lmats
    for _ in range(int(math.log2(c)) - 1):
        ps = _raw_dots(ps, ps, "nn", "bf16")
        xs = [x + d for x, d in zip(xs, _raw_dots(xs, ps, "nn", "bf16"))]
    rs = [x - eye + d for x, d in zip(xs, _raw_dots(lmats, xs, "nn", "x3"))]
    return [x - d for x, d in zip(xs, _raw_dots(xs, rs, "nn", "bf16"))]


@jax.custom_vjp
def _inv_unit_lower(lmats, hints):
    return _inv_unit_lower_raw(lmats) if hints is None else hints


def _inv_fwd(lmats, hints):
    tms = _inv_unit_lower_raw(lmats) if hints is None else hints
    return tms, (tms, hints)


def _inv_bwd(saved, gs):
    tms, hints = saved
    ds = [-d for d in _raw_dots(_raw_dots(tms, gs, "tn", "x3"), tms, "nt", "x3")]
    return ds, (None if hints is None else [jnp.zeros_like(h) for h in hints])


_inv_unit_lower.defvjp(_inv_fwd, _inv_bwd)


def _delta_prep(qs, ks, vs, gs, bs, hints=None):
    c = qs[0].shape[0]
    nh = len(qs)
    ii = lax.broadcasted_iota(jnp.int32, (c, c), 0)
    jj = lax.broadcasted_iota(jnp.int32, (c, c), 1)
    incl = ii >= jj
    strict = ii > jj
    ltri = incl.astype(F32)
    eye = _eye(c)
    m1 = _dots([ltri] * nh, gs, "nn", "s3")
    gtot = [jnp.sum(g, axis=0, keepdims=True) for g in gs]
    decay = [jnp.exp(jnp.where(incl, m - m.T, NEG_BIG)) for m in m1]
    eg = [jnp.exp(m) for m in m1]
    kk = _dots(ks, ks, "nt", "bf16")
    lmats = [jnp.where(strict, b * x * d, 0.0) for b, x, d in zip(bs, kk, decay)]
    tinv = _inv_unit_lower(lmats, hints)
    toff = [t - eye for t in tinv]
    bv = [b * v for b, v in zip(bs, vs)]
    bk = [b * e * k for b, e, k in zip(bs, eg, ks)]
    u0 = [x + d for x, d in zip(bv, _dots(toff, bv, "nn", "bf16"))]
    wk = [x + d for x, d in zip(bk, _dots(toff, bk, "nn", "bf16"))]
    qsc = [q * (HEAD_DIM_A ** -0.5) for q in qs]
    qk = [x * d for x, d in zip(_dots(qsc, ks, "nt", "bf16"), decay)]
    q_dec = [q * e for q, e in zip(qsc, eg)]
    k_dec = [k * jnp.exp(t - m) for k, t, m in zip(ks, gtot, m1)]
    glast = [jnp.broadcast_to(jnp.exp(t), (c, c)) for t in gtot]
    return (u0, wk, qk, q_dec, k_dec, glast), tinv


def _delta_step(ss, u0, wk, qk, q_dec, k_dec, glast):
    us = [a - d for a, d in zip(u0, _dots(wk, ss, "nn", "bf16"))]
    os_ = [a + d for a, d in zip(_dots(q_dec, ss, "nn", "bf16"), _dots(qk, us, "nn", "bf16"))]
    s_new = [g * s + d for g, s, d in zip(glast, ss, _dots(k_dec, us, "tn", "bf16"))]
    return os_, s_new


HEADS_PER_STEP = 8


def _chunk_spec(nc, reverse=False):
    w = HEADS_PER_STEP * HEAD_DIM_A
    if reverse:
        return pl.BlockSpec((CHUNK, w), lambda h, n: (nc - 1 - n, h))
    return pl.BlockSpec((CHUNK, w), lambda h, n: (n, h))


def _head_slices():
    return [slice(j * HEAD_DIM_A, (j + 1) * HEAD_DIM_A) for j in range(HEADS_PER_STEP)]


def _heads(ref):
    return [ref[:, sl] for sl in _head_slices()]


def _delta_prep_fwd(q, k, v, gbc, bbc, *, name):
    t, d = q.shape
    nc = t // CHUNK

    def body(q_ref, k_ref, v_ref, g_ref, b_ref, *outs):
        res, tinv = _delta_prep(*[_heads(r) for r in (q_ref, k_ref, v_ref, g_ref, b_ref)])
        for ref, vals in zip(outs, res + (tinv,)):
            for sl, val in zip(_head_slices(), vals):
                ref[:, sl] = val

    spec = _chunk_spec(nc)
    return pl.pallas_call(
        body,
        name=name,
        grid=(N_HEADS_A // HEADS_PER_STEP, nc),
        in_specs=[spec] * 5,
        out_specs=[spec] * 7,
        out_shape=[jax.ShapeDtypeStruct((t, d), F32)] * 7,
        compiler_params=_params(("parallel", "parallel")),
    )(q, k, v, gbc, bbc)


def _delta_prep_bwd(q, k, v, gbc, bbc, tinv, cts, *, name):
    t, d = q.shape
    nc = t // CHUNK

    def body(q_ref, k_ref, v_ref, g_ref, b_ref, t_ref, c0, c1, c2, c3, c4, c5, *outs):
        def f(q_, k_, v_, g_, b_):
            return _delta_prep(q_, k_, v_, g_, b_, hints=_heads(t_ref))[0]

        _, vjp = jax.vjp(f, *[_heads(r) for r in (q_ref, k_ref, v_ref, g_ref, b_ref)])
        grads = vjp(tuple(_heads(c) for c in (c0, c1, c2, c3, c4, c5)))
        for ref, vals in zip(outs, grads):
            for sl, val in zip(_head_slices(), vals):
                ref[:, sl] = val

    spec = _chunk_spec(nc)
    return pl.pallas_call(
        body,
        name=name,
        grid=(N_HEADS_A // HEADS_PER_STEP, nc),
        in_specs=[spec] * 12,
        out_specs=[spec] * 5,
        out_shape=[jax.ShapeDtypeStruct((t, d), F32)] * 5,
        compiler_params=_params(("parallel", "parallel")),
    )(q, k, v, gbc, bbc, tinv, *cts)


def _delta_scan_fwd(prep, *, name):
    t, d = prep[0].shape
    nc = t // CHUNK

    def body(u0, wk, qk, qd, kd, gl, o_ref, st_ref, s_ref):
        n = pl.program_id(1)

        @pl.when(n == 0)
        def _():
            s_ref[...] = jnp.zeros(s_ref.shape, F32)

        ss = [s_ref[j] for j in range(HEADS_PER_STEP)]
        os_, s_new = _delta_step(ss, *[_heads(r) for r in (u0, wk, qk, qd, kd, gl)])
        for j, sl in enumerate(_head_slices()):
            st_ref[:, sl] = ss[j]
            o_ref[:, sl] = os_[j]
            s_ref[j] = s_new[j]

    spec = _chunk_spec(nc)
    return pl.pallas_call(
        body,
        name=name,
        grid=(N_HEADS_A // HEADS_PER_STEP, nc),
        in_specs=[spec] * 6,
        out_specs=[spec] * 2,
        out_shape=[jax.ShapeDtypeStruct((t, d), F32)] * 2,
        scratch_shapes=[pltpu.VMEM((HEADS_PER_STEP, HEAD_DIM_A, HEAD_DIM_A), F32)],
        compiler_params=_params(("parallel", "arbitrary")),
    )(*prep)


def _delta_scan_bwd(prep, states, do, *, name):
    t, d = do.shape
    nc = t // CHUNK

    def body(u0, wk, qk, qd, kd, gl, st_ref, do_ref, *rest):
        outs, ds_ref = rest[:6], rest[6]
        n = pl.program_id(1)

        @pl.when(n == 0)
        def _():
            ds_ref[...] = jnp.zeros(ds_ref.shape, F32)

        _, vjp = jax.vjp(_delta_step, *[_heads(r) for r in (st_ref, u0, wk, qk, qd, kd, gl)])
        grads = vjp((_heads(do_ref), [ds_ref[j] for j in range(HEADS_PER_STEP)]))
        for j, sl in enumerate(_head_slices()):
            ds_ref[j] = grads[0][j]
            for ref, vals in zip(outs, grads[1:]):
                ref[:, sl] = vals[j]

    spec = _chunk_spec(nc, reverse=True)
    return pl.pallas_call(
        body,
        name=name,
        grid=(N_HEADS_A // HEADS_PER_STEP, nc),
        in_specs=[spec] * 8,
        out_specs=[spec] * 6,
        out_shape=[jax.ShapeDtypeStruct((t, d), F32)] * 6,
        scratch_shapes=[pltpu.VMEM((HEADS_PER_STEP, HEAD_DIM_A, HEAD_DIM_A), F32)],
        compiler_params=_params(("parallel", "arbitrary")),
    )(*prep, states, do)


def _delta_fwd(q, k, v, gbc, bbc, proj, norm_w, *, name):
    assert HEADS_PER_STEP == N_HEADS_A
    t, d = q.shape
    nc = t // CHUNK

    def body(q_ref, k_ref, v_ref, g_ref, b_ref, z_ref, w_ref, o_ref, st_ref, t_ref, og_ref, s_ref):
        n = pl.program_id(0)

        @pl.when(n == 0)
        def _():
            s_ref[...] = jnp.zeros(s_ref.shape, F32)

        ss = [s_ref[j] for j in range(N_HEADS_A)]
        res, tinv = _delta_prep(*[_heads(r) for r in (q_ref, k_ref, v_ref, g_ref, b_ref)])
        os_, s_new = _delta_step(ss, *res)
        for j, sl in enumerate(_head_slices()):
            st_ref[:, sl] = ss[j]
            t_ref[:, sl] = tinv[j]
            o_ref[:, sl] = os_[j]
            s_ref[j] = s_new[j]
            zh = z_ref[:, sl]
            r = lax.rsqrt(jnp.mean(os_[j] * os_[j], axis=-1, keepdims=True) + EPS)
            og_ref[:, sl] = (os_[j] * r * w_ref[...] * (zh * _sigmoid(zh))).astype(BF16)

    spec = pl.BlockSpec((CHUNK, d), lambda n: (n, 0))
    f32 = jax.ShapeDtypeStruct((t, d), F32)
    return pl.pallas_call(
        body,
        name=name,
        grid=(nc,),
        in_specs=[spec] * 5 + [pl.BlockSpec((CHUNK, d), lambda n: (n, Z_COL_BLOCK)), pl.BlockSpec((1, HEAD_DIM_A), lambda n: (0, 0))],
        out_specs=[spec] * 4,
        out_shape=[f32, f32, f32, jax.ShapeDtypeStruct((t, d), BF16)],
        scratch_shapes=[pltpu.VMEM((N_HEADS_A, HEAD_DIM_A, HEAD_DIM_A), F32)],
        compiler_params=_params(("arbitrary",)),
    )(q, k, v, gbc, bbc, proj, norm_w)


def _delta_bwd(q, k, v, gbc, bbc, tinv, states, o, proj, norm_w, dog, *, name):
    t, d = q.shape
    nc = t // CHUNK

    def body(q_ref, k_ref, v_ref, g_ref, b_ref, t_ref, st_ref, o_ref, z_ref, w_ref, dog_ref,
             dq_ref, dk_ref, dv_ref, dg_ref, db_ref, dz_ref, dw_ref, ds_ref):
        n = pl.program_id(0)

        @pl.when(n == 0)
        def _():
            ds_ref[...] = jnp.zeros(ds_ref.shape, F32)

        dos = []
        dw = jnp.zeros((1, HEAD_DIM_A), F32)
        for sl in _head_slices():
            oh, zh, g = o_ref[:, sl], z_ref[:, sl], dog_ref[:, sl]
            r = lax.rsqrt(jnp.mean(oh * oh, axis=-1, keepdims=True) + EPS)
            nh = oh * r
            sg = _sigmoid(zh)
            dz_ref[:, sl] = (g * nh * w_ref[...] * (sg * (1.0 + zh * (1.0 - sg)))).astype(BF16)
            dt = g * (zh * sg)
            dw = dw + jnp.sum(dt * nh, axis=0, keepdims=True)
            dnh = dt * w_ref[...]
            dos.append(r * (dnh - nh * jnp.mean(dnh * nh, axis=-1, keepdims=True)))

        @pl.when(n == 0)
        def _():
            dw_ref[...] = dw

        @pl.when(n > 0)
        def _():
            dw_ref[...] += dw

        def chunk(qs, ks, vs, gs, bs, ss):
            return _delta_step(ss, *_delta_prep(qs, ks, vs, gs, bs, hints=_heads(t_ref))[0])

        _, vjp = jax.vjp(chunk, *[_heads(r) for r in (q_ref, k_ref, v_ref, g_ref, b_ref, st_ref)])
        grads = vjp((dos, [ds_ref[j] for j in range(N_HEADS_A)]))
        for j, sl in enumerate(_head_slices()):
            ds_ref[j] = grads[5][j]
            for ref, vals in zip((dq_ref, dk_ref, dv_ref, dg_ref, db_ref), grads[:5]):
                ref[:, sl] = vals[j]

    spec = pl.BlockSpec((CHUNK, d), lambda n: (nc - 1 - n, 0))
    vec = pl.BlockSpec((1, HEAD_DIM_A), lambda n: (0, 0))
    f32 = jax.ShapeDtypeStruct((t, d), F32)
    return pl.pallas_call(
        body,
        name=name,
        grid=(nc,),
        in_specs=[spec] * 8 + [pl.BlockSpec((CHUNK, d), lambda n: (nc - 1 - n, Z_COL_BLOCK)), vec, spec],
        out_specs=[spec] * 6 + [vec],
        out_shape=[f32] * 5 + [jax.ShapeDtypeStruct((t, d), BF16), jax.ShapeDtypeStruct((1, HEAD_DIM_A), F32)],
        scratch_shapes=[pltpu.VMEM((N_HEADS_A, HEAD_DIM_A, HEAD_DIM_A), F32)],
        compiler_params=_params(("arbitrary",)),
    )(q, k, v, gbc, bbc, tinv, states, o, proj, norm_w, dog)


def _alibi_slope(h):
    return 2.0 ** (-8.0 * (h + 1) / N_HEADS_B)


def _swa_load(sink_ref, q_ref, kp_ref, kc_ref, vp_ref, vc_ref):
    rg = lax.broadcasted_iota(jnp.int32, (GROUP_B * WINDOW, 1), 0) // WINDOW
    q4s, kcats, vcats, slopes, sinkcols = [], [], [], [], []
    for hk in range(N_KV_B):
        ks = slice(hk * HEAD_DIM_B, (hk + 1) * HEAD_DIM_B)
        heads = [hk * GROUP_B + g for g in range(GROUP_B)]
        q4s.append(jnp.concatenate([q_ref[:, h * HEAD_DIM_B:(h + 1) * HEAD_DIM_B] for h in heads], axis=0).astype(BF16))
        kcats.append(jnp.concatenate([kp_ref[:, ks], kc_ref[:, ks]], axis=0).astype(BF16))
        vcats.append(jnp.concatenate([vp_ref[:, ks], vc_ref[:, ks]], axis=0).astype(BF16))
        slope = jnp.zeros((GROUP_B * WINDOW, 1), F32)
        sink = jnp.zeros((GROUP_B * WINDOW, 1), F32)
        for g, h in enumerate(heads):
            slope = jnp.where(rg == g, _alibi_slope(h), slope)
            sink = jnp.where(rg == g, sink_ref[0, h], sink)
        slopes.append(slope)
        sinkcols.append(sink)
    return q4s, kcats, vcats, slopes, sinkcols


def _swa_probs(q4s, kcats, slopes, sinkcols, blk):
    rows = GROUP_B * WINDOW
    qi = lax.broadcasted_iota(jnp.int32, (rows, 2 * WINDOW), 0) % WINDOW
    kj = lax.broadcasted_iota(jnp.int32, (rows, 2 * WINDOW), 1)
    dist = qi + WINDOW - kj
    valid = (dist >= 0) & (dist < WINDOW) & (blk * WINDOW - WINDOW + kj >= 0)
    distf = dist.astype(F32)
    ss = [lax.dot_general(q, kc, _NT, preferred_element_type=F32) for q, kc in zip(q4s, kcats)]
    logits = [jnp.where(valid, s * (HEAD_DIM_B ** -0.5) - sl * distf, NEG_BIG) for s, sl in zip(ss, slopes)]
    ms = [jnp.maximum(jnp.max(l, axis=-1, keepdims=True), sk) for l, sk in zip(logits, sinkcols)]
    es = [jnp.exp(l - m) for l, m in zip(logits, ms)]
    esk = [jnp.exp(sk - m) for sk, m in zip(sinkcols, ms)]
    invs = [1.0 / (jnp.sum(e, axis=-1, keepdims=True) + k) for e, k in zip(es, esk)]
    return [e * i for e, i in zip(es, invs)], [k * i for k, i in zip(esk, invs)]


def _swa_fwd(proj, sinks, *, name):
    t = proj.shape[0]
    nb = t // WINDOW
    qd = N_HEADS_B * HEAD_DIM_B
    kd = N_KV_B * HEAD_DIM_B

    def body(sink_ref, q_ref, kp_ref, kc_ref, vp_ref, vc_ref, o_ref):
        blk = pl.program_id(0)
        q4s, kcats, vcats, slopes, sinkcols = _swa_load(sink_ref, q_ref, kp_ref, kc_ref, vp_ref, vc_ref)
        ps, _ = _swa_probs(q4s, kcats, slopes, sinkcols, blk)
        outs = [jnp.dot(p.astype(BF16), vc, preferred_element_type=F32) for p, vc in zip(ps, vcats)]
        for hk, out in enumerate(outs):
            for g in range(GROUP_B):
                h = hk * GROUP_B + g
                o_ref[:, h * HEAD_DIM_B:(h + 1) * HEAD_DIM_B] = out[g * WINDOW:(g + 1) * WINDOW, :].astype(BF16)

    q_spec = pl.BlockSpec((WINDOW, qd), lambda i: (i, 0))
    kv = lambda col, prev: pl.BlockSpec((WINDOW, kd), (lambda i: (jnp.maximum(i - 1, 0), col)) if prev else (lambda i: (i, col)))
    kcol, vcol = qd // kd, qd // kd + 1
    return pl.pallas_call(
        body,
        name=name,
        grid=(nb,),
        in_specs=[pl.BlockSpec(memory_space=pltpu.SMEM), q_spec, kv(kcol, True), kv(kcol, False), kv(vcol, True), kv(vcol, False)],
        out_specs=q_spec,
        out_shape=jax.ShapeDtypeStruct((t, qd), BF16),
        compiler_params=_params(("parallel",)),
    )(sinks, proj, proj, proj, proj, proj)


def _swa_bwd(proj, sinks, dout, *, name):
    t = proj.shape[0]
    nb = t // WINDOW
    qd = N_HEADS_B * HEAD_DIM_B
    kd = N_KV_B * HEAD_DIM_B
    scale = HEAD_DIM_B ** -0.5

    def body(sink_ref, q_ref, kp_ref, kc_ref, vp_ref, vc_ref, do_ref, dq_ref, dk_ref, dv_ref, dsk_ref):
        blk = pl.program_id(0)
        lane = lax.broadcasted_iota(jnp.int32, (1, LANES), 1)

        @pl.when(blk == 0)
        def _():
            dk_ref[...] = jnp.zeros((t, kd), F32)
            dv_ref[...] = jnp.zeros((t, kd), F32)
            dsk_ref[...] = jnp.zeros((1, LANES), F32)

        cur = pl.ds(pl.multiple_of(blk * WINDOW, WINDOW), WINDOW)
        prv = pl.ds(pl.multiple_of(jnp.maximum(blk - 1, 0) * WINDOW, WINDOW), WINDOW)
        q4s, kcats, vcats, slopes, sinkcols = _swa_load(sink_ref, q_ref, kp_ref, kc_ref, vp_ref, vc_ref)
        ps, psinks = _swa_probs(q4s, kcats, slopes, sinkcols, blk)
        do4s = [jnp.concatenate([do_ref[:, (hk * GROUP_B + g) * HEAD_DIM_B:(hk * GROUP_B + g + 1) * HEAD_DIM_B]
                                 for g in range(GROUP_B)], axis=0).astype(BF16) for hk in range(N_KV_B)]
        dps = [lax.dot_general(d, vc, _NT, preferred_element_type=F32) for d, vc in zip(do4s, vcats)]
        deltas = [jnp.sum(p * dp, axis=-1, keepdims=True) for p, dp in zip(ps, dps)]
        dss = [(p * (dp - dl) * scale).astype(BF16) for p, dp, dl in zip(ps, dps, deltas)]
        dq4s = [jnp.dot(ds, kc, preferred_element_type=F32) for ds, kc in zip(dss, kcats)]
        dkcs = [lax.dot_general(ds, q, _TN, preferred_element_type=F32) for ds, q in zip(dss, q4s)]
        dvcs = [lax.dot_general(p.astype(BF16), d, _TN, preferred_element_type=F32) for p, d in zip(ps, do4s)]
        dsk = jnp.zeros((1, LANES), F32)
        for hk in range(N_KV_B):
            ks = slice(hk * HEAD_DIM_B, (hk + 1) * HEAD_DIM_B)
            dsink = -psinks[hk] * deltas[hk]
            for g in range(GROUP_B):
                h = hk * GROUP_B + g
                rows = slice(g * WINDOW, (g + 1) * WINDOW)
                dq_ref[:, h * HEAD_DIM_B:(h + 1) * HEAD_DIM_B] = dq4s[hk][rows, :].astype(BF16)
                dsk = dsk + jnp.where(lane == h, jnp.sum(dsink[rows, :], axis=0, keepdims=True), 0.0)
            dk_ref[cur, ks] += dkcs[hk][WINDOW:, :]
            dv_ref[cur, ks] += dvcs[hk][WINDOW:, :]

            @pl.when(blk > 0)
            def _():
                dk_ref[prv, ks] += dkcs[hk][:WINDOW, :]
                dv_ref[prv, ks] += dvcs[hk][:WINDOW, :]

        dsk_ref[...] += dsk

    q_spec = pl.BlockSpec((WINDOW, qd), lambda i: (i, 0))
    kv = lambda col, prev: pl.BlockSpec((WINDOW, kd), (lambda i: (jnp.maximum(i - 1, 0), col)) if prev else (lambda i: (i, col)))
    kcol, vcol = qd // kd, qd // kd + 1
    full = pl.BlockSpec((t, kd), lambda i: (0, 0))
    return pl.pallas_call(
        body,
        name=name,
        grid=(nb,),
        in_specs=[pl.BlockSpec(memory_space=pltpu.SMEM), q_spec, kv(kcol, True), kv(kcol, False), kv(vcol, True), kv(vcol, False), q_spec],
        out_specs=[q_spec, full, full, pl.BlockSpec((1, LANES), lambda i: (0, 0))],
        out_shape=[jax.ShapeDtypeStruct((t, qd), BF16), jax.ShapeDtypeStruct((t, kd), F32),
                   jax.ShapeDtypeStruct((t, kd), F32), jax.ShapeDtypeStruct((1, LANES), F32)],
        compiler_params=_params(("arbitrary",)),
    )(sinks, proj, proj, proj, proj, proj, dout)


def _ffn_act_fwd(up, cw, *, name, tm=512, cb=256):
    _, t, f = up.shape
    tm, cb = _tile(t, tm), _tile(f, cb)

    rc = min(ROW_CHUNK, tm)

    def body(ug_ref, uv_ref, cg_ref, cv_ref, a_ref, hg, hv):
        i = pl.program_id(1)

        @pl.when(i == 0)
        def _():
            hg[0:HALO, :] = jnp.zeros((HALO, cb), F32)
            hv[0:HALO, :] = jnp.zeros((HALO, cb), F32)

        hg[pl.ds(HALO, rc), :] = ug_ref[0:rc, :]
        hv[pl.ds(HALO, rc), :] = uv_ref[0:rc, :]
        for r in range(tm // rc):
            if r == 0:
                yg = _conv_rows(hg, HALO, rc, cg_ref, FFN_CONV)
                yv = _conv_rows(hv, HALO, rc, cv_ref, FFN_CONV)
            else:
                yg = _conv_rows(ug_ref, r * rc, rc, cg_ref, FFN_CONV)
                yv = _conv_rows(uv_ref, r * rc, rc, cv_ref, FFN_CONV)
            a_ref[r * rc:(r + 1) * rc, :] = (yg * _sigmoid(yg) * yv).astype(BF16)
        hg[0:HALO, :] = ug_ref[tm - HALO:tm, :]
        hv[0:HALO, :] = uv_ref[tm - HALO:tm, :]

    ncb = f // cb
    half = lambda s: pl.BlockSpec((None, tm, cb), lambda c, i: (s, i, c))
    taps = lambda s: pl.BlockSpec((FFN_CONV, cb), lambda c, i: (0, c + s * ncb))
    return pl.pallas_call(
        body,
        name=name,
        grid=(ncb, t // tm),
        in_specs=[half(0), half(1), taps(0), taps(1)],
        out_specs=pl.BlockSpec((tm, cb), lambda c, i: (i, c)),
        out_shape=jax.ShapeDtypeStruct((t, f), BF16),
        scratch_shapes=[pltpu.VMEM((HALO + rc, cb), F32)] * 2,
        compiler_params=_params(("parallel", "arbitrary")),
    )(up, up, cw, cw)


def _ffn_act_bwd(up, cw, dact, *, name, tm=512, cb=256):
    _, t, f = up.shape
    tm, cb = _tile(t, tm), _tile(f, cb)
    nt = t // tm
    hb = tm // HALO

    rc = min(ROW_CHUNK, tm)
    nr = tm // rc
    kw = FFN_CONV

    def body(ug_ref, uv_ref, pg_ref, pv_ref, cg_ref, cv_ref, da_ref, du_ref, dcg_ref, dcv_ref,
             hg, hv, dyg, dyv):
        i = pl.program_id(1)
        first = i == 0
        tile = nt - 1 - i

        @pl.when(tile == 0)
        def _():
            hg[0:HALO, :] = jnp.zeros((HALO, cb), F32)
            hv[0:HALO, :] = jnp.zeros((HALO, cb), F32)

        @pl.when(tile > 0)
        def _():
            hg[0:HALO, :] = pg_ref[...]
            hv[0:HALO, :] = pv_ref[...]

        @pl.when(first)
        def _():
            dyg[pl.ds(tm, HALO), :] = jnp.zeros((HALO, cb), F32)
            dyv[pl.ds(tm, HALO), :] = jnp.zeros((HALO, cb), F32)

        hg[pl.ds(HALO, rc), :] = ug_ref[0:rc, :]
        hv[pl.ds(HALO, rc), :] = uv_ref[0:rc, :]
        dcg = [jnp.zeros((1, cb), F32) for _ in range(kw)]
        dcv = [jnp.zeros((1, cb), F32) for _ in range(kw)]
        for r in reversed(range(nr)):
            rows = slice(r * rc, (r + 1) * rc)
            src_g, src_v, base = (hg, hv, HALO) if r == 0 else (ug_ref, uv_ref, r * rc)
            yg = _conv_rows(src_g, base, rc, cg_ref, kw)
            yv = _conv_rows(src_v, base, rc, cv_ref, kw)
            sg = _sigmoid(yg)
            da = da_ref[rows, :]
            dy_g = da * yv * (sg * (1.0 + yg * (1.0 - sg)))
            dy_v = da * (yg * sg)
            dyg[rows, :] = dy_g
            dyv[rows, :] = dy_v
            du_ref[0, rows, :] = _conv_t_rows(dyg, r * rc, rc, cg_ref, kw).astype(BF16)
            du_ref[1, rows, :] = _conv_t_rows(dyv, r * rc, rc, cv_ref, kw).astype(BF16)
            for j in range(kw):
                dcg[j] = dcg[j] + jnp.sum(dy_g * src_g[pl.ds(base - (kw - 1) + j, rc), :], axis=0, keepdims=True)
                dcv[j] = dcv[j] + jnp.sum(dy_v * src_v[pl.ds(base - (kw - 1) + j, rc), :], axis=0, keepdims=True)
        dyg[pl.ds(tm, HALO), :] = dyg[0:HALO, :]
        dyv[pl.ds(tm, HALO), :] = dyv[0:HALO, :]
        for j in range(kw):
            @pl.when(first)
            def _():
                dcg_ref[j:j + 1, :] = dcg[j]
                dcv_ref[j:j + 1, :] = dcv[j]

            @pl.when(jnp.logical_not(first))
            def _():
                dcg_ref[j:j + 1, :] += dcg[j]
                dcv_ref[j:j + 1, :] += dcv[j]

    ncb = f // cb
    half = lambda s: pl.BlockSpec((None, tm, cb), lambda c, i: (s, nt - 1 - i, c))
    prev = lambda s: pl.BlockSpec((None, HALO, cb), lambda c, i: (s, jnp.maximum((nt - 1 - i) * hb - 1, 0), c))
    taps = lambda s: pl.BlockSpec((FFN_CONV, cb), lambda c, i: (0, c + s * ncb))
    dtaps = pl.BlockSpec((FFN_CONV, cb), lambda c, i: (0, c))
    return pl.pallas_call(
        body,
        name=name,
        grid=(ncb, nt),
        in_specs=[half(0), half(1), prev(0), prev(1), taps(0), taps(1), pl.BlockSpec((tm, cb), lambda c, i: (nt - 1 - i, c))],
        out_specs=[pl.BlockSpec((2, tm, cb), lambda c, i: (0, nt - 1 - i, c)), dtaps, dtaps],
        out_shape=[jax.ShapeDtypeStruct((2, t, f), BF16), jax.ShapeDtypeStruct((FFN_CONV, f), F32),
                   jax.ShapeDtypeStruct((FFN_CONV, f), F32)],
        scratch_shapes=[pltpu.VMEM((HALO + rc, cb), F32)] * 2 + [pltpu.VMEM((tm + HALO, cb), F32)] * 2,
        compiler_params=_params(("parallel", "arbitrary")),
    )(up, up, up, up, cw, cw, dact)


FFN_COL_TILE = 1408
FFN_SUB = 512
FFN_ROW_CHUNK = 16


def _sub_blocks(width):
    return [slice(c, min(c + FFN_SUB, width)) for c in range(0, width, FFN_SUB)]


def _ffn_up_act(n_f, w_up_t, cw, *, name, tm=512):
    t, d = n_f.shape
    f = w_up_t.shape[1]
    tm, tn = _tile(t, tm), _tile(f, FFN_COL_TILE)
    nj = f // tn
    rc = min(FFN_ROW_CHUNK, tm)
    kw = FFN_CONV

    def body(n_ref, wg_ref, wv_ref, cg_ref, cv_ref, up_ref, y_ref, a_ref, hg, hv):
        i = pl.program_id(1)

        @pl.when(i == 0)
        def _():
            hg[0:HALO, :] = jnp.zeros((HALO, tn), F32)
            hv[0:HALO, :] = jnp.zeros((HALO, tn), F32)

        def products(cs):
            up_ref[0, :, cs] = lax.dot_general(n_ref[...], wg_ref[cs, :], _NT, preferred_element_type=F32)
            up_ref[1, :, cs] = lax.dot_general(n_ref[...], wv_ref[cs, :], _NT, preferred_element_type=F32)

        subs = _sub_blocks(tn)
        ug, uv = up_ref.at[0], up_ref.at[1]
        products(subs[0])
        for ci, cs in enumerate(subs):
            if ci + 1 < len(subs):
                products(subs[ci + 1])
            hg[pl.ds(HALO, rc), cs] = ug[0:rc, cs]
            hv[pl.ds(HALO, rc), cs] = uv[0:rc, cs]
            for r in range(tm // rc):
                src_g, src_v, base = (hg, hv, HALO) if r == 0 else (ug, uv, r * rc)
                yg = _conv_rows(src_g, base, rc, cg_ref, kw, cs)
                yv = _conv_rows(src_v, base, rc, cv_ref, kw, cs)
                y_ref[0, r * rc:(r + 1) * rc, cs] = yg
                y_ref[1, r * rc:(r + 1) * rc, cs] = yv
                a_ref[r * rc:(r + 1) * rc, cs] = (yg * _sigmoid(yg) * yv).astype(BF16)
            hg[0:HALO, cs] = ug[tm - HALO:tm, cs]
            hv[0:HALO, cs] = uv[tm - HALO:tm, cs]

    half = lambda s: pl.BlockSpec((None, tn, d), lambda j, i: (s, j, 0))
    taps = lambda s: pl.BlockSpec((kw, tn), lambda j, i: (0, j + s * nj))
    pair = pl.BlockSpec((2, tm, tn), lambda j, i: (0, i, j))
    return pl.pallas_call(
        body,
        name=name,
        grid=(nj, t // tm),
        in_specs=[pl.BlockSpec((tm, d), lambda j, i: (i, 0)), half(0), half(1), taps(0), taps(1)],
        out_specs=[pair, pair, pl.BlockSpec((tm, tn), lambda j, i: (i, j))],
        out_shape=[jax.ShapeDtypeStruct((2, t, f), F32), jax.ShapeDtypeStruct((2, t, f), F32),
                   jax.ShapeDtypeStruct((t, f), BF16)],
        scratch_shapes=[pltpu.VMEM((HALO + rc, tn), F32)] * 2,
        compiler_params=_params(("parallel", "arbitrary")),
    )(n_f, w_up_t, w_up_t, cw, cw)


def _ffn_down_dx_act_bwd(dh, w_down, up, y, cw, *, name, tm=512):
    t, d = dh.shape
    f = w_down.shape[0]
    tm, tn = _tile(t, tm), _tile(f, FFN_COL_TILE)
    nj, nt = f // tn, t // tm
    rc = min(FFN_ROW_CHUNK, tm)
    nr = tm // rc
    kw = FFN_CONV

    def body(dh_ref, wd_ref, ug_ref, uv_ref, yg_ref, yv_ref, cg_ref, cv_ref, du_ref, dcg_ref, dcv_ref,
             dyg, dyv, da_s, dh_s):
        i = pl.program_id(1)
        first = i == 0

        @pl.when(first)
        def _():
            dyg[pl.ds(tm, HALO), :] = jnp.zeros((HALO, tn), F32)
            dyv[pl.ds(tm, HALO), :] = jnp.zeros((HALO, tn), F32)

        dh_s[...] = dh_ref[...].astype(BF16)

        def product(cs):
            da_s[:, cs] = lax.dot_general(dh_s[...], wd_ref[cs, :], _NT, preferred_element_type=F32)

        subs = _sub_blocks(tn)
        product(subs[0])
        for ci, cs in enumerate(subs):
            width = cs.stop - cs.start
            if ci + 1 < len(subs):
                product(subs[ci + 1])
            dcg = [jnp.zeros((HALO, width), F32) for _ in range(kw)]
            dcv = [jnp.zeros((HALO, width), F32) for _ in range(kw)]
            for r in reversed(range(nr)):
                rows = slice(r * rc, (r + 1) * rc)
                yg, yv = yg_ref[rows, cs], yv_ref[rows, cs]
                sg = _sigmoid(yg)
                da = da_s[rows, cs]
                dyg[rows, cs] = da * yv * (sg * (1.0 + yg * (1.0 - sg)))
                dyv[rows, cs] = da * (yg * sg)
                ahead_g = _ahead_rows(dyg, r * rc, rc, range(kw), cs)
                ahead_v = _ahead_rows(dyv, r * rc, rc, range(kw), cs)
                du_ref[0, rows, cs] = sum(cg_ref[kw - 1 - s:kw - s, cs] * ahead_g[s] for s in range(kw)).astype(BF16)
                du_ref[1, rows, cs] = sum(cv_ref[kw - 1 - s:kw - s, cs] * ahead_v[s] for s in range(kw)).astype(BF16)
                xg, xv = ug_ref[rows, cs], uv_ref[rows, cs]
                for s in range(kw):
                    dcg[kw - 1 - s] = dcg[kw - 1 - s] + _fold_rows(xg * ahead_g[s])
                    dcv[kw - 1 - s] = dcv[kw - 1 - s] + _fold_rows(xv * ahead_v[s])
            dyg[pl.ds(tm, HALO), cs] = dyg[0:HALO, cs]
            dyv[pl.ds(tm, HALO), cs] = dyv[0:HALO, cs]
            for j in range(kw):
                tg = jnp.sum(dcg[j], axis=0, keepdims=True)
                tv = jnp.sum(dcv[j], axis=0, keepdims=True)

                @pl.when(first)
                def _():
                    dcg_ref[j:j + 1, cs] = tg
                    dcv_ref[j:j + 1, cs] = tv

                @pl.when(jnp.logical_not(first))
                def _():
                    dcg_ref[j:j + 1, cs] += tg
                    dcv_ref[j:j + 1, cs] += tv

    half = lambda s: pl.BlockSpec((None, tm, tn), lambda j, i: (s, nt - 1 - i, j))
    taps = lambda s: pl.BlockSpec((kw, tn), lambda j, i: (0, j + s * nj))
    dtaps = pl.BlockSpec((kw, tn), lambda j, i: (0, j))
    return pl.pallas_call(
        body,
        name=name,
        grid=(nj, nt),
        in_specs=[pl.BlockSpec((tm, d), lambda j, i: (nt - 1 - i, 0)), pl.BlockSpec((tn, d), lambda j, i: (j, 0)),
                  half(0), half(1), half(0), half(1), taps(0), taps(1)],
        out_specs=[pl.BlockSpec((2, tm, tn), lambda j, i: (0, nt - 1 - i, j)), dtaps, dtaps],
        out_shape=[jax.ShapeDtypeStruct((2, t, f), BF16), jax.ShapeDtypeStruct((kw, f), F32),
                   jax.ShapeDtypeStruct((kw, f), F32)],
        scratch_shapes=[pltpu.VMEM((tm + HALO, tn), F32)] * 2 + [pltpu.VMEM((tm, tn), F32), pltpu.VMEM((tm, d), BF16)],
        compiler_params=_params(("parallel", "arbitrary")),
    )(dh, w_down, up, up, y, y, cw, cw)


def _ple_fwd(h, zg, pe, *, name, tm=512):
    t, d = h.shape
    tm = _tile(t, tm)

    def body(h_ref, z_ref, p_ref, o_ref):
        o_ref[...] = h_ref[...] + _sigmoid(z_ref[...]) * p_ref[...]

    row = pl.BlockSpec((tm, d), lambda i: (i, 0))
    return pl.pallas_call(
        body, name=name, grid=(t // tm,), in_specs=[row] * 3, out_specs=row,
        out_shape=jax.ShapeDtypeStruct((t, d), F32), compiler_params=_params(("parallel",)),
    )(h, zg, pe)


def _ple_bwd(dh, zg, pe, *, name, tm=512):
    t, d = dh.shape
    tm = _tile(t, tm)

    def body(g_ref, z_ref, p_ref, dz_ref, dp_ref):
        g = g_ref[...]
        sg = _sigmoid(z_ref[...])
        dz_ref[...] = (g * p_ref[...] * sg * (1.0 - sg)).astype(BF16)
        dp_ref[...] = (g * sg).astype(BF16)

    row = pl.BlockSpec((tm, d), lambda i: (i, 0))
    return pl.pallas_call(
        body, name=name, grid=(t // tm,), in_specs=[row] * 3, out_specs=[row] * 2,
        out_shape=[jax.ShapeDtypeStruct((t, d), BF16)] * 2, compiler_params=_params(("parallel",)),
    )(dh, zg, pe)


def _my_pos():
    return lax.axis_index("x"), lax.axis_index("y"), lax.axis_index("c")


def _all_gather(block, *, name, dep=None):
    r, w = block.shape
    has_dep = dep is not None

    def body(*refs):
        x_ref, out_ref, send_sems, recv_sems, local_sem = refs[:1] + refs[1 + has_dep:]
        x, y, c = _my_pos()
        me, sibling = (x, y, c), (x, y, 1 - c)
        chips = [(1 - x, y), (x, 1 - y), (1 - x, 1 - y)]

        def slot(px, py, pc):
            return out_ref.at[4 * px + 2 * py + pc]

        def copy(k, blk, to, src=None):
            return pltpu.make_async_remote_copy(
                src_ref=slot(*blk) if src is None else src, dst_ref=slot(*blk),
                send_sem=send_sems.at[k], recv_sem=recv_sems.at[k],
                device_id=to, device_id_type=pl.DeviceIdType.MESH)

        mine = pltpu.make_async_copy(x_ref, slot(*me), local_sem)
        mine.start()
        first = [copy(0, me, sibling, src=x_ref)]
        first += [copy(1 + j, me, (*chip, c), src=x_ref) for j, chip in enumerate(chips)]
        for cp in first:
            cp.start()
        passed = [copy(4 + j, (*chip, c), sibling) for j, chip in enumerate(chips)]
        for j, chip in enumerate(chips):
            copy(1 + j, (*chip, c), me).wait_recv()
            passed[j].start()
        copy(0, sibling, me).wait_recv()
        for j, chip in enumerate(chips):
            copy(4 + j, (*chip, 1 - c), me).wait_recv()
        for cp in first + passed:
            cp.wait_send()
        mine.wait()

    return pl.pallas_call(
        body,
        name=name,
        out_shape=jax.ShapeDtypeStruct((N_DEV, r, w), block.dtype),
        in_specs=[pl.BlockSpec(memory_space=pl.ANY)] * (1 + has_dep),
        out_specs=pl.BlockSpec(memory_space=pl.ANY),
        scratch_shapes=[pltpu.SemaphoreType.DMA((7,)), pltpu.SemaphoreType.DMA((7,)), pltpu.SemaphoreType.DMA],
    )(*((block, dep) if has_dep else (block,)))


def _all_to_all(slabs, *, name):
    n, r, w = slabs.shape

    def body(x_ref, out_ref, send_sems, recv_sems, local_sem):
        x, y, c = _my_pos()
        my_idx = 4 * x + 2 * y + c
        mine = pltpu.make_async_copy(x_ref.at[my_idx], out_ref.at[my_idx], local_sem)
        mine.start()
        copies = []
        for k in range(1, N_DEV):
            fx, fy, fc = (k >> 2) & 1, (k >> 1) & 1, k & 1
            px = (1 - x) if fx else x
            py = (1 - y) if fy else y
            pc = (1 - c) if fc else c
            cp = pltpu.make_async_remote_copy(
                src_ref=x_ref.at[4 * px + 2 * py + pc], dst_ref=out_ref.at[my_idx],
                send_sem=send_sems.at[k - 1], recv_sem=recv_sems.at[k - 1],
                device_id=(px, py, pc), device_id_type=pl.DeviceIdType.MESH)
            cp.start()
            copies.append(cp)
        for cp in copies:
            cp.wait_recv()
        for cp in copies:
            cp.wait_send()
        mine.wait()

    return pl.pallas_call(
        body,
        name=name,
        out_shape=jax.ShapeDtypeStruct((n, r, w), slabs.dtype),
        in_specs=[pl.BlockSpec(memory_space=pl.ANY)],
        out_specs=pl.BlockSpec(memory_space=pl.ANY),
        scratch_shapes=[pltpu.SemaphoreType.DMA((7,)), pltpu.SemaphoreType.DMA((7,)), pltpu.SemaphoreType.DMA],
    )(slabs)


def _exchange_copies(scatter, src_refs, land_refs, send_sems, recv_sems, local_sems):
    x, y, c = _my_pos()
    me = 4 * x + 2 * y + c
    local, remote = [], []
    for i, (s, l) in enumerate(zip(src_refs, land_refs)):
        local.append(pltpu.make_async_copy(s.at[me] if scatter else s, l.at[me], local_sems.at[i]))
        for k in range(1, N_DEV):
            px = (1 - x) if (k >> 2) & 1 else x
            py = (1 - y) if (k >> 1) & 1 else y
            pc = (1 - c) if k & 1 else c
            remote.append(pltpu.make_async_remote_copy(
                src_ref=s.at[4 * px + 2 * py + pc] if scatter else s, dst_ref=l.at[me],
                send_sem=send_sems.at[(N_DEV - 1) * i + k - 1], recv_sem=recv_sems.at[(N_DEV - 1) * i + k - 1],
                device_id=(px, py, pc), device_id_type=pl.DeviceIdType.MESH))
    return local, remote


def _exchange(arrays, *, scatter, name):
    n = len(arrays)

    def body(*refs):
        srcs, lands = refs[:n], refs[n:2 * n]
        local, remote = _exchange_copies(scatter, srcs, lands, *refs[2 * n:])
        for cp in local + remote:
            cp.start()
        for cp in remote:
            cp.wait_recv()
        for cp in remote:
            cp.wait_send()
        for cp in local:
            cp.wait()

    hbm = pl.BlockSpec(memory_space=pl.ANY)
    out = pl.pallas_call(
        body,
        name=name,
        out_shape=[jax.ShapeDtypeStruct(a.shape if scatter else (N_DEV,) + a.shape, a.dtype) for a in arrays],
        in_specs=[hbm] * n,
        out_specs=[hbm] * n,
        scratch_shapes=[pltpu.SemaphoreType.DMA(((N_DEV - 1) * n,)), pltpu.SemaphoreType.DMA(((N_DEV - 1) * n,)),
                        pltpu.SemaphoreType.DMA((n,))],
    )(*arrays)
    return list(out)


_HBM_SPEC = pl.BlockSpec(memory_space=pltpu.HBM)
_SEM_SPEC = pl.BlockSpec(memory_space=pltpu.SEMAPHORE)
_EFFECT = pltpu.SideEffectType.DATAFLOW_SIDE_EFFECTING


def _exchange_start(arrays, *, scatter, name, dep):
    n = len(arrays)
    srcs = [pltpu.with_memory_space_constraint(a, pltpu.HBM) for a in arrays]
    lands = [pltpu.with_memory_space_constraint(lax.empty(a.shape if scatter else (N_DEV,) + a.shape, a.dtype), pltpu.HBM)
             for a in arrays]

    def body(*refs):
        src_refs, land_refs = refs[:n], refs[n:2 * n]
        send_sems, recv_sems, local_sems = refs[2 * n + 1:2 * n + 4]
        token = refs[-1]
        local, remote = _exchange_copies(scatter, src_refs, land_refs, send_sems, recv_sems, local_sems)
        for cp in local + remote:
            cp.start()
        token[...] = jnp.zeros_like(token)

    sems = (pltpu.SemaphoreType.DMA(((N_DEV - 1) * n,)), pltpu.SemaphoreType.DMA(((N_DEV - 1) * n,)),
            pltpu.SemaphoreType.DMA((n,)))
    out = pl.pallas_call(
        body,
        name=name,
        out_shape=sems + tuple(pltpu.HBM(a.shape, a.dtype) for a in srcs + lands) + (jax.ShapeDtypeStruct((8, LANES), F32),),
        in_specs=[_HBM_SPEC] * (2 * n) + [pl.BlockSpec(memory_space=pl.ANY)],
        out_specs=(_SEM_SPEC,) * 3 + (_HBM_SPEC,) * (2 * n) + (pl.BlockSpec(memory_space=pltpu.VMEM),),
        input_output_aliases={i: 3 + i for i in range(2 * n)},
        compiler_params=pltpu.CompilerParams(has_side_effects=_EFFECT),
    )(*srcs, *lands, dep)
    return (out[:3], list(out[3:3 + n]), list(out[3 + n:3 + 2 * n])), out[-1]


def _exchange_wait(handle, after, *, scatter, name):
    sems, srcs, lands = handle
    n = len(srcs)

    def body(*refs):
        src_refs, land_refs = refs[:n], refs[n:2 * n]
        send_sems, recv_sems, local_sems = refs[2 * n:2 * n + 3]
        local, remote = _exchange_copies(scatter, src_refs, land_refs, send_sems, recv_sems, local_sems)
        for cp in remote:
            cp.wait_send()
            cp.wait_recv()
        for cp in local:
            cp.wait()

    out = pl.pallas_call(
        body,
        name=name,
        out_shape=tuple(pltpu.HBM(a.shape, a.dtype) for a in srcs + lands),
        in_specs=[_HBM_SPEC] * (2 * n) + [_SEM_SPEC] * 3 + [pl.BlockSpec(memory_space=pl.ANY)],
        out_specs=(_HBM_SPEC,) * (2 * n),
        input_output_aliases={i: i for i in range(2 * n)},
        compiler_params=pltpu.CompilerParams(has_side_effects=_EFFECT),
    )(*srcs, *lands, *sems, after)
    return list(out[n:])


def _sum_parts(parts, *, name, tr=512):
    n, r, lanes = parts.shape
    tr = tr if (r % tr == 0 and r > 1024) else r

    def body(p_ref, g_ref):
        g = p_ref[0].astype(F32)
        for j in range(1, n):
            g = g + p_ref[j].astype(F32)
        g_ref[...] = g

    row = pl.BlockSpec((tr, lanes), lambda i: (i, 0))
    return pl.pallas_call(
        body,
        name=name,
        grid=(r // tr,),
        in_specs=[pl.BlockSpec((n, tr, lanes), lambda i: (0, i, 0))],
        out_specs=row,
        out_shape=jax.ShapeDtypeStruct((r, lanes), F32),
        compiler_params=_params(("parallel",)),
    )(parts)


def _adamw_update(g, w, m, v):
    c1 = 1.0 / (1.0 - ADAM_B1 ** ADAM_STEP)
    c2 = 1.0 / (1.0 - ADAM_B2 ** ADAM_STEP)
    nm = ADAM_B1 * m + (1.0 - ADAM_B1) * g
    nv = ADAM_B2 * v + (1.0 - ADAM_B2) * (g * g)
    return -ADAM_LR * ((nm * c1) / (jnp.sqrt(nv * c2) + ADAM_EPS) + ADAM_WD * w), nm, nv


def _adamw_layer(g, w, m, v, layer, prev, *, name):
    nl, k, n = w.shape
    tr = max([d for d in range(8, min(k, 256) + 1, 8) if k % d == 0] or [k])
    in_parts = g.ndim == 3

    def body(g_ref, w_ref, m_ref, v_ref, *rest):
        go_ref, d_ref, nm_ref, nv_ref = rest[-4:]
        if in_parts:
            gg = g_ref[0].astype(F32)
            for j in range(1, g_ref.shape[0]):
                gg = gg + g_ref[j].astype(F32)
        else:
            gg = g_ref[...]
        d, nm, nv = _adamw_update(gg, w_ref[...], m_ref[...], v_ref[...])
        go_ref[...] = gg
        d_ref[...] = d
        nm_ref[...] = nm
        nv_ref[...] = nv

    lay = pl.BlockSpec((None, tr, n), lambda i: (layer, i, 0))
    n_prev = 0 if prev is None else 4
    out = jax.ShapeDtypeStruct((nl, k, n), F32)
    return pl.pallas_call(
        body,
        name=name,
        grid=(k // tr,),
        in_specs=[pl.BlockSpec((g.shape[0], tr, n), lambda i: (0, i, 0)) if in_parts else pl.BlockSpec((tr, n), lambda i: (i, 0)),
                  lay, lay, lay] + [pl.BlockSpec(memory_space=pl.ANY)] * n_prev,
        out_specs=[lay] * 4,
        out_shape=[out] * 4,
        input_output_aliases={4 + j: j for j in range(n_prev)},
        compiler_params=_params(("parallel",)),
    )(g, w, m, v, *(prev or ()))


def _adamw_packed(g, w, m, v, *, name, tr=512):
    r, lanes = g.shape
    tr = tr if r % tr == 0 else r
    c1 = 1.0 / (1.0 - ADAM_B1 ** ADAM_STEP)
    c2 = 1.0 / (1.0 - ADAM_B2 ** ADAM_STEP)

    def body(g_ref, w_ref, m_ref, v_ref, d_ref, nm_ref, nv_ref):
        g = g_ref[...]
        nm = ADAM_B1 * m_ref[...] + (1.0 - ADAM_B1) * g
        nv = ADAM_B2 * v_ref[...] + (1.0 - ADAM_B2) * (g * g)
        nm_ref[...] = nm
        nv_ref[...] = nv
        d_ref[...] = -ADAM_LR * ((nm * c1) / (jnp.sqrt(nv * c2) + ADAM_EPS) + ADAM_WD * w_ref[...])

    row = pl.BlockSpec((tr, lanes), lambda i: (i, 0))
    out = jax.ShapeDtypeStruct((r, lanes), F32)
    return pl.pallas_call(
        body,
        name=name,
        grid=(r // tr,),
        in_specs=[row] * 4,
        out_specs=[row] * 3,
        out_shape=[out] * 3,
        compiler_params=_params(("parallel",)),
    )(g, w, m, v)


BIG = ("a_w_in", "a_w_out", "b_w_in", "b_w_out", "f_w_up", "f_w_down", "ple_w_proj", "ple_w_gate")
CONVS = ("a_conv", "f_conv")
SMALL = ("norm_mix", "norm_ffn", "norm_ple", "norm_final", "a_log", "a_dt_bias", "a_norm", "b_sinks")
WEIGHTS = ("norm_mix", "norm_ffn", "norm_ple", "norm_final", "a_w_in", "a_conv", "a_log", "a_dt_bias", "a_norm",
           "a_w_out", "b_w_in", "b_sinks", "b_w_out", "f_w_up", "f_conv", "f_w_down", "ple_w_proj", "ple_w_gate")
SLAB_ROW_MULTIPLE = 512


def _pack(arrs, dtype, row_multiple):
    flat = jnp.concatenate([a.reshape(-1).astype(dtype) for a in arrs])
    rows = -(-flat.shape[0] // LANES)
    rows = -(-rows // row_multiple) * row_multiple
    return jnp.pad(flat, (0, rows * LANES - flat.shape[0])).reshape(rows, LANES)


def _unpack(slab, shapes):
    lead = slab.shape[:-2]
    flat = slab.reshape(lead + (-1,))
    out, off = [], 0
    for s in shapes:
        size = math.prod(s)
        out.append(flat[..., off:off + size].reshape(lead + tuple(s)))
        off += size
    return out


def _cols_full(g):
    g = jnp.moveaxis(g, 0, -2)
    return g.reshape(g.shape[:-2] + (g.shape[-2] * g.shape[-1],))


def _rows_full(g):
    g = jnp.moveaxis(g, 0, -3)
    return g.reshape(g.shape[:-3] + (g.shape[-3] * g.shape[-2], g.shape[-1]))


def _cols_split(wfull):
    n = wfull.shape[-1] // N_DEV
    g = wfull.reshape(wfull.shape[:-1] + (N_DEV, n))
    return jnp.moveaxis(g, -2, 0)


def _rows_split(wfull):
    k = wfull.shape[-2] // N_DEV
    g = wfull.reshape(wfull.shape[:-2] + (N_DEV, k, wfull.shape[-1]))
    return jnp.moveaxis(g, -3, 0)


TRANSPOSED = ("a_w_in", "b_w_in", "f_w_up", "ple_w_proj")


def _wire(name, a):
    return jnp.swapaxes(a, -1, -2) if name in TRANSPOSED else a


def _wire_shape(name, shape):
    return shape[:-2] + (shape[-1], shape[-2]) if name in TRANSPOSED else tuple(shape)


def _full(name, g):
    return _cols_full(g) if name in CONVS else _rows_full(g)


def _split(name, wfull):
    return _cols_split(wfull) if name in CONVS else _rows_split(wfull)


def _pack_split(grads, names, dtype, row_multiple):
    flat = jnp.concatenate([_split(n, grads[n]).reshape(N_DEV, -1).astype(dtype) for n in names], axis=1)
    rows = -(-flat.shape[1] // LANES)
    rows = -(-rows // row_multiple) * row_multiple
    return jnp.pad(flat, ((0, 0), (0, rows * LANES - flat.shape[1]))).reshape(N_DEV, rows, LANES)


def _pad_cols(a, width):
    return jnp.pad(a, ((0, 0), (0, width - a.shape[1])))


def kernel(x, p, norm_mix, norm_ffn, norm_ple, norm_final, a_w_in, a_conv, a_log, a_dt_bias, a_norm, a_w_out, b_w_in, b_sinks, b_w_out, f_w_up, f_conv, f_w_down, ple_w_proj, ple_w_gate, loss_target, m_norm_mix, m_norm_ffn, m_norm_ple, m_norm_final, m_a_w_in, m_a_conv, m_a_log, m_a_dt_bias, m_a_norm, m_a_w_out, m_b_w_in, m_b_sinks, m_b_w_out, m_f_w_up, m_f_conv, m_f_w_down, m_ple_w_proj, m_ple_w_gate, v_norm_mix, v_norm_ffn, v_norm_ple, v_norm_final, v_a_w_in, v_a_conv, v_a_log, v_a_dt_bias, v_a_norm, v_a_w_out, v_b_w_in, v_b_sinks, v_b_w_out, v_f_w_up, v_f_conv, v_f_w_down, v_ple_w_proj, v_ple_w_gate):
    wts = dict(norm_mix=norm_mix, norm_ffn=norm_ffn, norm_ple=norm_ple, norm_final=norm_final, a_w_in=a_w_in,
               a_conv=a_conv, a_log=a_log, a_dt_bias=a_dt_bias, a_norm=a_norm, a_w_out=a_w_out, b_w_in=b_w_in,
               b_sinks=b_sinks, b_w_out=b_w_out, f_w_up=f_w_up, f_conv=f_conv, f_w_down=f_w_down,
               ple_w_proj=ple_w_proj, ple_w_gate=ple_w_gate)
    mom = dict(norm_mix=m_norm_mix, norm_ffn=m_norm_ffn, norm_ple=m_norm_ple, norm_final=m_norm_final,
               a_w_in=m_a_w_in, a_conv=m_a_conv, a_log=m_a_log, a_dt_bias=m_a_dt_bias, a_norm=m_a_norm,
               a_w_out=m_a_w_out, b_w_in=m_b_w_in, b_sinks=m_b_sinks, b_w_out=m_b_w_out, f_w_up=m_f_w_up,
               f_conv=m_f_conv, f_w_down=m_f_w_down, ple_w_proj=m_ple_w_proj, ple_w_gate=m_ple_w_gate)
    var = dict(norm_mix=v_norm_mix, norm_ffn=v_norm_ffn, norm_ple=v_norm_ple, norm_final=v_norm_final,
               a_w_in=v_a_w_in, a_conv=v_a_conv, a_log=v_a_log, a_dt_bias=v_a_dt_bias, a_norm=v_a_norm,
               a_w_out=v_a_w_out, b_w_in=v_b_w_in, b_sinks=v_b_sinks, b_w_out=v_b_w_out, f_w_up=v_f_w_up,
               f_conv=v_f_conv, f_w_down=v_f_w_down, ple_w_proj=v_ple_w_proj, ple_w_gate=v_ple_w_gate)
    hk = N_HEADS_A * HEAD_DIM_A
    xs = x[0]
    tgt = loss_target[0]
    p_bf = p.astype(BF16)

    def shard(name, layer):
        return _wire(name, wts[name][layer]).astype(BF16)

    def stacked_rows(g):
        return g.reshape(g.shape[0] * g.shape[1], g.shape[2])

    n_in = a_w_in.shape[-1]
    first = _all_gather(jnp.concatenate([shard("a_w_in", 0), shard("a_w_out", 0)]), name="gather_mixer0")
    wa_in_t = jnp.pad(stacked_rows(first[:, :n_in]), ((0, PROJ_A - PROJ_A_REAL), (0, 0)))
    wa_out = stacked_rows(first[:, n_in:])
    gconv = _all_gather(_pack([wts[n] for n in CONVS], F32, 8), dep=first, name="gather_convs")
    conv_full = {n: _cols_full(g) for n, g in zip(CONVS, _unpack(gconv, [wts[n].shape for n in CONVS]))}
    cv_a, cv_f = conv_full["a_conv"][0], conv_full["f_conv"]
    layer_names = ("f_w_up", "f_w_down", "ple_w_proj", "ple_w_gate")
    gather0, tok = _exchange_start([shard(n, 0) for n in layer_names], scatter=False, name="gather_layer0_start", dep=gconv)
    gather1, tok = _exchange_start([shard(n, 0) for n in ("b_w_in", "b_w_out")] + [shard(n, 1) for n in layer_names],
                                   scatter=False, name="gather_layer1_start", dep=tok)

    alog_row = jnp.pad(a_log, ((0, 0), (N_HEADS_A, LANES - 2 * N_HEADS_A)))
    dtb_row = jnp.pad(a_dt_bias, ((0, 0), (N_HEADS_A, LANES - 2 * N_HEADS_A)))

    tile_f32, tile_bf16, rowsum = (F32, "tile"), (BF16, "tile"), (F32, "rowsum")

    def ffn_ple_fwd(i, h_a, n_f, next_norm, w_up_t, w_down, w_pp_t, w_pg):
        up, y, act = _ffn_up_act(n_f, w_up_t, cv_f[i], name=f"l{i}_ffn_up")
        h_b, n_p = _matmul_rows(act, w_down, _epi_res_norm, [h_a], [norm_ple[i]], [tile_f32, tile_bf16],
                                name=f"l{i}_ffn_down")
        pe = _matmul(p_bf[i, 0], w_pp_t, tb=True, name=f"l{i}_ple_proj")
        res = _matmul_rows(n_p, w_pg, _epi_ple, [h_b, pe], [] if next_norm is None else [next_norm],
                           [tile_f32, tile_f32] + ([] if next_norm is None else [tile_bf16]), name=f"l{i}_ple_gate")
        return res[1], (None if next_norm is None else res[2]), dict(n_f=n_f, up=up, y=y, act=act, h_b=h_b, n_p=n_p, zg=res[0], pe=pe)

    def layer_weights(lands):
        up_t, down, pp_t, pg = (stacked_rows(g) for g in lands)
        return up_t.reshape(2, D_FF, D_MODEL), down, pp_t, pg

    n0 = _rms_fwd(xs, norm_mix[0], name="l0_mix_norm")
    proj = _matmul(n0, wa_in_t, tb=True, tm=512, dep=tok, name="l0_in_proj")
    q, k, v, gbc, bbc, y_qkv = _delta_pre_fwd(proj, cv_a, alog_row, dtb_row, name="l0_delta_pre")
    o, states, tinv, og = _delta_fwd(q, k, v, gbc, bbc, proj, a_norm, name="l0_delta")
    h1, nf0 = _matmul_rows(og, wa_out, _epi_res_norm, [xs], [norm_ffn[0]], [tile_f32, tile_bf16], name="l0_mix_out")
    lw0 = layer_weights(_exchange_wait(gather0, h1, scatter=False, name="gather_layer0_wait"))
    h3, n1, sv0 = ffn_ple_fwd(0, h1, nf0, norm_mix[1], *lw0)

    lands1 = _exchange_wait(gather1, h3, scatter=False, name="gather_layer1_wait")
    wb_in_t, wb_out = stacked_rows(lands1[0]), stacked_rows(lands1[1])
    lw1 = layer_weights(lands1[2:])
    pb = _matmul(n1, wb_in_t, tb=True, name="l1_in_qkv")
    att = _swa_fwd(pb, b_sinks, name="l1_swa")
    h4, nf1 = _matmul_rows(att, wb_out, _epi_res_norm, [h3], [norm_ffn[1]], [tile_f32, tile_bf16], name="l1_mix_out")
    h6, _, sv1 = ffn_ple_fwd(1, h4, nf1, None, *lw1)

    loss_row, dh6, d_norm_final = _final_loss(h6, norm_final, tgt, name="final_loss")
    loss = lax.psum(loss_row[0, 0], MESH_AXES)

    def ffn_ple_bwd(i, dh_c, h_a, sv, lw, dep):
        w_up_t, w_down, _, w_pg = lw
        dzg, dpe = _ple_bwd(dh_c, sv["zg"], sv["pe"], name=f"l{i}_ple_mix_bwd")
        d_pg = _matmul(sv["n_p"], dzg, ta=True, out_dtype=BF16, dep=dep, name=f"l{i}_ple_gate_dw")
        d_pp_t = _matmul(dpe, p_bf[i, 0], ta=True, out_dtype=BF16, name=f"l{i}_ple_proj_dw")
        dh_b, d_np = _matmul_rows(dzg, w_pg, _epi_rms_bwd, [sv["h_b"], dh_c], [norm_ple[i]], [tile_f32, rowsum], tb=True,
                                  name=f"l{i}_ple_gate_dx")
        d_down = _matmul(sv["act"], dh_b, ta=True, out_dtype=BF16, name=f"l{i}_ffn_down_dw")
        dup, d_cg, d_cv = _ffn_down_dx_act_bwd(dh_b, w_down, sv["up"], sv["y"], cv_f[i], name=f"l{i}_ffn_down_dx")
        d_up_t = _matmul(dup, sv["n_f"], ta=True, out_dtype=BF16, name=f"l{i}_ffn_up_dw")
        dh_a, d_nf = _matmul_rows(dup, w_up_t, _epi_rms_bwd, [h_a, dh_b], [norm_ffn[i]], [tile_f32, rowsum],
                                  name=f"l{i}_ffn_up_dx")
        mats = [d_up_t.reshape(2 * D_FF, D_MODEL), d_down, d_pp_t, d_pg]
        return dh_a, mats, dict(norm_ple=d_np, norm_ffn=d_nf, f_conv=jnp.concatenate([d_cg, d_cv], axis=1))

    def slabs(g):
        return g.reshape(N_DEV, g.shape[0] // N_DEV, g.shape[1])

    dh4, mats1, g1 = ffn_ple_bwd(1, dh6, h4, sv1, lw1, None)
    datt = _matmul(dh4, wb_out, tb=True, out_dtype=BF16, name="l1_mix_out_dx")
    d_wb_out = _matmul(att, dh4, ta=True, out_dtype=BF16, name="l1_mix_out_dw")
    dq_b, dk_b, dv_b, dsinks = _swa_bwd(pb, b_sinks, datt, name="l1_swa_bwd")
    dpb = jnp.concatenate([dq_b, dk_b.astype(BF16), dv_b.astype(BF16)], axis=1)
    d_wb_in_t = _matmul(dpb, n1, ta=True, out_dtype=BF16, name="l1_in_qkv_dw")
    send1, tok = _exchange_start([slabs(g) for g in [d_wb_in_t, d_wb_out] + mats1], scatter=True,
                                 name="exchange_layer1_start", dep=d_wb_in_t)
    dh3, d_nm1 = _matmul_rows(dpb, wb_in_t, _epi_rms_bwd, [h3, dh4], [norm_mix[1]], [tile_f32, rowsum], name="l1_in_qkv_dx")

    dh1, mats0, g0 = ffn_ple_bwd(0, dh3, h1, sv0, lw0, tok)
    send0, tok = _exchange_start([slabs(g) for g in mats0], scatter=True, name="exchange_layer0_start", dep=mats0[0])
    dog = _matmul(dh1, wa_out, tb=True, dep=tok, name="l0_mix_out_dx")
    d_wa_out = _matmul(og, dh1, ta=True, out_dtype=BF16, name="l0_mix_out_dw")
    dq, dk, dv, dgbc, dbbc, dz0, d_anorm = _delta_bwd(q, k, v, gbc, bbc, tinv, states, o, proj, a_norm, dog,
                                                      name="l0_delta_bwd")
    dproj, d_aconv, d_alog, d_dtb = _delta_pre_bwd(proj, y_qkv, cv_a, alog_row, dtb_row, dq, dk, dv, dgbc, dbbc, dz0,
                                                   name="l0_delta_pre_bwd")
    d_wa_in_t = _matmul(dproj, n0, ta=True, out_dtype=BF16, name="l0_in_proj_dw")
    sendm, tok = _exchange_start([slabs(d_wa_in_t[:PROJ_A_REAL]), slabs(d_wa_out)], scatter=True,
                                 name="exchange_mixer0_start", dep=d_wa_in_t)
    dx, d_nm0 = _matmul_rows(dproj, wa_in_t, _epi_rms_bwd, [xs, dh1], [norm_mix[0]], [tile_f32, rowsum], dep=tok,
                             name="l0_in_proj_dx")

    recv1 = _exchange_wait(send1, dx, scatter=True, name="exchange_layer1_wait")
    recv0 = _exchange_wait(send0, recv1[0], scatter=True, name="exchange_layer0_wait")
    parts = {("b_w_in", 0): recv1[0], ("b_w_out", 0): recv1[1]}
    parts.update({(n, 1): r for n, r in zip(layer_names, recv1[2:])})
    parts.update({(n, 0): r for n, r in zip(layer_names, recv0)})

    outs = {}

    def update_matrix(name):
        w_, m_, v_ = (_wire(name, a) for a in (wts[name], mom[name], var[name]))
        res = None
        for layer in range(w_.shape[0]):
            res = _adamw_layer(parts[(name, layer)], w_, m_, v_, layer, res, name=f"adamw_{name}_{layer}")
        for kind, arr in zip(("grad", "delta", "new_m", "new_v"), res):
            outs[(kind, name)] = _wire(name, arr)
        return res

    last = [update_matrix(n) for n in ("b_w_in", "b_w_out") + layer_names][-1]
    recvm = _exchange_wait(sendm, last[0], scatter=True, name="exchange_mixer0_wait")
    parts.update({("a_w_in", 0): recvm[0], ("a_w_out", 0): recvm[1]})
    update_matrix("a_w_in")
    update_matrix("a_w_out")

    gconvs = dict(a_conv=d_aconv[None], f_conv=jnp.stack([g0["f_conv"], g1["f_conv"]]))
    small_g = dict(norm_mix=jnp.concatenate([d_nm0, d_nm1]), norm_ffn=jnp.concatenate([g0["norm_ffn"], g1["norm_ffn"]]),
                   norm_ple=jnp.concatenate([g0["norm_ple"], g1["norm_ple"]]), norm_final=d_norm_final[0],
                   a_log=d_alog[:, N_HEADS_A:2 * N_HEADS_A], a_dt_bias=d_dtb[:, N_HEADS_A:2 * N_HEADS_A],
                   a_norm=d_anorm, b_sinks=dsinks[:, :N_HEADS_B])
    recv_conv = _all_to_all(_pack_split(gconvs, CONVS, F32, 8), name="exchange_conv_grads")
    recv_small = _all_gather(_pack([small_g[n] for n in SMALL], F32, 8), name="gather_small_grads")
    for names, recv, tag in ((CONVS, recv_conv, "convs"), (SMALL, recv_small, "small")):
        shapes = [wts[n].shape for n in names]
        g_slab = _sum_parts(recv, name=f"sum_{tag}")
        packed = [_pack([d[n] for n in names], F32, 8) for d in (wts, mom, var)]
        res = _adamw_packed(g_slab, *packed, name=f"adamw_{tag}")
        for kind, slab in zip(("grad", "delta", "new_m", "new_v"), (g_slab,) + tuple(res)):
            for n, arr in zip(names, _unpack(slab, shapes)):
                outs[(kind, n)] = arr

    result = [loss, dx[None]]
    for kind in ("grad", "delta", "new_m", "new_v"):
        result += [outs[(kind, n)] for n in WEIGHTS]
    return tuple(result)
```

```python
import functools
import math

import jax
import jax.numpy as jnp
from jax import lax
from jax.experimental import pallas as pl
from jax.experimental.pallas import tpu as pltpu

F32 = jnp.float32
BF16 = jnp.bfloat16

D_MODEL = 1024
N_HEADS_A = 8
HEAD_DIM_A = 128
CONV_A = 4
CHUNK = 128
N_HEADS_B = 16
N_KV_B = 4
GROUP_B = N_HEADS_B // N_KV_B
HEAD_DIM_B = 64
WINDOW = 128
D_FF = 2816
FFN_CONV = 3
PLE_DIM = 256
EPS = 1e-6
N_DEV = 8
HALO = 8
PROJ_A_REAL = 4 * N_HEADS_A * HEAD_DIM_A + 2 * N_HEADS_A
PROJ_A = 4 * N_HEADS_A * HEAD_DIM_A + 128
Z_COL_BLOCK = 3
BA_COL_BLOCK = 32

ADAM_LR = 0.001
ADAM_B1 = 0.9
ADAM_B2 = 0.999
ADAM_EPS = 1e-08
ADAM_WD = 0.01
ADAM_STEP = 10

LANES = 128
VMEM_LIMIT_BYTES = 56 * 1024 * 1024
NEG_BIG = -1e30

MESH_AXES = ("x", "y", "c")


def _params(sem=None):
    return pltpu.CompilerParams(dimension_semantics=sem, vmem_limit_bytes=VMEM_LIMIT_BYTES)


def _tile(n, target):
    best = None
    for t in range(LANES, min(n, target) + 1, LANES):
        if n % t == 0:
            best = t
    return best or n


def _sigmoid(x):
    return 0.5 * jnp.tanh(0.5 * x) + 0.5


def _softplus(x):
    return jnp.maximum(x, 0.0) + jnp.log1p(jnp.exp(-jnp.abs(x)))


def _matmul(a, b, *, name, ta=False, tb=False, res=None, out_dtype=F32, tm=1408, tn=1408, tk=None, dep=None):
    sa, sb = a.ndim == 3, b.ndim == 3
    ns = a.shape[0] if sa else (b.shape[0] if sb else 1)
    contract_stack = sa and sb
    out_stacked = sa != sb
    m = a.shape[-1] if ta else a.shape[-2]
    k = a.shape[-2] if ta else a.shape[-1]
    n = b.shape[-2] if tb else b.shape[-1]
    assert (b.shape[-1] if tb else b.shape[-2]) == k, (a.shape, b.shape, ta, tb)
    if tk is None:
        tk = 1024 if ta else 2816
    tm, tn, tk = _tile(m, tm), _tile(n, tn), _tile(k, tk)
    nk = k // tk
    nsteps = nk * (ns if contract_stack else 1)
    dims = (((0 if ta else 1,), (1 if tb else 0,)), ((), ()))

    def spec(block, stacked, order):
        def index(g, i, j, kk):
            two = order(i, j, kk % nk)
            if not stacked:
                return two
            return (kk // nk if contract_stack else g,) + two
        return pl.BlockSpec(((None,) if stacked else ()) + block, index)

    a_spec = spec((tk, tm), sa, lambda i, j, kq: (kq, i)) if ta else spec((tm, tk), sa, lambda i, j, kq: (i, kq))
    b_spec = spec((tn, tk), sb, lambda i, j, kq: (j, kq)) if tb else spec((tk, tn), sb, lambda i, j, kq: (kq, j))
    o_spec = spec((tm, tn), out_stacked, lambda i, j, kq: (i, j))
    has_res = res is not None
    has_dep = dep is not None

    def body(*refs):
        a_ref, b_ref = refs[0], refs[1]
        r_ref = refs[2] if has_res else None
        o_ref = refs[2 + has_res + has_dep]
        part = lax.dot_general(a_ref[...].astype(BF16), b_ref[...].astype(BF16), dims, preferred_element_type=F32)

        def finish(acc):
            if has_res:
                acc = acc + r_ref[...].astype(F32)
            o_ref[...] = acc.astype(out_dtype)

        if nsteps == 1:
            finish(part)
        else:
            acc_ref = refs[-1]
            kk = pl.program_id(3)

            @pl.when(kk == 0)
            def _():
                acc_ref[...] = part

            @pl.when(kk > 0)
            def _():
                acc_ref[...] += part

            @pl.when(kk == nsteps - 1)
            def _():
                finish(acc_ref[...])

    in_specs = [a_spec, b_spec] + ([o_spec] if has_res else []) + ([pl.BlockSpec(memory_space=pl.ANY)] if has_dep else [])
    args = (a, b) + ((res,) if has_res else ()) + ((dep,) if has_dep else ())
    return pl.pallas_call(
        body,
        name=name,
        grid=(ns if out_stacked else 1, m // tm, n // tn, nsteps),
        in_specs=in_specs,
        out_specs=o_spec,
        out_shape=jax.ShapeDtypeStruct(((ns,) if out_stacked else ()) + (m, n), out_dtype),
        scratch_shapes=[pltpu.VMEM((tm, tn), F32)] if nsteps > 1 else [],
        compiler_params=_params(("parallel", "parallel", "parallel", "arbitrary")),
    )(*args)


EPI_ROWS = 32


def _matmul_rows(a, b, epilogue, tiles_in, rows_in, outs, *, name, tb=False, tm=512, tk=None, dep=None):
    stacked = a.ndim == 3
    ns = a.shape[0] if stacked else 1
    m, k = a.shape[-2], a.shape[-1]
    n = b.shape[-2] if tb else b.shape[-1]
    assert (b.shape[-1] if tb else b.shape[-2]) == k and (b.ndim == 3) == stacked, (a.shape, b.shape, tb)
    tm, tk = _tile(m, tm), _tile(k, 2816 if tk is None else tk)
    nk = k // tk
    nsteps = nk * ns
    dims = (((1,), (1 if tb else 0,)), ((), ()))
    lead = (None,) if stacked else ()
    front = (lambda kk: (kk // nk,)) if stacked else (lambda kk: ())
    a_spec = pl.BlockSpec(lead + (tm, tk), lambda i, kk: front(kk) + (i, kk % nk))
    if tb:
        b_spec = pl.BlockSpec(lead + (n, tk), lambda i, kk: front(kk) + (0, kk % nk))
    else:
        b_spec = pl.BlockSpec(lead + (tk, n), lambda i, kk: front(kk) + (kk % nk, 0))
    tile_spec = pl.BlockSpec((tm, n), lambda i, kk: (i, 0))
    row_spec = pl.BlockSpec((1, n), lambda i, kk: (0, 0))
    n_t, n_r, has_dep = len(tiles_in), len(rows_in), dep is not None

    def body(*refs):
        a_ref, b_ref = refs[:2]
        tile_refs = refs[2:2 + n_t]
        row_refs = refs[2 + n_t:2 + n_t + n_r]
        out_refs = refs[2 + n_t + n_r + has_dep:-1]
        acc_ref = refs[-1]
        part = lax.dot_general(a_ref[...].astype(BF16), b_ref[...].astype(BF16), dims, preferred_element_type=F32)
        kk = pl.program_id(1)
        if nsteps == 1:
            acc_ref[...] = part
        else:
            @pl.when(kk == 0)
            def _():
                acc_ref[...] = part

            @pl.when(kk > 0)
            def _():
                acc_ref[...] += part

        @pl.when(kk == nsteps - 1)
        def _():
            epilogue(acc_ref, tile_refs, row_refs, out_refs, pl.program_id(0) == 0)

    return pl.pallas_call(
        body,
        name=name,
        grid=(m // tm, nsteps),
        in_specs=[a_spec, b_spec] + [tile_spec] * n_t + [row_spec] * n_r + ([pl.BlockSpec(memory_space=pl.ANY)] if has_dep else []),
        out_specs=[tile_spec if kind == "tile" else row_spec for _, kind in outs],
        out_shape=[jax.ShapeDtypeStruct((m, n) if kind == "tile" else (1, n), dt) for dt, kind in outs],
        scratch_shapes=[pltpu.VMEM((tm, n), F32)],
        compiler_params=_params(("arbitrary", "arbitrary")),
    )(a, b, *tiles_in, *[r.reshape(1, n) for r in rows_in], *((dep,) if has_dep else ()))


def _row_chunks(ref):
    return [pl.ds(r, EPI_ROWS) for r in range(0, ref.shape[0], EPI_ROWS)]


def _rstd(x):
    return lax.rsqrt(jnp.mean(x * x, axis=-1, keepdims=True) + EPS)


def _epi_res_norm(acc, tiles, rows, outs, first):
    (res,), (w,), (h_out, n_out) = tiles, rows, outs
    chunks = _row_chunks(acc)
    for rs in chunks:
        h_out[rs, :] = acc[rs, :] + res[rs, :]
    rstds = [_rstd(h_out[rs, :]) for rs in chunks]
    for rs, r in zip(chunks, rstds):
        n_out[rs, :] = (h_out[rs, :] * r * w[...]).astype(BF16)


def _epi_ple(acc, tiles, rows, outs, first):
    hb, pe = tiles
    chunks = _row_chunks(acc)
    for rs in chunks:
        zg = acc[rs, :]
        outs[0][rs, :] = zg
        outs[1][rs, :] = hb[rs, :] + _sigmoid(zg) * pe[rs, :]
    if rows:
        rstds = [_rstd(outs[1][rs, :]) for rs in chunks]
        for rs, r in zip(chunks, rstds):
            outs[2][rs, :] = (outs[1][rs, :] * r * rows[0][...]).astype(BF16)


def _epi_rms_bwd(acc, tiles, rows, outs, first):
    (h_ref, skip), (w,), (dh_out, dw_out) = tiles, rows, outs
    chunks = _row_chunks(acc)
    n = acc.shape[1]
    rstds = [_rstd(h_ref[rs, :]) for rs in chunks]
    dots = [jnp.sum(acc[rs, :] * w[...] * h_ref[rs, :], axis=-1, keepdims=True) * r * (1.0 / n)
            for rs, r in zip(chunks, rstds)]
    dw = jnp.zeros((1, n), F32)
    for rs, r, dt in zip(chunks, rstds, dots):
        nh = h_ref[rs, :] * r
        g = acc[rs, :]
        dh_out[rs, :] = r * (g * w[...] - nh * dt) + skip[rs, :]
        dw = dw + jnp.sum(g * nh, axis=0, keepdims=True)

    @pl.when(first)
    def _():
        dw_out[...] = dw

    @pl.when(jnp.logical_not(first))
    def _():
        dw_out[...] += dw


def _rms_fwd(h, w, *, name, tm=512):
    t, d = h.shape
    tm = _tile(t, tm)

    def body(h_ref, w_ref, o_ref):
        x = h_ref[...]
        r = lax.rsqrt(jnp.mean(x * x, axis=-1, keepdims=True) + EPS)
        o_ref[...] = (x * r * w_ref[...]).astype(BF16)

    return pl.pallas_call(
        body,
        name=name,
        grid=(t // tm,),
        in_specs=[pl.BlockSpec((tm, d), lambda i: (i, 0)), pl.BlockSpec((1, d), lambda i: (0, 0))],
        out_specs=pl.BlockSpec((tm, d), lambda i: (i, 0)),
        out_shape=jax.ShapeDtypeStruct((t, d), BF16),
        compiler_params=_params(("parallel",)),
    )(h, w.reshape(1, d))


def _rms_bwd(h, w, dn, skip, *, name, tm=512):
    t, d = h.shape
    tm = _tile(t, tm)

    def body(h_ref, w_ref, dn_ref, skip_ref, dh_ref, dw_ref):
        i = pl.program_id(0)
        x = h_ref[...]
        r = lax.rsqrt(jnp.mean(x * x, axis=-1, keepdims=True) + EPS)
        nh = x * r
        g = dn_ref[...].astype(F32)
        gw = g * w_ref[...]
        dh_ref[...] = r * (gw - nh * jnp.mean(gw * nh, axis=-1, keepdims=True)) + skip_ref[...]
        part = jnp.sum(g * nh, axis=0, keepdims=True)

        @pl.when(i == 0)
        def _():
            dw_ref[...] = part

        @pl.when(i > 0)
        def _():
            dw_ref[...] += part

    row = pl.BlockSpec((tm, d), lambda i: (i, 0))
    vec = pl.BlockSpec((1, d), lambda i: (0, 0))
    return pl.pallas_call(
        body,
        name=name,
        grid=(t // tm,),
        in_specs=[row, vec, row, row],
        out_specs=[row, vec],
        out_shape=[jax.ShapeDtypeStruct((t, d), F32), jax.ShapeDtypeStruct((1, d), F32)],
        compiler_params=_params(("arbitrary",)),
    )(h, w.reshape(1, d), dn, skip)


def _final_loss(h, w, target, *, name, tm=512):
    t, d = h.shape
    tm = _tile(t, tm)

    def body(h_ref, w_ref, tg_ref, loss_ref, dh_ref, dw_ref):
        i = pl.program_id(0)
        x = h_ref[...]
        r = lax.rsqrt(jnp.mean(x * x, axis=-1, keepdims=True) + EPS)
        nh = x * r
        err = nh * w_ref[...] - tg_ref[...]
        lpart = (0.5 / d) * jnp.sum(jnp.sum(err * err, axis=-1, keepdims=True), axis=0, keepdims=True)
        g = err * (1.0 / d)
        gw = g * w_ref[...]
        dh_ref[...] = r * (gw - nh * jnp.mean(gw * nh, axis=-1, keepdims=True))
        part = jnp.sum(g * nh, axis=0, keepdims=True)
        lrow = jnp.broadcast_to(lpart, (1, LANES))

        @pl.when(i == 0)
        def _():
            dw_ref[...] = part
            loss_ref[...] = lrow

        @pl.when(i > 0)
        def _():
            dw_ref[...] += part
            loss_ref[...] += lrow

    row = pl.BlockSpec((tm, d), lambda i: (i, 0))
    vec = pl.BlockSpec((1, d), lambda i: (0, 0))
    return pl.pallas_call(
        body,
        name=name,
        grid=(t // tm,),
        in_specs=[row, vec, row],
        out_specs=[pl.BlockSpec((1, LANES), lambda i: (0, 0)), row, vec],
        out_shape=[jax.ShapeDtypeStruct((1, LANES), F32), jax.ShapeDtypeStruct((t, d), F32), jax.ShapeDtypeStruct((1, d), F32)],
        compiler_params=_params(("arbitrary",)),
    )(h, w.reshape(1, d), target)


def _conv_from_ext(ext_ref, cw_ref, kw, tm):
    y = cw_ref[kw - 1:kw, :] * ext_ref[pl.ds(HALO, tm), :]
    for i in range(kw - 1):
        y = y + cw_ref[i:i + 1, :] * ext_ref[pl.ds(HALO - (kw - 1) + i, tm), :]
    return y


ROW_CHUNK = 64


def _shifted_rows(src_ref, base, rows, shifts, cols=slice(None)):
    ext = src_ref[pl.ds(base - HALO, rows + HALO), cols]
    return [ext[HALO:, :] if s == 0 else pltpu.roll(ext, s, 0)[HALO:, :] for s in shifts]


def _conv_rows(src_ref, base, rows, cw_ref, kw, cols=slice(None)):
    wins = _shifted_rows(src_ref, base, rows, range(kw), cols)
    y = cw_ref[kw - 1:kw, cols] * wins[0]
    for s in range(1, kw):
        y = y + cw_ref[kw - 1 - s:kw - s, cols] * wins[s]
    return y


def _ahead_rows(src_ref, base, rows, shifts, cols=slice(None)):
    ext = src_ref[pl.ds(base, rows + HALO), cols]
    return [ext[:rows, :] if s == 0 else pltpu.roll(ext, rows + HALO - s, 0)[:rows, :] for s in shifts]


def _conv_t_rows(dy_ref, base, rows, cw_ref, kw, cols=slice(None)):
    wins = _ahead_rows(dy_ref, base, rows, range(kw), cols)
    dx = cw_ref[kw - 1:kw, cols] * wins[0]
    for s in range(1, kw):
        dx = dx + cw_ref[kw - 1 - s:kw - s, cols] * wins[s]
    return dx


def _fold_rows(x):
    out = x[0:HALO, :]
    for g in range(1, x.shape[0] // HALO):
        out = out + x[g * HALO:(g + 1) * HALO, :]
    return out


def _conv_bwd_from_ext(xext_ref, dyext_ref, cw_ref, dcw_ref, kw, tm, first):
    dy = dyext_ref[pl.ds(0, tm), :]
    dx = cw_ref[kw - 1:kw, :] * dy
    for i in range(kw - 1):
        dx = dx + cw_ref[i:i + 1, :] * dyext_ref[pl.ds(kw - 1 - i, tm), :]
    for i in range(kw):
        part = jnp.sum(dy * xext_ref[pl.ds(HALO - (kw - 1) + i, tm), :], axis=0, keepdims=True)

        @pl.when(first)
        def _():
            dcw_ref[i:i + 1, :] = part

        @pl.when(jnp.logical_not(first))
        def _():
            dcw_ref[i:i + 1, :] += part

    return dx


def _delta_pre_fwd(proj, conv_w, alog_row, dtb_row, *, name, tm=256):
    t = proj.shape[0]
    c3 = 3 * N_HEADS_A * HEAD_DIM_A
    hk = N_HEADS_A * HEAD_DIM_A
    tm = _tile(t, tm)

    rc = min(ROW_CHUNK, tm)

    def body(x_ref, ba_ref, cw_ref, al_ref, db_ref, q_ref, k_ref, v_ref, g_ref, b_ref, y_ref, hx):
        i = pl.program_id(0)

        @pl.when(i == 0)
        def _():
            hx[0:HALO, :] = jnp.zeros((HALO, c3), F32)

        hx[pl.ds(HALO, rc), :] = x_ref[0:rc, :]
        dsts = (q_ref, k_ref, v_ref)
        for r in range(tm // rc):
            rows = slice(r * rc, (r + 1) * rc)
            src, base = (hx, HALO) if r == 0 else (x_ref, r * rc)
            for cb in range(c3 // HEAD_DIM_A):
                cols = slice(cb * HEAD_DIM_A, (cb + 1) * HEAD_DIM_A)
                y = _conv_rows(src, base, rc, cw_ref, CONV_A, cols)
                y_ref[rows, cols] = y
                s = y * _sigmoid(y)
                kind, h = divmod(cb, N_HEADS_A)
                if kind < 2:
                    s = s * lax.rsqrt(jnp.sum(s * s, axis=-1, keepdims=True) + EPS)
                dsts[kind][rows, h * HEAD_DIM_A:(h + 1) * HEAD_DIM_A] = s
        hx[0:HALO, :] = x_ref[tm - HALO:tm, :]
        ba = ba_ref[...]
        beta = _sigmoid(ba)
        gfull = -jnp.exp(al_ref[...]) * _softplus(ba + db_ref[...])
        for h in range(N_HEADS_A):
            lo = h * HEAD_DIM_A
            b_ref[:, lo:lo + HEAD_DIM_A] = jnp.broadcast_to(beta[:, h:h + 1], (tm, HEAD_DIM_A))
            g_ref[:, lo:lo + HEAD_DIM_A] = jnp.broadcast_to(gfull[:, N_HEADS_A + h:N_HEADS_A + h + 1], (tm, HEAD_DIM_A))

    row = lambda w: pl.BlockSpec((tm, w), lambda i: (i, 0))
    fixed = lambda r, w: pl.BlockSpec((r, w), lambda i: (0, 0))
    out = jax.ShapeDtypeStruct((t, hk), F32)
    return pl.pallas_call(
        body,
        name=name,
        grid=(t // tm,),
        in_specs=[row(c3), pl.BlockSpec((tm, LANES), lambda i: (i, BA_COL_BLOCK)), fixed(CONV_A, c3), fixed(1, LANES),
                  fixed(1, LANES)],
        out_specs=[row(hk)] * 5 + [row(c3)],
        out_shape=[out] * 5 + [jax.ShapeDtypeStruct((t, c3), F32)],
        scratch_shapes=[pltpu.VMEM((HALO + rc, c3), F32)],
        compiler_params=_params(("arbitrary",)),
    )(proj, proj, conv_w, alog_row, dtb_row)


def _delta_pre_bwd(proj, y, conv_w, alog_row, dtb_row, dq, dk, dv, dg, db, dz, *, name, tm=256):
    t, pw = proj.shape
    c3 = 3 * N_HEADS_A * HEAD_DIM_A
    hk = N_HEADS_A * HEAD_DIM_A
    tm = _tile(t, tm)
    nt = t // tm

    rc = min(ROW_CHUNK, tm)
    kw = CONV_A

    def body(x_ref, y_ref, ba_ref, cw_ref, al_ref, db_ref, dq_ref, dk_ref, dv_ref, dg_ref, dbt_ref, dz_ref,
             dp_ref, dcw_ref, dal_ref, ddb_ref, *scratch):
        dys, acc = scratch[:-1], scratch[-1]
        i = pl.program_id(0)
        first = i == 0

        @pl.when(first)
        def _():
            for dyb in dys:
                dyb[pl.ds(tm, HALO), :] = jnp.zeros((HALO, HEAD_DIM_A), F32)

        srcs = (dq_ref, dk_ref, dv_ref)
        ncb = c3 // HEAD_DIM_A
        taps = [[jnp.zeros((HALO, HEAD_DIM_A), F32) for _ in range(kw)] for _ in range(ncb)]
        for r in reversed(range(tm // rc)):
            rows = slice(r * rc, (r + 1) * rc)
            for kind in range(3):
                cbs = range(kind * N_HEADS_A, (kind + 1) * N_HEADS_A)
                hs = [slice(h * HEAD_DIM_A, (h + 1) * HEAD_DIM_A) for h in range(N_HEADS_A)]
                ys = [y_ref[rows, cb * HEAD_DIM_A:(cb + 1) * HEAD_DIM_A] for cb in cbs]
                sgs = [_sigmoid(yv) for yv in ys]
                dss = [srcs[kind][rows, hsl] for hsl in hs]
                if kind < 2:
                    ss = [yv * sg for yv, sg in zip(ys, sgs)]
                    rns = [lax.rsqrt(jnp.sum(s * s, axis=-1, keepdims=True) + EPS) for s in ss]
                    qns = [s * rn for s, rn in zip(ss, rns)]
                    dots = [jnp.sum(ds * qn, axis=-1, keepdims=True) for ds, qn in zip(dss, qns)]
                    dss = [rn * (ds - qn * dt) for rn, ds, qn, dt in zip(rns, dss, qns, dots)]
                for cb, yv, sg, ds in zip(cbs, ys, sgs, dss):
                    cols = slice(cb * HEAD_DIM_A, (cb + 1) * HEAD_DIM_A)
                    dys[cb][rows, :] = ds * (sg * (1.0 + yv * (1.0 - sg)))
                    ahead = _ahead_rows(dys[cb], r * rc, rc, range(kw))
                    dp_ref[rows, cols] = sum(cw_ref[kw - 1 - s:kw - s, cols] * ahead[s] for s in range(kw)).astype(BF16)
                    xv = x_ref[rows, cols]
                    for s in range(kw):
                        taps[cb][kw - 1 - s] = taps[cb][kw - 1 - s] + _fold_rows(xv * ahead[s])
        for cb in range(ncb):
            cols = slice(cb * HEAD_DIM_A, (cb + 1) * HEAD_DIM_A)
            for j in range(kw):
                acc[j * HALO:(j + 1) * HALO, cols] = taps[cb][j]
            dys[cb][pl.ds(tm, HALO), :] = dys[cb][0:HALO, :]
        for j in range(kw):
            tap = jnp.sum(acc[j * HALO:(j + 1) * HALO, :], axis=0, keepdims=True)

            @pl.when(first)
            def _():
                dcw_ref[j:j + 1, :] = tap

            @pl.when(jnp.logical_not(first))
            def _():
                dcw_ref[j:j + 1, :] += tap

        dp_ref[:, c3:c3 + hk] = dz_ref[...]

        lane = lax.broadcasted_iota(jnp.int32, (tm, LANES), 1)
        gcol = jnp.zeros((tm, LANES), F32)
        for h in range(N_HEADS_A):
            lo = h * HEAD_DIM_A
            dbh = jnp.sum(dbt_ref[:, lo:lo + HEAD_DIM_A], axis=-1, keepdims=True)
            dgh = jnp.sum(dg_ref[:, lo:lo + HEAD_DIM_A], axis=-1, keepdims=True)
            gcol = gcol + jnp.where(lane == h, dbh, 0.0) + jnp.where(lane == N_HEADS_A + h, dgh, 0.0)
        ba = ba_ref[...]
        beta = _sigmoid(ba)
        a_neg = -jnp.exp(al_ref[...])
        z = ba + db_ref[...]
        dz = gcol * a_neg * _sigmoid(z)
        is_g = jnp.logical_and(lane >= N_HEADS_A, lane < 2 * N_HEADS_A)
        dba = jnp.where(lane < N_HEADS_A, gcol * beta * (1.0 - beta), jnp.where(is_g, dz, 0.0))
        dp_ref[:, c3 + hk:pw] = dba.astype(BF16)
        dal = jnp.sum(jnp.where(is_g, gcol * a_neg * _softplus(z), 0.0), axis=0, keepdims=True)
        ddb = jnp.sum(jnp.where(is_g, dz, 0.0), axis=0, keepdims=True)

        @pl.when(first)
        def _():
            dal_ref[...] = dal
            ddb_ref[...] = ddb

        @pl.when(jnp.logical_not(first))
        def _():
            dal_ref[...] += dal
            ddb_ref[...] += ddb

    rev = lambda w: pl.BlockSpec((tm, w), lambda i: (nt - 1 - i, 0))
    fixed = lambda r, w: pl.BlockSpec((r, w), lambda i: (0, 0))
    return pl.pallas_call(
        body,
        name=name,
        grid=(nt,),
        in_specs=[rev(c3), rev(c3), pl.BlockSpec((tm, LANES), lambda i: (nt - 1 - i, BA_COL_BLOCK)), fixed(CONV_A, c3),
                  fixed(1, LANES), fixed(1, LANES)] + [rev(hk)] * 6,
        out_specs=[rev(pw), fixed(CONV_A, c3), fixed(1, LANES), fixed(1, LANES)],
        out_shape=[jax.ShapeDtypeStruct((t, pw), BF16), jax.ShapeDtypeStruct((CONV_A, c3), F32),
                   jax.ShapeDtypeStruct((1, LANES), F32), jax.ShapeDtypeStruct((1, LANES), F32)],
        scratch_shapes=[pltpu.VMEM((tm + HALO, HEAD_DIM_A), F32)] * (c3 // HEAD_DIM_A) + [pltpu.VMEM((CONV_A * HALO, c3), F32)],
        compiler_params=_params(("arbitrary",)),
    )(proj, y, proj, conv_w, alog_row, dtb_row, dq, dk, dv, dg, db, dz)


def _gated_norm_fwd(o, proj, w, *, name, tm=512):
    t, d = o.shape
    tm = _tile(t, tm)

    def body(o_ref, z_ref, w_ref, y_ref):
        for h in range(N_HEADS_A):
            sl = slice(h * HEAD_DIM_A, (h + 1) * HEAD_DIM_A)
            oh = o_ref[:, sl]
            zh = z_ref[:, sl]
            r = lax.rsqrt(jnp.mean(oh * oh, axis=-1, keepdims=True) + EPS)
            y_ref[:, sl] = (oh * r * w_ref[...] * (zh * _sigmoid(zh))).astype(BF16)

    row = pl.BlockSpec((tm, d), lambda i: (i, 0))
    return pl.pallas_call(
        body,
        name=name,
        grid=(t // tm,),
        in_specs=[row, pl.BlockSpec((tm, d), lambda i: (i, Z_COL_BLOCK)), pl.BlockSpec((1, HEAD_DIM_A), lambda i: (0, 0))],
        out_specs=row,
        out_shape=jax.ShapeDtypeStruct((t, d), BF16),
        compiler_params=_params(("parallel",)),
    )(o, proj, w)


def _gated_norm_bwd(o, proj, w, dy, *, name, tm=512):
    t, d = o.shape
    tm = _tile(t, tm)

    def body(o_ref, z_ref, w_ref, dy_ref, do_ref, dz_ref, dw_ref):
        i = pl.program_id(0)
        dw = jnp.zeros((1, HEAD_DIM_A), F32)
        for h in range(N_HEADS_A):
            sl = slice(h * HEAD_DIM_A, (h + 1) * HEAD_DIM_A)
            oh = o_ref[:, sl]
            zh = z_ref[:, sl]
            g = dy_ref[:, sl]
            r = lax.rsqrt(jnp.mean(oh * oh, axis=-1, keepdims=True) + EPS)
            nh = oh * r
            sg = _sigmoid(zh)
            dz_ref[:, sl] = (g * nh * w_ref[...] * (sg * (1.0 + zh * (1.0 - sg)))).astype(BF16)
            dt = g * (zh * sg)
            dw = dw + jnp.sum(dt * nh, axis=0, keepdims=True)
            dnh = dt * w_ref[...]
            do_ref[:, sl] = r * (dnh - nh * jnp.mean(dnh * nh, axis=-1, keepdims=True))

        @pl.when(i == 0)
        def _():
            dw_ref[...] = dw

        @pl.when(i > 0)
        def _():
            dw_ref[...] += dw

    row = pl.BlockSpec((tm, d), lambda i: (i, 0))
    vec = pl.BlockSpec((1, HEAD_DIM_A), lambda i: (0, 0))
    return pl.pallas_call(
        body,
        name=name,
        grid=(t // tm,),
        in_specs=[row, pl.BlockSpec((tm, d), lambda i: (i, Z_COL_BLOCK)), vec, row],
        out_specs=[row, row, vec],
        out_shape=[jax.ShapeDtypeStruct((t, d), F32), jax.ShapeDtypeStruct((t, d), BF16),
                   jax.ShapeDtypeStruct((1, HEAD_DIM_A), F32)],
        compiler_params=_params(("arbitrary",)),
    )(o, proj, w, dy)


_NN = (((1,), (0,)), ((), ()))
_NT = (((1,), (1,)), ((), ()))
_TN = (((0,), (0,)), ((), ()))
_DIMS = {"nn": _NN, "nt": _NT, "tn": _TN}


def _raw_dot(a, b, kind, prec):
    dims = _DIMS[kind]
    a_hi, b_hi = a.astype(BF16), b.astype(BF16)
    out = lax.dot_general(a_hi, b_hi, dims, preferred_element_type=F32)
    if prec == "x3":
        a_lo = (a - a_hi.astype(F32)).astype(BF16)
        b_lo = (b - b_hi.astype(F32)).astype(BF16)
        out = out + lax.dot_general(a_hi, b_lo, dims, preferred_element_type=F32)
        out = out + lax.dot_general(a_lo, b_hi, dims, preferred_element_type=F32)
    elif prec == "s3":
        r1 = b - b_hi.astype(F32)
        b_mid = r1.astype(BF16)
        b_lo = (r1 - b_mid.astype(F32)).astype(BF16)
        out = out + lax.dot_general(a_hi, b_mid, dims, preferred_element_type=F32)
        out = out + lax.dot_general(a_hi, b_lo, dims, preferred_element_type=F32)
    return out


def _raw_dots(xs, ys, kind, prec):
    return [_raw_dot(x, y, kind, prec) for x, y in zip(xs, ys)]


@functools.partial(jax.custom_vjp, nondiff_argnums=(2, 3))
def _dots(xs, ys, kind, prec):
    return _raw_dots(xs, ys, kind, prec)


def _dots_fwd(xs, ys, kind, prec):
    return _raw_dots(xs, ys, kind, prec), (xs, ys)


def _dots_bwd(kind, prec, saved, gs):
    xs, ys = saved
    if kind == "nn":
        return _raw_dots(gs, ys, "nt", prec), _raw_dots(xs, gs, "tn", prec)
    if kind == "nt":
        return _raw_dots(gs, ys, "nn", prec), _raw_dots(gs, xs, "tn", prec)
    return _raw_dots(ys, gs, "nt", prec), _raw_dots(xs, gs, "nn", prec)


_dots.defvjp(_dots_fwd, _dots_bwd)


def _eye(c):
    return (lax.broadcasted_iota(jnp.int32, (c, c), 0) == lax.broadcasted_iota(jnp.int32, (c, c), 1)).astype(F32)


def _inv_unit_lower_raw(lmats):
    c = lmats[0].shape[0]
    eye = _eye(c)
    xs = [eye - l for l in lmats]
    ps = lmats
    for _ in range(int(math.log2(c)) - 1):
        ps = _raw_dots(ps, ps, "nn", "bf16")
        xs = [x + d for x, d in zip(xs, _raw_dots(xs, ps, "nn", "bf16"))]
    rs = [x - eye + d for x, d in zip(xs, _raw_dots(lmats, xs, "nn", "x3"))]
    return [x - d for x, d in zip(xs, _raw_dots(xs, rs, "nn", "bf16"))]


@jax.custom_vjp
def _inv_unit_lower(lmats, hints):
    return _inv_unit_lower_raw(lmats) if hints is None else hints


def _inv_fwd(lmats, hints):
    tms = _inv_unit_lower_raw(lmats) if hints is None else hints
    return tms, (tms, hints)


def _inv_bwd(saved, gs):
    tms, hints = saved
    ds = [-d for d in _raw_dots(_raw_dots(tms, gs, "tn", "x3"), tms, "nt", "x3")]
    return ds, (None if hints is None else [jnp.zeros_like(h) for h in hints])


_inv_unit_lower.defvjp(_inv_fwd, _inv_bwd)


def _delta_prep(qs, ks, vs, gs, bs, hints=None):
    c = qs[0].shape[0]
    nh = len(qs)
    ii = lax.broadcasted_iota(jnp.int32, (c, c), 0)
    jj = lax.broadcasted_iota(jnp.int32, (c, c), 1)
    incl = ii >= jj
    strict = ii > jj
    ltri = incl.astype(F32)
    eye = _eye(c)
    m1 = _dots([ltri] * nh, gs, "nn", "s3")
    gtot = [jnp.sum(g, axis=0, keepdims=True) for g in gs]
    decay = [jnp.exp(jnp.where(incl, m - m.T, NEG_BIG)) for m in m1]
    eg = [jnp.exp(m) for m in m1]
    kk = _dots(ks, ks, "nt", "bf16")
    lmats = [jnp.where(strict, b * x * d, 0.0) for b, x, d in zip(bs, kk, decay)]
    tinv = _inv_unit_lower(lmats, hints)
    toff = [t - eye for t in tinv]
    bv = [b * v for b, v in zip(bs, vs)]
    bk = [b * e * k for b, e, k in zip(bs, eg, ks)]
    u0 = [x + d for x, d in zip(bv, _dots(toff, bv, "nn", "bf16"))]
    wk = [x + d for x, d in zip(bk, _dots(toff, bk, "nn", "bf16"))]
    qsc = [q * (HEAD_DIM_A ** -0.5) for q in qs]
    qk = [x * d for x, d in zip(_dots(qsc, ks, "nt", "bf16"), decay)]
    q_dec = [q * e for q, e in zip(qsc, eg)]
    k_dec = [k * jnp.exp(t - m) for k, t, m in zip(ks, gtot, m1)]
    glast = [jnp.broadcast_to(jnp.exp(t), (c, c)) for t in gtot]
    return (u0, wk, qk, q_dec, k_dec, glast), tinv


def _delta_step(ss, u0, wk, qk, q_dec, k_dec, glast):
    us = [a - d for a, d in zip(u0, _dots(wk, ss, "nn", "bf16"))]
    os_ = [a + d for a, d in zip(_dots(q_dec, ss, "nn", "bf16"), _dots(qk, us, "nn", "bf16"))]
    s_new = [g * s + d for g, s, d in zip(glast, ss, _dots(k_dec, us, "tn", "bf16"))]
    return os_, s_new


HEADS_PER_STEP = 8


def _chunk_spec(nc, reverse=False):
    w = HEADS_PER_STEP * HEAD_DIM_A
    if reverse:
        return pl.BlockSpec((CHUNK, w), lambda h, n: (nc - 1 - n, h))
    return pl.BlockSpec((CHUNK, w), lambda h, n: (n, h))


def _head_slices():
    return [slice(j * HEAD_DIM_A, (j + 1) * HEAD_DIM_A) for j in range(HEADS_PER_STEP)]


def _heads(ref):
    return [ref[:, sl] for sl in _head_slices()]


def _delta_prep_fwd(q, k, v, gbc, bbc, *, name):
    t, d = q.shape
    nc = t // CHUNK

    def body(q_ref, k_ref, v_ref, g_ref, b_ref, *outs):
        res, tinv = _delta_prep(*[_heads(r) for r in (q_ref, k_ref, v_ref, g_ref, b_ref)])
        for ref, vals in zip(outs, res + (tinv,)):
            for sl, val in zip(_head_slices(), vals):
                ref[:, sl] = val

    spec = _chunk_spec(nc)
    return pl.pallas_call(
        body,
        name=name,
        grid=(N_HEADS_A // HEADS_PER_STEP, nc),
        in_specs=[spec] * 5,
        out_specs=[spec] * 7,
        out_shape=[jax.ShapeDtypeStruct((t, d), F32)] * 7,
        compiler_params=_params(("parallel", "parallel")),
    )(q, k, v, gbc, bbc)


def _delta_prep_bwd(q, k, v, gbc, bbc, tinv, cts, *, name):
    t, d = q.shape
    nc = t // CHUNK

    def body(q_ref, k_ref, v_ref, g_ref, b_ref, t_ref, c0, c1, c2, c3, c4, c5, *outs):
        def f(q_, k_, v_, g_, b_):
            return _delta_prep(q_, k_, v_, g_, b_, hints=_heads(t_ref))[0]

        _, vjp = jax.vjp(f, *[_heads(r) for r in (q_ref, k_ref, v_ref, g_ref, b_ref)])
        grads = vjp(tuple(_heads(c) for c in (c0, c1, c2, c3, c4, c5)))
        for ref, vals in zip(outs, grads):
            for sl, val in zip(_head_slices(), vals):
                ref[:, sl] = val

    spec = _chunk_spec(nc)
    return pl.pallas_call(
        body,
        name=name,
        grid=(N_HEADS_A // HEADS_PER_STEP, nc),
        in_specs=[spec] * 12,
        out_specs=[spec] * 5,
        out_shape=[jax.ShapeDtypeStruct((t, d), F32)] * 5,
        compiler_params=_params(("parallel", "parallel")),
    )(q, k, v, gbc, bbc, tinv, *cts)


def _delta_scan_fwd(prep, *, name):
    t, d = prep[0].shape
    nc = t // CHUNK

    def body(u0, wk, qk, qd, kd, gl, o_ref, st_ref, s_ref):
        n = pl.program_id(1)

        @pl.when(n == 0)
        def _():
            s_ref[...] = jnp.zeros(s_ref.shape, F32)

        ss = [s_ref[j] for j in range(HEADS_PER_STEP)]
        os_, s_new = _delta_step(ss, *[_heads(r) for r in (u0, wk, qk, qd, kd, gl)])
        for j, sl in enumerate(_head_slices()):
            st_ref[:, sl] = ss[j]
            o_ref[:, sl] = os_[j]
            s_ref[j] = s_new[j]

    spec = _chunk_spec(nc)
    return pl.pallas_call(
        body,
        name=name,
        grid=(N_HEADS_A // HEADS_PER_STEP, nc),
        in_specs=[spec] * 6,
        out_specs=[spec] * 2,
        out_shape=[jax.ShapeDtypeStruct((t, d), F32)] * 2,
        scratch_shapes=[pltpu.VMEM((HEADS_PER_STEP, HEAD_DIM_A, HEAD_DIM_A), F32)],
        compiler_params=_params(("parallel", "arbitrary")),
    )(*prep)


def _delta_scan_bwd(prep, states, do, *, name):
    t, d = do.shape
    nc = t // CHUNK

    def body(u0, wk, qk, qd, kd, gl, st_ref, do_ref, *rest):
        outs, ds_ref = rest[:6], rest[6]
        n = pl.program_id(1)

        @pl.when(n == 0)
        def _():
            ds_ref[...] = jnp.zeros(ds_ref.shape, F32)

        _, vjp = jax.vjp(_delta_step, *[_heads(r) for r in (st_ref, u0, wk, qk, qd, kd, gl)])
        grads = vjp((_heads(do_ref), [ds_ref[j] for j in range(HEADS_PER_STEP)]))
        for j, sl in enumerate(_head_slices()):
            ds_ref[j] = grads[0][j]
            for ref, vals in zip(outs, grads[1:]):
                ref[:, sl] = vals[j]

    spec = _chunk_spec(nc, reverse=True)
    return pl.pallas_call(
        body,
        name=name,
        grid=(N_HEADS_A // HEADS_PER_STEP, nc),
        in_specs=[spec] * 8,
        out_specs=[spec] * 6,
        out_shape=[jax.ShapeDtypeStruct((t, d), F32)] * 6,
        scratch_shapes=[pltpu.VMEM((HEADS_PER_STEP, HEAD_DIM_A, HEAD_DIM_A), F32)],
        compiler_params=_params(("parallel", "arbitrary")),
    )(*prep, states, do)


def _delta_fwd(q, k, v, gbc, bbc, proj, norm_w, *, name):
    assert HEADS_PER_STEP == N_HEADS_A
    t, d = q.shape
    nc = t // CHUNK

    def body(q_ref, k_ref, v_ref, g_ref, b_ref, z_ref, w_ref, o_ref, st_ref, t_ref, og_ref, s_ref):
        n = pl.program_id(0)

        @pl.when(n == 0)
        def _():
            s_ref[...] = jnp.zeros(s_ref.shape, F32)

        ss = [s_ref[j] for j in range(N_HEADS_A)]
        res, tinv = _delta_prep(*[_heads(r) for r in (q_ref, k_ref, v_ref, g_ref, b_ref)])
        os_, s_new = _delta_step(ss, *res)
        for j, sl in enumerate(_head_slices()):
            st_ref[:, sl] = ss[j]
            t_ref[:, sl] = tinv[j]
            o_ref[:, sl] = os_[j]
            s_ref[j] = s_new[j]
            zh = z_ref[:, sl]
            r = lax.rsqrt(jnp.mean(os_[j] * os_[j], axis=-1, keepdims=True) + EPS)
            og_ref[:, sl] = (os_[j] * r * w_ref[...] * (zh * _sigmoid(zh))).astype(BF16)

    spec = pl.BlockSpec((CHUNK, d), lambda n: (n, 0))
    f32 = jax.ShapeDtypeStruct((t, d), F32)
    return pl.pallas_call(
        body,
        name=name,
        grid=(nc,),
        in_specs=[spec] * 5 + [pl.BlockSpec((CHUNK, d), lambda n: (n, Z_COL_BLOCK)), pl.BlockSpec((1, HEAD_DIM_A), lambda n: (0, 0))],
        out_specs=[spec] * 4,
        out_shape=[f32, f32, f32, jax.ShapeDtypeStruct((t, d), BF16)],
        scratch_shapes=[pltpu.VMEM((N_HEADS_A, HEAD_DIM_A, HEAD_DIM_A), F32)],
        compiler_params=_params(("arbitrary",)),
    )(q, k, v, gbc, bbc, proj, norm_w)


def _delta_bwd(q, k, v, gbc, bbc, tinv, states, o, proj, norm_w, dog, *, name):
    t, d = q.shape
    nc = t // CHUNK

    def body(q_ref, k_ref, v_ref, g_ref, b_ref, t_ref, st_ref, o_ref, z_ref, w_ref, dog_ref,
             dq_ref, dk_ref, dv_ref, dg_ref, db_ref, dz_ref, dw_ref, ds_ref):
        n = pl.program_id(0)

        @pl.when(n == 0)
        def _():
            ds_ref[...] = jnp.zeros(ds_ref.shape, F32)

        hsl = _head_slices()
        rstds = [_rstd(o_ref[:, sl]) for sl in hsl]
        nhs = [o_ref[:, sl] * r for sl, r in zip(hsl, rstds)]
        sgs = [_sigmoid(z_ref[:, sl]) for sl in hsl]
        dts = [dog_ref[:, sl] * (z_ref[:, sl] * sg) for sl, sg in zip(hsl, sgs)]
        dnhs = [dt * w_ref[...] for dt in dts]
        means = [jnp.mean(dnh * nh, axis=-1, keepdims=True) for dnh, nh in zip(dnhs, nhs)]
        dos = [r * (dnh - nh * mn) for r, dnh, nh, mn in zip(rstds, dnhs, nhs, means)]
        dw = jnp.zeros((1, HEAD_DIM_A), F32)
        for sl, nh, sg, dt in zip(hsl, nhs, sgs, dts):
            zh = z_ref[:, sl]
            dz_ref[:, sl] = (dog_ref[:, sl] * nh * w_ref[...] * (sg * (1.0 + zh * (1.0 - sg)))).astype(BF16)
            dw = dw + jnp.sum(dt * nh, axis=0, keepdims=True)

        @pl.when(n == 0)
        def _():
            dw_ref[...] = dw

        @pl.when(n > 0)
        def _():
            dw_ref[...] += dw

        def chunk(qs, ks, vs, gs, bs, ss):
            return _delta_step(ss, *_delta_prep(qs, ks, vs, gs, bs, hints=_heads(t_ref))[0])

        _, vjp = jax.vjp(chunk, *[_heads(r) for r in (q_ref, k_ref, v_ref, g_ref, b_ref, st_ref)])
        grads = vjp((dos, [ds_ref[j] for j in range(N_HEADS_A)]))
        for j, sl in enumerate(_head_slices()):
            ds_ref[j] = grads[5][j]
            for ref, vals in zip((dq_ref, dk_ref, dv_ref, dg_ref, db_ref), grads[:5]):
                ref[:, sl] = vals[j]

    spec = pl.BlockSpec((CHUNK, d), lambda n: (nc - 1 - n, 0))
    vec = pl.BlockSpec((1, HEAD_DIM_A), lambda n: (0, 0))
    f32 = jax.ShapeDtypeStruct((t, d), F32)
    return pl.pallas_call(
        body,
        name=name,
        grid=(nc,),
        in_specs=[spec] * 8 + [pl.BlockSpec((CHUNK, d), lambda n: (nc - 1 - n, Z_COL_BLOCK)), vec, spec],
        out_specs=[spec] * 6 + [vec],
        out_shape=[f32] * 5 + [jax.ShapeDtypeStruct((t, d), BF16), jax.ShapeDtypeStruct((1, HEAD_DIM_A), F32)],
        scratch_shapes=[pltpu.VMEM((N_HEADS_A, HEAD_DIM_A, HEAD_DIM_A), F32)],
        compiler_params=_params(("arbitrary",)),
    )(q, k, v, gbc, bbc, tinv, states, o, proj, norm_w, dog)


def _alibi_slope(h):
    return 2.0 ** (-8.0 * (h + 1) / N_HEADS_B)


def _swa_load(sink_ref, q_ref, kp_ref, kc_ref, vp_ref, vc_ref):
    rg = lax.broadcasted_iota(jnp.int32, (GROUP_B * WINDOW, 1), 0) // WINDOW
    q4s, kcats, vcats, slopes, sinkcols = [], [], [], [], []
    for hk in range(N_KV_B):
        ks = slice(hk * HEAD_DIM_B, (hk + 1) * HEAD_DIM_B)
        heads = [hk * GROUP_B + g for g in range(GROUP_B)]
        q4s.append(jnp.concatenate([q_ref[:, h * HEAD_DIM_B:(h + 1) * HEAD_DIM_B] for h in heads], axis=0).astype(BF16))
        kcats.append(jnp.concatenate([kp_ref[:, ks], kc_ref[:, ks]], axis=0).astype(BF16))
        vcats.append(jnp.concatenate([vp_ref[:, ks], vc_ref[:, ks]], axis=0).astype(BF16))
        slope = jnp.zeros((GROUP_B * WINDOW, 1), F32)
        sink = jnp.zeros((GROUP_B * WINDOW, 1), F32)
        for g, h in enumerate(heads):
            slope = jnp.where(rg == g, _alibi_slope(h), slope)
            sink = jnp.where(rg == g, sink_ref[0, h], sink)
        slopes.append(slope)
        sinkcols.append(sink)
    return q4s, kcats, vcats, slopes, sinkcols


def _swa_probs(q4s, kcats, slopes, sinkcols, blk):
    rows = GROUP_B * WINDOW
    qi = lax.broadcasted_iota(jnp.int32, (rows, 2 * WINDOW), 0) % WINDOW
    kj = lax.broadcasted_iota(jnp.int32, (rows, 2 * WINDOW), 1)
    dist = qi + WINDOW - kj
    valid = (dist >= 0) & (dist < WINDOW) & (blk * WINDOW - WINDOW + kj >= 0)
    distf = dist.astype(F32)
    ss = [lax.dot_general(q, kc, _NT, preferred_element_type=F32) for q, kc in zip(q4s, kcats)]
    logits = [jnp.where(valid, s * (HEAD_DIM_B ** -0.5) - sl * distf, NEG_BIG) for s, sl in zip(ss, slopes)]
    ms = [jnp.maximum(jnp.max(l, axis=-1, keepdims=True), sk) for l, sk in zip(logits, sinkcols)]
    es = [jnp.exp(l - m) for l, m in zip(logits, ms)]
    esk = [jnp.exp(sk - m) for sk, m in zip(sinkcols, ms)]
    invs = [1.0 / (jnp.sum(e, axis=-1, keepdims=True) + k) for e, k in zip(es, esk)]
    return [e * i for e, i in zip(es, invs)], [k * i for k, i in zip(esk, invs)]


def _swa_fwd(proj, sinks, *, name):
    t = proj.shape[0]
    nb = t // WINDOW
    qd = N_HEADS_B * HEAD_DIM_B
    kd = N_KV_B * HEAD_DIM_B

    def body(sink_ref, q_ref, kp_ref, kc_ref, vp_ref, vc_ref, o_ref):
        blk = pl.program_id(0)
        q4s, kcats, vcats, slopes, sinkcols = _swa_load(sink_ref, q_ref, kp_ref, kc_ref, vp_ref, vc_ref)
        ps, _ = _swa_probs(q4s, kcats, slopes, sinkcols, blk)
        outs = [jnp.dot(p.astype(BF16), vc, preferred_element_type=F32) for p, vc in zip(ps, vcats)]
        for hk, out in enumerate(outs):
            for g in range(GROUP_B):
                h = hk * GROUP_B + g
                o_ref[:, h * HEAD_DIM_B:(h + 1) * HEAD_DIM_B] = out[g * WINDOW:(g + 1) * WINDOW, :].astype(BF16)

    q_spec = pl.BlockSpec((WINDOW, qd), lambda i: (i, 0))
    kv = lambda col, prev: pl.BlockSpec((WINDOW, kd), (lambda i: (jnp.maximum(i - 1, 0), col)) if prev else (lambda i: (i, col)))
    kcol, vcol = qd // kd, qd // kd + 1
    return pl.pallas_call(
        body,
        name=name,
        grid=(nb,),
        in_specs=[pl.BlockSpec(memory_space=pltpu.SMEM), q_spec, kv(kcol, True), kv(kcol, False), kv(vcol, True), kv(vcol, False)],
        out_specs=q_spec,
        out_shape=jax.ShapeDtypeStruct((t, qd), BF16),
        compiler_params=_params(("parallel",)),
    )(sinks, proj, proj, proj, proj, proj)


def _swa_bwd(proj, sinks, dout, *, name):
    t = proj.shape[0]
    nb = t // WINDOW
    qd = N_HEADS_B * HEAD_DIM_B
    kd = N_KV_B * HEAD_DIM_B
    scale = HEAD_DIM_B ** -0.5

    def body(sink_ref, q_ref, kp_ref, kc_ref, vp_ref, vc_ref, do_ref, dq_ref, dk_ref, dv_ref, dsk_ref):
        blk = pl.program_id(0)
        lane = lax.broadcasted_iota(jnp.int32, (1, LANES), 1)

        @pl.when(blk == 0)
        def _():
            dk_ref[...] = jnp.zeros((t, kd), F32)
            dv_ref[...] = jnp.zeros((t, kd), F32)
            dsk_ref[...] = jnp.zeros((1, LANES), F32)

        cur = pl.ds(pl.multiple_of(blk * WINDOW, WINDOW), WINDOW)
        prv = pl.ds(pl.multiple_of(jnp.maximum(blk - 1, 0) * WINDOW, WINDOW), WINDOW)
        q4s, kcats, vcats, slopes, sinkcols = _swa_load(sink_ref, q_ref, kp_ref, kc_ref, vp_ref, vc_ref)
        ps, psinks = _swa_probs(q4s, kcats, slopes, sinkcols, blk)
        do4s = [jnp.concatenate([do_ref[:, (hk * GROUP_B + g) * HEAD_DIM_B:(hk * GROUP_B + g + 1) * HEAD_DIM_B]
                                 for g in range(GROUP_B)], axis=0).astype(BF16) for hk in range(N_KV_B)]
        dps = [lax.dot_general(d, vc, _NT, preferred_element_type=F32) for d, vc in zip(do4s, vcats)]
        deltas = [jnp.sum(p * dp, axis=-1, keepdims=True) for p, dp in zip(ps, dps)]
        dss = [(p * (dp - dl) * scale).astype(BF16) for p, dp, dl in zip(ps, dps, deltas)]
        dq4s = [jnp.dot(ds, kc, preferred_element_type=F32) for ds, kc in zip(dss, kcats)]
        dkcs = [lax.dot_general(ds, q, _TN, preferred_element_type=F32) for ds, q in zip(dss, q4s)]
        dvcs = [lax.dot_general(p.astype(BF16), d, _TN, preferred_element_type=F32) for p, d in zip(ps, do4s)]
        dsk = jnp.zeros((1, LANES), F32)
        for hk in range(N_KV_B):
            ks = slice(hk * HEAD_DIM_B, (hk + 1) * HEAD_DIM_B)
            dsink = -psinks[hk] * deltas[hk]
            for g in range(GROUP_B):
                h = hk * GROUP_B + g
                rows = slice(g * WINDOW, (g + 1) * WINDOW)
                dq_ref[:, h * HEAD_DIM_B:(h + 1) * HEAD_DIM_B] = dq4s[hk][rows, :].astype(BF16)
                dsk = dsk + jnp.where(lane == h, jnp.sum(dsink[rows, :], axis=0, keepdims=True), 0.0)
            dk_ref[cur, ks] += dkcs[hk][WINDOW:, :]
            dv_ref[cur, ks] += dvcs[hk][WINDOW:, :]

            @pl.when(blk > 0)
            def _():
                dk_ref[prv, ks] += dkcs[hk][:WINDOW, :]
                dv_ref[prv, ks] += dvcs[hk][:WINDOW, :]

        dsk_ref[...] += dsk

    q_spec = pl.BlockSpec((WINDOW, qd), lambda i: (i, 0))
    kv = lambda col, prev: pl.BlockSpec((WINDOW, kd), (lambda i: (jnp.maximum(i - 1, 0), col)) if prev else (lambda i: (i, col)))
    kcol, vcol = qd // kd, qd // kd + 1
    full = pl.BlockSpec((t, kd), lambda i: (0, 0))
    return pl.pallas_call(
        body,
        name=name,
        grid=(nb,),
        in_specs=[pl.BlockSpec(memory_space=pltpu.SMEM), q_spec, kv(kcol, True), kv(kcol, False), kv(vcol, True), kv(vcol, False), q_spec],
        out_specs=[q_spec, full, full, pl.BlockSpec((1, LANES), lambda i: (0, 0))],
        out_shape=[jax.ShapeDtypeStruct((t, qd), BF16), jax.ShapeDtypeStruct((t, kd), F32),
                   jax.ShapeDtypeStruct((t, kd), F32), jax.ShapeDtypeStruct((1, LANES), F32)],
        compiler_params=_params(("arbitrary",)),
    )(sinks, proj, proj, proj, proj, proj, dout)


def _ffn_act_fwd(up, cw, *, name, tm=512, cb=256):
    _, t, f = up.shape
    tm, cb = _tile(t, tm), _tile(f, cb)

    rc = min(ROW_CHUNK, tm)

    def body(ug_ref, uv_ref, cg_ref, cv_ref, a_ref, hg, hv):
        i = pl.program_id(1)

        @pl.when(i == 0)
        def _():
            hg[0:HALO, :] = jnp.zeros((HALO, cb), F32)
            hv[0:HALO, :] = jnp.zeros((HALO, cb), F32)

        hg[pl.ds(HALO, rc), :] = ug_ref[0:rc, :]
        hv[pl.ds(HALO, rc), :] = uv_ref[0:rc, :]
        for r in range(tm // rc):
            if r == 0:
                yg = _conv_rows(hg, HALO, rc, cg_ref, FFN_CONV)
                yv = _conv_rows(hv, HALO, rc, cv_ref, FFN_CONV)
            else:
                yg = _conv_rows(ug_ref, r * rc, rc, cg_ref, FFN_CONV)
                yv = _conv_rows(uv_ref, r * rc, rc, cv_ref, FFN_CONV)
            a_ref[r * rc:(r + 1) * rc, :] = (yg * _sigmoid(yg) * yv).astype(BF16)
        hg[0:HALO, :] = ug_ref[tm - HALO:tm, :]
        hv[0:HALO, :] = uv_ref[tm - HALO:tm, :]

    ncb = f // cb
    half = lambda s: pl.BlockSpec((None, tm, cb), lambda c, i: (s, i, c))
    taps = lambda s: pl.BlockSpec((FFN_CONV, cb), lambda c, i: (0, c + s * ncb))
    return pl.pallas_call(
        body,
        name=name,
        grid=(ncb, t // tm),
        in_specs=[half(0), half(1), taps(0), taps(1)],
        out_specs=pl.BlockSpec((tm, cb), lambda c, i: (i, c)),
        out_shape=jax.ShapeDtypeStruct((t, f), BF16),
        scratch_shapes=[pltpu.VMEM((HALO + rc, cb), F32)] * 2,
        compiler_params=_params(("parallel", "arbitrary")),
    )(up, up, cw, cw)


def _ffn_act_bwd(up, cw, dact, *, name, tm=512, cb=256):
    _, t, f = up.shape
    tm, cb = _tile(t, tm), _tile(f, cb)
    nt = t // tm
    hb = tm // HALO

    rc = min(ROW_CHUNK, tm)
    nr = tm // rc
    kw = FFN_CONV

    def body(ug_ref, uv_ref, pg_ref, pv_ref, cg_ref, cv_ref, da_ref, du_ref, dcg_ref, dcv_ref,
             hg, hv, dyg, dyv):
        i = pl.program_id(1)
        first = i == 0
        tile = nt - 1 - i

        @pl.when(tile == 0)
        def _():
            hg[0:HALO, :] = jnp.zeros((HALO, cb), F32)
            hv[0:HALO, :] = jnp.zeros((HALO, cb), F32)

        @pl.when(tile > 0)
        def _():
            hg[0:HALO, :] = pg_ref[...]
            hv[0:HALO, :] = pv_ref[...]

        @pl.when(first)
        def _():
            dyg[pl.ds(tm, HALO), :] = jnp.zeros((HALO, cb), F32)
            dyv[pl.ds(tm, HALO), :] = jnp.zeros((HALO, cb), F32)

        hg[pl.ds(HALO, rc), :] = ug_ref[0:rc, :]
        hv[pl.ds(HALO, rc), :] = uv_ref[0:rc, :]
        dcg = [jnp.zeros((1, cb), F32) for _ in range(kw)]
        dcv = [jnp.zeros((1, cb), F32) for _ in range(kw)]
        for r in reversed(range(nr)):
            rows = slice(r * rc, (r + 1) * rc)
            src_g, src_v, base = (hg, hv, HALO) if r == 0 else (ug_ref, uv_ref, r * rc)
            yg = _conv_rows(src_g, base, rc, cg_ref, kw)
            yv = _conv_rows(src_v, base, rc, cv_ref, kw)
            sg = _sigmoid(yg)
            da = da_ref[rows, :]
            dy_g = da * yv * (sg * (1.0 + yg * (1.0 - sg)))
            dy_v = da * (yg * sg)
            dyg[rows, :] = dy_g
            dyv[rows, :] = dy_v
            du_ref[0, rows, :] = _conv_t_rows(dyg, r * rc, rc, cg_ref, kw).astype(BF16)
            du_ref[1, rows, :] = _conv_t_rows(dyv, r * rc, rc, cv_ref, kw).astype(BF16)
            for j in range(kw):
                dcg[j] = dcg[j] + jnp.sum(dy_g * src_g[pl.ds(base - (kw - 1) + j, rc), :], axis=0, keepdims=True)
                dcv[j] = dcv[j] + jnp.sum(dy_v * src_v[pl.ds(base - (kw - 1) + j, rc), :], axis=0, keepdims=True)
        dyg[pl.ds(tm, HALO), :] = dyg[0:HALO, :]
        dyv[pl.ds(tm, HALO), :] = dyv[0:HALO, :]
        for j in range(kw):
            @pl.when(first)
            def _():
                dcg_ref[j:j + 1, :] = dcg[j]
                dcv_ref[j:j + 1, :] = dcv[j]

            @pl.when(jnp.logical_not(first))
            def _():
                dcg_ref[j:j + 1, :] += dcg[j]
                dcv_ref[j:j + 1, :] += dcv[j]

    ncb = f // cb
    half = lambda s: pl.BlockSpec((None, tm, cb), lambda c, i: (s, nt - 1 - i, c))
    prev = lambda s: pl.BlockSpec((None, HALO, cb), lambda c, i: (s, jnp.maximum((nt - 1 - i) * hb - 1, 0), c))
    taps = lambda s: pl.BlockSpec((FFN_CONV, cb), lambda c, i: (0, c + s * ncb))
    dtaps = pl.BlockSpec((FFN_CONV, cb), lambda c, i: (0, c))
    return pl.pallas_call(
        body,
        name=name,
        grid=(ncb, nt),
        in_specs=[half(0), half(1), prev(0), prev(1), taps(0), taps(1), pl.BlockSpec((tm, cb), lambda c, i: (nt - 1 - i, c))],
        out_specs=[pl.BlockSpec((2, tm, cb), lambda c, i: (0, nt - 1 - i, c)), dtaps, dtaps],
        out_shape=[jax.ShapeDtypeStruct((2, t, f), BF16), jax.ShapeDtypeStruct((FFN_CONV, f), F32),
                   jax.ShapeDtypeStruct((FFN_CONV, f), F32)],
        scratch_shapes=[pltpu.VMEM((HALO + rc, cb), F32)] * 2 + [pltpu.VMEM((tm + HALO, cb), F32)] * 2,
        compiler_params=_params(("parallel", "arbitrary")),
    )(up, up, up, up, cw, cw, dact)


FFN_COL_TILE = 1408
FFN_SUB = 512
FFN_ROW_CHUNK = 16


def _sub_blocks(width):
    return [slice(c, min(c + FFN_SUB, width)) for c in range(0, width, FFN_SUB)]


def _ffn_up_act(n_f, w_up_t, cw, *, name, tm=512):
    t, d = n_f.shape
    f = w_up_t.shape[1]
    tm, tn = _tile(t, tm), _tile(f, FFN_COL_TILE)
    nj = f // tn
    rc = min(FFN_ROW_CHUNK, tm)
    kw = FFN_CONV

    def body(n_ref, wg_ref, wv_ref, cg_ref, cv_ref, up_ref, y_ref, a_ref, hg, hv):
        i = pl.program_id(1)

        @pl.when(i == 0)
        def _():
            hg[0:HALO, :] = jnp.zeros((HALO, tn), F32)
            hv[0:HALO, :] = jnp.zeros((HALO, tn), F32)

        def products(cs):
            up_ref[0, :, cs] = lax.dot_general(n_ref[...], wg_ref[cs, :], _NT, preferred_element_type=F32)
            up_ref[1, :, cs] = lax.dot_general(n_ref[...], wv_ref[cs, :], _NT, preferred_element_type=F32)

        subs = _sub_blocks(tn)
        ug, uv = up_ref.at[0], up_ref.at[1]
        products(subs[0])
        for ci, cs in enumerate(subs):
            if ci + 1 < len(subs):
                products(subs[ci + 1])
            hg[pl.ds(HALO, rc), cs] = ug[0:rc, cs]
            hv[pl.ds(HALO, rc), cs] = uv[0:rc, cs]
            for r in range(tm // rc):
                src_g, src_v, base = (hg, hv, HALO) if r == 0 else (ug, uv, r * rc)
                yg = _conv_rows(src_g, base, rc, cg_ref, kw, cs)
                yv = _conv_rows(src_v, base, rc, cv_ref, kw, cs)
                y_ref[0, r * rc:(r + 1) * rc, cs] = yg
                y_ref[1, r * rc:(r + 1) * rc, cs] = yv
                a_ref[r * rc:(r + 1) * rc, cs] = (yg * _sigmoid(yg) * yv).astype(BF16)
            hg[0:HALO, cs] = ug[tm - HALO:tm, cs]
            hv[0:HALO, cs] = uv[tm - HALO:tm, cs]

    half = lambda s: pl.BlockSpec((None, tn, d), lambda j, i: (s, j, 0))
    taps = lambda s: pl.BlockSpec((kw, tn), lambda j, i: (0, j + s * nj))
    pair = pl.BlockSpec((2, tm, tn), lambda j, i: (0, i, j))
    return pl.pallas_call(
        body,
        name=name,
        grid=(nj, t // tm),
        in_specs=[pl.BlockSpec((tm, d), lambda j, i: (i, 0)), half(0), half(1), taps(0), taps(1)],
        out_specs=[pair, pair, pl.BlockSpec((tm, tn), lambda j, i: (i, j))],
        out_shape=[jax.ShapeDtypeStruct((2, t, f), F32), jax.ShapeDtypeStruct((2, t, f), F32),
                   jax.ShapeDtypeStruct((t, f), BF16)],
        scratch_shapes=[pltpu.VMEM((HALO + rc, tn), F32)] * 2,
        compiler_params=_params(("parallel", "arbitrary")),
    )(n_f, w_up_t, w_up_t, cw, cw)


def _ffn_down_dx_act_bwd(dh, w_down, up, y, cw, *, name, tm=512):
    t, d = dh.shape
    f = w_down.shape[0]
    tm, tn = _tile(t, tm), _tile(f, FFN_COL_TILE)
    nj, nt = f // tn, t // tm
    rc = min(FFN_ROW_CHUNK, tm)
    nr = tm // rc
    kw = FFN_CONV

    def body(dh_ref, wd_ref, ug_ref, uv_ref, yg_ref, yv_ref, cg_ref, cv_ref, du_ref, dcg_ref, dcv_ref,
             dyg, dyv, da_s, dh_s):
        i = pl.program_id(1)
        first = i == 0

        @pl.when(first)
        def _():
            dyg[pl.ds(tm, HALO), :] = jnp.zeros((HALO, tn), F32)
            dyv[pl.ds(tm, HALO), :] = jnp.zeros((HALO, tn), F32)

        dh_s[...] = dh_ref[...].astype(BF16)

        def product(cs):
            da_s[:, cs] = lax.dot_general(dh_s[...], wd_ref[cs, :], _NT, preferred_element_type=F32)

        subs = _sub_blocks(tn)
        product(subs[0])
        for ci, cs in enumerate(subs):
            width = cs.stop - cs.start
            if ci + 1 < len(subs):
                product(subs[ci + 1])
            dcg = [jnp.zeros((HALO, width), F32) for _ in range(kw)]
            dcv = [jnp.zeros((HALO, width), F32) for _ in range(kw)]
            for r in reversed(range(nr)):
                rows = slice(r * rc, (r + 1) * rc)
                yg, yv = yg_ref[rows, cs], yv_ref[rows, cs]
                sg = _sigmoid(yg)
                da = da_s[rows, cs]
                dyg[rows, cs] = da * yv * (sg * (1.0 + yg * (1.0 - sg)))
                dyv[rows, cs] = da * (yg * sg)
                ahead_g = _ahead_rows(dyg, r * rc, rc, range(kw), cs)
                ahead_v = _ahead_rows(dyv, r * rc, rc, range(kw), cs)
                du_ref[0, rows, cs] = sum(cg_ref[kw - 1 - s:kw - s, cs] * ahead_g[s] for s in range(kw)).astype(BF16)
                du_ref[1, rows, cs] = sum(cv_ref[kw - 1 - s:kw - s, cs] * ahead_v[s] for s in range(kw)).astype(BF16)
                xg, xv = ug_ref[rows, cs], uv_ref[rows, cs]
                for s in range(kw):
                    dcg[kw - 1 - s] = dcg[kw - 1 - s] + _fold_rows(xg * ahead_g[s])
                    dcv[kw - 1 - s] = dcv[kw - 1 - s] + _fold_rows(xv * ahead_v[s])
            dyg[pl.ds(tm, HALO), cs] = dyg[0:HALO, cs]
            dyv[pl.ds(tm, HALO), cs] = dyv[0:HALO, cs]
            for j in range(kw):
                tg = jnp.sum(dcg[j], axis=0, keepdims=True)
                tv = jnp.sum(dcv[j], axis=0, keepdims=True)

                @pl.when(first)
                def _():
                    dcg_ref[j:j + 1, cs] = tg
                    dcv_ref[j:j + 1, cs] = tv

                @pl.when(jnp.logical_not(first))
                def _():
                    dcg_ref[j:j + 1, cs] += tg
                    dcv_ref[j:j + 1, cs] += tv

    half = lambda s: pl.BlockSpec((None, tm, tn), lambda j, i: (s, nt - 1 - i, j))
    taps = lambda s: pl.BlockSpec((kw, tn), lambda j, i: (0, j + s * nj))
    dtaps = pl.BlockSpec((kw, tn), lambda j, i: (0, j))
    return pl.pallas_call(
        body,
        name=name,
        grid=(nj, nt),
        in_specs=[pl.BlockSpec((tm, d), lambda j, i: (nt - 1 - i, 0)), pl.BlockSpec((tn, d), lambda j, i: (j, 0)),
                  half(0), half(1), half(0), half(1), taps(0), taps(1)],
        out_specs=[pl.BlockSpec((2, tm, tn), lambda j, i: (0, nt - 1 - i, j)), dtaps, dtaps],
        out_shape=[jax.ShapeDtypeStruct((2, t, f), BF16), jax.ShapeDtypeStruct((kw, f), F32),
                   jax.ShapeDtypeStruct((kw, f), F32)],
        scratch_shapes=[pltpu.VMEM((tm + HALO, tn), F32)] * 2 + [pltpu.VMEM((tm, tn), F32), pltpu.VMEM((tm, d), BF16)],
        compiler_params=_params(("parallel", "arbitrary")),
    )(dh, w_down, up, up, y, y, cw, cw)


def _ple_fwd(h, zg, pe, *, name, tm=512):
    t, d = h.shape
    tm = _tile(t, tm)

    def body(h_ref, z_ref, p_ref, o_ref):
        o_ref[...] = h_ref[...] + _sigmoid(z_ref[...]) * p_ref[...]

    row = pl.BlockSpec((tm, d), lambda i: (i, 0))
    return pl.pallas_call(
        body, name=name, grid=(t // tm,), in_specs=[row] * 3, out_specs=row,
        out_shape=jax.ShapeDtypeStruct((t, d), F32), compiler_params=_params(("parallel",)),
    )(h, zg, pe)


def _ple_bwd(dh, zg, pe, *, name, tm=512):
    t, d = dh.shape
    tm = _tile(t, tm)

    def body(g_ref, z_ref, p_ref, dz_ref, dp_ref):
        g = g_ref[...]
        sg = _sigmoid(z_ref[...])
        dz_ref[...] = (g * p_ref[...] * sg * (1.0 - sg)).astype(BF16)
        dp_ref[...] = (g * sg).astype(BF16)

    row = pl.BlockSpec((tm, d), lambda i: (i, 0))
    return pl.pallas_call(
        body, name=name, grid=(t // tm,), in_specs=[row] * 3, out_specs=[row] * 2,
        out_shape=[jax.ShapeDtypeStruct((t, d), BF16)] * 2, compiler_params=_params(("parallel",)),
    )(dh, zg, pe)


def _my_pos():
    return lax.axis_index("x"), lax.axis_index("y"), lax.axis_index("c")


def _all_gather(block, *, name, dep=None):
    r, w = block.shape
    has_dep = dep is not None

    def body(*refs):
        x_ref, out_ref, send_sems, recv_sems, local_sem = refs[:1] + refs[1 + has_dep:]
        x, y, c = _my_pos()
        me, sibling = (x, y, c), (x, y, 1 - c)
        chips = [(1 - x, y), (x, 1 - y), (1 - x, 1 - y)]

        def slot(px, py, pc):
            return out_ref.at[4 * px + 2 * py + pc]

        def copy(k, blk, to, src=None):
            return pltpu.make_async_remote_copy(
                src_ref=slot(*blk) if src is None else src, dst_ref=slot(*blk),
                send_sem=send_sems.at[k], recv_sem=recv_sems.at[k],
                device_id=to, device_id_type=pl.DeviceIdType.MESH)

        mine = pltpu.make_async_copy(x_ref, slot(*me), local_sem)
        mine.start()
        first = [copy(0, me, sibling, src=x_ref)]
        first += [copy(1 + j, me, (*chip, c), src=x_ref) for j, chip in enumerate(chips)]
        for cp in first:
            cp.start()
        passed = [copy(4 + j, (*chip, c), sibling) for j, chip in enumerate(chips)]
        for j, chip in enumerate(chips):
            copy(1 + j, (*chip, c), me).wait_recv()
            passed[j].start()
        copy(0, sibling, me).wait_recv()
        for j, chip in enumerate(chips):
            copy(4 + j, (*chip, 1 - c), me).wait_recv()
        for cp in first + passed:
            cp.wait_send()
        mine.wait()

    return pl.pallas_call(
        body,
        name=name,
        out_shape=jax.ShapeDtypeStruct((N_DEV, r, w), block.dtype),
        in_specs=[pl.BlockSpec(memory_space=pl.ANY)] * (1 + has_dep),
        out_specs=pl.BlockSpec(memory_space=pl.ANY),
        scratch_shapes=[pltpu.SemaphoreType.DMA((7,)), pltpu.SemaphoreType.DMA((7,)), pltpu.SemaphoreType.DMA],
    )(*((block, dep) if has_dep else (block,)))


def _all_to_all(slabs, *, name):
    n, r, w = slabs.shape

    def body(x_ref, out_ref, send_sems, recv_sems, local_sem):
        x, y, c = _my_pos()
        my_idx = 4 * x + 2 * y + c
        mine = pltpu.make_async_copy(x_ref.at[my_idx], out_ref.at[my_idx], local_sem)
        mine.start()
        copies = []
        for k in range(1, N_DEV):
            fx, fy, fc = (k >> 2) & 1, (k >> 1) & 1, k & 1
            px = (1 - x) if fx else x
            py = (1 - y) if fy else y
            pc = (1 - c) if fc else c
            cp = pltpu.make_async_remote_copy(
                src_ref=x_ref.at[4 * px + 2 * py + pc], dst_ref=out_ref.at[my_idx],
                send_sem=send_sems.at[k - 1], recv_sem=recv_sems.at[k - 1],
                device_id=(px, py, pc), device_id_type=pl.DeviceIdType.MESH)
            cp.start()
            copies.append(cp)
        for cp in copies:
            cp.wait_recv()
        for cp in copies:
            cp.wait_send()
        mine.wait()

    return pl.pallas_call(
        body,
        name=name,
        out_shape=jax.ShapeDtypeStruct((n, r, w), slabs.dtype),
        in_specs=[pl.BlockSpec(memory_space=pl.ANY)],
        out_specs=pl.BlockSpec(memory_space=pl.ANY),
        scratch_shapes=[pltpu.SemaphoreType.DMA((7,)), pltpu.SemaphoreType.DMA((7,)), pltpu.SemaphoreType.DMA],
    )(slabs)


def _exchange_copies(scatter, src_refs, land_refs, send_sems, recv_sems, local_sems):
    x, y, c = _my_pos()
    me = 4 * x + 2 * y + c
    local, remote = [], []
    for i, (s, l) in enumerate(zip(src_refs, land_refs)):
        local.append(pltpu.make_async_copy(s.at[me] if scatter else s, l.at[me], local_sems.at[i]))
        for k in range(1, N_DEV):
            px = (1 - x) if (k >> 2) & 1 else x
            py = (1 - y) if (k >> 1) & 1 else y
            pc = (1 - c) if k & 1 else c
            remote.append(pltpu.make_async_remote_copy(
                src_ref=s.at[4 * px + 2 * py + pc] if scatter else s, dst_ref=l.at[me],
                send_sem=send_sems.at[(N_DEV - 1) * i + k - 1], recv_sem=recv_sems.at[(N_DEV - 1) * i + k - 1],
                device_id=(px, py, pc), device_id_type=pl.DeviceIdType.MESH))
    return local, remote


def _exchange(arrays, *, scatter, name):
    n = len(arrays)

    def body(*refs):
        srcs, lands = refs[:n], refs[n:2 * n]
        local, remote = _exchange_copies(scatter, srcs, lands, *refs[2 * n:])
        for cp in local + remote:
            cp.start()
        for cp in remote:
            cp.wait_recv()
        for cp in remote:
            cp.wait_send()
        for cp in local:
            cp.wait()

    hbm = pl.BlockSpec(memory_space=pl.ANY)
    out = pl.pallas_call(
        body,
        name=name,
        out_shape=[jax.ShapeDtypeStruct(a.shape if scatter else (N_DEV,) + a.shape, a.dtype) for a in arrays],
        in_specs=[hbm] * n,
        out_specs=[hbm] * n,
        scratch_shapes=[pltpu.SemaphoreType.DMA(((N_DEV - 1) * n,)), pltpu.SemaphoreType.DMA(((N_DEV - 1) * n,)),
                        pltpu.SemaphoreType.DMA((n,))],
    )(*arrays)
    return list(out)


_HBM_SPEC = pl.BlockSpec(memory_space=pltpu.HBM)
_SEM_SPEC = pl.BlockSpec(memory_space=pltpu.SEMAPHORE)
_EFFECT = pltpu.SideEffectType.DATAFLOW_SIDE_EFFECTING


def _exchange_start(arrays, *, scatter, name, dep):
    n = len(arrays)
    srcs = [pltpu.with_memory_space_constraint(a, pltpu.HBM) for a in arrays]
    lands = [pltpu.with_memory_space_constraint(lax.empty(a.shape if scatter else (N_DEV,) + a.shape, a.dtype), pltpu.HBM)
             for a in arrays]

    def body(*refs):
        src_refs, land_refs = refs[:n], refs[n:2 * n]
        send_sems, recv_sems, local_sems = refs[2 * n + 1:2 * n + 4]
        token = refs[-1]
        local, remote = _exchange_copies(scatter, src_refs, land_refs, send_sems, recv_sems, local_sems)
        for cp in local + remote:
            cp.start()
        token[...] = jnp.zeros_like(token)

    sems = (pltpu.SemaphoreType.DMA(((N_DEV - 1) * n,)), pltpu.SemaphoreType.DMA(((N_DEV - 1) * n,)),
            pltpu.SemaphoreType.DMA((n,)))
    out = pl.pallas_call(
        body,
        name=name,
        out_shape=sems + tuple(pltpu.HBM(a.shape, a.dtype) for a in srcs + lands) + (jax.ShapeDtypeStruct((8, LANES), F32),),
        in_specs=[_HBM_SPEC] * (2 * n) + [pl.BlockSpec(memory_space=pl.ANY)],
        out_specs=(_SEM_SPEC,) * 3 + (_HBM_SPEC,) * (2 * n) + (pl.BlockSpec(memory_space=pltpu.VMEM),),
        input_output_aliases={i: 3 + i for i in range(2 * n)},
        compiler_params=pltpu.CompilerParams(has_side_effects=_EFFECT),
    )(*srcs, *lands, dep)
    return (out[:3], list(out[3:3 + n]), list(out[3 + n:3 + 2 * n])), out[-1]


def _exchange_wait(handle, after, *, scatter, name):
    sems, srcs, lands = handle
    n = len(srcs)

    def body(*refs):
        src_refs, land_refs = refs[:n], refs[n:2 * n]
        send_sems, recv_sems, local_sems = refs[2 * n:2 * n + 3]
        local, remote = _exchange_copies(scatter, src_refs, land_refs, send_sems, recv_sems, local_sems)
        for cp in remote:
            cp.wait_send()
            cp.wait_recv()
        for cp in local:
            cp.wait()

    out = pl.pallas_call(
        body,
        name=name,
        out_shape=tuple(pltpu.HBM(a.shape, a.dtype) for a in srcs + lands),
        in_specs=[_HBM_SPEC] * (2 * n) + [_SEM_SPEC] * 3 + [pl.BlockSpec(memory_space=pl.ANY)],
        out_specs=(_HBM_SPEC,) * (2 * n),
        input_output_aliases={i: i for i in range(2 * n)},
        compiler_params=pltpu.CompilerParams(has_side_effects=_EFFECT),
    )(*srcs, *lands, *sems, after)
    return list(out[n:])


def _sum_parts(parts, *, name, tr=512):
    n, r, lanes = parts.shape
    tr = tr if (r % tr == 0 and r > 1024) else r

    def body(p_ref, g_ref):
        g = p_ref[0].astype(F32)
        for j in range(1, n):
            g = g + p_ref[j].astype(F32)
        g_ref[...] = g

    row = pl.BlockSpec((tr, lanes), lambda i: (i, 0))
    return pl.pallas_call(
        body,
        name=name,
        grid=(r // tr,),
        in_specs=[pl.BlockSpec((n, tr, lanes), lambda i: (0, i, 0))],
        out_specs=row,
        out_shape=jax.ShapeDtypeStruct((r, lanes), F32),
        compiler_params=_params(("parallel",)),
    )(parts)


def _adamw_update(g, w, m, v):
    c1 = 1.0 / (1.0 - ADAM_B1 ** ADAM_STEP)
    c2 = 1.0 / (1.0 - ADAM_B2 ** ADAM_STEP)
    nm = ADAM_B1 * m + (1.0 - ADAM_B1) * g
    nv = ADAM_B2 * v + (1.0 - ADAM_B2) * (g * g)
    return -ADAM_LR * ((nm * c1) / (jnp.sqrt(nv * c2) + ADAM_EPS) + ADAM_WD * w), nm, nv


def _adamw_layer(g, w, m, v, layer, prev, *, name):
    nl, k, n = w.shape
    tr = max([d for d in range(8, min(k, 256) + 1, 8) if k % d == 0] or [k])
    in_parts = g.ndim == 3

    def body(g_ref, w_ref, m_ref, v_ref, *rest):
        go_ref, d_ref, nm_ref, nv_ref = rest[-4:]
        if in_parts:
            gg = g_ref[0].astype(F32)
            for j in range(1, g_ref.shape[0]):
                gg = gg + g_ref[j].astype(F32)
        else:
            gg = g_ref[...]
        d, nm, nv = _adamw_update(gg, w_ref[...], m_ref[...], v_ref[...])
        go_ref[...] = gg
        d_ref[...] = d
        nm_ref[...] = nm
        nv_ref[...] = nv

    lay = pl.BlockSpec((None, tr, n), lambda i: (layer, i, 0))
    n_prev = 0 if prev is None else 4
    out = jax.ShapeDtypeStruct((nl, k, n), F32)
    return pl.pallas_call(
        body,
        name=name,
        grid=(k // tr,),
        in_specs=[pl.BlockSpec((g.shape[0], tr, n), lambda i: (0, i, 0)) if in_parts else pl.BlockSpec((tr, n), lambda i: (i, 0)),
                  lay, lay, lay] + [pl.BlockSpec(memory_space=pl.ANY)] * n_prev,
        out_specs=[lay] * 4,
        out_shape=[out] * 4,
        input_output_aliases={4 + j: j for j in range(n_prev)},
        compiler_params=_params(("parallel",)),
    )(g, w, m, v, *(prev or ()))


def _adamw_packed(g, w, m, v, *, name, tr=512):
    r, lanes = g.shape
    tr = tr if r % tr == 0 else r
    c1 = 1.0 / (1.0 - ADAM_B1 ** ADAM_STEP)
    c2 = 1.0 / (1.0 - ADAM_B2 ** ADAM_STEP)

    def body(g_ref, w_ref, m_ref, v_ref, d_ref, nm_ref, nv_ref):
        g = g_ref[...]
        nm = ADAM_B1 * m_ref[...] + (1.0 - ADAM_B1) * g
        nv = ADAM_B2 * v_ref[...] + (1.0 - ADAM_B2) * (g * g)
        nm_ref[...] = nm
        nv_ref[...] = nv
        d_ref[...] = -ADAM_LR * ((nm * c1) / (jnp.sqrt(nv * c2) + ADAM_EPS) + ADAM_WD * w_ref[...])

    row = pl.BlockSpec((tr, lanes), lambda i: (i, 0))
    out = jax.ShapeDtypeStruct((r, lanes), F32)
    return pl.pallas_call(
        body,
        name=name,
        grid=(r // tr,),
        in_specs=[row] * 4,
        out_specs=[row] * 3,
        out_shape=[out] * 3,
        compiler_params=_params(("parallel",)),
    )(g, w, m, v)


BIG = ("a_w_in", "a_w_out", "b_w_in", "b_w_out", "f_w_up", "f_w_down", "ple_w_proj", "ple_w_gate")
CONVS = ("a_conv", "f_conv")
SMALL = ("norm_mix", "norm_ffn", "norm_ple", "norm_final", "a_log", "a_dt_bias", "a_norm", "b_sinks")
WEIGHTS = ("norm_mix", "norm_ffn", "norm_ple", "norm_final", "a_w_in", "a_conv", "a_log", "a_dt_bias", "a_norm",
           "a_w_out", "b_w_in", "b_sinks", "b_w_out", "f_w_up", "f_conv", "f_w_down", "ple_w_proj", "ple_w_gate")
SLAB_ROW_MULTIPLE = 512


def _pack(arrs, dtype, row_multiple):
    flat = jnp.concatenate([a.reshape(-1).astype(dtype) for a in arrs])
    rows = -(-flat.shape[0] // LANES)
    rows = -(-rows // row_multiple) * row_multiple
    return jnp.pad(flat, (0, rows * LANES - flat.shape[0])).reshape(rows, LANES)


def _unpack(slab, shapes):
    lead = slab.shape[:-2]
    flat = slab.reshape(lead + (-1,))
    out, off = [], 0
    for s in shapes:
        size = math.prod(s)
        out.append(flat[..., off:off + size].reshape(lead + tuple(s)))
        off += size
    return out


def _cols_full(g):
    g = jnp.moveaxis(g, 0, -2)
    return g.reshape(g.shape[:-2] + (g.shape[-2] * g.shape[-1],))


def _rows_full(g):
    g = jnp.moveaxis(g, 0, -3)
    return g.reshape(g.shape[:-3] + (g.shape[-3] * g.shape[-2], g.shape[-1]))


def _cols_split(wfull):
    n = wfull.shape[-1] // N_DEV
    g = wfull.reshape(wfull.shape[:-1] + (N_DEV, n))
    return jnp.moveaxis(g, -2, 0)


def _rows_split(wfull):
    k = wfull.shape[-2] // N_DEV
    g = wfull.reshape(wfull.shape[:-2] + (N_DEV, k, wfull.shape[-1]))
    return jnp.moveaxis(g, -3, 0)


TRANSPOSED = ("a_w_in", "b_w_in", "f_w_up", "ple_w_proj")


def _wire(name, a):
    return jnp.swapaxes(a, -1, -2) if name in TRANSPOSED else a


def _wire_shape(name, shape):
    return shape[:-2] + (shape[-1], shape[-2]) if name in TRANSPOSED else tuple(shape)


def _full(name, g):
    return _cols_full(g) if name in CONVS else _rows_full(g)


def _split(name, wfull):
    return _cols_split(wfull) if name in CONVS else _rows_split(wfull)


def _pack_split(grads, names, dtype, row_multiple):
    flat = jnp.concatenate([_split(n, grads[n]).reshape(N_DEV, -1).astype(dtype) for n in names], axis=1)
    rows = -(-flat.shape[1] // LANES)
    rows = -(-rows // row_multiple) * row_multiple
    return jnp.pad(flat, ((0, 0), (0, rows * LANES - flat.shape[1]))).reshape(N_DEV, rows, LANES)


def _pad_cols(a, width):
    return jnp.pad(a, ((0, 0), (0, width - a.shape[1])))


def kernel(x, p, norm_mix, norm_ffn, norm_ple, norm_final, a_w_in, a_conv, a_log, a_dt_bias, a_norm, a_w_out, b_w_in, b_sinks, b_w_out, f_w_up, f_conv, f_w_down, ple_w_proj, ple_w_gate, loss_target, m_norm_mix, m_norm_ffn, m_norm_ple, m_norm_final, m_a_w_in, m_a_conv, m_a_log, m_a_dt_bias, m_a_norm, m_a_w_out, m_b_w_in, m_b_sinks, m_b_w_out, m_f_w_up, m_f_conv, m_f_w_down, m_ple_w_proj, m_ple_w_gate, v_norm_mix, v_norm_ffn, v_norm_ple, v_norm_final, v_a_w_in, v_a_conv, v_a_log, v_a_dt_bias, v_a_norm, v_a_w_out, v_b_w_in, v_b_sinks, v_b_w_out, v_f_w_up, v_f_conv, v_f_w_down, v_ple_w_proj, v_ple_w_gate):
    wts = dict(norm_mix=norm_mix, norm_ffn=norm_ffn, norm_ple=norm_ple, norm_final=norm_final, a_w_in=a_w_in,
               a_conv=a_conv, a_log=a_log, a_dt_bias=a_dt_bias, a_norm=a_norm, a_w_out=a_w_out, b_w_in=b_w_in,
               b_sinks=b_sinks, b_w_out=b_w_out, f_w_up=f_w_up, f_conv=f_conv, f_w_down=f_w_down,
               ple_w_proj=ple_w_proj, ple_w_gate=ple_w_gate)
    mom = dict(norm_mix=m_norm_mix, norm_ffn=m_norm_ffn, norm_ple=m_norm_ple, norm_final=m_norm_final,
               a_w_in=m_a_w_in, a_conv=m_a_conv, a_log=m_a_log, a_dt_bias=m_a_dt_bias, a_norm=m_a_norm,
               a_w_out=m_a_w_out, b_w_in=m_b_w_in, b_sinks=m_b_sinks, b_w_out=m_b_w_out, f_w_up=m_f_w_up,
               f_conv=m_f_conv, f_w_down=m_f_w_down, ple_w_proj=m_ple_w_proj, ple_w_gate=m_ple_w_gate)
    var = dict(norm_mix=v_norm_mix, norm_ffn=v_norm_ffn, norm_ple=v_norm_ple, norm_final=v_norm_final,
               a_w_in=v_a_w_in, a_conv=v_a_conv, a_log=v_a_log, a_dt_bias=v_a_dt_bias, a_norm=v_a_norm,
               a_w_out=v_a_w_out, b_w_in=v_b_w_in, b_sinks=v_b_sinks, b_w_out=v_b_w_out, f_w_up=v_f_w_up,
               f_conv=v_f_conv, f_w_down=v_f_w_down, ple_w_proj=v_ple_w_proj, ple_w_gate=v_ple_w_gate)
    hk = N_HEADS_A * HEAD_DIM_A
    xs = x[0]
    tgt = loss_target[0]
    p_bf = p.astype(BF16)

    def shard(name, layer):
        return _wire(name, wts[name][layer]).astype(BF16)

    def stacked_rows(g):
        return g.reshape(g.shape[0] * g.shape[1], g.shape[2])

    n_in = a_w_in.shape[-1]
    first = _all_gather(jnp.concatenate([shard("a_w_in", 0), shard("a_w_out", 0)]), name="gather_mixer0")
    wa_in_t = jnp.pad(stacked_rows(first[:, :n_in]), ((0, PROJ_A - PROJ_A_REAL), (0, 0)))
    wa_out = stacked_rows(first[:, n_in:])
    gconv = _all_gather(_pack([wts[n] for n in CONVS], F32, 8), dep=first, name="gather_convs")
    conv_full = {n: _cols_full(g) for n, g in zip(CONVS, _unpack(gconv, [wts[n].shape for n in CONVS]))}
    cv_a, cv_f = conv_full["a_conv"][0], conv_full["f_conv"]
    layer_names = ("f_w_up", "f_w_down", "ple_w_proj", "ple_w_gate")
    gather0, tok = _exchange_start([shard(n, 0) for n in layer_names], scatter=False, name="gather_layer0_start", dep=gconv)
    gather1, tok = _exchange_start([shard(n, 0) for n in ("b_w_in", "b_w_out")] + [shard(n, 1) for n in layer_names],
                                   scatter=False, name="gather_layer1_start", dep=tok)

    alog_row = jnp.pad(a_log, ((0, 0), (N_HEADS_A, LANES - 2 * N_HEADS_A)))
    dtb_row = jnp.pad(a_dt_bias, ((0, 0), (N_HEADS_A, LANES - 2 * N_HEADS_A)))

    tile_f32, tile_bf16, rowsum = (F32, "tile"), (BF16, "tile"), (F32, "rowsum")

    def ffn_ple_fwd(i, h_a, n_f, next_norm, w_up_t, w_down, w_pp_t, w_pg):
        up, y, act = _ffn_up_act(n_f, w_up_t, cv_f[i], name=f"l{i}_ffn_up")
        h_b, n_p = _matmul_rows(act, w_down, _epi_res_norm, [h_a], [norm_ple[i]], [tile_f32, tile_bf16],
                                name=f"l{i}_ffn_down")
        pe = _matmul(p_bf[i, 0], w_pp_t, tb=True, name=f"l{i}_ple_proj")
        res = _matmul_rows(n_p, w_pg, _epi_ple, [h_b, pe], [] if next_norm is None else [next_norm],
                           [tile_f32, tile_f32] + ([] if next_norm is None else [tile_bf16]), name=f"l{i}_ple_gate")
        return res[1], (None if next_norm is None else res[2]), dict(n_f=n_f, up=up, y=y, act=act, h_b=h_b, n_p=n_p, zg=res[0], pe=pe)

    def layer_weights(lands):
        up_t, down, pp_t, pg = (stacked_rows(g) for g in lands)
        return up_t.reshape(2, D_FF, D_MODEL), down, pp_t, pg

    n0 = _rms_fwd(xs, norm_mix[0], name="l0_mix_norm")
    proj = _matmul(n0, wa_in_t, tb=True, tm=512, dep=tok, name="l0_in_proj")
    q, k, v, gbc, bbc, y_qkv = _delta_pre_fwd(proj, cv_a, alog_row, dtb_row, name="l0_delta_pre")
    o, states, tinv, og = _delta_fwd(q, k, v, gbc, bbc, proj, a_norm, name="l0_delta")
    h1, nf0 = _matmul_rows(og, wa_out, _epi_res_norm, [xs], [norm_ffn[0]], [tile_f32, tile_bf16], name="l0_mix_out")
    lw0 = layer_weights(_exchange_wait(gather0, h1, scatter=False, name="gather_layer0_wait"))
    h3, n1, sv0 = ffn_ple_fwd(0, h1, nf0, norm_mix[1], *lw0)

    lands1 = _exchange_wait(gather1, h3, scatter=False, name="gather_layer1_wait")
    wb_in_t, wb_out = stacked_rows(lands1[0]), stacked_rows(lands1[1])
    lw1 = layer_weights(lands1[2:])
    pb = _matmul(n1, wb_in_t, tb=True, name="l1_in_qkv")
    att = _swa_fwd(pb, b_sinks, name="l1_swa")
    h4, nf1 = _matmul_rows(att, wb_out, _epi_res_norm, [h3], [norm_ffn[1]], [tile_f32, tile_bf16], name="l1_mix_out")
    h6, _, sv1 = ffn_ple_fwd(1, h4, nf1, None, *lw1)

    loss_row, dh6, d_norm_final = _final_loss(h6, norm_final, tgt, name="final_loss")
    loss = lax.psum(loss_row[0, 0], MESH_AXES)

    def ffn_ple_bwd(i, dh_c, h_a, sv, lw, dep):
        w_up_t, w_down, _, w_pg = lw
        dzg, dpe = _ple_bwd(dh_c, sv["zg"], sv["pe"], name=f"l{i}_ple_mix_bwd")
        d_pg = _matmul(sv["n_p"], dzg, ta=True, out_dtype=BF16, dep=dep, name=f"l{i}_ple_gate_dw")
        d_pp_t = _matmul(dpe, p_bf[i, 0], ta=True, out_dtype=BF16, name=f"l{i}_ple_proj_dw")
        dh_b, d_np = _matmul_rows(dzg, w_pg, _epi_rms_bwd, [sv["h_b"], dh_c], [norm_ple[i]], [tile_f32, rowsum], tb=True,
                                  name=f"l{i}_ple_gate_dx")
        d_down = _matmul(sv["act"], dh_b, ta=True, out_dtype=BF16, name=f"l{i}_ffn_down_dw")
        dup, d_cg, d_cv = _ffn_down_dx_act_bwd(dh_b, w_down, sv["up"], sv["y"], cv_f[i], name=f"l{i}_ffn_down_dx")
        d_up_t = _matmul(dup, sv["n_f"], ta=True, out_dtype=BF16, name=f"l{i}_ffn_up_dw")
        dh_a, d_nf = _matmul_rows(dup, w_up_t, _epi_rms_bwd, [h_a, dh_b], [norm_ffn[i]], [tile_f32, rowsum],
                                  name=f"l{i}_ffn_up_dx")
        mats = [d_up_t.reshape(2 * D_FF, D_MODEL), d_down, d_pp_t, d_pg]
        return dh_a, mats, dict(norm_ple=d_np, norm_ffn=d_nf, f_conv=jnp.concatenate([d_cg, d_cv], axis=1))

    def slabs(g):
        return g.reshape(N_DEV, g.shape[0] // N_DEV, g.shape[1])

    dh4, mats1, g1 = ffn_ple_bwd(1, dh6, h4, sv1, lw1, None)
    datt = _matmul(dh4, wb_out, tb=True, out_dtype=BF16, name="l1_mix_out_dx")
    d_wb_out = _matmul(att, dh4, ta=True, out_dtype=BF16, name="l1_mix_out_dw")
    dq_b, dk_b, dv_b, dsinks = _swa_bwd(pb, b_sinks, datt, name="l1_swa_bwd")
    dpb = jnp.concatenate([dq_b, dk_b.astype(BF16), dv_b.astype(BF16)], axis=1)
    d_wb_in_t = _matmul(dpb, n1, ta=True, out_dtype=BF16, name="l1_in_qkv_dw")
    send1, tok = _exchange_start([slabs(g) for g in [d_wb_in_t, d_wb_out] + mats1], scatter=True,
                                 name="exchange_layer1_start", dep=d_wb_in_t)
    dh3, d_nm1 = _matmul_rows(dpb, wb_in_t, _epi_rms_bwd, [h3, dh4], [norm_mix[1]], [tile_f32, rowsum], name="l1_in_qkv_dx")

    dh1, mats0, g0 = ffn_ple_bwd(0, dh3, h1, sv0, lw0, tok)
    send0, tok = _exchange_start([slabs(g) for g in mats0], scatter=True, name="exchange_layer0_start", dep=mats0[0])
    dog = _matmul(dh1, wa_out, tb=True, dep=tok, name="l0_mix_out_dx")
    d_wa_out = _matmul(og, dh1, ta=True, out_dtype=BF16, name="l0_mix_out_dw")
    dq, dk, dv, dgbc, dbbc, dz0, d_anorm = _delta_bwd(q, k, v, gbc, bbc, tinv, states, o, proj, a_norm, dog,
                                                      name="l0_delta_bwd")
    dproj, d_aconv, d_alog, d_dtb = _delta_pre_bwd(proj, y_qkv, cv_a, alog_row, dtb_row, dq, dk, dv, dgbc, dbbc, dz0,
                                                   name="l0_delta_pre_bwd")
    d_wa_in_t = _matmul(dproj, n0, ta=True, out_dtype=BF16, name="l0_in_proj_dw")
    sendm, tok = _exchange_start([slabs(d_wa_in_t[:PROJ_A_REAL]), slabs(d_wa_out)], scatter=True,
                                 name="exchange_mixer0_start", dep=d_wa_in_t)
    dx, d_nm0 = _matmul_rows(dproj, wa_in_t, _epi_rms_bwd, [xs, dh1], [norm_mix[0]], [tile_f32, rowsum], dep=tok,
                             name="l0_in_proj_dx")

    recv1 = _exchange_wait(send1, dx, scatter=True, name="exchange_layer1_wait")
    recv0 = _exchange_wait(send0, recv1[0], scatter=True, name="exchange_layer0_wait")
    parts = {("b_w_in", 0): recv1[0], ("b_w_out", 0): recv1[1]}
    parts.update({(n, 1): r for n, r in zip(layer_names, recv1[2:])})
    parts.update({(n, 0): r for n, r in zip(layer_names, recv0)})

    outs = {}

    def update_matrix(name):
        w_, m_, v_ = (_wire(name, a) for a in (wts[name], mom[name], var[name]))
        res = None
        for layer in range(w_.shape[0]):
            res = _adamw_layer(parts[(name, layer)], w_, m_, v_, layer, res, name=f"adamw_{name}_{layer}")
        for kind, arr in zip(("grad", "delta", "new_m", "new_v"), res):
            outs[(kind, name)] = _wire(name, arr)
        return res

    last = [update_matrix(n) for n in ("b_w_in", "b_w_out") + layer_names][-1]
    recvm = _exchange_wait(sendm, last[0], scatter=True, name="exchange_mixer0_wait")
    parts.update({("a_w_in", 0): recvm[0], ("a_w_out", 0): recvm[1]})
    update_matrix("a_w_in")
    update_matrix("a_w_out")

    gconvs = dict(a_conv=d_aconv[None], f_conv=jnp.stack([g0["f_conv"], g1["f_conv"]]))
    small_g = dict(norm_mix=jnp.concatenate([d_nm0, d_nm1]), norm_ffn=jnp.concatenate([g0["norm_ffn"], g1["norm_ffn"]]),
                   norm_ple=jnp.concatenate([g0["norm_ple"], g1["norm_ple"]]), norm_final=d_norm_final[0],
                   a_log=d_alog[:, N_HEADS_A:2 * N_HEADS_A], a_dt_bias=d_dtb[:, N_HEADS_A:2 * N_HEADS_A],
                   a_norm=d_anorm, b_sinks=dsinks[:, :N_HEADS_B])
    recv_conv = _all_to_all(_pack_split(gconvs, CONVS, F32, 8), name="exchange_conv_grads")
    recv_small = _all_gather(_pack([small_g[n] for n in SMALL], F32, 8), name="gather_small_grads")
    for names, recv, tag in ((CONVS, recv_conv, "convs"), (SMALL, recv_small, "small")):
        shapes = [wts[n].shape for n in names]
        g_slab = _sum_parts(recv, name=f"sum_{tag}")
        packed = [_pack([d[n] for n in names], F32, 8) for d in (wts, mom, var)]
        res = _adamw_packed(g_slab, *packed, name=f"adamw_{tag}")
        for kind, slab in zip(("grad", "delta", "new_m", "new_v"), (g_slab,) + tuple(res)):
            for n, arr in zip(names, _unpack(slab, shapes)):
                outs[(kind, n)] = arr

    result = [loss, dx[None]]
    for kind in ("grad", "delta", "new_m", "new_v"):
        result += [outs[(kind, n)] for n in WEIGHTS]
    return tuple(result)
```

```python
import functools
import math

import jax
import jax.numpy as jnp
from jax import lax
from jax.experimental import pallas as pl
from jax.experimental.pallas import tpu as pltpu

F32 = jnp.float32
BF16 = jnp.bfloat16

D_MODEL = 1024
N_HEADS_A = 8
HEAD_DIM_A = 128
CONV_A = 4
CHUNK = 128
N_HEADS_B = 16
N_KV_B = 4
GROUP_B = N_HEADS_B // N_KV_B
HEAD_DIM_B = 64
WINDOW = 128
D_FF = 2816
FFN_CONV = 3
PLE_DIM = 256
EPS = 1e-6
N_DEV = 8
HALO = 8
PROJ_A_REAL = 4 * N_HEADS_A * HEAD_DIM_A + 2 * N_HEADS_A
PROJ_A = 4 * N_HEADS_A * HEAD_DIM_A + 128
Z_COL_BLOCK = 3
BA_COL_BLOCK = 32

ADAM_LR = 0.001
ADAM_B1 = 0.9
ADAM_B2 = 0.999
ADAM_EPS = 1e-08
ADAM_WD = 0.01
ADAM_STEP = 10

LANES = 128
VMEM_LIMIT_BYTES = 56 * 1024 * 1024
NEG_BIG = -1e30

MESH_AXES = ("x", "y", "c")


def _params(sem=None):
    return pltpu.CompilerParams(dimension_semantics=sem, vmem_limit_bytes=VMEM_LIMIT_BYTES)


def _tile(n, target):
    best = None
    for t in range(LANES, min(n, target) + 1, LANES):
        if n % t == 0:
            best = t
    return best or n


def _sigmoid(x):
    return 0.5 * jnp.tanh(0.5 * x) + 0.5


def _softplus(x):
    return jnp.maximum(x, 0.0) + jnp.log1p(jnp.exp(-jnp.abs(x)))


def _matmul(a, b, *, name, ta=False, tb=False, res=None, out_dtype=F32, tm=1408, tn=1408, tk=None, dep=None):
    sa, sb = a.ndim == 3, b.ndim == 3
    ns = a.shape[0] if sa else (b.shape[0] if sb else 1)
    contract_stack = sa and sb
    out_stacked = sa != sb
    m = a.shape[-1] if ta else a.shape[-2]
    k = a.shape[-2] if ta else a.shape[-1]
    n = b.shape[-2] if tb else b.shape[-1]
    assert (b.shape[-1] if tb else b.shape[-2]) == k, (a.shape, b.shape, ta, tb)
    if tk is None:
        tk = 1024 if ta else 2816
    tm, tn, tk = _tile(m, tm), _tile(n, tn), _tile(k, tk)
    nk = k // tk
    nsteps = nk * (ns if contract_stack else 1)
    dims = (((0 if ta else 1,), (1 if tb else 0,)), ((), ()))

    def spec(block, stacked, order):
        def index(g, i, j, kk):
            two = order(i, j, kk % nk)
            if not stacked:
                return two
            return (kk // nk if contract_stack else g,) + two
        return pl.BlockSpec(((None,) if stacked else ()) + block, index)

    a_spec = spec((tk, tm), sa, lambda i, j, kq: (kq, i)) if ta else spec((tm, tk), sa, lambda i, j, kq: (i, kq))
    b_spec = spec((tn, tk), sb, lambda i, j, kq: (j, kq)) if tb else spec((tk, tn), sb, lambda i, j, kq: (kq, j))
    o_spec = spec((tm, tn), out_stacked, lambda i, j, kq: (i, j))
    has_res = res is not None
    has_dep = dep is not None

    def body(*refs):
        a_ref, b_ref = refs[0], refs[1]
        r_ref = refs[2] if has_res else None
        o_ref = refs[2 + has_res + has_dep]
        part = lax.dot_general(a_ref[...].astype(BF16), b_ref[...].astype(BF16), dims, preferred_element_type=F32)

        def finish(acc):
            if has_res:
                acc = acc + r_ref[...].astype(F32)
            o_ref[...] = acc.astype(out_dtype)

        if nsteps == 1:
            finish(part)
        else:
            acc_ref = refs[-1]
            kk = pl.program_id(3)

            @pl.when(kk == 0)
            def _():
                acc_ref[...] = part

            @pl.when(kk > 0)
            def _():
                acc_ref[...] += part

            @pl.when(kk == nsteps - 1)
            def _():
                finish(acc_ref[...])

    in_specs = [a_spec, b_spec] + ([o_spec] if has_res else []) + ([pl.BlockSpec(memory_space=pl.ANY)] if has_dep else [])
    args = (a, b) + ((res,) if has_res else ()) + ((dep,) if has_dep else ())
    return pl.pallas_call(
        body,
        name=name,
        grid=(ns if out_stacked else 1, m // tm, n // tn, nsteps),
        in_specs=in_specs,
        out_specs=o_spec,
        out_shape=jax.ShapeDtypeStruct(((ns,) if out_stacked else ()) + (m, n), out_dtype),
        scratch_shapes=[pltpu.VMEM((tm, tn), F32)] if nsteps > 1 else [],
        compiler_params=_params(("parallel", "parallel", "parallel", "arbitrary")),
    )(*args)


EPI_ROWS = 32


def _matmul_rows(a, b, epilogue, tiles_in, rows_in, outs, *, name, tb=False, tm=512, tk=None, dep=None):
    stacked = a.ndim == 3
    ns = a.shape[0] if stacked else 1
    m, k = a.shape[-2], a.shape[-1]
    n = b.shape[-2] if tb else b.shape[-1]
    assert (b.shape[-1] if tb else b.shape[-2]) == k and (b.ndim == 3) == stacked, (a.shape, b.shape, tb)
    tm, tk = _tile(m, tm), _tile(k, 2816 if tk is None else tk)
    nk = k // tk
    nsteps = nk * ns
    dims = (((1,), (1 if tb else 0,)), ((), ()))
    lead = (None,) if stacked else ()
    front = (lambda kk: (kk // nk,)) if stacked else (lambda kk: ())
    a_spec = pl.BlockSpec(lead + (tm, tk), lambda i, kk: front(kk) + (i, kk % nk))
    if tb:
        b_spec = pl.BlockSpec(lead + (n, tk), lambda i, kk: front(kk) + (0, kk % nk))
    else:
        b_spec = pl.BlockSpec(lead + (tk, n), lambda i, kk: front(kk) + (kk % nk, 0))
    tile_spec = pl.BlockSpec((tm, n), lambda i, kk: (i, 0))
    row_spec = pl.BlockSpec((1, n), lambda i, kk: (0, 0))
    n_t, n_r, has_dep = len(tiles_in), len(rows_in), dep is not None

    def body(*refs):
        a_ref, b_ref = refs[:2]
        tile_refs = refs[2:2 + n_t]
        row_refs = refs[2 + n_t:2 + n_t + n_r]
        out_refs = refs[2 + n_t + n_r + has_dep:-1]
        acc_ref = refs[-1]
        part = lax.dot_general(a_ref[...].astype(BF16), b_ref[...].astype(BF16), dims, preferred_element_type=F32)
        kk = pl.program_id(1)
        if nsteps == 1:
            acc_ref[...] = part
        else:
            @pl.when(kk == 0)
            def _():
                acc_ref[...] = part

            @pl.when(kk > 0)
            def _():
                acc_ref[...] += part

        @pl.when(kk == nsteps - 1)
        def _():
            epilogue(acc_ref, tile_refs, row_refs, out_refs, pl.program_id(0) == 0)

    return pl.pallas_call(
        body,
        name=name,
        grid=(m // tm, nsteps),
        in_specs=[a_spec, b_spec] + [tile_spec] * n_t + [row_spec] * n_r + ([pl.BlockSpec(memory_space=pl.ANY)] if has_dep else []),
        out_specs=[tile_spec if kind == "tile" else row_spec for _, kind in outs],
        out_shape=[jax.ShapeDtypeStruct((m, n) if kind == "tile" else (1, n), dt) for dt, kind in outs],
        scratch_shapes=[pltpu.VMEM((tm, n), F32)],
        compiler_params=_params(("arbitrary", "arbitrary")),
    )(a, b, *tiles_in, *[r.reshape(1, n) for r in rows_in], *((dep,) if has_dep else ()))


def _row_chunks(ref):
    return [pl.ds(r, EPI_ROWS) for r in range(0, ref.shape[0], EPI_ROWS)]


def _rstd(x):
    return lax.rsqrt(jnp.mean(x * x, axis=-1, keepdims=True) + EPS)


def _epi_res_norm(acc, tiles, rows, outs, first):
    (res,), (w,), (h_out, n_out) = tiles, rows, outs
    chunks = _row_chunks(acc)
    for rs in chunks:
        h_out[rs, :] = acc[rs, :] + res[rs, :]
    rstds = [_rstd(h_out[rs, :]) for rs in chunks]
    for rs, r in zip(chunks, rstds):
        n_out[rs, :] = (h_out[rs, :] * r * w[...]).astype(BF16)


def _epi_ple(acc, tiles, rows, outs, first):
    hb, pe = tiles
    chunks = _row_chunks(acc)
    for rs in chunks:
        zg = acc[rs, :]
        outs[0][rs, :] = zg
        outs[1][rs, :] = hb[rs, :] + _sigmoid(zg) * pe[rs, :]
    if rows:
        rstds = [_rstd(outs[1][rs, :]) for rs in chunks]
        for rs, r in zip(chunks, rstds):
            outs[2][rs, :] = (outs[1][rs, :] * r * rows[0][...]).astype(BF16)


def _epi_rms_bwd(acc, tiles, rows, outs, first):
    (h_ref, skip), (w,), (dh_out, dw_out) = tiles, rows, outs
    chunks = _row_chunks(acc)
    n = acc.shape[1]
    rstds = [_rstd(h_ref[rs, :]) for rs in chunks]
    dots = [jnp.sum(acc[rs, :] * w[...] * h_ref[rs, :], axis=-1, keepdims=True) * r * (1.0 / n)
            for rs, r in zip(chunks, rstds)]
    dw = jnp.zeros((1, n), F32)
    for rs, r, dt in zip(chunks, rstds, dots):
        nh = h_ref[rs, :] * r
        g = acc[rs, :]
        dh_out[rs, :] = r * (g * w[...] - nh * dt) + skip[rs, :]
        dw = dw + jnp.sum(g * nh, axis=0, keepdims=True)

    @pl.when(first)
    def _():
        dw_out[...] = dw

    @pl.when(jnp.logical_not(first))
    def _():
        dw_out[...] += dw


def _rms_fwd(h, w, *, name, tm=512):
    t, d = h.shape
    tm = _tile(t, tm)

    def body(h_ref, w_ref, o_ref):
        x = h_ref[...]
        r = lax.rsqrt(jnp.mean(x * x, axis=-1, keepdims=True) + EPS)
        o_ref[...] = (x * r * w_ref[...]).astype(BF16)

    return pl.pallas_call(
        body,
        name=name,
        grid=(t // tm,),
        in_specs=[pl.BlockSpec((tm, d), lambda i: (i, 0)), pl.BlockSpec((1, d), lambda i: (0, 0))],
        out_specs=pl.BlockSpec((tm, d), lambda i: (i, 0)),
        out_shape=jax.ShapeDtypeStruct((t, d), BF16),
        compiler_params=_params(("parallel",)),
    )(h, w.reshape(1, d))


def _rms_bwd(h, w, dn, skip, *, name, tm=512):
    t, d = h.shape
    tm = _tile(t, tm)

    def body(h_ref, w_ref, dn_ref, skip_ref, dh_ref, dw_ref):
        i = pl.program_id(0)
        x = h_ref[...]
        r = lax.rsqrt(jnp.mean(x * x, axis=-1, keepdims=True) + EPS)
        nh = x * r
        g = dn_ref[...].astype(F32)
        gw = g * w_ref[...]
        dh_ref[...] = r * (gw - nh * jnp.mean(gw * nh, axis=-1, keepdims=True)) + skip_ref[...]
        part = jnp.sum(g * nh, axis=0, keepdims=True)

        @pl.when(i == 0)
        def _():
            dw_ref[...] = part

        @pl.when(i > 0)
        def _():
            dw_ref[...] += part

    row = pl.BlockSpec((tm, d), lambda i: (i, 0))
    vec = pl.BlockSpec((1, d), lambda i: (0, 0))
    return pl.pallas_call(
        body,
        name=name,
        grid=(t // tm,),
        in_specs=[row, vec, row, row],
        out_specs=[row, vec],
        out_shape=[jax.ShapeDtypeStruct((t, d), F32), jax.ShapeDtypeStruct((1, d), F32)],
        compiler_params=_params(("arbitrary",)),
    )(h, w.reshape(1, d), dn, skip)


def _final_loss(h, w, target, *, name, tm=512):
    t, d = h.shape
    tm = _tile(t, tm)

    def body(h_ref, w_ref, tg_ref, loss_ref, dh_ref, dw_ref):
        i = pl.program_id(0)
        x = h_ref[...]
        r = lax.rsqrt(jnp.mean(x * x, axis=-1, keepdims=True) + EPS)
        nh = x * r
        err = nh * w_ref[...] - tg_ref[...]
        lpart = (0.5 / d) * jnp.sum(jnp.sum(err * err, axis=-1, keepdims=True), axis=0, keepdims=True)
        g = err * (1.0 / d)
        gw = g * w_ref[...]
        dh_ref[...] = r * (gw - nh * jnp.mean(gw * nh, axis=-1, keepdims=True))
        part = jnp.sum(g * nh, axis=0, keepdims=True)
        lrow = jnp.broadcast_to(lpart, (1, LANES))

        @pl.when(i == 0)
        def _():
            dw_ref[...] = part
            loss_ref[...] = lrow

        @pl.when(i > 0)
        def _():
            dw_ref[...] += part
            loss_ref[...] += lrow

    row = pl.BlockSpec((tm, d), lambda i: (i, 0))
    vec = pl.BlockSpec((1, d), lambda i: (0, 0))
    return pl.pallas_call(
        body,
        name=name,
        grid=(t // tm,),
        in_specs=[row, vec, row],
        out_specs=[pl.BlockSpec((1, LANES), lambda i: (0, 0)), row, vec],
        out_shape=[jax.ShapeDtypeStruct((1, LANES), F32), jax.ShapeDtypeStruct((t, d), F32), jax.ShapeDtypeStruct((1, d), F32)],
        compiler_params=_params(("arbitrary",)),
    )(h, w.reshape(1, d), target)


def _conv_from_ext(ext_ref, cw_ref, kw, tm):
    y = cw_ref[kw - 1:kw, :] * ext_ref[pl.ds(HALO, tm), :]
    for i in range(kw - 1):
        y = y + cw_ref[i:i + 1, :] * ext_ref[pl.ds(HALO - (kw - 1) + i, tm), :]
    return y


ROW_CHUNK = 64


def _shifted_rows(src_ref, base, rows, shifts, cols=slice(None)):
    ext = src_ref[pl.ds(base - HALO, rows + HALO), cols]
    return [ext[HALO:, :] if s == 0 else pltpu.roll(ext, s, 0)[HALO:, :] for s in shifts]


def _conv_rows(src_ref, base, rows, cw_ref, kw, cols=slice(None)):
    wins = _shifted_rows(src_ref, base, rows, range(kw), cols)
    y = cw_ref[kw - 1:kw, cols] * wins[0]
    for s in range(1, kw):
        y = y + cw_ref[kw - 1 - s:kw - s, cols] * wins[s]
    return y


def _ahead_rows(src_ref, base, rows, shifts, cols=slice(None)):
    ext = src_ref[pl.ds(base, rows + HALO), cols]
    return [ext[:rows, :] if s == 0 else pltpu.roll(ext, rows + HALO - s, 0)[:rows, :] for s in shifts]


def _conv_t_rows(dy_ref, base, rows, cw_ref, kw, cols=slice(None)):
    wins = _ahead_rows(dy_ref, base, rows, range(kw), cols)
    dx = cw_ref[kw - 1:kw, cols] * wins[0]
    for s in range(1, kw):
        dx = dx + cw_ref[kw - 1 - s:kw - s, cols] * wins[s]
    return dx


def _fold_rows(x):
    out = x[0:HALO, :]
    for g in range(1, x.shape[0] // HALO):
        out = out + x[g * HALO:(g + 1) * HALO, :]
    return out


def _conv_bwd_from_ext(xext_ref, dyext_ref, cw_ref, dcw_ref, kw, tm, first):
    dy = dyext_ref[pl.ds(0, tm), :]
    dx = cw_ref[kw - 1:kw, :] * dy
    for i in range(kw - 1):
        dx = dx + cw_ref[i:i + 1, :] * dyext_ref[pl.ds(kw - 1 - i, tm), :]
    for i in range(kw):
        part = jnp.sum(dy * xext_ref[pl.ds(HALO - (kw - 1) + i, tm), :], axis=0, keepdims=True)

        @pl.when(first)
        def _():
            dcw_ref[i:i + 1, :] = part

        @pl.when(jnp.logical_not(first))
        def _():
            dcw_ref[i:i + 1, :] += part

    return dx


def _delta_pre_fwd(proj, conv_w, alog_row, dtb_row, *, name, tm=256):
    t = proj.shape[0]
    c3 = 3 * N_HEADS_A * HEAD_DIM_A
    hk = N_HEADS_A * HEAD_DIM_A
    tm = _tile(t, tm)

    rc = min(ROW_CHUNK, tm)

    def body(x_ref, ba_ref, cw_ref, al_ref, db_ref, q_ref, k_ref, v_ref, g_ref, b_ref, y_ref, hx):
        i = pl.program_id(0)

        @pl.when(i == 0)
        def _():
            hx[0:HALO, :] = jnp.zeros((HALO, c3), F32)

        hx[pl.ds(HALO, rc), :] = x_ref[0:rc, :]
        dsts = (q_ref, k_ref, v_ref)
        for r in range(tm // rc):
            rows = slice(r * rc, (r + 1) * rc)
            src, base = (hx, HALO) if r == 0 else (x_ref, r * rc)
            for kind in range(3):
                cbs = range(kind * N_HEADS_A, (kind + 1) * N_HEADS_A)
                ss = []
                for cb in cbs:
                    cols = slice(cb * HEAD_DIM_A, (cb + 1) * HEAD_DIM_A)
                    y = _conv_rows(src, base, rc, cw_ref, CONV_A, cols)
                    y_ref[rows, cols] = y
                    ss.append(y * _sigmoid(y))
                if kind < 2:
                    norms = [lax.rsqrt(jnp.sum(s * s, axis=-1, keepdims=True) + EPS) for s in ss]
                    ss = [s * nrm for s, nrm in zip(ss, norms)]
                for h, s in enumerate(ss):
                    dsts[kind][rows, h * HEAD_DIM_A:(h + 1) * HEAD_DIM_A] = s
        hx[0:HALO, :] = x_ref[tm - HALO:tm, :]
        ba = ba_ref[...]
        beta = _sigmoid(ba)
        gfull = -jnp.exp(al_ref[...]) * _softplus(ba + db_ref[...])
        for h in range(N_HEADS_A):
            lo = h * HEAD_DIM_A
            b_ref[:, lo:lo + HEAD_DIM_A] = jnp.broadcast_to(beta[:, h:h + 1], (tm, HEAD_DIM_A))
            g_ref[:, lo:lo + HEAD_DIM_A] = jnp.broadcast_to(gfull[:, N_HEADS_A + h:N_HEADS_A + h + 1], (tm, HEAD_DIM_A))

    row = lambda w: pl.BlockSpec((tm, w), lambda i: (i, 0))
    fixed = lambda r, w: pl.BlockSpec((r, w), lambda i: (0, 0))
    out = jax.ShapeDtypeStruct((t, hk), F32)
    return pl.pallas_call(
        body,
        name=name,
        grid=(t // tm,),
        in_specs=[row(c3), pl.BlockSpec((tm, LANES), lambda i: (i, BA_COL_BLOCK)), fixed(CONV_A, c3), fixed(1, LANES),
                  fixed(1, LANES)],
        out_specs=[row(hk)] * 5 + [row(c3)],
        out_shape=[out] * 5 + [jax.ShapeDtypeStruct((t, c3), F32)],
        scratch_shapes=[pltpu.VMEM((HALO + rc, c3), F32)],
        compiler_params=_params(("arbitrary",)),
    )(proj, proj, conv_w, alog_row, dtb_row)


def _delta_pre_bwd(proj, y, conv_w, alog_row, dtb_row, dq, dk, dv, dg, db, dz, *, name, tm=256):
    t, pw = proj.shape
    c3 = 3 * N_HEADS_A * HEAD_DIM_A
    hk = N_HEADS_A * HEAD_DIM_A
    tm = _tile(t, tm)
    nt = t // tm

    rc = min(ROW_CHUNK, tm)
    kw = CONV_A

    def body(x_ref, y_ref, ba_ref, cw_ref, al_ref, db_ref, dq_ref, dk_ref, dv_ref, dg_ref, dbt_ref, dz_ref,
             dp_ref, dcw_ref, dal_ref, ddb_ref, *scratch):
        dys, acc = scratch[:-1], scratch[-1]
        i = pl.program_id(0)
        first = i == 0

        @pl.when(first)
        def _():
            for dyb in dys:
                dyb[pl.ds(tm, HALO), :] = jnp.zeros((HALO, HEAD_DIM_A), F32)

        srcs = (dq_ref, dk_ref, dv_ref)
        ncb = c3 // HEAD_DIM_A
        taps = [[jnp.zeros((HALO, HEAD_DIM_A), F32) for _ in range(kw)] for _ in range(ncb)]
        for r in reversed(range(tm // rc)):
            rows = slice(r * rc, (r + 1) * rc)
            for kind in range(3):
                cbs = range(kind * N_HEADS_A, (kind + 1) * N_HEADS_A)
                hs = [slice(h * HEAD_DIM_A, (h + 1) * HEAD_DIM_A) for h in range(N_HEADS_A)]
                ys = [y_ref[rows, cb * HEAD_DIM_A:(cb + 1) * HEAD_DIM_A] for cb in cbs]
                sgs = [_sigmoid(yv) for yv in ys]
                dss = [srcs[kind][rows, hsl] for hsl in hs]
                if kind < 2:
                    ss = [yv * sg for yv, sg in zip(ys, sgs)]
                    rns = [lax.rsqrt(jnp.sum(s * s, axis=-1, keepdims=True) + EPS) for s in ss]
                    qns = [s * rn for s, rn in zip(ss, rns)]
                    dots = [jnp.sum(ds * qn, axis=-1, keepdims=True) for ds, qn in zip(dss, qns)]
                    dss = [rn * (ds - qn * dt) for rn, ds, qn, dt in zip(rns, dss, qns, dots)]
                for cb, yv, sg, ds in zip(cbs, ys, sgs, dss):
                    cols = slice(cb * HEAD_DIM_A, (cb + 1) * HEAD_DIM_A)
                    dys[cb][rows, :] = ds * (sg * (1.0 + yv * (1.0 - sg)))
                    ahead = _ahead_rows(dys[cb], r * rc, rc, range(kw))
                    dp_ref[rows, cols] = sum(cw_ref[kw - 1 - s:kw - s, cols] * ahead[s] for s in range(kw)).astype(BF16)
                    xv = x_ref[rows, cols]
                    for s in range(kw):
                        taps[cb][kw - 1 - s] = taps[cb][kw - 1 - s] + _fold_rows(xv * ahead[s])
        for cb in range(ncb):
            cols = slice(cb * HEAD_DIM_A, (cb + 1) * HEAD_DIM_A)
            for j in range(kw):
                acc[j * HALO:(j + 1) * HALO, cols] = taps[cb][j]
            dys[cb][pl.ds(tm, HALO), :] = dys[cb][0:HALO, :]
        for j in range(kw):
            tap = jnp.sum(acc[j * HALO:(j + 1) * HALO, :], axis=0, keepdims=True)

            @pl.when(first)
            def _():
                dcw_ref[j:j + 1, :] = tap

            @pl.when(jnp.logical_not(first))
            def _():
                dcw_ref[j:j + 1, :] += tap

        dp_ref[:, c3:c3 + hk] = dz_ref[...]

        lane = lax.broadcasted_iota(jnp.int32, (tm, LANES), 1)
        gcol = jnp.zeros((tm, LANES), F32)
        for h in range(N_HEADS_A):
            lo = h * HEAD_DIM_A
            dbh = jnp.sum(dbt_ref[:, lo:lo + HEAD_DIM_A], axis=-1, keepdims=True)
            dgh = jnp.sum(dg_ref[:, lo:lo + HEAD_DIM_A], axis=-1, keepdims=True)
            gcol = gcol + jnp.where(lane == h, dbh, 0.0) + jnp.where(lane == N_HEADS_A + h, dgh, 0.0)
        ba = ba_ref[...]
        beta = _sigmoid(ba)
        a_neg = -jnp.exp(al_ref[...])
        z = ba + db_ref[...]
        dz = gcol * a_neg * _sigmoid(z)
        is_g = jnp.logical_and(lane >= N_HEADS_A, lane < 2 * N_HEADS_A)
        dba = jnp.where(lane < N_HEADS_A, gcol * beta * (1.0 - beta), jnp.where(is_g, dz, 0.0))
        dp_ref[:, c3 + hk:pw] = dba.astype(BF16)
        dal = jnp.sum(jnp.where(is_g, gcol * a_neg * _softplus(z), 0.0), axis=0, keepdims=True)
        ddb = jnp.sum(jnp.where(is_g, dz, 0.0), axis=0, keepdims=True)

        @pl.when(first)
        def _():
            dal_ref[...] = dal
            ddb_ref[...] = ddb

        @pl.when(jnp.logical_not(first))
        def _():
            dal_ref[...] += dal
            ddb_ref[...] += ddb

    rev = lambda w: pl.BlockSpec((tm, w), lambda i: (nt - 1 - i, 0))
    fixed = lambda r, w: pl.BlockSpec((r, w), lambda i: (0, 0))
    return pl.pallas_call(
        body,
        name=name,
        grid=(nt,),
        in_specs=[rev(c3), rev(c3), pl.BlockSpec((tm, LANES), lambda i: (nt - 1 - i, BA_COL_BLOCK)), fixed(CONV_A, c3),
                  fixed(1, LANES), fixed(1, LANES)] + [rev(hk)] * 6,
        out_specs=[rev(pw), fixed(CONV_A, c3), fixed(1, LANES), fixed(1, LANES)],
        out_shape=[jax.ShapeDtypeStruct((t, pw), BF16), jax.ShapeDtypeStruct((CONV_A, c3), F32),
                   jax.ShapeDtypeStruct((1, LANES), F32), jax.ShapeDtypeStruct((1, LANES), F32)],
        scratch_shapes=[pltpu.VMEM((tm + HALO, HEAD_DIM_A), F32)] * (c3 // HEAD_DIM_A) + [pltpu.VMEM((CONV_A * HALO, c3), F32)],
        compiler_params=_params(("arbitrary",)),
    )(proj, y, proj, conv_w, alog_row, dtb_row, dq, dk, dv, dg, db, dz)


def _gated_norm_fwd(o, proj, w, *, name, tm=512):
    t, d = o.shape
    tm = _tile(t, tm)

    def body(o_ref, z_ref, w_ref, y_ref):
        for h in range(N_HEADS_A):
            sl = slice(h * HEAD_DIM_A, (h + 1) * HEAD_DIM_A)
            oh = o_ref[:, sl]
            zh = z_ref[:, sl]
            r = lax.rsqrt(jnp.mean(oh * oh, axis=-1, keepdims=True) + EPS)
            y_ref[:, sl] = (oh * r * w_ref[...] * (zh * _sigmoid(zh))).astype(BF16)

    row = pl.BlockSpec((tm, d), lambda i: (i, 0))
    return pl.pallas_call(
        body,
        name=name,
        grid=(t // tm,),
        in_specs=[row, pl.BlockSpec((tm, d), lambda i: (i, Z_COL_BLOCK)), pl.BlockSpec((1, HEAD_DIM_A), lambda i: (0, 0))],
        out_specs=row,
        out_shape=jax.ShapeDtypeStruct((t, d), BF16),
        compiler_params=_params(("parallel",)),
    )(o, proj, w)


def _gated_norm_bwd(o, proj, w, dy, *, name, tm=512):
    t, d = o.shape
    tm = _tile(t, tm)

    def body(o_ref, z_ref, w_ref, dy_ref, do_ref, dz_ref, dw_ref):
        i = pl.program_id(0)
        dw = jnp.zeros((1, HEAD_DIM_A), F32)
        for h in range(N_HEADS_A):
            sl = slice(h * HEAD_DIM_A, (h + 1) * HEAD_DIM_A)
            oh = o_ref[:, sl]
            zh = z_ref[:, sl]
            g = dy_ref[:, sl]
            r = lax.rsqrt(jnp.mean(oh * oh, axis=-1, keepdims=True) + EPS)
            nh = oh * r
            sg = _sigmoid(zh)
            dz_ref[:, sl] = (g * nh * w_ref[...] * (sg * (1.0 + zh * (1.0 - sg)))).astype(BF16)
            dt = g * (zh * sg)
            dw = dw + jnp.sum(dt * nh, axis=0, keepdims=True)
            dnh = dt * w_ref[...]
            do_ref[:, sl] = r * (dnh - nh * jnp.mean(dnh * nh, axis=-1, keepdims=True))

        @pl.when(i == 0)
        def _():
            dw_ref[...] = dw

        @pl.when(i > 0)
        def _():
            dw_ref[...] += dw

    row = pl.BlockSpec((tm, d), lambda i: (i, 0))
    vec = pl.BlockSpec((1, HEAD_DIM_A), lambda i: (0, 0))
    return pl.pallas_call(
        body,
        name=name,
        grid=(t // tm,),
        in_specs=[row, pl.BlockSpec((tm, d), lambda i: (i, Z_COL_BLOCK)), vec, row],
        out_specs=[row, row, vec],
        out_shape=[jax.ShapeDtypeStruct((t, d), F32), jax.ShapeDtypeStruct((t, d), BF16),
                   jax.ShapeDtypeStruct((1, HEAD_DIM_A), F32)],
        compiler_params=_params(("arbitrary",)),
    )(o, proj, w, dy)


_NN = (((1,), (0,)), ((), ()))
_NT = (((1,), (1,)), ((), ()))
_TN = (((0,), (0,)), ((), ()))
_DIMS = {"nn": _NN, "nt": _NT, "tn": _TN}


def _raw_dot(a, b, kind, prec):
    dims = _DIMS[kind]
    a_hi, b_hi = a.astype(BF16), b.astype(BF16)
    out = lax.dot_general(a_hi, b_hi, dims, preferred_element_type=F32)
    if prec == "x3":
        a_lo = (a - a_hi.astype(F32)).astype(BF16)
        b_lo = (b - b_hi.astype(F32)).astype(BF16)
        out = out + lax.dot_general(a_hi, b_lo, dims, preferred_element_type=F32)
        out = out + lax.dot_general(a_lo, b_hi, dims, preferred_element_type=F32)
    elif prec == "s3":
        r1 = b - b_hi.astype(F32)
        b_mid = r1.astype(BF16)
        b_lo = (r1 - b_mid.astype(F32)).astype(BF16)
        out = out + lax.dot_general(a_hi, b_mid, dims, preferred_element_type=F32)
        out = out + lax.dot_general(a_hi, b_lo, dims, preferred_element_type=F32)
    return out


def _raw_dots(xs, ys, kind, prec):
    return [_raw_dot(x, y, kind, prec) for x, y in zip(xs, ys)]


@functools.partial(jax.custom_vjp, nondiff_argnums=(2, 3))
def _dots(xs, ys, kind, prec):
    return _raw_dots(xs, ys, kind, prec)


def _dots_fwd(xs, ys, kind, prec):
    return _raw_dots(xs, ys, kind, prec), (xs, ys)


def _dots_bwd(kind, prec, saved, gs):
    xs, ys = saved
    if kind == "nn":
        return _raw_dots(gs, ys, "nt", prec), _raw_dots(xs, gs, "tn", prec)
    if kind == "nt":
        return _raw_dots(gs, ys, "nn", prec), _raw_dots(gs, xs, "tn", prec)
    return _raw_dots(ys, gs, "nt", prec), _raw_dots(xs, gs, "nn", prec)


_dots.defvjp(_dots_fwd, _dots_bwd)


def _eye(c):
    return (lax.broadcasted_iota(jnp.int32, (c, c), 0) == lax.broadcasted_iota(jnp.int32, (c, c), 1)).astype(F32)


def _inv_unit_lower_raw(lmats):
    c = lmats[0].shape[0]
    eye = _eye(c)
    xs = [eye - l for l in lmats]
    ps = lmats
    for _ in range(int(math.log2(c)) - 1):
        ps = _raw_dots(ps, ps, "nn", "bf16")
        xs = [x + d for x, d in zip(xs, _raw_dots(xs, ps, "nn", "bf16"))]
    rs = [x - eye + d for x, d in zip(xs, _raw_dots(lmats, xs, "nn", "x3"))]
    return [x - d for x, d in zip(xs, _raw_dots(xs, rs, "nn", "bf16"))]


@jax.custom_vjp
def _inv_unit_lower(lmats, hints):
    return _inv_unit_lower_raw(lmats) if hints is None else hints


def _inv_fwd(lmats, hints):
    tms = _inv_unit_lower_raw(lmats) if hints is None else hints
    return tms, (tms, hints)


def _inv_bwd(saved, gs):
    tms, hints = saved
    ds = [-d for d in _raw_dots(_raw_dots(tms, gs, "tn", "x3"), tms, "nt", "x3")]
    return ds, (None if hints is None else [jnp.zeros_like(h) for h in hints])


_inv_unit_lower.defvjp(_inv_fwd, _inv_bwd)


def _delta_prep(qs, ks, vs, gs, bs, hints=None):
    c = qs[0].shape[0]
    nh = len(qs)
    ii = lax.broadcasted_iota(jnp.int32, (c, c), 0)
    jj = lax.broadcasted_iota(jnp.int32, (c, c), 1)
    incl = ii >= jj
    strict = ii > jj
    ltri = incl.astype(F32)
    eye = _eye(c)
    m1 = _dots([ltri] * nh, gs, "nn", "s3")
    gtot = [jnp.sum(g, axis=0, keepdims=True) for g in gs]
    decay = [jnp.exp(jnp.where(incl, m - m.T, NEG_BIG)) for m in m1]
    eg = [jnp.exp(m) for m in m1]
    kk = _dots(ks, ks, "nt", "bf16")
    lmats = [jnp.where(strict, b * x * d, 0.0) for b, x, d in zip(bs, kk, decay)]
    tinv = _inv_unit_lower(lmats, hints)
    toff = [t - eye for t in tinv]
    bv = [b * v for b, v in zip(bs, vs)]
    bk = [b * e * k for b, e, k in zip(bs, eg, ks)]
    u0 = [x + d for x, d in zip(bv, _dots(toff, bv, "nn", "bf16"))]
    wk = [x + d for x, d in zip(bk, _dots(toff, bk, "nn", "bf16"))]
    qsc = [q * (HEAD_DIM_A ** -0.5) for q in qs]
    qk = [x * d for x, d in zip(_dots(qsc, ks, "nt", "bf16"), decay)]
    q_dec = [q * e for q, e in zip(qsc, eg)]
    k_dec = [k * jnp.exp(t - m) for k, t, m in zip(ks, gtot, m1)]
    glast = [jnp.broadcast_to(jnp.exp(t), (c, c)) for t in gtot]
    return (u0, wk, qk, q_dec, k_dec, glast), tinv


def _delta_step(ss, u0, wk, qk, q_dec, k_dec, glast):
    us = [a - d for a, d in zip(u0, _dots(wk, ss, "nn", "bf16"))]
    os_ = [a + d for a, d in zip(_dots(q_dec, ss, "nn", "bf16"), _dots(qk, us, "nn", "bf16"))]
    s_new = [g * s + d for g, s, d in zip(glast, ss, _dots(k_dec, us, "tn", "bf16"))]
    return os_, s_new


HEADS_PER_STEP = 8


def _chunk_spec(nc, reverse=False):
    w = HEADS_PER_STEP * HEAD_DIM_A
    if reverse:
        return pl.BlockSpec((CHUNK, w), lambda h, n: (nc - 1 - n, h))
    return pl.BlockSpec((CHUNK, w), lambda h, n: (n, h))


def _head_slices():
    return [slice(j * HEAD_DIM_A, (j + 1) * HEAD_DIM_A) for j in range(HEADS_PER_STEP)]


def _heads(ref):
    return [ref[:, sl] for sl in _head_slices()]


def _delta_prep_fwd(q, k, v, gbc, bbc, *, name):
    t, d = q.shape
    nc = t // CHUNK

    def body(q_ref, k_ref, v_ref, g_ref, b_ref, *outs):
        res, tinv = _delta_prep(*[_heads(r) for r in (q_ref, k_ref, v_ref, g_ref, b_ref)])
        for ref, vals in zip(outs, res + (tinv,)):
            for sl, val in zip(_head_slices(), vals):
                ref[:, sl] = val

    spec = _chunk_spec(nc)
    return pl.pallas_call(
        body,
        name=name,
        grid=(N_HEADS_A // HEADS_PER_STEP, nc),
        in_specs=[spec] * 5,
        out_specs=[spec] * 7,
        out_shape=[jax.ShapeDtypeStruct((t, d), F32)] * 7,
        compiler_params=_params(("parallel", "parallel")),
    )(q, k, v, gbc, bbc)


def _delta_prep_bwd(q, k, v, gbc, bbc, tinv, cts, *, name):
    t, d = q.shape
    nc = t // CHUNK

    def body(q_ref, k_ref, v_ref, g_ref, b_ref, t_ref, c0, c1, c2, c3, c4, c5, *outs):
        def f(q_, k_, v_, g_, b_):
            return _delta_prep(q_, k_, v_, g_, b_, hints=_heads(t_ref))[0]

        _, vjp = jax.vjp(f, *[_heads(r) for r in (q_ref, k_ref, v_ref, g_ref, b_ref)])
        grads = vjp(tuple(_heads(c) for c in (c0, c1, c2, c3, c4, c5)))
        for ref, vals in zip(outs, grads):
            for sl, val in zip(_head_slices(), vals):
                ref[:, sl] = val

    spec = _chunk_spec(nc)
    return pl.pallas_call(
        body,
        name=name,
        grid=(N_HEADS_A // HEADS_PER_STEP, nc),
        in_specs=[spec] * 12,
        out_specs=[spec] * 5,
        out_shape=[jax.ShapeDtypeStruct((t, d), F32)] * 5,
        compiler_params=_params(("parallel", "parallel")),
    )(q, k, v, gbc, bbc, tinv, *cts)


def _delta_scan_fwd(prep, *, name):
    t, d = prep[0].shape
    nc = t // CHUNK

    def body(u0, wk, qk, qd, kd, gl, o_ref, st_ref, s_ref):
        n = pl.program_id(1)

        @pl.when(n == 0)
        def _():
            s_ref[...] = jnp.zeros(s_ref.shape, F32)

        ss = [s_ref[j] for j in range(HEADS_PER_STEP)]
        os_, s_new = _delta_step(ss, *[_heads(r) for r in (u0, wk, qk, qd, kd, gl)])
        for j, sl in enumerate(_head_slices()):
            st_ref[:, sl] = ss[j]
            o_ref[:, sl] = os_[j]
            s_ref[j] = s_new[j]

    spec = _chunk_spec(nc)
    return pl.pallas_call(
        body,
        name=name,
        grid=(N_HEADS_A // HEADS_PER_STEP, nc),
        in_specs=[spec] * 6,
        out_specs=[spec] * 2,
        out_shape=[jax.ShapeDtypeStruct((t, d), F32)] * 2,
        scratch_shapes=[pltpu.VMEM((HEADS_PER_STEP, HEAD_DIM_A, HEAD_DIM_A), F32)],
        compiler_params=_params(("parallel", "arbitrary")),
    )(*prep)


def _delta_scan_bwd(prep, states, do, *, name):
    t, d = do.shape
    nc = t // CHUNK

    def body(u0, wk, qk, qd, kd, gl, st_ref, do_ref, *rest):
        outs, ds_ref = rest[:6], rest[6]
        n = pl.program_id(1)

        @pl.when(n == 0)
        def _():
            ds_ref[...] = jnp.zeros(ds_ref.shape, F32)

        _, vjp = jax.vjp(_delta_step, *[_heads(r) for r in (st_ref, u0, wk, qk, qd, kd, gl)])
        grads = vjp((_heads(do_ref), [ds_ref[j] for j in range(HEADS_PER_STEP)]))
        for j, sl in enumerate(_head_slices()):
            ds_ref[j] = grads[0][j]
            for ref, vals in zip(outs, grads[1:]):
                ref[:, sl] = vals[j]

    spec = _chunk_spec(nc, reverse=True)
    return pl.pallas_call(
        body,
        name=name,
        grid=(N_HEADS_A // HEADS_PER_STEP, nc),
        in_specs=[spec] * 8,
        out_specs=[spec] * 6,
        out_shape=[jax.ShapeDtypeStruct((t, d), F32)] * 6,
        scratch_shapes=[pltpu.VMEM((HEADS_PER_STEP, HEAD_DIM_A, HEAD_DIM_A), F32)],
        compiler_params=_params(("parallel", "arbitrary")),
    )(*prep, states, do)


def _delta_fwd(q, k, v, gbc, bbc, proj, norm_w, *, name):
    assert HEADS_PER_STEP == N_HEADS_A
    t, d = q.shape
    nc = t // CHUNK

    def body(q_ref, k_ref, v_ref, g_ref, b_ref, z_ref, w_ref, o_ref, st_ref, t_ref, og_ref, s_ref):
        n = pl.program_id(0)

        @pl.when(n == 0)
        def _():
            s_ref[...] = jnp.zeros(s_ref.shape, F32)

        ss = [s_ref[j] for j in range(N_HEADS_A)]
        res, tinv = _delta_prep(*[_heads(r) for r in (q_ref, k_ref, v_ref, g_ref, b_ref)])
        os_, s_new = _delta_step(ss, *res)
        rstds = [_rstd(o) for o in os_]
        for j, sl in enumerate(_head_slices()):
            st_ref[:, sl] = ss[j]
            t_ref[:, sl] = tinv[j]
            o_ref[:, sl] = os_[j]
            s_ref[j] = s_new[j]
            zh = z_ref[:, sl]
            og_ref[:, sl] = (os_[j] * rstds[j] * w_ref[...] * (zh * _sigmoid(zh))).astype(BF16)

    spec = pl.BlockSpec((CHUNK, d), lambda n: (n, 0))
    f32 = jax.ShapeDtypeStruct((t, d), F32)
    return pl.pallas_call(
        body,
        name=name,
        grid=(nc,),
        in_specs=[spec] * 5 + [pl.BlockSpec((CHUNK, d), lambda n: (n, Z_COL_BLOCK)), pl.BlockSpec((1, HEAD_DIM_A), lambda n: (0, 0))],
        out_specs=[spec] * 4,
        out_shape=[f32, f32, f32, jax.ShapeDtypeStruct((t, d), BF16)],
        scratch_shapes=[pltpu.VMEM((N_HEADS_A, HEAD_DIM_A, HEAD_DIM_A), F32)],
        compiler_params=_params(("arbitrary",)),
    )(q, k, v, gbc, bbc, proj, norm_w)


def _delta_bwd(q, k, v, gbc, bbc, tinv, states, o, proj, norm_w, dog, *, name):
    t, d = q.shape
    nc = t // CHUNK

    def body(q_ref, k_ref, v_ref, g_ref, b_ref, t_ref, st_ref, o_ref, z_ref, w_ref, dog_ref,
             dq_ref, dk_ref, dv_ref, dg_ref, db_ref, dz_ref, dw_ref, ds_ref):
        n = pl.program_id(0)

        @pl.when(n == 0)
        def _():
            ds_ref[...] = jnp.zeros(ds_ref.shape, F32)

        hsl = _head_slices()
        rstds = [_rstd(o_ref[:, sl]) for sl in hsl]
        nhs = [o_ref[:, sl] * r for sl, r in zip(hsl, rstds)]
        sgs = [_sigmoid(z_ref[:, sl]) for sl in hsl]
        dts = [dog_ref[:, sl] * (z_ref[:, sl] * sg) for sl, sg in zip(hsl, sgs)]
        dnhs = [dt * w_ref[...] for dt in dts]
        means = [jnp.mean(dnh * nh, axis=-1, keepdims=True) for dnh, nh in zip(dnhs, nhs)]
        dos = [r * (dnh - nh * mn) for r, dnh, nh, mn in zip(rstds, dnhs, nhs, means)]
        dw = jnp.zeros((1, HEAD_DIM_A), F32)
        for sl, nh, sg, dt in zip(hsl, nhs, sgs, dts):
            zh = z_ref[:, sl]
            dz_ref[:, sl] = (dog_ref[:, sl] * nh * w_ref[...] * (sg * (1.0 + zh * (1.0 - sg)))).astype(BF16)
            dw = dw + jnp.sum(dt * nh, axis=0, keepdims=True)

        @pl.when(n == 0)
        def _():
            dw_ref[...] = dw

        @pl.when(n > 0)
        def _():
            dw_ref[...] += dw

        def chunk(qs, ks, vs, gs, bs, ss):
            return _delta_step(ss, *_delta_prep(qs, ks, vs, gs, bs, hints=_heads(t_ref))[0])

        _, vjp = jax.vjp(chunk, *[_heads(r) for r in (q_ref, k_ref, v_ref, g_ref, b_ref, st_ref)])
        grads = vjp((dos, [ds_ref[j] for j in range(N_HEADS_A)]))
        for j, sl in enumerate(_head_slices()):
            ds_ref[j] = grads[5][j]
            for ref, vals in zip((dq_ref, dk_ref, dv_ref, dg_ref, db_ref), grads[:5]):
                ref[:, sl] = vals[j]

    spec = pl.BlockSpec((CHUNK, d), lambda n: (nc - 1 - n, 0))
    vec = pl.BlockSpec((1, HEAD_DIM_A), lambda n: (0, 0))
    f32 = jax.ShapeDtypeStruct((t, d), F32)
    return pl.pallas_call(
        body,
        name=name,
        grid=(nc,),
        in_specs=[spec] * 8 + [pl.BlockSpec((CHUNK, d), lambda n: (nc - 1 - n, Z_COL_BLOCK)), vec, spec],
        out_specs=[spec] * 6 + [vec],
        out_shape=[f32] * 5 + [jax.ShapeDtypeStruct((t, d), BF16), jax.ShapeDtypeStruct((1, HEAD_DIM_A), F32)],
        scratch_shapes=[pltpu.VMEM((N_HEADS_A, HEAD_DIM_A, HEAD_DIM_A), F32)],
        compiler_params=_params(("arbitrary",)),
    )(q, k, v, gbc, bbc, tinv, states, o, proj, norm_w, dog)


def _alibi_slope(h):
    return 2.0 ** (-8.0 * (h + 1) / N_HEADS_B)


def _swa_load(sink_ref, q_ref, kp_ref, kc_ref, vp_ref, vc_ref):
    rg = lax.broadcasted_iota(jnp.int32, (GROUP_B * WINDOW, 1), 0) // WINDOW
    q4s, kcats, vcats, slopes, sinkcols = [], [], [], [], []
    for hk in range(N_KV_B):
        ks = slice(hk * HEAD_DIM_B, (hk + 1) * HEAD_DIM_B)
        heads = [hk * GROUP_B + g for g in range(GROUP_B)]
        q4s.append(jnp.concatenate([q_ref[:, h * HEAD_DIM_B:(h + 1) * HEAD_DIM_B] for h in heads], axis=0).astype(BF16))
        kcats.append(jnp.concatenate([kp_ref[:, ks], kc_ref[:, ks]], axis=0).astype(BF16))
        vcats.append(jnp.concatenate([vp_ref[:, ks], vc_ref[:, ks]], axis=0).astype(BF16))
        slope = jnp.zeros((GROUP_B * WINDOW, 1), F32)
        sink = jnp.zeros((GROUP_B * WINDOW, 1), F32)
        for g, h in enumerate(heads):
            slope = jnp.where(rg == g, _alibi_slope(h), slope)
            sink = jnp.where(rg == g, sink_ref[0, h], sink)
        slopes.append(slope)
        sinkcols.append(sink)
    return q4s, kcats, vcats, slopes, sinkcols


def _swa_probs(q4s, kcats, slopes, sinkcols, blk):
    rows = GROUP_B * WINDOW
    qi = lax.broadcasted_iota(jnp.int32, (rows, 2 * WINDOW), 0) % WINDOW
    kj = lax.broadcasted_iota(jnp.int32, (rows, 2 * WINDOW), 1)
    dist = qi + WINDOW - kj
    valid = (dist >= 0) & (dist < WINDOW) & (blk * WINDOW - WINDOW + kj >= 0)
    distf = dist.astype(F32)
    ss = [lax.dot_general(q, kc, _NT, preferred_element_type=F32) for q, kc in zip(q4s, kcats)]
    logits = [jnp.where(valid, s * (HEAD_DIM_B ** -0.5) - sl * distf, NEG_BIG) for s, sl in zip(ss, slopes)]
    ms = [jnp.maximum(jnp.max(l, axis=-1, keepdims=True), sk) for l, sk in zip(logits, sinkcols)]
    es = [jnp.exp(l - m) for l, m in zip(logits, ms)]
    esk = [jnp.exp(sk - m) for sk, m in zip(sinkcols, ms)]
    invs = [1.0 / (jnp.sum(e, axis=-1, keepdims=True) + k) for e, k in zip(es, esk)]
    return [e * i for e, i in zip(es, invs)], [k * i for k, i in zip(esk, invs)]


def _swa_fwd(proj, sinks, *, name):
    t = proj.shape[0]
    nb = t // WINDOW
    qd = N_HEADS_B * HEAD_DIM_B
    kd = N_KV_B * HEAD_DIM_B

    def body(sink_ref, q_ref, kp_ref, kc_ref, vp_ref, vc_ref, o_ref):
        blk = pl.program_id(0)
        q4s, kcats, vcats, slopes, sinkcols = _swa_load(sink_ref, q_ref, kp_ref, kc_ref, vp_ref, vc_ref)
        ps, _ = _swa_probs(q4s, kcats, slopes, sinkcols, blk)
        outs = [jnp.dot(p.astype(BF16), vc, preferred_element_type=F32) for p, vc in zip(ps, vcats)]
        for hk, out in enumerate(outs):
            for g in range(GROUP_B):
                h = hk * GROUP_B + g
                o_ref[:, h * HEAD_DIM_B:(h + 1) * HEAD_DIM_B] = out[g * WINDOW:(g + 1) * WINDOW, :].astype(BF16)

    q_spec = pl.BlockSpec((WINDOW, qd), lambda i: (i, 0))
    kv = lambda col, prev: pl.BlockSpec((WINDOW, kd), (lambda i: (jnp.maximum(i - 1, 0), col)) if prev else (lambda i: (i, col)))
    kcol, vcol = qd // kd, qd // kd + 1
    return pl.pallas_call(
        body,
        name=name,
        grid=(nb,),
        in_specs=[pl.BlockSpec(memory_space=pltpu.SMEM), q_spec, kv(kcol, True), kv(kcol, False), kv(vcol, True), kv(vcol, False)],
        out_specs=q_spec,
        out_shape=jax.ShapeDtypeStruct((t, qd), BF16),
        compiler_params=_params(("parallel",)),
    )(sinks, proj, proj, proj, proj, proj)


def _swa_bwd(proj, sinks, dout, *, name):
    t = proj.shape[0]
    nb = t // WINDOW
    qd = N_HEADS_B * HEAD_DIM_B
    kd = N_KV_B * HEAD_DIM_B
    scale = HEAD_DIM_B ** -0.5

    def body(sink_ref, q_ref, kp_ref, kc_ref, vp_ref, vc_ref, do_ref, dq_ref, dk_ref, dv_ref, dsk_ref):
        blk = pl.program_id(0)
        lane = lax.broadcasted_iota(jnp.int32, (1, LANES), 1)

        @pl.when(blk == 0)
        def _():
            dk_ref[...] = jnp.zeros((t, kd), F32)
            dv_ref[...] = jnp.zeros((t, kd), F32)
            dsk_ref[...] = jnp.zeros((1, LANES), F32)

        cur = pl.ds(pl.multiple_of(blk * WINDOW, WINDOW), WINDOW)
        prv = pl.ds(pl.multiple_of(jnp.maximum(blk - 1, 0) * WINDOW, WINDOW), WINDOW)
        q4s, kcats, vcats, slopes, sinkcols = _swa_load(sink_ref, q_ref, kp_ref, kc_ref, vp_ref, vc_ref)
        ps, psinks = _swa_probs(q4s, kcats, slopes, sinkcols, blk)
        do4s = [jnp.concatenate([do_ref[:, (hk * GROUP_B + g) * HEAD_DIM_B:(hk * GROUP_B + g + 1) * HEAD_DIM_B]
                                 for g in range(GROUP_B)], axis=0).astype(BF16) for hk in range(N_KV_B)]
        dps = [lax.dot_general(d, vc, _NT, preferred_element_type=F32) for d, vc in zip(do4s, vcats)]
        deltas = [jnp.sum(p * dp, axis=-1, keepdims=True) for p, dp in zip(ps, dps)]
        dss = [(p * (dp - dl) * scale).astype(BF16) for p, dp, dl in zip(ps, dps, deltas)]
        dq4s = [jnp.dot(ds, kc, preferred_element_type=F32) for ds, kc in zip(dss, kcats)]
        dkcs = [lax.dot_general(ds, q, _TN, preferred_element_type=F32) for ds, q in zip(dss, q4s)]
        dvcs = [lax.dot_general(p.astype(BF16), d, _TN, preferred_element_type=F32) for p, d in zip(ps, do4s)]
        dsk = jnp.zeros((1, LANES), F32)
        for hk in range(N_KV_B):
            ks = slice(hk * HEAD_DIM_B, (hk + 1) * HEAD_DIM_B)
            dsink = -psinks[hk] * deltas[hk]
            for g in range(GROUP_B):
                h = hk * GROUP_B + g
                rows = slice(g * WINDOW, (g + 1) * WINDOW)
                dq_ref[:, h * HEAD_DIM_B:(h + 1) * HEAD_DIM_B] = dq4s[hk][rows, :].astype(BF16)
                dsk = dsk + jnp.where(lane == h, jnp.sum(dsink[rows, :], axis=0, keepdims=True), 0.0)
            dk_ref[cur, ks] += dkcs[hk][WINDOW:, :]
            dv_ref[cur, ks] += dvcs[hk][WINDOW:, :]

            @pl.when(blk > 0)
            def _():
                dk_ref[prv, ks] += dkcs[hk][:WINDOW, :]
                dv_ref[prv, ks] += dvcs[hk][:WINDOW, :]

        dsk_ref[...] += dsk

    q_spec = pl.BlockSpec((WINDOW, qd), lambda i: (i, 0))
    kv = lambda col, prev: pl.BlockSpec((WINDOW, kd), (lambda i: (jnp.maximum(i - 1, 0), col)) if prev else (lambda i: (i, col)))
    kcol, vcol = qd // kd, qd // kd + 1
    full = pl.BlockSpec((t, kd), lambda i: (0, 0))
    return pl.pallas_call(
        body,
        name=name,
        grid=(nb,),
        in_specs=[pl.BlockSpec(memory_space=pltpu.SMEM), q_spec, kv(kcol, True), kv(kcol, False), kv(vcol, True), kv(vcol, False), q_spec],
        out_specs=[q_spec, full, full, pl.BlockSpec((1, LANES), lambda i: (0, 0))],
        out_shape=[jax.ShapeDtypeStruct((t, qd), BF16), jax.ShapeDtypeStruct((t, kd), F32),
                   jax.ShapeDtypeStruct((t, kd), F32), jax.ShapeDtypeStruct((1, LANES), F32)],
        compiler_params=_params(("arbitrary",)),
    )(sinks, proj, proj, proj, proj, proj, dout)


def _ffn_act_fwd(up, cw, *, name, tm=512, cb=256):
    _, t, f = up.shape
    tm, cb = _tile(t, tm), _tile(f, cb)

    rc = min(ROW_CHUNK, tm)

    def body(ug_ref, uv_ref, cg_ref, cv_ref, a_ref, hg, hv):
        i = pl.program_id(1)

        @pl.when(i == 0)
        def _():
            hg[0:HALO, :] = jnp.zeros((HALO, cb), F32)
            hv[0:HALO, :] = jnp.zeros((HALO, cb), F32)

        hg[pl.ds(HALO, rc), :] = ug_ref[0:rc, :]
        hv[pl.ds(HALO, rc), :] = uv_ref[0:rc, :]
        for r in range(tm // rc):
            if r == 0:
                yg = _conv_rows(hg, HALO, rc, cg_ref, FFN_CONV)
                yv = _conv_rows(hv, HALO, rc, cv_ref, FFN_CONV)
            else:
                yg = _conv_rows(ug_ref, r * rc, rc, cg_ref, FFN_CONV)
                yv = _conv_rows(uv_ref, r * rc, rc, cv_ref, FFN_CONV)
            a_ref[r * rc:(r + 1) * rc, :] = (yg * _sigmoid(yg) * yv).astype(BF16)
        hg[0:HALO, :] = ug_ref[tm - HALO:tm, :]
        hv[0:HALO, :] = uv_ref[tm - HALO:tm, :]

    ncb = f // cb
    half = lambda s: pl.BlockSpec((None, tm, cb), lambda c, i: (s, i, c))
    taps = lambda s: pl.BlockSpec((FFN_CONV, cb), lambda c, i: (0, c + s * ncb))
    return pl.pallas_call(
        body,
        name=name,
        grid=(ncb, t // tm),
        in_specs=[half(0), half(1), taps(0), taps(1)],
        out_specs=pl.BlockSpec((tm, cb), lambda c, i: (i, c)),
        out_shape=jax.ShapeDtypeStruct((t, f), BF16),
        scratch_shapes=[pltpu.VMEM((HALO + rc, cb), F32)] * 2,
        compiler_params=_params(("parallel", "arbitrary")),
    )(up, up, cw, cw)


def _ffn_act_bwd(up, cw, dact, *, name, tm=512, cb=256):
    _, t, f = up.shape
    tm, cb = _tile(t, tm), _tile(f, cb)
    nt = t // tm
    hb = tm // HALO

    rc = min(ROW_CHUNK, tm)
    nr = tm // rc
    kw = FFN_CONV

    def body(ug_ref, uv_ref, pg_ref, pv_ref, cg_ref, cv_ref, da_ref, du_ref, dcg_ref, dcv_ref,
             hg, hv, dyg, dyv):
        i = pl.program_id(1)
        first = i == 0
        tile = nt - 1 - i

        @pl.when(tile == 0)
        def _():
            hg[0:HALO, :] = jnp.zeros((HALO, cb), F32)
            hv[0:HALO, :] = jnp.zeros((HALO, cb), F32)

        @pl.when(tile > 0)
        def _():
            hg[0:HALO, :] = pg_ref[...]
            hv[0:HALO, :] = pv_ref[...]

        @pl.when(first)
        def _():
            dyg[pl.ds(tm, HALO), :] = jnp.zeros((HALO, cb), F32)
            dyv[pl.ds(tm, HALO), :] = jnp.zeros((HALO, cb), F32)

        hg[pl.ds(HALO, rc), :] = ug_ref[0:rc, :]
        hv[pl.ds(HALO, rc), :] = uv_ref[0:rc, :]
        dcg = [jnp.zeros((1, cb), F32) for _ in range(kw)]
        dcv = [jnp.zeros((1, cb), F32) for _ in range(kw)]
        for r in reversed(range(nr)):
            rows = slice(r * rc, (r + 1) * rc)
            src_g, src_v, base = (hg, hv, HALO) if r == 0 else (ug_ref, uv_ref, r * rc)
            yg = _conv_rows(src_g, base, rc, cg_ref, kw)
            yv = _conv_rows(src_v, base, rc, cv_ref, kw)
            sg = _sigmoid(yg)
            da = da_ref[rows, :]
            dy_g = da * yv * (sg * (1.0 + yg * (1.0 - sg)))
            dy_v = da * (yg * sg)
            dyg[rows, :] = dy_g
            dyv[rows, :] = dy_v
            du_ref[0, rows, :] = _conv_t_rows(dyg, r * rc, rc, cg_ref, kw).astype(BF16)
            du_ref[1, rows, :] = _conv_t_rows(dyv, r * rc, rc, cv_ref, kw).astype(BF16)
            for j in range(kw):
                dcg[j] = dcg[j] + jnp.sum(dy_g * src_g[pl.ds(base - (kw - 1) + j, rc), :], axis=0, keepdims=True)
                dcv[j] = dcv[j] + jnp.sum(dy_v * src_v[pl.ds(base - (kw - 1) + j, rc), :], axis=0, keepdims=True)
        dyg[pl.ds(tm, HALO), :] = dyg[0:HALO, :]
        dyv[pl.ds(tm, HALO), :] = dyv[0:HALO, :]
        for j in range(kw):
            @pl.when(first)
            def _():
                dcg_ref[j:j + 1, :] = dcg[j]
                dcv_ref[j:j + 1, :] = dcv[j]

            @pl.when(jnp.logical_not(first))
            def _():
                dcg_ref[j:j + 1, :] += dcg[j]
                dcv_ref[j:j + 1, :] += dcv[j]

    ncb = f // cb
    half = lambda s: pl.BlockSpec((None, tm, cb), lambda c, i: (s, nt - 1 - i, c))
    prev = lambda s: pl.BlockSpec((None, HALO, cb), lambda c, i: (s, jnp.maximum((nt - 1 - i) * hb - 1, 0), c))
    taps = lambda s: pl.BlockSpec((FFN_CONV, cb), lambda c, i: (0, c + s * ncb))
    dtaps = pl.BlockSpec((FFN_CONV, cb), lambda c, i: (0, c))
    return pl.pallas_call(
        body,
        name=name,
        grid=(ncb, nt),
        in_specs=[half(0), half(1), prev(0), prev(1), taps(0), taps(1), pl.BlockSpec((tm, cb), lambda c, i: (nt - 1 - i, c))],
        out_specs=[pl.BlockSpec((2, tm, cb), lambda c, i: (0, nt - 1 - i, c)), dtaps, dtaps],
        out_shape=[jax.ShapeDtypeStruct((2, t, f), BF16), jax.ShapeDtypeStruct((FFN_CONV, f), F32),
                   jax.ShapeDtypeStruct((FFN_CONV, f), F32)],
        scratch_shapes=[pltpu.VMEM((HALO + rc, cb), F32)] * 2 + [pltpu.VMEM((tm + HALO, cb), F32)] * 2,
        compiler_params=_params(("parallel", "arbitrary")),
    )(up, up, up, up, cw, cw, dact)


FFN_COL_TILE = 1408
FFN_SUB = 512
FFN_ROW_CHUNK = 16


def _sub_blocks(width):
    return [slice(c, min(c + FFN_SUB, width)) for c in range(0, width, FFN_SUB)]


def _ffn_up_act(n_f, w_up_t, cw, *, name, tm=512):
    t, d = n_f.shape
    f = w_up_t.shape[1]
    tm, tn = _tile(t, tm), _tile(f, FFN_COL_TILE)
    nj = f // tn
    rc = min(FFN_ROW_CHUNK, tm)
    kw = FFN_CONV

    def body(n_ref, wg_ref, wv_ref, cg_ref, cv_ref, up_ref, y_ref, a_ref, hg, hv):
        i = pl.program_id(1)

        @pl.when(i == 0)
        def _():
            hg[0:HALO, :] = jnp.zeros((HALO, tn), F32)
            hv[0:HALO, :] = jnp.zeros((HALO, tn), F32)

        def products(cs):
            up_ref[0, :, cs] = lax.dot_general(n_ref[...], wg_ref[cs, :], _NT, preferred_element_type=F32)
            up_ref[1, :, cs] = lax.dot_general(n_ref[...], wv_ref[cs, :], _NT, preferred_element_type=F32)

        subs = _sub_blocks(tn)
        ug, uv = up_ref.at[0], up_ref.at[1]
        products(subs[0])
        for ci, cs in enumerate(subs):
            if ci + 1 < len(subs):
                products(subs[ci + 1])
            hg[pl.ds(HALO, rc), cs] = ug[0:rc, cs]
            hv[pl.ds(HALO, rc), cs] = uv[0:rc, cs]
            for r in range(tm // rc):
                src_g, src_v, base = (hg, hv, HALO) if r == 0 else (ug, uv, r * rc)
                yg = _conv_rows(src_g, base, rc, cg_ref, kw, cs)
                yv = _conv_rows(src_v, base, rc, cv_ref, kw, cs)
                y_ref[0, r * rc:(r + 1) * rc, cs] = yg
                y_ref[1, r * rc:(r + 1) * rc, cs] = yv
                a_ref[r * rc:(r + 1) * rc, cs] = (yg * _sigmoid(yg) * yv).astype(BF16)
            hg[0:HALO, cs] = ug[tm - HALO:tm, cs]
            hv[0:HALO, cs] = uv[tm - HALO:tm, cs]

    half = lambda s: pl.BlockSpec((None, tn, d), lambda j, i: (s, j, 0))
    taps = lambda s: pl.BlockSpec((kw, tn), lambda j, i: (0, j + s * nj))
    pair = pl.BlockSpec((2, tm, tn), lambda j, i: (0, i, j))
    return pl.pallas_call(
        body,
        name=name,
        grid=(nj, t // tm),
        in_specs=[pl.BlockSpec((tm, d), lambda j, i: (i, 0)), half(0), half(1), taps(0), taps(1)],
        out_specs=[pair, pair, pl.BlockSpec((tm, tn), lambda j, i: (i, j))],
        out_shape=[jax.ShapeDtypeStruct((2, t, f), F32), jax.ShapeDtypeStruct((2, t, f), F32),
                   jax.ShapeDtypeStruct((t, f), BF16)],
        scratch_shapes=[pltpu.VMEM((HALO + rc, tn), F32)] * 2,
        compiler_params=_params(("parallel", "arbitrary")),
    )(n_f, w_up_t, w_up_t, cw, cw)


def _ffn_down_dx_act_bwd(dh, w_down, up, y, cw, *, name, tm=512):
    t, d = dh.shape
    f = w_down.shape[0]
    tm, tn = _tile(t, tm), _tile(f, FFN_COL_TILE)
    nj, nt = f // tn, t // tm
    rc = min(FFN_ROW_CHUNK, tm)
    nr = tm // rc
    kw = FFN_CONV

    def body(dh_ref, wd_ref, ug_ref, uv_ref, yg_ref, yv_ref, cg_ref, cv_ref, du_ref, dcg_ref, dcv_ref,
             dyg, dyv, da_s, dh_s):
        i = pl.program_id(1)
        first = i == 0

        @pl.when(first)
        def _():
            dyg[pl.ds(tm, HALO), :] = jnp.zeros((HALO, tn), F32)
            dyv[pl.ds(tm, HALO), :] = jnp.zeros((HALO, tn), F32)

        dh_s[...] = dh_ref[...].astype(BF16)

        def product(cs):
            da_s[:, cs] = lax.dot_general(dh_s[...], wd_ref[cs, :], _NT, preferred_element_type=F32)

        subs = _sub_blocks(tn)
        product(subs[0])
        for ci, cs in enumerate(subs):
            width = cs.stop - cs.start
            if ci + 1 < len(subs):
                product(subs[ci + 1])
            dcg = [jnp.zeros((HALO, width), F32) for _ in range(kw)]
            dcv = [jnp.zeros((HALO, width), F32) for _ in range(kw)]
            for r in reversed(range(nr)):
                rows = slice(r * rc, (r + 1) * rc)
                yg, yv = yg_ref[rows, cs], yv_ref[rows, cs]
                sg = _sigmoid(yg)
                da = da_s[rows, cs]
                dyg[rows, cs] = da * yv * (sg * (1.0 + yg * (1.0 - sg)))
                dyv[rows, cs] = da * (yg * sg)
                ahead_g = _ahead_rows(dyg, r * rc, rc, range(kw), cs)
                ahead_v = _ahead_rows(dyv, r * rc, rc, range(kw), cs)
                du_ref[0, rows, cs] = sum(cg_ref[kw - 1 - s:kw - s, cs] * ahead_g[s] for s in range(kw)).astype(BF16)
                du_ref[1, rows, cs] = sum(cv_ref[kw - 1 - s:kw - s, cs] * ahead_v[s] for s in range(kw)).astype(BF16)
                xg, xv = ug_ref[rows, cs], uv_ref[rows, cs]
                for s in range(kw):
                    dcg[kw - 1 - s] = dcg[kw - 1 - s] + _fold_rows(xg * ahead_g[s])
                    dcv[kw - 1 - s] = dcv[kw - 1 - s] + _fold_rows(xv * ahead_v[s])
            dyg[pl.ds(tm, HALO), cs] = dyg[0:HALO, cs]
            dyv[pl.ds(tm, HALO), cs] = dyv[0:HALO, cs]
            for j in range(kw):
                tg = jnp.sum(dcg[j], axis=0, keepdims=True)
                tv = jnp.sum(dcv[j], axis=0, keepdims=True)

                @pl.when(first)
                def _():
                    dcg_ref[j:j + 1, cs] = tg
                    dcv_ref[j:j + 1, cs] = tv

                @pl.when(jnp.logical_not(first))
                def _():
                    dcg_ref[j:j + 1, cs] += tg
                    dcv_ref[j:j + 1, cs] += tv

    half = lambda s: pl.BlockSpec((None, tm, tn), lambda j, i: (s, nt - 1 - i, j))
    taps = lambda s: pl.BlockSpec((kw, tn), lambda j, i: (0, j + s * nj))
    dtaps = pl.BlockSpec((kw, tn), lambda j, i: (0, j))
    return pl.pallas_call(
        body,
        name=name,
        grid=(nj, nt),
        in_specs=[pl.BlockSpec((tm, d), lambda j, i: (nt - 1 - i, 0)), pl.BlockSpec((tn, d), lambda j, i: (j, 0)),
                  half(0), half(1), half(0), half(1), taps(0), taps(1)],
        out_specs=[pl.BlockSpec((2, tm, tn), lambda j, i: (0, nt - 1 - i, j)), dtaps, dtaps],
        out_shape=[jax.ShapeDtypeStruct((2, t, f), BF16), jax.ShapeDtypeStruct((kw, f), F32),
                   jax.ShapeDtypeStruct((kw, f), F32)],
        scratch_shapes=[pltpu.VMEM((tm + HALO, tn), F32)] * 2 + [pltpu.VMEM((tm, tn), F32), pltpu.VMEM((tm, d), BF16)],
        compiler_params=_params(("parallel", "arbitrary")),
    )(dh, w_down, up, up, y, y, cw, cw)


def _ple_fwd(h, zg, pe, *, name, tm=512):
    t, d = h.shape
    tm = _tile(t, tm)

    def body(h_ref, z_ref, p_ref, o_ref):
        o_ref[...] = h_ref[...] + _sigmoid(z_ref[...]) * p_ref[...]

    row = pl.BlockSpec((tm, d), lambda i: (i, 0))
    return pl.pallas_call(
        body, name=name, grid=(t // tm,), in_specs=[row] * 3, out_specs=row,
        out_shape=jax.ShapeDtypeStruct((t, d), F32), compiler_params=_params(("parallel",)),
    )(h, zg, pe)


def _ple_bwd(dh, zg, pe, *, name, tm=512):
    t, d = dh.shape
    tm = _tile(t, tm)

    def body(g_ref, z_ref, p_ref, dz_ref, dp_ref):
        g = g_ref[...]
        sg = _sigmoid(z_ref[...])
        dz_ref[...] = (g * p_ref[...] * sg * (1.0 - sg)).astype(BF16)
        dp_ref[...] = (g * sg).astype(BF16)

    row = pl.BlockSpec((tm, d), lambda i: (i, 0))
    return pl.pallas_call(
        body, name=name, grid=(t // tm,), in_specs=[row] * 3, out_specs=[row] * 2,
        out_shape=[jax.ShapeDtypeStruct((t, d), BF16)] * 2, compiler_params=_params(("parallel",)),
    )(dh, zg, pe)


def _my_pos():
    return lax.axis_index("x"), lax.axis_index("y"), lax.axis_index("c")


def _all_gather(block, *, name, dep=None):
    r, w = block.shape
    has_dep = dep is not None

    def body(*refs):
        x_ref, out_ref, send_sems, recv_sems, local_sem = refs[:1] + refs[1 + has_dep:]
        x, y, c = _my_pos()
        me, sibling = (x, y, c), (x, y, 1 - c)
        chips = [(1 - x, y), (x, 1 - y), (1 - x, 1 - y)]

        def slot(px, py, pc):
            return out_ref.at[4 * px + 2 * py + pc]

        def copy(k, blk, to, src=None):
            return pltpu.make_async_remote_copy(
                src_ref=slot(*blk) if src is None else src, dst_ref=slot(*blk),
                send_sem=send_sems.at[k], recv_sem=recv_sems.at[k],
                device_id=to, device_id_type=pl.DeviceIdType.MESH)

        mine = pltpu.make_async_copy(x_ref, slot(*me), local_sem)
        mine.start()
        first = [copy(0, me, sibling, src=x_ref)]
        first += [copy(1 + j, me, (*chip, c), src=x_ref) for j, chip in enumerate(chips)]
        for cp in first:
            cp.start()
        passed = [copy(4 + j, (*chip, c), sibling) for j, chip in enumerate(chips)]
        for j, chip in enumerate(chips):
            copy(1 + j, (*chip, c), me).wait_recv()
            passed[j].start()
        copy(0, sibling, me).wait_recv()
        for j, chip in enumerate(chips):
            copy(4 + j, (*chip, 1 - c), me).wait_recv()
        for cp in first + passed:
            cp.wait_send()
        mine.wait()

    return pl.pallas_call(
        body,
        name=name,
        out_shape=jax.ShapeDtypeStruct((N_DEV, r, w), block.dtype),
        in_specs=[pl.BlockSpec(memory_space=pl.ANY)] * (1 + has_dep),
        out_specs=pl.BlockSpec(memory_space=pl.ANY),
        scratch_shapes=[pltpu.SemaphoreType.DMA((7,)), pltpu.SemaphoreType.DMA((7,)), pltpu.SemaphoreType.DMA],
    )(*((block, dep) if has_dep else (block,)))


def _all_to_all(slabs, *, name):
    n, r, w = slabs.shape

    def body(x_ref, out_ref, send_sems, recv_sems, local_sem):
        x, y, c = _my_pos()
        my_idx = 4 * x + 2 * y + c
        mine = pltpu.make_async_copy(x_ref.at[my_idx], out_ref.at[my_idx], local_sem)
        mine.start()
        copies = []
        for k in range(1, N_DEV):
            fx, fy, fc = (k >> 2) & 1, (k >> 1) & 1, k & 1
            px = (1 - x) if fx else x
            py = (1 - y) if fy else y
            pc = (1 - c) if fc else c
            cp = pltpu.make_async_remote_copy(
                src_ref=x_ref.at[4 * px + 2 * py + pc], dst_ref=out_ref.at[my_idx],
                send_sem=send_sems.at[k - 1], recv_sem=recv_sems.at[k - 1],
                device_id=(px, py, pc), device_id_type=pl.DeviceIdType.MESH)
            cp.start()
            copies.append(cp)
        for cp in copies:
            cp.wait_recv()
        for cp in copies:
            cp.wait_send()
        mine.wait()

    return pl.pallas_call(
        body,
        name=name,
        out_shape=jax.ShapeDtypeStruct((n, r, w), slabs.dtype),
        in_specs=[pl.BlockSpec(memory_space=pl.ANY)],
        out_specs=pl.BlockSpec(memory_space=pl.ANY),
        scratch_shapes=[pltpu.SemaphoreType.DMA((7,)), pltpu.SemaphoreType.DMA((7,)), pltpu.SemaphoreType.DMA],
    )(slabs)


def _exchange_copies(scatter, src_refs, land_refs, send_sems, recv_sems, local_sems):
    x, y, c = _my_pos()
    me = 4 * x + 2 * y + c
    local, remote = [], []
    for i, (s, l) in enumerate(zip(src_refs, land_refs)):
        local.append(pltpu.make_async_copy(s.at[me] if scatter else s, l.at[me], local_sems.at[i]))
        for k in range(1, N_DEV):
            px = (1 - x) if (k >> 2) & 1 else x
            py = (1 - y) if (k >> 1) & 1 else y
            pc = (1 - c) if k & 1 else c
            remote.append(pltpu.make_async_remote_copy(
                src_ref=s.at[4 * px + 2 * py + pc] if scatter else s, dst_ref=l.at[me],
                send_sem=send_sems.at[(N_DEV - 1) * i + k - 1], recv_sem=recv_sems.at[(N_DEV - 1) * i + k - 1],
                device_id=(px, py, pc), device_id_type=pl.DeviceIdType.MESH))
    return local, remote


def _exchange(arrays, *, scatter, name):
    n = len(arrays)

    def body(*refs):
        srcs, lands = refs[:n], refs[n:2 * n]
        local, remote = _exchange_copies(scatter, srcs, lands, *refs[2 * n:])
        for cp in local + remote:
            cp.start()
        for cp in remote:
            cp.wait_recv()
        for cp in remote:
            cp.wait_send()
        for cp in local:
            cp.wait()

    hbm = pl.BlockSpec(memory_space=pl.ANY)
    out = pl.pallas_call(
        body,
        name=name,
        out_shape=[jax.ShapeDtypeStruct(a.shape if scatter else (N_DEV,) + a.shape, a.dtype) for a in arrays],
        in_specs=[hbm] * n,
        out_specs=[hbm] * n,
        scratch_shapes=[pltpu.SemaphoreType.DMA(((N_DEV - 1) * n,)), pltpu.SemaphoreType.DMA(((N_DEV - 1) * n,)),
                        pltpu.SemaphoreType.DMA((n,))],
    )(*arrays)
    return list(out)


_HBM_SPEC = pl.BlockSpec(memory_space=pltpu.HBM)
_SEM_SPEC = pl.BlockSpec(memory_space=pltpu.SEMAPHORE)
_EFFECT = pltpu.SideEffectType.DATAFLOW_SIDE_EFFECTING


def _exchange_start(arrays, *, scatter, name, dep):
    n = len(arrays)
    srcs = [pltpu.with_memory_space_constraint(a, pltpu.HBM) for a in arrays]
    lands = [pltpu.with_memory_space_constraint(lax.empty(a.shape if scatter else (N_DEV,) + a.shape, a.dtype), pltpu.HBM)
             for a in arrays]

    def body(*refs):
        src_refs, land_refs = refs[:n], refs[n:2 * n]
        send_sems, recv_sems, local_sems = refs[2 * n + 1:2 * n + 4]
        token = refs[-1]
        local, remote = _exchange_copies(scatter, src_refs, land_refs, send_sems, recv_sems, local_sems)
        for cp in local + remote:
            cp.start()
        token[...] = jnp.zeros_like(token)

    sems = (pltpu.SemaphoreType.DMA(((N_DEV - 1) * n,)), pltpu.SemaphoreType.DMA(((N_DEV - 1) * n,)),
            pltpu.SemaphoreType.DMA((n,)))
    out = pl.pallas_call(
        body,
        name=name,
        out_shape=sems + tuple(pltpu.HBM(a.shape, a.dtype) for a in srcs + lands) + (jax.ShapeDtypeStruct((8, LANES), F32),),
        in_specs=[_HBM_SPEC] * (2 * n) + [pl.BlockSpec(memory_space=pl.ANY)],
        out_specs=(_SEM_SPEC,) * 3 + (_HBM_SPEC,) * (2 * n) + (pl.BlockSpec(memory_space=pltpu.VMEM),),
        input_output_aliases={i: 3 + i for i in range(2 * n)},
        compiler_params=pltpu.CompilerParams(has_side_effects=_EFFECT),
    )(*srcs, *lands, dep)
    return (out[:3], list(out[3:3 + n]), list(out[3 + n:3 + 2 * n])), out[-1]


def _exchange_wait(handle, after, *, scatter, name):
    sems, srcs, lands = handle
    n = len(srcs)

    def body(*refs):
        src_refs, land_refs = refs[:n], refs[n:2 * n]
        send_sems, recv_sems, local_sems = refs[2 * n:2 * n + 3]
        local, remote = _exchange_copies(scatter, src_refs, land_refs, send_sems, recv_sems, local_sems)
        for cp in remote:
            cp.wait_send()
            cp.wait_recv()
        for cp in local:
            cp.wait()

    out = pl.pallas_call(
        body,
        name=name,
        out_shape=tuple(pltpu.HBM(a.shape, a.dtype) for a in srcs + lands),
        in_specs=[_HBM_SPEC] * (2 * n) + [_SEM_SPEC] * 3 + [pl.BlockSpec(memory_space=pl.ANY)],
        out_specs=(_HBM_SPEC,) * (2 * n),
        input_output_aliases={i: i for i in range(2 * n)},
        compiler_params=pltpu.CompilerParams(has_side_effects=_EFFECT),
    )(*srcs, *lands, *sems, after)
    return list(out[n:])


def _sum_parts(parts, *, name, tr=512):
    n, r, lanes = parts.shape
    tr = tr if (r % tr == 0 and r > 1024) else r

    def body(p_ref, g_ref):
        g = p_ref[0].astype(F32)
        for j in range(1, n):
            g = g + p_ref[j].astype(F32)
        g_ref[...] = g

    row = pl.BlockSpec((tr, lanes), lambda i: (i, 0))
    return pl.pallas_call(
        body,
        name=name,
        grid=(r // tr,),
        in_specs=[pl.BlockSpec((n, tr, lanes), lambda i: (0, i, 0))],
        out_specs=row,
        out_shape=jax.ShapeDtypeStruct((r, lanes), F32),
        compiler_params=_params(("parallel",)),
    )(parts)


def _adamw_update(g, w, m, v):
    c1 = 1.0 / (1.0 - ADAM_B1 ** ADAM_STEP)
    c2 = 1.0 / (1.0 - ADAM_B2 ** ADAM_STEP)
    nm = ADAM_B1 * m + (1.0 - ADAM_B1) * g
    nv = ADAM_B2 * v + (1.0 - ADAM_B2) * (g * g)
    return -ADAM_LR * ((nm * c1) / (jnp.sqrt(nv * c2) + ADAM_EPS) + ADAM_WD * w), nm, nv


def _adamw_layer(g, w, m, v, layer, prev, *, name):
    nl, k, n = w.shape
    tr = max([d for d in range(8, min(k, 256) + 1, 8) if k % d == 0] or [k])
    in_parts = g.ndim == 3

    def body(g_ref, w_ref, m_ref, v_ref, *rest):
        go_ref, d_ref, nm_ref, nv_ref = rest[-4:]
        if in_parts:
            gg = g_ref[0].astype(F32)
            for j in range(1, g_ref.shape[0]):
                gg = gg + g_ref[j].astype(F32)
        else:
            gg = g_ref[...]
        d, nm, nv = _adamw_update(gg, w_ref[...], m_ref[...], v_ref[...])
        go_ref[...] = gg
        d_ref[...] = d
        nm_ref[...] = nm
        nv_ref[...] = nv

    lay = pl.BlockSpec((None, tr, n), lambda i: (layer, i, 0))
    n_prev = 0 if prev is None else 4
    out = jax.ShapeDtypeStruct((nl, k, n), F32)
    return pl.pallas_call(
        body,
        name=name,
        grid=(k // tr,),
        in_specs=[pl.BlockSpec((g.shape[0], tr, n), lambda i: (0, i, 0)) if in_parts else pl.BlockSpec((tr, n), lambda i: (i, 0)),
                  lay, lay, lay] + [pl.BlockSpec(memory_space=pl.ANY)] * n_prev,
        out_specs=[lay] * 4,
        out_shape=[out] * 4,
        input_output_aliases={4 + j: j for j in range(n_prev)},
        compiler_params=_params(("parallel",)),
    )(g, w, m, v, *(prev or ()))


def _adamw_packed(g, w, m, v, *, name, tr=512):
    r, lanes = g.shape
    tr = tr if r % tr == 0 else r
    c1 = 1.0 / (1.0 - ADAM_B1 ** ADAM_STEP)
    c2 = 1.0 / (1.0 - ADAM_B2 ** ADAM_STEP)

    def body(g_ref, w_ref, m_ref, v_ref, d_ref, nm_ref, nv_ref):
        g = g_ref[...]
        nm = ADAM_B1 * m_ref[...] + (1.0 - ADAM_B1) * g
        nv = ADAM_B2 * v_ref[...] + (1.0 - ADAM_B2) * (g * g)
        nm_ref[...] = nm
        nv_ref[...] = nv
        d_ref[...] = -ADAM_LR * ((nm * c1) / (jnp.sqrt(nv * c2) + ADAM_EPS) + ADAM_WD * w_ref[...])

    row = pl.BlockSpec((tr, lanes), lambda i: (i, 0))
    out = jax.ShapeDtypeStruct((r, lanes), F32)
    return pl.pallas_call(
        body,
        name=name,
        grid=(r // tr,),
        in_specs=[row] * 4,
        out_specs=[row] * 3,
        out_shape=[out] * 3,
        compiler_params=_params(("parallel",)),
    )(g, w, m, v)


BIG = ("a_w_in", "a_w_out", "b_w_in", "b_w_out", "f_w_up", "f_w_down", "ple_w_proj", "ple_w_gate")
CONVS = ("a_conv", "f_conv")
SMALL = ("norm_mix", "norm_ffn", "norm_ple", "norm_final", "a_log", "a_dt_bias", "a_norm", "b_sinks")
WEIGHTS = ("norm_mix", "norm_ffn", "norm_ple", "norm_final", "a_w_in", "a_conv", "a_log", "a_dt_bias", "a_norm",
           "a_w_out", "b_w_in", "b_sinks", "b_w_out", "f_w_up", "f_conv", "f_w_down", "ple_w_proj", "ple_w_gate")
SLAB_ROW_MULTIPLE = 512


def _pack(arrs, dtype, row_multiple):
    flat = jnp.concatenate([a.reshape(-1).astype(dtype) for a in arrs])
    rows = -(-flat.shape[0] // LANES)
    rows = -(-rows // row_multiple) * row_multiple
    return jnp.pad(flat, (0, rows * LANES - flat.shape[0])).reshape(rows, LANES)


def _unpack(slab, shapes):
    lead = slab.shape[:-2]
    flat = slab.reshape(lead + (-1,))
    out, off = [], 0
    for s in shapes:
        size = math.prod(s)
        out.append(flat[..., off:off + size].reshape(lead + tuple(s)))
        off += size
    return out


def _cols_full(g):
    g = jnp.moveaxis(g, 0, -2)
    return g.reshape(g.shape[:-2] + (g.shape[-2] * g.shape[-1],))


def _rows_full(g):
    g = jnp.moveaxis(g, 0, -3)
    return g.reshape(g.shape[:-3] + (g.shape[-3] * g.shape[-2], g.shape[-1]))


def _cols_split(wfull):
    n = wfull.shape[-1] // N_DEV
    g = wfull.reshape(wfull.shape[:-1] + (N_DEV, n))
    return jnp.moveaxis(g, -2, 0)


def _rows_split(wfull):
    k = wfull.shape[-2] // N_DEV
    g = wfull.reshape(wfull.shape[:-2] + (N_DEV, k, wfull.shape[-1]))
    return jnp.moveaxis(g, -3, 0)


TRANSPOSED = ("a_w_in", "b_w_in", "f_w_up", "ple_w_proj")


def _wire(name, a):
    return jnp.swapaxes(a, -1, -2) if name in TRANSPOSED else a


def _wire_shape(name, shape):
    return shape[:-2] + (shape[-1], shape[-2]) if name in TRANSPOSED else tuple(shape)


def _full(name, g):
    return _cols_full(g) if name in CONVS else _rows_full(g)


def _split(name, wfull):
    return _cols_split(wfull) if name in CONVS else _rows_split(wfull)


def _pack_split(grads, names, dtype, row_multiple):
    flat = jnp.concatenate([_split(n, grads[n]).reshape(N_DEV, -1).astype(dtype) for n in names], axis=1)
    rows = -(-flat.shape[1] // LANES)
    rows = -(-rows // row_multiple) * row_multiple
    return jnp.pad(flat, ((0, 0), (0, rows * LANES - flat.shape[1]))).reshape(N_DEV, rows, LANES)


def _pad_cols(a, width):
    return jnp.pad(a, ((0, 0), (0, width - a.shape[1])))


def kernel(x, p, norm_mix, norm_ffn, norm_ple, norm_final, a_w_in, a_conv, a_log, a_dt_bias, a_norm, a_w_out, b_w_in, b_sinks, b_w_out, f_w_up, f_conv, f_w_down, ple_w_proj, ple_w_gate, loss_target, m_norm_mix, m_norm_ffn, m_norm_ple, m_norm_final, m_a_w_in, m_a_conv, m_a_log, m_a_dt_bias, m_a_norm, m_a_w_out, m_b_w_in, m_b_sinks, m_b_w_out, m_f_w_up, m_f_conv, m_f_w_down, m_ple_w_proj, m_ple_w_gate, v_norm_mix, v_norm_ffn, v_norm_ple, v_norm_final, v_a_w_in, v_a_conv, v_a_log, v_a_dt_bias, v_a_norm, v_a_w_out, v_b_w_in, v_b_sinks, v_b_w_out, v_f_w_up, v_f_conv, v_f_w_down, v_ple_w_proj, v_ple_w_gate):
    wts = dict(norm_mix=norm_mix, norm_ffn=norm_ffn, norm_ple=norm_ple, norm_final=norm_final, a_w_in=a_w_in,
               a_conv=a_conv, a_log=a_log, a_dt_bias=a_dt_bias, a_norm=a_norm, a_w_out=a_w_out, b_w_in=b_w_in,
               b_sinks=b_sinks, b_w_out=b_w_out, f_w_up=f_w_up, f_conv=f_conv, f_w_down=f_w_down,
               ple_w_proj=ple_w_proj, ple_w_gate=ple_w_gate)
    mom = dict(norm_mix=m_norm_mix, norm_ffn=m_norm_ffn, norm_ple=m_norm_ple, norm_final=m_norm_final,
               a_w_in=m_a_w_in, a_conv=m_a_conv, a_log=m_a_log, a_dt_bias=m_a_dt_bias, a_norm=m_a_norm,
               a_w_out=m_a_w_out, b_w_in=m_b_w_in, b_sinks=m_b_sinks, b_w_out=m_b_w_out, f_w_up=m_f_w_up,
               f_conv=m_f_conv, f_w_down=m_f_w_down, ple_w_proj=m_ple_w_proj, ple_w_gate=m_ple_w_gate)
    var = dict(norm_mix=v_norm_mix, norm_ffn=v_norm_ffn, norm_ple=v_norm_ple, norm_final=v_norm_final,
               a_w_in=v_a_w_in, a_conv=v_a_conv, a_log=v_a_log, a_dt_bias=v_a_dt_bias, a_norm=v_a_norm,
               a_w_out=v_a_w_out, b_w_in=v_b_w_in, b_sinks=v_b_sinks, b_w_out=v_b_w_out, f_w_up=v_f_w_up,
               f_conv=v_f_conv, f_w_down=v_f_w_down, ple_w_proj=v_ple_w_proj, ple_w_gate=v_ple_w_gate)
    hk = N_HEADS_A * HEAD_DIM_A
    xs = x[0]
    tgt = loss_target[0]
    p_bf = p.astype(BF16)

    def shard(name, layer):
        return _wire(name, wts[name][layer]).astype(BF16)

    def stacked_rows(g):
        return g.reshape(g.shape[0] * g.shape[1], g.shape[2])

    n_in = a_w_in.shape[-1]
    first = _all_gather(jnp.concatenate([shard("a_w_in", 0), shard("a_w_out", 0)]), name="gather_mixer0")
    wa_in_t = jnp.pad(stacked_rows(first[:, :n_in]), ((0, PROJ_A - PROJ_A_REAL), (0, 0)))
    wa_out = stacked_rows(first[:, n_in:])
    gconv = _all_gather(_pack([wts[n] for n in CONVS], F32, 8), dep=first, name="gather_convs")
    conv_full = {n: _cols_full(g) for n, g in zip(CONVS, _unpack(gconv, [wts[n].shape for n in CONVS]))}
    cv_a, cv_f = conv_full["a_conv"][0], conv_full["f_conv"]
    layer_names = ("f_w_up", "f_w_down", "ple_w_proj", "ple_w_gate")
    gather0, tok = _exchange_start([shard(n, 0) for n in layer_names], scatter=False, name="gather_layer0_start", dep=gconv)
    gather1, tok = _exchange_start([shard(n, 0) for n in ("b_w_in", "b_w_out")] + [shard(n, 1) for n in layer_names],
                                   scatter=False, name="gather_layer1_start", dep=tok)

    alog_row = jnp.pad(a_log, ((0, 0), (N_HEADS_A, LANES - 2 * N_HEADS_A)))
    dtb_row = jnp.pad(a_dt_bias, ((0, 0), (N_HEADS_A, LANES - 2 * N_HEADS_A)))

    tile_f32, tile_bf16, rowsum = (F32, "tile"), (BF16, "tile"), (F32, "rowsum")

    def ffn_ple_fwd(i, h_a, n_f, next_norm, w_up_t, w_down, w_pp_t, w_pg):
        up, y, act = _ffn_up_act(n_f, w_up_t, cv_f[i], name=f"l{i}_ffn_up")
        h_b, n_p = _matmul_rows(act, w_down, _epi_res_norm, [h_a], [norm_ple[i]], [tile_f32, tile_bf16],
                                name=f"l{i}_ffn_down")
        pe = _matmul(p_bf[i, 0], w_pp_t, tb=True, name=f"l{i}_ple_proj")
        res = _matmul_rows(n_p, w_pg, _epi_ple, [h_b, pe], [] if next_norm is None else [next_norm],
                           [tile_f32, tile_f32] + ([] if next_norm is None else [tile_bf16]), name=f"l{i}_ple_gate")
        return res[1], (None if next_norm is None else res[2]), dict(n_f=n_f, up=up, y=y, act=act, h_b=h_b, n_p=n_p, zg=res[0], pe=pe)

    def layer_weights(lands):
        up_t, down, pp_t, pg = (stacked_rows(g) for g in lands)
        return up_t.reshape(2, D_FF, D_MODEL), down, pp_t, pg

    n0 = _rms_fwd(xs, norm_mix[0], name="l0_mix_norm")
    proj = _matmul(n0, wa_in_t, tb=True, tm=512, dep=tok, name="l0_in_proj")
    q, k, v, gbc, bbc, y_qkv = _delta_pre_fwd(proj, cv_a, alog_row, dtb_row, name="l0_delta_pre")
    o, states, tinv, og = _delta_fwd(q, k, v, gbc, bbc, proj, a_norm, name="l0_delta")
    h1, nf0 = _matmul_rows(og, wa_out, _epi_res_norm, [xs], [norm_ffn[0]], [tile_f32, tile_bf16], name="l0_mix_out")
    lw0 = layer_weights(_exchange_wait(gather0, h1, scatter=False, name="gather_layer0_wait"))
    h3, n1, sv0 = ffn_ple_fwd(0, h1, nf0, norm_mix[1], *lw0)

    lands1 = _exchange_wait(gather1, h3, scatter=False, name="gather_layer1_wait")
    wb_in_t, wb_out = stacked_rows(lands1[0]), stacked_rows(lands1[1])
    lw1 = layer_weights(lands1[2:])
    pb = _matmul(n1, wb_in_t, tb=True, name="l1_in_qkv")
    att = _swa_fwd(pb, b_sinks, name="l1_swa")
    h4, nf1 = _matmul_rows(att, wb_out, _epi_res_norm, [h3], [norm_ffn[1]], [tile_f32, tile_bf16], name="l1_mix_out")
    h6, _, sv1 = ffn_ple_fwd(1, h4, nf1, None, *lw1)

    loss_row, dh6, d_norm_final = _final_loss(h6, norm_final, tgt, name="final_loss")
    loss = lax.psum(loss_row[0, 0], MESH_AXES)

    def ffn_ple_bwd(i, dh_c, h_a, sv, lw, dep):
        w_up_t, w_down, _, w_pg = lw
        dzg, dpe = _ple_bwd(dh_c, sv["zg"], sv["pe"], name=f"l{i}_ple_mix_bwd")
        d_pg = _matmul(sv["n_p"], dzg, ta=True, out_dtype=BF16, dep=dep, name=f"l{i}_ple_gate_dw")
        d_pp_t = _matmul(dpe, p_bf[i, 0], ta=True, out_dtype=BF16, name=f"l{i}_ple_proj_dw")
        dh_b, d_np = _matmul_rows(dzg, w_pg, _epi_rms_bwd, [sv["h_b"], dh_c], [norm_ple[i]], [tile_f32, rowsum], tb=True,
                                  name=f"l{i}_ple_gate_dx")
        d_down = _matmul(sv["act"], dh_b, ta=True, out_dtype=BF16, name=f"l{i}_ffn_down_dw")
        dup, d_cg, d_cv = _ffn_down_dx_act_bwd(dh_b, w_down, sv["up"], sv["y"], cv_f[i], name=f"l{i}_ffn_down_dx")
        d_up_t = _matmul(dup, sv["n_f"], ta=True, out_dtype=BF16, name=f"l{i}_ffn_up_dw")
        dh_a, d_nf = _matmul_rows(dup, w_up_t, _epi_rms_bwd, [h_a, dh_b], [norm_ffn[i]], [tile_f32, rowsum],
                                  name=f"l{i}_ffn_up_dx")
        mats = [d_up_t.reshape(2 * D_FF, D_MODEL), d_down, d_pp_t, d_pg]
        return dh_a, mats, dict(norm_ple=d_np, norm_ffn=d_nf, f_conv=jnp.concatenate([d_cg, d_cv], axis=1))

    def slabs(g):
        return g.reshape(N_DEV, g.shape[0] // N_DEV, g.shape[1])

    dh4, mats1, g1 = ffn_ple_bwd(1, dh6, h4, sv1, lw1, None)
    datt = _matmul(dh4, wb_out, tb=True, out_dtype=BF16, name="l1_mix_out_dx")
    d_wb_out = _matmul(att, dh4, ta=True, out_dtype=BF16, name="l1_mix_out_dw")
    dq_b, dk_b, dv_b, dsinks = _swa_bwd(pb, b_sinks, datt, name="l1_swa_bwd")
    dpb = jnp.concatenate([dq_b, dk_b.astype(BF16), dv_b.astype(BF16)], axis=1)
    d_wb_in_t = _matmul(dpb, n1, ta=True, out_dtype=BF16, name="l1_in_qkv_dw")
    send1, tok = _exchange_start([slabs(g) for g in [d_wb_in_t, d_wb_out] + mats1], scatter=True,
                                 name="exchange_layer1_start", dep=d_wb_in_t)
    dh3, d_nm1 = _matmul_rows(dpb, wb_in_t, _epi_rms_bwd, [h3, dh4], [norm_mix[1]], [tile_f32, rowsum], name="l1_in_qkv_dx")

    dh1, mats0, g0 = ffn_ple_bwd(0, dh3, h1, sv0, lw0, tok)
    send0, tok = _exchange_start([slabs(g) for g in mats0], scatter=True, name="exchange_layer0_start", dep=mats0[0])
    dog = _matmul(dh1, wa_out, tb=True, dep=tok, name="l0_mix_out_dx")
    d_wa_out = _matmul(og, dh1, ta=True, out_dtype=BF16, name="l0_mix_out_dw")
    dq, dk, dv, dgbc, dbbc, dz0, d_anorm = _delta_bwd(q, k, v, gbc, bbc, tinv, states, o, proj, a_norm, dog,
                                                      name="l0_delta_bwd")
    dproj, d_aconv, d_alog, d_dtb = _delta_pre_bwd(proj, y_qkv, cv_a, alog_row, dtb_row, dq, dk, dv, dgbc, dbbc, dz0,
                                                   name="l0_delta_pre_bwd")
    d_wa_in_t = _matmul(dproj, n0, ta=True, out_dtype=BF16, name="l0_in_proj_dw")
    sendm, tok = _exchange_start([slabs(d_wa_in_t[:PROJ_A_REAL]), slabs(d_wa_out)], scatter=True,
                                 name="exchange_mixer0_start", dep=d_wa_in_t)
    dx, d_nm0 = _matmul_rows(dproj, wa_in_t, _epi_rms_bwd, [xs, dh1], [norm_mix[0]], [tile_f32, rowsum], dep=tok,
                             name="l0_in_proj_dx")

    recv1 = _exchange_wait(send1, dx, scatter=True, name="exchange_layer1_wait")
    recv0 = _exchange_wait(send0, recv1[0], scatter=True, name="exchange_layer0_wait")
    parts = {("b_w_in", 0): recv1[0], ("b_w_out", 0): recv1[1]}
    parts.update({(n, 1): r for n, r in zip(layer_names, recv1[2:])})
    parts.update({(n, 0): r for n, r in zip(layer_names, recv0)})

    outs = {}

    def update_matrix(name):
        w_, m_, v_ = (_wire(name, a) for a in (wts[name], mom[name], var[name]))
        res = None
        for layer in range(w_.shape[0]):
            res = _adamw_layer(parts[(name, layer)], w_, m_, v_, layer, res, name=f"adamw_{name}_{layer}")
        for kind, arr in zip(("grad", "delta", "new_m", "new_v"), res):
            outs[(kind, name)] = _wire(name, arr)
        return res

    last = [update_matrix(n) for n in ("b_w_in", "b_w_out") + layer_names][-1]
    recvm = _exchange_wait(sendm, last[0], scatter=True, name="exchange_mixer0_wait")
    parts.update({("a_w_in", 0): recvm[0], ("a_w_out", 0): recvm[1]})
    update_matrix("a_w_in")
    update_matrix("a_w_out")

    gconvs = dict(a_conv=d_aconv[None], f_conv=jnp.stack([g0["f_conv"], g1["f_conv"]]))
    small_g = dict(norm_mix=jnp.concatenate([d_nm0, d_nm1]), norm_ffn=jnp.concatenate([g0["norm_ffn"], g1["norm_ffn"]]),
                   norm_ple=jnp.concatenate([g0["norm_ple"], g1["norm_ple"]]), norm_final=d_norm_final[0],
                   a_log=d_alog[:, N_HEADS_A:2 * N_HEADS_A], a_dt_bias=d_dtb[:, N_HEADS_A:2 * N_HEADS_A],
                   a_norm=d_anorm, b_sinks=dsinks[:, :N_HEADS_B])
    recv_conv = _all_to_all(_pack_split(gconvs, CONVS, F32, 8), name="exchange_conv_grads")
    recv_small = _all_gather(_pack([small_g[n] for n in SMALL], F32, 8), name="gather_small_grads")
    for names, recv, tag in ((CONVS, recv_conv, "convs"), (SMALL, recv_small, "small")):
        shapes = [wts[n].shape for n in names]
        g_slab = _sum_parts(recv, name=f"sum_{tag}")
        packed = [_pack([d[n] for n in names], F32, 8) for d in (wts, mom, var)]
        res = _adamw_packed(g_slab, *packed, name=f"adamw_{tag}")
        for kind, slab in zip(("grad", "delta", "new_m", "new_v"), (g_slab,) + tuple(res)):
            for n, arr in zip(names, _unpack(slab, shapes)):
                outs[(kind, n)] = arr

    result = [loss, dx[None]]
    for kind in ("grad", "delta", "new_m", "new_v"):
        result += [outs[(kind, n)] for n in WEIGHTS]
    return tuple(result)
```

```python
import functools
import math

import jax
import jax.numpy as jnp
from jax import lax
from jax.experimental import pallas as pl
from jax.experimental.pallas import tpu as pltpu

F32 = jnp.float32
BF16 = jnp.bfloat16

D_MODEL = 1024
N_HEADS_A = 8
HEAD_DIM_A = 128
CONV_A = 4
CHUNK = 128
N_HEADS_B = 16
N_KV_B = 4
GROUP_B = N_HEADS_B // N_KV_B
HEAD_DIM_B = 64
WINDOW = 128
D_FF = 2816
FFN_CONV = 3
PLE_DIM = 256
EPS = 1e-6
N_DEV = 8
HALO = 8
PROJ_A_REAL = 4 * N_HEADS_A * HEAD_DIM_A + 2 * N_HEADS_A
PROJ_A = 4 * N_HEADS_A * HEAD_DIM_A + 128
Z_COL_BLOCK = 3
BA_COL_BLOCK = 32

ADAM_LR = 0.001
ADAM_B1 = 0.9
ADAM_B2 = 0.999
ADAM_EPS = 1e-08
ADAM_WD = 0.01
ADAM_STEP = 10

LANES = 128
VMEM_LIMIT_BYTES = 56 * 1024 * 1024
NEG_BIG = -1e30

MESH_AXES = ("x", "y", "c")


def _params(sem=None):
    return pltpu.CompilerParams(dimension_semantics=sem, vmem_limit_bytes=VMEM_LIMIT_BYTES)


def _tile(n, target):
    best = None
    for t in range(LANES, min(n, target) + 1, LANES):
        if n % t == 0:
            best = t
    return best or n


def _sigmoid(x):
    return 0.5 * jnp.tanh(0.5 * x) + 0.5


def _softplus(x):
    return jnp.maximum(x, 0.0) + jnp.log1p(jnp.exp(-jnp.abs(x)))


def _matmul(a, b, *, name, ta=False, tb=False, res=None, out_dtype=F32, tm=1408, tn=1408, tk=None, dep=None):
    sa, sb = a.ndim == 3, b.ndim == 3
    ns = a.shape[0] if sa else (b.shape[0] if sb else 1)
    contract_stack = sa and sb
    out_stacked = sa != sb
    m = a.shape[-1] if ta else a.shape[-2]
    k = a.shape[-2] if ta else a.shape[-1]
    n = b.shape[-2] if tb else b.shape[-1]
    assert (b.shape[-1] if tb else b.shape[-2]) == k, (a.shape, b.shape, ta, tb)
    if tk is None:
        tk = 1024 if ta else 2816
    tm, tn, tk = _tile(m, tm), _tile(n, tn), _tile(k, tk)
    nk = k // tk
    nsteps = nk * (ns if contract_stack else 1)
    dims = (((0 if ta else 1,), (1 if tb else 0,)), ((), ()))

    def spec(block, stacked, order):
        def index(g, i, j, kk):
            two = order(i, j, kk % nk)
            if not stacked:
                return two
            return (kk // nk if contract_stack else g,) + two
        return pl.BlockSpec(((None,) if stacked else ()) + block, index)

    a_spec = spec((tk, tm), sa, lambda i, j, kq: (kq, i)) if ta else spec((tm, tk), sa, lambda i, j, kq: (i, kq))
    b_spec = spec((tn, tk), sb, lambda i, j, kq: (j, kq)) if tb else spec((tk, tn), sb, lambda i, j, kq: (kq, j))
    o_spec = spec((tm, tn), out_stacked, lambda i, j, kq: (i, j))
    has_res = res is not None
    has_dep = dep is not None

    def body(*refs):
        a_ref, b_ref = refs[0], refs[1]
        r_ref = refs[2] if has_res else None
        o_ref = refs[2 + has_res + has_dep]
        part = lax.dot_general(a_ref[...].astype(BF16), b_ref[...].astype(BF16), dims, preferred_element_type=F32)

        def finish(acc):
            if has_res:
                acc = acc + r_ref[...].astype(F32)
            o_ref[...] = acc.astype(out_dtype)

        if nsteps == 1:
            finish(part)
        else:
            acc_ref = refs[-1]
            kk = pl.program_id(3)

            @pl.when(kk == 0)
            def _():
                acc_ref[...] = part

            @pl.when(kk > 0)
            def _():
                acc_ref[...] += part

            @pl.when(kk == nsteps - 1)
            def _():
                finish(acc_ref[...])

    in_specs = [a_spec, b_spec] + ([o_spec] if has_res else []) + ([pl.BlockSpec(memory_space=pl.ANY)] if has_dep else [])
    args = (a, b) + ((res,) if has_res else ()) + ((dep,) if has_dep else ())
    return pl.pallas_call(
        body,
        name=name,
        grid=(ns if out_stacked else 1, m // tm, n // tn, nsteps),
        in_specs=in_specs,
        out_specs=o_spec,
        out_shape=jax.ShapeDtypeStruct(((ns,) if out_stacked else ()) + (m, n), out_dtype),
        scratch_shapes=[pltpu.VMEM((tm, tn), F32)] if nsteps > 1 else [],
        compiler_params=_params(("parallel", "parallel", "parallel", "arbitrary")),
    )(*args)


EPI_ROWS = 32


def _matmul_rows(a, b, epilogue, tiles_in, rows_in, outs, *, name, tb=False, tm=512, tk=None, dep=None):
    stacked = a.ndim == 3
    ns = a.shape[0] if stacked else 1
    m, k = a.shape[-2], a.shape[-1]
    n = b.shape[-2] if tb else b.shape[-1]
    assert (b.shape[-1] if tb else b.shape[-2]) == k and (b.ndim == 3) == stacked, (a.shape, b.shape, tb)
    tm, tk = _tile(m, tm), _tile(k, 2816 if tk is None else tk)
    nk = k // tk
    nsteps = nk * ns
    dims = (((1,), (1 if tb else 0,)), ((), ()))
    lead = (None,) if stacked else ()
    front = (lambda kk: (kk // nk,)) if stacked else (lambda kk: ())
    a_spec = pl.BlockSpec(lead + (tm, tk), lambda i, kk: front(kk) + (i, kk % nk))
    if tb:
        b_spec = pl.BlockSpec(lead + (n, tk), lambda i, kk: front(kk) + (0, kk % nk))
    else:
        b_spec = pl.BlockSpec(lead + (tk, n), lambda i, kk: front(kk) + (kk % nk, 0))
    tile_spec = pl.BlockSpec((tm, n), lambda i, kk: (i, 0))
    row_spec = pl.BlockSpec((1, n), lambda i, kk: (0, 0))
    n_t, n_r, has_dep = len(tiles_in), len(rows_in), dep is not None

    def body(*refs):
        a_ref, b_ref = refs[:2]
        tile_refs = refs[2:2 + n_t]
        row_refs = refs[2 + n_t:2 + n_t + n_r]
        out_refs = refs[2 + n_t + n_r + has_dep:-1]
        acc_ref = refs[-1]
        part = lax.dot_general(a_ref[...].astype(BF16), b_ref[...].astype(BF16), dims, preferred_element_type=F32)
        kk = pl.program_id(1)
        if nsteps == 1:
            acc_ref[...] = part
        else:
            @pl.when(kk == 0)
            def _():
                acc_ref[...] = part

            @pl.when(kk > 0)
            def _():
                acc_ref[...] += part

        @pl.when(kk == nsteps - 1)
        def _():
            epilogue(acc_ref, tile_refs, row_refs, out_refs, pl.program_id(0) == 0)

    return pl.pallas_call(
        body,
        name=name,
        grid=(m // tm, nsteps),
        in_specs=[a_spec, b_spec] + [tile_spec] * n_t + [row_spec] * n_r + ([pl.BlockSpec(memory_space=pl.ANY)] if has_dep else []),
        out_specs=[tile_spec if kind == "tile" else row_spec for _, kind in outs],
        out_shape=[jax.ShapeDtypeStruct((m, n) if kind == "tile" else (1, n), dt) for dt, kind in outs],
        scratch_shapes=[pltpu.VMEM((tm, n), F32)],
        compiler_params=_params(("arbitrary", "arbitrary")),
    )(a, b, *tiles_in, *[r.reshape(1, n) for r in rows_in], *((dep,) if has_dep else ()))


def _row_chunks(ref):
    return [pl.ds(r, EPI_ROWS) for r in range(0, ref.shape[0], EPI_ROWS)]


def _rstd(x):
    return lax.rsqrt(jnp.mean(x * x, axis=-1, keepdims=True) + EPS)


def _epi_res_norm(acc, tiles, rows, outs, first):
    (res,), (w,), (h_out, n_out) = tiles, rows, outs
    chunks = _row_chunks(acc)
    for rs in chunks:
        h_out[rs, :] = acc[rs, :] + res[rs, :]
    rstds = [_rstd(h_out[rs, :]) for rs in chunks]
    for rs, r in zip(chunks, rstds):
        n_out[rs, :] = (h_out[rs, :] * r * w[...]).astype(BF16)


def _epi_ple(acc, tiles, rows, outs, first):
    hb, pe = tiles
    chunks = _row_chunks(acc)
    for rs in chunks:
        zg = acc[rs, :]
        outs[0][rs, :] = zg
        outs[1][rs, :] = hb[rs, :] + _sigmoid(zg) * pe[rs, :]
    if rows:
        rstds = [_rstd(outs[1][rs, :]) for rs in chunks]
        for rs, r in zip(chunks, rstds):
            outs[2][rs, :] = (outs[1][rs, :] * r * rows[0][...]).astype(BF16)


def _epi_rms_bwd(acc, tiles, rows, outs, first):
    (h_ref, skip), (w,), (dh_out, dw_out) = tiles, rows, outs
    chunks = _row_chunks(acc)
    n = acc.shape[1]
    rstds = [_rstd(h_ref[rs, :]) for rs in chunks]
    dots = [jnp.sum(acc[rs, :] * w[...] * h_ref[rs, :], axis=-1, keepdims=True) * r * (1.0 / n)
            for rs, r in zip(chunks, rstds)]
    dw = jnp.zeros((1, n), F32)
    for rs, r, dt in zip(chunks, rstds, dots):
        nh = h_ref[rs, :] * r
        g = acc[rs, :]
        dh_out[rs, :] = r * (g * w[...] - nh * dt) + skip[rs, :]
        dw = dw + jnp.sum(g * nh, axis=0, keepdims=True)

    @pl.when(first)
    def _():
        dw_out[...] = dw

    @pl.when(jnp.logical_not(first))
    def _():
        dw_out[...] += dw


def _rms_fwd(h, w, *, name, tm=512):
    t, d = h.shape
    tm = _tile(t, tm)

    def body(h_ref, w_ref, o_ref):
        x = h_ref[...]
        r = lax.rsqrt(jnp.mean(x * x, axis=-1, keepdims=True) + EPS)
        o_ref[...] = (x * r * w_ref[...]).astype(BF16)

    return pl.pallas_call(
        body,
        name=name,
        grid=(t // tm,),
        in_specs=[pl.BlockSpec((tm, d), lambda i: (i, 0)), pl.BlockSpec((1, d), lambda i: (0, 0))],
        out_specs=pl.BlockSpec((tm, d), lambda i: (i, 0)),
        out_shape=jax.ShapeDtypeStruct((t, d), BF16),
        compiler_params=_params(("parallel",)),
    )(h, w.reshape(1, d))


def _rms_bwd(h, w, dn, skip, *, name, tm=512):
    t, d = h.shape
    tm = _tile(t, tm)

    def body(h_ref, w_ref, dn_ref, skip_ref, dh_ref, dw_ref):
        i = pl.program_id(0)
        x = h_ref[...]
        r = lax.rsqrt(jnp.mean(x * x, axis=-1, keepdims=True) + EPS)
        nh = x * r
        g = dn_ref[...].astype(F32)
        gw = g * w_ref[...]
        dh_ref[...] = r * (gw - nh * jnp.mean(gw * nh, axis=-1, keepdims=True)) + skip_ref[...]
        part = jnp.sum(g * nh, axis=0, keepdims=True)

        @pl.when(i == 0)
        def _():
            dw_ref[...] = part

        @pl.when(i > 0)
        def _():
            dw_ref[...] += part

    row = pl.BlockSpec((tm, d), lambda i: (i, 0))
    vec = pl.BlockSpec((1, d), lambda i: (0, 0))
    return pl.pallas_call(
        body,
        name=name,
        grid=(t // tm,),
        in_specs=[row, vec, row, row],
        out_specs=[row, vec],
        out_shape=[jax.ShapeDtypeStruct((t, d), F32), jax.ShapeDtypeStruct((1, d), F32)],
        compiler_params=_params(("arbitrary",)),
    )(h, w.reshape(1, d), dn, skip)


def _final_loss(h, w, target, *, name, tm=512):
    t, d = h.shape
    tm = _tile(t, tm)

    def body(h_ref, w_ref, tg_ref, loss_ref, dh_ref, dw_ref):
        i = pl.program_id(0)
        x = h_ref[...]
        r = lax.rsqrt(jnp.mean(x * x, axis=-1, keepdims=True) + EPS)
        nh = x * r
        err = nh * w_ref[...] - tg_ref[...]
        lpart = (0.5 / d) * jnp.sum(jnp.sum(err * err, axis=-1, keepdims=True), axis=0, keepdims=True)
        g = err * (1.0 / d)
        gw = g * w_ref[...]
        dh_ref[...] = r * (gw - nh * jnp.mean(gw * nh, axis=-1, keepdims=True))
        part = jnp.sum(g * nh, axis=0, keepdims=True)
        lrow = jnp.broadcast_to(lpart, (1, LANES))

        @pl.when(i == 0)
        def _():
            dw_ref[...] = part
            loss_ref[...] = lrow

        @pl.when(i > 0)
        def _():
            dw_ref[...] += part
            loss_ref[...] += lrow

    row = pl.BlockSpec((tm, d), lambda i: (i, 0))
    vec = pl.BlockSpec((1, d), lambda i: (0, 0))
    return pl.pallas_call(
        body,
        name=name,
        grid=(t // tm,),
        in_specs=[row, vec, row],
        out_specs=[pl.BlockSpec((1, LANES), lambda i: (0, 0)), row, vec],
        out_shape=[jax.ShapeDtypeStruct((1, LANES), F32), jax.ShapeDtypeStruct((t, d), F32), jax.ShapeDtypeStruct((1, d), F32)],
        compiler_params=_params(("arbitrary",)),
    )(h, w.reshape(1, d), target)


def _conv_from_ext(ext_ref, cw_ref, kw, tm):
    y = cw_ref[kw - 1:kw, :] * ext_ref[pl.ds(HALO, tm), :]
    for i in range(kw - 1):
        y = y + cw_ref[i:i + 1, :] * ext_ref[pl.ds(HALO - (kw - 1) + i, tm), :]
    return y


ROW_CHUNK = 64


def _shifted_rows(src_ref, base, rows, shifts, cols=slice(None)):
    ext = src_ref[pl.ds(base - HALO, rows + HALO), cols]
    return [ext[HALO:, :] if s == 0 else pltpu.roll(ext, s, 0)[HALO:, :] for s in shifts]


def _conv_rows(src_ref, base, rows, cw_ref, kw, cols=slice(None)):
    wins = _shifted_rows(src_ref, base, rows, range(kw), cols)
    y = cw_ref[kw - 1:kw, cols] * wins[0]
    for s in range(1, kw):
        y = y + cw_ref[kw - 1 - s:kw - s, cols] * wins[s]
    return y


def _ahead_rows(src_ref, base, rows, shifts, cols=slice(None)):
    ext = src_ref[pl.ds(base, rows + HALO), cols]
    return [ext[:rows, :] if s == 0 else pltpu.roll(ext, rows + HALO - s, 0)[:rows, :] for s in shifts]


def _conv_t_rows(dy_ref, base, rows, cw_ref, kw, cols=slice(None)):
    wins = _ahead_rows(dy_ref, base, rows, range(kw), cols)
    dx = cw_ref[kw - 1:kw, cols] * wins[0]
    for s in range(1, kw):
        dx = dx + cw_ref[kw - 1 - s:kw - s, cols] * wins[s]
    return dx


def _fold_rows(x):
    out = x[0:HALO, :]
    for g in range(1, x.shape[0] // HALO):
        out = out + x[g * HALO:(g + 1) * HALO, :]
    return out


def _conv_bwd_from_ext(xext_ref, dyext_ref, cw_ref, dcw_ref, kw, tm, first):
    dy = dyext_ref[pl.ds(0, tm), :]
    dx = cw_ref[kw - 1:kw, :] * dy
    for i in range(kw - 1):
        dx = dx + cw_ref[i:i + 1, :] * dyext_ref[pl.ds(kw - 1 - i, tm), :]
    for i in range(kw):
        part = jnp.sum(dy * xext_ref[pl.ds(HALO - (kw - 1) + i, tm), :], axis=0, keepdims=True)

        @pl.when(first)
        def _():
            dcw_ref[i:i + 1, :] = part

        @pl.when(jnp.logical_not(first))
        def _():
            dcw_ref[i:i + 1, :] += part

    return dx


def _delta_pre_fwd(proj, conv_w, alog_row, dtb_row, *, name, tm=256):
    t = proj.shape[0]
    c3 = 3 * N_HEADS_A * HEAD_DIM_A
    hk = N_HEADS_A * HEAD_DIM_A
    tm = _tile(t, tm)

    rc = min(ROW_CHUNK, tm)

    def body(x_ref, ba_ref, cw_ref, al_ref, db_ref, q_ref, k_ref, v_ref, g_ref, b_ref, y_ref, hx):
        i = pl.program_id(0)

        @pl.when(i == 0)
        def _():
            hx[0:HALO, :] = jnp.zeros((HALO, c3), F32)

        hx[pl.ds(HALO, rc), :] = x_ref[0:rc, :]
        dsts = (q_ref, k_ref, v_ref)
        for r in range(tm // rc):
            rows = slice(r * rc, (r + 1) * rc)
            src, base = (hx, HALO) if r == 0 else (x_ref, r * rc)
            for cb in range(c3 // HEAD_DIM_A):
                cols = slice(cb * HEAD_DIM_A, (cb + 1) * HEAD_DIM_A)
                y = _conv_rows(src, base, rc, cw_ref, CONV_A, cols)
                y_ref[rows, cols] = y
                s = y * _sigmoid(y)
                kind, h = divmod(cb, N_HEADS_A)
                if kind < 2:
                    s = s * lax.rsqrt(jnp.sum(s * s, axis=-1, keepdims=True) + EPS)
                dsts[kind][rows, h * HEAD_DIM_A:(h + 1) * HEAD_DIM_A] = s
        hx[0:HALO, :] = x_ref[tm - HALO:tm, :]
        ba = ba_ref[...]
        beta = _sigmoid(ba)
        gfull = -jnp.exp(al_ref[...]) * _softplus(ba + db_ref[...])
        for h in range(N_HEADS_A):
            lo = h * HEAD_DIM_A
            b_ref[:, lo:lo + HEAD_DIM_A] = jnp.broadcast_to(beta[:, h:h + 1], (tm, HEAD_DIM_A))
            g_ref[:, lo:lo + HEAD_DIM_A] = jnp.broadcast_to(gfull[:, N_HEADS_A + h:N_HEADS_A + h + 1], (tm, HEAD_DIM_A))

    row = lambda w: pl.BlockSpec((tm, w), lambda i: (i, 0))
    fixed = lambda r, w: pl.BlockSpec((r, w), lambda i: (0, 0))
    out = jax.ShapeDtypeStruct((t, hk), F32)
    return pl.pallas_call(
        body,
        name=name,
        grid=(t // tm,),
        in_specs=[row(c3), pl.BlockSpec((tm, LANES), lambda i: (i, BA_COL_BLOCK)), fixed(CONV_A, c3), fixed(1, LANES),
                  fixed(1, LANES)],
        out_specs=[row(hk)] * 5 + [row(c3)],
        out_shape=[out] * 5 + [jax.ShapeDtypeStruct((t, c3), F32)],
        scratch_shapes=[pltpu.VMEM((HALO + rc, c3), F32)],
        compiler_params=_params(("arbitrary",)),
    )(proj, proj, conv_w, alog_row, dtb_row)


def _delta_pre_bwd(proj, y, conv_w, alog_row, dtb_row, dq, dk, dv, dg, db, dz, *, name, tm=256):
    t, pw = proj.shape
    c3 = 3 * N_HEADS_A * HEAD_DIM_A
    hk = N_HEADS_A * HEAD_DIM_A
    tm = _tile(t, tm)
    nt = t // tm

    rc = min(ROW_CHUNK, tm)
    kw = CONV_A

    def body(x_ref, y_ref, ba_ref, cw_ref, al_ref, db_ref, dq_ref, dk_ref, dv_ref, dg_ref, dbt_ref, dz_ref,
             dp_ref, dcw_ref, dal_ref, ddb_ref, *scratch):
        dys, acc = scratch[:-1], scratch[-1]
        i = pl.program_id(0)
        first = i == 0

        @pl.when(first)
        def _():
            for dyb in dys:
                dyb[pl.ds(tm, HALO), :] = jnp.zeros((HALO, HEAD_DIM_A), F32)

        srcs = (dq_ref, dk_ref, dv_ref)
        ncb = c3 // HEAD_DIM_A
        taps = [[jnp.zeros((HALO, HEAD_DIM_A), F32) for _ in range(kw)] for _ in range(ncb)]
        for r in reversed(range(tm // rc)):
            rows = slice(r * rc, (r + 1) * rc)
            for kind in range(3):
                cbs = range(kind * N_HEADS_A, (kind + 1) * N_HEADS_A)
                hs = [slice(h * HEAD_DIM_A, (h + 1) * HEAD_DIM_A) for h in range(N_HEADS_A)]
                ys = [y_ref[rows, cb * HEAD_DIM_A:(cb + 1) * HEAD_DIM_A] for cb in cbs]
                sgs = [_sigmoid(yv) for yv in ys]
                dss = [srcs[kind][rows, hsl] for hsl in hs]
                if kind < 2:
                    ss = [yv * sg for yv, sg in zip(ys, sgs)]
                    rns = [lax.rsqrt(jnp.sum(s * s, axis=-1, keepdims=True) + EPS) for s in ss]
                    qns = [s * rn for s, rn in zip(ss, rns)]
                    dots = [jnp.sum(ds * qn, axis=-1, keepdims=True) for ds, qn in zip(dss, qns)]
                    dss = [rn * (ds - qn * dt) for rn, ds, qn, dt in zip(rns, dss, qns, dots)]
                for cb, yv, sg, ds in zip(cbs, ys, sgs, dss):
                    cols = slice(cb * HEAD_DIM_A, (cb + 1) * HEAD_DIM_A)
                    dys[cb][rows, :] = ds * (sg * (1.0 + yv * (1.0 - sg)))
                    ahead = _ahead_rows(dys[cb], r * rc, rc, range(kw))
                    dp_ref[rows, cols] = sum(cw_ref[kw - 1 - s:kw - s, cols] * ahead[s] for s in range(kw)).astype(BF16)
                    xv = x_ref[rows, cols]
                    for s in range(kw):
                        taps[cb][kw - 1 - s] = taps[cb][kw - 1 - s] + _fold_rows(xv * ahead[s])
        for cb in range(ncb):
            cols = slice(cb * HEAD_DIM_A, (cb + 1) * HEAD_DIM_A)
            for j in range(kw):
                acc[j * HALO:(j + 1) * HALO, cols] = taps[cb][j]
            dys[cb][pl.ds(tm, HALO), :] = dys[cb][0:HALO, :]
        for j in range(kw):
            tap = jnp.sum(acc[j * HALO:(j + 1) * HALO, :], axis=0, keepdims=True)

            @pl.when(first)
            def _():
                dcw_ref[j:j + 1, :] = tap

            @pl.when(jnp.logical_not(first))
            def _():
                dcw_ref[j:j + 1, :] += tap

        dp_ref[:, c3:c3 + hk] = dz_ref[...]

        lane = lax.broadcasted_iota(jnp.int32, (tm, LANES), 1)
        gcol = jnp.zeros((tm, LANES), F32)
        for h in range(N_HEADS_A):
            lo = h * HEAD_DIM_A
            dbh = jnp.sum(dbt_ref[:, lo:lo + HEAD_DIM_A], axis=-1, keepdims=True)
            dgh = jnp.sum(dg_ref[:, lo:lo + HEAD_DIM_A], axis=-1, keepdims=True)
            gcol = gcol + jnp.where(lane == h, dbh, 0.0) + jnp.where(lane == N_HEADS_A + h, dgh, 0.0)
        ba = ba_ref[...]
        beta = _sigmoid(ba)
        a_neg = -jnp.exp(al_ref[...])
        z = ba + db_ref[...]
        dz = gcol * a_neg * _sigmoid(z)
        is_g = jnp.logical_and(lane >= N_HEADS_A, lane < 2 * N_HEADS_A)
        dba = jnp.where(lane < N_HEADS_A, gcol * beta * (1.0 - beta), jnp.where(is_g, dz, 0.0))
        dp_ref[:, c3 + hk:pw] = dba.astype(BF16)
        dal = jnp.sum(jnp.where(is_g, gcol * a_neg * _softplus(z), 0.0), axis=0, keepdims=True)
        ddb = jnp.sum(jnp.where(is_g, dz, 0.0), axis=0, keepdims=True)

        @pl.when(first)
        def _():
            dal_ref[...] = dal
            ddb_ref[...] = ddb

        @pl.when(jnp.logical_not(first))
        def _():
            dal_ref[...] += dal
            ddb_ref[...] += ddb

    rev = lambda w: pl.BlockSpec((tm, w), lambda i: (nt - 1 - i, 0))
    fixed = lambda r, w: pl.BlockSpec((r, w), lambda i: (0, 0))
    return pl.pallas_call(
        body,
        name=name,
        grid=(nt,),
        in_specs=[rev(c3), rev(c3), pl.BlockSpec((tm, LANES), lambda i: (nt - 1 - i, BA_COL_BLOCK)), fixed(CONV_A, c3),
                  fixed(1, LANES), fixed(1, LANES)] + [rev(hk)] * 6,
        out_specs=[rev(pw), fixed(CONV_A, c3), fixed(1, LANES), fixed(1, LANES)],
        out_shape=[jax.ShapeDtypeStruct((t, pw), BF16), jax.ShapeDtypeStruct((CONV_A, c3), F32),
                   jax.ShapeDtypeStruct((1, LANES), F32), jax.ShapeDtypeStruct((1, LANES), F32)],
        scratch_shapes=[pltpu.VMEM((tm + HALO, HEAD_DIM_A), F32)] * (c3 // HEAD_DIM_A) + [pltpu.VMEM((CONV_A * HALO, c3), F32)],
        compiler_params=_params(("arbitrary",)),
    )(proj, y, proj, conv_w, alog_row, dtb_row, dq, dk, dv, dg, db, dz)


def _gated_norm_fwd(o, proj, w, *, name, tm=512):
    t, d = o.shape
    tm = _tile(t, tm)

    def body(o_ref, z_ref, w_ref, y_ref):
        for h in range(N_HEADS_A):
            sl = slice(h * HEAD_DIM_A, (h + 1) * HEAD_DIM_A)
            oh = o_ref[:, sl]
            zh = z_ref[:, sl]
            r = lax.rsqrt(jnp.mean(oh * oh, axis=-1, keepdims=True) + EPS)
            y_ref[:, sl] = (oh * r * w_ref[...] * (zh * _sigmoid(zh))).astype(BF16)

    row = pl.BlockSpec((tm, d), lambda i: (i, 0))
    return pl.pallas_call(
        body,
        name=name,
        grid=(t // tm,),
        in_specs=[row, pl.BlockSpec((tm, d), lambda i: (i, Z_COL_BLOCK)), pl.BlockSpec((1, HEAD_DIM_A), lambda i: (0, 0))],
        out_specs=row,
        out_shape=jax.ShapeDtypeStruct((t, d), BF16),
        compiler_params=_params(("parallel",)),
    )(o, proj, w)


def _gated_norm_bwd(o, proj, w, dy, *, name, tm=512):
    t, d = o.shape
    tm = _tile(t, tm)

    def body(o_ref, z_ref, w_ref, dy_ref, do_ref, dz_ref, dw_ref):
        i = pl.program_id(0)
        dw = jnp.zeros((1, HEAD_DIM_A), F32)
        for h in range(N_HEADS_A):
            sl = slice(h * HEAD_DIM_A, (h + 1) * HEAD_DIM_A)
            oh = o_ref[:, sl]
            zh = z_ref[:, sl]
            g = dy_ref[:, sl]
            r = lax.rsqrt(jnp.mean(oh * oh, axis=-1, keepdims=True) + EPS)
            nh = oh * r
            sg = _sigmoid(zh)
            dz_ref[:, sl] = (g * nh * w_ref[...] * (sg * (1.0 + zh * (1.0 - sg)))).astype(BF16)
            dt = g * (zh * sg)
            dw = dw + jnp.sum(dt * nh, axis=0, keepdims=True)
            dnh = dt * w_ref[...]
            do_ref[:, sl] = r * (dnh - nh * jnp.mean(dnh * nh, axis=-1, keepdims=True))

        @pl.when(i == 0)
        def _():
            dw_ref[...] = dw

        @pl.when(i > 0)
        def _():
            dw_ref[...] += dw

    row = pl.BlockSpec((tm, d), lambda i: (i, 0))
    vec = pl.BlockSpec((1, HEAD_DIM_A), lambda i: (0, 0))
    return pl.pallas_call(
        body,
        name=name,
        grid=(t // tm,),
        in_specs=[row, pl.BlockSpec((tm, d), lambda i: (i, Z_COL_BLOCK)), vec, row],
        out_specs=[row, row, vec],
        out_shape=[jax.ShapeDtypeStruct((t, d), F32), jax.ShapeDtypeStruct((t, d), BF16),
                   jax.ShapeDtypeStruct((1, HEAD_DIM_A), F32)],
        compiler_params=_params(("arbitrary",)),
    )(o, proj, w, dy)


_NN = (((1,), (0,)), ((), ()))
_NT = (((1,), (1,)), ((), ()))
_TN = (((0,), (0,)), ((), ()))
_DIMS = {"nn": _NN, "nt": _NT, "tn": _TN}


def _raw_dot(a, b, kind, prec):
    dims = _DIMS[kind]
    a_hi, b_hi = a.astype(BF16), b.astype(BF16)
    out = lax.dot_general(a_hi, b_hi, dims, preferred_element_type=F32)
    if prec == "x3":
        a_lo = (a - a_hi.astype(F32)).astype(BF16)
        b_lo = (b - b_hi.astype(F32)).astype(BF16)
        out = out + lax.dot_general(a_hi, b_lo, dims, preferred_element_type=F32)
        out = out + lax.dot_general(a_lo, b_hi, dims, preferred_element_type=F32)
    elif prec == "s3":
        r1 = b - b_hi.astype(F32)
        b_mid = r1.astype(BF16)
        b_lo = (r1 - b_mid.astype(F32)).astype(BF16)
        out = out + lax.dot_general(a_hi, b_mid, dims, preferred_element_type=F32)
        out = out + lax.dot_general(a_hi, b_lo, dims, preferred_element_type=F32)
    return out


def _raw_dots(xs, ys, kind, prec):
    return [_raw_dot(x, y, kind, prec) for x, y in zip(xs, ys)]


@functools.partial(jax.custom_vjp, nondiff_argnums=(2, 3))
def _dots(xs, ys, kind, prec):
    return _raw_dots(xs, ys, kind, prec)


def _dots_fwd(xs, ys, kind, prec):
    return _raw_dots(xs, ys, kind, prec), (xs, ys)


def _dots_bwd(kind, prec, saved, gs):
    xs, ys = saved
    if kind == "nn":
        return _raw_dots(gs, ys, "nt", prec), _raw_dots(xs, gs, "tn", prec)
    if kind == "nt":
        return _raw_dots(gs, ys, "nn", prec), _raw_dots(gs, xs, "tn", prec)
    return _raw_dots(ys, gs, "nt", prec), _raw_dots(xs, gs, "nn", prec)


_dots.defvjp(_dots_fwd, _dots_bwd)


def _eye(c):
    return (lax.broadcasted_iota(jnp.int32, (c, c), 0) == lax.broadcasted_iota(jnp.int32, (c, c), 1)).astype(F32)


def _inv_unit_lower_raw(lmats):
    c = lmats[0].shape[0]
    eye = _eye(c)
    xs = [eye - l for l in lmats]
    ps = lmats
    for _ in range(int(math.log2(c)) - 1):
        ps = _raw_dots(ps, ps, "nn", "bf16")
        xs = [x + d for x, d in zip(xs, _raw_dots(xs, ps, "nn", "bf16"))]
    rs = [x - eye + d for x, d in zip(xs, _raw_dots(lmats, xs, "nn", "x3"))]
    return [x - d for x, d in zip(xs, _raw_dots(xs, rs, "nn", "bf16"))]


@jax.custom_vjp
def _inv_unit_lower(lmats, hints):
    return _inv_unit_lower_raw(lmats) if hints is None else hints


def _inv_fwd(lmats, hints):
    tms = _inv_unit_lower_raw(lmats) if hints is None else hints
    return tms, (tms, hints)


def _inv_bwd(saved, gs):
    tms, hints = saved
    ds = [-d for d in _raw_dots(_raw_dots(tms, gs, "tn", "x3"), tms, "nt", "x3")]
    return ds, (None if hints is None else [jnp.zeros_like(h) for h in hints])


_inv_unit_lower.defvjp(_inv_fwd, _inv_bwd)


def _delta_prep(qs, ks, vs, gs, bs, hints=None):
    c = qs[0].shape[0]
    nh = len(qs)
    ii = lax.broadcasted_iota(jnp.int32, (c, c), 0)
    jj = lax.broadcasted_iota(jnp.int32, (c, c), 1)
    incl = ii >= jj
    strict = ii > jj
    ltri = incl.astype(F32)
    eye = _eye(c)
    m1 = _dots([ltri] * nh, gs, "nn", "s3")
    gtot = [jnp.sum(g, axis=0, keepdims=True) for g in gs]
    decay = [jnp.exp(jnp.where(incl, m - m.T, NEG_BIG)) for m in m1]
    eg = [jnp.exp(m) for m in m1]
    kk = _dots(ks, ks, "nt", "bf16")
    lmats = [jnp.where(strict, b * x * d, 0.0) for b, x, d in zip(bs, kk, decay)]
    tinv = _inv_unit_lower(lmats, hints)
    toff = [t - eye for t in tinv]
    bv = [b * v for b, v in zip(bs, vs)]
    bk = [b * e * k for b, e, k in zip(bs, eg, ks)]
    u0 = [x + d for x, d in zip(bv, _dots(toff, bv, "nn", "bf16"))]
    wk = [x + d for x, d in zip(bk, _dots(toff, bk, "nn", "bf16"))]
    qsc = [q * (HEAD_DIM_A ** -0.5) for q in qs]
    qk = [x * d for x, d in zip(_dots(qsc, ks, "nt", "bf16"), decay)]
    q_dec = [q * e for q, e in zip(qsc, eg)]
    k_dec = [k * jnp.exp(t - m) for k, t, m in zip(ks, gtot, m1)]
    glast = [jnp.broadcast_to(jnp.exp(t), (c, c)) for t in gtot]
    return (u0, wk, qk, q_dec, k_dec, glast), tinv


def _delta_step(ss, u0, wk, qk, q_dec, k_dec, glast):
    us = [a - d for a, d in zip(u0, _dots(wk, ss, "nn", "bf16"))]
    os_ = [a + d for a, d in zip(_dots(q_dec, ss, "nn", "bf16"), _dots(qk, us, "nn", "bf16"))]
    s_new = [g * s + d for g, s, d in zip(glast, ss, _dots(k_dec, us, "tn", "bf16"))]
    return os_, s_new


HEADS_PER_STEP = 8


def _chunk_spec(nc, reverse=False):
    w = HEADS_PER_STEP * HEAD_DIM_A
    if reverse:
        return pl.BlockSpec((CHUNK, w), lambda h, n: (nc - 1 - n, h))
    return pl.BlockSpec((CHUNK, w), lambda h, n: (n, h))


def _head_slices():
    return [slice(j * HEAD_DIM_A, (j + 1) * HEAD_DIM_A) for j in range(HEADS_PER_STEP)]


def _heads(ref):
    return [ref[:, sl] for sl in _head_slices()]


def _delta_prep_fwd(q, k, v, gbc, bbc, *, name):
    t, d = q.shape
    nc = t // CHUNK

    def body(q_ref, k_ref, v_ref, g_ref, b_ref, *outs):
        res, tinv = _delta_prep(*[_heads(r) for r in (q_ref, k_ref, v_ref, g_ref, b_ref)])
        for ref, vals in zip(outs, res + (tinv,)):
            for sl, val in zip(_head_slices(), vals):
                ref[:, sl] = val

    spec = _chunk_spec(nc)
    return pl.pallas_call(
        body,
        name=name,
        grid=(N_HEADS_A // HEADS_PER_STEP, nc),
        in_specs=[spec] * 5,
        out_specs=[spec] * 7,
        out_shape=[jax.ShapeDtypeStruct((t, d), F32)] * 7,
        compiler_params=_params(("parallel", "parallel")),
    )(q, k, v, gbc, bbc)


def _delta_prep_bwd(q, k, v, gbc, bbc, tinv, cts, *, name):
    t, d = q.shape
    nc = t // CHUNK

    def body(q_ref, k_ref, v_ref, g_ref, b_ref, t_ref, c0, c1, c2, c3, c4, c5, *outs):
        def f(q_, k_, v_, g_, b_):
            return _delta_prep(q_, k_, v_, g_, b_, hints=_heads(t_ref))[0]

        _, vjp = jax.vjp(f, *[_heads(r) for r in (q_ref, k_ref, v_ref, g_ref, b_ref)])
        grads = vjp(tuple(_heads(c) for c in (c0, c1, c2, c3, c4, c5)))
        for ref, vals in zip(outs, grads):
            for sl, val in zip(_head_slices(), vals):
                ref[:, sl] = val

    spec = _chunk_spec(nc)
    return pl.pallas_call(
        body,
        name=name,
        grid=(N_HEADS_A // HEADS_PER_STEP, nc),
        in_specs=[spec] * 12,
        out_specs=[spec] * 5,
        out_shape=[jax.ShapeDtypeStruct((t, d), F32)] * 5,
        compiler_params=_params(("parallel", "parallel")),
    )(q, k, v, gbc, bbc, tinv, *cts)


def _delta_scan_fwd(prep, *, name):
    t, d = prep[0].shape
    nc = t // CHUNK

    def body(u0, wk, qk, qd, kd, gl, o_ref, st_ref, s_ref):
        n = pl.program_id(1)

        @pl.when(n == 0)
        def _():
            s_ref[...] = jnp.zeros(s_ref.shape, F32)

        ss = [s_ref[j] for j in range(HEADS_PER_STEP)]
        os_, s_new = _delta_step(ss, *[_heads(r) for r in (u0, wk, qk, qd, kd, gl)])
        for j, sl in enumerate(_head_slices()):
            st_ref[:, sl] = ss[j]
            o_ref[:, sl] = os_[j]
            s_ref[j] = s_new[j]

    spec = _chunk_spec(nc)
    return pl.pallas_call(
        body,
        name=name,
        grid=(N_HEADS_A // HEADS_PER_STEP, nc),
        in_specs=[spec] * 6,
        out_specs=[spec] * 2,
        out_shape=[jax.ShapeDtypeStruct((t, d), F32)] * 2,
        scratch_shapes=[pltpu.VMEM((HEADS_PER_STEP, HEAD_DIM_A, HEAD_DIM_A), F32)],
        compiler_params=_params(("parallel", "arbitrary")),
    )(*prep)


def _delta_scan_bwd(prep, states, do, *, name):
    t, d = do.shape
    nc = t // CHUNK

    def body(u0, wk, qk, qd, kd, gl, st_ref, do_ref, *rest):
        outs, ds_ref = rest[:6], rest[6]
        n = pl.program_id(1)

        @pl.when(n == 0)
        def _():
            ds_ref[...] = jnp.zeros(ds_ref.shape, F32)

        _, vjp = jax.vjp(_delta_step, *[_heads(r) for r in (st_ref, u0, wk, qk, qd, kd, gl)])
        grads = vjp((_heads(do_ref), [ds_ref[j] for j in range(HEADS_PER_STEP)]))
        for j, sl in enumerate(_head_slices()):
            ds_ref[j] = grads[0][j]
            for ref, vals in zip(outs, grads[1:]):
                ref[:, sl] = vals[j]

    spec = _chunk_spec(nc, reverse=True)
    return pl.pallas_call(
        body,
        name=name,
        grid=(N_HEADS_A // HEADS_PER_STEP, nc),
        in_specs=[spec] * 8,
        out_specs=[spec] * 6,
        out_shape=[jax.ShapeDtypeStruct((t, d), F32)] * 6,
        scratch_shapes=[pltpu.VMEM((HEADS_PER_STEP, HEAD_DIM_A, HEAD_DIM_A), F32)],
        compiler_params=_params(("parallel", "arbitrary")),
    )(*prep, states, do)


def _delta_fwd(q, k, v, gbc, bbc, proj, norm_w, *, name):
    assert HEADS_PER_STEP == N_HEADS_A
    t, d = q.shape
    nc = t // CHUNK

    def body(q_ref, k_ref, v_ref, g_ref, b_ref, z_ref, w_ref, o_ref, st_ref, t_ref, og_ref, s_ref):
        n = pl.program_id(0)

        @pl.when(n == 0)
        def _():
            s_ref[...] = jnp.zeros(s_ref.shape, F32)

        ss = [s_ref[j] for j in range(N_HEADS_A)]
        res, tinv = _delta_prep(*[_heads(r) for r in (q_ref, k_ref, v_ref, g_ref, b_ref)])
        os_, s_new = _delta_step(ss, *res)
        for j, sl in enumerate(_head_slices()):
            st_ref[:, sl] = ss[j]
            t_ref[:, sl] = tinv[j]
            o_ref[:, sl] = os_[j]
            s_ref[j] = s_new[j]
            zh = z_ref[:, sl]
            r = lax.rsqrt(jnp.mean(os_[j] * os_[j], axis=-1, keepdims=True) + EPS)
            og_ref[:, sl] = (os_[j] * r * w_ref[...] * (zh * _sigmoid(zh))).astype(BF16)

    spec = pl.BlockSpec((CHUNK, d), lambda n: (n, 0))
    f32 = jax.ShapeDtypeStruct((t, d), F32)
    return pl.pallas_call(
        body,
        name=name,
        grid=(nc,),
        in_specs=[spec] * 5 + [pl.BlockSpec((CHUNK, d), lambda n: (n, Z_COL_BLOCK)), pl.BlockSpec((1, HEAD_DIM_A), lambda n: (0, 0))],
        out_specs=[spec] * 4,
        out_shape=[f32, f32, f32, jax.ShapeDtypeStruct((t, d), BF16)],
        scratch_shapes=[pltpu.VMEM((N_HEADS_A, HEAD_DIM_A, HEAD_DIM_A), F32)],
        compiler_params=_params(("arbitrary",)),
    )(q, k, v, gbc, bbc, proj, norm_w)


def _delta_bwd(q, k, v, gbc, bbc, tinv, states, o, proj, norm_w, dog, *, name):
    t, d = q.shape
    nc = t // CHUNK

    def body(q_ref, k_ref, v_ref, g_ref, b_ref, t_ref, st_ref, o_ref, z_ref, w_ref, dog_ref,
             dq_ref, dk_ref, dv_ref, dg_ref, db_ref, dz_ref, dw_ref, ds_ref):
        n = pl.program_id(0)

        @pl.when(n == 0)
        def _():
            ds_ref[...] = jnp.zeros(ds_ref.shape, F32)

        hsl = _head_slices()
        rstds = [_rstd(o_ref[:, sl]) for sl in hsl]
        nhs = [o_ref[:, sl] * r for sl, r in zip(hsl, rstds)]
        sgs = [_sigmoid(z_ref[:, sl]) for sl in hsl]
        dts = [dog_ref[:, sl] * (z_ref[:, sl] * sg) for sl, sg in zip(hsl, sgs)]
        dnhs = [dt * w_ref[...] for dt in dts]
        means = [jnp.mean(dnh * nh, axis=-1, keepdims=True) for dnh, nh in zip(dnhs, nhs)]
        dos = [r * (dnh - nh * mn) for r, dnh, nh, mn in zip(rstds, dnhs, nhs, means)]
        dw = jnp.zeros((1, HEAD_DIM_A), F32)
        for sl, nh, sg, dt in zip(hsl, nhs, sgs, dts):
            zh = z_ref[:, sl]
            dz_ref[:, sl] = (dog_ref[:, sl] * nh * w_ref[...] * (sg * (1.0 + zh * (1.0 - sg)))).astype(BF16)
            dw = dw + jnp.sum(dt * nh, axis=0, keepdims=True)

        @pl.when(n == 0)
        def _():
            dw_ref[...] = dw

        @pl.when(n > 0)
        def _():
            dw_ref[...] += dw

        def chunk(qs, ks, vs, gs, bs, ss):
            return _delta_step(ss, *_delta_prep(qs, ks, vs, gs, bs, hints=_heads(t_ref))[0])

        _, vjp = jax.vjp(chunk, *[_heads(r) for r in (q_ref, k_ref, v_ref, g_ref, b_ref, st_ref)])
        grads = vjp((dos, [ds_ref[j] for j in range(N_HEADS_A)]))
        for j, sl in enumerate(_head_slices()):
            ds_ref[j] = grads[5][j]
            for ref, vals in zip((dq_ref, dk_ref, dv_ref, dg_ref, db_ref), grads[:5]):
                ref[:, sl] = vals[j]

    spec = pl.BlockSpec((CHUNK, d), lambda n: (nc - 1 - n, 0))
    vec = pl.BlockSpec((1, HEAD_DIM_A), lambda n: (0, 0))
    f32 = jax.ShapeDtypeStruct((t, d), F32)
    return pl.pallas_call(
        body,
        name=name,
        grid=(nc,),
        in_specs=[spec] * 8 + [pl.BlockSpec((CHUNK, d), lambda n: (nc - 1 - n, Z_COL_BLOCK)), vec, spec],
        out_specs=[spec] * 6 + [vec],
        out_shape=[f32] * 5 + [jax.ShapeDtypeStruct((t, d), BF16), jax.ShapeDtypeStruct((1, HEAD_DIM_A), F32)],
        scratch_shapes=[pltpu.VMEM((N_HEADS_A, HEAD_DIM_A, HEAD_DIM_A), F32)],
        compiler_params=_params(("arbitrary",)),
    )(q, k, v, gbc, bbc, tinv, states, o, proj, norm_w, dog)


def _alibi_slope(h):
    return 2.0 ** (-8.0 * (h + 1) / N_HEADS_B)


def _swa_load(sink_ref, q_ref, kp_ref, kc_ref, vp_ref, vc_ref):
    rg = lax.broadcasted_iota(jnp.int32, (GROUP_B * WINDOW, 1), 0) // WINDOW
    q4s, kcats, vcats, slopes, sinkcols = [], [], [], [], []
    for hk in range(N_KV_B):
        ks = slice(hk * HEAD_DIM_B, (hk + 1) * HEAD_DIM_B)
        heads = [hk * GROUP_B + g for g in range(GROUP_B)]
        q4s.append(jnp.concatenate([q_ref[:, h * HEAD_DIM_B:(h + 1) * HEAD_DIM_B] for h in heads], axis=0).astype(BF16))
        kcats.append(jnp.concatenate([kp_ref[:, ks], kc_ref[:, ks]], axis=0).astype(BF16))
        vcats.append(jnp.concatenate([vp_ref[:, ks], vc_ref[:, ks]], axis=0).astype(BF16))
        slope = jnp.zeros((GROUP_B * WINDOW, 1), F32)
        sink = jnp.zeros((GROUP_B * WINDOW, 1), F32)
        for g, h in enumerate(heads):
            slope = jnp.where(rg == g, _alibi_slope(h), slope)
            sink = jnp.where(rg == g, sink_ref[0, h], sink)
        slopes.append(slope)
        sinkcols.append(sink)
    return q4s, kcats, vcats, slopes, sinkcols


def _swa_probs(q4s, kcats, slopes, sinkcols, blk):
    rows = GROUP_B * WINDOW
    qi = lax.broadcasted_iota(jnp.int32, (rows, 2 * WINDOW), 0) % WINDOW
    kj = lax.broadcasted_iota(jnp.int32, (rows, 2 * WINDOW), 1)
    dist = qi + WINDOW - kj
    valid = (dist >= 0) & (dist < WINDOW) & (blk * WINDOW - WINDOW + kj >= 0)
    distf = dist.astype(F32)
    ss = [lax.dot_general(q, kc, _NT, preferred_element_type=F32) for q, kc in zip(q4s, kcats)]
    logits = [jnp.where(valid, s * (HEAD_DIM_B ** -0.5) - sl * distf, NEG_BIG) for s, sl in zip(ss, slopes)]
    ms = [jnp.maximum(jnp.max(l, axis=-1, keepdims=True), sk) for l, sk in zip(logits, sinkcols)]
    es = [jnp.exp(l - m) for l, m in zip(logits, ms)]
    esk = [jnp.exp(sk - m) for sk, m in zip(sinkcols, ms)]
    invs = [1.0 / (jnp.sum(e, axis=-1, keepdims=True) + k) for e, k in zip(es, esk)]
    return [e * i for e, i in zip(es, invs)], [k * i for k, i in zip(esk, invs)]


def _swa_fwd(proj, sinks, *, name):
    t = proj.shape[0]
    nb = t // WINDOW
    qd = N_HEADS_B * HEAD_DIM_B
    kd = N_KV_B * HEAD_DIM_B

    def body(sink_ref, q_ref, kp_ref, kc_ref, vp_ref, vc_ref, o_ref):
        blk = pl.program_id(0)
        q4s, kcats, vcats, slopes, sinkcols = _swa_load(sink_ref, q_ref, kp_ref, kc_ref, vp_ref, vc_ref)
        ps, _ = _swa_probs(q4s, kcats, slopes, sinkcols, blk)
        outs = [jnp.dot(p.astype(BF16), vc, preferred_element_type=F32) for p, vc in zip(ps, vcats)]
        for hk, out in enumerate(outs):
            for g in range(GROUP_B):
                h = hk * GROUP_B + g
                o_ref[:, h * HEAD_DIM_B:(h + 1) * HEAD_DIM_B] = out[g * WINDOW:(g + 1) * WINDOW, :].astype(BF16)

    q_spec = pl.BlockSpec((WINDOW, qd), lambda i: (i, 0))
    kv = lambda col, prev: pl.BlockSpec((WINDOW, kd), (lambda i: (jnp.maximum(i - 1, 0), col)) if prev else (lambda i: (i, col)))
    kcol, vcol = qd // kd, qd // kd + 1
    return pl.pallas_call(
        body,
        name=name,
        grid=(nb,),
        in_specs=[pl.BlockSpec(memory_space=pltpu.SMEM), q_spec, kv(kcol, True), kv(kcol, False), kv(vcol, True), kv(vcol, False)],
        out_specs=q_spec,
        out_shape=jax.ShapeDtypeStruct((t, qd), BF16),
        compiler_params=_params(("parallel",)),
    )(sinks, proj, proj, proj, proj, proj)


def _swa_bwd(proj, sinks, dout, *, name):
    t = proj.shape[0]
    nb = t // WINDOW
    qd = N_HEADS_B * HEAD_DIM_B
    kd = N_KV_B * HEAD_DIM_B
    scale = HEAD_DIM_B ** -0.5

    def body(sink_ref, q_ref, kp_ref, kc_ref, vp_ref, vc_ref, do_ref, dq_ref, dk_ref, dv_ref, dsk_ref):
        blk = pl.program_id(0)
        lane = lax.broadcasted_iota(jnp.int32, (1, LANES), 1)

        @pl.when(blk == 0)
        def _():
            dk_ref[...] = jnp.zeros((t, kd), F32)
            dv_ref[...] = jnp.zeros((t, kd), F32)
            dsk_ref[...] = jnp.zeros((1, LANES), F32)

        cur = pl.ds(pl.multiple_of(blk * WINDOW, WINDOW), WINDOW)
        prv = pl.ds(pl.multiple_of(jnp.maximum(blk - 1, 0) * WINDOW, WINDOW), WINDOW)
        q4s, kcats, vcats, slopes, sinkcols = _swa_load(sink_ref, q_ref, kp_ref, kc_ref, vp_ref, vc_ref)
        ps, psinks = _swa_probs(q4s, kcats, slopes, sinkcols, blk)
        do4s = [jnp.concatenate([do_ref[:, (hk * GROUP_B + g) * HEAD_DIM_B:(hk * GROUP_B + g + 1) * HEAD_DIM_B]
                                 for g in range(GROUP_B)], axis=0).astype(BF16) for hk in range(N_KV_B)]
        dps = [lax.dot_general(d, vc, _NT, preferred_element_type=F32) for d, vc in zip(do4s, vcats)]
        deltas = [jnp.sum(p * dp, axis=-1, keepdims=True) for p, dp in zip(ps, dps)]
        dss = [(p * (dp - dl) * scale).astype(BF16) for p, dp, dl in zip(ps, dps, deltas)]
        dq4s = [jnp.dot(ds, kc, preferred_element_type=F32) for ds, kc in zip(dss, kcats)]
        dkcs = [lax.dot_general(ds, q, _TN, preferred_element_type=F32) for ds, q in zip(dss, q4s)]
        dvcs = [lax.dot_general(p.astype(BF16), d, _TN, preferred_element_type=F32) for p, d in zip(ps, do4s)]
        dsk = jnp.zeros((1, LANES), F32)
        for hk in range(N_KV_B):
            ks = slice(hk * HEAD_DIM_B, (hk + 1) * HEAD_DIM_B)
            dsink = -psinks[hk] * deltas[hk]
            for g in range(GROUP_B):
                h = hk * GROUP_B + g
                rows = slice(g * WINDOW, (g + 1) * WINDOW)
                dq_ref[:, h * HEAD_DIM_B:(h + 1) * HEAD_DIM_B] = dq4s[hk][rows, :].astype(BF16)
                dsk = dsk + jnp.where(lane == h, jnp.sum(dsink[rows, :], axis=0, keepdims=True), 0.0)
            dk_ref[cur, ks] += dkcs[hk][WINDOW:, :]
            dv_ref[cur, ks] += dvcs[hk][WINDOW:, :]

            @pl.when(blk > 0)
            def _():
                dk_ref[prv, ks] += dkcs[hk][:WINDOW, :]
                dv_ref[prv, ks] += dvcs[hk][:WINDOW, :]

        dsk_ref[...] += dsk

    q_spec = pl.BlockSpec((WINDOW, qd), lambda i: (i, 0))
    kv = lambda col, prev: pl.BlockSpec((WINDOW, kd), (lambda i: (jnp.maximum(i - 1, 0), col)) if prev else (lambda i: (i, col)))
    kcol, vcol = qd // kd, qd // kd + 1
    full = pl.BlockSpec((t, kd), lambda i: (0, 0))
    return pl.pallas_call(
        body,
        name=name,
        grid=(nb,),
        in_specs=[pl.BlockSpec(memory_space=pltpu.SMEM), q_spec, kv(kcol, True), kv(kcol, False), kv(vcol, True), kv(vcol, False), q_spec],
        out_specs=[q_spec, full, full, pl.BlockSpec((1, LANES), lambda i: (0, 0))],
        out_shape=[jax.ShapeDtypeStruct((t, qd), BF16), jax.ShapeDtypeStruct((t, kd), F32),
                   jax.ShapeDtypeStruct((t, kd), F32), jax.ShapeDtypeStruct((1, LANES), F32)],
        compiler_params=_params(("arbitrary",)),
    )(sinks, proj, proj, proj, proj, proj, dout)


def _ffn_act_fwd(up, cw, *, name, tm=512, cb=256):
    _, t, f = up.shape
    tm, cb = _tile(t, tm), _tile(f, cb)

    rc = min(ROW_CHUNK, tm)

    def body(ug_ref, uv_ref, cg_ref, cv_ref, a_ref, hg, hv):
        i = pl.program_id(1)

        @pl.when(i == 0)
        def _():
            hg[0:HALO, :] = jnp.zeros((HALO, cb), F32)
            hv[0:HALO, :] = jnp.zeros((HALO, cb), F32)

        hg[pl.ds(HALO, rc), :] = ug_ref[0:rc, :]
        hv[pl.ds(HALO, rc), :] = uv_ref[0:rc, :]
        for r in range(tm // rc):
            if r == 0:
                yg = _conv_rows(hg, HALO, rc, cg_ref, FFN_CONV)
                yv = _conv_rows(hv, HALO, rc, cv_ref, FFN_CONV)
            else:
                yg = _conv_rows(ug_ref, r * rc, rc, cg_ref, FFN_CONV)
                yv = _conv_rows(uv_ref, r * rc, rc, cv_ref, FFN_CONV)
            a_ref[r * rc:(r + 1) * rc, :] = (yg * _sigmoid(yg) * yv).astype(BF16)
        hg[0:HALO, :] = ug_ref[tm - HALO:tm, :]
        hv[0:HALO, :] = uv_ref[tm - HALO:tm, :]

    ncb = f // cb
    half = lambda s: pl.BlockSpec((None, tm, cb), lambda c, i: (s, i, c))
    taps = lambda s: pl.BlockSpec((FFN_CONV, cb), lambda c, i: (0, c + s * ncb))
    return pl.pallas_call(
        body,
        name=name,
        grid=(ncb, t // tm),
        in_specs=[half(0), half(1), taps(0), taps(1)],
        out_specs=pl.BlockSpec((tm, cb), lambda c, i: (i, c)),
        out_shape=jax.ShapeDtypeStruct((t, f), BF16),
        scratch_shapes=[pltpu.VMEM((HALO + rc, cb), F32)] * 2,
        compiler_params=_params(("parallel", "arbitrary")),
    )(up, up, cw, cw)


def _ffn_act_bwd(up, cw, dact, *, name, tm=512, cb=256):
    _, t, f = up.shape
    tm, cb = _tile(t, tm), _tile(f, cb)
    nt = t // tm
    hb = tm // HALO

    rc = min(ROW_CHUNK, tm)
    nr = tm // rc
    kw = FFN_CONV

    def body(ug_ref, uv_ref, pg_ref, pv_ref, cg_ref, cv_ref, da_ref, du_ref, dcg_ref, dcv_ref,
             hg, hv, dyg, dyv):
        i = pl.program_id(1)
        first = i == 0
        tile = nt - 1 - i

        @pl.when(tile == 0)
        def _():
            hg[0:HALO, :] = jnp.zeros((HALO, cb), F32)
            hv[0:HALO, :] = jnp.zeros((HALO, cb), F32)

        @pl.when(tile > 0)
        def _():
            hg[0:HALO, :] = pg_ref[...]
            hv[0:HALO, :] = pv_ref[...]

        @pl.when(first)
        def _():
            dyg[pl.ds(tm, HALO), :] = jnp.zeros((HALO, cb), F32)
            dyv[pl.ds(tm, HALO), :] = jnp.zeros((HALO, cb), F32)

        hg[pl.ds(HALO, rc), :] = ug_ref[0:rc, :]
        hv[pl.ds(HALO, rc), :] = uv_ref[0:rc, :]
        dcg = [jnp.zeros((1, cb), F32) for _ in range(kw)]
        dcv = [jnp.zeros((1, cb), F32) for _ in range(kw)]
        for r in reversed(range(nr)):
            rows = slice(r * rc, (r + 1) * rc)
            src_g, src_v, base = (hg, hv, HALO) if r == 0 else (ug_ref, uv_ref, r * rc)
            yg = _conv_rows(src_g, base, rc, cg_ref, kw)
            yv = _conv_rows(src_v, base, rc, cv_ref, kw)
            sg = _sigmoid(yg)
            da = da_ref[rows, :]
            dy_g = da * yv * (sg * (1.0 + yg * (1.0 - sg)))
            dy_v = da * (yg * sg)
            dyg[rows, :] = dy_g
            dyv[rows, :] = dy_v
            du_ref[0, rows, :] = _conv_t_rows(dyg, r * rc, rc, cg_ref, kw).astype(BF16)
            du_ref[1, rows, :] = _conv_t_rows(dyv, r * rc, rc, cv_ref, kw).astype(BF16)
            for j in range(kw):
                dcg[j] = dcg[j] + jnp.sum(dy_g * src_g[pl.ds(base - (kw - 1) + j, rc), :], axis=0, keepdims=True)
                dcv[j] = dcv[j] + jnp.sum(dy_v * src_v[pl.ds(base - (kw - 1) + j, rc), :], axis=0, keepdims=True)
        dyg[pl.ds(tm, HALO), :] = dyg[0:HALO, :]
        dyv[pl.ds(tm, HALO), :] = dyv[0:HALO, :]
        for j in range(kw):
            @pl.when(first)
            def _():
                dcg_ref[j:j + 1, :] = dcg[j]
                dcv_ref[j:j + 1, :] = dcv[j]

            @pl.when(jnp.logical_not(first))
            def _():
                dcg_ref[j:j + 1, :] += dcg[j]
                dcv_ref[j:j + 1, :] += dcv[j]

    ncb = f // cb
    half = lambda s: pl.BlockSpec((None, tm, cb), lambda c, i: (s, nt - 1 - i, c))
    prev = lambda s: pl.BlockSpec((None, HALO, cb), lambda c, i: (s, jnp.maximum((nt - 1 - i) * hb - 1, 0), c))
    taps = lambda s: pl.BlockSpec((FFN_CONV, cb), lambda c, i: (0, c + s * ncb))
    dtaps = pl.BlockSpec((FFN_CONV, cb), lambda c, i: (0, c))
    return pl.pallas_call(
        body,
        name=name,
        grid=(ncb, nt),
        in_specs=[half(0), half(1), prev(0), prev(1), taps(0), taps(1), pl.BlockSpec((tm, cb), lambda c, i: (nt - 1 - i, c))],
        out_specs=[pl.BlockSpec((2, tm, cb), lambda c, i: (0, nt - 1 - i, c)), dtaps, dtaps],
        out_shape=[jax.ShapeDtypeStruct((2, t, f), BF16), jax.ShapeDtypeStruct((FFN_CONV, f), F32),
                   jax.ShapeDtypeStruct((FFN_CONV, f), F32)],
        scratch_shapes=[pltpu.VMEM((HALO + rc, cb), F32)] * 2 + [pltpu.VMEM((tm + HALO, cb), F32)] * 2,
        compiler_params=_params(("parallel", "arbitrary")),
    )(up, up, up, up, cw, cw, dact)


FFN_COL_TILE = 1408
FFN_SUB = 512
FFN_ROW_CHUNK = 16


def _sub_blocks(width):
    return [slice(c, min(c + FFN_SUB, width)) for c in range(0, width, FFN_SUB)]


def _ffn_up_act(n_f, w_up_t, cw, *, name, tm=512):
    t, d = n_f.shape
    f = w_up_t.shape[1]
    tm, tn = _tile(t, tm), _tile(f, FFN_COL_TILE)
    nj = f // tn
    rc = min(FFN_ROW_CHUNK, tm)
    kw = FFN_CONV

    def body(n_ref, wg_ref, wv_ref, cg_ref, cv_ref, up_ref, y_ref, a_ref, hg, hv):
        i = pl.program_id(1)

        @pl.when(i == 0)
        def _():
            hg[0:HALO, :] = jnp.zeros((HALO, tn), F32)
            hv[0:HALO, :] = jnp.zeros((HALO, tn), F32)

        def products(cs):
            up_ref[0, :, cs] = lax.dot_general(n_ref[...], wg_ref[cs, :], _NT, preferred_element_type=F32)
            up_ref[1, :, cs] = lax.dot_general(n_ref[...], wv_ref[cs, :], _NT, preferred_element_type=F32)

        subs = _sub_blocks(tn)
        ug, uv = up_ref.at[0], up_ref.at[1]
        products(subs[0])
        for ci, cs in enumerate(subs):
            if ci + 1 < len(subs):
                products(subs[ci + 1])
            hg[pl.ds(HALO, rc), cs] = ug[0:rc, cs]
            hv[pl.ds(HALO, rc), cs] = uv[0:rc, cs]
            for r in range(tm // rc):
                src_g, src_v, base = (hg, hv, HALO) if r == 0 else (ug, uv, r * rc)
                yg = _conv_rows(src_g, base, rc, cg_ref, kw, cs)
                yv = _conv_rows(src_v, base, rc, cv_ref, kw, cs)
                y_ref[0, r * rc:(r + 1) * rc, cs] = yg
                y_ref[1, r * rc:(r + 1) * rc, cs] = yv
                a_ref[r * rc:(r + 1) * rc, cs] = (yg * _sigmoid(yg) * yv).astype(BF16)
            hg[0:HALO, cs] = ug[tm - HALO:tm, cs]
            hv[0:HALO, cs] = uv[tm - HALO:tm, cs]

    half = lambda s: pl.BlockSpec((None, tn, d), lambda j, i: (s, j, 0))
    taps = lambda s: pl.BlockSpec((kw, tn), lambda j, i: (0, j + s * nj))
    pair = pl.BlockSpec((2, tm, tn), lambda j, i: (0, i, j))
    return pl.pallas_call(
        body,
        name=name,
        grid=(nj, t // tm),
        in_specs=[pl.BlockSpec((tm, d), lambda j, i: (i, 0)), half(0), half(1), taps(0), taps(1)],
        out_specs=[pair, pair, pl.BlockSpec((tm, tn), lambda j, i: (i, j))],
        out_shape=[jax.ShapeDtypeStruct((2, t, f), F32), jax.ShapeDtypeStruct((2, t, f), F32),
                   jax.ShapeDtypeStruct((t, f), BF16)],
        scratch_shapes=[pltpu.VMEM((HALO + rc, tn), F32)] * 2,
        compiler_params=_params(("parallel", "arbitrary")),
    )(n_f, w_up_t, w_up_t, cw, cw)


def _ffn_down_dx_act_bwd(dh, w_down, up, y, cw, *, name, tm=512):
    t, d = dh.shape
    f = w_down.shape[0]
    tm, tn = _tile(t, tm), _tile(f, FFN_COL_TILE)
    nj, nt = f // tn, t // tm
    rc = min(FFN_ROW_CHUNK, tm)
    nr = tm // rc
    kw = FFN_CONV

    def body(dh_ref, wd_ref, ug_ref, uv_ref, yg_ref, yv_ref, cg_ref, cv_ref, du_ref, dcg_ref, dcv_ref,
             dyg, dyv, da_s, dh_s):
        i = pl.program_id(1)
        first = i == 0

        @pl.when(first)
        def _():
            dyg[pl.ds(tm, HALO), :] = jnp.zeros((HALO, tn), F32)
            dyv[pl.ds(tm, HALO), :] = jnp.zeros((HALO, tn), F32)

        dh_s[...] = dh_ref[...].astype(BF16)

        def product(cs):
            da_s[:, cs] = lax.dot_general(dh_s[...], wd_ref[cs, :], _NT, preferred_element_type=F32)

        subs = _sub_blocks(tn)
        product(subs[0])
        for ci, cs in enumerate(subs):
            width = cs.stop - cs.start
            if ci + 1 < len(subs):
                product(subs[ci + 1])
            dcg = [jnp.zeros((HALO, width), F32) for _ in range(kw)]
            dcv = [jnp.zeros((HALO, width), F32) for _ in range(kw)]
            for r in reversed(range(nr)):
                rows = slice(r * rc, (r + 1) * rc)
                yg, yv = yg_ref[rows, cs], yv_ref[rows, cs]
                sg = _sigmoid(yg)
                da = da_s[rows, cs]
                dyg[rows, cs] = da * yv * (sg * (1.0 + yg * (1.0 - sg)))
                dyv[rows, cs] = da * (yg * sg)
                ahead_g = _ahead_rows(dyg, r * rc, rc, range(kw), cs)
                ahead_v = _ahead_rows(dyv, r * rc, rc, range(kw), cs)
                du_ref[0, rows, cs] = sum(cg_ref[kw - 1 - s:kw - s, cs] * ahead_g[s] for s in range(kw)).astype(BF16)
                du_ref[1, rows, cs] = sum(cv_ref[kw - 1 - s:kw - s, cs] * ahead_v[s] for s in range(kw)).astype(BF16)
                xg, xv = ug_ref[rows, cs], uv_ref[rows, cs]
                for s in range(kw):
                    dcg[kw - 1 - s] = dcg[kw - 1 - s] + _fold_rows(xg * ahead_g[s])
                    dcv[kw - 1 - s] = dcv[kw - 1 - s] + _fold_rows(xv * ahead_v[s])
            dyg[pl.ds(tm, HALO), cs] = dyg[0:HALO, cs]
            dyv[pl.ds(tm, HALO), cs] = dyv[0:HALO, cs]
            for j in range(kw):
                tg = jnp.sum(dcg[j], axis=0, keepdims=True)
                tv = jnp.sum(dcv[j], axis=0, keepdims=True)

                @pl.when(first)
                def _():
                    dcg_ref[j:j + 1, cs] = tg
                    dcv_ref[j:j + 1, cs] = tv

                @pl.when(jnp.logical_not(first))
                def _():
                    dcg_ref[j:j + 1, cs] += tg
                    dcv_ref[j:j + 1, cs] += tv

    half = lambda s: pl.BlockSpec((None, tm, tn), lambda j, i: (s, nt - 1 - i, j))
    taps = lambda s: pl.BlockSpec((kw, tn), lambda j, i: (0, j + s * nj))
    dtaps = pl.BlockSpec((kw, tn), lambda j, i: (0, j))
    return pl.pallas_call(
        body,
        name=name,
        grid=(nj, nt),
        in_specs=[pl.BlockSpec((tm, d), lambda j, i: (nt - 1 - i, 0)), pl.BlockSpec((tn, d), lambda j, i: (j, 0)),
                  half(0), half(1), half(0), half(1), taps(0), taps(1)],
        out_specs=[pl.BlockSpec((2, tm, tn), lambda j, i: (0, nt - 1 - i, j)), dtaps, dtaps],
        out_shape=[jax.ShapeDtypeStruct((2, t, f), BF16), jax.ShapeDtypeStruct((kw, f), F32),
                   jax.ShapeDtypeStruct((kw, f), F32)],
        scratch_shapes=[pltpu.VMEM((tm + HALO, tn), F32)] * 2 + [pltpu.VMEM((tm, tn), F32), pltpu.VMEM((tm, d), BF16)],
        compiler_params=_params(("parallel", "arbitrary")),
    )(dh, w_down, up, up, y, y, cw, cw)


def _ple_fwd(h, zg, pe, *, name, tm=512):
    t, d = h.shape
    tm = _tile(t, tm)

    def body(h_ref, z_ref, p_ref, o_ref):
        o_ref[...] = h_ref[...] + _sigmoid(z_ref[...]) * p_ref[...]

    row = pl.BlockSpec((tm, d), lambda i: (i, 0))
    return pl.pallas_call(
        body, name=name, grid=(t // tm,), in_specs=[row] * 3, out_specs=row,
        out_shape=jax.ShapeDtypeStruct((t, d), F32), compiler_params=_params(("parallel",)),
    )(h, zg, pe)


def _ple_bwd(dh, zg, pe, *, name, tm=512):
    t, d = dh.shape
    tm = _tile(t, tm)

    def body(g_ref, z_ref, p_ref, dz_ref, dp_ref):
        g = g_ref[...]
        sg = _sigmoid(z_ref[...])
        dz_ref[...] = (g * p_ref[...] * sg * (1.0 - sg)).astype(BF16)
        dp_ref[...] = (g * sg).astype(BF16)

    row = pl.BlockSpec((tm, d), lambda i: (i, 0))
    return pl.pallas_call(
        body, name=name, grid=(t // tm,), in_specs=[row] * 3, out_specs=[row] * 2,
        out_shape=[jax.ShapeDtypeStruct((t, d), BF16)] * 2, compiler_params=_params(("parallel",)),
    )(dh, zg, pe)


def _my_pos():
    return lax.axis_index("x"), lax.axis_index("y"), lax.axis_index("c")


def _all_gather(block, *, name, dep=None):
    r, w = block.shape
    has_dep = dep is not None

    def body(*refs):
        x_ref, out_ref, send_sems, recv_sems, local_sem = refs[:1] + refs[1 + has_dep:]
        x, y, c = _my_pos()
        me, sibling = (x, y, c), (x, y, 1 - c)
        chips = [(1 - x, y), (x, 1 - y), (1 - x, 1 - y)]

        def slot(px, py, pc):
            return out_ref.at[4 * px + 2 * py + pc]

        def copy(k, blk, to, src=None):
            return pltpu.make_async_remote_copy(
                src_ref=slot(*blk) if src is None else src, dst_ref=slot(*blk),
                send_sem=send_sems.at[k], recv_sem=recv_sems.at[k],
                device_id=to, device_id_type=pl.DeviceIdType.MESH)

        mine = pltpu.make_async_copy(x_ref, slot(*me), local_sem)
        mine.start()
        first = [copy(0, me, sibling, src=x_ref)]
        first += [copy(1 + j, me, (*chip, c), src=x_ref) for j, chip in enumerate(chips)]
        for cp in first:
            cp.start()
        passed = [copy(4 + j, (*chip, c), sibling) for j, chip in enumerate(chips)]
        for j, chip in enumerate(chips):
            copy(1 + j, (*chip, c), me).wait_recv()
            passed[j].start()
        copy(0, sibling, me).wait_recv()
        for j, chip in enumerate(chips):
            copy(4 + j, (*chip, 1 - c), me).wait_recv()
        for cp in first + passed:
            cp.wait_send()
        mine.wait()

    return pl.pallas_call(
        body,
        name=name,
        out_shape=jax.ShapeDtypeStruct((N_DEV, r, w), block.dtype),
        in_specs=[pl.BlockSpec(memory_space=pl.ANY)] * (1 + has_dep),
        out_specs=pl.BlockSpec(memory_space=pl.ANY),
        scratch_shapes=[pltpu.SemaphoreType.DMA((7,)), pltpu.SemaphoreType.DMA((7,)), pltpu.SemaphoreType.DMA],
    )(*((block, dep) if has_dep else (block,)))


def _all_to_all(slabs, *, name):
    n, r, w = slabs.shape

    def body(x_ref, out_ref, send_sems, recv_sems, local_sem):
        x, y, c = _my_pos()
        my_idx = 4 * x + 2 * y + c
        mine = pltpu.make_async_copy(x_ref.at[my_idx], out_ref.at[my_idx], local_sem)
        mine.start()
        copies = []
        for k in range(1, N_DEV):
            fx, fy, fc = (k >> 2) & 1, (k >> 1) & 1, k & 1
            px = (1 - x) if fx else x
            py = (1 - y) if fy else y
            pc = (1 - c) if fc else c
            cp = pltpu.make_async_remote_copy(
                src_ref=x_ref.at[4 * px + 2 * py + pc], dst_ref=out_ref.at[my_idx],
                send_sem=send_sems.at[k - 1], recv_sem=recv_sems.at[k - 1],
                device_id=(px, py, pc), device_id_type=pl.DeviceIdType.MESH)
            cp.start()
            copies.append(cp)
        for cp in copies:
            cp.wait_recv()
        for cp in copies:
            cp.wait_send()
        mine.wait()

    return pl.pallas_call(
        body,
        name=name,
        out_shape=jax.ShapeDtypeStruct((n, r, w), slabs.dtype),
        in_specs=[pl.BlockSpec(memory_space=pl.ANY)],
        out_specs=pl.BlockSpec(memory_space=pl.ANY),
        scratch_shapes=[pltpu.SemaphoreType.DMA((7,)), pltpu.SemaphoreType.DMA((7,)), pltpu.SemaphoreType.DMA],
    )(slabs)


def _exchange_copies(scatter, src_refs, land_refs, send_sems, recv_sems, local_sems):
    x, y, c = _my_pos()
    me = 4 * x + 2 * y + c
    local, remote = [], []
    for i, (s, l) in enumerate(zip(src_refs, land_refs)):
        local.append(pltpu.make_async_copy(s.at[me] if scatter else s, l.at[me], local_sems.at[i]))
        for k in range(1, N_DEV):
            px = (1 - x) if (k >> 2) & 1 else x
            py = (1 - y) if (k >> 1) & 1 else y
            pc = (1 - c) if k & 1 else c
            remote.append(pltpu.make_async_remote_copy(
                src_ref=s.at[4 * px + 2 * py + pc] if scatter else s, dst_ref=l.at[me],
                send_sem=send_sems.at[(N_DEV - 1) * i + k - 1], recv_sem=recv_sems.at[(N_DEV - 1) * i + k - 1],
                device_id=(px, py, pc), device_id_type=pl.DeviceIdType.MESH))
    return local, remote


def _exchange(arrays, *, scatter, name):
    n = len(arrays)

    def body(*refs):
        srcs, lands = refs[:n], refs[n:2 * n]
        local, remote = _exchange_copies(scatter, srcs, lands, *refs[2 * n:])
        for cp in local + remote:
            cp.start()
        for cp in remote:
            cp.wait_recv()
        for cp in remote:
            cp.wait_send()
        for cp in local:
            cp.wait()

    hbm = pl.BlockSpec(memory_space=pl.ANY)
    out = pl.pallas_call(
        body,
        name=name,
        out_shape=[jax.ShapeDtypeStruct(a.shape if scatter else (N_DEV,) + a.shape, a.dtype) for a in arrays],
        in_specs=[hbm] * n,
        out_specs=[hbm] * n,
        scratch_shapes=[pltpu.SemaphoreType.DMA(((N_DEV - 1) * n,)), pltpu.SemaphoreType.DMA(((N_DEV - 1) * n,)),
                        pltpu.SemaphoreType.DMA((n,))],
    )(*arrays)
    return list(out)


_HBM_SPEC = pl.BlockSpec(memory_space=pltpu.HBM)
_SEM_SPEC = pl.BlockSpec(memory_space=pltpu.SEMAPHORE)
_EFFECT = pltpu.SideEffectType.DATAFLOW_SIDE_EFFECTING


def _exchange_start(arrays, *, scatter, name, dep):
    n = len(arrays)
    srcs = [pltpu.with_memory_space_constraint(a, pltpu.HBM) for a in arrays]
    lands = [pltpu.with_memory_space_constraint(lax.empty(a.shape if scatter else (N_DEV,) + a.shape, a.dtype), pltpu.HBM)
             for a in arrays]

    def body(*refs):
        src_refs, land_refs = refs[:n], refs[n:2 * n]
        send_sems, recv_sems, local_sems = refs[2 * n + 1:2 * n + 4]
        token = refs[-1]
        local, remote = _exchange_copies(scatter, src_refs, land_refs, send_sems, recv_sems, local_sems)
        for cp in local + remote:
            cp.start()
        token[...] = jnp.zeros_like(token)

    sems = (pltpu.SemaphoreType.DMA(((N_DEV - 1) * n,)), pltpu.SemaphoreType.DMA(((N_DEV - 1) * n,)),
            pltpu.SemaphoreType.DMA((n,)))
    out = pl.pallas_call(
        body,
        name=name,
        out_shape=sems + tuple(pltpu.HBM(a.shape, a.dtype) for a in srcs + lands) + (jax.ShapeDtypeStruct((8, LANES), F32),),
        in_specs=[_HBM_SPEC] * (2 * n) + [pl.BlockSpec(memory_space=pl.ANY)],
        out_specs=(_SEM_SPEC,) * 3 + (_HBM_SPEC,) * (2 * n) + (pl.BlockSpec(memory_space=pltpu.VMEM),),
        input_output_aliases={i: 3 + i for i in range(2 * n)},
        compiler_params=pltpu.CompilerParams(has_side_effects=_EFFECT),
    )(*srcs, *lands, dep)
    return (out[:3], list(out[3:3 + n]), list(out[3 + n:3 + 2 * n])), out[-1]


def _exchange_wait(handle, after, *, scatter, name):
    sems, srcs, lands = handle
    n = len(srcs)

    def body(*refs):
        src_refs, land_refs = refs[:n], refs[n:2 * n]
        send_sems, recv_sems, local_sems = refs[2 * n:2 * n + 3]
        local, remote = _exchange_copies(scatter, src_refs, land_refs, send_sems, recv_sems, local_sems)
        for cp in remote:
            cp.wait_send()
            cp.wait_recv()
        for cp in local:
            cp.wait()

    out = pl.pallas_call(
        body,
        name=name,
        out_shape=tuple(pltpu.HBM(a.shape, a.dtype) for a in srcs + lands),
        in_specs=[_HBM_SPEC] * (2 * n) + [_SEM_SPEC] * 3 + [pl.BlockSpec(memory_space=pl.ANY)],
        out_specs=(_HBM_SPEC,) * (2 * n),
        input_output_aliases={i: i for i in range(2 * n)},
        compiler_params=pltpu.CompilerParams(has_side_effects=_EFFECT),
    )(*srcs, *lands, *sems, after)
    return list(out[n:])


def _sum_parts(parts, *, name, tr=512):
    n, r, lanes = parts.shape
    tr = tr if (r % tr == 0 and r > 1024) else r

    def body(p_ref, g_ref):
        g = p_ref[0].astype(F32)
        for j in range(1, n):
            g = g + p_ref[j].astype(F32)
        g_ref[...] = g

    row = pl.BlockSpec((tr, lanes), lambda i: (i, 0))
    return pl.pallas_call(
        body,
        name=name,
        grid=(r // tr,),
        in_specs=[pl.BlockSpec((n, tr, lanes), lambda i: (0, i, 0))],
        out_specs=row,
        out_shape=jax.ShapeDtypeStruct((r, lanes), F32),
        compiler_params=_params(("parallel",)),
    )(parts)


def _adamw_update(g, w, m, v):
    c1 = 1.0 / (1.0 - ADAM_B1 ** ADAM_STEP)
    c2 = 1.0 / (1.0 - ADAM_B2 ** ADAM_STEP)
    nm = ADAM_B1 * m + (1.0 - ADAM_B1) * g
    nv = ADAM_B2 * v + (1.0 - ADAM_B2) * (g * g)
    return -ADAM_LR * ((nm * c1) / (jnp.sqrt(nv * c2) + ADAM_EPS) + ADAM_WD * w), nm, nv


def _adamw_layer(g, w, m, v, layer, prev, *, name):
    nl, k, n = w.shape
    tr = max([d for d in range(8, min(k, 256) + 1, 8) if k % d == 0] or [k])
    in_parts = g.ndim == 3

    def body(g_ref, w_ref, m_ref, v_ref, *rest):
        go_ref, d_ref, nm_ref, nv_ref = rest[-4:]
        if in_parts:
            gg = g_ref[0].astype(F32)
            for j in range(1, g_ref.shape[0]):
                gg = gg + g_ref[j].astype(F32)
        else:
            gg = g_ref[...]
        d, nm, nv = _adamw_update(gg, w_ref[...], m_ref[...], v_ref[...])
        go_ref[...] = gg
        d_ref[...] = d
        nm_ref[...] = nm
        nv_ref[...] = nv

    lay = pl.BlockSpec((None, tr, n), lambda i: (layer, i, 0))
    n_prev = 0 if prev is None else 4
    out = jax.ShapeDtypeStruct((nl, k, n), F32)
    return pl.pallas_call(
        body,
        name=name,
        grid=(k // tr,),
        in_specs=[pl.BlockSpec((g.shape[0], tr, n), lambda i: (0, i, 0)) if in_parts else pl.BlockSpec((tr, n), lambda i: (i, 0)),
                  lay, lay, lay] + [pl.BlockSpec(memory_space=pl.ANY)] * n_prev,
        out_specs=[lay] * 4,
        out_shape=[out] * 4,
        input_output_aliases={4 + j: j for j in range(n_prev)},
        compiler_params=_params(("parallel",)),
    )(g, w, m, v, *(prev or ()))


def _adamw_packed(g, w, m, v, *, name, tr=512):
    r, lanes = g.shape
    tr = tr if r % tr == 0 else r
    c1 = 1.0 / (1.0 - ADAM_B1 ** ADAM_STEP)
    c2 = 1.0 / (1.0 - ADAM_B2 ** ADAM_STEP)

    def body(g_ref, w_ref, m_ref, v_ref, d_ref, nm_ref, nv_ref):
        g = g_ref[...]
        nm = ADAM_B1 * m_ref[...] + (1.0 - ADAM_B1) * g
        nv = ADAM_B2 * v_ref[...] + (1.0 - ADAM_B2) * (g * g)
        nm_ref[...] = nm
        nv_ref[...] = nv
        d_ref[...] = -ADAM_LR * ((nm * c1) / (jnp.sqrt(nv * c2) + ADAM_EPS) + ADAM_WD * w_ref[...])

    row = pl.BlockSpec((tr, lanes), lambda i: (i, 0))
    out = jax.ShapeDtypeStruct((r, lanes), F32)
    return pl.pallas_call(
        body,
        name=name,
        grid=(r // tr,),
        in_specs=[row] * 4,
        out_specs=[row] * 3,
        out_shape=[out] * 3,
        compiler_params=_params(("parallel",)),
    )(g, w, m, v)


BIG = ("a_w_in", "a_w_out", "b_w_in", "b_w_out", "f_w_up", "f_w_down", "ple_w_proj", "ple_w_gate")
CONVS = ("a_conv", "f_conv")
SMALL = ("norm_mix", "norm_ffn", "norm_ple", "norm_final", "a_log", "a_dt_bias", "a_norm", "b_sinks")
WEIGHTS = ("norm_mix", "norm_ffn", "norm_ple", "norm_final", "a_w_in", "a_conv", "a_log", "a_dt_bias", "a_norm",
           "a_w_out", "b_w_in", "b_sinks", "b_w_out", "f_w_up", "f_conv", "f_w_down", "ple_w_proj", "ple_w_gate")
SLAB_ROW_MULTIPLE = 512


def _pack(arrs, dtype, row_multiple):
    flat = jnp.concatenate([a.reshape(-1).astype(dtype) for a in arrs])
    rows = -(-flat.shape[0] // LANES)
    rows = -(-rows // row_multiple) * row_multiple
    return jnp.pad(flat, (0, rows * LANES - flat.shape[0])).reshape(rows, LANES)


def _unpack(slab, shapes):
    lead = slab.shape[:-2]
    flat = slab.reshape(lead + (-1,))
    out, off = [], 0
    for s in shapes:
        size = math.prod(s)
        out.append(flat[..., off:off + size].reshape(lead + tuple(s)))
        off += size
    return out


def _cols_full(g):
    g = jnp.moveaxis(g, 0, -2)
    return g.reshape(g.shape[:-2] + (g.shape[-2] * g.shape[-1],))


def _rows_full(g):
    g = jnp.moveaxis(g, 0, -3)
    return g.reshape(g.shape[:-3] + (g.shape[-3] * g.shape[-2], g.shape[-1]))


def _cols_split(wfull):
    n = wfull.shape[-1] // N_DEV
    g = wfull.reshape(wfull.shape[:-1] + (N_DEV, n))
    return jnp.moveaxis(g, -2, 0)


def _rows_split(wfull):
    k = wfull.shape[-2] // N_DEV
    g = wfull.reshape(wfull.shape[:-2] + (N_DEV, k, wfull.shape[-1]))
    return jnp.moveaxis(g, -3, 0)


TRANSPOSED = ("a_w_in", "b_w_in", "f_w_up", "ple_w_proj")


def _wire(name, a):
    return jnp.swapaxes(a, -1, -2) if name in TRANSPOSED else a


def _wire_shape(name, shape):
    return shape[:-2] + (shape[-1], shape[-2]) if name in TRANSPOSED else tuple(shape)


def _full(name, g):
    return _cols_full(g) if name in CONVS else _rows_full(g)


def _split(name, wfull):
    return _cols_split(wfull) if name in CONVS else _rows_split(wfull)


def _pack_split(grads, names, dtype, row_multiple):
    flat = jnp.concatenate([_split(n, grads[n]).reshape(N_DEV, -1).astype(dtype) for n in names], axis=1)
    rows = -(-flat.shape[1] // LANES)
    rows = -(-rows // row_multiple) * row_multiple
    return jnp.pad(flat, ((0, 0), (0, rows * LANES - flat.shape[1]))).reshape(N_DEV, rows, LANES)


def _pad_cols(a, width):
    return jnp.pad(a, ((0, 0), (0, width - a.shape[1])))


def kernel(x, p, norm_mix, norm_ffn, norm_ple, norm_final, a_w_in, a_conv, a_log, a_dt_bias, a_norm, a_w_out, b_w_in, b_sinks, b_w_out, f_w_up, f_conv, f_w_down, ple_w_proj, ple_w_gate, loss_target, m_norm_mix, m_norm_ffn, m_norm_ple, m_norm_final, m_a_w_in, m_a_conv, m_a_log, m_a_dt_bias, m_a_norm, m_a_w_out, m_b_w_in, m_b_sinks, m_b_w_out, m_f_w_up, m_f_conv, m_f_w_down, m_ple_w_proj, m_ple_w_gate, v_norm_mix, v_norm_ffn, v_norm_ple, v_norm_final, v_a_w_in, v_a_conv, v_a_log, v_a_dt_bias, v_a_norm, v_a_w_out, v_b_w_in, v_b_sinks, v_b_w_out, v_f_w_up, v_f_conv, v_f_w_down, v_ple_w_proj, v_ple_w_gate):
    wts = dict(norm_mix=norm_mix, norm_ffn=norm_ffn, norm_ple=norm_ple, norm_final=norm_final, a_w_in=a_w_in,
               a_conv=a_conv, a_log=a_log, a_dt_bias=a_dt_bias, a_norm=a_norm, a_w_out=a_w_out, b_w_in=b_w_in,
               b_sinks=b_sinks, b_w_out=b_w_out, f_w_up=f_w_up, f_conv=f_conv, f_w_down=f_w_down,
               ple_w_proj=ple_w_proj, ple_w_gate=ple_w_gate)
    mom = dict(norm_mix=m_norm_mix, norm_ffn=m_norm_ffn, norm_ple=m_norm_ple, norm_final=m_norm_final,
               a_w_in=m_a_w_in, a_conv=m_a_conv, a_log=m_a_log, a_dt_bias=m_a_dt_bias, a_norm=m_a_norm,
               a_w_out=m_a_w_out, b_w_in=m_b_w_in, b_sinks=m_b_sinks, b_w_out=m_b_w_out, f_w_up=m_f_w_up,
               f_conv=m_f_conv, f_w_down=m_f_w_down, ple_w_proj=m_ple_w_proj, ple_w_gate=m_ple_w_gate)
    var = dict(norm_mix=v_norm_mix, norm_ffn=v_norm_ffn, norm_ple=v_norm_ple, norm_final=v_norm_final,
               a_w_in=v_a_w_in, a_conv=v_a_conv, a_log=v_a_log, a_dt_bias=v_a_dt_bias, a_norm=v_a_norm,
               a_w_out=v_a_w_out, b_w_in=v_b_w_in, b_sinks=v_b_sinks, b_w_out=v_b_w_out, f_w_up=v_f_w_up,
               f_conv=v_f_conv, f_w_down=v_f_w_down, ple_w_proj=v_ple_w_proj, ple_w_gate=v_ple_w_gate)
    hk = N_HEADS_A * HEAD_DIM_A
    xs = x[0]
    tgt = loss_target[0]
    p_bf = p.astype(BF16)

    def shard(name, layer):
        return _wire(name, wts[name][layer]).astype(BF16)

    def stacked_rows(g):
        return g.reshape(g.shape[0] * g.shape[1], g.shape[2])

    first = _all_gather(shard("a_w_in", 0), name="gather_mixer0")
    wa_in_t = jnp.pad(stacked_rows(first), ((0, PROJ_A - PROJ_A_REAL), (0, 0)))
    gconv = _all_gather(_pack([wts[n] for n in CONVS], F32, 8), dep=first, name="gather_convs")
    conv_full = {n: _cols_full(g) for n, g in zip(CONVS, _unpack(gconv, [wts[n].shape for n in CONVS]))}
    cv_a, cv_f = conv_full["a_conv"][0], conv_full["f_conv"]
    layer_names = ("f_w_up", "f_w_down", "ple_w_proj", "ple_w_gate")
    gather0, tok = _exchange_start([shard(n, 0) for n in layer_names + ("a_w_out",)], scatter=False,
                                   name="gather_layer0_start", dep=gconv)
    gather1, tok = _exchange_start([shard(n, 0) for n in ("b_w_in", "b_w_out")] + [shard(n, 1) for n in layer_names],
                                   scatter=False, name="gather_layer1_start", dep=tok)

    alog_row = jnp.pad(a_log, ((0, 0), (N_HEADS_A, LANES - 2 * N_HEADS_A)))
    dtb_row = jnp.pad(a_dt_bias, ((0, 0), (N_HEADS_A, LANES - 2 * N_HEADS_A)))

    tile_f32, tile_bf16, rowsum = (F32, "tile"), (BF16, "tile"), (F32, "rowsum")

    def ffn_ple_fwd(i, h_a, n_f, next_norm, w_up_t, w_down, w_pp_t, w_pg):
        up, y, act = _ffn_up_act(n_f, w_up_t, cv_f[i], name=f"l{i}_ffn_up")
        h_b, n_p = _matmul_rows(act, w_down, _epi_res_norm, [h_a], [norm_ple[i]], [tile_f32, tile_bf16],
                                name=f"l{i}_ffn_down")
        pe = _matmul(p_bf[i, 0], w_pp_t, tb=True, name=f"l{i}_ple_proj")
        res = _matmul_rows(n_p, w_pg, _epi_ple, [h_b, pe], [] if next_norm is None else [next_norm],
                           [tile_f32, tile_f32] + ([] if next_norm is None else [tile_bf16]), name=f"l{i}_ple_gate")
        return res[1], (None if next_norm is None else res[2]), dict(n_f=n_f, up=up, y=y, act=act, h_b=h_b, n_p=n_p, zg=res[0], pe=pe)

    def layer_weights(lands):
        up_t, down, pp_t, pg = (stacked_rows(g) for g in lands)
        return up_t.reshape(2, D_FF, D_MODEL), down, pp_t, pg

    n0 = _rms_fwd(xs, norm_mix[0], name="l0_mix_norm")
    proj = _matmul(n0, wa_in_t, tb=True, tm=512, dep=tok, name="l0_in_proj")
    q, k, v, gbc, bbc, y_qkv = _delta_pre_fwd(proj, cv_a, alog_row, dtb_row, name="l0_delta_pre")
    o, states, tinv, og = _delta_fwd(q, k, v, gbc, bbc, proj, a_norm, name="l0_delta")
    lands0 = _exchange_wait(gather0, og, scatter=False, name="gather_layer0_wait")
    lw0, wa_out = layer_weights(lands0[:4]), stacked_rows(lands0[4])
    h1, nf0 = _matmul_rows(og, wa_out, _epi_res_norm, [xs], [norm_ffn[0]], [tile_f32, tile_bf16], name="l0_mix_out")
    h3, n1, sv0 = ffn_ple_fwd(0, h1, nf0, norm_mix[1], *lw0)

    lands1 = _exchange_wait(gather1, h3, scatter=False, name="gather_layer1_wait")
    wb_in_t, wb_out = stacked_rows(lands1[0]), stacked_rows(lands1[1])
    lw1 = layer_weights(lands1[2:])
    pb = _matmul(n1, wb_in_t, tb=True, name="l1_in_qkv")
    att = _swa_fwd(pb, b_sinks, name="l1_swa")
    h4, nf1 = _matmul_rows(att, wb_out, _epi_res_norm, [h3], [norm_ffn[1]], [tile_f32, tile_bf16], name="l1_mix_out")
    h6, _, sv1 = ffn_ple_fwd(1, h4, nf1, None, *lw1)

    loss_row, dh6, d_norm_final = _final_loss(h6, norm_final, tgt, name="final_loss")
    loss = lax.psum(loss_row[0, 0], MESH_AXES)

    def ffn_ple_bwd(i, dh_c, h_a, sv, lw, dep):
        w_up_t, w_down, _, w_pg = lw
        dzg, dpe = _ple_bwd(dh_c, sv["zg"], sv["pe"], name=f"l{i}_ple_mix_bwd")
        d_pg = _matmul(sv["n_p"], dzg, ta=True, out_dtype=BF16, dep=dep, name=f"l{i}_ple_gate_dw")
        d_pp_t = _matmul(dpe, p_bf[i, 0], ta=True, out_dtype=BF16, name=f"l{i}_ple_proj_dw")
        dh_b, d_np = _matmul_rows(dzg, w_pg, _epi_rms_bwd, [sv["h_b"], dh_c], [norm_ple[i]], [tile_f32, rowsum], tb=True,
                                  name=f"l{i}_ple_gate_dx")
        d_down = _matmul(sv["act"], dh_b, ta=True, out_dtype=BF16, name=f"l{i}_ffn_down_dw")
        dup, d_cg, d_cv = _ffn_down_dx_act_bwd(dh_b, w_down, sv["up"], sv["y"], cv_f[i], name=f"l{i}_ffn_down_dx")
        d_up_t = _matmul(dup, sv["n_f"], ta=True, out_dtype=BF16, name=f"l{i}_ffn_up_dw")
        dh_a, d_nf = _matmul_rows(dup, w_up_t, _epi_rms_bwd, [h_a, dh_b], [norm_ffn[i]], [tile_f32, rowsum],
                                  name=f"l{i}_ffn_up_dx")
        mats = [d_up_t.reshape(2 * D_FF, D_MODEL), d_down, d_pp_t, d_pg]
        return dh_a, mats, dict(norm_ple=d_np, norm_ffn=d_nf, f_conv=jnp.concatenate([d_cg, d_cv], axis=1))

    def slabs(g):
        return g.reshape(N_DEV, g.shape[0] // N_DEV, g.shape[1])

    dh4, mats1, g1 = ffn_ple_bwd(1, dh6, h4, sv1, lw1, None)
    datt = _matmul(dh4, wb_out, tb=True, out_dtype=BF16, name="l1_mix_out_dx")
    d_wb_out = _matmul(att, dh4, ta=True, out_dtype=BF16, name="l1_mix_out_dw")
    dq_b, dk_b, dv_b, dsinks = _swa_bwd(pb, b_sinks, datt, name="l1_swa_bwd")
    dpb = jnp.concatenate([dq_b, dk_b.astype(BF16), dv_b.astype(BF16)], axis=1)
    d_wb_in_t = _matmul(dpb, n1, ta=True, out_dtype=BF16, name="l1_in_qkv_dw")
    send1, tok = _exchange_start([slabs(g) for g in [d_wb_in_t, d_wb_out] + mats1], scatter=True,
                                 name="exchange_layer1_start", dep=d_wb_in_t)
    dh3, d_nm1 = _matmul_rows(dpb, wb_in_t, _epi_rms_bwd, [h3, dh4], [norm_mix[1]], [tile_f32, rowsum], name="l1_in_qkv_dx")

    dh1, mats0, g0 = ffn_ple_bwd(0, dh3, h1, sv0, lw0, tok)
    send0, tok = _exchange_start([slabs(g) for g in mats0], scatter=True, name="exchange_layer0_start", dep=mats0[0])
    dog = _matmul(dh1, wa_out, tb=True, dep=tok, name="l0_mix_out_dx")
    d_wa_out = _matmul(og, dh1, ta=True, out_dtype=BF16, name="l0_mix_out_dw")
    dq, dk, dv, dgbc, dbbc, dz0, d_anorm = _delta_bwd(q, k, v, gbc, bbc, tinv, states, o, proj, a_norm, dog,
                                                      name="l0_delta_bwd")
    dproj, d_aconv, d_alog, d_dtb = _delta_pre_bwd(proj, y_qkv, cv_a, alog_row, dtb_row, dq, dk, dv, dgbc, dbbc, dz0,
                                                   name="l0_delta_pre_bwd")
    d_wa_in_t = _matmul(dproj, n0, ta=True, out_dtype=BF16, name="l0_in_proj_dw")
    sendm, tok = _exchange_start([slabs(d_wa_in_t[:PROJ_A_REAL]), slabs(d_wa_out)], scatter=True,
                                 name="exchange_mixer0_start", dep=d_wa_in_t)
    dx, d_nm0 = _matmul_rows(dproj, wa_in_t, _epi_rms_bwd, [xs, dh1], [norm_mix[0]], [tile_f32, rowsum], dep=tok,
                             name="l0_in_proj_dx")

    recv1 = _exchange_wait(send1, dx, scatter=True, name="exchange_layer1_wait")
    recv0 = _exchange_wait(send0, recv1[0], scatter=True, name="exchange_layer0_wait")
    parts = {("b_w_in", 0): recv1[0], ("b_w_out", 0): recv1[1]}
    parts.update({(n, 1): r for n, r in zip(layer_names, recv1[2:])})
    parts.update({(n, 0): r for n, r in zip(layer_names, recv0)})

    outs = {}

    def update_matrix(name):
        w_, m_, v_ = (_wire(name, a) for a in (wts[name], mom[name], var[name]))
        res = None
        for layer in range(w_.shape[0]):
            res = _adamw_layer(parts[(name, layer)], w_, m_, v_, layer, res, name=f"adamw_{name}_{layer}")
        for kind, arr in zip(("grad", "delta", "new_m", "new_v"), res):
            outs[(kind, name)] = _wire(name, arr)
        return res

    last = [update_matrix(n) for n in ("b_w_in", "b_w_out") + layer_names][-1]
    recvm = _exchange_wait(sendm, last[0], scatter=True, name="exchange_mixer0_wait")
    parts.update({("a_w_in", 0): recvm[0], ("a_w_out", 0): recvm[1]})
    update_matrix("a_w_in")
    update_matrix("a_w_out")

    gconvs = dict(a_conv=d_aconv[None], f_conv=jnp.stack([g0["f_conv"], g1["f_conv"]]))
    small_g = dict(norm_mix=jnp.concatenate([d_nm0, d_nm1]), norm_ffn=jnp.concatenate([g0["norm_ffn"], g1["norm_ffn"]]),
                   norm_ple=jnp.concatenate([g0["norm_ple"], g1["norm_ple"]]), norm_final=d_norm_final[0],
                   a_log=d_alog[:, N_HEADS_A:2 * N_HEADS_A], a_dt_bias=d_dtb[:, N_HEADS_A:2 * N_HEADS_A],
                   a_norm=d_anorm, b_sinks=dsinks[:, :N_HEADS_B])
    recv_conv = _all_to_all(_pack_split(gconvs, CONVS, F32, 8), name="exchange_conv_grads")
    recv_small = _all_gather(_pack([small_g[n] for n in SMALL], F32, 8), name="gather_small_grads")
    for names, recv, tag in ((CONVS, recv_conv, "convs"), (SMALL, recv_small, "small")):
        shapes = [wts[n].shape for n in names]
        g_slab = _sum_parts(recv, name=f"sum_{tag}")
        packed = [_pack([d[n] for n in names], F32, 8) for d in (wts, mom, var)]
        res = _adamw_packed(g_slab, *packed, name=f"adamw_{tag}")
        for kind, slab in zip(("grad", "delta", "new_m", "new_v"), (g_slab,) + tuple(res)):
            for n, arr in zip(names, _unpack(slab, shapes)):
                outs[(kind, n)] = arr

    result = [loss, dx[None]]
    for kind in ("grad", "delta", "new_m", "new_v"):
        result += [outs[(kind, n)] for n in WEIGHTS]
    return tuple(result)
```

```python
import functools
import math

import jax
import jax.numpy as jnp
from jax import lax
from jax.experimental import pallas as pl
from jax.experimental.pallas import tpu as pltpu

F32 = jnp.float32
BF16 = jnp.bfloat16

D_MODEL = 1024
N_HEADS_A = 8
HEAD_DIM_A = 128
CONV_A = 4
CHUNK = 128
N_HEADS_B = 16
N_KV_B = 4
GROUP_B = N_HEADS_B // N_KV_B
HEAD_DIM_B = 64
WINDOW = 128
D_FF = 2816
FFN_CONV = 3
PLE_DIM = 256
EPS = 1e-6
N_DEV = 8
HALO = 8
PROJ_A_REAL = 4 * N_HEADS_A * HEAD_DIM_A + 2 * N_HEADS_A
PROJ_A = 4 * N_HEADS_A * HEAD_DIM_A + 128
Z_COL_BLOCK = 3
BA_COL_BLOCK = 32

ADAM_LR = 0.001
ADAM_B1 = 0.9
ADAM_B2 = 0.999
ADAM_EPS = 1e-08
ADAM_WD = 0.01
ADAM_STEP = 10

LANES = 128
VMEM_LIMIT_BYTES = 56 * 1024 * 1024
NEG_BIG = -1e30

MESH_AXES = ("x", "y", "c")


def _params(sem=None):
    return pltpu.CompilerParams(dimension_semantics=sem, vmem_limit_bytes=VMEM_LIMIT_BYTES)


def _tile(n, target):
    best = None
    for t in range(LANES, min(n, target) + 1, LANES):
        if n % t == 0:
            best = t
    return best or n


def _sigmoid(x):
    return 0.5 * jnp.tanh(0.5 * x) + 0.5


def _softplus(x):
    return jnp.maximum(x, 0.0) + jnp.log1p(jnp.exp(-jnp.abs(x)))


def _matmul(a, b, *, name, ta=False, tb=False, res=None, out_dtype=F32, tm=1408, tn=1408, tk=None, dep=None):
    sa, sb = a.ndim == 3, b.ndim == 3
    ns = a.shape[0] if sa else (b.shape[0] if sb else 1)
    contract_stack = sa and sb
    out_stacked = sa != sb
    m = a.shape[-1] if ta else a.shape[-2]
    k = a.shape[-2] if ta else a.shape[-1]
    n = b.shape[-2] if tb else b.shape[-1]
    assert (b.shape[-1] if tb else b.shape[-2]) == k, (a.shape, b.shape, ta, tb)
    if tk is None:
        tk = 1024 if ta else 2816
    tm, tn, tk = _tile(m, tm), _tile(n, tn), _tile(k, tk)
    nk = k // tk
    nsteps = nk * (ns if contract_stack else 1)
    dims = (((0 if ta else 1,), (1 if tb else 0,)), ((), ()))

    def spec(block, stacked, order):
        def index(g, i, j, kk):
            two = order(i, j, kk % nk)
            if not stacked:
                return two
            return (kk // nk if contract_stack else g,) + two
        return pl.BlockSpec(((None,) if stacked else ()) + block, index)

    a_spec = spec((tk, tm), sa, lambda i, j, kq: (kq, i)) if ta else spec((tm, tk), sa, lambda i, j, kq: (i, kq))
    b_spec = spec((tn, tk), sb, lambda i, j, kq: (j, kq)) if tb else spec((tk, tn), sb, lambda i, j, kq: (kq, j))
    o_spec = spec((tm, tn), out_stacked, lambda i, j, kq: (i, j))
    has_res = res is not None
    has_dep = dep is not None

    def body(*refs):
        a_ref, b_ref = refs[0], refs[1]
        r_ref = refs[2] if has_res else None
        o_ref = refs[2 + has_res + has_dep]
        part = lax.dot_general(a_ref[...].astype(BF16), b_ref[...].astype(BF16), dims, preferred_element_type=F32)

        def finish(acc):
            if has_res:
                acc = acc + r_ref[...].astype(F32)
            o_ref[...] = acc.astype(out_dtype)

        if nsteps == 1:
            finish(part)
        else:
            acc_ref = refs[-1]
            kk = pl.program_id(3)

            @pl.when(kk == 0)
            def _():
                acc_ref[...] = part

            @pl.when(kk > 0)
            def _():
                acc_ref[...] += part

            @pl.when(kk == nsteps - 1)
            def _():
                finish(acc_ref[...])

    in_specs = [a_spec, b_spec] + ([o_spec] if has_res else []) + ([pl.BlockSpec(memory_space=pl.ANY)] if has_dep else [])
    args = (a, b) + ((res,) if has_res else ()) + ((dep,) if has_dep else ())
    return pl.pallas_call(
        body,
        name=name,
        grid=(ns if out_stacked else 1, m // tm, n // tn, nsteps),
        in_specs=in_specs,
        out_specs=o_spec,
        out_shape=jax.ShapeDtypeStruct(((ns,) if out_stacked else ()) + (m, n), out_dtype),
        scratch_shapes=[pltpu.VMEM((tm, tn), F32)] if nsteps > 1 else [],
        compiler_params=_params(("parallel", "parallel", "parallel", "arbitrary")),
    )(*args)


EPI_ROWS = 32


def _matmul_rows(a, b, epilogue, tiles_in, rows_in, outs, *, name, tb=False, tm=512, tk=None, dep=None):
    stacked = a.ndim == 3
    ns = a.shape[0] if stacked else 1
    m, k = a.shape[-2], a.shape[-1]
    n = b.shape[-2] if tb else b.shape[-1]
    assert (b.shape[-1] if tb else b.shape[-2]) == k and (b.ndim == 3) == stacked, (a.shape, b.shape, tb)
    tm, tk = _tile(m, tm), _tile(k, 2816 if tk is None else tk)
    nk = k // tk
    nsteps = nk * ns
    dims = (((1,), (1 if tb else 0,)), ((), ()))
    lead = (None,) if stacked else ()
    front = (lambda kk: (kk // nk,)) if stacked else (lambda kk: ())
    a_spec = pl.BlockSpec(lead + (tm, tk), lambda i, kk: front(kk) + (i, kk % nk))
    if tb:
        b_spec = pl.BlockSpec(lead + (n, tk), lambda i, kk: front(kk) + (0, kk % nk))
    else:
        b_spec = pl.BlockSpec(lead + (tk, n), lambda i, kk: front(kk) + (kk % nk, 0))
    tile_spec = pl.BlockSpec((tm, n), lambda i, kk: (i, 0))
    row_spec = pl.BlockSpec((1, n), lambda i, kk: (0, 0))
    n_t, n_r, has_dep = len(tiles_in), len(rows_in), dep is not None

    def body(*refs):
        a_ref, b_ref = refs[:2]
        tile_refs = refs[2:2 + n_t]
        row_refs = refs[2 + n_t:2 + n_t + n_r]
        out_refs = refs[2 + n_t + n_r + has_dep:-1]
        acc_ref = refs[-1]
        part = lax.dot_general(a_ref[...].astype(BF16), b_ref[...].astype(BF16), dims, preferred_element_type=F32)
        kk = pl.program_id(1)
        if nsteps == 1:
            acc_ref[...] = part
        else:
            @pl.when(kk == 0)
            def _():
                acc_ref[...] = part

            @pl.when(kk > 0)
            def _():
                acc_ref[...] += part

        @pl.when(kk == nsteps - 1)
        def _():
            epilogue(acc_ref, tile_refs, row_refs, out_refs, pl.program_id(0) == 0)

    return pl.pallas_call(
        body,
        name=name,
        grid=(m // tm, nsteps),
        in_specs=[a_spec, b_spec] + [tile_spec] * n_t + [row_spec] * n_r + ([pl.BlockSpec(memory_space=pl.ANY)] if has_dep else []),
        out_specs=[tile_spec if kind == "tile" else row_spec for _, kind in outs],
        out_shape=[jax.ShapeDtypeStruct((m, n) if kind == "tile" else (1, n), dt) for dt, kind in outs],
        scratch_shapes=[pltpu.VMEM((tm, n), F32)],
        compiler_params=_params(("arbitrary", "arbitrary")),
    )(a, b, *tiles_in, *[r.reshape(1, n) for r in rows_in], *((dep,) if has_dep else ()))


def _row_chunks(ref):
    return [pl.ds(r, EPI_ROWS) for r in range(0, ref.shape[0], EPI_ROWS)]


def _rstd(x):
    return lax.rsqrt(jnp.mean(x * x, axis=-1, keepdims=True) + EPS)


def _epi_res_norm(acc, tiles, rows, outs, first):
    (res,), (w,), (h_out, n_out) = tiles, rows, outs
    chunks = _row_chunks(acc)
    for rs in chunks:
        h_out[rs, :] = acc[rs, :] + res[rs, :]
    rstds = [_rstd(h_out[rs, :]) for rs in chunks]
    for rs, r in zip(chunks, rstds):
        n_out[rs, :] = (h_out[rs, :] * r * w[...]).astype(BF16)


def _epi_ple(acc, tiles, rows, outs, first):
    hb, pe = tiles
    chunks = _row_chunks(acc)
    for rs in chunks:
        zg = acc[rs, :]
        outs[0][rs, :] = zg
        outs[1][rs, :] = hb[rs, :] + _sigmoid(zg) * pe[rs, :]
    if rows:
        rstds = [_rstd(outs[1][rs, :]) for rs in chunks]
        for rs, r in zip(chunks, rstds):
            outs[2][rs, :] = (outs[1][rs, :] * r * rows[0][...]).astype(BF16)


def _epi_rms_bwd(acc, tiles, rows, outs, first):
    (h_ref, skip), (w,), (dh_out, dw_out) = tiles, rows, outs
    chunks = _row_chunks(acc)
    n = acc.shape[1]
    rstds = [_rstd(h_ref[rs, :]) for rs in chunks]
    dots = [jnp.sum(acc[rs, :] * w[...] * h_ref[rs, :], axis=-1, keepdims=True) * r * (1.0 / n)
            for rs, r in zip(chunks, rstds)]
    dw = jnp.zeros((1, n), F32)
    for rs, r, dt in zip(chunks, rstds, dots):
        nh = h_ref[rs, :] * r
        g = acc[rs, :]
        dh_out[rs, :] = r * (g * w[...] - nh * dt) + skip[rs, :]
        dw = dw + jnp.sum(g * nh, axis=0, keepdims=True)

    @pl.when(first)
    def _():
        dw_out[...] = dw

    @pl.when(jnp.logical_not(first))
    def _():
        dw_out[...] += dw


def _rms_fwd(h, w, *, name, tm=512):
    t, d = h.shape
    tm = _tile(t, tm)

    def body(h_ref, w_ref, o_ref):
        x = h_ref[...]
        r = lax.rsqrt(jnp.mean(x * x, axis=-1, keepdims=True) + EPS)
        o_ref[...] = (x * r * w_ref[...]).astype(BF16)

    return pl.pallas_call(
        body,
        name=name,
        grid=(t // tm,),
        in_specs=[pl.BlockSpec((tm, d), lambda i: (i, 0)), pl.BlockSpec((1, d), lambda i: (0, 0))],
        out_specs=pl.BlockSpec((tm, d), lambda i: (i, 0)),
        out_shape=jax.ShapeDtypeStruct((t, d), BF16),
        compiler_params=_params(("parallel",)),
    )(h, w.reshape(1, d))


def _rms_bwd(h, w, dn, skip, *, name, tm=512):
    t, d = h.shape
    tm = _tile(t, tm)

    def body(h_ref, w_ref, dn_ref, skip_ref, dh_ref, dw_ref):
        i = pl.program_id(0)
        x = h_ref[...]
        r = lax.rsqrt(jnp.mean(x * x, axis=-1, keepdims=True) + EPS)
        nh = x * r
        g = dn_ref[...].astype(F32)
        gw = g * w_ref[...]
        dh_ref[...] = r * (gw - nh * jnp.mean(gw * nh, axis=-1, keepdims=True)) + skip_ref[...]
        part = jnp.sum(g * nh, axis=0, keepdims=True)

        @pl.when(i == 0)
        def _():
            dw_ref[...] = part

        @pl.when(i > 0)
        def _():
            dw_ref[...] += part

    row = pl.BlockSpec((tm, d), lambda i: (i, 0))
    vec = pl.BlockSpec((1, d), lambda i: (0, 0))
    return pl.pallas_call(
        body,
        name=name,
        grid=(t // tm,),
        in_specs=[row, vec, row, row],
        out_specs=[row, vec],
        out_shape=[jax.ShapeDtypeStruct((t, d), F32), jax.ShapeDtypeStruct((1, d), F32)],
        compiler_params=_params(("arbitrary",)),
    )(h, w.reshape(1, d), dn, skip)


def _final_loss(h, w, target, *, name, tm=512):
    t, d = h.shape
    tm = _tile(t, tm)

    def body(h_ref, w_ref, tg_ref, loss_ref, dh_ref, dw_ref):
        i = pl.program_id(0)
        x = h_ref[...]
        r = lax.rsqrt(jnp.mean(x * x, axis=-1, keepdims=True) + EPS)
        nh = x * r
        err = nh * w_ref[...] - tg_ref[...]
        lpart = (0.5 / d) * jnp.sum(jnp.sum(err * err, axis=-1, keepdims=True), axis=0, keepdims=True)
        g = err * (1.0 / d)
        gw = g * w_ref[...]
        dh_ref[...] = r * (gw - nh * jnp.mean(gw * nh, axis=-1, keepdims=True))
        part = jnp.sum(g * nh, axis=0, keepdims=True)
        lrow = jnp.broadcast_to(lpart, (1, LANES))

        @pl.when(i == 0)
        def _():
            dw_ref[...] = part
            loss_ref[...] = lrow

        @pl.when(i > 0)
        def _():
            dw_ref[...] += part
            loss_ref[...] += lrow

    row = pl.BlockSpec((tm, d), lambda i: (i, 0))
    vec = pl.BlockSpec((1, d), lambda i: (0, 0))
    return pl.pallas_call(
        body,
        name=name,
        grid=(t // tm,),
        in_specs=[row, vec, row],
        out_specs=[pl.BlockSpec((1, LANES), lambda i: (0, 0)), row, vec],
        out_shape=[jax.ShapeDtypeStruct((1, LANES), F32), jax.ShapeDtypeStruct((t, d), F32), jax.ShapeDtypeStruct((1, d), F32)],
        compiler_params=_params(("arbitrary",)),
    )(h, w.reshape(1, d), target)


def _conv_from_ext(ext_ref, cw_ref, kw, tm):
    y = cw_ref[kw - 1:kw, :] * ext_ref[pl.ds(HALO, tm), :]
    for i in range(kw - 1):
        y = y + cw_ref[i:i + 1, :] * ext_ref[pl.ds(HALO - (kw - 1) + i, tm), :]
    return y


ROW_CHUNK = 64


def _shifted_rows(src_ref, base, rows, shifts, cols=slice(None)):
    ext = src_ref[pl.ds(base - HALO, rows + HALO), cols]
    return [ext[HALO:, :] if s == 0 else pltpu.roll(ext, s, 0)[HALO:, :] for s in shifts]


def _conv_rows(src_ref, base, rows, cw_ref, kw, cols=slice(None)):
    wins = _shifted_rows(src_ref, base, rows, range(kw), cols)
    y = cw_ref[kw - 1:kw, cols] * wins[0]
    for s in range(1, kw):
        y = y + cw_ref[kw - 1 - s:kw - s, cols] * wins[s]
    return y


def _ahead_rows(src_ref, base, rows, shifts, cols=slice(None)):
    ext = src_ref[pl.ds(base, rows + HALO), cols]
    return [ext[:rows, :] if s == 0 else pltpu.roll(ext, rows + HALO - s, 0)[:rows, :] for s in shifts]


def _conv_t_rows(dy_ref, base, rows, cw_ref, kw, cols=slice(None)):
    wins = _ahead_rows(dy_ref, base, rows, range(kw), cols)
    dx = cw_ref[kw - 1:kw, cols] * wins[0]
    for s in range(1, kw):
        dx = dx + cw_ref[kw - 1 - s:kw - s, cols] * wins[s]
    return dx


def _fold_rows(x):
    out = x[0:HALO, :]
    for g in range(1, x.shape[0] // HALO):
        out = out + x[g * HALO:(g + 1) * HALO, :]
    return out


def _conv_bwd_from_ext(xext_ref, dyext_ref, cw_ref, dcw_ref, kw, tm, first):
    dy = dyext_ref[pl.ds(0, tm), :]
    dx = cw_ref[kw - 1:kw, :] * dy
    for i in range(kw - 1):
        dx = dx + cw_ref[i:i + 1, :] * dyext_ref[pl.ds(kw - 1 - i, tm), :]
    for i in range(kw):
        part = jnp.sum(dy * xext_ref[pl.ds(HALO - (kw - 1) + i, tm), :], axis=0, keepdims=True)

        @pl.when(first)
        def _():
            dcw_ref[i:i + 1, :] = part

        @pl.when(jnp.logical_not(first))
        def _():
            dcw_ref[i:i + 1, :] += part

    return dx


def _delta_pre_fwd(proj, conv_w, alog_row, dtb_row, *, name, tm=256):
    t = proj.shape[0]
    c3 = 3 * N_HEADS_A * HEAD_DIM_A
    hk = N_HEADS_A * HEAD_DIM_A
    tm = _tile(t, tm)

    rc = min(ROW_CHUNK, tm)

    def body(x_ref, ba_ref, cw_ref, al_ref, db_ref, q_ref, k_ref, v_ref, g_ref, b_ref, y_ref, hx):
        i = pl.program_id(0)

        @pl.when(i == 0)
        def _():
            hx[0:HALO, :] = jnp.zeros((HALO, c3), F32)

        hx[pl.ds(HALO, rc), :] = x_ref[0:rc, :]
        dsts = (q_ref, k_ref, v_ref)
        for r in range(tm // rc):
            rows = slice(r * rc, (r + 1) * rc)
            src, base = (hx, HALO) if r == 0 else (x_ref, r * rc)
            for cb in range(c3 // HEAD_DIM_A):
                cols = slice(cb * HEAD_DIM_A, (cb + 1) * HEAD_DIM_A)
                y = _conv_rows(src, base, rc, cw_ref, CONV_A, cols)
                y_ref[rows, cols] = y
                s = y * _sigmoid(y)
                kind, h = divmod(cb, N_HEADS_A)
                if kind < 2:
                    s = s * lax.rsqrt(jnp.sum(s * s, axis=-1, keepdims=True) + EPS)
                dsts[kind][rows, h * HEAD_DIM_A:(h + 1) * HEAD_DIM_A] = s
        hx[0:HALO, :] = x_ref[tm - HALO:tm, :]
        ba = ba_ref[...]
        beta = _sigmoid(ba)
        gfull = -jnp.exp(al_ref[...]) * _softplus(ba + db_ref[...])
        for h in range(N_HEADS_A):
            lo = h * HEAD_DIM_A
            b_ref[:, lo:lo + HEAD_DIM_A] = jnp.broadcast_to(beta[:, h:h + 1], (tm, HEAD_DIM_A))
            g_ref[:, lo:lo + HEAD_DIM_A] = jnp.broadcast_to(gfull[:, N_HEADS_A + h:N_HEADS_A + h + 1], (tm, HEAD_DIM_A))

    row = lambda w: pl.BlockSpec((tm, w), lambda i: (i, 0))
    fixed = lambda r, w: pl.BlockSpec((r, w), lambda i: (0, 0))
    out = jax.ShapeDtypeStruct((t, hk), F32)
    return pl.pallas_call(
        body,
        name=name,
        grid=(t // tm,),
        in_specs=[row(c3), pl.BlockSpec((tm, LANES), lambda i: (i, BA_COL_BLOCK)), fixed(CONV_A, c3), fixed(1, LANES),
                  fixed(1, LANES)],
        out_specs=[row(hk)] * 5 + [row(c3)],
        out_shape=[out] * 5 + [jax.ShapeDtypeStruct((t, c3), F32)],
        scratch_shapes=[pltpu.VMEM((HALO + rc, c3), F32)],
        compiler_params=_params(("arbitrary",)),
    )(proj, proj, conv_w, alog_row, dtb_row)


def _delta_pre_bwd(proj, y, conv_w, alog_row, dtb_row, dq, dk, dv, dg, db, dz, *, name, tm=256):
    t, pw = proj.shape
    c3 = 3 * N_HEADS_A * HEAD_DIM_A
    hk = N_HEADS_A * HEAD_DIM_A
    tm = _tile(t, tm)
    nt = t // tm

    rc = min(ROW_CHUNK, tm)
    kw = CONV_A

    def body(x_ref, y_ref, ba_ref, cw_ref, al_ref, db_ref, dq_ref, dk_ref, dv_ref, dg_ref, dbt_ref, dz_ref,
             dp_ref, dcw_ref, dal_ref, ddb_ref, *scratch):
        dys, acc = scratch[:-1], scratch[-1]
        i = pl.program_id(0)
        first = i == 0

        @pl.when(first)
        def _():
            for dyb in dys:
                dyb[pl.ds(tm, HALO), :] = jnp.zeros((HALO, HEAD_DIM_A), F32)

        srcs = (dq_ref, dk_ref, dv_ref)
        ncb = c3 // HEAD_DIM_A
        taps = [[jnp.zeros((HALO, HEAD_DIM_A), F32) for _ in range(kw)] for _ in range(ncb)]
        for r in reversed(range(tm // rc)):
            rows = slice(r * rc, (r + 1) * rc)
            for kind in range(3):
                cbs = range(kind * N_HEADS_A, (kind + 1) * N_HEADS_A)
                hs = [slice(h * HEAD_DIM_A, (h + 1) * HEAD_DIM_A) for h in range(N_HEADS_A)]
                ys = [y_ref[rows, cb * HEAD_DIM_A:(cb + 1) * HEAD_DIM_A] for cb in cbs]
                sgs = [_sigmoid(yv) for yv in ys]
                dss = [srcs[kind][rows, hsl] for hsl in hs]
                if kind < 2:
                    ss = [yv * sg for yv, sg in zip(ys, sgs)]
                    rns = [lax.rsqrt(jnp.sum(s * s, axis=-1, keepdims=True) + EPS) for s in ss]
                    qns = [s * rn for s, rn in zip(ss, rns)]
                    dots = [jnp.sum(ds * qn, axis=-1, keepdims=True) for ds, qn in zip(dss, qns)]
                    dss = [rn * (ds - qn * dt) for rn, ds, qn, dt in zip(rns, dss, qns, dots)]
                for cb, yv, sg, ds in zip(cbs, ys, sgs, dss):
                    cols = slice(cb * HEAD_DIM_A, (cb + 1) * HEAD_DIM_A)
                    dys[cb][rows, :] = ds * (sg * (1.0 + yv * (1.0 - sg)))
                    ahead = _ahead_rows(dys[cb], r * rc, rc, range(kw))
                    dp_ref[rows, cols] = sum(cw_ref[kw - 1 - s:kw - s, cols] * ahead[s] for s in range(kw)).astype(BF16)
                    xv = x_ref[rows, cols]
                    for s in range(kw):
                        taps[cb][kw - 1 - s] = taps[cb][kw - 1 - s] + _fold_rows(xv * ahead[s])
        for cb in range(ncb):
            cols = slice(cb * HEAD_DIM_A, (cb + 1) * HEAD_DIM_A)
            for j in range(kw):
                acc[j * HALO:(j + 1) * HALO, cols] = taps[cb][j]
            dys[cb][pl.ds(tm, HALO), :] = dys[cb][0:HALO, :]
        for j in range(kw):
            tap = jnp.sum(acc[j * HALO:(j + 1) * HALO, :], axis=0, keepdims=True)

            @pl.when(first)
            def _():
                dcw_ref[j:j + 1, :] = tap

            @pl.when(jnp.logical_not(first))
            def _():
                dcw_ref[j:j + 1, :] += tap

        dp_ref[:, c3:c3 + hk] = dz_ref[...]

        lane = lax.broadcasted_iota(jnp.int32, (tm, LANES), 1)
        gcol = jnp.zeros((tm, LANES), F32)
        for h in range(N_HEADS_A):
            lo = h * HEAD_DIM_A
            dbh = jnp.sum(dbt_ref[:, lo:lo + HEAD_DIM_A], axis=-1, keepdims=True)
            dgh = jnp.sum(dg_ref[:, lo:lo + HEAD_DIM_A], axis=-1, keepdims=True)
            gcol = gcol + jnp.where(lane == h, dbh, 0.0) + jnp.where(lane == N_HEADS_A + h, dgh, 0.0)
        ba = ba_ref[...]
        beta = _sigmoid(ba)
        a_neg = -jnp.exp(al_ref[...])
        z = ba + db_ref[...]
        dz = gcol * a_neg * _sigmoid(z)
        is_g = jnp.logical_and(lane >= N_HEADS_A, lane < 2 * N_HEADS_A)
        dba = jnp.where(lane < N_HEADS_A, gcol * beta * (1.0 - beta), jnp.where(is_g, dz, 0.0))
        dp_ref[:, c3 + hk:pw] = dba.astype(BF16)
        dal = jnp.sum(jnp.where(is_g, gcol * a_neg * _softplus(z), 0.0), axis=0, keepdims=True)
        ddb = jnp.sum(jnp.where(is_g, dz, 0.0), axis=0, keepdims=True)

        @pl.when(first)
        def _():
            dal_ref[...] = dal
            ddb_ref[...] = ddb

        @pl.when(jnp.logical_not(first))
        def _():
            dal_ref[...] += dal
            ddb_ref[...] += ddb

    rev = lambda w: pl.BlockSpec((tm, w), lambda i: (nt - 1 - i, 0))
    fixed = lambda r, w: pl.BlockSpec((r, w), lambda i: (0, 0))
    return pl.pallas_call(
        body,
        name=name,
        grid=(nt,),
        in_specs=[rev(c3), rev(c3), pl.BlockSpec((tm, LANES), lambda i: (nt - 1 - i, BA_COL_BLOCK)), fixed(CONV_A, c3),
                  fixed(1, LANES), fixed(1, LANES)] + [rev(hk)] * 6,
        out_specs=[rev(pw), fixed(CONV_A, c3), fixed(1, LANES), fixed(1, LANES)],
        out_shape=[jax.ShapeDtypeStruct((t, pw), BF16), jax.ShapeDtypeStruct((CONV_A, c3), F32),
                   jax.ShapeDtypeStruct((1, LANES), F32), jax.ShapeDtypeStruct((1, LANES), F32)],
        scratch_shapes=[pltpu.VMEM((tm + HALO, HEAD_DIM_A), F32)] * (c3 // HEAD_DIM_A) + [pltpu.VMEM((CONV_A * HALO, c3), F32)],
        compiler_params=_params(("arbitrary",)),
    )(proj, y, proj, conv_w, alog_row, dtb_row, dq, dk, dv, dg, db, dz)


def _gated_norm_fwd(o, proj, w, *, name, tm=512):
    t, d = o.shape
    tm = _tile(t, tm)

    def body(o_ref, z_ref, w_ref, y_ref):
        for h in range(N_HEADS_A):
            sl = slice(h * HEAD_DIM_A, (h + 1) * HEAD_DIM_A)
            oh = o_ref[:, sl]
            zh = z_ref[:, sl]
            r = lax.rsqrt(jnp.mean(oh * oh, axis=-1, keepdims=True) + EPS)
            y_ref[:, sl] = (oh * r * w_ref[...] * (zh * _sigmoid(zh))).astype(BF16)

    row = pl.BlockSpec((tm, d), lambda i: (i, 0))
    return pl.pallas_call(
        body,
        name=name,
        grid=(t // tm,),
        in_specs=[row, pl.BlockSpec((tm, d), lambda i: (i, Z_COL_BLOCK)), pl.BlockSpec((1, HEAD_DIM_A), lambda i: (0, 0))],
        out_specs=row,
        out_shape=jax.ShapeDtypeStruct((t, d), BF16),
        compiler_params=_params(("parallel",)),
    )(o, proj, w)


def _gated_norm_bwd(o, proj, w, dy, *, name, tm=512):
    t, d = o.shape
    tm = _tile(t, tm)

    def body(o_ref, z_ref, w_ref, dy_ref, do_ref, dz_ref, dw_ref):
        i = pl.program_id(0)
        dw = jnp.zeros((1, HEAD_DIM_A), F32)
        for h in range(N_HEADS_A):
            sl = slice(h * HEAD_DIM_A, (h + 1) * HEAD_DIM_A)
            oh = o_ref[:, sl]
            zh = z_ref[:, sl]
            g = dy_ref[:, sl]
            r = lax.rsqrt(jnp.mean(oh * oh, axis=-1, keepdims=True) + EPS)
            nh = oh * r
            sg = _sigmoid(zh)
            dz_ref[:, sl] = (g * nh * w_ref[...] * (sg * (1.0 + zh * (1.0 - sg)))).astype(BF16)
            dt = g * (zh * sg)
            dw = dw + jnp.sum(dt * nh, axis=0, keepdims=True)
            dnh = dt * w_ref[...]
            do_ref[:, sl] = r * (dnh - nh * jnp.mean(dnh * nh, axis=-1, keepdims=True))

        @pl.when(i == 0)
        def _():
            dw_ref[...] = dw

        @pl.when(i > 0)
        def _():
            dw_ref[...] += dw

    row = pl.BlockSpec((tm, d), lambda i: (i, 0))
    vec = pl.BlockSpec((1, HEAD_DIM_A), lambda i: (0, 0))
    return pl.pallas_call(
        body,
        name=name,
        grid=(t // tm,),
        in_specs=[row, pl.BlockSpec((tm, d), lambda i: (i, Z_COL_BLOCK)), vec, row],
        out_specs=[row, row, vec],
        out_shape=[jax.ShapeDtypeStruct((t, d), F32), jax.ShapeDtypeStruct((t, d), BF16),
                   jax.ShapeDtypeStruct((1, HEAD_DIM_A), F32)],
        compiler_params=_params(("arbitrary",)),
    )(o, proj, w, dy)


_NN = (((1,), (0,)), ((), ()))
_NT = (((1,), (1,)), ((), ()))
_TN = (((0,), (0,)), ((), ()))
_DIMS = {"nn": _NN, "nt": _NT, "tn": _TN}


def _raw_dot(a, b, kind, prec):
    dims = _DIMS[kind]
    a_hi, b_hi = a.astype(BF16), b.astype(BF16)
    out = lax.dot_general(a_hi, b_hi, dims, preferred_element_type=F32)
    if prec == "x3":
        a_lo = (a - a_hi.astype(F32)).astype(BF16)
        b_lo = (b - b_hi.astype(F32)).astype(BF16)
        out = out + lax.dot_general(a_hi, b_lo, dims, preferred_element_type=F32)
        out = out + lax.dot_general(a_lo, b_hi, dims, preferred_element_type=F32)
    elif prec == "s3":
        r1 = b - b_hi.astype(F32)
        b_mid = r1.astype(BF16)
        b_lo = (r1 - b_mid.astype(F32)).astype(BF16)
        out = out + lax.dot_general(a_hi, b_mid, dims, preferred_element_type=F32)
        out = out + lax.dot_general(a_hi, b_lo, dims, preferred_element_type=F32)
    return out


def _raw_dots(xs, ys, kind, prec):
    return [_raw_dot(x, y, kind, prec) for x, y in zip(xs, ys)]


@functools.partial(jax.custom_vjp, nondiff_argnums=(2, 3))
def _dots(xs, ys, kind, prec):
    return _raw_dots(xs, ys, kind, prec)


def _dots_fwd(xs, ys, kind, prec):
    return _raw_dots(xs, ys, kind, prec), (xs, ys)


def _dots_bwd(kind, prec, saved, gs):
    xs, ys = saved
    if kind == "nn":
        return _raw_dots(gs, ys, "nt", prec), _raw_dots(xs, gs, "tn", prec)
    if kind == "nt":
        return _raw_dots(gs, ys, "nn", prec), _raw_dots(gs, xs, "tn", prec)
    return _raw_dots(ys, gs, "nt", prec), _raw_dots(xs, gs, "nn", prec)


_dots.defvjp(_dots_fwd, _dots_bwd)


def _eye(c):
    return (lax.broadcasted_iota(jnp.int32, (c, c), 0) == lax.broadcasted_iota(jnp.int32, (c, c), 1)).astype(F32)


def _inv_unit_lower_raw(lmats):
    c = lmats[0].shape[0]
    eye = _eye(c)
    xs = [eye - l for l in lmats]
    ps = lmats
    for _ in range(int(math.log2(c)) - 1):
        ps = _raw_dots(ps, ps, "nn", "bf16")
        xs = [x + d for x, d in zip(xs, _raw_dots(xs, ps, "nn", "bf16"))]
    rs = [x - eye + d for x, d in zip(xs, _raw_dots(lmats, xs, "nn", "x3"))]
    return [x - d for x, d in zip(xs, _raw_dots(xs, rs, "nn", "bf16"))]


@jax.custom_vjp
def _inv_unit_lower(lmats, hints):
    return _inv_unit_lower_raw(lmats) if hints is None else hints


def _inv_fwd(lmats, hints):
    tms = _inv_unit_lower_raw(lmats) if hints is None else hints
    return tms, (tms, hints)


def _inv_bwd(saved, gs):
    tms, hints = saved
    ds = [-d for d in _raw_dots(_raw_dots(tms, gs, "tn", "bf16"), tms, "nt", "bf16")]
    return ds, (None if hints is None else [jnp.zeros_like(h) for h in hints])


_inv_unit_lower.defvjp(_inv_fwd, _inv_bwd)


def _delta_prep(qs, ks, vs, gs, bs, hints=None):
    c = qs[0].shape[0]
    nh = len(qs)
    ii = lax.broadcasted_iota(jnp.int32, (c, c), 0)
    jj = lax.broadcasted_iota(jnp.int32, (c, c), 1)
    incl = ii >= jj
    strict = ii > jj
    ltri = incl.astype(F32)
    eye = _eye(c)
    m1 = _dots([ltri] * nh, gs, "nn", "s3")
    gtot = [jnp.sum(g, axis=0, keepdims=True) for g in gs]
    decay = [jnp.exp(jnp.where(incl, m - m.T, NEG_BIG)) for m in m1]
    eg = [jnp.exp(m) for m in m1]
    kk = _dots(ks, ks, "nt", "bf16")
    lmats = [jnp.where(strict, b * x * d, 0.0) for b, x, d in zip(bs, kk, decay)]
    tinv = _inv_unit_lower(lmats, hints)
    toff = [t - eye for t in tinv]
    bv = [b * v for b, v in zip(bs, vs)]
    bk = [b * e * k for b, e, k in zip(bs, eg, ks)]
    u0 = [x + d for x, d in zip(bv, _dots(toff, bv, "nn", "bf16"))]
    wk = [x + d for x, d in zip(bk, _dots(toff, bk, "nn", "bf16"))]
    qsc = [q * (HEAD_DIM_A ** -0.5) for q in qs]
    qk = [x * d for x, d in zip(_dots(qsc, ks, "nt", "bf16"), decay)]
    q_dec = [q * e for q, e in zip(qsc, eg)]
    k_dec = [k * jnp.exp(t - m) for k, t, m in zip(ks, gtot, m1)]
    glast = [jnp.broadcast_to(jnp.exp(t), (c, c)) for t in gtot]
    return (u0, wk, qk, q_dec, k_dec, glast), tinv


def _delta_step(ss, u0, wk, qk, q_dec, k_dec, glast):
    us = [a - d for a, d in zip(u0, _dots(wk, ss, "nn", "bf16"))]
    os_ = [a + d for a, d in zip(_dots(q_dec, ss, "nn", "bf16"), _dots(qk, us, "nn", "bf16"))]
    s_new = [g * s + d for g, s, d in zip(glast, ss, _dots(k_dec, us, "tn", "bf16"))]
    return os_, s_new


HEADS_PER_STEP = 8


def _chunk_spec(nc, reverse=False):
    w = HEADS_PER_STEP * HEAD_DIM_A
    if reverse:
        return pl.BlockSpec((CHUNK, w), lambda h, n: (nc - 1 - n, h))
    return pl.BlockSpec((CHUNK, w), lambda h, n: (n, h))


def _head_slices():
    return [slice(j * HEAD_DIM_A, (j + 1) * HEAD_DIM_A) for j in range(HEADS_PER_STEP)]


def _heads(ref):
    return [ref[:, sl] for sl in _head_slices()]


def _delta_prep_fwd(q, k, v, gbc, bbc, *, name):
    t, d = q.shape
    nc = t // CHUNK

    def body(q_ref, k_ref, v_ref, g_ref, b_ref, *outs):
        res, tinv = _delta_prep(*[_heads(r) for r in (q_ref, k_ref, v_ref, g_ref, b_ref)])
        for ref, vals in zip(outs, res + (tinv,)):
            for sl, val in zip(_head_slices(), vals):
                ref[:, sl] = val

    spec = _chunk_spec(nc)
    return pl.pallas_call(
        body,
        name=name,
        grid=(N_HEADS_A // HEADS_PER_STEP, nc),
        in_specs=[spec] * 5,
        out_specs=[spec] * 7,
        out_shape=[jax.ShapeDtypeStruct((t, d), F32)] * 7,
        compiler_params=_params(("parallel", "parallel")),
    )(q, k, v, gbc, bbc)


def _delta_prep_bwd(q, k, v, gbc, bbc, tinv, cts, *, name):
    t, d = q.shape
    nc = t // CHUNK

    def body(q_ref, k_ref, v_ref, g_ref, b_ref, t_ref, c0, c1, c2, c3, c4, c5, *outs):
        def f(q_, k_, v_, g_, b_):
            return _delta_prep(q_, k_, v_, g_, b_, hints=_heads(t_ref))[0]

        _, vjp = jax.vjp(f, *[_heads(r) for r in (q_ref, k_ref, v_ref, g_ref, b_ref)])
        grads = vjp(tuple(_heads(c) for c in (c0, c1, c2, c3, c4, c5)))
        for ref, vals in zip(outs, grads):
            for sl, val in zip(_head_slices(), vals):
                ref[:, sl] = val

    spec = _chunk_spec(nc)
    return pl.pallas_call(
        body,
        name=name,
        grid=(N_HEADS_A // HEADS_PER_STEP, nc),
        in_specs=[spec] * 12,
        out_specs=[spec] * 5,
        out_shape=[jax.ShapeDtypeStruct((t, d), F32)] * 5,
        compiler_params=_params(("parallel", "parallel")),
    )(q, k, v, gbc, bbc, tinv, *cts)


def _delta_scan_fwd(prep, *, name):
    t, d = prep[0].shape
    nc = t // CHUNK

    def body(u0, wk, qk, qd, kd, gl, o_ref, st_ref, s_ref):
        n = pl.program_id(1)

        @pl.when(n == 0)
        def _():
            s_ref[...] = jnp.zeros(s_ref.shape, F32)

        ss = [s_ref[j] for j in range(HEADS_PER_STEP)]
        os_, s_new = _delta_step(ss, *[_heads(r) for r in (u0, wk, qk, qd, kd, gl)])
        for j, sl in enumerate(_head_slices()):
            st_ref[:, sl] = ss[j]
            o_ref[:, sl] = os_[j]
            s_ref[j] = s_new[j]

    spec = _chunk_spec(nc)
    return pl.pallas_call(
        body,
        name=name,
        grid=(N_HEADS_A // HEADS_PER_STEP, nc),
        in_specs=[spec] * 6,
        out_specs=[spec] * 2,
        out_shape=[jax.ShapeDtypeStruct((t, d), F32)] * 2,
        scratch_shapes=[pltpu.VMEM((HEADS_PER_STEP, HEAD_DIM_A, HEAD_DIM_A), F32)],
        compiler_params=_params(("parallel", "arbitrary")),
    )(*prep)


def _delta_scan_bwd(prep, states, do, *, name):
    t, d = do.shape
    nc = t // CHUNK

    def body(u0, wk, qk, qd, kd, gl, st_ref, do_ref, *rest):
        outs, ds_ref = rest[:6], rest[6]
        n = pl.program_id(1)

        @pl.when(n == 0)
        def _():
            ds_ref[...] = jnp.zeros(ds_ref.shape, F32)

        _, vjp = jax.vjp(_delta_step, *[_heads(r) for r in (st_ref, u0, wk, qk, qd, kd, gl)])
        grads = vjp((_heads(do_ref), [ds_ref[j] for j in range(HEADS_PER_STEP)]))
        for j, sl in enumerate(_head_slices()):
            ds_ref[j] = grads[0][j]
            for ref, vals in zip(outs, grads[1:]):
                ref[:, sl] = vals[j]

    spec = _chunk_spec(nc, reverse=True)
    return pl.pallas_call(
        body,
        name=name,
        grid=(N_HEADS_A // HEADS_PER_STEP, nc),
        in_specs=[spec] * 8,
        out_specs=[spec] * 6,
        out_shape=[jax.ShapeDtypeStruct((t, d), F32)] * 6,
        scratch_shapes=[pltpu.VMEM((HEADS_PER_STEP, HEAD_DIM_A, HEAD_DIM_A), F32)],
        compiler_params=_params(("parallel", "arbitrary")),
    )(*prep, states, do)


def _delta_fwd(q, k, v, gbc, bbc, proj, norm_w, *, name):
    assert HEADS_PER_STEP == N_HEADS_A
    t, d = q.shape
    nc = t // CHUNK

    def body(q_ref, k_ref, v_ref, g_ref, b_ref, z_ref, w_ref, o_ref, st_ref, t_ref, og_ref, s_ref):
        n = pl.program_id(0)

        @pl.when(n == 0)
        def _():
            s_ref[...] = jnp.zeros(s_ref.shape, F32)

        ss = [s_ref[j] for j in range(N_HEADS_A)]
        res, tinv = _delta_prep(*[_heads(r) for r in (q_ref, k_ref, v_ref, g_ref, b_ref)])
        os_, s_new = _delta_step(ss, *res)
        for j, sl in enumerate(_head_slices()):
            st_ref[:, sl] = ss[j]
            t_ref[:, sl] = tinv[j]
            o_ref[:, sl] = os_[j]
            s_ref[j] = s_new[j]
            zh = z_ref[:, sl]
            r = lax.rsqrt(jnp.mean(os_[j] * os_[j], axis=-1, keepdims=True) + EPS)
            og_ref[:, sl] = (os_[j] * r * w_ref[...] * (zh * _sigmoid(zh))).astype(BF16)

    spec = pl.BlockSpec((CHUNK, d), lambda n: (n, 0))
    f32 = jax.ShapeDtypeStruct((t, d), F32)
    return pl.pallas_call(
        body,
        name=name,
        grid=(nc,),
        in_specs=[spec] * 5 + [pl.BlockSpec((CHUNK, d), lambda n: (n, Z_COL_BLOCK)), pl.BlockSpec((1, HEAD_DIM_A), lambda n: (0, 0))],
        out_specs=[spec] * 4,
        out_shape=[f32, f32, f32, jax.ShapeDtypeStruct((t, d), BF16)],
        scratch_shapes=[pltpu.VMEM((N_HEADS_A, HEAD_DIM_A, HEAD_DIM_A), F32)],
        compiler_params=_params(("arbitrary",)),
    )(q, k, v, gbc, bbc, proj, norm_w)


def _delta_bwd(q, k, v, gbc, bbc, tinv, states, o, proj, norm_w, dog, *, name):
    t, d = q.shape
    nc = t // CHUNK

    def body(q_ref, k_ref, v_ref, g_ref, b_ref, t_ref, st_ref, o_ref, z_ref, w_ref, dog_ref,
             dq_ref, dk_ref, dv_ref, dg_ref, db_ref, dz_ref, dw_ref, ds_ref):
        n = pl.program_id(0)

        @pl.when(n == 0)
        def _():
            ds_ref[...] = jnp.zeros(ds_ref.shape, F32)

        hsl = _head_slices()
        rstds = [_rstd(o_ref[:, sl]) for sl in hsl]
        nhs = [o_ref[:, sl] * r for sl, r in zip(hsl, rstds)]
        sgs = [_sigmoid(z_ref[:, sl]) for sl in hsl]
        dts = [dog_ref[:, sl] * (z_ref[:, sl] * sg) for sl, sg in zip(hsl, sgs)]
        dnhs = [dt * w_ref[...] for dt in dts]
        means = [jnp.mean(dnh * nh, axis=-1, keepdims=True) for dnh, nh in zip(dnhs, nhs)]
        dos = [r * (dnh - nh * mn) for r, dnh, nh, mn in zip(rstds, dnhs, nhs, means)]
        dw = jnp.zeros((1, HEAD_DIM_A), F32)
        for sl, nh, sg, dt in zip(hsl, nhs, sgs, dts):
            zh = z_ref[:, sl]
            dz_ref[:, sl] = (dog_ref[:, sl] * nh * w_ref[...] * (sg * (1.0 + zh * (1.0 - sg)))).astype(BF16)
            dw = dw + jnp.sum(dt * nh, axis=0, keepdims=True)

        @pl.when(n == 0)
        def _():
            dw_ref[...] = dw

        @pl.when(n > 0)
        def _():
            dw_ref[...] += dw

        def chunk(qs, ks, vs, gs, bs, ss):
            return _delta_step(ss, *_delta_prep(qs, ks, vs, gs, bs, hints=_heads(t_ref))[0])

        _, vjp = jax.vjp(chunk, *[_heads(r) for r in (q_ref, k_ref, v_ref, g_ref, b_ref, st_ref)])
        grads = vjp((dos, [ds_ref[j] for j in range(N_HEADS_A)]))
        for j, sl in enumerate(_head_slices()):
            ds_ref[j] = grads[5][j]
            for ref, vals in zip((dq_ref, dk_ref, dv_ref, dg_ref, db_ref), grads[:5]):
                ref[:, sl] = vals[j]

    spec = pl.BlockSpec((CHUNK, d), lambda n: (nc - 1 - n, 0))
    vec = pl.BlockSpec((1, HEAD_DIM_A), lambda n: (0, 0))
    f32 = jax.ShapeDtypeStruct((t, d), F32)
    return pl.pallas_call(
        body,
        name=name,
        grid=(nc,),
        in_specs=[spec] * 8 + [pl.BlockSpec((CHUNK, d), lambda n: (nc - 1 - n, Z_COL_BLOCK)), vec, spec],
        out_specs=[spec] * 6 + [vec],
        out_shape=[f32] * 5 + [jax.ShapeDtypeStruct((t, d), BF16), jax.ShapeDtypeStruct((1, HEAD_DIM_A), F32)],
        scratch_shapes=[pltpu.VMEM((N_HEADS_A, HEAD_DIM_A, HEAD_DIM_A), F32)],
        compiler_params=_params(("arbitrary",)),
    )(q, k, v, gbc, bbc, tinv, states, o, proj, norm_w, dog)


def _alibi_slope(h):
    return 2.0 ** (-8.0 * (h + 1) / N_HEADS_B)


def _swa_load(sink_ref, q_ref, kp_ref, kc_ref, vp_ref, vc_ref):
    rg = lax.broadcasted_iota(jnp.int32, (GROUP_B * WINDOW, 1), 0) // WINDOW
    q4s, kcats, vcats, slopes, sinkcols = [], [], [], [], []
    for hk in range(N_KV_B):
        ks = slice(hk * HEAD_DIM_B, (hk + 1) * HEAD_DIM_B)
        heads = [hk * GROUP_B + g for g in range(GROUP_B)]
        q4s.append(jnp.concatenate([q_ref[:, h * HEAD_DIM_B:(h + 1) * HEAD_DIM_B] for h in heads], axis=0).astype(BF16))
        kcats.append(jnp.concatenate([kp_ref[:, ks], kc_ref[:, ks]], axis=0).astype(BF16))
        vcats.append(jnp.concatenate([vp_ref[:, ks], vc_ref[:, ks]], axis=0).astype(BF16))
        slope = jnp.zeros((GROUP_B * WINDOW, 1), F32)
        sink = jnp.zeros((GROUP_B * WINDOW, 1), F32)
        for g, h in enumerate(heads):
            slope = jnp.where(rg == g, _alibi_slope(h), slope)
            sink = jnp.where(rg == g, sink_ref[0, h], sink)
        slopes.append(slope)
        sinkcols.append(sink)
    return q4s, kcats, vcats, slopes, sinkcols


def _swa_probs(q4s, kcats, slopes, sinkcols, blk):
    rows = GROUP_B * WINDOW
    qi = lax.broadcasted_iota(jnp.int32, (rows, 2 * WINDOW), 0) % WINDOW
    kj = lax.broadcasted_iota(jnp.int32, (rows, 2 * WINDOW), 1)
    dist = qi + WINDOW - kj
    valid = (dist >= 0) & (dist < WINDOW) & (blk * WINDOW - WINDOW + kj >= 0)
    distf = dist.astype(F32)
    ss = [lax.dot_general(q, kc, _NT, preferred_element_type=F32) for q, kc in zip(q4s, kcats)]
    logits = [jnp.where(valid, s * (HEAD_DIM_B ** -0.5) - sl * distf, NEG_BIG) for s, sl in zip(ss, slopes)]
    ms = [jnp.maximum(jnp.max(l, axis=-1, keepdims=True), sk) for l, sk in zip(logits, sinkcols)]
    es = [jnp.exp(l - m) for l, m in zip(logits, ms)]
    esk = [jnp.exp(sk - m) for sk, m in zip(sinkcols, ms)]
    invs = [1.0 / (jnp.sum(e, axis=-1, keepdims=True) + k) for e, k in zip(es, esk)]
    return [e * i for e, i in zip(es, invs)], [k * i for k, i in zip(esk, invs)]


def _swa_fwd(proj, sinks, *, name):
    t = proj.shape[0]
    nb = t // WINDOW
    qd = N_HEADS_B * HEAD_DIM_B
    kd = N_KV_B * HEAD_DIM_B

    def body(sink_ref, q_ref, kp_ref, kc_ref, vp_ref, vc_ref, o_ref):
        blk = pl.program_id(0)
        q4s, kcats, vcats, slopes, sinkcols = _swa_load(sink_ref, q_ref, kp_ref, kc_ref, vp_ref, vc_ref)
        ps, _ = _swa_probs(q4s, kcats, slopes, sinkcols, blk)
        outs = [jnp.dot(p.astype(BF16), vc, preferred_element_type=F32) for p, vc in zip(ps, vcats)]
        for hk, out in enumerate(outs):
            for g in range(GROUP_B):
                h = hk * GROUP_B + g
                o_ref[:, h * HEAD_DIM_B:(h + 1) * HEAD_DIM_B] = out[g * WINDOW:(g + 1) * WINDOW, :].astype(BF16)

    q_spec = pl.BlockSpec((WINDOW, qd), lambda i: (i, 0))
    kv = lambda col, prev: pl.BlockSpec((WINDOW, kd), (lambda i: (jnp.maximum(i - 1, 0), col)) if prev else (lambda i: (i, col)))
    kcol, vcol = qd // kd, qd // kd + 1
    return pl.pallas_call(
        body,
        name=name,
        grid=(nb,),
        in_specs=[pl.BlockSpec(memory_space=pltpu.SMEM), q_spec, kv(kcol, True), kv(kcol, False), kv(vcol, True), kv(vcol, False)],
        out_specs=q_spec,
        out_shape=jax.ShapeDtypeStruct((t, qd), BF16),
        compiler_params=_params(("parallel",)),
    )(sinks, proj, proj, proj, proj, proj)


def _swa_bwd(proj, sinks, dout, *, name):
    t = proj.shape[0]
    nb = t // WINDOW
    qd = N_HEADS_B * HEAD_DIM_B
    kd = N_KV_B * HEAD_DIM_B
    scale = HEAD_DIM_B ** -0.5

    def body(sink_ref, q_ref, kp_ref, kc_ref, vp_ref, vc_ref, do_ref, dq_ref, dk_ref, dv_ref, dsk_ref):
        blk = pl.program_id(0)
        lane = lax.broadcasted_iota(jnp.int32, (1, LANES), 1)

        @pl.when(blk == 0)
        def _():
            dk_ref[...] = jnp.zeros((t, kd), F32)
            dv_ref[...] = jnp.zeros((t, kd), F32)
            dsk_ref[...] = jnp.zeros((1, LANES), F32)

        cur = pl.ds(pl.multiple_of(blk * WINDOW, WINDOW), WINDOW)
        prv = pl.ds(pl.multiple_of(jnp.maximum(blk - 1, 0) * WINDOW, WINDOW), WINDOW)
        q4s, kcats, vcats, slopes, sinkcols = _swa_load(sink_ref, q_ref, kp_ref, kc_ref, vp_ref, vc_ref)
        ps, psinks = _swa_probs(q4s, kcats, slopes, sinkcols, blk)
        do4s = [jnp.concatenate([do_ref[:, (hk * GROUP_B + g) * HEAD_DIM_B:(hk * GROUP_B + g + 1) * HEAD_DIM_B]
                                 for g in range(GROUP_B)], axis=0).astype(BF16) for hk in range(N_KV_B)]
        dps = [lax.dot_general(d, vc, _NT, preferred_element_type=F32) for d, vc in zip(do4s, vcats)]
        deltas = [jnp.sum(p * dp, axis=-1, keepdims=True) for p, dp in zip(ps, dps)]
        dss = [(p * (dp - dl) * scale).astype(BF16) for p, dp, dl in zip(ps, dps, deltas)]
        dq4s = [jnp.dot(ds, kc, preferred_element_type=F32) for ds, kc in zip(dss, kcats)]
        dkcs = [lax.dot_general(ds, q, _TN, preferred_element_type=F32) for ds, q in zip(dss, q4s)]
        dvcs = [lax.dot_general(p.astype(BF16), d, _TN, preferred_element_type=F32) for p, d in zip(ps, do4s)]
        dsk = jnp.zeros((1, LANES), F32)
        for hk in range(N_KV_B):
            ks = slice(hk * HEAD_DIM_B, (hk + 1) * HEAD_DIM_B)
            dsink = -psinks[hk] * deltas[hk]
            for g in range(GROUP_B):
                h = hk * GROUP_B + g
                rows = slice(g * WINDOW, (g + 1) * WINDOW)
                dq_ref[:, h * HEAD_DIM_B:(h + 1) * HEAD_DIM_B] = dq4s[hk][rows, :].astype(BF16)
                dsk = dsk + jnp.where(lane == h, jnp.sum(dsink[rows, :], axis=0, keepdims=True), 0.0)
            dk_ref[cur, ks] += dkcs[hk][WINDOW:, :]
            dv_ref[cur, ks] += dvcs[hk][WINDOW:, :]

            @pl.when(blk > 0)
            def _():
                dk_ref[prv, ks] += dkcs[hk][:WINDOW, :]
                dv_ref[prv, ks] += dvcs[hk][:WINDOW, :]

        dsk_ref[...] += dsk

    q_spec = pl.BlockSpec((WINDOW, qd), lambda i: (i, 0))
    kv = lambda col, prev: pl.BlockSpec((WINDOW, kd), (lambda i: (jnp.maximum(i - 1, 0), col)) if prev else (lambda i: (i, col)))
    kcol, vcol = qd // kd, qd // kd + 1
    full = pl.BlockSpec((t, kd), lambda i: (0, 0))
    return pl.pallas_call(
        body,
        name=name,
        grid=(nb,),
        in_specs=[pl.BlockSpec(memory_space=pltpu.SMEM), q_spec, kv(kcol, True), kv(kcol, False), kv(vcol, True), kv(vcol, False), q_spec],
        out_specs=[q_spec, full, full, pl.BlockSpec((1, LANES), lambda i: (0, 0))],
        out_shape=[jax.ShapeDtypeStruct((t, qd), BF16), jax.ShapeDtypeStruct((t, kd), F32),
                   jax.ShapeDtypeStruct((t, kd), F32), jax.ShapeDtypeStruct((1, LANES), F32)],
        compiler_params=_params(("arbitrary",)),
    )(sinks, proj, proj, proj, proj, proj, dout)


def _ffn_act_fwd(up, cw, *, name, tm=512, cb=256):
    _, t, f = up.shape
    tm, cb = _tile(t, tm), _tile(f, cb)

    rc = min(ROW_CHUNK, tm)

    def body(ug_ref, uv_ref, cg_ref, cv_ref, a_ref, hg, hv):
        i = pl.program_id(1)

        @pl.when(i == 0)
        def _():
            hg[0:HALO, :] = jnp.zeros((HALO, cb), F32)
            hv[0:HALO, :] = jnp.zeros((HALO, cb), F32)

        hg[pl.ds(HALO, rc), :] = ug_ref[0:rc, :]
        hv[pl.ds(HALO, rc), :] = uv_ref[0:rc, :]
        for r in range(tm // rc):
            if r == 0:
                yg = _conv_rows(hg, HALO, rc, cg_ref, FFN_CONV)
                yv = _conv_rows(hv, HALO, rc, cv_ref, FFN_CONV)
            else:
                yg = _conv_rows(ug_ref, r * rc, rc, cg_ref, FFN_CONV)
                yv = _conv_rows(uv_ref, r * rc, rc, cv_ref, FFN_CONV)
            a_ref[r * rc:(r + 1) * rc, :] = (yg * _sigmoid(yg) * yv).astype(BF16)
        hg[0:HALO, :] = ug_ref[tm - HALO:tm, :]
        hv[0:HALO, :] = uv_ref[tm - HALO:tm, :]

    ncb = f // cb
    half = lambda s: pl.BlockSpec((None, tm, cb), lambda c, i: (s, i, c))
    taps = lambda s: pl.BlockSpec((FFN_CONV, cb), lambda c, i: (0, c + s * ncb))
    return pl.pallas_call(
        body,
        name=name,
        grid=(ncb, t // tm),
        in_specs=[half(0), half(1), taps(0), taps(1)],
        out_specs=pl.BlockSpec((tm, cb), lambda c, i: (i, c)),
        out_shape=jax.ShapeDtypeStruct((t, f), BF16),
        scratch_shapes=[pltpu.VMEM((HALO + rc, cb), F32)] * 2,
        compiler_params=_params(("parallel", "arbitrary")),
    )(up, up, cw, cw)


def _ffn_act_bwd(up, cw, dact, *, name, tm=512, cb=256):
    _, t, f = up.shape
    tm, cb = _tile(t, tm), _tile(f, cb)
    nt = t // tm
    hb = tm // HALO

    rc = min(ROW_CHUNK, tm)
    nr = tm // rc
    kw = FFN_CONV

    def body(ug_ref, uv_ref, pg_ref, pv_ref, cg_ref, cv_ref, da_ref, du_ref, dcg_ref, dcv_ref,
             hg, hv, dyg, dyv):
        i = pl.program_id(1)
        first = i == 0
        tile = nt - 1 - i

        @pl.when(tile == 0)
        def _():
            hg[0:HALO, :] = jnp.zeros((HALO, cb), F32)
            hv[0:HALO, :] = jnp.zeros((HALO, cb), F32)

        @pl.when(tile > 0)
        def _():
            hg[0:HALO, :] = pg_ref[...]
            hv[0:HALO, :] = pv_ref[...]

        @pl.when(first)
        def _():
            dyg[pl.ds(tm, HALO), :] = jnp.zeros((HALO, cb), F32)
            dyv[pl.ds(tm, HALO), :] = jnp.zeros((HALO, cb), F32)

        hg[pl.ds(HALO, rc), :] = ug_ref[0:rc, :]
        hv[pl.ds(HALO, rc), :] = uv_ref[0:rc, :]
        dcg = [jnp.zeros((1, cb), F32) for _ in range(kw)]
        dcv = [jnp.zeros((1, cb), F32) for _ in range(kw)]
        for r in reversed(range(nr)):
            rows = slice(r * rc, (r + 1) * rc)
            src_g, src_v, base = (hg, hv, HALO) if r == 0 else (ug_ref, uv_ref, r * rc)
            yg = _conv_rows(src_g, base, rc, cg_ref, kw)
            yv = _conv_rows(src_v, base, rc, cv_ref, kw)
            sg = _sigmoid(yg)
            da = da_ref[rows, :]
            dy_g = da * yv * (sg * (1.0 + yg * (1.0 - sg)))
            dy_v = da * (yg * sg)
            dyg[rows, :] = dy_g
            dyv[rows, :] = dy_v
            du_ref[0, rows, :] = _conv_t_rows(dyg, r * rc, rc, cg_ref, kw).astype(BF16)
            du_ref[1, rows, :] = _conv_t_rows(dyv, r * rc, rc, cv_ref, kw).astype(BF16)
            for j in range(kw):
                dcg[j] = dcg[j] + jnp.sum(dy_g * src_g[pl.ds(base - (kw - 1) + j, rc), :], axis=0, keepdims=True)
                dcv[j] = dcv[j] + jnp.sum(dy_v * src_v[pl.ds(base - (kw - 1) + j, rc), :], axis=0, keepdims=True)
        dyg[pl.ds(tm, HALO), :] = dyg[0:HALO, :]
        dyv[pl.ds(tm, HALO), :] = dyv[0:HALO, :]
        for j in range(kw):
            @pl.when(first)
            def _():
                dcg_ref[j:j + 1, :] = dcg[j]
                dcv_ref[j:j + 1, :] = dcv[j]

            @pl.when(jnp.logical_not(first))
            def _():
                dcg_ref[j:j + 1, :] += dcg[j]
                dcv_ref[j:j + 1, :] += dcv[j]

    ncb = f // cb
    half = lambda s: pl.BlockSpec((None, tm, cb), lambda c, i: (s, nt - 1 - i, c))
    prev = lambda s: pl.BlockSpec((None, HALO, cb), lambda c, i: (s, jnp.maximum((nt - 1 - i) * hb - 1, 0), c))
    taps = lambda s: pl.BlockSpec((FFN_CONV, cb), lambda c, i: (0, c + s * ncb))
    dtaps = pl.BlockSpec((FFN_CONV, cb), lambda c, i: (0, c))
    return pl.pallas_call(
        body,
        name=name,
        grid=(ncb, nt),
        in_specs=[half(0), half(1), prev(0), prev(1), taps(0), taps(1), pl.BlockSpec((tm, cb), lambda c, i: (nt - 1 - i, c))],
        out_specs=[pl.BlockSpec((2, tm, cb), lambda c, i: (0, nt - 1 - i, c)), dtaps, dtaps],
        out_shape=[jax.ShapeDtypeStruct((2, t, f), BF16), jax.ShapeDtypeStruct((FFN_CONV, f), F32),
                   jax.ShapeDtypeStruct((FFN_CONV, f), F32)],
        scratch_shapes=[pltpu.VMEM((HALO + rc, cb), F32)] * 2 + [pltpu.VMEM((tm + HALO, cb), F32)] * 2,
        compiler_params=_params(("parallel", "arbitrary")),
    )(up, up, up, up, cw, cw, dact)


FFN_COL_TILE = 1408
FFN_SUB = 512
FFN_ROW_CHUNK = 16


def _sub_blocks(width):
    return [slice(c, min(c + FFN_SUB, width)) for c in range(0, width, FFN_SUB)]


def _ffn_up_act(n_f, w_up_t, cw, *, name, tm=512):
    t, d = n_f.shape
    f = w_up_t.shape[1]
    tm, tn = _tile(t, tm), _tile(f, FFN_COL_TILE)
    nj = f // tn
    rc = min(FFN_ROW_CHUNK, tm)
    kw = FFN_CONV

    def body(n_ref, wg_ref, wv_ref, cg_ref, cv_ref, up_ref, y_ref, a_ref, hg, hv):
        i = pl.program_id(1)

        @pl.when(i == 0)
        def _():
            hg[0:HALO, :] = jnp.zeros((HALO, tn), F32)
            hv[0:HALO, :] = jnp.zeros((HALO, tn), F32)

        def products(cs):
            up_ref[0, :, cs] = lax.dot_general(n_ref[...], wg_ref[cs, :], _NT, preferred_element_type=F32)
            up_ref[1, :, cs] = lax.dot_general(n_ref[...], wv_ref[cs, :], _NT, preferred_element_type=F32)

        subs = _sub_blocks(tn)
        ug, uv = up_ref.at[0], up_ref.at[1]
        products(subs[0])
        for ci, cs in enumerate(subs):
            if ci + 1 < len(subs):
                products(subs[ci + 1])
            hg[pl.ds(HALO, rc), cs] = ug[0:rc, cs]
            hv[pl.ds(HALO, rc), cs] = uv[0:rc, cs]
            for r in range(tm // rc):
                src_g, src_v, base = (hg, hv, HALO) if r == 0 else (ug, uv, r * rc)
                yg = _conv_rows(src_g, base, rc, cg_ref, kw, cs)
                yv = _conv_rows(src_v, base, rc, cv_ref, kw, cs)
                y_ref[0, r * rc:(r + 1) * rc, cs] = yg
                y_ref[1, r * rc:(r + 1) * rc, cs] = yv
                a_ref[r * rc:(r + 1) * rc, cs] = (yg * _sigmoid(yg) * yv).astype(BF16)
            hg[0:HALO, cs] = ug[tm - HALO:tm, cs]
            hv[0:HALO, cs] = uv[tm - HALO:tm, cs]

    half = lambda s: pl.BlockSpec((None, tn, d), lambda j, i: (s, j, 0))
    taps = lambda s: pl.BlockSpec((kw, tn), lambda j, i: (0, j + s * nj))
    pair = pl.BlockSpec((2, tm, tn), lambda j, i: (0, i, j))
    return pl.pallas_call(
        body,
        name=name,
        grid=(nj, t // tm),
        in_specs=[pl.BlockSpec((tm, d), lambda j, i: (i, 0)), half(0), half(1), taps(0), taps(1)],
        out_specs=[pair, pair, pl.BlockSpec((tm, tn), lambda j, i: (i, j))],
        out_shape=[jax.ShapeDtypeStruct((2, t, f), F32), jax.ShapeDtypeStruct((2, t, f), F32),
                   jax.ShapeDtypeStruct((t, f), BF16)],
        scratch_shapes=[pltpu.VMEM((HALO + rc, tn), F32)] * 2,
        compiler_params=_params(("parallel", "arbitrary")),
    )(n_f, w_up_t, w_up_t, cw, cw)


def _ffn_down_dx_act_bwd(dh, w_down, up, y, cw, *, name, tm=512):
    t, d = dh.shape
    f = w_down.shape[0]
    tm, tn = _tile(t, tm), _tile(f, FFN_COL_TILE)
    nj, nt = f // tn, t // tm
    rc = min(FFN_ROW_CHUNK, tm)
    nr = tm // rc
    kw = FFN_CONV

    def body(dh_ref, wd_ref, ug_ref, uv_ref, yg_ref, yv_ref, cg_ref, cv_ref, du_ref, dcg_ref, dcv_ref,
             dyg, dyv, da_s, dh_s):
        i = pl.program_id(1)
        first = i == 0

        @pl.when(first)
        def _():
            dyg[pl.ds(tm, HALO), :] = jnp.zeros((HALO, tn), F32)
            dyv[pl.ds(tm, HALO), :] = jnp.zeros((HALO, tn), F32)

        dh_s[...] = dh_ref[...].astype(BF16)

        def product(cs):
            da_s[:, cs] = lax.dot_general(dh_s[...], wd_ref[cs, :], _NT, preferred_element_type=F32)

        subs = _sub_blocks(tn)
        product(subs[0])
        for ci, cs in enumerate(subs):
            width = cs.stop - cs.start
            if ci + 1 < len(subs):
                product(subs[ci + 1])
            dcg = [jnp.zeros((HALO, width), F32) for _ in range(kw)]
            dcv = [jnp.zeros((HALO, width), F32) for _ in range(kw)]
            for r in reversed(range(nr)):
                rows = slice(r * rc, (r + 1) * rc)
                yg, yv = yg_ref[rows, cs], yv_ref[rows, cs]
                sg = _sigmoid(yg)
                da = da_s[rows, cs]
                dyg[rows, cs] = da * yv * (sg * (1.0 + yg * (1.0 - sg)))
                dyv[rows, cs] = da * (yg * sg)
                ahead_g = _ahead_rows(dyg, r * rc, rc, range(kw), cs)
                ahead_v = _ahead_rows(dyv, r * rc, rc, range(kw), cs)
                du_ref[0, rows, cs] = sum(cg_ref[kw - 1 - s:kw - s, cs] * ahead_g[s] for s in range(kw)).astype(BF16)
                du_ref[1, rows, cs] = sum(cv_ref[kw - 1 - s:kw - s, cs] * ahead_v[s] for s in range(kw)).astype(BF16)
                xg, xv = ug_ref[rows, cs], uv_ref[rows, cs]
                for s in range(kw):
                    dcg[kw - 1 - s] = dcg[kw - 1 - s] + _fold_rows(xg * ahead_g[s])
                    dcv[kw - 1 - s] = dcv[kw - 1 - s] + _fold_rows(xv * ahead_v[s])
            dyg[pl.ds(tm, HALO), cs] = dyg[0:HALO, cs]
            dyv[pl.ds(tm, HALO), cs] = dyv[0:HALO, cs]
            for j in range(kw):
                tg = jnp.sum(dcg[j], axis=0, keepdims=True)
                tv = jnp.sum(dcv[j], axis=0, keepdims=True)

                @pl.when(first)
                def _():
                    dcg_ref[j:j + 1, cs] = tg
                    dcv_ref[j:j + 1, cs] = tv

                @pl.when(jnp.logical_not(first))
                def _():
                    dcg_ref[j:j + 1, cs] += tg
                    dcv_ref[j:j + 1, cs] += tv

    half = lambda s: pl.BlockSpec((None, tm, tn), lambda j, i: (s, nt - 1 - i, j))
    taps = lambda s: pl.BlockSpec((kw, tn), lambda j, i: (0, j + s * nj))
    dtaps = pl.BlockSpec((kw, tn), lambda j, i: (0, j))
    return pl.pallas_call(
        body,
        name=name,
        grid=(nj, nt),
        in_specs=[pl.BlockSpec((tm, d), lambda j, i: (nt - 1 - i, 0)), pl.BlockSpec((tn, d), lambda j, i: (j, 0)),
                  half(0), half(1), half(0), half(1), taps(0), taps(1)],
        out_specs=[pl.BlockSpec((2, tm, tn), lambda j, i: (0, nt - 1 - i, j)), dtaps, dtaps],
        out_shape=[jax.ShapeDtypeStruct((2, t, f), BF16), jax.ShapeDtypeStruct((kw, f), F32),
                   jax.ShapeDtypeStruct((kw, f), F32)],
        scratch_shapes=[pltpu.VMEM((tm + HALO, tn), F32)] * 2 + [pltpu.VMEM((tm, tn), F32), pltpu.VMEM((tm, d), BF16)],
        compiler_params=_params(("parallel", "arbitrary")),
    )(dh, w_down, up, up, y, y, cw, cw)


def _ple_fwd(h, zg, pe, *, name, tm=512):
    t, d = h.shape
    tm = _tile(t, tm)

    def body(h_ref, z_ref, p_ref, o_ref):
        o_ref[...] = h_ref[...] + _sigmoid(z_ref[...]) * p_ref[...]

    row = pl.BlockSpec((tm, d), lambda i: (i, 0))
    return pl.pallas_call(
        body, name=name, grid=(t // tm,), in_specs=[row] * 3, out_specs=row,
        out_shape=jax.ShapeDtypeStruct((t, d), F32), compiler_params=_params(("parallel",)),
    )(h, zg, pe)


def _ple_bwd(dh, zg, pe, *, name, tm=512):
    t, d = dh.shape
    tm = _tile(t, tm)

    def body(g_ref, z_ref, p_ref, dz_ref, dp_ref):
        g = g_ref[...]
        sg = _sigmoid(z_ref[...])
        dz_ref[...] = (g * p_ref[...] * sg * (1.0 - sg)).astype(BF16)
        dp_ref[...] = (g * sg).astype(BF16)

    row = pl.BlockSpec((tm, d), lambda i: (i, 0))
    return pl.pallas_call(
        body, name=name, grid=(t // tm,), in_specs=[row] * 3, out_specs=[row] * 2,
        out_shape=[jax.ShapeDtypeStruct((t, d), BF16)] * 2, compiler_params=_params(("parallel",)),
    )(dh, zg, pe)


def _my_pos():
    return lax.axis_index("x"), lax.axis_index("y"), lax.axis_index("c")


def _all_gather(block, *, name, dep=None):
    r, w = block.shape
    has_dep = dep is not None

    def body(*refs):
        x_ref, out_ref, send_sems, recv_sems, local_sem = refs[:1] + refs[1 + has_dep:]
        x, y, c = _my_pos()
        me, sibling = (x, y, c), (x, y, 1 - c)
        chips = [(1 - x, y), (x, 1 - y), (1 - x, 1 - y)]

        def slot(px, py, pc):
            return out_ref.at[4 * px + 2 * py + pc]

        def copy(k, blk, to, src=None):
            return pltpu.make_async_remote_copy(
                src_ref=slot(*blk) if src is None else src, dst_ref=slot(*blk),
                send_sem=send_sems.at[k], recv_sem=recv_sems.at[k],
                device_id=to, device_id_type=pl.DeviceIdType.MESH)

        mine = pltpu.make_async_copy(x_ref, slot(*me), local_sem)
        mine.start()
        first = [copy(0, me, sibling, src=x_ref)]
        first += [copy(1 + j, me, (*chip, c), src=x_ref) for j, chip in enumerate(chips)]
        for cp in first:
            cp.start()
        passed = [copy(4 + j, (*chip, c), sibling) for j, chip in enumerate(chips)]
        for j, chip in enumerate(chips):
            copy(1 + j, (*chip, c), me).wait_recv()
            passed[j].start()
        copy(0, sibling, me).wait_recv()
        for j, chip in enumerate(chips):
            copy(4 + j, (*chip, 1 - c), me).wait_recv()
        for cp in first + passed:
            cp.wait_send()
        mine.wait()

    return pl.pallas_call(
        body,
        name=name,
        out_shape=jax.ShapeDtypeStruct((N_DEV, r, w), block.dtype),
        in_specs=[pl.BlockSpec(memory_space=pl.ANY)] * (1 + has_dep),
        out_specs=pl.BlockSpec(memory_space=pl.ANY),
        scratch_shapes=[pltpu.SemaphoreType.DMA((7,)), pltpu.SemaphoreType.DMA((7,)), pltpu.SemaphoreType.DMA],
    )(*((block, dep) if has_dep else (block,)))


def _all_to_all(slabs, *, name):
    n, r, w = slabs.shape

    def body(x_ref, out_ref, send_sems, recv_sems, local_sem):
        x, y, c = _my_pos()
        my_idx = 4 * x + 2 * y + c
        mine = pltpu.make_async_copy(x_ref.at[my_idx], out_ref.at[my_idx], local_sem)
        mine.start()
        copies = []
        for k in range(1, N_DEV):
            fx, fy, fc = (k >> 2) & 1, (k >> 1) & 1, k & 1
            px = (1 - x) if fx else x
            py = (1 - y) if fy else y
            pc = (1 - c) if fc else c
            cp = pltpu.make_async_remote_copy(
                src_ref=x_ref.at[4 * px + 2 * py + pc], dst_ref=out_ref.at[my_idx],
                send_sem=send_sems.at[k - 1], recv_sem=recv_sems.at[k - 1],
                device_id=(px, py, pc), device_id_type=pl.DeviceIdType.MESH)
            cp.start()
            copies.append(cp)
        for cp in copies:
            cp.wait_recv()
        for cp in copies:
            cp.wait_send()
        mine.wait()

    return pl.pallas_call(
        body,
        name=name,
        out_shape=jax.ShapeDtypeStruct((n, r, w), slabs.dtype),
        in_specs=[pl.BlockSpec(memory_space=pl.ANY)],
        out_specs=pl.BlockSpec(memory_space=pl.ANY),
        scratch_shapes=[pltpu.SemaphoreType.DMA((7,)), pltpu.SemaphoreType.DMA((7,)), pltpu.SemaphoreType.DMA],
    )(slabs)


def _exchange_copies(scatter, src_refs, land_refs, send_sems, recv_sems, local_sems):
    x, y, c = _my_pos()
    me = 4 * x + 2 * y + c
    local, remote = [], []
    for i, (s, l) in enumerate(zip(src_refs, land_refs)):
        local.append(pltpu.make_async_copy(s.at[me] if scatter else s, l.at[me], local_sems.at[i]))
        for k in range(1, N_DEV):
            px = (1 - x) if (k >> 2) & 1 else x
            py = (1 - y) if (k >> 1) & 1 else y
            pc = (1 - c) if k & 1 else c
            remote.append(pltpu.make_async_remote_copy(
                src_ref=s.at[4 * px + 2 * py + pc] if scatter else s, dst_ref=l.at[me],
                send_sem=send_sems.at[(N_DEV - 1) * i + k - 1], recv_sem=recv_sems.at[(N_DEV - 1) * i + k - 1],
                device_id=(px, py, pc), device_id_type=pl.DeviceIdType.MESH))
    return local, remote


def _exchange(arrays, *, scatter, name):
    n = len(arrays)

    def body(*refs):
        srcs, lands = refs[:n], refs[n:2 * n]
        local, remote = _exchange_copies(scatter, srcs, lands, *refs[2 * n:])
        for cp in local + remote:
            cp.start()
        for cp in remote:
            cp.wait_recv()
        for cp in remote:
            cp.wait_send()
        for cp in local:
            cp.wait()

    hbm = pl.BlockSpec(memory_space=pl.ANY)
    out = pl.pallas_call(
        body,
        name=name,
        out_shape=[jax.ShapeDtypeStruct(a.shape if scatter else (N_DEV,) + a.shape, a.dtype) for a in arrays],
        in_specs=[hbm] * n,
        out_specs=[hbm] * n,
        scratch_shapes=[pltpu.SemaphoreType.DMA(((N_DEV - 1) * n,)), pltpu.SemaphoreType.DMA(((N_DEV - 1) * n,)),
                        pltpu.SemaphoreType.DMA((n,))],
    )(*arrays)
    return list(out)


_HBM_SPEC = pl.BlockSpec(memory_space=pltpu.HBM)
_SEM_SPEC = pl.BlockSpec(memory_space=pltpu.SEMAPHORE)
_EFFECT = pltpu.SideEffectType.DATAFLOW_SIDE_EFFECTING


def _exchange_start(arrays, *, scatter, name, dep):
    n = len(arrays)
    srcs = [pltpu.with_memory_space_constraint(a, pltpu.HBM) for a in arrays]
    lands = [pltpu.with_memory_space_constraint(lax.empty(a.shape if scatter else (N_DEV,) + a.shape, a.dtype), pltpu.HBM)
             for a in arrays]

    def body(*refs):
        src_refs, land_refs = refs[:n], refs[n:2 * n]
        send_sems, recv_sems, local_sems = refs[2 * n + 1:2 * n + 4]
        token = refs[-1]
        local, remote = _exchange_copies(scatter, src_refs, land_refs, send_sems, recv_sems, local_sems)
        for cp in local + remote:
            cp.start()
        token[...] = jnp.zeros_like(token)

    sems = (pltpu.SemaphoreType.DMA(((N_DEV - 1) * n,)), pltpu.SemaphoreType.DMA(((N_DEV - 1) * n,)),
            pltpu.SemaphoreType.DMA((n,)))
    out = pl.pallas_call(
        body,
        name=name,
        out_shape=sems + tuple(pltpu.HBM(a.shape, a.dtype) for a in srcs + lands) + (jax.ShapeDtypeStruct((8, LANES), F32),),
        in_specs=[_HBM_SPEC] * (2 * n) + [pl.BlockSpec(memory_space=pl.ANY)],
        out_specs=(_SEM_SPEC,) * 3 + (_HBM_SPEC,) * (2 * n) + (pl.BlockSpec(memory_space=pltpu.VMEM),),
        input_output_aliases={i: 3 + i for i in range(2 * n)},
        compiler_params=pltpu.CompilerParams(has_side_effects=_EFFECT),
    )(*srcs, *lands, dep)
    return (out[:3], list(out[3:3 + n]), list(out[3 + n:3 + 2 * n])), out[-1]


def _exchange_wait(handle, after, *, scatter, name):
    sems, srcs, lands = handle
    n = len(srcs)

    def body(*refs):
        src_refs, land_refs = refs[:n], refs[n:2 * n]
        send_sems, recv_sems, local_sems = refs[2 * n:2 * n + 3]
        local, remote = _exchange_copies(scatter, src_refs, land_refs, send_sems, recv_sems, local_sems)
        for cp in remote:
            cp.wait_send()
            cp.wait_recv()
        for cp in local:
            cp.wait()

    out = pl.pallas_call(
        body,
        name=name,
        out_shape=tuple(pltpu.HBM(a.shape, a.dtype) for a in srcs + lands),
        in_specs=[_HBM_SPEC] * (2 * n) + [_SEM_SPEC] * 3 + [pl.BlockSpec(memory_space=pl.ANY)],
        out_specs=(_HBM_SPEC,) * (2 * n),
        input_output_aliases={i: i for i in range(2 * n)},
        compiler_params=pltpu.CompilerParams(has_side_effects=_EFFECT),
    )(*srcs, *lands, *sems, after)
    return list(out[n:])


def _sum_parts(parts, *, name, tr=512):
    n, r, lanes = parts.shape
    tr = tr if (r % tr == 0 and r > 1024) else r

    def body(p_ref, g_ref):
        g = p_ref[0].astype(F32)
        for j in range(1, n):
            g = g + p_ref[j].astype(F32)
        g_ref[...] = g

    row = pl.BlockSpec((tr, lanes), lambda i: (i, 0))
    return pl.pallas_call(
        body,
        name=name,
        grid=(r // tr,),
        in_specs=[pl.BlockSpec((n, tr, lanes), lambda i: (0, i, 0))],
        out_specs=row,
        out_shape=jax.ShapeDtypeStruct((r, lanes), F32),
        compiler_params=_params(("parallel",)),
    )(parts)


def _adamw_update(g, w, m, v):
    c1 = 1.0 / (1.0 - ADAM_B1 ** ADAM_STEP)
    c2 = 1.0 / (1.0 - ADAM_B2 ** ADAM_STEP)
    nm = ADAM_B1 * m + (1.0 - ADAM_B1) * g
    nv = ADAM_B2 * v + (1.0 - ADAM_B2) * (g * g)
    return -ADAM_LR * ((nm * c1) / (jnp.sqrt(nv * c2) + ADAM_EPS) + ADAM_WD * w), nm, nv


def _adamw_layer(g, w, m, v, layer, prev, *, name):
    nl, k, n = w.shape
    tr = max([d for d in range(8, min(k, 256) + 1, 8) if k % d == 0] or [k])
    in_parts = g.ndim == 3

    def body(g_ref, w_ref, m_ref, v_ref, *rest):
        go_ref, d_ref, nm_ref, nv_ref = rest[-4:]
        if in_parts:
            gg = g_ref[0].astype(F32)
            for j in range(1, g_ref.shape[0]):
                gg = gg + g_ref[j].astype(F32)
        else:
            gg = g_ref[...]
        d, nm, nv = _adamw_update(gg, w_ref[...], m_ref[...], v_ref[...])
        go_ref[...] = gg
        d_ref[...] = d
        nm_ref[...] = nm
        nv_ref[...] = nv

    lay = pl.BlockSpec((None, tr, n), lambda i: (layer, i, 0))
    n_prev = 0 if prev is None else 4
    out = jax.ShapeDtypeStruct((nl, k, n), F32)
    return pl.pallas_call(
        body,
        name=name,
        grid=(k // tr,),
        in_specs=[pl.BlockSpec((g.shape[0], tr, n), lambda i: (0, i, 0)) if in_parts else pl.BlockSpec((tr, n), lambda i: (i, 0)),
                  lay, lay, lay] + [pl.BlockSpec(memory_space=pl.ANY)] * n_prev,
        out_specs=[lay] * 4,
        out_shape=[out] * 4,
        input_output_aliases={4 + j: j for j in range(n_prev)},
        compiler_params=_params(("parallel",)),
    )(g, w, m, v, *(prev or ()))


def _adamw_packed(g, w, m, v, *, name, tr=512):
    r, lanes = g.shape
    tr = tr if r % tr == 0 else r
    c1 = 1.0 / (1.0 - ADAM_B1 ** ADAM_STEP)
    c2 = 1.0 / (1.0 - ADAM_B2 ** ADAM_STEP)

    def body(g_ref, w_ref, m_ref, v_ref, d_ref, nm_ref, nv_ref):
        g = g_ref[...]
        nm = ADAM_B1 * m_ref[...] + (1.0 - ADAM_B1) * g
        nv = ADAM_B2 * v_ref[...] + (1.0 - ADAM_B2) * (g * g)
        nm_ref[...] = nm
        nv_ref[...] = nv
        d_ref[...] = -ADAM_LR * ((nm * c1) / (jnp.sqrt(nv * c2) + ADAM_EPS) + ADAM_WD * w_ref[...])

    row = pl.BlockSpec((tr, lanes), lambda i: (i, 0))
    out = jax.ShapeDtypeStruct((r, lanes), F32)
    return pl.pallas_call(
        body,
        name=name,
        grid=(r // tr,),
        in_specs=[row] * 4,
        out_specs=[row] * 3,
        out_shape=[out] * 3,
        compiler_params=_params(("parallel",)),
    )(g, w, m, v)


BIG = ("a_w_in", "a_w_out", "b_w_in", "b_w_out", "f_w_up", "f_w_down", "ple_w_proj", "ple_w_gate")
CONVS = ("a_conv", "f_conv")
SMALL = ("norm_mix", "norm_ffn", "norm_ple", "norm_final", "a_log", "a_dt_bias", "a_norm", "b_sinks")
WEIGHTS = ("norm_mix", "norm_ffn", "norm_ple", "norm_final", "a_w_in", "a_conv", "a_log", "a_dt_bias", "a_norm",
           "a_w_out", "b_w_in", "b_sinks", "b_w_out", "f_w_up", "f_conv", "f_w_down", "ple_w_proj", "ple_w_gate")
SLAB_ROW_MULTIPLE = 512


def _pack(arrs, dtype, row_multiple):
    flat = jnp.concatenate([a.reshape(-1).astype(dtype) for a in arrs])
    rows = -(-flat.shape[0] // LANES)
    rows = -(-rows // row_multiple) * row_multiple
    return jnp.pad(flat, (0, rows * LANES - flat.shape[0])).reshape(rows, LANES)


def _unpack(slab, shapes):
    lead = slab.shape[:-2]
    flat = slab.reshape(lead + (-1,))
    out, off = [], 0
    for s in shapes:
        size = math.prod(s)
        out.append(flat[..., off:off + size].reshape(lead + tuple(s)))
        off += size
    return out


def _cols_full(g):
    g = jnp.moveaxis(g, 0, -2)
    return g.reshape(g.shape[:-2] + (g.shape[-2] * g.shape[-1],))


def _rows_full(g):
    g = jnp.moveaxis(g, 0, -3)
    return g.reshape(g.shape[:-3] + (g.shape[-3] * g.shape[-2], g.shape[-1]))


def _cols_split(wfull):
    n = wfull.shape[-1] // N_DEV
    g = wfull.reshape(wfull.shape[:-1] + (N_DEV, n))
    return jnp.moveaxis(g, -2, 0)


def _rows_split(wfull):
    k = wfull.shape[-2] // N_DEV
    g = wfull.reshape(wfull.shape[:-2] + (N_DEV, k, wfull.shape[-1]))
    return jnp.moveaxis(g, -3, 0)


TRANSPOSED = ("a_w_in", "b_w_in", "f_w_up", "ple_w_proj")


def _wire(name, a):
    return jnp.swapaxes(a, -1, -2) if name in TRANSPOSED else a


def _wire_shape(name, shape):
    return shape[:-2] + (shape[-1], shape[-2]) if name in TRANSPOSED else tuple(shape)


def _full(name, g):
    return _cols_full(g) if name in CONVS else _rows_full(g)


def _split(name, wfull):
    return _cols_split(wfull) if name in CONVS else _rows_split(wfull)


def _pack_split(grads, names, dtype, row_multiple):
    flat = jnp.concatenate([_split(n, grads[n]).reshape(N_DEV, -1).astype(dtype) for n in names], axis=1)
    rows = -(-flat.shape[1] // LANES)
    rows = -(-rows // row_multiple) * row_multiple
    return jnp.pad(flat, ((0, 0), (0, rows * LANES - flat.shape[1]))).reshape(N_DEV, rows, LANES)


def _pad_cols(a, width):
    return jnp.pad(a, ((0, 0), (0, width - a.shape[1])))


def kernel(x, p, norm_mix, norm_ffn, norm_ple, norm_final, a_w_in, a_conv, a_log, a_dt_bias, a_norm, a_w_out, b_w_in, b_sinks, b_w_out, f_w_up, f_conv, f_w_down, ple_w_proj, ple_w_gate, loss_target, m_norm_mix, m_norm_ffn, m_norm_ple, m_norm_final, m_a_w_in, m_a_conv, m_a_log, m_a_dt_bias, m_a_norm, m_a_w_out, m_b_w_in, m_b_sinks, m_b_w_out, m_f_w_up, m_f_conv, m_f_w_down, m_ple_w_proj, m_ple_w_gate, v_norm_mix, v_norm_ffn, v_norm_ple, v_norm_final, v_a_w_in, v_a_conv, v_a_log, v_a_dt_bias, v_a_norm, v_a_w_out, v_b_w_in, v_b_sinks, v_b_w_out, v_f_w_up, v_f_conv, v_f_w_down, v_ple_w_proj, v_ple_w_gate):
    wts = dict(norm_mix=norm_mix, norm_ffn=norm_ffn, norm_ple=norm_ple, norm_final=norm_final, a_w_in=a_w_in,
               a_conv=a_conv, a_log=a_log, a_dt_bias=a_dt_bias, a_norm=a_norm, a_w_out=a_w_out, b_w_in=b_w_in,
               b_sinks=b_sinks, b_w_out=b_w_out, f_w_up=f_w_up, f_conv=f_conv, f_w_down=f_w_down,
               ple_w_proj=ple_w_proj, ple_w_gate=ple_w_gate)
    mom = dict(norm_mix=m_norm_mix, norm_ffn=m_norm_ffn, norm_ple=m_norm_ple, norm_final=m_norm_final,
               a_w_in=m_a_w_in, a_conv=m_a_conv, a_log=m_a_log, a_dt_bias=m_a_dt_bias, a_norm=m_a_norm,
               a_w_out=m_a_w_out, b_w_in=m_b_w_in, b_sinks=m_b_sinks, b_w_out=m_b_w_out, f_w_up=m_f_w_up,
               f_conv=m_f_conv, f_w_down=m_f_w_down, ple_w_proj=m_ple_w_proj, ple_w_gate=m_ple_w_gate)
    var = dict(norm_mix=v_norm_mix, norm_ffn=v_norm_ffn, norm_ple=v_norm_ple, norm_final=v_norm_final,
               a_w_in=v_a_w_in, a_conv=v_a_conv, a_log=v_a_log, a_dt_bias=v_a_dt_bias, a_norm=v_a_norm,
               a_w_out=v_a_w_out, b_w_in=v_b_w_in, b_sinks=v_b_sinks, b_w_out=v_b_w_out, f_w_up=v_f_w_up,
               f_conv=v_f_conv, f_w_down=v_f_w_down, ple_w_proj=v_ple_w_proj, ple_w_gate=v_ple_w_gate)
    hk = N_HEADS_A * HEAD_DIM_A
    xs = x[0]
    tgt = loss_target[0]
    p_bf = p.astype(BF16)

    def shard(name, layer):
        return _wire(name, wts[name][layer]).astype(BF16)

    def stacked_rows(g):
        return g.reshape(g.shape[0] * g.shape[1], g.shape[2])

    first = _all_gather(shard("a_w_in", 0), name="gather_mixer0")
    wa_in_t = jnp.pad(stacked_rows(first), ((0, PROJ_A - PROJ_A_REAL), (0, 0)))
    gconv = _all_gather(_pack([wts[n] for n in CONVS], F32, 8), dep=first, name="gather_convs")
    conv_full = {n: _cols_full(g) for n, g in zip(CONVS, _unpack(gconv, [wts[n].shape for n in CONVS]))}
    cv_a, cv_f = conv_full["a_conv"][0], conv_full["f_conv"]
    layer_names = ("f_w_up", "f_w_down", "ple_w_proj", "ple_w_gate")
    gather0, tok = _exchange_start([shard(n, 0) for n in layer_names + ("a_w_out",)], scatter=False,
                                   name="gather_layer0_start", dep=gconv)
    gather1, tok = _exchange_start([shard(n, 0) for n in ("b_w_in", "b_w_out")] + [shard(n, 1) for n in layer_names],
                                   scatter=False, name="gather_layer1_start", dep=tok)

    alog_row = jnp.pad(a_log, ((0, 0), (N_HEADS_A, LANES - 2 * N_HEADS_A)))
    dtb_row = jnp.pad(a_dt_bias, ((0, 0), (N_HEADS_A, LANES - 2 * N_HEADS_A)))

    tile_f32, tile_bf16, rowsum = (F32, "tile"), (BF16, "tile"), (F32, "rowsum")

    def ffn_ple_fwd(i, h_a, n_f, next_norm, w_up_t, w_down, w_pp_t, w_pg):
        up, y, act = _ffn_up_act(n_f, w_up_t, cv_f[i], name=f"l{i}_ffn_up")
        h_b, n_p = _matmul_rows(act, w_down, _epi_res_norm, [h_a], [norm_ple[i]], [tile_f32, tile_bf16],
                                name=f"l{i}_ffn_down")
        pe = _matmul(p_bf[i, 0], w_pp_t, tb=True, name=f"l{i}_ple_proj")
        res = _matmul_rows(n_p, w_pg, _epi_ple, [h_b, pe], [] if next_norm is None else [next_norm],
                           [tile_f32, tile_f32] + ([] if next_norm is None else [tile_bf16]), name=f"l{i}_ple_gate")
        return res[1], (None if next_norm is None else res[2]), dict(n_f=n_f, up=up, y=y, act=act, h_b=h_b, n_p=n_p, zg=res[0], pe=pe)

    def layer_weights(lands):
        up_t, down, pp_t, pg = (stacked_rows(g) for g in lands)
        return up_t.reshape(2, D_FF, D_MODEL), down, pp_t, pg

    n0 = _rms_fwd(xs, norm_mix[0], name="l0_mix_norm")
    proj = _matmul(n0, wa_in_t, tb=True, tm=512, dep=tok, name="l0_in_proj")
    q, k, v, gbc, bbc, y_qkv = _delta_pre_fwd(proj, cv_a, alog_row, dtb_row, name="l0_delta_pre")
    o, states, tinv, og = _delta_fwd(q, k, v, gbc, bbc, proj, a_norm, name="l0_delta")
    lands0 = _exchange_wait(gather0, og, scatter=False, name="gather_layer0_wait")
    lw0, wa_out = layer_weights(lands0[:4]), stacked_rows(lands0[4])
    h1, nf0 = _matmul_rows(og, wa_out, _epi_res_norm, [xs], [norm_ffn[0]], [tile_f32, tile_bf16], name="l0_mix_out")
    h3, n1, sv0 = ffn_ple_fwd(0, h1, nf0, norm_mix[1], *lw0)

    lands1 = _exchange_wait(gather1, h3, scatter=False, name="gather_layer1_wait")
    wb_in_t, wb_out = stacked_rows(lands1[0]), stacked_rows(lands1[1])
    lw1 = layer_weights(lands1[2:])
    pb = _matmul(n1, wb_in_t, tb=True, name="l1_in_qkv")
    att = _swa_fwd(pb, b_sinks, name="l1_swa")
    h4, nf1 = _matmul_rows(att, wb_out, _epi_res_norm, [h3], [norm_ffn[1]], [tile_f32, tile_bf16], name="l1_mix_out")
    h6, _, sv1 = ffn_ple_fwd(1, h4, nf1, None, *lw1)

    loss_row, dh6, d_norm_final = _final_loss(h6, norm_final, tgt, name="final_loss")
    loss = lax.psum(loss_row[0, 0], MESH_AXES)

    def ffn_ple_bwd(i, dh_c, h_a, sv, lw, dep):
        w_up_t, w_down, _, w_pg = lw
        dzg, dpe = _ple_bwd(dh_c, sv["zg"], sv["pe"], name=f"l{i}_ple_mix_bwd")
        d_pg = _matmul(sv["n_p"], dzg, ta=True, out_dtype=BF16, dep=dep, name=f"l{i}_ple_gate_dw")
        d_pp_t = _matmul(dpe, p_bf[i, 0], ta=True, out_dtype=BF16, name=f"l{i}_ple_proj_dw")
        dh_b, d_np = _matmul_rows(dzg, w_pg, _epi_rms_bwd, [sv["h_b"], dh_c], [norm_ple[i]], [tile_f32, rowsum], tb=True,
                                  name=f"l{i}_ple_gate_dx")
        d_down = _matmul(sv["act"], dh_b, ta=True, out_dtype=BF16, name=f"l{i}_ffn_down_dw")
        dup, d_cg, d_cv = _ffn_down_dx_act_bwd(dh_b, w_down, sv["up"], sv["y"], cv_f[i], name=f"l{i}_ffn_down_dx")
        d_up_t = _matmul(dup, sv["n_f"], ta=True, out_dtype=BF16, name=f"l{i}_ffn_up_dw")
        dh_a, d_nf = _matmul_rows(dup, w_up_t, _epi_rms_bwd, [h_a, dh_b], [norm_ffn[i]], [tile_f32, rowsum],
                                  name=f"l{i}_ffn_up_dx")
        mats = [d_up_t.reshape(2 * D_FF, D_MODEL), d_down, d_pp_t, d_pg]
        return dh_a, mats, dict(norm_ple=d_np, norm_ffn=d_nf, f_conv=jnp.concatenate([d_cg, d_cv], axis=1))

    def slabs(g):
        return g.reshape(N_DEV, g.shape[0] // N_DEV, g.shape[1])

    dh4, mats1, g1 = ffn_ple_bwd(1, dh6, h4, sv1, lw1, None)
    datt = _matmul(dh4, wb_out, tb=True, out_dtype=BF16, name="l1_mix_out_dx")
    d_wb_out = _matmul(att, dh4, ta=True, out_dtype=BF16, name="l1_mix_out_dw")
    dq_b, dk_b, dv_b, dsinks = _swa_bwd(pb, b_sinks, datt, name="l1_swa_bwd")
    dpb = jnp.concatenate([dq_b, dk_b.astype(BF16), dv_b.astype(BF16)], axis=1)
    d_wb_in_t = _matmul(dpb, n1, ta=True, out_dtype=BF16, name="l1_in_qkv_dw")
    send1, tok = _exchange_start([slabs(g) for g in [d_wb_in_t, d_wb_out] + mats1], scatter=True,
                                 name="exchange_layer1_start", dep=d_wb_in_t)
    dh3, d_nm1 = _matmul_rows(dpb, wb_in_t, _epi_rms_bwd, [h3, dh4], [norm_mix[1]], [tile_f32, rowsum], name="l1_in_qkv_dx")

    dh1, mats0, g0 = ffn_ple_bwd(0, dh3, h1, sv0, lw0, tok)
    send0, tok = _exchange_start([slabs(g) for g in mats0], scatter=True, name="exchange_layer0_start", dep=mats0[0])
    dog = _matmul(dh1, wa_out, tb=True, dep=tok, name="l0_mix_out_dx")
    d_wa_out = _matmul(og, dh1, ta=True, out_dtype=BF16, name="l0_mix_out_dw")
    dq, dk, dv, dgbc, dbbc, dz0, d_anorm = _delta_bwd(q, k, v, gbc, bbc, tinv, states, o, proj, a_norm, dog,
                                                      name="l0_delta_bwd")
    dproj, d_aconv, d_alog, d_dtb = _delta_pre_bwd(proj, y_qkv, cv_a, alog_row, dtb_row, dq, dk, dv, dgbc, dbbc, dz0,
                                                   name="l0_delta_pre_bwd")
    d_wa_in_t = _matmul(dproj, n0, ta=True, out_dtype=BF16, name="l0_in_proj_dw")
    sendm, tok = _exchange_start([slabs(d_wa_in_t[:PROJ_A_REAL]), slabs(d_wa_out)], scatter=True,
                                 name="exchange_mixer0_start", dep=d_wa_in_t)
    dx, d_nm0 = _matmul_rows(dproj, wa_in_t, _epi_rms_bwd, [xs, dh1], [norm_mix[0]], [tile_f32, rowsum], dep=tok,
                             name="l0_in_proj_dx")

    recv1 = _exchange_wait(send1, dx, scatter=True, name="exchange_layer1_wait")
    recv0 = _exchange_wait(send0, recv1[0], scatter=True, name="exchange_layer0_wait")
    parts = {("b_w_in", 0): recv1[0], ("b_w_out", 0): recv1[1]}
    parts.update({(n, 1): r for n, r in zip(layer_names, recv1[2:])})
    parts.update({(n, 0): r for n, r in zip(layer_names, recv0)})

    outs = {}

    def update_matrix(name):
        w_, m_, v_ = (_wire(name, a) for a in (wts[name], mom[name], var[name]))
        res = None
        for layer in range(w_.shape[0]):
            res = _adamw_layer(parts[(name, layer)], w_, m_, v_, layer, res, name=f"adamw_{name}_{layer}")
        for kind, arr in zip(("grad", "delta", "new_m", "new_v"), res):
            outs[(kind, name)] = _wire(name, arr)
        return res

    last = [update_matrix(n) for n in ("b_w_in", "b_w_out") + layer_names][-1]
    recvm = _exchange_wait(sendm, last[0], scatter=True, name="exchange_mixer0_wait")
    parts.update({("a_w_in", 0): recvm[0], ("a_w_out", 0): recvm[1]})
    update_matrix("a_w_in")
    update_matrix("a_w_out")

    gconvs = dict(a_conv=d_aconv[None], f_conv=jnp.stack([g0["f_conv"], g1["f_conv"]]))
    small_g = dict(norm_mix=jnp.concatenate([d_nm0, d_nm1]), norm_ffn=jnp.concatenate([g0["norm_ffn"], g1["norm_ffn"]]),
                   norm_ple=jnp.concatenate([g0["norm_ple"], g1["norm_ple"]]), norm_final=d_norm_final[0],
                   a_log=d_alog[:, N_HEADS_A:2 * N_HEADS_A], a_dt_bias=d_dtb[:, N_HEADS_A:2 * N_HEADS_A],
                   a_norm=d_anorm, b_sinks=dsinks[:, :N_HEADS_B])
    recv_conv = _all_to_all(_pack_split(gconvs, CONVS, F32, 8), name="exchange_conv_grads")
    recv_small = _all_gather(_pack([small_g[n] for n in SMALL], F32, 8), name="gather_small_grads")
    for names, recv, tag in ((CONVS, recv_conv, "convs"), (SMALL, recv_small, "small")):
        shapes = [wts[n].shape for n in names]
        g_slab = _sum_parts(recv, name=f"sum_{tag}")
        packed = [_pack([d[n] for n in names], F32, 8) for d in (wts, mom, var)]
        res = _adamw_packed(g_slab, *packed, name=f"adamw_{tag}")
        for kind, slab in zip(("grad", "delta", "new_m", "new_v"), (g_slab,) + tuple(res)):
            for n, arr in zip(names, _unpack(slab, shapes)):
                outs[(kind, n)] = arr

    result = [loss, dx[None]]
    for kind in ("grad", "delta", "new_m", "new_v"):
        result += [outs[(kind, n)] for n in WEIGHTS]
    return tuple(result)
```
